```python
import math
import jax, jax.numpy as jnp
from jax import lax
import numpy as np

D_MODEL = 1024
BATCH = 8
SEQ = 8192
DEPTH = 2

N_A_LAYERS = DEPTH // 2
N_B_LAYERS = DEPTH - N_A_LAYERS
CHUNK = 128
A_WIDTH = D_MODEL
A_GROUPS = 8
A_GROUP_DIM = A_WIDTH // A_GROUPS
HEAD_DIM = 64
N_Q_HEADS = D_MODEL // HEAD_DIM
N_KV_HEADS = 4
GQA_GROUP = N_Q_HEADS // N_KV_HEADS
WINDOW = 128
BLOCK = 128
D_FF = 2816
CONV_WIDTH = 3
PLE_DIM = 256
EPS = 1e-6

kernel_name = "yoco_gmlp_swa_sink_hybrid"


def _alibi_slopes(n):
    return np.array([2.0 ** (-8.0 * (h + 1) / n) for h in range(n)], dtype=np.float32)


def rmsnorm(x, g):
    xf = x.astype(jnp.float32)
    y = xf * lax.rsqrt(jnp.mean(xf * xf, axis=-1, keepdims=True) + EPS)
    return (y * g.astype(jnp.float32)).astype(x.dtype)


def mixer_a(xn, w_in, g_v, w_s, b_s, w_out):
    B, S, _ = xn.shape
    z = jax.nn.gelu(xn @ w_in)
    u, v = jnp.split(z, 2, axis=-1)
    v = rmsnorm(v, g_v)
    nc = S // CHUNK
    v = v.reshape(B, nc, CHUNK, A_GROUPS, A_GROUP_DIM)
    causal = jnp.tril(jnp.ones((CHUNK, CHUNK), dtype=bool))
    w = jnp.where(causal[None], w_s, jnp.zeros((), w_s.dtype)).astype(v.dtype)
    s = jnp.einsum('hts,bcshd->bcthd', w, v) + b_s.T.astype(v.dtype)[None, None, :, :, None]
    s = s.reshape(B, S, A_WIDTH)
    return (u * s) @ w_out


def shared_kv(h, g, w_kv):
    B, S, _ = h.shape
    nb = S // BLOCK
    kv = rmsnorm(h, g) @ w_kv
    k, v = jnp.split(kv, 2, axis=-1)

    def band(t):
        tb = t.reshape(B, nb, BLOCK, N_KV_HEADS, HEAD_DIM)
        prev = jnp.pad(tb[:, :-1], ((0, 0), (1, 0), (0, 0), (0, 0), (0, 0)))
        return jnp.concatenate([prev, tb], axis=2)

    return band(k), band(v)


def mixer_b(xn, w_q, sinks, w_o, kblk, vblk):
    B, S, _ = xn.shape
    nb = S // BLOCK
    q = (xn @ w_q).reshape(B, nb, BLOCK, N_KV_HEADS, GQA_GROUP, HEAD_DIM)
    scores = jnp.einsum('bnikgd,bnjkd->bnkgij', q.astype(jnp.float32),
                        kblk.astype(jnp.float32)) * (HEAD_DIM ** -0.5)
    i = jnp.arange(BLOCK)[:, None]
    j = jnp.arange(2 * BLOCK)[None, :]
    dist = i + BLOCK - j
    in_band = (dist >= 0) & (dist < WINDOW)
    blk = jnp.arange(nb)[:, None, None]
    valid = in_band[None] & ((blk > 0) | (j[None] >= BLOCK))
    slopes = jnp.asarray(_alibi_slopes(N_Q_HEADS)).reshape(N_KV_HEADS, GQA_GROUP)
    scores = scores - slopes[:, :, None, None] * dist.astype(jnp.float32)
    scores = jnp.where(valid[None, :, None, None], scores, -jnp.inf)
    sink = sinks.astype(jnp.float32).reshape(N_KV_HEADS, GQA_GROUP)[:, :, None, None]
    m = jnp.maximum(jnp.max(scores, axis=-1, keepdims=True), sink)
    e = jnp.exp(scores - m)
    probs = e / (jnp.sum(e, axis=-1, keepdims=True) + jnp.exp(sink - m))
    out = jnp.einsum('bnkgij,bnjkd->bnikgd', probs.astype(vblk.dtype), vblk)
    return out.reshape(B, S, N_Q_HEADS * HEAD_DIM) @ w_o


def conv_ffn(xn, w_up, conv_w, conv_b, w_down):
    S = xn.shape[1]
    h = xn @ w_up
    hp = jnp.pad(h, ((0, 0), (CONV_WIDTH - 1, 0), (0, 0)))
    c = conv_b + sum(hp[:, t:t + S] * conv_w[t] for t in range(CONV_WIDTH))
    g, u = jnp.split(c, 2, axis=-1)
    return (jax.nn.silu(g) * u) @ w_down


def per_layer_embed(h, p_i, g_norm, w_in, w_gate, b_gate):
    gate = jax.nn.sigmoid(rmsnorm(h, g_norm) @ w_gate + b_gate)
    return (p_i @ w_in) * gate


def _fwd_setup_inputs(seed: int = 0) -> dict:
    key = jax.random.key(seed)
    ks = jax.random.split(key, 26)
    f32 = jnp.float32

    def nrm(k, shape, scale):
        return jax.random.normal(k, shape, f32) * scale

    def gain(k, shape):
        return 1.0 + 0.02 * jax.random.normal(k, shape, f32)

    d = D_MODEL
    return {
        "x": nrm(ks[0], (BATCH, SEQ, d), 1.0),
        "p": nrm(ks[1], (DEPTH, BATCH, SEQ, PLE_DIM), 1.0),
        "norm_mix": gain(ks[2], (DEPTH, d)),
        "norm_ffn": gain(ks[3], (DEPTH, d)),
        "norm_ple": gain(ks[4], (DEPTH, d)),
        "norm_kv": gain(ks[5], (d,)),
        "norm_final": gain(ks[6], (d,)),
        "a_w_in": nrm(ks[7], (N_A_LAYERS, d, 2 * A_WIDTH), d ** -0.5),
        "a_norm_v": gain(ks[8], (N_A_LAYERS, A_WIDTH)),
        "a_w_s": nrm(ks[9], (N_A_LAYERS, A_GROUPS, CHUNK, CHUNK), CHUNK ** -0.5),
        "a_b_s": 1.0 + nrm(ks[10], (N_A_LAYERS, A_GROUPS, CHUNK), 0.02),
        "a_w_out": nrm(ks[11], (N_A_LAYERS, A_WIDTH, d), A_WIDTH ** -0.5),
        "w_kv": nrm(ks[12], (d, 2 * N_KV_HEADS * HEAD_DIM), d ** -0.5),
        "b_w_q": nrm(ks[13], (N_B_LAYERS, d, N_Q_HEADS * HEAD_DIM), d ** -0.5),
        "b_sinks": nrm(ks[14], (N_B_LAYERS, N_Q_HEADS), 0.5),
        "b_w_o": nrm(ks[15], (N_B_LAYERS, N_Q_HEADS * HEAD_DIM, d), (N_Q_HEADS * HEAD_DIM) ** -0.5),
        "f_w_up": nrm(ks[16], (DEPTH, d, 2 * D_FF), d ** -0.5),
        "f_conv_w": nrm(ks[17], (DEPTH, CONV_WIDTH, 2 * D_FF), CONV_WIDTH ** -0.5),
        "f_conv_b": nrm(ks[18], (DEPTH, 2 * D_FF), 0.01),
        "f_w_down": nrm(ks[19], (DEPTH, D_FF, d), D_FF ** -0.5),
        "ple_w_in": nrm(ks[20], (DEPTH, PLE_DIM, d), PLE_DIM ** -0.5),
        "ple_w_gate": nrm(ks[21], (DEPTH, d, d), d ** -0.5),
        "ple_b_gate": nrm(ks[22], (DEPTH, d), 0.01),
    }


def _fwd_reference(x, p, norm_mix, norm_ffn, norm_ple, norm_kv, norm_final,
              a_w_in, a_norm_v, a_w_s, a_b_s, a_w_out,
              w_kv, b_w_q, b_sinks, b_w_o,
              f_w_up, f_conv_w, f_conv_b, f_w_down,
              ple_w_in, ple_w_gate, ple_b_gate):
    h = x
    kblk = None
    vblk = None
    for i in range(DEPTH):
        xn = rmsnorm(h, norm_mix[i])
        if i < N_A_LAYERS:
            a = i
            h = h + mixer_a(xn, a_w_in[a], a_norm_v[a], a_w_s[a], a_b_s[a], a_w_out[a])
        else:
            b = i - N_A_LAYERS
            h = h + mixer_b(xn, b_w_q[b], b_sinks[b], b_w_o[b], kblk, vblk)
        h = h + conv_ffn(rmsnorm(h, norm_ffn[i]), f_w_up[i], f_conv_w[i], f_conv_b[i], f_w_down[i])
        h = h + per_layer_embed(h, p[i], norm_ple[i], ple_w_in[i], ple_w_gate[i], ple_b_gate[i])
        if i == N_A_LAYERS - 1:
            kblk, vblk = shared_kv(h, norm_kv, w_kv)
    return rmsnorm(h, norm_final)


import jax as _jax
import jax.numpy as _jnp

TWIN_FORMAT = 'train_step'
FWD_PARAMS = ['x', 'p', 'norm_mix', 'norm_ffn', 'norm_ple', 'norm_kv', 'norm_final', 'a_w_in', 'a_norm_v', 'a_w_s', 'a_b_s', 'a_w_out', 'w_kv', 'b_w_q', 'b_sinks', 'b_w_o', 'f_w_up', 'f_conv_w', 'f_conv_b', 'f_w_down', 'ple_w_in', 'ple_w_gate', 'ple_b_gate']
TWIN_WEIGHTS = ['norm_mix', 'norm_ffn', 'norm_ple', 'norm_kv', 'norm_final', 'a_w_in', 'a_norm_v', 'a_w_s', 'a_b_s', 'a_w_out', 'w_kv', 'b_w_q', 'b_sinks', 'b_w_o', 'f_w_up', 'f_conv_w', 'f_conv_b', 'f_w_down', 'ple_w_in', 'ple_w_gate', 'ple_b_gate']
TWIN_DIFF_INPUT = 'x'
TWIN_INPUTS = ['x', 'p', 'norm_mix', 'norm_ffn', 'norm_ple', 'norm_kv', 'norm_final', 'a_w_in', 'a_norm_v', 'a_w_s', 'a_b_s', 'a_w_out', 'w_kv', 'b_w_q', 'b_sinks', 'b_w_o', 'f_w_up', 'f_conv_w', 'f_conv_b', 'f_w_down', 'ple_w_in', 'ple_w_gate', 'ple_b_gate', 'loss_target', 'm_norm_mix', 'm_norm_ffn', 'm_norm_ple', 'm_norm_kv', 'm_norm_final', 'm_a_w_in', 'm_a_norm_v', 'm_a_w_s', 'm_a_b_s', 'm_a_w_out', 'm_w_kv', 'm_b_w_q', 'm_b_sinks', 'm_b_w_o', 'm_f_w_up', 'm_f_conv_w', 'm_f_conv_b', 'm_f_w_down', 'm_ple_w_in', 'm_ple_w_gate', 'm_ple_b_gate', 'v_norm_mix', 'v_norm_ffn', 'v_norm_ple', 'v_norm_kv', 'v_norm_final', 'v_a_w_in', 'v_a_norm_v', 'v_a_w_s', 'v_a_b_s', 'v_a_w_out', 'v_w_kv', 'v_b_w_q', 'v_b_sinks', 'v_b_w_o', 'v_f_w_up', 'v_f_conv_w', 'v_f_conv_b', 'v_f_w_down', 'v_ple_w_in', 'v_ple_w_gate', 'v_ple_b_gate']
TWIN_OUTPUTS = ['loss', 'grad_x', 'grad_norm_mix', 'grad_norm_ffn', 'grad_norm_ple', 'grad_norm_kv', 'grad_norm_final', 'grad_a_w_in', 'grad_a_norm_v', 'grad_a_w_s', 'grad_a_b_s', 'grad_a_w_out', 'grad_w_kv', 'grad_b_w_q', 'grad_b_sinks', 'grad_b_w_o', 'grad_f_w_up', 'grad_f_conv_w', 'grad_f_conv_b', 'grad_f_w_down', 'grad_ple_w_in', 'grad_ple_w_gate', 'grad_ple_b_gate', 'delta_norm_mix', 'delta_norm_ffn', 'delta_norm_ple', 'delta_norm_kv', 'delta_norm_final', 'delta_a_w_in', 'delta_a_norm_v', 'delta_a_w_s', 'delta_a_b_s', 'delta_a_w_out', 'delta_w_kv', 'delta_b_w_q', 'delta_b_sinks', 'delta_b_w_o', 'delta_f_w_up', 'delta_f_conv_w', 'delta_f_conv_b', 'delta_f_w_down', 'delta_ple_w_in', 'delta_ple_w_gate', 'delta_ple_b_gate', 'new_m_norm_mix', 'new_m_norm_ffn', 'new_m_norm_ple', 'new_m_norm_kv', 'new_m_norm_final', 'new_m_a_w_in', 'new_m_a_norm_v', 'new_m_a_w_s', 'new_m_a_b_s', 'new_m_a_w_out', 'new_m_w_kv', 'new_m_b_w_q', 'new_m_b_sinks', 'new_m_b_w_o', 'new_m_f_w_up', 'new_m_f_conv_w', 'new_m_f_conv_b', 'new_m_f_w_down', 'new_m_ple_w_in', 'new_m_ple_w_gate', 'new_m_ple_b_gate', 'new_v_norm_mix', 'new_v_norm_ffn', 'new_v_norm_ple', 'new_v_norm_kv', 'new_v_norm_final', 'new_v_a_w_in', 'new_v_a_norm_v', 'new_v_a_w_s', 'new_v_a_b_s', 'new_v_a_w_out', 'new_v_w_kv', 'new_v_b_w_q', 'new_v_b_sinks', 'new_v_b_w_o', 'new_v_f_w_up', 'new_v_f_conv_w', 'new_v_f_conv_b', 'new_v_f_w_down', 'new_v_ple_w_in', 'new_v_ple_w_gate', 'new_v_ple_b_gate']
TWIN_LEAF_KINDS = {'loss': 'loss', 'grad_x': 'grad_x', 'grad_norm_mix': 'grad_w', 'grad_norm_ffn': 'grad_w', 'grad_norm_ple': 'grad_w', 'grad_norm_kv': 'grad_w', 'grad_norm_final': 'grad_w', 'grad_a_w_in': 'grad_w', 'grad_a_norm_v': 'grad_w', 'grad_a_w_s': 'grad_w', 'grad_a_b_s': 'grad_w', 'grad_a_w_out': 'grad_w', 'grad_w_kv': 'grad_w', 'grad_b_w_q': 'grad_w', 'grad_b_sinks': 'grad_w', 'grad_b_w_o': 'grad_w', 'grad_f_w_up': 'grad_w', 'grad_f_conv_w': 'grad_w', 'grad_f_conv_b': 'grad_w', 'grad_f_w_down': 'grad_w', 'grad_ple_w_in': 'grad_w', 'grad_ple_w_gate': 'grad_w', 'grad_ple_b_gate': 'grad_w', 'delta_norm_mix': 'delta_w', 'delta_norm_ffn': 'delta_w', 'delta_norm_ple': 'delta_w', 'delta_norm_kv': 'delta_w', 'delta_norm_final': 'delta_w', 'delta_a_w_in': 'delta_w', 'delta_a_norm_v': 'delta_w', 'delta_a_w_s': 'delta_w', 'delta_a_b_s': 'delta_w', 'delta_a_w_out': 'delta_w', 'delta_w_kv': 'delta_w', 'delta_b_w_q': 'delta_w', 'delta_b_sinks': 'delta_w', 'delta_b_w_o': 'delta_w', 'delta_f_w_up': 'delta_w', 'delta_f_conv_w': 'delta_w', 'delta_f_conv_b': 'delta_w', 'delta_f_w_down': 'delta_w', 'delta_ple_w_in': 'delta_w', 'delta_ple_w_gate': 'delta_w', 'delta_ple_b_gate': 'delta_w', 'new_m_norm_mix': 'new_m', 'new_m_norm_ffn': 'new_m', 'new_m_norm_ple': 'new_m', 'new_m_norm_kv': 'new_m', 'new_m_norm_final': 'new_m', 'new_m_a_w_in': 'new_m', 'new_m_a_norm_v': 'new_m', 'new_m_a_w_s': 'new_m', 'new_m_a_b_s': 'new_m', 'new_m_a_w_out': 'new_m', 'new_m_w_kv': 'new_m', 'new_m_b_w_q': 'new_m', 'new_m_b_sinks': 'new_m', 'new_m_b_w_o': 'new_m', 'new_m_f_w_up': 'new_m', 'new_m_f_conv_w': 'new_m', 'new_m_f_conv_b': 'new_m', 'new_m_f_w_down': 'new_m', 'new_m_ple_w_in': 'new_m', 'new_m_ple_w_gate': 'new_m', 'new_m_ple_b_gate': 'new_m', 'new_v_norm_mix': 'new_v', 'new_v_norm_ffn': 'new_v', 'new_v_norm_ple': 'new_v', 'new_v_norm_kv': 'new_v', 'new_v_norm_final': 'new_v', 'new_v_a_w_in': 'new_v', 'new_v_a_norm_v': 'new_v', 'new_v_a_w_s': 'new_v', 'new_v_a_b_s': 'new_v', 'new_v_a_w_out': 'new_v', 'new_v_w_kv': 'new_v', 'new_v_b_w_q': 'new_v', 'new_v_b_sinks': 'new_v', 'new_v_b_w_o': 'new_v', 'new_v_f_w_up': 'new_v', 'new_v_f_conv_w': 'new_v', 'new_v_f_conv_b': 'new_v', 'new_v_f_w_down': 'new_v', 'new_v_ple_w_in': 'new_v', 'new_v_ple_w_gate': 'new_v', 'new_v_ple_b_gate': 'new_v'}


def _forward(args):
    return _fwd_reference(*[args[k] for k in FWD_PARAMS])


def _output_shape():
    out = _jax.eval_shape(lambda: _forward(_fwd_setup_inputs(0)))
    return out.shape, out.dtype

N_MICROBATCH = 1
ADAM_LR = 0.001
ADAM_B1 = 0.9
ADAM_B2 = 0.999
ADAM_EPS = 1e-08
ADAM_WD = 0.01
ADAM_STEP = 10
PER_EXAMPLE_BATCH_AXIS = {'x': 0, 'p': 1, 'loss_target': 0}
SHARED_INPUTS = []
_WEIGHT_DTYPES = {'norm_mix': _jnp.float32, 'norm_ffn': _jnp.float32, 'norm_ple': _jnp.float32, 'norm_kv': _jnp.float32, 'norm_final': _jnp.float32, 'a_w_in': _jnp.float32, 'a_norm_v': _jnp.float32, 'a_w_s': _jnp.float32, 'a_b_s': _jnp.float32, 'a_w_out': _jnp.float32, 'w_kv': _jnp.float32, 'b_w_q': _jnp.float32, 'b_sinks': _jnp.float32, 'b_w_o': _jnp.float32, 'f_w_up': _jnp.float32, 'f_conv_w': _jnp.float32, 'f_conv_b': _jnp.float32, 'f_w_down': _jnp.float32, 'ple_w_in': _jnp.float32, 'ple_w_gate': _jnp.float32, 'ple_b_gate': _jnp.float32}
MOMENT_SCALE = {'norm_mix': 1.417395e-01, 'norm_ffn': 1.420529e-01, 'norm_ple': 3.536040e-02, 'norm_kv': 7.820053e-02, 'norm_final': 6.410451e+01, 'a_w_in': 1.410653e-01, 'a_norm_v': 9.275015e-02, 'a_w_s': 9.437407e-02, 'a_b_s': 1.441539e-01, 'a_w_out': 1.761374e-01, 'w_kv': 1.029655e-01, 'b_w_q': 3.864196e-02, 'b_sinks': 6.206227e-02, 'b_w_o': 6.205626e-02, 'f_w_up': 5.942589e-02, 'f_conv_w': 6.004159e-02, 'f_conv_b': 5.967724e-02, 'f_w_down': 9.764929e-02, 'ple_w_in': 8.532127e-02, 'ple_w_gate': 3.415914e-02, 'ple_b_gate': 5.449905e-02}


def _to_microbatches(a, axis):
    t = _jnp.moveaxis(a, axis, 0)
    t = t.reshape((N_MICROBATCH, t.shape[0] // N_MICROBATCH) + t.shape[1:])
    return _jnp.moveaxis(t, 1, axis + 1)


def setup_inputs(seed: int = 0) -> dict:
    inp = _fwd_setup_inputs(seed)
    key = _jax.random.fold_in(_jax.random.key(seed), 7919)
    shape, _ = _output_shape()
    out = dict(inp)
    out["loss_target"] = _jax.random.normal(_jax.random.fold_in(key, 0), shape, _jnp.float32)
    for i, name in enumerate(TWIN_WEIGHTS):
        w = inp[name].astype(_jnp.float32)
        if MOMENT_SCALE is None:
            s = _jnp.sqrt(_jnp.mean(_jnp.square(w)) + 1e-30)
        else:
            s = MOMENT_SCALE[name]
        km, kv = _jax.random.split(_jax.random.fold_in(key, i + 1))
        out[name] = w
        out["m_" + name] = s * _jax.random.normal(km, w.shape, _jnp.float32)
        out["v_" + name] = (s * s) * _jax.random.uniform(kv, w.shape, _jnp.float32, 0.5, 1.5)
    if N_MICROBATCH > 1:
        for name, axis in PER_EXAMPLE_BATCH_AXIS.items():
            out[name] = _to_microbatches(out[name], axis)
    return {'x': out['x'], 'p': out['p'], 'norm_mix': out['norm_mix'], 'norm_ffn': out['norm_ffn'], 'norm_ple': out['norm_ple'], 'norm_kv': out['norm_kv'], 'norm_final': out['norm_final'], 'a_w_in': out['a_w_in'], 'a_norm_v': out['a_norm_v'], 'a_w_s': out['a_w_s'], 'a_b_s': out['a_b_s'], 'a_w_out': out['a_w_out'], 'w_kv': out['w_kv'], 'b_w_q': out['b_w_q'], 'b_sinks': out['b_sinks'], 'b_w_o': out['b_w_o'], 'f_w_up': out['f_w_up'], 'f_conv_w': out['f_conv_w'], 'f_conv_b': out['f_conv_b'], 'f_w_down': out['f_w_down'], 'ple_w_in': out['ple_w_in'], 'ple_w_gate': out['ple_w_gate'], 'ple_b_gate': out['ple_b_gate'], 'loss_target': out['loss_target'], 'm_norm_mix': out['m_norm_mix'], 'm_norm_ffn': out['m_norm_ffn'], 'm_norm_ple': out['m_norm_ple'], 'm_norm_kv': out['m_norm_kv'], 'm_norm_final': out['m_norm_final'], 'm_a_w_in': out['m_a_w_in'], 'm_a_norm_v': out['m_a_norm_v'], 'm_a_w_s': out['m_a_w_s'], 'm_a_b_s': out['m_a_b_s'], 'm_a_w_out': out['m_a_w_out'], 'm_w_kv': out['m_w_kv'], 'm_b_w_q': out['m_b_w_q'], 'm_b_sinks': out['m_b_sinks'], 'm_b_w_o': out['m_b_w_o'], 'm_f_w_up': out['m_f_w_up'], 'm_f_conv_w': out['m_f_conv_w'], 'm_f_conv_b': out['m_f_conv_b'], 'm_f_w_down': out['m_f_w_down'], 'm_ple_w_in': out['m_ple_w_in'], 'm_ple_w_gate': out['m_ple_w_gate'], 'm_ple_b_gate': out['m_ple_b_gate'], 'v_norm_mix': out['v_norm_mix'], 'v_norm_ffn': out['v_norm_ffn'], 'v_norm_ple': out['v_norm_ple'], 'v_norm_kv': out['v_norm_kv'], 'v_norm_final': out['v_norm_final'], 'v_a_w_in': out['v_a_w_in'], 'v_a_norm_v': out['v_a_norm_v'], 'v_a_w_s': out['v_a_w_s'], 'v_a_b_s': out['v_a_b_s'], 'v_a_w_out': out['v_a_w_out'], 'v_w_kv': out['v_w_kv'], 'v_b_w_q': out['v_b_w_q'], 'v_b_sinks': out['v_b_sinks'], 'v_b_w_o': out['v_b_w_o'], 'v_f_w_up': out['v_f_w_up'], 'v_f_conv_w': out['v_f_conv_w'], 'v_f_conv_b': out['v_f_conv_b'], 'v_f_w_down': out['v_f_w_down'], 'v_ple_w_in': out['v_ple_w_in'], 'v_ple_w_gate': out['v_ple_w_gate'], 'v_ple_b_gate': out['v_ple_b_gate']}


def _loss(weights, diff, rest, loss_target):
    with _jax.named_scope("forward"):
        args = {**rest, TWIN_DIFF_INPUT: diff, **{k: w.astype(_WEIGHT_DTYPES[k]) for k, w in weights.items()}}
        y = _forward(args)
    with _jax.named_scope("loss_head"):
        err = _jnp.square(y.astype(_jnp.float32) - loss_target)
        return 0.5 * _jnp.sum(_jnp.mean(err, axis=-1)) if err.ndim else 0.5 * err


def _adamw(w, g, m, v):
    m = ADAM_B1 * m + (1.0 - ADAM_B1) * g
    v = ADAM_B2 * v + (1.0 - ADAM_B2) * _jnp.square(g)
    m_hat = m / (1.0 - ADAM_B1 ** ADAM_STEP)
    v_hat = v / (1.0 - ADAM_B2 ** ADAM_STEP)
    delta = -ADAM_LR * (m_hat / (_jnp.sqrt(v_hat) + ADAM_EPS) + ADAM_WD * w)
    return delta, m, v


def reference(x, p, norm_mix, norm_ffn, norm_ple, norm_kv, norm_final, a_w_in, a_norm_v, a_w_s, a_b_s, a_w_out, w_kv, b_w_q, b_sinks, b_w_o, f_w_up, f_conv_w, f_conv_b, f_w_down, ple_w_in, ple_w_gate, ple_b_gate, loss_target, m_norm_mix, m_norm_ffn, m_norm_ple, m_norm_kv, m_norm_final, m_a_w_in, m_a_norm_v, m_a_w_s, m_a_b_s, m_a_w_out, m_w_kv, m_b_w_q, m_b_sinks, m_b_w_o, m_f_w_up, m_f_conv_w, m_f_conv_b, m_f_w_down, m_ple_w_in, m_ple_w_gate, m_ple_b_gate, v_norm_mix, v_norm_ffn, v_norm_ple, v_norm_kv, v_norm_final, v_a_w_in, v_a_norm_v, v_a_w_s, v_a_b_s, v_a_w_out, v_w_kv, v_b_w_q, v_b_sinks, v_b_w_o, v_f_w_up, v_f_conv_w, v_f_conv_b, v_f_w_down, v_ple_w_in, v_ple_w_gate, v_ple_b_gate):
    given = dict(x=x, p=p, norm_mix=norm_mix, norm_ffn=norm_ffn, norm_ple=norm_ple, norm_kv=norm_kv, norm_final=norm_final, a_w_in=a_w_in, a_norm_v=a_norm_v, a_w_s=a_w_s, a_b_s=a_b_s, a_w_out=a_w_out, w_kv=w_kv, b_w_q=b_w_q, b_sinks=b_sinks, b_w_o=b_w_o, f_w_up=f_w_up, f_conv_w=f_conv_w, f_conv_b=f_conv_b, f_w_down=f_w_down, ple_w_in=ple_w_in, ple_w_gate=ple_w_gate, ple_b_gate=ple_b_gate, loss_target=loss_target, m_norm_mix=m_norm_mix, m_norm_ffn=m_norm_ffn, m_norm_ple=m_norm_ple, m_norm_kv=m_norm_kv, m_norm_final=m_norm_final, m_a_w_in=m_a_w_in, m_a_norm_v=m_a_norm_v, m_a_w_s=m_a_w_s, m_a_b_s=m_a_b_s, m_a_w_out=m_a_w_out, m_w_kv=m_w_kv, m_b_w_q=m_b_w_q, m_b_sinks=m_b_sinks, m_b_w_o=m_b_w_o, m_f_w_up=m_f_w_up, m_f_conv_w=m_f_conv_w, m_f_conv_b=m_f_conv_b, m_f_w_down=m_f_w_down, m_ple_w_in=m_ple_w_in, m_ple_w_gate=m_ple_w_gate, m_ple_b_gate=m_ple_b_gate, v_norm_mix=v_norm_mix, v_norm_ffn=v_norm_ffn, v_norm_ple=v_norm_ple, v_norm_kv=v_norm_kv, v_norm_final=v_norm_final, v_a_w_in=v_a_w_in, v_a_norm_v=v_a_norm_v, v_a_w_s=v_a_w_s, v_a_b_s=v_a_b_s, v_a_w_out=v_a_w_out, v_w_kv=v_w_kv, v_b_w_q=v_b_w_q, v_b_sinks=v_b_sinks, v_b_w_o=v_b_w_o, v_f_w_up=v_f_w_up, v_f_conv_w=v_f_conv_w, v_f_conv_b=v_f_conv_b, v_f_w_down=v_f_w_down, v_ple_w_in=v_ple_w_in, v_ple_w_gate=v_ple_w_gate, v_ple_b_gate=v_ple_b_gate)
    weights = {n: given[n] for n in TWIN_WEIGHTS}
    shared = {n: given[n] for n in SHARED_INPUTS}
    per_example = {n: given[n] for n in ['x', 'p']}
    grad_fn = _jax.value_and_grad(_loss, argnums=(0, 1))

    def one_microbatch(ex, loss_target):
        ex = dict(ex)
        diff = ex.pop(TWIN_DIFF_INPUT)
        return grad_fn(weights, diff, {**shared, **ex}, loss_target)

    if N_MICROBATCH == 1:
        loss, (grad_w, grad_x) = one_microbatch(per_example, given["loss_target"])
    else:
        def body(carry, xs):
            loss_sum, grad_sum = carry
            l_k, (gw_k, gx_k) = one_microbatch(xs[0], xs[1])
            with _jax.named_scope("update"):
                return (loss_sum + l_k, _jax.tree.map(_jnp.add, grad_sum, gw_k)), gx_k

        init = (_jnp.zeros((), _jnp.float32), _jax.tree.map(_jnp.zeros_like, weights))
        (loss, grad_w), grad_x = _jax.lax.scan(body, init, (per_example, given["loss_target"]))
    with _jax.named_scope("update"):
        delta_w, new_m, new_v = {}, {}, {}
        for n in TWIN_WEIGHTS:
            delta_w[n], new_m[n], new_v[n] = _adamw(weights[n], grad_w[n], given["m_" + n], given["v_" + n])
    return (loss, grad_x, *[grad_w[n] for n in TWIN_WEIGHTS], *[delta_w[n] for n in TWIN_WEIGHTS],
            *[new_m[n] for n in TWIN_WEIGHTS], *[new_v[n] for n in TWIN_WEIGHTS])
```

```python
import functools

import numpy as np
import jax
import jax.numpy as jnp
from jax import lax
from jax.experimental import pallas as pl
from jax.experimental.pallas import tpu as pltpu

F32 = jnp.float32
BF = jnp.bfloat16

N_DEV = 8
D = 1024
AW = 1024
GROUPS = 8
CHUNK = 128
HD = 64
N_Q = 16
N_KV = 4
GQA = 4
BLOCK = 128
D_FF = 2816
FF_BLK = 704
N_FF_BLK = 4
PLE = 256
EPS = 1e-6
LR, B1, B2, ADAM_EPS, WD, STEP = 0.001, 0.9, 0.999, 1e-08, 0.01, 10
GELU_C0 = 0.7978845608028654
GELU_C1 = 0.044715
NEG = -1e30
ROW = 1024
VMEM_BIG = 56 * 1024 * 1024

_PARAMS = ['norm_mix', 'norm_ffn', 'norm_ple', 'norm_kv', 'norm_final', 'a_w_in', 'a_norm_v', 'a_w_s', 'a_b_s',
           'a_w_out', 'w_kv', 'b_w_q', 'b_sinks', 'b_w_o', 'f_w_up', 'f_conv_w', 'f_conv_b', 'f_w_down',
           'ple_w_in', 'ple_w_gate', 'ple_b_gate']
_BIG = ['a_w_in', 'a_w_out', 'w_kv', 'b_w_q', 'b_w_o', 'f_w_up', 'f_w_down', 'ple_w_in', 'ple_w_gate']
_SMALL_SHARDED = ['a_norm_v', 'f_conv_w']
_REPL = ['norm_mix', 'norm_ffn', 'norm_ple', 'norm_kv', 'norm_final', 'a_w_s', 'a_b_s', 'b_sinks', 'f_conv_b',
         'ple_b_gate']


def _mm(a, b):
    return jnp.dot(a, b, preferred_element_type=F32)


def _mm_nt(a, b):
    return lax.dot_general(a, b, (((1,), (1,)), ((), ())), preferred_element_type=F32)


def _mm_tn(a, b):
    return lax.dot_general(a, b, (((0,), (0,)), ((), ())), preferred_element_type=F32)


def _const_spec(shape, single=False):
    nd = len(shape)
    if single:
        return pl.BlockSpec(shape, lambda *_: (0,) * nd, pipeline_mode=pl.Buffered(1))
    return pl.BlockSpec(shape, lambda *_: (0,) * nd)


def _row_spec(tm, n):
    return pl.BlockSpec((tm, n), lambda i: (i, 0))


def _rms_fwd(x, g):
    r = lax.rsqrt(jnp.mean(x * x, axis=-1, keepdims=True) + EPS)
    xhat = x * r
    return xhat * g, xhat, r


def _rms_bwd(dy, xhat, r, g):
    a = dy * g
    dx = r * (a - xhat * jnp.mean(a * xhat, axis=-1, keepdims=True))
    return dx, jnp.sum(dy * xhat, axis=0, keepdims=True)


def _sigmoid(x):
    return 1.0 / (1.0 + jnp.exp(-x))


def _params(vmem=None):
    return pltpu.CompilerParams(dimension_semantics=("arbitrary",), vmem_limit_bytes=vmem)


def _tri_masks():
    t = lax.broadcasted_iota(jnp.int32, (CHUNK, CHUNK), 0)
    s = lax.broadcasted_iota(jnp.int32, (CHUNK, CHUNK), 1)
    return t >= s, s >= t


def _mixa_fwd(x, g, win, gv, ws, bsb, wout, tm):
    S = x.shape[0]
    nc = tm // CHUNK

    def body(x_ref, g_ref, win_ref, gv_ref, ws_ref, bsb_ref, wout_ref, h_ref, z_ref, gated):
        x = x_ref[...]
        xn, _, _ = _rms_fwd(x, g_ref[...])
        z = _mm(xn.astype(BF), win_ref[...])
        z_ref[...] = z.astype(BF)
        zg = 0.5 * z * (1.0 + jnp.tanh(GELU_C0 * (z + GELU_C1 * z * z * z)))
        u = zg[:, :AW]
        vn, _, _ = _rms_fwd(zg[:, AW:], gv_ref[...])
        vn = vn.astype(BF)
        tril, _ = _tri_masks()
        for h in range(GROUPS):
            wm = jnp.where(tril, ws_ref[h], jnp.zeros((), BF))
            for c in range(nc):
                rs, cs = slice(c * CHUNK, (c + 1) * CHUNK), slice(h * CHUNK, (h + 1) * CHUNK)
                s = _mm(wm, vn[rs, cs]) + bsb_ref[h]
                gated[rs, cs] = (u[rs, cs] * s).astype(BF)
        h_ref[...] = x + _mm(gated[...], wout_ref[...])

    return pl.pallas_call(
        body, name="mixa_fwd", grid=(S // tm,),
        in_specs=[_row_spec(tm, D), _const_spec((1, D)), _const_spec((D, 2 * AW)), _const_spec((1, AW)),
                  _const_spec((GROUPS, CHUNK, CHUNK)), _const_spec((GROUPS, CHUNK, CHUNK)), _const_spec((AW, D))],
        out_specs=[_row_spec(tm, D), _row_spec(tm, 2 * AW)],
        out_shape=[jax.ShapeDtypeStruct((S, D), F32), jax.ShapeDtypeStruct((S, 2 * AW), BF)],
        scratch_shapes=[pltpu.VMEM((tm, AW), BF)],
        compiler_params=_params(VMEM_BIG),
    )(x, g, win, gv, ws, bsb, wout)


def _mixa_bwd(dh, x, z, g, win, gv, ws, wst, bsb, wout, tm):
    S = x.shape[0]
    nt = S // tm
    nc = tm // CHUNK

    def body(dh_ref, x_ref, z_ref, g_ref, win_ref, gv_ref, ws_ref, wst_ref, bsb_ref, wout_ref,
             dx_ref, xn_ref, dz_ref, gated_ref, dhb_ref, dws_ref, dbs_ref, dgv_ref, dg_ref, du_scr, dvn_scr):
        i = pl.program_id(0)

        @pl.when(i == 0)
        def _():
            dws_ref[...] = jnp.zeros_like(dws_ref)
            dbs_ref[...] = jnp.zeros_like(dbs_ref)
            dgv_ref[...] = jnp.zeros_like(dgv_ref)
            dg_ref[...] = jnp.zeros_like(dg_ref)

        x = x_ref[...]
        gmix = g_ref[...]
        xn, xhat, r = _rms_fwd(x, gmix)
        xn_ref[...] = xn.astype(BF)
        z = z_ref[...].astype(F32)
        z2 = z * z
        t = jnp.tanh(GELU_C0 * (z + GELU_C1 * z2 * z))
        zg = 0.5 * z * (1.0 + t)
        gp = 0.5 * (1.0 + t) + 0.5 * z * (1.0 - t * t) * (GELU_C0 * (1.0 + 3.0 * GELU_C1 * z2))
        u = zg[:, :AW]
        gvv = gv_ref[...]
        vn, vhat, rv = _rms_fwd(zg[:, AW:], gvv)
        vn = vn.astype(BF)
        dh = dh_ref[...]
        dhb = dh.astype(BF)
        dhb_ref[...] = dhb
        dgated = _mm_nt(dhb, wout_ref[...])
        tril, triu = _tri_masks()
        for h in range(GROUPS):
            wm = jnp.where(tril, ws_ref[h], jnp.zeros((), BF))
            wmt = jnp.where(triu, wst_ref[h], jnp.zeros((), BF))
            for c in range(nc):
                rs, cs = slice(c * CHUNK, (c + 1) * CHUNK), slice(h * CHUNK, (h + 1) * CHUNK)
                vnb = vn[rs, cs]
                s = _mm(wm, vnb) + bsb_ref[h]
                gated_ref[rs, cs] = (u[rs, cs] * s).astype(BF)
                dgt = dgated[rs, cs]
                du_scr[rs, cs] = dgt * s
                ds = dgt * u[rs, cs]
                dsb = ds.astype(BF)
                dws_ref[h] += jnp.where(tril, _mm_nt(dsb, vnb), 0.0)
                dbs_ref[h] += ds
                dvn_scr[rs, cs] = _mm(wmt, dsb)
        dvn = dvn_scr[...]
        dv, dgv = _rms_bwd(dvn, vhat, rv, gvv)
        dgv_ref[...] += dgv
        dz_ref[:, :AW] = (du_scr[...] * gp[:, :AW]).astype(BF)
        dz_ref[:, AW:] = (dv * gp[:, AW:]).astype(BF)
        dxn = _mm_nt(dz_ref[...], win_ref[...])
        dxr, dg = _rms_bwd(dxn, xhat, r, gmix)
        dg_ref[...] += dg
        dx_ref[...] = dh + dxr

        @pl.when(i == nt - 1)
        def _():
            for h in range(GROUPS):
                dbs_ref[h] = jnp.broadcast_to(jnp.sum(dbs_ref[h], axis=-1, keepdims=True), (CHUNK, CHUNK))

    gshape = (GROUPS, CHUNK, CHUNK)
    return pl.pallas_call(
        body, name="mixa_bwd", grid=(nt,),
        in_specs=[_row_spec(tm, D), _row_spec(tm, D), _row_spec(tm, 2 * AW), _const_spec((1, D)),
                  _const_spec((D, 2 * AW)), _const_spec((1, AW)), _const_spec(gshape), _const_spec(gshape),
                  _const_spec(gshape), _const_spec((AW, D))],
        out_specs=[_row_spec(tm, D), _row_spec(tm, D), _row_spec(tm, 2 * AW), _row_spec(tm, AW), _row_spec(tm, D),
                   _const_spec(gshape), _const_spec(gshape), _const_spec((1, AW)), _const_spec((1, D))],
        out_shape=[jax.ShapeDtypeStruct((S, D), F32), jax.ShapeDtypeStruct((S, D), BF),
                   jax.ShapeDtypeStruct((S, 2 * AW), BF), jax.ShapeDtypeStruct((S, AW), BF),
                   jax.ShapeDtypeStruct((S, D), BF), jax.ShapeDtypeStruct(gshape, F32),
                   jax.ShapeDtypeStruct(gshape, F32), jax.ShapeDtypeStruct((1, AW), F32),
                   jax.ShapeDtypeStruct((1, D), F32)],
        scratch_shapes=[pltpu.VMEM((tm, AW), F32), pltpu.VMEM((tm, AW), F32)],
        compiler_params=_params(VMEM_BIG),
    )(dh, x, z, g, win, gv, ws, wst, bsb, wout)


def _conv_fwd(h, prev8, w, b, rows):
    hm1 = jnp.where(rows == 0, prev8[7:8], pltpu.roll(h, 1, 0))
    hm2 = jnp.where(rows == 0, prev8[6:7], jnp.where(rows == 1, prev8[7:8], pltpu.roll(h, 2, 0)))
    return b + w[0:1] * hm2 + w[1:2] * hm1 + w[2:3] * h, hm1, hm2


def _ffn_fwd(h, g, wup, cw, cb, wdown, tm):
    S = h.shape[0]

    def body(h_ref, g_ref, wup_ref, cw_ref, cb_ref, wdown_ref, o_ref, hh_ref, carry):
        i = pl.program_id(0)

        @pl.when(i == 0)
        def _():
            carry[...] = jnp.zeros_like(carry)

        x = h_ref[...]
        xn, _, _ = _rms_fwd(x, g_ref[...])
        xn = xn.astype(BF)
        rows = lax.broadcasted_iota(jnp.int32, (tm, 1), 0)
        acc = x
        for d in range(N_FF_BLK):
            cs = []
            for blk in (d, N_FF_BLK + d):
                hb = _mm(xn, wup_ref[blk])
                hh_ref[blk] = hb.astype(BF)
                c, _, _ = _conv_fwd(hb, carry[blk], cw_ref[blk], cb_ref[blk], rows)
                carry[blk] = hb[tm - 8:, :]
                cs.append(c)
            act = cs[0] * _sigmoid(cs[0]) * cs[1]
            acc = acc + _mm(act.astype(BF), wdown_ref[d * FF_BLK:(d + 1) * FF_BLK, :])
        o_ref[...] = acc

    return pl.pallas_call(
        body, name="ffn_fwd", grid=(S // tm,),
        in_specs=[_row_spec(tm, D), _const_spec((1, D)), _const_spec((N_DEV, D, FF_BLK), True),
                  _const_spec((N_DEV, 3, FF_BLK)), _const_spec((N_DEV, 1, FF_BLK)), _const_spec((D_FF, D), True)],
        out_specs=[_row_spec(tm, D), pl.BlockSpec((N_DEV, tm, FF_BLK), lambda i: (0, i, 0))],
        out_shape=[jax.ShapeDtypeStruct((S, D), F32), jax.ShapeDtypeStruct((N_DEV, S, FF_BLK), BF)],
        scratch_shapes=[pltpu.VMEM((N_DEV, 8, FF_BLK), F32)],
        compiler_params=_params(VMEM_BIG),
    )(h, g, wup, cw, cb, wdown)


def _ffn_bwd(dh, h, hh, g, wup, cw, cb, wdown, tm):
    S = h.shape[0]
    nt = S // tm
    halo = 16
    per = tm // halo

    def rev(i):
        return (nt - 1 - i, 0)

    def body(dh_ref, h_ref, hh_ref, halo_ref, g_ref, wup_ref, cw_ref, cb_ref, wdown_ref,
             dx_ref, xn_ref, dhh_ref, act_ref, dhb_ref, dcw_ref, dcb_ref, dg_ref, carry):
        i = pl.program_id(0)

        @pl.when(i == 0)
        def _():
            carry[...] = jnp.zeros_like(carry)
            dcw_ref[...] = jnp.zeros_like(dcw_ref)
            dcb_ref[...] = jnp.zeros_like(dcb_ref)
            dg_ref[...] = jnp.zeros_like(dg_ref)

        first_tile = i == nt - 1
        x = h_ref[...]
        gain = g_ref[...]
        xn, xhat, r = _rms_fwd(x, gain)
        xn_ref[...] = xn.astype(BF)
        dh = dh_ref[...]
        dhb = dh.astype(BF)
        dhb_ref[...] = dhb
        rows = lax.broadcasted_iota(jnp.int32, (tm, 1), 0)
        dxn = jnp.zeros((tm, D), F32)
        for d in range(N_FF_BLK):
            blks = (d, N_FF_BLK + d)
            hs, cs, hm1s, hm2s = [], [], [], []
            for blk in blks:
                hb = hh_ref[blk].astype(F32)
                prev8 = jnp.where(first_tile, 0.0, halo_ref[blk][halo - 8:, :].astype(F32))
                c, hm1, hm2 = _conv_fwd(hb, prev8, cw_ref[blk], cb_ref[blk], rows)
                hs.append(hb), cs.append(c), hm1s.append(hm1), hm2s.append(hm2)
            cg, cu = cs
            sg = _sigmoid(cg)
            sil = cg * sg
            act_ref[d] = (sil * cu).astype(BF)
            dact = _mm_nt(dhb, wdown_ref[d * FF_BLK:(d + 1) * FF_BLK, :])
            dcs = (dact * cu * (sg * (1.0 + cg * (1.0 - sg))), dact * sil)
            for blk, dc, hb, hm1, hm2 in zip(blks, dcs, hs, hm1s, hm2s):
                dcb_ref[blk] += jnp.sum(dc, axis=0, keepdims=True)
                dcw_ref[blk, 0:1, :] += jnp.sum(dc * hm2, axis=0, keepdims=True)
                dcw_ref[blk, 1:2, :] += jnp.sum(dc * hm1, axis=0, keepdims=True)
                dcw_ref[blk, 2:3, :] += jnp.sum(dc * hb, axis=0, keepdims=True)
                nxt = carry[blk]
                dp1 = jnp.where(rows == tm - 1, nxt[0:1], pltpu.roll(dc, tm - 1, 0))
                dp2 = jnp.where(rows == tm - 1, nxt[1:2], jnp.where(rows == tm - 2, nxt[0:1], pltpu.roll(dc, tm - 2, 0)))
                w = cw_ref[blk]
                dhh = (w[2:3] * dc + w[1:2] * dp1 + w[0:1] * dp2).astype(BF)
                carry[blk] = dc[0:8, :]
                dhh_ref[blk] = dhh
                dxn = dxn + _mm_nt(dhh, wup_ref[blk])
        dxr, dg = _rms_bwd(dxn, xhat, r, gain)
        dg_ref[...] += dg
        dx_ref[...] = dh + dxr

    blk3 = lambda i: (0, nt - 1 - i, 0)
    return pl.pallas_call(
        body, name="ffn_bwd", grid=(nt,),
        in_specs=[pl.BlockSpec((tm, D), rev), pl.BlockSpec((tm, D), rev),
                  pl.BlockSpec((N_DEV, tm, FF_BLK), blk3),
                  pl.BlockSpec((N_DEV, halo, FF_BLK), lambda i: (0, jnp.maximum((nt - 1 - i) * per - 1, 0), 0)),
                  _const_spec((1, D)), _const_spec((N_DEV, D, FF_BLK), True), _const_spec((N_DEV, 3, FF_BLK)),
                  _const_spec((N_DEV, 1, FF_BLK)), _const_spec((D_FF, D), True)],
        out_specs=[pl.BlockSpec((tm, D), rev), pl.BlockSpec((tm, D), rev),
                   pl.BlockSpec((N_DEV, tm, FF_BLK), blk3), pl.BlockSpec((N_FF_BLK, tm, FF_BLK), blk3),
                   pl.BlockSpec((tm, D), rev), _const_spec((N_DEV, 3, FF_BLK)), _const_spec((N_DEV, 1, FF_BLK)),
                   _const_spec((1, D))],
        out_shape=[jax.ShapeDtypeStruct((S, D), F32), jax.ShapeDtypeStruct((S, D), BF),
                   jax.ShapeDtypeStruct((N_DEV, S, FF_BLK), BF), jax.ShapeDtypeStruct((N_FF_BLK, S, FF_BLK), BF),
                   jax.ShapeDtypeStruct((S, D), BF), jax.ShapeDtypeStruct((N_DEV, 3, FF_BLK), F32),
                   jax.ShapeDtypeStruct((N_DEV, 1, FF_BLK), F32), jax.ShapeDtypeStruct((1, D), F32)],
        scratch_shapes=[pltpu.VMEM((N_DEV, 8, FF_BLK), F32)],
        compiler_params=_params(VMEM_BIG),
    )(dh, h, hh, hh, g, wup, cw, cb, wdown)


def _ple_fwd(h, g, wgate, bgate, p, wple, tm):
    S = h.shape[0]

    def body(h_ref, g_ref, wg_ref, bg_ref, p_ref, wp_ref, o_ref):
        x = h_ref[...]
        xn, _, _ = _rms_fwd(x, g_ref[...])
        gate = _sigmoid(_mm(xn.astype(BF), wg_ref[...]) + bg_ref[...])
        o_ref[...] = x + _mm(p_ref[...], wp_ref[...]) * gate

    return pl.pallas_call(
        body, name="ple_fwd", grid=(S // tm,),
        in_specs=[_row_spec(tm, D), _const_spec((1, D)), _const_spec((D, D)), _const_spec((1, D)),
                  _row_spec(tm, PLE), _const_spec((PLE, D))],
        out_specs=_row_spec(tm, D), out_shape=jax.ShapeDtypeStruct((S, D), F32),
        compiler_params=_params(),
    )(h, g, wgate, bgate, p, wple)


def _ple_bwd(dh, h, g, wgate, bgate, p, wple, tm):
    S = h.shape[0]

    def body(dh_ref, h_ref, g_ref, wg_ref, bg_ref, p_ref, wp_ref, dx_ref, de_ref, xn_ref, dpre_ref, dbg_ref, dg_ref):
        i = pl.program_id(0)

        @pl.when(i == 0)
        def _():
            dbg_ref[...] = jnp.zeros_like(dbg_ref)
            dg_ref[...] = jnp.zeros_like(dg_ref)

        x = h_ref[...]
        gain = g_ref[...]
        xn, xhat, r = _rms_fwd(x, gain)
        xnb = xn.astype(BF)
        xn_ref[...] = xnb
        gate = _sigmoid(_mm(xnb, wg_ref[...]) + bg_ref[...])
        e = _mm(p_ref[...], wp_ref[...])
        dh = dh_ref[...]
        de_ref[...] = (dh * gate).astype(BF)
        dpre = dh * e * gate * (1.0 - gate)
        dpreb = dpre.astype(BF)
        dpre_ref[...] = dpreb
        dbg_ref[...] += jnp.sum(dpre, axis=0, keepdims=True)
        dxr, dg = _rms_bwd(_mm_nt(dpreb, wg_ref[...]), xhat, r, gain)
        dg_ref[...] += dg
        dx_ref[...] = dh + dxr

    return pl.pallas_call(
        body, name="ple_bwd", grid=(S // tm,),
        in_specs=[_row_spec(tm, D), _row_spec(tm, D), _const_spec((1, D)), _const_spec((D, D)), _const_spec((1, D)),
                  _row_spec(tm, PLE), _const_spec((PLE, D))],
        out_specs=[_row_spec(tm, D), _row_spec(tm, D), _row_spec(tm, D), _row_spec(tm, D), _const_spec((1, D)),
                   _const_spec((1, D))],
        out_shape=[jax.ShapeDtypeStruct((S, D), F32), jax.ShapeDtypeStruct((S, D), BF),
                   jax.ShapeDtypeStruct((S, D), BF), jax.ShapeDtypeStruct((S, D), BF),
                   jax.ShapeDtypeStruct((1, D), F32), jax.ShapeDtypeStruct((1, D), F32)],
        compiler_params=_params(),
    )(dh, h, g, wgate, bgate, p, wple)


def _qkv_fwd(h, gq, wq, gkv, wkv, tm):
    S = h.shape[0]
    nkv = wkv.shape[1]

    def body(h_ref, gq_ref, wq_ref, gkv_ref, wkv_ref, q_ref, kv_ref):
        x = h_ref[...]
        _, xhat, _ = _rms_fwd(x, gq_ref[...])
        q_ref[...] = _mm((xhat * gq_ref[...]).astype(BF), wq_ref[...]).astype(BF)
        kv_ref[...] = _mm((xhat * gkv_ref[...]).astype(BF), wkv_ref[...]).astype(BF)

    return pl.pallas_call(
        body, name="qkv_fwd", grid=(S // tm,),
        in_specs=[_row_spec(tm, D), _const_spec((1, D)), _const_spec((D, D)), _const_spec((1, D)),
                  _const_spec((D, nkv))],
        out_specs=[_row_spec(tm, D), _row_spec(tm, nkv)],
        out_shape=[jax.ShapeDtypeStruct((S, D), BF), jax.ShapeDtypeStruct((S, nkv), BF)],
        compiler_params=_params(),
    )(h, gq, wq, gkv, wkv)


def _qkv_bwd(dh, h, dq, dkv, gq, wq, gkv, wkv, tm):
    S = h.shape[0]
    nkv = wkv.shape[1]

    def body(dh_ref, h_ref, dq_ref, dkv_ref, gq_ref, wq_ref, gkv_ref, wkv_ref,
             dx_ref, xq_ref, xkv_ref, dgq_ref, dgkv_ref):
        i = pl.program_id(0)

        @pl.when(i == 0)
        def _():
            dgq_ref[...] = jnp.zeros_like(dgq_ref)
            dgkv_ref[...] = jnp.zeros_like(dgkv_ref)

        x = h_ref[...]
        gq_, gkv_ = gq_ref[...], gkv_ref[...]
        _, xhat, r = _rms_fwd(x, gq_)
        xq_ref[...] = (xhat * gq_).astype(BF)
        xkv_ref[...] = (xhat * gkv_).astype(BF)
        d1, dg1 = _rms_bwd(_mm_nt(dq_ref[...], wq_ref[...]), xhat, r, gq_)
        d2, dg2 = _rms_bwd(_mm_nt(dkv_ref[...], wkv_ref[...]), xhat, r, gkv_)
        dgq_ref[...] += dg1
        dgkv_ref[...] += dg2
        dx_ref[...] = dh_ref[...] + d1 + d2

    return pl.pallas_call(
        body, name="qkv_bwd", grid=(S // tm,),
        in_specs=[_row_spec(tm, D), _row_spec(tm, D), _row_spec(tm, D), _row_spec(tm, nkv), _const_spec((1, D)),
                  _const_spec((D, D)), _const_spec((1, D)), _const_spec((D, nkv))],
        out_specs=[_row_spec(tm, D), _row_spec(tm, D), _row_spec(tm, D), _const_spec((1, D)), _const_spec((1, D))],
        out_shape=[jax.ShapeDtypeStruct((S, D), F32), jax.ShapeDtypeStruct((S, D), BF),
                   jax.ShapeDtypeStruct((S, D), BF), jax.ShapeDtypeStruct((1, D), F32),
                   jax.ShapeDtypeStruct((1, D), F32)],
        compiler_params=_params(),
    )(dh, h, dq, dkv, gq, wq, gkv, wkv)


def _oproj_fwd(a, w, res, tm):
    S = a.shape[0]

    def body(a_ref, w_ref, r_ref, o_ref):
        o_ref[...] = r_ref[...] + _mm(a_ref[...], w_ref[...])

    return pl.pallas_call(
        body, name="oproj_fwd", grid=(S // tm,),
        in_specs=[_row_spec(tm, D), _const_spec((D, D)), _row_spec(tm, D)],
        out_specs=_row_spec(tm, D), out_shape=jax.ShapeDtypeStruct((S, D), F32),
        compiler_params=_params(),
    )(a, w, res)


def _oproj_bwd(dh, w, tm):
    S = dh.shape[0]

    def body(dh_ref, w_ref, da_ref, dhb_ref):
        dhb = dh_ref[...].astype(BF)
        dhb_ref[...] = dhb
        da_ref[...] = _mm_nt(dhb, w_ref[...]).astype(BF)

    return pl.pallas_call(
        body, name="oproj_bwd", grid=(S // tm,),
        in_specs=[_row_spec(tm, D), _const_spec((D, D))],
        out_specs=[_row_spec(tm, D), _row_spec(tm, D)],
        out_shape=[jax.ShapeDtypeStruct((S, D), BF), jax.ShapeDtypeStruct((S, D), BF)],
        compiler_params=_params(),
    )(dh, w)


def _alibi_slope(hq):
    return float(np.float32(2.0 ** (-8.0 * (hq + 1) / N_Q)))


def _replication():
    r = np.zeros((N_KV, N_KV * HD, GQA * HD), np.float32)
    for kh in range(N_KV):
        for c in range(GQA * HD):
            r[kh, kh * HD + c % HD, c] = 1.0
    return jnp.asarray(r, BF)


def _band_masks(blk):
    qi = lax.broadcasted_iota(jnp.int32, (BLOCK, 2 * BLOCK), 0)
    kj = lax.broadcasted_iota(jnp.int32, (BLOCK, 2 * BLOCK), 1)
    dist = qi + BLOCK - kj
    valid = (dist >= 0) & (dist < BLOCK) & ((blk > 0) | (kj >= BLOCK))
    lane_head = lax.broadcasted_iota(jnp.int32, (BLOCK, GQA * HD), 1) // HD
    return dist.astype(F32), valid, lane_head


def _probs(qm, k4, dist, valid, slope, sink):
    s = _mm_nt(qm, k4) * (HD ** -0.5) - slope * dist
    s = jnp.where(valid, s, NEG)
    m = jnp.maximum(jnp.max(s, axis=-1, keepdims=True), sink)
    e = jnp.exp(s - m)
    es = jnp.exp(sink - m)
    inv = 1.0 / (jnp.sum(e, axis=-1, keepdims=True) + es)
    return e * inv, es * inv


def _attn_fwd(q, kv, sinks, rep):
    S = q.shape[0]
    nb = S // BLOCK
    W = N_KV * HD

    def body(sink_ref, q_ref, kvc_ref, kvp_ref, rep_ref, o_ref):
        blk = pl.program_id(0)
        dist, valid, lane_head = _band_masks(blk)
        kvp, kvc = kvp_ref[...], kvc_ref[...]
        kband = jnp.concatenate([kvp[:, :W], kvc[:, :W]], axis=0)
        vband = jnp.concatenate([kvp[:, W:], kvc[:, W:]], axis=0)
        for kh in range(N_KV):
            k4 = _mm(kband, rep_ref[kh]).astype(BF)
            v4 = _mm(vband, rep_ref[kh]).astype(BF)
            qg = q_ref[:, kh * W:(kh + 1) * W]
            og = jnp.zeros((BLOCK, W), F32)
            for g in range(GQA):
                hq = kh * GQA + g
                qm = jnp.where(lane_head == g, qg, jnp.zeros((), BF))
                p, _ = _probs(qm, k4, dist, valid, _alibi_slope(hq), sink_ref[hq])
                og = jnp.where(lane_head == g, _mm(p.astype(BF), v4), og)
            o_ref[:, kh * W:(kh + 1) * W] = og.astype(BF)

    return pl.pallas_call(
        body, name="attn_fwd", grid=(nb,),
        in_specs=[pl.BlockSpec(memory_space=pltpu.SMEM), _row_spec(BLOCK, D), _row_spec(BLOCK, 2 * W),
                  pl.BlockSpec((BLOCK, 2 * W), lambda i: (jnp.maximum(i - 1, 0), 0)), _const_spec((N_KV, W, W))],
        out_specs=_row_spec(BLOCK, D), out_shape=jax.ShapeDtypeStruct((S, D), BF),
        compiler_params=_params(),
    )(sinks, q, kv, kv, rep)


def _attn_bwd(q, kv, do, sinks, rep):
    S = q.shape[0]
    nb = S // BLOCK
    W = N_KV * HD

    def qmap(i):
        return (jnp.minimum(i, nb - 1), 0)

    def body(sink_ref, q_ref, kvc_ref, kvp_ref, do_ref, rep_ref, dq_ref, dkv_ref, dsink_ref, carry):
        i = pl.program_id(0)

        @pl.when(i == 0)
        def _():
            dsink_ref[...] = jnp.zeros_like(dsink_ref)
            carry[...] = jnp.zeros_like(carry)

        @pl.when(i < nb)
        def _():
            dist, valid, lane_head = _band_masks(i)
            kvp, kvc = kvp_ref[...], kvc_ref[...]
            kband = jnp.concatenate([kvp[:, :W], kvc[:, :W]], axis=0)
            vband = jnp.concatenate([kvp[:, W:], kvc[:, W:]], axis=0)
            dkband = jnp.zeros((2 * BLOCK, W), F32)
            dvband = jnp.zeros((2 * BLOCK, W), F32)
            for kh in range(N_KV):
                k4 = _mm(kband, rep_ref[kh]).astype(BF)
                v4 = _mm(vband, rep_ref[kh]).astype(BF)
                qg = q_ref[:, kh * W:(kh + 1) * W]
                dog = do_ref[:, kh * W:(kh + 1) * W]
                dqg = jnp.zeros((BLOCK, W), F32)
                dk4 = jnp.zeros((2 * BLOCK, W), F32)
                dv4 = jnp.zeros((2 * BLOCK, W), F32)
                for g in range(GQA):
                    hq = kh * GQA + g
                    sel = lane_head == g
                    qm = jnp.where(sel, qg, jnp.zeros((), BF))
                    dom = jnp.where(sel, dog, jnp.zeros((), BF))
                    p, psink = _probs(qm, k4, dist, valid, _alibi_slope(hq), sink_ref[hq])
                    dp = _mm_nt(dom, v4)
                    delta = jnp.sum(p * dp, axis=-1, keepdims=True)
                    ds = (p * (dp - delta)).astype(BF)
                    dsink_ref[hq:hq + 1, :] += jnp.broadcast_to(
                        -jnp.sum(psink * delta, axis=0, keepdims=True), (1, BLOCK))
                    dqg = jnp.where(sel, _mm(ds, k4), dqg)
                    dk4 = dk4 + _mm_tn(ds, qm)
                    dv4 = dv4 + _mm_tn(p.astype(BF), dom)
                dq_ref[:, kh * W:(kh + 1) * W] = (dqg * (HD ** -0.5)).astype(BF)
                dkband = dkband + _mm_nt((dk4 * (HD ** -0.5)).astype(BF), rep_ref[kh])
                dvband = dvband + _mm_nt(dv4.astype(BF), rep_ref[kh])
            dkv_ref[:, :W] = (carry[:, :W] + dkband[:BLOCK]).astype(BF)
            dkv_ref[:, W:] = (carry[:, W:] + dvband[:BLOCK]).astype(BF)
            carry[:, :W] = dkband[BLOCK:]
            carry[:, W:] = dvband[BLOCK:]

        @pl.when(i == nb)
        def _():
            dkv_ref[...] = carry[...].astype(BF)

    return pl.pallas_call(
        body, name="attn_bwd", grid=(nb + 1,),
        in_specs=[pl.BlockSpec(memory_space=pltpu.SMEM), pl.BlockSpec((BLOCK, D), qmap),
                  pl.BlockSpec((BLOCK, 2 * W), qmap),
                  pl.BlockSpec((BLOCK, 2 * W), lambda i: (jnp.maximum(jnp.minimum(i, nb - 1) - 1, 0), 0)),
                  pl.BlockSpec((BLOCK, D), qmap), _const_spec((N_KV, W, W))],
        out_specs=[pl.BlockSpec((BLOCK, D), qmap),
                   pl.BlockSpec((BLOCK, 2 * W), lambda i: (jnp.maximum(i - 1, 0), 0)),
                   _const_spec((N_Q, BLOCK))],
        out_shape=[jax.ShapeDtypeStruct((S, D), BF), jax.ShapeDtypeStruct((S, 2 * W), BF),
                   jax.ShapeDtypeStruct((N_Q, BLOCK), F32)],
        scratch_shapes=[pltpu.VMEM((BLOCK, 2 * W), F32)],
        compiler_params=_params(),
    )(sinks, q, kv, kv, do, rep)


def _head(h, g, target, tm):
    S = h.shape[0]

    def body(h_ref, g_ref, t_ref, dh_ref, loss_ref, dg_ref):
        i = pl.program_id(0)

        @pl.when(i == 0)
        def _():
            loss_ref[...] = jnp.zeros_like(loss_ref)
            dg_ref[...] = jnp.zeros_like(dg_ref)

        gain = g_ref[...]
        y, xhat, r = _rms_fwd(h_ref[...], gain)
        err = y - t_ref[...]
        tile = 0.5 * jnp.sum(jnp.sum(err * err, axis=-1, keepdims=True) * (1.0 / D), axis=0, keepdims=True)
        loss_ref[...] += jnp.broadcast_to(tile, loss_ref.shape)
        dx, dg = _rms_bwd(err * (1.0 / D), xhat, r, gain)
        dg_ref[...] += dg
        dh_ref[...] = dx

    return pl.pallas_call(
        body, name="loss_head", grid=(S // tm,),
        in_specs=[_row_spec(tm, D), _const_spec((1, D)), _row_spec(tm, D)],
        out_specs=[_row_spec(tm, D), _const_spec((8, 128)), _const_spec((1, D))],
        out_shape=[jax.ShapeDtypeStruct((S, D), F32), jax.ShapeDtypeStruct((8, 128), F32),
                   jax.ShapeDtypeStruct((1, D), F32)],
        compiler_params=_params(),
    )(h, g, target)


def _wgrad(a, b, name, *, tn, out_blocked=False, ts=512):
    a_blocked, b_blocked = a.ndim == 3, b.ndim == 3
    na = a.shape[0] if a_blocked else 1
    S, K = a.shape[-2], a.shape[-1]
    nbk = b.shape[0] if b_blocked else b.shape[1] // tn
    ts = min(ts, S)
    ns = S // ts
    assert not (a_blocked and out_blocked)

    def body(a_ref, b_ref, o_ref, acc):
        s = pl.program_id(2)

        @pl.when(s == 0)
        def _():
            acc[...] = jnp.zeros_like(acc)

        acc[...] += _mm_tn(a_ref[...], b_ref[...])

        @pl.when(s == ns - 1)
        def _():
            o_ref[...] = acc[...].astype(o_ref.dtype)

    a_spec = (pl.BlockSpec((None, ts, K), lambda i, j, s: (i, s, 0)) if a_blocked
              else pl.BlockSpec((ts, K), lambda i, j, s: (s, 0)))
    b_spec = (pl.BlockSpec((None, ts, tn), lambda i, j, s: (j, s, 0)) if b_blocked
              else pl.BlockSpec((ts, tn), lambda i, j, s: (s, j)))
    if out_blocked:
        o_spec = pl.BlockSpec((None, K, tn), lambda i, j, s: (j, 0, 0))
        o_shape = jax.ShapeDtypeStruct((nbk, K, tn), BF)
    else:
        o_spec = pl.BlockSpec((K, tn), lambda i, j, s: (i, j))
        o_shape = jax.ShapeDtypeStruct((na * K, nbk * tn), BF)
    return pl.pallas_call(
        body, name=name, grid=(na, nbk, ns), in_specs=[a_spec, b_spec], out_specs=o_spec, out_shape=o_shape,
        scratch_shapes=[pltpu.VMEM((K, tn), F32)],
        compiler_params=pltpu.CompilerParams(dimension_semantics=("arbitrary", "arbitrary", "arbitrary")),
    )(a, b)


def _adamw_math(w, g, m, v):
    m = B1 * m + (1.0 - B1) * g
    v = B2 * v + (1.0 - B2) * (g * g)
    m_hat = m / (1.0 - B1 ** STEP)
    v_hat = v / (1.0 - B2 ** STEP)
    return -LR * (m_hat / (jnp.sqrt(v_hat) + ADAM_EPS) + WD * w), m, v


def _adamw(parts, w, m, v, name, tr):
    R = w.shape[0]
    tr = min(tr, R)

    def body(p_ref, w_ref, m_ref, v_ref, g_ref, d_ref, nm_ref, nv_ref):
        g = p_ref[0].astype(F32)
        for s in range(1, N_DEV):
            g = g + p_ref[s].astype(F32)
        g_ref[...] = g
        d_ref[...], nm_ref[...], nv_ref[...] = _adamw_math(w_ref[...], g, m_ref[...], v_ref[...])

    rs = _row_spec(tr, ROW)
    return pl.pallas_call(
        body, name=name, grid=(R // tr,),
        in_specs=[pl.BlockSpec((N_DEV, tr, ROW), lambda i: (0, i, 0)), rs, rs, rs],
        out_specs=[rs, rs, rs, rs], out_shape=[jax.ShapeDtypeStruct((R, ROW), F32)] * 4,
        compiler_params=_params(),
    )(parts, w, m, v)


def _sum8(parts, name):
    R = parts.shape[1]

    def body(p_ref, o_ref):
        g = p_ref[0]
        for s in range(1, N_DEV):
            g = g + p_ref[s]
        o_ref[...] = g

    return pl.pallas_call(
        body, name=name, grid=(1,),
        in_specs=[pl.BlockSpec((N_DEV, R, ROW), lambda i: (0, 0, 0))],
        out_specs=pl.BlockSpec((R, ROW), lambda i: (0, 0)), out_shape=jax.ShapeDtypeStruct((R, ROW), F32),
        compiler_params=_params(),
    )(parts)


def _adamw_small(g, w, m, v, name):
    R = w.shape[0]

    def body(g_ref, w_ref, m_ref, v_ref, d_ref, nm_ref, nv_ref):
        d_ref[...], nm_ref[...], nv_ref[...] = _adamw_math(w_ref[...], g_ref[...], m_ref[...], v_ref[...])

    spec = pl.BlockSpec((R, ROW), lambda i: (0, 0))
    return pl.pallas_call(
        body, name=name, grid=(1,), in_specs=[spec] * 4, out_specs=[spec] * 3,
        out_shape=[jax.ShapeDtypeStruct((R, ROW), F32)] * 3, compiler_params=_params(),
    )(g, w, m, v)


def _mesh_pos():
    return lax.axis_index("x"), lax.axis_index("y"), lax.axis_index("c")


def _flip(v, bit):
    return 1 - v if bit else v


def _all_gather(xs, name):
    R, C = xs.shape
    MESH = pl.DeviceIdType.MESH

    def body(x_ref, out_ref, send_sems, recv_sems, local_sem):
        x, y, c = _mesh_pos()
        me, sibling = (x, y, c), (x, y, 1 - c)
        chips = [(1 - x, y), (x, 1 - y), (1 - x, 1 - y)]

        def rows(px, py, pc):
            return out_ref.at[4 * px + 2 * py + pc]

        def copy(k, block, to, src=None):
            return pltpu.make_async_remote_copy(
                src_ref=rows(*block) if src is None else src, dst_ref=rows(*block),
                send_sem=send_sems.at[k], recv_sem=recv_sems.at[k], device_id=to, device_id_type=MESH)

        mine = pltpu.make_async_copy(x_ref, rows(*me), local_sem)
        mine.start()
        first = [copy(0, me, sibling, src=x_ref)]
        first += [copy(1 + j, me, (*chip, c), src=x_ref) for j, chip in enumerate(chips)]
        for cp in first:
            cp.start()
        passed = [copy(4 + j, (*chip, c), sibling) for j, chip in enumerate(chips)]
        for j, chip in enumerate(chips):
            copy(1 + j, (*chip, c), me).wait_recv()
            passed[j].start()
        copy(0, sibling, me).wait_recv()
        for j, chip in enumerate(chips):
            copy(4 + j, (*chip, 1 - c), me).wait_recv()
        for cp in first + passed:
            cp.wait_send()
        mine.wait()

    return pl.pallas_call(
        body, name=name, out_shape=jax.ShapeDtypeStruct((N_DEV, R, C), xs.dtype),
        in_specs=[pl.BlockSpec(memory_space=pl.ANY)], out_specs=pl.BlockSpec(memory_space=pl.ANY),
        scratch_shapes=[pltpu.SemaphoreType.DMA((7,)), pltpu.SemaphoreType.DMA((7,)), pltpu.SemaphoreType.DMA(())],
    )(xs)


def _all_to_all(g, name):
    _, R, C = g.shape
    MESH = pl.DeviceIdType.MESH

    def body(g_ref, out_ref, send_sems, recv_sems, local_sem):
        x, y, c = _mesh_pos()
        me = 4 * x + 2 * y + c
        mine = pltpu.make_async_copy(g_ref.at[me], out_ref.at[me], local_sem)
        mine.start()
        copies = []
        for k in range(1, N_DEV):
            px, py, pc = _flip(x, k & 4), _flip(y, k & 2), _flip(c, k & 1)
            peer = 4 * px + 2 * py + pc
            copies.append((pltpu.make_async_remote_copy(
                src_ref=g_ref.at[peer], dst_ref=out_ref.at[me], send_sem=send_sems.at[k - 1],
                recv_sem=recv_sems.at[k - 1], device_id=(px, py, pc), device_id_type=MESH),
                pltpu.make_async_remote_copy(
                src_ref=g_ref.at[peer], dst_ref=out_ref.at[peer], send_sem=send_sems.at[k - 1],
                recv_sem=recv_sems.at[k - 1], device_id=(px, py, pc), device_id_type=MESH)))
        for send, _ in copies:
            send.start()
        for send, recv in copies:
            recv.wait_recv()
        for send, _ in copies:
            send.wait_send()
        mine.wait()

    return pl.pallas_call(
        body, name=name, out_shape=jax.ShapeDtypeStruct(g.shape, g.dtype),
        in_specs=[pl.BlockSpec(memory_space=pl.ANY)], out_specs=pl.BlockSpec(memory_space=pl.ANY),
        scratch_shapes=[pltpu.SemaphoreType.DMA((7,)), pltpu.SemaphoreType.DMA((7,)), pltpu.SemaphoreType.DMA(())],
    )(g)


def _seg_rows(n, align):
    rows = -(-n // ROW)
    return -(-rows // align) * align


def _pack(arrs, align, lead=()):
    parts = []
    for a in arrs:
        n = int(np.prod(a.shape[len(lead):]))
        rows = _seg_rows(n, align)
        flat = a.reshape(lead + (n,))
        flat = jnp.pad(flat, [(0, 0)] * len(lead) + [(0, rows * ROW - n)])
        parts.append(flat.reshape(lead + (rows, ROW)))
    return jnp.concatenate(parts, axis=len(lead))


def _unpack(buf, shapes, align, lead=()):
    out, r0 = [], 0
    for shp in shapes:
        n = int(np.prod(shp))
        rows = _seg_rows(n, align)
        seg = lax.slice_in_dim(buf, r0, r0 + rows, axis=len(lead))
        out.append(seg.reshape(lead + (rows * ROW,))[..., :n].reshape(lead + tuple(shp)))
        r0 += rows
    return out


def kernel(x, p, norm_mix, norm_ffn, norm_ple, norm_kv, norm_final, a_w_in, a_norm_v, a_w_s, a_b_s, a_w_out, w_kv, b_w_q, b_sinks, b_w_o, f_w_up, f_conv_w, f_conv_b, f_w_down, ple_w_in, ple_w_gate, ple_b_gate, loss_target, m_norm_mix, m_norm_ffn, m_norm_ple, m_norm_kv, m_norm_final, m_a_w_in, m_a_norm_v, m_a_w_s, m_a_b_s, m_a_w_out, m_w_kv, m_b_w_q, m_b_sinks, m_b_w_o, m_f_w_up, m_f_conv_w, m_f_conv_b, m_f_w_down, m_ple_w_in, m_ple_w_gate, m_ple_b_gate, v_norm_mix, v_norm_ffn, v_norm_ple, v_norm_kv, v_norm_final, v_a_w_in, v_a_norm_v, v_a_w_s, v_a_b_s, v_a_w_out, v_w_kv, v_b_w_q, v_b_sinks, v_b_w_o, v_f_w_up, v_f_conv_w, v_f_conv_b, v_f_w_down, v_ple_w_in, v_ple_w_gate, v_ple_b_gate):
    w = dict(norm_mix=norm_mix, norm_ffn=norm_ffn, norm_ple=norm_ple, norm_kv=norm_kv, norm_final=norm_final,
             a_w_in=a_w_in, a_norm_v=a_norm_v, a_w_s=a_w_s, a_b_s=a_b_s, a_w_out=a_w_out, w_kv=w_kv, b_w_q=b_w_q,
             b_sinks=b_sinks, b_w_o=b_w_o, f_w_up=f_w_up, f_conv_w=f_conv_w, f_conv_b=f_conv_b, f_w_down=f_w_down,
             ple_w_in=ple_w_in, ple_w_gate=ple_w_gate, ple_b_gate=ple_b_gate)
    mom_m = dict(norm_mix=m_norm_mix, norm_ffn=m_norm_ffn, norm_ple=m_norm_ple, norm_kv=m_norm_kv,
                 norm_final=m_norm_final, a_w_in=m_a_w_in, a_norm_v=m_a_norm_v, a_w_s=m_a_w_s, a_b_s=m_a_b_s,
                 a_w_out=m_a_w_out, w_kv=m_w_kv, b_w_q=m_b_w_q, b_sinks=m_b_sinks, b_w_o=m_b_w_o, f_w_up=m_f_w_up,
                 f_conv_w=m_f_conv_w, f_conv_b=m_f_conv_b, f_w_down=m_f_w_down, ple_w_in=m_ple_w_in,
                 ple_w_gate=m_ple_w_gate, ple_b_gate=m_ple_b_gate)
    mom_v = dict(norm_mix=v_norm_mix, norm_ffn=v_norm_ffn, norm_ple=v_norm_ple, norm_kv=v_norm_kv,
                 norm_final=v_norm_final, a_w_in=v_a_w_in, a_norm_v=v_a_norm_v, a_w_s=v_a_w_s, a_b_s=v_a_b_s,
                 a_w_out=v_a_w_out, w_kv=v_w_kv, b_w_q=v_b_w_q, b_sinks=v_b_sinks, b_w_o=v_b_w_o, f_w_up=v_f_w_up,
                 f_conv_w=v_f_conv_w, f_conv_b=v_f_conv_b, f_w_down=v_f_w_down, ple_w_in=v_ple_w_in,
                 ple_w_gate=v_ple_w_gate, ple_b_gate=v_ple_b_gate)
    S = x.shape[1]
    tm = min(256, S)
    me = 4 * lax.axis_index("x") + 2 * lax.axis_index("y") + lax.axis_index("c")

    big_shapes = [w[n].shape for n in _BIG]
    small_shapes = [w[n].shape for n in _SMALL_SHARDED]
    wbig = _unpack(_all_gather(_pack([w[n].astype(BF) for n in _BIG], 16), "gather_weights"),
                   big_shapes, 16, (N_DEV,))
    wsmall = _unpack(_all_gather(_pack([w[n] for n in _SMALL_SHARDED], 8), "gather_small"),
                     small_shapes, 8, (N_DEV,))
    gw = dict(zip(_BIG, wbig))
    gw.update(zip(_SMALL_SHARDED, wsmall))
    win = gw['a_w_in'][:, 0].transpose(1, 0, 2).reshape(D, 2 * AW)
    gv = gw['a_norm_v'].reshape(1, AW)
    wout = gw['a_w_out'].reshape(AW, D)
    wkv = gw['w_kv'].reshape(D, 2 * N_KV * HD)
    wq = gw['b_w_q'].reshape(D, D)
    wo = gw['b_w_o'].reshape(D, D)
    wup = [gw['f_w_up'][:, l] for l in range(2)]
    cw = [gw['f_conv_w'][:, l] for l in range(2)]
    cb = [w['f_conv_b'][l].reshape(N_DEV, 1, FF_BLK) for l in range(2)]
    wdown = [gw['f_w_down'][:, l].reshape(D_FF, D) for l in range(2)]
    wple = [gw['ple_w_in'][:, l].transpose(1, 0, 2).reshape(PLE, D) for l in range(2)]
    wgate = [gw['ple_w_gate'][:, l].reshape(D, D) for l in range(2)]
    ws = a_w_s[0].astype(BF)
    wst = jnp.swapaxes(ws, 1, 2)
    bsb = jnp.broadcast_to(a_b_s[0][:, :, None], (GROUPS, CHUNK, CHUNK))
    row = lambda a: a.reshape(1, -1)
    pb = [p[l, 0].astype(BF) for l in range(2)]
    sinks = b_sinks.reshape(N_Q)
    rep = _replication()
    x0 = x[0]

    h1, z = _mixa_fwd(x0, row(norm_mix[0]), win, gv, ws, bsb, wout, tm)
    h2, hh0 = _ffn_fwd(h1, row(norm_ffn[0]), wup[0], cw[0], cb[0], wdown[0], tm)
    h3 = _ple_fwd(h2, row(norm_ple[0]), wgate[0], row(ple_b_gate[0]), pb[0], wple[0], tm)
    q, kv = _qkv_fwd(h3, row(norm_mix[1]), wq, row(norm_kv), wkv, tm)
    att = _attn_fwd(q, kv, sinks, rep)
    h4 = _oproj_fwd(att, wo, h3, tm)
    h5, hh1 = _ffn_fwd(h4, row(norm_ffn[1]), wup[1], cw[1], cb[1], wdown[1], tm)
    h6 = _ple_fwd(h5, row(norm_ple[1]), wgate[1], row(ple_b_gate[1]), pb[1], wple[1], tm)

    dh6, loss_blk, d_norm_final = _head(h6, row(norm_final), loss_target[0], tm)
    dh5, de1, xp1, dpre1, dbg1, dgp1 = _ple_bwd(dh6, h5, row(norm_ple[1]), wgate[1], row(ple_b_gate[1]), pb[1],
                                                 wple[1], tm)
    dh4, xf1, dhh1, act1, dhb5, dcw1, dcb1, dgf1 = _ffn_bwd(dh5, h4, hh1, row(norm_ffn[1]), wup[1], cw[1], cb[1],
                                                            wdown[1], tm)
    datt, dhb4 = _oproj_bwd(dh4, wo, tm)
    dq, dkv, dsink = _attn_bwd(q, kv, datt, sinks, rep)
    dh3, xq, xkv, dgq, dgkv = _qkv_bwd(dh4, h3, dq, dkv, row(norm_mix[1]), wq, row(norm_kv), wkv, tm)
    dh2, de0, xp0, dpre0, dbg0, dgp0 = _ple_bwd(dh3, h2, row(norm_ple[0]), wgate[0], row(ple_b_gate[0]), pb[0],
                                                 wple[0], tm)
    dh1, xf0, dhh0, act0, dhb2, dcw0, dcb0, dgf0 = _ffn_bwd(dh2, h1, hh0, row(norm_ffn[0]), wup[0], cw[0], cb[0],
                                                            wdown[0], tm)
    dx, xa, dz, gated, dhb1, dws, dbs, dgv, dga = _mixa_bwd(dh1, x0, z, row(norm_mix[0]), win, gv, ws, wst, bsb,
                                                            wout, tm)

    def col_blocks(g2d, n):
        return g2d.reshape(g2d.shape[0], N_DEV, n).transpose(1, 0, 2)

    def row_blocks(g2d):
        return g2d.reshape(N_DEV, g2d.shape[0] // N_DEV, g2d.shape[1])

    def layers(a, b):
        return jnp.stack([a, b], axis=1)

    pg = {
        'a_w_in': _wgrad(xa, dz, "wgrad_a_in", tn=2 * AW // N_DEV, out_blocked=True)[:, None],
        'a_w_out': row_blocks(_wgrad(gated, dhb1, "wgrad_a_out", tn=512))[:, None],
        'w_kv': row_blocks(_wgrad(xkv, dkv, "wgrad_kv", tn=512)),
        'b_w_q': row_blocks(_wgrad(xq, dq, "wgrad_q", tn=512))[:, None],
        'b_w_o': row_blocks(_wgrad(att, dhb4, "wgrad_o", tn=512))[:, None],
        'f_w_up': layers(_wgrad(xf0, dhh0, "wgrad_up0", tn=FF_BLK, out_blocked=True),
                         _wgrad(xf1, dhh1, "wgrad_up1", tn=FF_BLK, out_blocked=True)),
        'f_w_down': layers(row_blocks(_wgrad(act0, dhb2, "wgrad_down0", tn=512)),
                           row_blocks(_wgrad(act1, dhb5, "wgrad_down1", tn=512))),
        'ple_w_in': layers(col_blocks(_wgrad(pb[0], de0, "wgrad_ple_in0", tn=512), D // N_DEV),
                           col_blocks(_wgrad(pb[1], de1, "wgrad_ple_in1", tn=512), D // N_DEV)),
        'ple_w_gate': layers(row_blocks(_wgrad(xp0, dpre0, "wgrad_gate0", tn=512)),
                             row_blocks(_wgrad(xp1, dpre1, "wgrad_gate1", tn=512))),
    }
    parts = _all_to_all(_pack([pg[n] for n in _BIG], 16, (N_DEV,)), "exchange_grads")

    packed = lambda d: _pack([d[n] for n in _BIG], 16)
    outs = _adamw(parts, packed(w), packed(mom_m), packed(mom_v), "adamw_big", 224)
    g_big, d_big, m_big, v_big = [dict(zip(_BIG, _unpack(o, big_shapes, 16))) for o in outs]

    small_g = {
        'norm_mix': jnp.concatenate([dga, dgq], axis=0), 'norm_ffn': jnp.concatenate([dgf0, dgf1], axis=0),
        'norm_ple': jnp.concatenate([dgp0, dgp1], axis=0), 'norm_kv': dgkv.reshape(D),
        'norm_final': d_norm_final.reshape(D), 'a_w_s': dws[None], 'a_b_s': dbs[None, :, :, 0],
        'b_sinks': dsink[:, 0].reshape(1, N_Q),
        'f_conv_b': jnp.stack([dcb0.reshape(2 * D_FF), dcb1.reshape(2 * D_FF)]),
        'ple_b_gate': jnp.concatenate([dbg0, dbg1], axis=0),
        'a_norm_v': dgv, 'f_conv_w': jnp.stack([dcw0, dcw1], axis=1),
    }
    names_all = _REPL + _SMALL_SHARDED
    full_shapes = [small_g[n].shape for n in names_all]
    gsum = _sum8(_all_gather(_pack([small_g[n] for n in names_all], 8), "gather_small_grads"), "sum_small_grads")
    gs = dict(zip(names_all, _unpack(gsum, full_shapes, 8)))
    gs['a_norm_v'] = lax.dynamic_slice_in_dim(gs['a_norm_v'], me * (AW // N_DEV), AW // N_DEV, axis=1)
    gs['f_conv_w'] = lax.dynamic_index_in_dim(gs['f_conv_w'], me, axis=0, keepdims=False)
    gs = {n: gs[n].reshape(w[n].shape) for n in names_all}
    packs = lambda d: _pack([d[n] for n in names_all], 8)
    souts = _adamw_small(packs(gs), packs(w), packs(mom_m), packs(mom_v), "adamw_small")
    shp = [w[n].shape for n in names_all]
    d_s, m_s, v_s = [dict(zip(names_all, _unpack(o, shp, 8))) for o in souts]

    grads = {**g_big, **gs}
    delta = {**d_big, **d_s}
    new_m = {**m_big, **m_s}
    new_v = {**v_big, **v_s}
    loss = lax.psum(loss_blk[0, 0], ("x", "y", "c"))
    return (loss, dx[None], *[grads[n] for n in _PARAMS], *[delta[n] for n in _PARAMS],
            *[new_m[n] for n in _PARAMS], *[new_v[n] for n in _PARAMS])
```

```python
import functools

import numpy as np
import jax
import jax.numpy as jnp
from jax import lax
from jax.experimental import pallas as pl
from jax.experimental.pallas import tpu as pltpu

F32 = jnp.float32
BF = jnp.bfloat16

N_DEV = 8
D = 1024
AW = 1024
GROUPS = 8
CHUNK = 128
HD = 64
N_Q = 16
N_KV = 4
GQA = 4
BLOCK = 128
D_FF = 2816
FF_BLK = 704
N_FF_BLK = 4
PLE = 256
EPS = 1e-6
LR, B1, B2, ADAM_EPS, WD, STEP = 0.001, 0.9, 0.999, 1e-08, 0.01, 10
GELU_C0 = 0.7978845608028654
GELU_C1 = 0.044715
NEG = -1e30
ROW = 1024
VMEM_BIG = 56 * 1024 * 1024

_PARAMS = ['norm_mix', 'norm_ffn', 'norm_ple', 'norm_kv', 'norm_final', 'a_w_in', 'a_norm_v', 'a_w_s', 'a_b_s',
           'a_w_out', 'w_kv', 'b_w_q', 'b_sinks', 'b_w_o', 'f_w_up', 'f_conv_w', 'f_conv_b', 'f_w_down',
           'ple_w_in', 'ple_w_gate', 'ple_b_gate']
_BIG = ['a_w_in', 'a_w_out', 'w_kv', 'b_w_q', 'b_w_o', 'f_w_up', 'f_w_down', 'ple_w_in', 'ple_w_gate']
_SMALL_SHARDED = ['a_norm_v', 'f_conv_w']
_REPL = ['norm_mix', 'norm_ffn', 'norm_ple', 'norm_kv', 'norm_final', 'a_w_s', 'a_b_s', 'b_sinks', 'f_conv_b',
         'ple_b_gate']


def _mm(a, b):
    return jnp.dot(a, b, preferred_element_type=F32)


def _mm_nt(a, b):
    return lax.dot_general(a, b, (((1,), (1,)), ((), ())), preferred_element_type=F32)


def _mm_tn(a, b):
    return lax.dot_general(a, b, (((0,), (0,)), ((), ())), preferred_element_type=F32)


def _const_spec(shape, single=False):
    nd = len(shape)
    if single:
        return pl.BlockSpec(shape, lambda *_: (0,) * nd, pipeline_mode=pl.Buffered(1))
    return pl.BlockSpec(shape, lambda *_: (0,) * nd)


def _row_spec(tm, n):
    return pl.BlockSpec((tm, n), lambda i: (i, 0))


def _rms_fwd(x, g):
    r = lax.rsqrt(jnp.mean(x * x, axis=-1, keepdims=True) + EPS)
    xhat = x * r
    return xhat * g, xhat, r


def _rms_bwd(dy, xhat, r, g):
    a = dy * g
    dx = r * (a - xhat * jnp.mean(a * xhat, axis=-1, keepdims=True))
    return dx, jnp.sum(dy * xhat, axis=0, keepdims=True)


def _sigmoid(x):
    return 1.0 / (1.0 + jnp.exp(-x))


def _params(vmem=None):
    return pltpu.CompilerParams(dimension_semantics=("arbitrary",), vmem_limit_bytes=vmem)


def _tri_masks():
    t = lax.broadcasted_iota(jnp.int32, (CHUNK, CHUNK), 0)
    s = lax.broadcasted_iota(jnp.int32, (CHUNK, CHUNK), 1)
    return t >= s, s >= t


def _mixa_fwd(x, g, win, gv, ws, bsb, wout, tm):
    S = x.shape[0]
    nc = tm // CHUNK

    def body(x_ref, g_ref, win_ref, gv_ref, ws_ref, bsb_ref, wout_ref, h_ref, z_ref, gated):
        x = x_ref[...]
        xn, _, _ = _rms_fwd(x, g_ref[...])
        z = _mm(xn.astype(BF), win_ref[...])
        z_ref[...] = z.astype(BF)
        zg = 0.5 * z * (1.0 + jnp.tanh(GELU_C0 * (z + GELU_C1 * z * z * z)))
        u = zg[:, :AW]
        vn, _, _ = _rms_fwd(zg[:, AW:], gv_ref[...])
        vn = vn.astype(BF)
        tril, _ = _tri_masks()
        for h in range(GROUPS):
            wm = jnp.where(tril, ws_ref[h], jnp.zeros((), BF))
            for c in range(nc):
                rs, cs = slice(c * CHUNK, (c + 1) * CHUNK), slice(h * CHUNK, (h + 1) * CHUNK)
                s = _mm(wm, vn[rs, cs]) + bsb_ref[h]
                gated[rs, cs] = (u[rs, cs] * s).astype(BF)
        h_ref[...] = x + _mm(gated[...], wout_ref[...])

    return pl.pallas_call(
        body, name="mixa_fwd", grid=(S // tm,),
        in_specs=[_row_spec(tm, D), _const_spec((1, D)), _const_spec((D, 2 * AW)), _const_spec((1, AW)),
                  _const_spec((GROUPS, CHUNK, CHUNK)), _const_spec((GROUPS, CHUNK, CHUNK)), _const_spec((AW, D))],
        out_specs=[_row_spec(tm, D), _row_spec(tm, 2 * AW)],
        out_shape=[jax.ShapeDtypeStruct((S, D), F32), jax.ShapeDtypeStruct((S, 2 * AW), BF)],
        scratch_shapes=[pltpu.VMEM((tm, AW), BF)],
        compiler_params=_params(VMEM_BIG),
    )(x, g, win, gv, ws, bsb, wout)


def _mixa_bwd(dh, x, z, g, win, gv, ws, wst, bsb, wout, tm):
    S = x.shape[0]
    nt = S // tm
    nc = tm // CHUNK

    def body(dh_ref, x_ref, z_ref, g_ref, win_ref, gv_ref, ws_ref, wst_ref, bsb_ref, wout_ref,
             dx_ref, xn_ref, dz_ref, gated_ref, dhb_ref, dws_ref, dbs_ref, dgv_ref, dg_ref, du_scr, dvn_scr):
        i = pl.program_id(0)

        @pl.when(i == 0)
        def _():
            dws_ref[...] = jnp.zeros_like(dws_ref)
            dbs_ref[...] = jnp.zeros_like(dbs_ref)
            dgv_ref[...] = jnp.zeros_like(dgv_ref)
            dg_ref[...] = jnp.zeros_like(dg_ref)

        x = x_ref[...]
        gmix = g_ref[...]
        xn, xhat, r = _rms_fwd(x, gmix)
        xn_ref[...] = xn.T.astype(BF)
        z = z_ref[...].astype(F32)
        z2 = z * z
        t = jnp.tanh(GELU_C0 * (z + GELU_C1 * z2 * z))
        zg = 0.5 * z * (1.0 + t)
        gp = 0.5 * (1.0 + t) + 0.5 * z * (1.0 - t * t) * (GELU_C0 * (1.0 + 3.0 * GELU_C1 * z2))
        u = zg[:, :AW]
        gvv = gv_ref[...]
        vn, vhat, rv = _rms_fwd(zg[:, AW:], gvv)
        vn = vn.astype(BF)
        dh = dh_ref[...]
        dhb = dh.astype(BF)
        dhb_ref[...] = dhb
        dgated = _mm_nt(dhb, wout_ref[...])
        tril, triu = _tri_masks()
        for h in range(GROUPS):
            wm = jnp.where(tril, ws_ref[h], jnp.zeros((), BF))
            wmt = jnp.where(triu, wst_ref[h], jnp.zeros((), BF))
            for c in range(nc):
                rs, cs = slice(c * CHUNK, (c + 1) * CHUNK), slice(h * CHUNK, (h + 1) * CHUNK)
                vnb = vn[rs, cs]
                s = _mm(wm, vnb) + bsb_ref[h]
                gated_ref[rs, cs] = (u[rs, cs] * s).astype(BF)
                dgt = dgated[rs, cs]
                du_scr[rs, cs] = dgt * s
                ds = dgt * u[rs, cs]
                dsb = ds.astype(BF)
                dws_ref[h] += jnp.where(tril, _mm_nt(dsb, vnb), 0.0)
                dbs_ref[h] += ds
                dvn_scr[rs, cs] = _mm(wmt, dsb)
        dvn = dvn_scr[...]
        dv, dgv = _rms_bwd(dvn, vhat, rv, gvv)
        dgv_ref[...] += dgv
        dz_ref[:, :AW] = (du_scr[...] * gp[:, :AW]).astype(BF)
        dz_ref[:, AW:] = (dv * gp[:, AW:]).astype(BF)
        dxn = _mm_nt(dz_ref[...], win_ref[...])
        dxr, dg = _rms_bwd(dxn, xhat, r, gmix)
        dg_ref[...] += dg
        dx_ref[...] = dh + dxr

        @pl.when(i == nt - 1)
        def _():
            for h in range(GROUPS):
                dbs_ref[h] = jnp.broadcast_to(jnp.sum(dbs_ref[h], axis=-1, keepdims=True), (CHUNK, CHUNK))

    gshape = (GROUPS, CHUNK, CHUNK)
    return pl.pallas_call(
        body, name="mixa_bwd", grid=(nt,),
        in_specs=[_row_spec(tm, D), _row_spec(tm, D), _row_spec(tm, 2 * AW), _const_spec((1, D)),
                  _const_spec((D, 2 * AW)), _const_spec((1, AW)), _const_spec(gshape), _const_spec(gshape),
                  _const_spec(gshape), _const_spec((AW, D))],
        out_specs=[_row_spec(tm, D), pl.BlockSpec((D, tm), lambda i: (0, i)), _row_spec(tm, 2 * AW), _row_spec(tm, AW),
                   _row_spec(tm, D), _const_spec(gshape), _const_spec(gshape), _const_spec((1, AW)), _const_spec((1, D))],
        out_shape=[jax.ShapeDtypeStruct((S, D), F32), jax.ShapeDtypeStruct((D, S), BF),
                   jax.ShapeDtypeStruct((S, 2 * AW), BF), jax.ShapeDtypeStruct((S, AW), BF),
                   jax.ShapeDtypeStruct((S, D), BF), jax.ShapeDtypeStruct(gshape, F32),
                   jax.ShapeDtypeStruct(gshape, F32), jax.ShapeDtypeStruct((1, AW), F32),
                   jax.ShapeDtypeStruct((1, D), F32)],
        scratch_shapes=[pltpu.VMEM((tm, AW), F32), pltpu.VMEM((tm, AW), F32)],
        compiler_params=_params(VMEM_BIG),
    )(dh, x, z, g, win, gv, ws, wst, bsb, wout)


def _conv_fwd(h, prev8, w, b, rows):
    hm1 = jnp.where(rows == 0, prev8[7:8], pltpu.roll(h, 1, 0))
    hm2 = jnp.where(rows == 0, prev8[6:7], jnp.where(rows == 1, prev8[7:8], pltpu.roll(h, 2, 0)))
    return b + w[0:1] * hm2 + w[1:2] * hm1 + w[2:3] * h, hm1, hm2


def _ffn_fwd(h, g, wup, cw, cb, wdown, tm):
    S = h.shape[0]

    def body(h_ref, g_ref, wup_ref, cw_ref, cb_ref, wdown_ref, o_ref, hh_ref, carry):
        i = pl.program_id(0)

        @pl.when(i == 0)
        def _():
            carry[...] = jnp.zeros_like(carry)

        x = h_ref[...]
        xn, _, _ = _rms_fwd(x, g_ref[...])
        xn = xn.astype(BF)
        rows = lax.broadcasted_iota(jnp.int32, (tm, 1), 0)
        acc = x
        for d in range(N_FF_BLK):
            cs = []
            for blk in (d, N_FF_BLK + d):
                hb = _mm(xn, wup_ref[blk])
                hh_ref[blk] = hb.astype(BF)
                c, _, _ = _conv_fwd(hb, carry[blk], cw_ref[blk], cb_ref[blk], rows)
                carry[blk] = hb[tm - 8:, :]
                cs.append(c)
            act = cs[0] * _sigmoid(cs[0]) * cs[1]
            acc = acc + _mm(act.astype(BF), wdown_ref[d * FF_BLK:(d + 1) * FF_BLK, :])
        o_ref[...] = acc

    return pl.pallas_call(
        body, name="ffn_fwd", grid=(S // tm,),
        in_specs=[_row_spec(tm, D), _const_spec((1, D)), _const_spec((N_DEV, D, FF_BLK), True),
                  _const_spec((N_DEV, 3, FF_BLK)), _const_spec((N_DEV, 1, FF_BLK)), _const_spec((D_FF, D), True)],
        out_specs=[_row_spec(tm, D), pl.BlockSpec((N_DEV, tm, FF_BLK), lambda i: (0, i, 0))],
        out_shape=[jax.ShapeDtypeStruct((S, D), F32), jax.ShapeDtypeStruct((N_DEV, S, FF_BLK), BF)],
        scratch_shapes=[pltpu.VMEM((N_DEV, 8, FF_BLK), F32)],
        compiler_params=_params(VMEM_BIG),
    )(h, g, wup, cw, cb, wdown)


def _ffn_bwd(dh, h, hh, g, wup, cw, cb, wdown, tm):
    S = h.shape[0]
    nt = S // tm
    halo = 16
    per = tm // halo

    def rev(i):
        return (nt - 1 - i, 0)

    def body(dh_ref, h_ref, hh_ref, halo_ref, g_ref, wup_ref, cw_ref, cb_ref, wdown_ref,
             dx_ref, xn_ref, dhh_ref, act_ref, dhb_ref, dcw_ref, dcb_ref, dg_ref, carry):
        i = pl.program_id(0)

        @pl.when(i == 0)
        def _():
            carry[...] = jnp.zeros_like(carry)
            dcw_ref[...] = jnp.zeros_like(dcw_ref)
            dcb_ref[...] = jnp.zeros_like(dcb_ref)
            dg_ref[...] = jnp.zeros_like(dg_ref)

        first_tile = i == nt - 1
        x = h_ref[...]
        gain = g_ref[...]
        xn, xhat, r = _rms_fwd(x, gain)
        xn_ref[...] = xn.T.astype(BF)
        dh = dh_ref[...]
        dhb = dh.astype(BF)
        dhb_ref[...] = dhb
        rows = lax.broadcasted_iota(jnp.int32, (tm, 1), 0)
        dxn = jnp.zeros((tm, D), F32)
        for d in range(N_FF_BLK):
            blks = (d, N_FF_BLK + d)
            hs, cs, hm1s, hm2s = [], [], [], []
            for blk in blks:
                hb = hh_ref[blk].astype(F32)
                prev8 = jnp.where(first_tile, 0.0, halo_ref[blk][halo - 8:, :].astype(F32))
                c, hm1, hm2 = _conv_fwd(hb, prev8, cw_ref[blk], cb_ref[blk], rows)
                hs.append(hb), cs.append(c), hm1s.append(hm1), hm2s.append(hm2)
            cg, cu = cs
            sg = _sigmoid(cg)
            sil = cg * sg
            act_ref[d] = (sil * cu).astype(BF)
            dact = _mm_nt(dhb, wdown_ref[d * FF_BLK:(d + 1) * FF_BLK, :])
            dcs = (dact * cu * (sg * (1.0 + cg * (1.0 - sg))), dact * sil)
            for blk, dc, hb, hm1, hm2 in zip(blks, dcs, hs, hm1s, hm2s):
                dcb_ref[blk] += jnp.sum(dc, axis=0, keepdims=True)
                dcw_ref[blk, 0:1, :] += jnp.sum(dc * hm2, axis=0, keepdims=True)
                dcw_ref[blk, 1:2, :] += jnp.sum(dc * hm1, axis=0, keepdims=True)
                dcw_ref[blk, 2:3, :] += jnp.sum(dc * hb, axis=0, keepdims=True)
                nxt = carry[blk]
                dp1 = jnp.where(rows == tm - 1, nxt[0:1], pltpu.roll(dc, tm - 1, 0))
                dp2 = jnp.where(rows == tm - 1, nxt[1:2], jnp.where(rows == tm - 2, nxt[0:1], pltpu.roll(dc, tm - 2, 0)))
                w = cw_ref[blk]
                dhh = (w[2:3] * dc + w[1:2] * dp1 + w[0:1] * dp2).astype(BF)
                carry[blk] = dc[0:8, :]
                dhh_ref[blk] = dhh
                dxn = dxn + _mm_nt(dhh, wup_ref[blk])
        dxr, dg = _rms_bwd(dxn, xhat, r, gain)
        dg_ref[...] += dg
        dx_ref[...] = dh + dxr

    blk3 = lambda i: (0, nt - 1 - i, 0)
    return pl.pallas_call(
        body, name="ffn_bwd", grid=(nt,),
        in_specs=[pl.BlockSpec((tm, D), rev), pl.BlockSpec((tm, D), rev),
                  pl.BlockSpec((N_DEV, tm, FF_BLK), blk3),
                  pl.BlockSpec((N_DEV, halo, FF_BLK), lambda i: (0, jnp.maximum((nt - 1 - i) * per - 1, 0), 0)),
                  _const_spec((1, D)), _const_spec((N_DEV, D, FF_BLK), True), _const_spec((N_DEV, 3, FF_BLK)),
                  _const_spec((N_DEV, 1, FF_BLK)), _const_spec((D_FF, D), True)],
        out_specs=[pl.BlockSpec((tm, D), rev), pl.BlockSpec((D, tm), lambda i: (0, nt - 1 - i)),
                   pl.BlockSpec((N_DEV, tm, FF_BLK), blk3), pl.BlockSpec((N_FF_BLK, tm, FF_BLK), blk3),
                   pl.BlockSpec((tm, D), rev), _const_spec((N_DEV, 3, FF_BLK)), _const_spec((N_DEV, 1, FF_BLK)),
                   _const_spec((1, D))],
        out_shape=[jax.ShapeDtypeStruct((S, D), F32), jax.ShapeDtypeStruct((D, S), BF),
                   jax.ShapeDtypeStruct((N_DEV, S, FF_BLK), BF), jax.ShapeDtypeStruct((N_FF_BLK, S, FF_BLK), BF),
                   jax.ShapeDtypeStruct((S, D), BF), jax.ShapeDtypeStruct((N_DEV, 3, FF_BLK), F32),
                   jax.ShapeDtypeStruct((N_DEV, 1, FF_BLK), F32), jax.ShapeDtypeStruct((1, D), F32)],
        scratch_shapes=[pltpu.VMEM((N_DEV, 8, FF_BLK), F32)],
        compiler_params=_params(VMEM_BIG),
    )(dh, h, hh, hh, g, wup, cw, cb, wdown)


def _ple_fwd(h, g, wgate, bgate, p, wple, tm):
    S = h.shape[0]

    def body(h_ref, g_ref, wg_ref, bg_ref, p_ref, wp_ref, o_ref):
        x = h_ref[...]
        xn, _, _ = _rms_fwd(x, g_ref[...])
        gate = _sigmoid(_mm(xn.astype(BF), wg_ref[...]) + bg_ref[...])
        o_ref[...] = x + _mm(p_ref[...], wp_ref[...]) * gate

    return pl.pallas_call(
        body, name="ple_fwd", grid=(S // tm,),
        in_specs=[_row_spec(tm, D), _const_spec((1, D)), _const_spec((D, D)), _const_spec((1, D)),
                  _row_spec(tm, PLE), _const_spec((PLE, D))],
        out_specs=_row_spec(tm, D), out_shape=jax.ShapeDtypeStruct((S, D), F32),
        compiler_params=_params(),
    )(h, g, wgate, bgate, p, wple)


def _ple_bwd(dh, h, g, wgate, bgate, p, wple, tm):
    S = h.shape[0]

    def body(dh_ref, h_ref, g_ref, wg_ref, bg_ref, p_ref, wp_ref, dx_ref, de_ref, xn_ref, dpre_ref, dbg_ref, dg_ref):
        i = pl.program_id(0)

        @pl.when(i == 0)
        def _():
            dbg_ref[...] = jnp.zeros_like(dbg_ref)
            dg_ref[...] = jnp.zeros_like(dg_ref)

        x = h_ref[...]
        gain = g_ref[...]
        xn, xhat, r = _rms_fwd(x, gain)
        xnb = xn.astype(BF)
        xn_ref[...] = xnb
        gate = _sigmoid(_mm(xnb, wg_ref[...]) + bg_ref[...])
        e = _mm(p_ref[...], wp_ref[...])
        dh = dh_ref[...]
        de_ref[...] = (dh * gate).astype(BF)
        dpre = dh * e * gate * (1.0 - gate)
        dpreb = dpre.astype(BF)
        dpre_ref[...] = dpreb
        dbg_ref[...] += jnp.sum(dpre, axis=0, keepdims=True)
        dxr, dg = _rms_bwd(_mm_nt(dpreb, wg_ref[...]), xhat, r, gain)
        dg_ref[...] += dg
        dx_ref[...] = dh + dxr

    return pl.pallas_call(
        body, name="ple_bwd", grid=(S // tm,),
        in_specs=[_row_spec(tm, D), _row_spec(tm, D), _const_spec((1, D)), _const_spec((D, D)), _const_spec((1, D)),
                  _row_spec(tm, PLE), _const_spec((PLE, D))],
        out_specs=[_row_spec(tm, D), _row_spec(tm, D), _row_spec(tm, D), _row_spec(tm, D), _const_spec((1, D)),
                   _const_spec((1, D))],
        out_shape=[jax.ShapeDtypeStruct((S, D), F32), jax.ShapeDtypeStruct((S, D), BF),
                   jax.ShapeDtypeStruct((S, D), BF), jax.ShapeDtypeStruct((S, D), BF),
                   jax.ShapeDtypeStruct((1, D), F32), jax.ShapeDtypeStruct((1, D), F32)],
        compiler_params=_params(),
    )(dh, h, g, wgate, bgate, p, wple)


def _qkv_fwd(h, gq, wq, gkv, wkv, tm):
    S = h.shape[0]
    nkv = wkv.shape[1]

    def body(h_ref, gq_ref, wq_ref, gkv_ref, wkv_ref, q_ref, kv_ref):
        x = h_ref[...]
        _, xhat, _ = _rms_fwd(x, gq_ref[...])
        q_ref[...] = _mm((xhat * gq_ref[...]).astype(BF), wq_ref[...]).astype(BF)
        kv_ref[...] = _mm((xhat * gkv_ref[...]).astype(BF), wkv_ref[...]).astype(BF)

    return pl.pallas_call(
        body, name="qkv_fwd", grid=(S // tm,),
        in_specs=[_row_spec(tm, D), _const_spec((1, D)), _const_spec((D, D)), _const_spec((1, D)),
                  _const_spec((D, nkv))],
        out_specs=[_row_spec(tm, D), _row_spec(tm, nkv)],
        out_shape=[jax.ShapeDtypeStruct((S, D), BF), jax.ShapeDtypeStruct((S, nkv), BF)],
        compiler_params=_params(),
    )(h, gq, wq, gkv, wkv)


def _qkv_bwd(dh, h, dq, dkv, gq, wq, gkv, wkv, tm):
    S = h.shape[0]
    nkv = wkv.shape[1]

    def body(dh_ref, h_ref, dq_ref, dkv_ref, gq_ref, wq_ref, gkv_ref, wkv_ref,
             dx_ref, xq_ref, xkv_ref, dgq_ref, dgkv_ref):
        i = pl.program_id(0)

        @pl.when(i == 0)
        def _():
            dgq_ref[...] = jnp.zeros_like(dgq_ref)
            dgkv_ref[...] = jnp.zeros_like(dgkv_ref)

        x = h_ref[...]
        gq_, gkv_ = gq_ref[...], gkv_ref[...]
        _, xhat, r = _rms_fwd(x, gq_)
        xq_ref[...] = (xhat * gq_).astype(BF)
        xkv_ref[...] = (xhat * gkv_).astype(BF)
        d1, dg1 = _rms_bwd(_mm_nt(dq_ref[...], wq_ref[...]), xhat, r, gq_)
        d2, dg2 = _rms_bwd(_mm_nt(dkv_ref[...], wkv_ref[...]), xhat, r, gkv_)
        dgq_ref[...] += dg1
        dgkv_ref[...] += dg2
        dx_ref[...] = dh_ref[...] + d1 + d2

    return pl.pallas_call(
        body, name="qkv_bwd", grid=(S // tm,),
        in_specs=[_row_spec(tm, D), _row_spec(tm, D), _row_spec(tm, D), _row_spec(tm, nkv), _const_spec((1, D)),
                  _const_spec((D, D)), _const_spec((1, D)), _const_spec((D, nkv))],
        out_specs=[_row_spec(tm, D), _row_spec(tm, D), _row_spec(tm, D), _const_spec((1, D)), _const_spec((1, D))],
        out_shape=[jax.ShapeDtypeStruct((S, D), F32), jax.ShapeDtypeStruct((S, D), BF),
                   jax.ShapeDtypeStruct((S, D), BF), jax.ShapeDtypeStruct((1, D), F32),
                   jax.ShapeDtypeStruct((1, D), F32)],
        compiler_params=_params(),
    )(dh, h, dq, dkv, gq, wq, gkv, wkv)


def _oproj_fwd(a, w, res, tm):
    S = a.shape[0]

    def body(a_ref, w_ref, r_ref, o_ref):
        o_ref[...] = r_ref[...] + _mm(a_ref[...], w_ref[...])

    return pl.pallas_call(
        body, name="oproj_fwd", grid=(S // tm,),
        in_specs=[_row_spec(tm, D), _const_spec((D, D)), _row_spec(tm, D)],
        out_specs=_row_spec(tm, D), out_shape=jax.ShapeDtypeStruct((S, D), F32),
        compiler_params=_params(),
    )(a, w, res)


def _oproj_bwd(dh, w, tm):
    S = dh.shape[0]

    def body(dh_ref, w_ref, da_ref, dhb_ref):
        dhb = dh_ref[...].astype(BF)
        dhb_ref[...] = dhb
        da_ref[...] = _mm_nt(dhb, w_ref[...]).astype(BF)

    return pl.pallas_call(
        body, name="oproj_bwd", grid=(S // tm,),
        in_specs=[_row_spec(tm, D), _const_spec((D, D))],
        out_specs=[_row_spec(tm, D), _row_spec(tm, D)],
        out_shape=[jax.ShapeDtypeStruct((S, D), BF), jax.ShapeDtypeStruct((S, D), BF)],
        compiler_params=_params(),
    )(dh, w)


def _alibi_slope(hq):
    return float(np.float32(2.0 ** (-8.0 * (hq + 1) / N_Q)))


def _attn_consts():
    W = N_KV * HD
    r = np.zeros((W, N_KV * W), np.float32)
    for kh in range(N_KV):
        for c in range(W):
            r[kh * HD + c % HD, kh * W + c] = 1.0
    qi = np.arange(BLOCK)[:, None]
    kj = np.arange(2 * BLOCK)[None, :]
    dist = qi + BLOCK - kj
    band = (dist >= 0) & (dist < BLOCK)
    bias = np.full((2, N_KV, GQA * BLOCK, 2 * BLOCK), NEG, np.float32)
    for later in (0, 1):
        valid = band & ((kj >= BLOCK) | bool(later))
        for kh in range(N_KV):
            for g in range(GQA):
                slope = np.float32(_alibi_slope(kh * GQA + g))
                bias[later, kh, g * BLOCK:(g + 1) * BLOCK] = np.where(valid, -slope * dist.astype(np.float32), NEG)
    return jnp.asarray(r, BF), jnp.asarray(bias)


def _head_masks():
    W = N_KV * HD
    rq = lax.broadcasted_iota(jnp.int32, (GQA * BLOCK, W), 0) >> 7
    cq = lax.broadcasted_iota(jnp.int32, (GQA * BLOCK, W), 1) >> 6
    rk = lax.broadcasted_iota(jnp.int32, (GQA * 2 * BLOCK, W), 0) >> 8
    ck = lax.broadcasted_iota(jnp.int32, (GQA * 2 * BLOCK, W), 1) >> 6
    return rq == cq, rk == ck


def _sink_col(sink_ref, kh):
    rb = lax.broadcasted_iota(jnp.int32, (GQA * BLOCK, 1), 0) >> 7
    col = jnp.full((GQA * BLOCK, 1), sink_ref[kh * GQA + GQA - 1], F32)
    for g in range(GQA - 1):
        col = jnp.where(rb == g, sink_ref[kh * GQA + g], col)
    return col


def _stack4(a):
    return jnp.concatenate([a] * GQA, axis=0)


def _rows_to_lanes(a):
    return jnp.concatenate([a[g * BLOCK:(g + 1) * BLOCK] for g in range(GQA)], axis=1)


def _lanes_to_rows(a, n):
    return jnp.concatenate([a[:, g * n:(g + 1) * n] for g in range(GQA)], axis=0)


def _scores(qg, k4, bias, sink_col, maskq):
    qs = jnp.where(maskq, _stack4(qg), jnp.zeros((), BF))
    s = _mm_nt(qs, k4) * (HD ** -0.5) + bias
    m = jnp.maximum(jnp.max(s, axis=-1, keepdims=True), sink_col)
    return qs, jnp.exp(s - m), jnp.exp(sink_col - m)


def _bands(kvc_ref, kvp_ref, rep_ref):
    W = N_KV * HD
    kvp, kvc = kvp_ref[...], kvc_ref[...]
    kband = jnp.concatenate([kvp[:, :W], kvc[:, :W]], axis=0)
    vband = jnp.concatenate([kvp[:, W:], kvc[:, W:]], axis=0)
    return _mm(kband, rep_ref[...]).astype(BF), _mm(vband, rep_ref[...]).astype(BF)


def _attn_fwd(q, kv, sinks, rep, bias):
    S = q.shape[0]
    nb = S // BLOCK
    W = N_KV * HD

    def body(sink_ref, q_ref, kvc_ref, kvp_ref, rep_ref, bias_ref, o_ref):
        maskq, maskk = _head_masks()
        ones_bd = maskk.astype(BF)
        lane_head = lax.broadcasted_iota(jnp.int32, (BLOCK, W), 1) >> 6
        k4all, v4all = _bands(kvc_ref, kvp_ref, rep_ref)
        for kh in range(N_KV):
            cols = slice(kh * W, (kh + 1) * W)
            _, e, es = _scores(q_ref[:, cols], k4all[:, cols], bias_ref[kh], _sink_col(sink_ref, kh), maskq)
            vbd = jnp.concatenate([jnp.where(maskk, _stack4(v4all[:, cols]), jnp.zeros((), BF)), ones_bd], axis=1)
            nd = _mm(_rows_to_lanes(e.astype(BF)), vbd)
            es_l = jnp.broadcast_to(es[(GQA - 1) * BLOCK:], (BLOCK, W))
            for g in range(GQA - 1):
                es_l = jnp.where(lane_head == g, es[g * BLOCK:(g + 1) * BLOCK], es_l)
            o_ref[:, cols] = (nd[:, :W] / (nd[:, W:] + es_l)).astype(BF)

    return pl.pallas_call(
        body, name="attn_fwd", grid=(nb,),
        in_specs=[pl.BlockSpec(memory_space=pltpu.SMEM), _row_spec(BLOCK, D), _row_spec(BLOCK, 2 * W),
                  pl.BlockSpec((BLOCK, 2 * W), lambda i: (jnp.maximum(i - 1, 0), 0)), _const_spec((W, N_KV * W)),
                  pl.BlockSpec((None, N_KV, GQA * BLOCK, 2 * BLOCK), lambda i: (jnp.minimum(i, 1), 0, 0, 0))],
        out_specs=_row_spec(BLOCK, D), out_shape=jax.ShapeDtypeStruct((S, D), BF),
        compiler_params=_params(),
    )(sinks, q, kv, kv, rep, bias)


def _attn_bwd(q, kv, att, do, sinks, rep, bias):
    S = q.shape[0]
    nb = S // BLOCK
    W = N_KV * HD

    def qmap(i):
        return (jnp.minimum(i, nb - 1), 0)

    def body(sink_ref, q_ref, kvc_ref, kvp_ref, att_ref, do_ref, rep_ref, bias_ref,
             dq_ref, dkv_ref, dsink_ref, carry, sink_acc):
        i = pl.program_id(0)

        @pl.when(i == 0)
        def _():
            sink_acc[...] = jnp.zeros_like(sink_acc)
            carry[...] = jnp.zeros_like(carry)

        @pl.when(i < nb)
        def _():
            maskq, maskk = _head_masks()
            ones_k = jnp.ones((2 * BLOCK, BLOCK), BF)
            head_sum = ((lax.broadcasted_iota(jnp.int32, (W, GQA * BLOCK), 0) >> 6)
                        == (lax.broadcasted_iota(jnp.int32, (W, GQA * BLOCK), 1) >> 7)).astype(BF)
            k4all, v4all = _bands(kvc_ref, kvp_ref, rep_ref)
            dk4s, dv4s = [], []
            for kh in range(N_KV):
                cols = slice(kh * W, (kh + 1) * W)
                k4, v4 = k4all[:, cols], v4all[:, cols]
                qs, e, es = _scores(q_ref[:, cols], k4, bias_ref[kh], _sink_col(sink_ref, kh), maskq)
                inv = 1.0 / (_mm(e.astype(BF), ones_k) + es)
                p = e * jnp.concatenate([inv, inv], axis=1)
                dog = do_ref[:, cols]
                dos = jnp.where(maskq, _stack4(dog), jnp.zeros((), BF))
                dp = _mm_nt(dos, v4)
                prod = (dog.astype(F32) * att_ref[:, cols].astype(F32)).astype(BF)
                delta = _lanes_to_rows(_mm(prod, head_sum), BLOCK)
                ds = (p * (dp - jnp.concatenate([delta, delta], axis=1))).astype(BF)
                sink_acc[kh] += es * inv * delta
                kbd = jnp.where(maskk, _stack4(k4), jnp.zeros((), BF))
                dq_ref[:, cols] = (_mm(_rows_to_lanes(ds), kbd) * (HD ** -0.5)).astype(BF)
                dk4s.append((_mm_tn(ds, qs) * (HD ** -0.5)).astype(BF))
                dv4s.append(_mm_tn(p.astype(BF), dos).astype(BF))
            dkband = _mm_nt(jnp.concatenate(dk4s, axis=1), rep_ref[...])
            dvband = _mm_nt(jnp.concatenate(dv4s, axis=1), rep_ref[...])
            dkv_ref[:, :W] = (carry[:, :W] + dkband[:BLOCK]).astype(BF)
            dkv_ref[:, W:] = (carry[:, W:] + dvband[:BLOCK]).astype(BF)
            carry[:, :W] = dkband[BLOCK:]
            carry[:, W:] = dvband[BLOCK:]

        @pl.when(i == nb)
        def _():
            dkv_ref[...] = carry[...].astype(BF)
            for hq in range(N_Q):
                kh, g = divmod(hq, GQA)
                dsink_ref[hq:hq + 1, :] = -jnp.sum(sink_acc[kh, g * BLOCK:(g + 1) * BLOCK, :], axis=0, keepdims=True)

    return pl.pallas_call(
        body, name="attn_bwd", grid=(nb + 1,),
        in_specs=[pl.BlockSpec(memory_space=pltpu.SMEM), pl.BlockSpec((BLOCK, D), qmap),
                  pl.BlockSpec((BLOCK, 2 * W), qmap),
                  pl.BlockSpec((BLOCK, 2 * W), lambda i: (jnp.maximum(jnp.minimum(i, nb - 1) - 1, 0), 0)),
                  pl.BlockSpec((BLOCK, D), qmap), pl.BlockSpec((BLOCK, D), qmap), _const_spec((W, N_KV * W)),
                  pl.BlockSpec((None, N_KV, GQA * BLOCK, 2 * BLOCK), lambda i: (jnp.minimum(i, 1), 0, 0, 0))],
        out_specs=[pl.BlockSpec((BLOCK, D), qmap),
                   pl.BlockSpec((BLOCK, 2 * W), lambda i: (jnp.maximum(i - 1, 0), 0)),
                   _const_spec((N_Q, BLOCK))],
        out_shape=[jax.ShapeDtypeStruct((S, D), BF), jax.ShapeDtypeStruct((S, 2 * W), BF),
                   jax.ShapeDtypeStruct((N_Q, BLOCK), F32)],
        scratch_shapes=[pltpu.VMEM((BLOCK, 2 * W), F32), pltpu.VMEM((N_KV, GQA * BLOCK, BLOCK), F32)],
        compiler_params=_params(),
    )(sinks, q, kv, kv, att, do, rep, bias)


def _head(h, g, target, tm):
    S = h.shape[0]

    def body(h_ref, g_ref, t_ref, dh_ref, loss_ref, dg_ref):
        i = pl.program_id(0)

        @pl.when(i == 0)
        def _():
            loss_ref[...] = jnp.zeros_like(loss_ref)
            dg_ref[...] = jnp.zeros_like(dg_ref)

        gain = g_ref[...]
        y, xhat, r = _rms_fwd(h_ref[...], gain)
        err = y - t_ref[...]
        tile = 0.5 * jnp.sum(jnp.sum(err * err, axis=-1, keepdims=True) * (1.0 / D), axis=0, keepdims=True)
        loss_ref[...] += jnp.broadcast_to(tile, loss_ref.shape)
        dx, dg = _rms_bwd(err * (1.0 / D), xhat, r, gain)
        dg_ref[...] += dg
        dh_ref[...] = dx

    return pl.pallas_call(
        body, name="loss_head", grid=(S // tm,),
        in_specs=[_row_spec(tm, D), _const_spec((1, D)), _row_spec(tm, D)],
        out_specs=[_row_spec(tm, D), _const_spec((8, 128)), _const_spec((1, D))],
        out_shape=[jax.ShapeDtypeStruct((S, D), F32), jax.ShapeDtypeStruct((8, 128), F32),
                   jax.ShapeDtypeStruct((1, D), F32)],
        compiler_params=_params(),
    )(h, g, target)


def _wgrad(a, b, name, *, tn, out_blocked=False, a_transposed=False, ts=1024):
    a_blocked, b_blocked = a.ndim == 3, b.ndim == 3
    na = a.shape[0] if a_blocked else 1
    S = b.shape[-2]
    K = a.shape[0] if a_transposed else a.shape[-1]
    nbk = b.shape[0] if b_blocked else b.shape[1] // tn
    ts = min(ts, S)
    ns = S // ts
    assert not (a_blocked and (out_blocked or a_transposed))

    def body(a_ref, b_ref, o_ref, acc):
        s = pl.program_id(2)

        @pl.when(s == 0)
        def _():
            acc[...] = jnp.zeros_like(acc)

        if a_transposed:
            acc[...] += _mm(a_ref[...], b_ref[...])
        else:
            acc[...] += _mm_tn(a_ref[...], b_ref[...])

        @pl.when(s == ns - 1)
        def _():
            o_ref[...] = acc[...].astype(o_ref.dtype)

    if a_transposed:
        a_spec = pl.BlockSpec((K, ts), lambda i, j, s: (0, s))
    elif a_blocked:
        a_spec = pl.BlockSpec((None, ts, K), lambda i, j, s: (i, s, 0))
    else:
        a_spec = pl.BlockSpec((ts, K), lambda i, j, s: (s, 0))
    b_spec = (pl.BlockSpec((None, ts, tn), lambda i, j, s: (j, s, 0)) if b_blocked
              else pl.BlockSpec((ts, tn), lambda i, j, s: (s, j)))
    if out_blocked:
        o_spec = pl.BlockSpec((None, K, tn), lambda i, j, s: (j, 0, 0))
        o_shape = jax.ShapeDtypeStruct((nbk, K, tn), BF)
    else:
        o_spec = pl.BlockSpec((K, tn), lambda i, j, s: (i, j))
        o_shape = jax.ShapeDtypeStruct((na * K, nbk * tn), BF)
    return pl.pallas_call(
        body, name=name, grid=(na, nbk, ns), in_specs=[a_spec, b_spec], out_specs=o_spec, out_shape=o_shape,
        scratch_shapes=[pltpu.VMEM((K, tn), F32)],
        compiler_params=pltpu.CompilerParams(dimension_semantics=("arbitrary", "arbitrary", "arbitrary"),
                                             vmem_limit_bytes=VMEM_BIG),
    )(a, b)


def _adamw_math(w, g, m, v):
    m = B1 * m + (1.0 - B1) * g
    v = B2 * v + (1.0 - B2) * (g * g)
    m_hat = m / (1.0 - B1 ** STEP)
    v_hat = v / (1.0 - B2 ** STEP)
    return -LR * (m_hat / (jnp.sqrt(v_hat) + ADAM_EPS) + WD * w), m, v


def _adamw(parts, w, m, v, name, tr):
    R = w.shape[0]
    tr = min(tr, R)

    def body(p_ref, w_ref, m_ref, v_ref, g_ref, d_ref, nm_ref, nv_ref):
        g = p_ref[0].astype(F32)
        for s in range(1, N_DEV):
            g = g + p_ref[s].astype(F32)
        g_ref[...] = g
        d_ref[...], nm_ref[...], nv_ref[...] = _adamw_math(w_ref[...], g, m_ref[...], v_ref[...])

    rs = _row_spec(tr, ROW)
    return pl.pallas_call(
        body, name=name, grid=(R // tr,),
        in_specs=[pl.BlockSpec((N_DEV, tr, ROW), lambda i: (0, i, 0)), rs, rs, rs],
        out_specs=[rs, rs, rs, rs], out_shape=[jax.ShapeDtypeStruct((R, ROW), F32)] * 4,
        compiler_params=_params(),
    )(parts, w, m, v)


def _sum8(parts, name):
    R = parts.shape[1]

    def body(p_ref, o_ref):
        g = p_ref[0]
        for s in range(1, N_DEV):
            g = g + p_ref[s]
        o_ref[...] = g

    return pl.pallas_call(
        body, name=name, grid=(1,),
        in_specs=[pl.BlockSpec((N_DEV, R, ROW), lambda i: (0, 0, 0))],
        out_specs=pl.BlockSpec((R, ROW), lambda i: (0, 0)), out_shape=jax.ShapeDtypeStruct((R, ROW), F32),
        compiler_params=_params(),
    )(parts)


def _adamw_small(g, w, m, v, name):
    R = w.shape[0]

    def body(g_ref, w_ref, m_ref, v_ref, d_ref, nm_ref, nv_ref):
        d_ref[...], nm_ref[...], nv_ref[...] = _adamw_math(w_ref[...], g_ref[...], m_ref[...], v_ref[...])

    spec = pl.BlockSpec((R, ROW), lambda i: (0, 0))
    return pl.pallas_call(
        body, name=name, grid=(1,), in_specs=[spec] * 4, out_specs=[spec] * 3,
        out_shape=[jax.ShapeDtypeStruct((R, ROW), F32)] * 3, compiler_params=_params(),
    )(g, w, m, v)


def _mesh_pos():
    return lax.axis_index("x"), lax.axis_index("y"), lax.axis_index("c")


def _flip(v, bit):
    return 1 - v if bit else v


def _all_gather(xs, name):
    R, C = xs.shape
    MESH = pl.DeviceIdType.MESH

    def body(x_ref, out_ref, send_sems, recv_sems, local_sem):
        x, y, c = _mesh_pos()
        me, sibling = (x, y, c), (x, y, 1 - c)
        chips = [(1 - x, y), (x, 1 - y), (1 - x, 1 - y)]

        def rows(px, py, pc):
            return out_ref.at[4 * px + 2 * py + pc]

        def copy(k, block, to, src=None):
            return pltpu.make_async_remote_copy(
                src_ref=rows(*block) if src is None else src, dst_ref=rows(*block),
                send_sem=send_sems.at[k], recv_sem=recv_sems.at[k], device_id=to, device_id_type=MESH)

        mine = pltpu.make_async_copy(x_ref, rows(*me), local_sem)
        mine.start()
        first = [copy(0, me, sibling, src=x_ref)]
        first += [copy(1 + j, me, (*chip, c), src=x_ref) for j, chip in enumerate(chips)]
        for cp in first:
            cp.start()
        passed = [copy(4 + j, (*chip, c), sibling) for j, chip in enumerate(chips)]
        for j, chip in enumerate(chips):
            copy(1 + j, (*chip, c), me).wait_recv()
            passed[j].start()
        copy(0, sibling, me).wait_recv()
        for j, chip in enumerate(chips):
            copy(4 + j, (*chip, 1 - c), me).wait_recv()
        for cp in first + passed:
            cp.wait_send()
        mine.wait()

    return pl.pallas_call(
        body, name=name, out_shape=jax.ShapeDtypeStruct((N_DEV, R, C), xs.dtype),
        in_specs=[pl.BlockSpec(memory_space=pl.ANY)], out_specs=pl.BlockSpec(memory_space=pl.ANY),
        scratch_shapes=[pltpu.SemaphoreType.DMA((7,)), pltpu.SemaphoreType.DMA((7,)), pltpu.SemaphoreType.DMA(())],
    )(xs)


def _all_to_all(g, name):
    _, R, C = g.shape
    MESH = pl.DeviceIdType.MESH

    def body(g_ref, out_ref, send_sems, recv_sems, local_sem):
        x, y, c = _mesh_pos()
        me = 4 * x + 2 * y + c
        mine = pltpu.make_async_copy(g_ref.at[me], out_ref.at[me], local_sem)
        mine.start()
        copies = []
        for k in range(1, N_DEV):
            px, py, pc = _flip(x, k & 4), _flip(y, k & 2), _flip(c, k & 1)
            peer = 4 * px + 2 * py + pc
            copies.append((pltpu.make_async_remote_copy(
                src_ref=g_ref.at[peer], dst_ref=out_ref.at[me], send_sem=send_sems.at[k - 1],
                recv_sem=recv_sems.at[k - 1], device_id=(px, py, pc), device_id_type=MESH),
                pltpu.make_async_remote_copy(
                src_ref=g_ref.at[peer], dst_ref=out_ref.at[peer], send_sem=send_sems.at[k - 1],
                recv_sem=recv_sems.at[k - 1], device_id=(px, py, pc), device_id_type=MESH)))
        for send, _ in copies:
            send.start()
        for send, recv in copies:
            recv.wait_recv()
        for send, _ in copies:
            send.wait_send()
        mine.wait()

    return pl.pallas_call(
        body, name=name, out_shape=jax.ShapeDtypeStruct(g.shape, g.dtype),
        in_specs=[pl.BlockSpec(memory_space=pl.ANY)], out_specs=pl.BlockSpec(memory_space=pl.ANY),
        scratch_shapes=[pltpu.SemaphoreType.DMA((7,)), pltpu.SemaphoreType.DMA((7,)), pltpu.SemaphoreType.DMA(())],
    )(g)


def _seg_rows(n, align):
    rows = -(-n // ROW)
    return -(-rows // align) * align


def _pack(arrs, align, lead=()):
    parts = []
    for a in arrs:
        n = int(np.prod(a.shape[len(lead):]))
        rows = _seg_rows(n, align)
        flat = a.reshape(lead + (n,))
        flat = jnp.pad(flat, [(0, 0)] * len(lead) + [(0, rows * ROW - n)])
        parts.append(flat.reshape(lead + (rows, ROW)))
    return jnp.concatenate(parts, axis=len(lead))


def _unpack(buf, shapes, align, lead=()):
    out, r0 = [], 0
    for shp in shapes:
        n = int(np.prod(shp))
        rows = _seg_rows(n, align)
        seg = lax.slice_in_dim(buf, r0, r0 + rows, axis=len(lead))
        out.append(seg.reshape(lead + (rows * ROW,))[..., :n].reshape(lead + tuple(shp)))
        r0 += rows
    return out


def kernel(x, p, norm_mix, norm_ffn, norm_ple, norm_kv, norm_final, a_w_in, a_norm_v, a_w_s, a_b_s, a_w_out, w_kv, b_w_q, b_sinks, b_w_o, f_w_up, f_conv_w, f_conv_b, f_w_down, ple_w_in, ple_w_gate, ple_b_gate, loss_target, m_norm_mix, m_norm_ffn, m_norm_ple, m_norm_kv, m_norm_final, m_a_w_in, m_a_norm_v, m_a_w_s, m_a_b_s, m_a_w_out, m_w_kv, m_b_w_q, m_b_sinks, m_b_w_o, m_f_w_up, m_f_conv_w, m_f_conv_b, m_f_w_down, m_ple_w_in, m_ple_w_gate, m_ple_b_gate, v_norm_mix, v_norm_ffn, v_norm_ple, v_norm_kv, v_norm_final, v_a_w_in, v_a_norm_v, v_a_w_s, v_a_b_s, v_a_w_out, v_w_kv, v_b_w_q, v_b_sinks, v_b_w_o, v_f_w_up, v_f_conv_w, v_f_conv_b, v_f_w_down, v_ple_w_in, v_ple_w_gate, v_ple_b_gate):
    w = dict(norm_mix=norm_mix, norm_ffn=norm_ffn, norm_ple=norm_ple, norm_kv=norm_kv, norm_final=norm_final,
             a_w_in=a_w_in, a_norm_v=a_norm_v, a_w_s=a_w_s, a_b_s=a_b_s, a_w_out=a_w_out, w_kv=w_kv, b_w_q=b_w_q,
             b_sinks=b_sinks, b_w_o=b_w_o, f_w_up=f_w_up, f_conv_w=f_conv_w, f_conv_b=f_conv_b, f_w_down=f_w_down,
             ple_w_in=ple_w_in, ple_w_gate=ple_w_gate, ple_b_gate=ple_b_gate)
    mom_m = dict(norm_mix=m_norm_mix, norm_ffn=m_norm_ffn, norm_ple=m_norm_ple, norm_kv=m_norm_kv,
                 norm_final=m_norm_final, a_w_in=m_a_w_in, a_norm_v=m_a_norm_v, a_w_s=m_a_w_s, a_b_s=m_a_b_s,
                 a_w_out=m_a_w_out, w_kv=m_w_kv, b_w_q=m_b_w_q, b_sinks=m_b_sinks, b_w_o=m_b_w_o, f_w_up=m_f_w_up,
                 f_conv_w=m_f_conv_w, f_conv_b=m_f_conv_b, f_w_down=m_f_w_down, ple_w_in=m_ple_w_in,
                 ple_w_gate=m_ple_w_gate, ple_b_gate=m_ple_b_gate)
    mom_v = dict(norm_mix=v_norm_mix, norm_ffn=v_norm_ffn, norm_ple=v_norm_ple, norm_kv=v_norm_kv,
                 norm_final=v_norm_final, a_w_in=v_a_w_in, a_norm_v=v_a_norm_v, a_w_s=v_a_w_s, a_b_s=v_a_b_s,
                 a_w_out=v_a_w_out, w_kv=v_w_kv, b_w_q=v_b_w_q, b_sinks=v_b_sinks, b_w_o=v_b_w_o, f_w_up=v_f_w_up,
                 f_conv_w=v_f_conv_w, f_conv_b=v_f_conv_b, f_w_down=v_f_w_down, ple_w_in=v_ple_w_in,
                 ple_w_gate=v_ple_w_gate, ple_b_gate=v_ple_b_gate)
    S = x.shape[1]
    tm = min(256, S)
    me = 4 * lax.axis_index("x") + 2 * lax.axis_index("y") + lax.axis_index("c")

    big_shapes = [w[n].shape for n in _BIG]
    small_shapes = [w[n].shape for n in _SMALL_SHARDED]
    wbig = _unpack(_all_gather(_pack([w[n].astype(BF) for n in _BIG], 16), "gather_weights"),
                   big_shapes, 16, (N_DEV,))
    wsmall = _unpack(_all_gather(_pack([w[n] for n in _SMALL_SHARDED], 8), "gather_small"),
                     small_shapes, 8, (N_DEV,))
    gw = dict(zip(_BIG, wbig))
    gw.update(zip(_SMALL_SHARDED, wsmall))
    win = gw['a_w_in'][:, 0].transpose(1, 0, 2).reshape(D, 2 * AW)
    gv = gw['a_norm_v'].reshape(1, AW)
    wout = gw['a_w_out'].reshape(AW, D)
    wkv = gw['w_kv'].reshape(D, 2 * N_KV * HD)
    wq = gw['b_w_q'].reshape(D, D)
    wo = gw['b_w_o'].reshape(D, D)
    wup = [gw['f_w_up'][:, l] for l in range(2)]
    cw = [gw['f_conv_w'][:, l] for l in range(2)]
    cb = [w['f_conv_b'][l].reshape(N_DEV, 1, FF_BLK) for l in range(2)]
    wdown = [gw['f_w_down'][:, l].reshape(D_FF, D) for l in range(2)]
    wple = [gw['ple_w_in'][:, l].transpose(1, 0, 2).reshape(PLE, D) for l in range(2)]
    wgate = [gw['ple_w_gate'][:, l].reshape(D, D) for l in range(2)]
    ws = a_w_s[0].astype(BF)
    wst = jnp.swapaxes(ws, 1, 2)
    bsb = jnp.broadcast_to(a_b_s[0][:, :, None], (GROUPS, CHUNK, CHUNK))
    row = lambda a: a.reshape(1, -1)
    pb = [p[l, 0].astype(BF) for l in range(2)]
    sinks = b_sinks.reshape(N_Q)
    rep, bias = _attn_consts()
    x0 = x[0]

    h1, z = _mixa_fwd(x0, row(norm_mix[0]), win, gv, ws, bsb, wout, tm)
    h2, hh0 = _ffn_fwd(h1, row(norm_ffn[0]), wup[0], cw[0], cb[0], wdown[0], tm)
    h3 = _ple_fwd(h2, row(norm_ple[0]), wgate[0], row(ple_b_gate[0]), pb[0], wple[0], tm)
    q, kv = _qkv_fwd(h3, row(norm_mix[1]), wq, row(norm_kv), wkv, tm)
    att = _attn_fwd(q, kv, sinks, rep, bias)
    h4 = _oproj_fwd(att, wo, h3, tm)
    h5, hh1 = _ffn_fwd(h4, row(norm_ffn[1]), wup[1], cw[1], cb[1], wdown[1], tm)
    h6 = _ple_fwd(h5, row(norm_ple[1]), wgate[1], row(ple_b_gate[1]), pb[1], wple[1], tm)

    dh6, loss_blk, d_norm_final = _head(h6, row(norm_final), loss_target[0], tm)
    dh5, de1, xp1, dpre1, dbg1, dgp1 = _ple_bwd(dh6, h5, row(norm_ple[1]), wgate[1], row(ple_b_gate[1]), pb[1],
                                                 wple[1], tm)
    dh4, xf1, dhh1, act1, dhb5, dcw1, dcb1, dgf1 = _ffn_bwd(dh5, h4, hh1, row(norm_ffn[1]), wup[1], cw[1], cb[1],
                                                            wdown[1], tm)
    datt, dhb4 = _oproj_bwd(dh4, wo, tm)
    dq, dkv, dsink = _attn_bwd(q, kv, att, datt, sinks, rep, bias)
    dh3, xq, xkv, dgq, dgkv = _qkv_bwd(dh4, h3, dq, dkv, row(norm_mix[1]), wq, row(norm_kv), wkv, tm)
    dh2, de0, xp0, dpre0, dbg0, dgp0 = _ple_bwd(dh3, h2, row(norm_ple[0]), wgate[0], row(ple_b_gate[0]), pb[0],
                                                 wple[0], tm)
    dh1, xf0, dhh0, act0, dhb2, dcw0, dcb0, dgf0 = _ffn_bwd(dh2, h1, hh0, row(norm_ffn[0]), wup[0], cw[0], cb[0],
                                                            wdown[0], tm)
    dx, xa, dz, gated, dhb1, dws, dbs, dgv, dga = _mixa_bwd(dh1, x0, z, row(norm_mix[0]), win, gv, ws, wst, bsb,
                                                            wout, tm)

    def col_blocks(g2d, n):
        return g2d.reshape(g2d.shape[0], N_DEV, n).transpose(1, 0, 2)

    def row_blocks(g2d):
        return g2d.reshape(N_DEV, g2d.shape[0] // N_DEV, g2d.shape[1])

    def layers(a, b):
        return jnp.stack([a, b], axis=1)

    pg = {
        'a_w_in': _wgrad(xa, dz, "wgrad_a_in", tn=2 * AW // N_DEV, out_blocked=True, a_transposed=True)[:, None],
        'a_w_out': row_blocks(_wgrad(gated, dhb1, "wgrad_a_out", tn=1024))[:, None],
        'w_kv': row_blocks(_wgrad(xkv, dkv, "wgrad_kv", tn=512)),
        'b_w_q': row_blocks(_wgrad(xq, dq, "wgrad_q", tn=1024))[:, None],
        'b_w_o': row_blocks(_wgrad(att, dhb4, "wgrad_o", tn=1024))[:, None],
        'f_w_up': layers(_wgrad(xf0, dhh0, "wgrad_up0", tn=FF_BLK, out_blocked=True, a_transposed=True),
                         _wgrad(xf1, dhh1, "wgrad_up1", tn=FF_BLK, out_blocked=True, a_transposed=True)),
        'f_w_down': layers(row_blocks(_wgrad(act0, dhb2, "wgrad_down0", tn=1024)),
                           row_blocks(_wgrad(act1, dhb5, "wgrad_down1", tn=1024))),
        'ple_w_in': layers(col_blocks(_wgrad(pb[0], de0, "wgrad_ple_in0", tn=1024), D // N_DEV),
                           col_blocks(_wgrad(pb[1], de1, "wgrad_ple_in1", tn=1024), D // N_DEV)),
        'ple_w_gate': layers(row_blocks(_wgrad(xp0, dpre0, "wgrad_gate0", tn=1024)),
                             row_blocks(_wgrad(xp1, dpre1, "wgrad_gate1", tn=1024))),
    }
    parts = _all_to_all(_pack([pg[n] for n in _BIG], 16, (N_DEV,)), "exchange_grads")

    packed = lambda d: _pack([d[n] for n in _BIG], 16)
    outs = _adamw(parts, packed(w), packed(mom_m), packed(mom_v), "adamw_big", 224)
    g_big, d_big, m_big, v_big = [dict(zip(_BIG, _unpack(o, big_shapes, 16))) for o in outs]

    small_g = {
        'norm_mix': jnp.concatenate([dga, dgq], axis=0), 'norm_ffn': jnp.concatenate([dgf0, dgf1], axis=0),
        'norm_ple': jnp.concatenate([dgp0, dgp1], axis=0), 'norm_kv': dgkv.reshape(D),
        'norm_final': d_norm_final.reshape(D), 'a_w_s': dws[None], 'a_b_s': dbs[None, :, :, 0],
        'b_sinks': dsink[:, 0].reshape(1, N_Q),
        'f_conv_b': jnp.stack([dcb0.reshape(2 * D_FF), dcb1.reshape(2 * D_FF)]),
        'ple_b_gate': jnp.concatenate([dbg0, dbg1], axis=0),
        'a_norm_v': dgv, 'f_conv_w': jnp.stack([dcw0, dcw1], axis=1),
    }
    names_all = _REPL + _SMALL_SHARDED
    full_shapes = [small_g[n].shape for n in names_all]
    gsum = _sum8(_all_gather(_pack([small_g[n] for n in names_all], 8), "gather_small_grads"), "sum_small_grads")
    gs = dict(zip(names_all, _unpack(gsum, full_shapes, 8)))
    gs['a_norm_v'] = lax.dynamic_slice_in_dim(gs['a_norm_v'], me * (AW // N_DEV), AW // N_DEV, axis=1)
    gs['f_conv_w'] = lax.dynamic_index_in_dim(gs['f_conv_w'], me, axis=0, keepdims=False)
    gs = {n: gs[n].reshape(w[n].shape) for n in names_all}
    packs = lambda d: _pack([d[n] for n in names_all], 8)
    souts = _adamw_small(packs(gs), packs(w), packs(mom_m), packs(mom_v), "adamw_small")
    shp = [w[n].shape for n in names_all]
    d_s, m_s, v_s = [dict(zip(names_all, _unpack(o, shp, 8))) for o in souts]

    grads = {**g_big, **gs}
    delta = {**d_big, **d_s}
    new_m = {**m_big, **m_s}
    new_v = {**v_big, **v_s}
    loss = lax.psum(loss_blk[0, 0], ("x", "y", "c"))
    return (loss, dx[None], *[grads[n] for n in _PARAMS], *[delta[n] for n in _PARAMS],
            *[new_m[n] for n in _PARAMS], *[new_v[n] for n in _PARAMS])
```

```python
import functools

import numpy as np
import jax
import jax.numpy as jnp
from jax import lax
from jax.experimental import pallas as pl
from jax.experimental.pallas import tpu as pltpu

F32 = jnp.float32
BF = jnp.bfloat16

N_DEV = 8
D = 1024
AW = 1024
GROUPS = 8
CHUNK = 128
HD = 64
N_Q = 16
N_KV = 4
GQA = 4
BLOCK = 128
D_FF = 2816
FF_BLK = 704
N_FF_BLK = 4
PLE = 256
EPS = 1e-6
LR, B1, B2, ADAM_EPS, WD, STEP = 0.001, 0.9, 0.999, 1e-08, 0.01, 10
GELU_C0 = 0.7978845608028654
GELU_C1 = 0.044715
NEG = -1e30
ROW = 1024
VMEM_BIG = 56 * 1024 * 1024

_PARAMS = ['norm_mix', 'norm_ffn', 'norm_ple', 'norm_kv', 'norm_final', 'a_w_in', 'a_norm_v', 'a_w_s', 'a_b_s',
           'a_w_out', 'w_kv', 'b_w_q', 'b_sinks', 'b_w_o', 'f_w_up', 'f_conv_w', 'f_conv_b', 'f_w_down',
           'ple_w_in', 'ple_w_gate', 'ple_b_gate']
_BIG = ['a_w_in', 'a_w_out', 'w_kv', 'b_w_q', 'b_w_o', 'f_w_up', 'f_w_down', 'ple_w_in', 'ple_w_gate']
_SMALL_SHARDED = ['a_norm_v', 'f_conv_w']
_REPL = ['norm_mix', 'norm_ffn', 'norm_ple', 'norm_kv', 'norm_final', 'a_w_s', 'a_b_s', 'b_sinks', 'f_conv_b',
         'ple_b_gate']


def _mm(a, b):
    return jnp.dot(a, b, preferred_element_type=F32)


def _mm_nt(a, b):
    return lax.dot_general(a, b, (((1,), (1,)), ((), ())), preferred_element_type=F32)


def _mm_tn(a, b):
    return lax.dot_general(a, b, (((0,), (0,)), ((), ())), preferred_element_type=F32)


def _const_spec(shape, single=False):
    nd = len(shape)
    if single:
        return pl.BlockSpec(shape, lambda *_: (0,) * nd, pipeline_mode=pl.Buffered(1))
    return pl.BlockSpec(shape, lambda *_: (0,) * nd)


def _row_spec(tm, n):
    return pl.BlockSpec((tm, n), lambda i: (i, 0))


def _rms_fwd(x, g):
    r = lax.rsqrt(jnp.mean(x * x, axis=-1, keepdims=True) + EPS)
    xhat = x * r
    return xhat * g, xhat, r


def _rms_bwd(dy, xhat, r, g):
    a = dy * g
    dx = r * (a - xhat * jnp.mean(a * xhat, axis=-1, keepdims=True))
    return dx, jnp.sum(dy * xhat, axis=0, keepdims=True)


def _sigmoid(x):
    return 1.0 / (1.0 + jnp.exp(-x))


def _params(vmem=None):
    return pltpu.CompilerParams(dimension_semantics=("arbitrary",), vmem_limit_bytes=vmem)


def _tri_masks():
    t = lax.broadcasted_iota(jnp.int32, (CHUNK, CHUNK), 0)
    s = lax.broadcasted_iota(jnp.int32, (CHUNK, CHUNK), 1)
    return t >= s, s >= t


def _mixa_fwd(x, g, win, gv, ws, bsb, wout, tm):
    S = x.shape[0]
    nc = tm // CHUNK

    def body(x_ref, g_ref, win_ref, gv_ref, ws_ref, bsb_ref, wout_ref, h_ref, z_ref, gated):
        x = x_ref[...]
        xn, _, _ = _rms_fwd(x, g_ref[...])
        z = _mm(xn.astype(BF), win_ref[...])
        z_ref[...] = z.astype(BF)
        zg = 0.5 * z * (1.0 + jnp.tanh(GELU_C0 * (z + GELU_C1 * z * z * z)))
        u = zg[:, :AW]
        vn, _, _ = _rms_fwd(zg[:, AW:], gv_ref[...])
        vn = vn.astype(BF)
        tril, _ = _tri_masks()
        for h in range(GROUPS):
            wm = jnp.where(tril, ws_ref[h], jnp.zeros((), BF))
            for c in range(nc):
                rs, cs = slice(c * CHUNK, (c + 1) * CHUNK), slice(h * CHUNK, (h + 1) * CHUNK)
                s = _mm(wm, vn[rs, cs]) + bsb_ref[h]
                gated[rs, cs] = (u[rs, cs] * s).astype(BF)
        h_ref[...] = x + _mm(gated[...], wout_ref[...])

    return pl.pallas_call(
        body, name="mixa_fwd", grid=(S // tm,),
        in_specs=[_row_spec(tm, D), _const_spec((1, D)), _const_spec((D, 2 * AW)), _const_spec((1, AW)),
                  _const_spec((GROUPS, CHUNK, CHUNK)), _const_spec((GROUPS, CHUNK, CHUNK)), _const_spec((AW, D))],
        out_specs=[_row_spec(tm, D), _row_spec(tm, 2 * AW)],
        out_shape=[jax.ShapeDtypeStruct((S, D), F32), jax.ShapeDtypeStruct((S, 2 * AW), BF)],
        scratch_shapes=[pltpu.VMEM((tm, AW), BF)],
        compiler_params=_params(VMEM_BIG),
    )(x, g, win, gv, ws, bsb, wout)


def _mixa_bwd(dh, x, z, g, win, gv, ws, wst, bsb, wout, tm):
    S = x.shape[0]
    nt = S // tm
    nc = tm // CHUNK

    def body(dh_ref, x_ref, z_ref, g_ref, win_ref, gv_ref, ws_ref, wst_ref, bsb_ref, wout_ref,
             dx_ref, xn_ref, dz_ref, gated_ref, dhb_ref, dws_ref, dbs_ref, dgv_ref, dg_ref, du_scr, dvn_scr):
        i = pl.program_id(0)

        @pl.when(i == 0)
        def _():
            dws_ref[...] = jnp.zeros_like(dws_ref)
            dbs_ref[...] = jnp.zeros_like(dbs_ref)
            dgv_ref[...] = jnp.zeros_like(dgv_ref)
            dg_ref[...] = jnp.zeros_like(dg_ref)

        x = x_ref[...]
        gmix = g_ref[...]
        xn, xhat, r = _rms_fwd(x, gmix)
        xn_ref[...] = xn.T.astype(BF)
        z = z_ref[...].astype(F32)
        z2 = z * z
        t = jnp.tanh(GELU_C0 * (z + GELU_C1 * z2 * z))
        zg = 0.5 * z * (1.0 + t)
        gp = 0.5 * (1.0 + t) + 0.5 * z * (1.0 - t * t) * (GELU_C0 * (1.0 + 3.0 * GELU_C1 * z2))
        u = zg[:, :AW]
        gvv = gv_ref[...]
        vn, vhat, rv = _rms_fwd(zg[:, AW:], gvv)
        vn = vn.astype(BF)
        dh = dh_ref[...]
        dhb = dh.astype(BF)
        dhb_ref[...] = dhb
        dgated = _mm_nt(dhb, wout_ref[...])
        tril, triu = _tri_masks()
        for h in range(GROUPS):
            wm = jnp.where(tril, ws_ref[h], jnp.zeros((), BF))
            wmt = jnp.where(triu, wst_ref[h], jnp.zeros((), BF))
            for c in range(nc):
                rs, cs = slice(c * CHUNK, (c + 1) * CHUNK), slice(h * CHUNK, (h + 1) * CHUNK)
                vnb = vn[rs, cs]
                s = _mm(wm, vnb) + bsb_ref[h]
                gated_ref[rs, cs] = (u[rs, cs] * s).astype(BF)
                dgt = dgated[rs, cs]
                du_scr[rs, cs] = dgt * s
                ds = dgt * u[rs, cs]
                dsb = ds.astype(BF)
                dws_ref[h] += jnp.where(tril, _mm_nt(dsb, vnb), 0.0)
                dbs_ref[h] += ds
                dvn_scr[rs, cs] = _mm(wmt, dsb)
        dvn = dvn_scr[...]
        dv, dgv = _rms_bwd(dvn, vhat, rv, gvv)
        dgv_ref[...] += dgv
        dz_ref[:, :AW] = (du_scr[...] * gp[:, :AW]).astype(BF)
        dz_ref[:, AW:] = (dv * gp[:, AW:]).astype(BF)
        dxn = _mm_nt(dz_ref[...], win_ref[...])
        dxr, dg = _rms_bwd(dxn, xhat, r, gmix)
        dg_ref[...] += dg
        dx_ref[...] = dh + dxr

        @pl.when(i == nt - 1)
        def _():
            for h in range(GROUPS):
                dbs_ref[h] = jnp.broadcast_to(jnp.sum(dbs_ref[h], axis=-1, keepdims=True), (CHUNK, CHUNK))

    gshape = (GROUPS, CHUNK, CHUNK)
    return pl.pallas_call(
        body, name="mixa_bwd", grid=(nt,),
        in_specs=[_row_spec(tm, D), _row_spec(tm, D), _row_spec(tm, 2 * AW), _const_spec((1, D)),
                  _const_spec((D, 2 * AW)), _const_spec((1, AW)), _const_spec(gshape), _const_spec(gshape),
                  _const_spec(gshape), _const_spec((AW, D))],
        out_specs=[_row_spec(tm, D), pl.BlockSpec((D, tm), lambda i: (0, i)), _row_spec(tm, 2 * AW), _row_spec(tm, AW),
                   _row_spec(tm, D), _const_spec(gshape), _const_spec(gshape), _const_spec((1, AW)), _const_spec((1, D))],
        out_shape=[jax.ShapeDtypeStruct((S, D), F32), jax.ShapeDtypeStruct((D, S), BF),
                   jax.ShapeDtypeStruct((S, 2 * AW), BF), jax.ShapeDtypeStruct((S, AW), BF),
                   jax.ShapeDtypeStruct((S, D), BF), jax.ShapeDtypeStruct(gshape, F32),
                   jax.ShapeDtypeStruct(gshape, F32), jax.ShapeDtypeStruct((1, AW), F32),
                   jax.ShapeDtypeStruct((1, D), F32)],
        scratch_shapes=[pltpu.VMEM((tm, AW), F32), pltpu.VMEM((tm, AW), F32)],
        compiler_params=_params(VMEM_BIG),
    )(dh, x, z, g, win, gv, ws, wst, bsb, wout)


def _conv_fwd(h, prev8, w, b, rows):
    hm1 = jnp.where(rows == 0, prev8[7:8], pltpu.roll(h, 1, 0))
    hm2 = jnp.where(rows == 0, prev8[6:7], jnp.where(rows == 1, prev8[7:8], pltpu.roll(h, 2, 0)))
    return b + w[0:1] * hm2 + w[1:2] * hm1 + w[2:3] * h, hm1, hm2


def _ffn_fwd(h, g, wup, cw, cb, wdown, tm):
    S = h.shape[0]

    def body(h_ref, g_ref, wup_ref, cw_ref, cb_ref, wdown_ref, o_ref, hh_ref, carry):
        i = pl.program_id(0)

        @pl.when(i == 0)
        def _():
            carry[...] = jnp.zeros_like(carry)

        x = h_ref[...]
        xn, _, _ = _rms_fwd(x, g_ref[...])
        xn = xn.astype(BF)
        rows = lax.broadcasted_iota(jnp.int32, (tm, 1), 0)
        acc = x
        for d in range(N_FF_BLK):
            cs = []
            for blk in (d, N_FF_BLK + d):
                hb = _mm(xn, wup_ref[blk])
                hh_ref[blk] = hb.astype(BF)
                c, _, _ = _conv_fwd(hb, carry[blk], cw_ref[blk], cb_ref[blk], rows)
                carry[blk] = hb[tm - 8:, :]
                cs.append(c)
            act = cs[0] * _sigmoid(cs[0]) * cs[1]
            acc = acc + _mm(act.astype(BF), wdown_ref[d * FF_BLK:(d + 1) * FF_BLK, :])
        o_ref[...] = acc

    return pl.pallas_call(
        body, name="ffn_fwd", grid=(S // tm,),
        in_specs=[_row_spec(tm, D), _const_spec((1, D)), _const_spec((N_DEV, D, FF_BLK), True),
                  _const_spec((N_DEV, 3, FF_BLK)), _const_spec((N_DEV, 1, FF_BLK)), _const_spec((D_FF, D), True)],
        out_specs=[_row_spec(tm, D), pl.BlockSpec((N_DEV, tm, FF_BLK), lambda i: (0, i, 0))],
        out_shape=[jax.ShapeDtypeStruct((S, D), F32), jax.ShapeDtypeStruct((N_DEV, S, FF_BLK), BF)],
        scratch_shapes=[pltpu.VMEM((N_DEV, 8, FF_BLK), F32)],
        compiler_params=_params(VMEM_BIG),
    )(h, g, wup, cw, cb, wdown)


def _ffn_bwd(dh, h, hh, g, wup, cw, cb, wdown, tm):
    S = h.shape[0]
    nt = S // tm
    halo = 16
    per = tm // halo

    def rev(i):
        return (nt - 1 - i, 0)

    def body(dh_ref, h_ref, hh_ref, halo_ref, g_ref, wup_ref, cw_ref, cb_ref, wdown_ref,
             dx_ref, xn_ref, dhh_ref, act_ref, dhb_ref, dcw_ref, dcb_ref, dg_ref, carry):
        i = pl.program_id(0)

        @pl.when(i == 0)
        def _():
            carry[...] = jnp.zeros_like(carry)
            dcw_ref[...] = jnp.zeros_like(dcw_ref)
            dcb_ref[...] = jnp.zeros_like(dcb_ref)
            dg_ref[...] = jnp.zeros_like(dg_ref)

        first_tile = i == nt - 1
        x = h_ref[...]
        gain = g_ref[...]
        xn, xhat, r = _rms_fwd(x, gain)
        xn_ref[...] = xn.T.astype(BF)
        dh = dh_ref[...]
        dhb = dh.astype(BF)
        dhb_ref[...] = dhb
        rows = lax.broadcasted_iota(jnp.int32, (tm, 1), 0)
        dxn = jnp.zeros((tm, D), F32)
        for d in range(N_FF_BLK):
            blks = (d, N_FF_BLK + d)
            hs, cs, hm1s, hm2s = [], [], [], []
            for blk in blks:
                hb = hh_ref[blk].astype(F32)
                prev8 = jnp.where(first_tile, 0.0, halo_ref[blk][halo - 8:, :].astype(F32))
                c, hm1, hm2 = _conv_fwd(hb, prev8, cw_ref[blk], cb_ref[blk], rows)
                hs.append(hb), cs.append(c), hm1s.append(hm1), hm2s.append(hm2)
            cg, cu = cs
            sg = _sigmoid(cg)
            sil = cg * sg
            act_ref[d] = (sil * cu).astype(BF)
            dact = _mm_nt(dhb, wdown_ref[d * FF_BLK:(d + 1) * FF_BLK, :])
            dcs = (dact * cu * (sg * (1.0 + cg * (1.0 - sg))), dact * sil)
            for blk, dc, hb, hm1, hm2 in zip(blks, dcs, hs, hm1s, hm2s):
                dcb_ref[blk] += jnp.sum(dc, axis=0, keepdims=True)
                dcw_ref[blk, 0:1, :] += jnp.sum(dc * hm2, axis=0, keepdims=True)
                dcw_ref[blk, 1:2, :] += jnp.sum(dc * hm1, axis=0, keepdims=True)
                dcw_ref[blk, 2:3, :] += jnp.sum(dc * hb, axis=0, keepdims=True)
                nxt = carry[blk]
                dp1 = jnp.where(rows == tm - 1, nxt[0:1], pltpu.roll(dc, tm - 1, 0))
                dp2 = jnp.where(rows == tm - 1, nxt[1:2], jnp.where(rows == tm - 2, nxt[0:1], pltpu.roll(dc, tm - 2, 0)))
                w = cw_ref[blk]
                dhh = (w[2:3] * dc + w[1:2] * dp1 + w[0:1] * dp2).astype(BF)
                carry[blk] = dc[0:8, :]
                dhh_ref[blk] = dhh
                dxn = dxn + _mm_nt(dhh, wup_ref[blk])
        dxr, dg = _rms_bwd(dxn, xhat, r, gain)
        dg_ref[...] += dg
        dx_ref[...] = dh + dxr

    blk3 = lambda i: (0, nt - 1 - i, 0)
    return pl.pallas_call(
        body, name="ffn_bwd", grid=(nt,),
        in_specs=[pl.BlockSpec((tm, D), rev), pl.BlockSpec((tm, D), rev),
                  pl.BlockSpec((N_DEV, tm, FF_BLK), blk3),
                  pl.BlockSpec((N_DEV, halo, FF_BLK), lambda i: (0, jnp.maximum((nt - 1 - i) * per - 1, 0), 0)),
                  _const_spec((1, D)), _const_spec((N_DEV, D, FF_BLK), True), _const_spec((N_DEV, 3, FF_BLK)),
                  _const_spec((N_DEV, 1, FF_BLK)), _const_spec((D_FF, D), True)],
        out_specs=[pl.BlockSpec((tm, D), rev), pl.BlockSpec((D, tm), lambda i: (0, nt - 1 - i)),
                   pl.BlockSpec((N_DEV, tm, FF_BLK), blk3), pl.BlockSpec((N_FF_BLK, tm, FF_BLK), blk3),
                   pl.BlockSpec((tm, D), rev), _const_spec((N_DEV, 3, FF_BLK)), _const_spec((N_DEV, 1, FF_BLK)),
                   _const_spec((1, D))],
        out_shape=[jax.ShapeDtypeStruct((S, D), F32), jax.ShapeDtypeStruct((D, S), BF),
                   jax.ShapeDtypeStruct((N_DEV, S, FF_BLK), BF), jax.ShapeDtypeStruct((N_FF_BLK, S, FF_BLK), BF),
                   jax.ShapeDtypeStruct((S, D), BF), jax.ShapeDtypeStruct((N_DEV, 3, FF_BLK), F32),
                   jax.ShapeDtypeStruct((N_DEV, 1, FF_BLK), F32), jax.ShapeDtypeStruct((1, D), F32)],
        scratch_shapes=[pltpu.VMEM((N_DEV, 8, FF_BLK), F32)],
        compiler_params=_params(VMEM_BIG),
    )(dh, h, hh, hh, g, wup, cw, cb, wdown)


def _ple_fwd(h, g, wgate, bgate, p, wple, tm):
    S = h.shape[0]

    def body(h_ref, g_ref, wg_ref, bg_ref, p_ref, wp_ref, o_ref):
        x = h_ref[...]
        xn, _, _ = _rms_fwd(x, g_ref[...])
        gate = _sigmoid(_mm(xn.astype(BF), wg_ref[...]) + bg_ref[...])
        o_ref[...] = x + _mm(p_ref[...], wp_ref[...]) * gate

    return pl.pallas_call(
        body, name="ple_fwd", grid=(S // tm,),
        in_specs=[_row_spec(tm, D), _const_spec((1, D)), _const_spec((D, D)), _const_spec((1, D)),
                  _row_spec(tm, PLE), _const_spec((PLE, D))],
        out_specs=_row_spec(tm, D), out_shape=jax.ShapeDtypeStruct((S, D), F32),
        compiler_params=_params(),
    )(h, g, wgate, bgate, p, wple)


def _ple_bwd(dh, h, g, wgate, bgate, p, wple, tm):
    S = h.shape[0]

    def body(dh_ref, h_ref, g_ref, wg_ref, bg_ref, p_ref, wp_ref, dx_ref, de_ref, xn_ref, dpre_ref, dbg_ref, dg_ref):
        i = pl.program_id(0)

        @pl.when(i == 0)
        def _():
            dbg_ref[...] = jnp.zeros_like(dbg_ref)
            dg_ref[...] = jnp.zeros_like(dg_ref)

        x = h_ref[...]
        gain = g_ref[...]
        xn, xhat, r = _rms_fwd(x, gain)
        xnb = xn.astype(BF)
        xn_ref[...] = xnb
        gate = _sigmoid(_mm(xnb, wg_ref[...]) + bg_ref[...])
        e = _mm(p_ref[...], wp_ref[...])
        dh = dh_ref[...]
        de_ref[...] = (dh * gate).astype(BF)
        dpre = dh * e * gate * (1.0 - gate)
        dpreb = dpre.astype(BF)
        dpre_ref[...] = dpreb
        dbg_ref[...] += jnp.sum(dpre, axis=0, keepdims=True)
        dxr, dg = _rms_bwd(_mm_nt(dpreb, wg_ref[...]), xhat, r, gain)
        dg_ref[...] += dg
        dx_ref[...] = dh + dxr

    return pl.pallas_call(
        body, name="ple_bwd", grid=(S // tm,),
        in_specs=[_row_spec(tm, D), _row_spec(tm, D), _const_spec((1, D)), _const_spec((D, D)), _const_spec((1, D)),
                  _row_spec(tm, PLE), _const_spec((PLE, D))],
        out_specs=[_row_spec(tm, D), _row_spec(tm, D), _row_spec(tm, D), _row_spec(tm, D), _const_spec((1, D)),
                   _const_spec((1, D))],
        out_shape=[jax.ShapeDtypeStruct((S, D), F32), jax.ShapeDtypeStruct((S, D), BF),
                   jax.ShapeDtypeStruct((S, D), BF), jax.ShapeDtypeStruct((S, D), BF),
                   jax.ShapeDtypeStruct((1, D), F32), jax.ShapeDtypeStruct((1, D), F32)],
        compiler_params=_params(),
    )(dh, h, g, wgate, bgate, p, wple)


def _qkv_fwd(h, gq, wq, gkv, wkv, tm):
    S = h.shape[0]
    nkv = wkv.shape[1]

    def body(h_ref, gq_ref, wq_ref, gkv_ref, wkv_ref, q_ref, kv_ref):
        x = h_ref[...]
        _, xhat, _ = _rms_fwd(x, gq_ref[...])
        q_ref[...] = _mm((xhat * gq_ref[...]).astype(BF), wq_ref[...]).astype(BF)
        kv_ref[...] = _mm((xhat * gkv_ref[...]).astype(BF), wkv_ref[...]).astype(BF)

    return pl.pallas_call(
        body, name="qkv_fwd", grid=(S // tm,),
        in_specs=[_row_spec(tm, D), _const_spec((1, D)), _const_spec((D, D)), _const_spec((1, D)),
                  _const_spec((D, nkv))],
        out_specs=[_row_spec(tm, D), _row_spec(tm, nkv)],
        out_shape=[jax.ShapeDtypeStruct((S, D), BF), jax.ShapeDtypeStruct((S, nkv), BF)],
        compiler_params=_params(),
    )(h, gq, wq, gkv, wkv)


def _qkv_bwd(dh, h, dq, dkv, gq, wq, gkv, wkv, tm):
    S = h.shape[0]
    nkv = wkv.shape[1]

    def body(dh_ref, h_ref, dq_ref, dkv_ref, gq_ref, wq_ref, gkv_ref, wkv_ref,
             dx_ref, xq_ref, xkv_ref, dgq_ref, dgkv_ref):
        i = pl.program_id(0)

        @pl.when(i == 0)
        def _():
            dgq_ref[...] = jnp.zeros_like(dgq_ref)
            dgkv_ref[...] = jnp.zeros_like(dgkv_ref)

        x = h_ref[...]
        gq_, gkv_ = gq_ref[...], gkv_ref[...]
        _, xhat, r = _rms_fwd(x, gq_)
        xq_ref[...] = (xhat * gq_).astype(BF)
        xkv_ref[...] = (xhat * gkv_).astype(BF)
        d1, dg1 = _rms_bwd(_mm_nt(dq_ref[...], wq_ref[...]), xhat, r, gq_)
        d2, dg2 = _rms_bwd(_mm_nt(dkv_ref[...], wkv_ref[...]), xhat, r, gkv_)
        dgq_ref[...] += dg1
        dgkv_ref[...] += dg2
        dx_ref[...] = dh_ref[...] + d1 + d2

    return pl.pallas_call(
        body, name="qkv_bwd", grid=(S // tm,),
        in_specs=[_row_spec(tm, D), _row_spec(tm, D), _row_spec(tm, D), _row_spec(tm, nkv), _const_spec((1, D)),
                  _const_spec((D, D)), _const_spec((1, D)), _const_spec((D, nkv))],
        out_specs=[_row_spec(tm, D), _row_spec(tm, D), _row_spec(tm, D), _const_spec((1, D)), _const_spec((1, D))],
        out_shape=[jax.ShapeDtypeStruct((S, D), F32), jax.ShapeDtypeStruct((S, D), BF),
                   jax.ShapeDtypeStruct((S, D), BF), jax.ShapeDtypeStruct((1, D), F32),
                   jax.ShapeDtypeStruct((1, D), F32)],
        compiler_params=_params(),
    )(dh, h, dq, dkv, gq, wq, gkv, wkv)


def _oproj_fwd(a, w, res, tm):
    S = a.shape[0]

    def body(a_ref, w_ref, r_ref, o_ref):
        o_ref[...] = r_ref[...] + _mm(a_ref[...], w_ref[...])

    return pl.pallas_call(
        body, name="oproj_fwd", grid=(S // tm,),
        in_specs=[_row_spec(tm, D), _const_spec((D, D)), _row_spec(tm, D)],
        out_specs=_row_spec(tm, D), out_shape=jax.ShapeDtypeStruct((S, D), F32),
        compiler_params=_params(),
    )(a, w, res)


def _oproj_bwd(dh, w, tm):
    S = dh.shape[0]

    def body(dh_ref, w_ref, da_ref, dhb_ref):
        dhb = dh_ref[...].astype(BF)
        dhb_ref[...] = dhb
        da_ref[...] = _mm_nt(dhb, w_ref[...]).astype(BF)

    return pl.pallas_call(
        body, name="oproj_bwd", grid=(S // tm,),
        in_specs=[_row_spec(tm, D), _const_spec((D, D))],
        out_specs=[_row_spec(tm, D), _row_spec(tm, D)],
        out_shape=[jax.ShapeDtypeStruct((S, D), BF), jax.ShapeDtypeStruct((S, D), BF)],
        compiler_params=_params(),
    )(dh, w)


def _alibi_slope(hq):
    return float(np.float32(2.0 ** (-8.0 * (hq + 1) / N_Q)))


def _attn_consts():
    W = N_KV * HD
    r = np.zeros((W, N_KV * W), np.float32)
    for kh in range(N_KV):
        for c in range(W):
            r[kh * HD + c % HD, kh * W + c] = 1.0
    qi = np.arange(BLOCK)[:, None]
    kj = np.arange(2 * BLOCK)[None, :]
    dist = qi + BLOCK - kj
    band = (dist >= 0) & (dist < BLOCK)
    bias = np.full((2, N_KV, GQA * BLOCK, 2 * BLOCK), NEG, np.float32)
    for later in (0, 1):
        valid = band & ((kj >= BLOCK) | bool(later))
        for kh in range(N_KV):
            for g in range(GQA):
                slope = np.float32(_alibi_slope(kh * GQA + g))
                bias[later, kh, g * BLOCK:(g + 1) * BLOCK] = np.where(valid, -slope * dist.astype(np.float32), NEG)
    return jnp.asarray(r, BF), jnp.asarray(bias)


def _head_masks():
    W = N_KV * HD
    rq = lax.broadcasted_iota(jnp.int32, (GQA * BLOCK, W), 0) >> 7
    cq = lax.broadcasted_iota(jnp.int32, (GQA * BLOCK, W), 1) >> 6
    rk = lax.broadcasted_iota(jnp.int32, (GQA * 2 * BLOCK, W), 0) >> 8
    ck = lax.broadcasted_iota(jnp.int32, (GQA * 2 * BLOCK, W), 1) >> 6
    return rq == cq, rk == ck


def _sink_col(sink_ref, kh):
    rb = lax.broadcasted_iota(jnp.int32, (GQA * BLOCK, 1), 0) >> 7
    col = jnp.full((GQA * BLOCK, 1), sink_ref[kh * GQA + GQA - 1], F32)
    for g in range(GQA - 1):
        col = jnp.where(rb == g, sink_ref[kh * GQA + g], col)
    return col


def _stack4(a):
    return jnp.concatenate([a] * GQA, axis=0)


def _rows_to_lanes(a):
    return jnp.concatenate([a[g * BLOCK:(g + 1) * BLOCK] for g in range(GQA)], axis=1)


def _lanes_to_rows(a, n):
    return jnp.concatenate([a[:, g * n:(g + 1) * n] for g in range(GQA)], axis=0)


def _scores(qg, k4, bias, sink_col, maskq):
    qs = jnp.where(maskq, _stack4(qg), jnp.zeros((), BF))
    s = _mm_nt(qs, k4) * (HD ** -0.5) + bias
    m = jnp.maximum(jnp.max(s, axis=-1, keepdims=True), sink_col)
    return qs, jnp.exp(s - m), jnp.exp(sink_col - m)


def _bands(kvc_ref, kvp_ref, rep_ref):
    W = N_KV * HD
    kvp, kvc = kvp_ref[...], kvc_ref[...]
    kband = jnp.concatenate([kvp[:, :W], kvc[:, :W]], axis=0)
    vband = jnp.concatenate([kvp[:, W:], kvc[:, W:]], axis=0)
    return _mm(kband, rep_ref[...]).astype(BF), _mm(vband, rep_ref[...]).astype(BF)


def _attn_fwd(q, kv, sinks, rep, bias):
    S = q.shape[0]
    nb = S // BLOCK
    W = N_KV * HD

    def body(sink_ref, q_ref, kvc_ref, kvp_ref, rep_ref, bias_ref, o_ref):
        maskq, maskk = _head_masks()
        ones_bd = maskk.astype(BF)
        lane_head = lax.broadcasted_iota(jnp.int32, (BLOCK, W), 1) >> 6
        k4all, v4all = _bands(kvc_ref, kvp_ref, rep_ref)
        for kh in range(N_KV):
            cols = slice(kh * W, (kh + 1) * W)
            _, e, es = _scores(q_ref[:, cols], k4all[:, cols], bias_ref[kh], _sink_col(sink_ref, kh), maskq)
            vbd = jnp.concatenate([jnp.where(maskk, _stack4(v4all[:, cols]), jnp.zeros((), BF)), ones_bd], axis=1)
            nd = _mm(_rows_to_lanes(e.astype(BF)), vbd)
            es_l = jnp.broadcast_to(es[(GQA - 1) * BLOCK:], (BLOCK, W))
            for g in range(GQA - 1):
                es_l = jnp.where(lane_head == g, es[g * BLOCK:(g + 1) * BLOCK], es_l)
            o_ref[:, cols] = (nd[:, :W] / (nd[:, W:] + es_l)).astype(BF)

    return pl.pallas_call(
        body, name="attn_fwd", grid=(nb,),
        in_specs=[pl.BlockSpec(memory_space=pltpu.SMEM), _row_spec(BLOCK, D), _row_spec(BLOCK, 2 * W),
                  pl.BlockSpec((BLOCK, 2 * W), lambda i: (jnp.maximum(i - 1, 0), 0)), _const_spec((W, N_KV * W)),
                  pl.BlockSpec((None, N_KV, GQA * BLOCK, 2 * BLOCK), lambda i: (jnp.minimum(i, 1), 0, 0, 0))],
        out_specs=_row_spec(BLOCK, D), out_shape=jax.ShapeDtypeStruct((S, D), BF),
        compiler_params=_params(),
    )(sinks, q, kv, kv, rep, bias)


def _attn_bwd(q, kv, att, do, sinks, rep, bias):
    S = q.shape[0]
    nb = S // BLOCK
    W = N_KV * HD

    def qmap(i):
        return (jnp.minimum(i, nb - 1), 0)

    def body(sink_ref, q_ref, kvc_ref, kvp_ref, att_ref, do_ref, rep_ref, bias_ref,
             dq_ref, dkv_ref, dsink_ref, carry, sink_acc):
        i = pl.program_id(0)

        @pl.when(i == 0)
        def _():
            sink_acc[...] = jnp.zeros_like(sink_acc)
            carry[...] = jnp.zeros_like(carry)

        @pl.when(i < nb)
        def _():
            maskq, maskk = _head_masks()
            ones_k = jnp.ones((2 * BLOCK, BLOCK), BF)
            head_sum = ((lax.broadcasted_iota(jnp.int32, (W, GQA * BLOCK), 0) >> 6)
                        == (lax.broadcasted_iota(jnp.int32, (W, GQA * BLOCK), 1) >> 7)).astype(BF)
            k4all, v4all = _bands(kvc_ref, kvp_ref, rep_ref)
            dk4s, dv4s = [], []
            for kh in range(N_KV):
                cols = slice(kh * W, (kh + 1) * W)
                k4, v4 = k4all[:, cols], v4all[:, cols]
                qs, e, es = _scores(q_ref[:, cols], k4, bias_ref[kh], _sink_col(sink_ref, kh), maskq)
                inv = 1.0 / (_mm(e.astype(BF), ones_k) + es)
                p = e * jnp.concatenate([inv, inv], axis=1)
                dog = do_ref[:, cols]
                dos = jnp.where(maskq, _stack4(dog), jnp.zeros((), BF))
                dp = _mm_nt(dos, v4)
                prod = (dog.astype(F32) * att_ref[:, cols].astype(F32)).astype(BF)
                delta = _lanes_to_rows(_mm(prod, head_sum), BLOCK)
                ds = (p * (dp - jnp.concatenate([delta, delta], axis=1))).astype(BF)
                sink_acc[kh] += es * inv * delta
                kbd = jnp.where(maskk, _stack4(k4), jnp.zeros((), BF))
                dq_ref[:, cols] = (_mm(_rows_to_lanes(ds), kbd) * (HD ** -0.5)).astype(BF)
                dk4s.append((_mm_tn(ds, qs) * (HD ** -0.5)).astype(BF))
                dv4s.append(_mm_tn(p.astype(BF), dos).astype(BF))
            dkband = _mm_nt(jnp.concatenate(dk4s, axis=1), rep_ref[...])
            dvband = _mm_nt(jnp.concatenate(dv4s, axis=1), rep_ref[...])
            dkv_ref[:, :W] = (carry[:, :W] + dkband[:BLOCK]).astype(BF)
            dkv_ref[:, W:] = (carry[:, W:] + dvband[:BLOCK]).astype(BF)
            carry[:, :W] = dkband[BLOCK:]
            carry[:, W:] = dvband[BLOCK:]

        @pl.when(i == nb)
        def _():
            dkv_ref[...] = carry[...].astype(BF)
            for hq in range(N_Q):
                kh, g = divmod(hq, GQA)
                dsink_ref[hq:hq + 1, :] = -jnp.sum(sink_acc[kh, g * BLOCK:(g + 1) * BLOCK, :], axis=0, keepdims=True)

    return pl.pallas_call(
        body, name="attn_bwd", grid=(nb + 1,),
        in_specs=[pl.BlockSpec(memory_space=pltpu.SMEM), pl.BlockSpec((BLOCK, D), qmap),
                  pl.BlockSpec((BLOCK, 2 * W), qmap),
                  pl.BlockSpec((BLOCK, 2 * W), lambda i: (jnp.maximum(jnp.minimum(i, nb - 1) - 1, 0), 0)),
                  pl.BlockSpec((BLOCK, D), qmap), pl.BlockSpec((BLOCK, D), qmap), _const_spec((W, N_KV * W)),
                  pl.BlockSpec((None, N_KV, GQA * BLOCK, 2 * BLOCK), lambda i: (jnp.minimum(i, 1), 0, 0, 0))],
        out_specs=[pl.BlockSpec((BLOCK, D), qmap),
                   pl.BlockSpec((BLOCK, 2 * W), lambda i: (jnp.maximum(i - 1, 0), 0)),
                   _const_spec((N_Q, BLOCK))],
        out_shape=[jax.ShapeDtypeStruct((S, D), BF), jax.ShapeDtypeStruct((S, 2 * W), BF),
                   jax.ShapeDtypeStruct((N_Q, BLOCK), F32)],
        scratch_shapes=[pltpu.VMEM((BLOCK, 2 * W), F32), pltpu.VMEM((N_KV, GQA * BLOCK, BLOCK), F32)],
        compiler_params=_params(),
    )(sinks, q, kv, kv, att, do, rep, bias)


def _head(h, g, target, tm):
    S = h.shape[0]

    def body(h_ref, g_ref, t_ref, dh_ref, loss_ref, dg_ref):
        i = pl.program_id(0)

        @pl.when(i == 0)
        def _():
            loss_ref[...] = jnp.zeros_like(loss_ref)
            dg_ref[...] = jnp.zeros_like(dg_ref)

        gain = g_ref[...]
        y, xhat, r = _rms_fwd(h_ref[...], gain)
        err = y - t_ref[...]
        tile = 0.5 * jnp.sum(jnp.sum(err * err, axis=-1, keepdims=True) * (1.0 / D), axis=0, keepdims=True)
        loss_ref[...] += jnp.broadcast_to(tile, loss_ref.shape)
        dx, dg = _rms_bwd(err * (1.0 / D), xhat, r, gain)
        dg_ref[...] += dg
        dh_ref[...] = dx

    return pl.pallas_call(
        body, name="loss_head", grid=(S // tm,),
        in_specs=[_row_spec(tm, D), _const_spec((1, D)), _row_spec(tm, D)],
        out_specs=[_row_spec(tm, D), _const_spec((8, 128)), _const_spec((1, D))],
        out_shape=[jax.ShapeDtypeStruct((S, D), F32), jax.ShapeDtypeStruct((8, 128), F32),
                   jax.ShapeDtypeStruct((1, D), F32)],
        compiler_params=_params(),
    )(h, g, target)


def _wgrad(a, b, name, *, tn, out_blocked=False, a_transposed=False, ts=1024):
    a_blocked, b_blocked = a.ndim == 3, b.ndim == 3
    na = a.shape[0] if a_blocked else 1
    S = b.shape[-2]
    K = a.shape[0] if a_transposed else a.shape[-1]
    nbk = b.shape[0] if b_blocked else b.shape[1] // tn
    ts = min(ts, S)
    ns = S // ts
    assert not (a_blocked and (out_blocked or a_transposed))

    def body(a_ref, b_ref, o_ref, acc):
        s = pl.program_id(2)

        @pl.when(s == 0)
        def _():
            acc[...] = jnp.zeros_like(acc)

        if a_transposed:
            acc[...] += _mm(a_ref[...], b_ref[...])
        else:
            acc[...] += _mm_tn(a_ref[...], b_ref[...])

        @pl.when(s == ns - 1)
        def _():
            o_ref[...] = acc[...].astype(o_ref.dtype)

    if a_transposed:
        a_spec = pl.BlockSpec((K, ts), lambda i, j, s: (0, s))
    elif a_blocked:
        a_spec = pl.BlockSpec((None, ts, K), lambda i, j, s: (i, s, 0))
    else:
        a_spec = pl.BlockSpec((ts, K), lambda i, j, s: (s, 0))
    b_spec = (pl.BlockSpec((None, ts, tn), lambda i, j, s: (j, s, 0)) if b_blocked
              else pl.BlockSpec((ts, tn), lambda i, j, s: (s, j)))
    if out_blocked:
        o_spec = pl.BlockSpec((None, K, tn), lambda i, j, s: (j, 0, 0))
        o_shape = jax.ShapeDtypeStruct((nbk, K, tn), BF)
    else:
        o_spec = pl.BlockSpec((K, tn), lambda i, j, s: (i, j))
        o_shape = jax.ShapeDtypeStruct((na * K, nbk * tn), BF)
    return pl.pallas_call(
        body, name=name, grid=(na, nbk, ns), in_specs=[a_spec, b_spec], out_specs=o_spec, out_shape=o_shape,
        scratch_shapes=[pltpu.VMEM((K, tn), F32)],
        compiler_params=pltpu.CompilerParams(dimension_semantics=("arbitrary", "arbitrary", "arbitrary"),
                                             vmem_limit_bytes=VMEM_BIG),
    )(a, b)


def _adamw_math(w, g, m, v):
    m = B1 * m + (1.0 - B1) * g
    v = B2 * v + (1.0 - B2) * (g * g)
    m_hat = m / (1.0 - B1 ** STEP)
    v_hat = v / (1.0 - B2 ** STEP)
    return -LR * (m_hat / (jnp.sqrt(v_hat) + ADAM_EPS) + WD * w), m, v


def _adamw(parts, w, m, v, name, tr):
    R = w.shape[0]
    tr = min(tr, R)

    def body(p_ref, w_ref, m_ref, v_ref, g_ref, d_ref, nm_ref, nv_ref):
        g = p_ref[0].astype(F32)
        for s in range(1, N_DEV):
            g = g + p_ref[s].astype(F32)
        g_ref[...] = g
        d_ref[...], nm_ref[...], nv_ref[...] = _adamw_math(w_ref[...], g, m_ref[...], v_ref[...])

    rs = _row_spec(tr, ROW)
    return pl.pallas_call(
        body, name=name, grid=(R // tr,),
        in_specs=[pl.BlockSpec((N_DEV, tr, ROW), lambda i: (0, i, 0)), rs, rs, rs],
        out_specs=[rs, rs, rs, rs], out_shape=[jax.ShapeDtypeStruct((R, ROW), F32)] * 4,
        compiler_params=_params(),
    )(parts, w, m, v)


def _sum8(parts, name):
    R = parts.shape[1]

    def body(p_ref, o_ref):
        g = p_ref[0]
        for s in range(1, N_DEV):
            g = g + p_ref[s]
        o_ref[...] = g

    return pl.pallas_call(
        body, name=name, grid=(1,),
        in_specs=[pl.BlockSpec((N_DEV, R, ROW), lambda i: (0, 0, 0))],
        out_specs=pl.BlockSpec((R, ROW), lambda i: (0, 0)), out_shape=jax.ShapeDtypeStruct((R, ROW), F32),
        compiler_params=_params(),
    )(parts)


def _adamw_small(g, w, m, v, name):
    R = w.shape[0]

    def body(g_ref, w_ref, m_ref, v_ref, d_ref, nm_ref, nv_ref):
        d_ref[...], nm_ref[...], nv_ref[...] = _adamw_math(w_ref[...], g_ref[...], m_ref[...], v_ref[...])

    spec = pl.BlockSpec((R, ROW), lambda i: (0, 0))
    return pl.pallas_call(
        body, name=name, grid=(1,), in_specs=[spec] * 4, out_specs=[spec] * 3,
        out_shape=[jax.ShapeDtypeStruct((R, ROW), F32)] * 3, compiler_params=_params(),
    )(g, w, m, v)


def _mesh_pos():
    return lax.axis_index("x"), lax.axis_index("y"), lax.axis_index("c")


def _flip(v, bit):
    return 1 - v if bit else v


def _all_gather(xs, name):
    R, C = xs.shape
    MESH = pl.DeviceIdType.MESH

    def body(x_ref, out_ref, send_sems, recv_sems, local_sem):
        x, y, c = _mesh_pos()
        me, sibling = (x, y, c), (x, y, 1 - c)
        chips = [(1 - x, y), (x, 1 - y), (1 - x, 1 - y)]

        def rows(px, py, pc):
            return out_ref.at[4 * px + 2 * py + pc]

        def copy(k, block, to, src=None):
            return pltpu.make_async_remote_copy(
                src_ref=rows(*block) if src is None else src, dst_ref=rows(*block),
                send_sem=send_sems.at[k], recv_sem=recv_sems.at[k], device_id=to, device_id_type=MESH)

        mine = pltpu.make_async_copy(x_ref, rows(*me), local_sem)
        mine.start()
        first = [copy(0, me, sibling, src=x_ref)]
        first += [copy(1 + j, me, (*chip, c), src=x_ref) for j, chip in enumerate(chips)]
        for cp in first:
            cp.start()
        passed = [copy(4 + j, (*chip, c), sibling) for j, chip in enumerate(chips)]
        for j, chip in enumerate(chips):
            copy(1 + j, (*chip, c), me).wait_recv()
            passed[j].start()
        copy(0, sibling, me).wait_recv()
        for j, chip in enumerate(chips):
            copy(4 + j, (*chip, 1 - c), me).wait_recv()
        for cp in first + passed:
            cp.wait_send()
        mine.wait()

    return pl.pallas_call(
        body, name=name, out_shape=jax.ShapeDtypeStruct((N_DEV, R, C), xs.dtype),
        in_specs=[pl.BlockSpec(memory_space=pl.ANY)], out_specs=pl.BlockSpec(memory_space=pl.ANY),
        scratch_shapes=[pltpu.SemaphoreType.DMA((7,)), pltpu.SemaphoreType.DMA((7,)), pltpu.SemaphoreType.DMA(())],
    )(xs)


def _all_to_all(g, name):
    _, R, C = g.shape
    MESH = pl.DeviceIdType.MESH

    def body(g_ref, out_ref, send_sems, recv_sems, local_sem):
        x, y, c = _mesh_pos()
        me = 4 * x + 2 * y + c
        mine = pltpu.make_async_copy(g_ref.at[me], out_ref.at[me], local_sem)
        mine.start()
        copies = []
        for k in range(1, N_DEV):
            px, py, pc = _flip(x, k & 4), _flip(y, k & 2), _flip(c, k & 1)
            peer = 4 * px + 2 * py + pc
            copies.append((pltpu.make_async_remote_copy(
                src_ref=g_ref.at[peer], dst_ref=out_ref.at[me], send_sem=send_sems.at[k - 1],
                recv_sem=recv_sems.at[k - 1], device_id=(px, py, pc), device_id_type=MESH),
                pltpu.make_async_remote_copy(
                src_ref=g_ref.at[peer], dst_ref=out_ref.at[peer], send_sem=send_sems.at[k - 1],
                recv_sem=recv_sems.at[k - 1], device_id=(px, py, pc), device_id_type=MESH)))
        for send, _ in copies:
            send.start()
        for send, recv in copies:
            recv.wait_recv()
        for send, _ in copies:
            send.wait_send()
        mine.wait()

    return pl.pallas_call(
        body, name=name, out_shape=jax.ShapeDtypeStruct(g.shape, g.dtype),
        in_specs=[pl.BlockSpec(memory_space=pl.ANY)], out_specs=pl.BlockSpec(memory_space=pl.ANY),
        scratch_shapes=[pltpu.SemaphoreType.DMA((7,)), pltpu.SemaphoreType.DMA((7,)), pltpu.SemaphoreType.DMA(())],
    )(g)


def _exchange_start(gs, name):
    n = len(gs)
    MESH = pl.DeviceIdType.MESH

    def body(*refs):
        g_refs, land_refs = refs[:n], refs[n:2 * n]
        send_sems, recv_sems = refs[2 * n:3 * n], refs[3 * n:4 * n]
        token = refs[6 * n]
        x, y, c = _mesh_pos()
        me = 4 * x + 2 * y + c
        for a in range(n):
            for k in range(1, N_DEV):
                px, py, pc = _flip(x, k & 4), _flip(y, k & 2), _flip(c, k & 1)
                pltpu.make_async_remote_copy(
                    src_ref=g_refs[a].at[4 * px + 2 * py + pc], dst_ref=land_refs[a].at[me],
                    send_sem=send_sems[a].at[k - 1], recv_sem=recv_sems[a].at[k - 1],
                    device_id=(px, py, pc), device_id_type=MESH).start()
        token[...] = jnp.zeros_like(token)

    hbm = pl.BlockSpec(memory_space=pltpu.HBM)
    sem = pl.BlockSpec(memory_space=pltpu.SEMAPHORE)
    bufs = [pltpu.HBM(g.shape, g.dtype) for g in gs]
    outs = pl.pallas_call(
        body, name=name,
        out_shape=([pltpu.SemaphoreType.DMA((N_DEV - 1,))] * (2 * n) + bufs + bufs
                   + [jax.ShapeDtypeStruct((8, 128), F32)]),
        in_specs=[hbm] * (2 * n), out_specs=[sem] * (2 * n) + [hbm] * (2 * n) + [pl.BlockSpec(memory_space=pltpu.VMEM)],
        input_output_aliases={**{a: 2 * n + a for a in range(n)}, **{n + a: 3 * n + a for a in range(n)}},
        compiler_params=pltpu.CompilerParams(has_side_effects=pltpu.SideEffectType.DATAFLOW_SIDE_EFFECTING),
    )(*[pltpu.with_memory_space_constraint(g, pltpu.HBM) for g in gs],
      *[pltpu.with_memory_space_constraint(lax.empty(g.shape, g.dtype), pltpu.HBM) for g in gs])
    return outs[:n], outs[n:2 * n], outs[2 * n:3 * n], outs[3 * n:4 * n], outs[4 * n]


def _exchange_wait(started, after, name):
    send_sems, recv_sems, gs, lands, _ = started
    n = len(gs)
    MESH = pl.DeviceIdType.MESH

    def body(*refs):
        g_refs, land_refs = refs[:n], refs[n:2 * n]
        send, recv = refs[2 * n:3 * n], refs[3 * n:4 * n]
        land_out = refs[5 * n + 1:6 * n + 1]
        local_sem = refs[6 * n + 1]
        x, y, c = _mesh_pos()
        me = 4 * x + 2 * y + c
        for a in range(n):
            for k in range(1, N_DEV):
                px, py, pc = _flip(x, k & 4), _flip(y, k & 2), _flip(c, k & 1)
                peer = 4 * px + 2 * py + pc
                cp = pltpu.make_async_remote_copy(
                    src_ref=g_refs[a].at[peer], dst_ref=land_refs[a].at[peer], send_sem=send[a].at[k - 1],
                    recv_sem=recv[a].at[k - 1], device_id=(px, py, pc), device_id_type=MESH)
                cp.wait_send()
                cp.wait_recv()
            own = pltpu.make_async_copy(g_refs[a].at[me], land_out[a].at[me], local_sem)
            own.start()
            own.wait()

    hbm = pl.BlockSpec(memory_space=pltpu.HBM)
    sem = pl.BlockSpec(memory_space=pltpu.SEMAPHORE)
    bufs = [pltpu.HBM(g.shape, g.dtype) for g in gs]
    outs = pl.pallas_call(
        body, name=name, out_shape=bufs + bufs,
        in_specs=[hbm] * (2 * n) + [sem] * (2 * n) + [pl.BlockSpec(memory_space=pl.ANY)], out_specs=[hbm] * (2 * n),
        input_output_aliases={a: a for a in range(2 * n)},
        scratch_shapes=[pltpu.SemaphoreType.DMA(())],
        compiler_params=pltpu.CompilerParams(has_side_effects=pltpu.SideEffectType.DATAFLOW_SIDE_EFFECTING),
    )(*gs, *lands, *send_sems, *recv_sems, after)
    return outs[n:]


def _adamw_weight(parts, w, m, v, name):
    L, R, C = w.shape
    tr = max(t for t in range(16, min(R, 256) + 1, 16) if R % t == 0)
    nr = R // tr

    def body(*refs):
        p_refs = refs[:L]
        w_ref, m_ref, v_ref, g_ref, d_ref, nm_ref, nv_ref, gsum = refs[L:]
        layer = pl.program_id(0)
        for l in range(L):
            @pl.when(layer == l)
            def _(l=l):
                g = p_refs[l][0].astype(F32)
                for s in range(1, N_DEV):
                    g = g + p_refs[l][s].astype(F32)
                gsum[...] = g
        g = gsum[...]
        g_ref[...] = g
        d_ref[...], nm_ref[...], nv_ref[...] = _adamw_math(w_ref[...], g, m_ref[...], v_ref[...])

    def part_spec(l):
        return pl.BlockSpec((N_DEV, tr, C), lambda layer, i: (0, jnp.where(layer == l, i, jnp.where(layer < l, 0, nr - 1)), 0))

    ws = pl.BlockSpec((None, tr, C), lambda layer, i: (layer, i, 0))
    return pl.pallas_call(
        body, name=name, grid=(L, nr), in_specs=[part_spec(l) for l in range(L)] + [ws] * 3, out_specs=[ws] * 4,
        out_shape=[jax.ShapeDtypeStruct((L, R, C), F32)] * 4, scratch_shapes=[pltpu.VMEM((tr, C), F32)],
        compiler_params=pltpu.CompilerParams(dimension_semantics=("arbitrary", "arbitrary")),
    )(*parts, w, m, v)


def _seg_rows(n, align):
    rows = -(-n // ROW)
    return -(-rows // align) * align


def _pack(arrs, align, lead=()):
    parts = []
    for a in arrs:
        n = int(np.prod(a.shape[len(lead):]))
        rows = _seg_rows(n, align)
        flat = a.reshape(lead + (n,))
        flat = jnp.pad(flat, [(0, 0)] * len(lead) + [(0, rows * ROW - n)])
        parts.append(flat.reshape(lead + (rows, ROW)))
    return jnp.concatenate(parts, axis=len(lead))


def _unpack(buf, shapes, align, lead=()):
    out, r0 = [], 0
    for shp in shapes:
        n = int(np.prod(shp))
        rows = _seg_rows(n, align)
        seg = lax.slice_in_dim(buf, r0, r0 + rows, axis=len(lead))
        out.append(seg.reshape(lead + (rows * ROW,))[..., :n].reshape(lead + tuple(shp)))
        r0 += rows
    return out


def kernel(x, p, norm_mix, norm_ffn, norm_ple, norm_kv, norm_final, a_w_in, a_norm_v, a_w_s, a_b_s, a_w_out, w_kv, b_w_q, b_sinks, b_w_o, f_w_up, f_conv_w, f_conv_b, f_w_down, ple_w_in, ple_w_gate, ple_b_gate, loss_target, m_norm_mix, m_norm_ffn, m_norm_ple, m_norm_kv, m_norm_final, m_a_w_in, m_a_norm_v, m_a_w_s, m_a_b_s, m_a_w_out, m_w_kv, m_b_w_q, m_b_sinks, m_b_w_o, m_f_w_up, m_f_conv_w, m_f_conv_b, m_f_w_down, m_ple_w_in, m_ple_w_gate, m_ple_b_gate, v_norm_mix, v_norm_ffn, v_norm_ple, v_norm_kv, v_norm_final, v_a_w_in, v_a_norm_v, v_a_w_s, v_a_b_s, v_a_w_out, v_w_kv, v_b_w_q, v_b_sinks, v_b_w_o, v_f_w_up, v_f_conv_w, v_f_conv_b, v_f_w_down, v_ple_w_in, v_ple_w_gate, v_ple_b_gate):
    w = dict(norm_mix=norm_mix, norm_ffn=norm_ffn, norm_ple=norm_ple, norm_kv=norm_kv, norm_final=norm_final,
             a_w_in=a_w_in, a_norm_v=a_norm_v, a_w_s=a_w_s, a_b_s=a_b_s, a_w_out=a_w_out, w_kv=w_kv, b_w_q=b_w_q,
             b_sinks=b_sinks, b_w_o=b_w_o, f_w_up=f_w_up, f_conv_w=f_conv_w, f_conv_b=f_conv_b, f_w_down=f_w_down,
             ple_w_in=ple_w_in, ple_w_gate=ple_w_gate, ple_b_gate=ple_b_gate)
    mom_m = dict(norm_mix=m_norm_mix, norm_ffn=m_norm_ffn, norm_ple=m_norm_ple, norm_kv=m_norm_kv,
                 norm_final=m_norm_final, a_w_in=m_a_w_in, a_norm_v=m_a_norm_v, a_w_s=m_a_w_s, a_b_s=m_a_b_s,
                 a_w_out=m_a_w_out, w_kv=m_w_kv, b_w_q=m_b_w_q, b_sinks=m_b_sinks, b_w_o=m_b_w_o, f_w_up=m_f_w_up,
                 f_conv_w=m_f_conv_w, f_conv_b=m_f_conv_b, f_w_down=m_f_w_down, ple_w_in=m_ple_w_in,
                 ple_w_gate=m_ple_w_gate, ple_b_gate=m_ple_b_gate)
    mom_v = dict(norm_mix=v_norm_mix, norm_ffn=v_norm_ffn, norm_ple=v_norm_ple, norm_kv=v_norm_kv,
                 norm_final=v_norm_final, a_w_in=v_a_w_in, a_norm_v=v_a_norm_v, a_w_s=v_a_w_s, a_b_s=v_a_b_s,
                 a_w_out=v_a_w_out, w_kv=v_w_kv, b_w_q=v_b_w_q, b_sinks=v_b_sinks, b_w_o=v_b_w_o, f_w_up=v_f_w_up,
                 f_conv_w=v_f_conv_w, f_conv_b=v_f_conv_b, f_w_down=v_f_w_down, ple_w_in=v_ple_w_in,
                 ple_w_gate=v_ple_w_gate, ple_b_gate=v_ple_b_gate)
    S = x.shape[1]
    tm = min(256, S)
    me = 4 * lax.axis_index("x") + 2 * lax.axis_index("y") + lax.axis_index("c")

    big_shapes = [w[n].shape for n in _BIG]
    small_shapes = [w[n].shape for n in _SMALL_SHARDED]
    wbig = _unpack(_all_gather(_pack([w[n].astype(BF) for n in _BIG], 16), "gather_weights"),
                   big_shapes, 16, (N_DEV,))
    wsmall = _unpack(_all_gather(_pack([w[n] for n in _SMALL_SHARDED], 8), "gather_small"),
                     small_shapes, 8, (N_DEV,))
    gw = dict(zip(_BIG, wbig))
    gw.update(zip(_SMALL_SHARDED, wsmall))
    win = gw['a_w_in'][:, 0].transpose(1, 0, 2).reshape(D, 2 * AW)
    gv = gw['a_norm_v'].reshape(1, AW)
    wout = gw['a_w_out'].reshape(AW, D)
    wkv = gw['w_kv'].reshape(D, 2 * N_KV * HD)
    wq = gw['b_w_q'].reshape(D, D)
    wo = gw['b_w_o'].reshape(D, D)
    wup = [gw['f_w_up'][:, l] for l in range(2)]
    cw = [gw['f_conv_w'][:, l] for l in range(2)]
    cb = [w['f_conv_b'][l].reshape(N_DEV, 1, FF_BLK) for l in range(2)]
    wdown = [gw['f_w_down'][:, l].reshape(D_FF, D) for l in range(2)]
    wple = [gw['ple_w_in'][:, l].transpose(1, 0, 2).reshape(PLE, D) for l in range(2)]
    wgate = [gw['ple_w_gate'][:, l].reshape(D, D) for l in range(2)]
    ws = a_w_s[0].astype(BF)
    wst = jnp.swapaxes(ws, 1, 2)
    bsb = jnp.broadcast_to(a_b_s[0][:, :, None], (GROUPS, CHUNK, CHUNK))
    row = lambda a: a.reshape(1, -1)
    pb = [p[l, 0].astype(BF) for l in range(2)]
    sinks = b_sinks.reshape(N_Q)
    rep, bias = _attn_consts()
    x0 = x[0]

    h1, z = _mixa_fwd(x0, row(norm_mix[0]), win, gv, ws, bsb, wout, tm)
    h2, hh0 = _ffn_fwd(h1, row(norm_ffn[0]), wup[0], cw[0], cb[0], wdown[0], tm)
    h3 = _ple_fwd(h2, row(norm_ple[0]), wgate[0], row(ple_b_gate[0]), pb[0], wple[0], tm)
    q, kv = _qkv_fwd(h3, row(norm_mix[1]), wq, row(norm_kv), wkv, tm)
    att = _attn_fwd(q, kv, sinks, rep, bias)
    h4 = _oproj_fwd(att, wo, h3, tm)
    h5, hh1 = _ffn_fwd(h4, row(norm_ffn[1]), wup[1], cw[1], cb[1], wdown[1], tm)
    h6 = _ple_fwd(h5, row(norm_ple[1]), wgate[1], row(ple_b_gate[1]), pb[1], wple[1], tm)

    def col_blocks(g2d, n):
        return g2d.reshape(g2d.shape[0], N_DEV, n).transpose(1, 0, 2)

    def row_blocks(g2d):
        return g2d.reshape(N_DEV, g2d.shape[0] // N_DEV, g2d.shape[1])

    def after(tok, a):
        return a + tok[0:1, 0:1]

    def ple_grads(l, de, xp, dpre):
        return [col_blocks(_wgrad(pb[l], de, f"wgrad_ple_in{l}", tn=1024), D // N_DEV),
                row_blocks(_wgrad(xp, dpre, f"wgrad_gate{l}", tn=1024))]

    def ffn_grads(l, xf, dhh, act, dhb):
        return [_wgrad(xf, dhh, f"wgrad_up{l}", tn=FF_BLK, out_blocked=True, a_transposed=True),
                row_blocks(_wgrad(act, dhb, f"wgrad_down{l}", tn=1024))]

    dh6, loss_blk, d_norm_final = _head(h6, row(norm_final), loss_target[0], tm)
    dh5, de1, xp1, dpre1, dbg1, dgp1 = _ple_bwd(dh6, h5, row(norm_ple[1]), wgate[1], row(ple_b_gate[1]), pb[1],
                                                 wple[1], tm)
    ex_ple1 = _exchange_start(ple_grads(1, de1, xp1, dpre1), "send_ple1")
    dh4, xf1, dhh1, act1, dhb5, dcw1, dcb1, dgf1 = _ffn_bwd(dh5, h4, hh1, after(ex_ple1[4], row(norm_ffn[1])), wup[1],
                                                            cw[1], cb[1], wdown[1], tm)
    ex_ffn1 = _exchange_start(ffn_grads(1, xf1, dhh1, act1, dhb5), "send_ffn1")
    datt, dhb4 = _oproj_bwd(dh4, wo, tm)
    dq, dkv, dsink = _attn_bwd(q, kv, att, datt, after(ex_ffn1[4], sinks.reshape(1, N_Q)).reshape(N_Q), rep, bias)
    dh3, xq, xkv, dgq, dgkv = _qkv_bwd(dh4, h3, dq, dkv, row(norm_mix[1]), wq, row(norm_kv), wkv, tm)
    ex_att = _exchange_start([row_blocks(_wgrad(att, dhb4, "wgrad_o", tn=1024)),
                              row_blocks(_wgrad(xq, dq, "wgrad_q", tn=1024)),
                              row_blocks(_wgrad(xkv, dkv, "wgrad_kv", tn=512))], "send_att")
    dh2, de0, xp0, dpre0, dbg0, dgp0 = _ple_bwd(dh3, h2, after(ex_att[4], row(norm_ple[0])), wgate[0],
                                                 row(ple_b_gate[0]), pb[0], wple[0], tm)
    ex_ple0 = _exchange_start(ple_grads(0, de0, xp0, dpre0), "send_ple0")
    dh1, xf0, dhh0, act0, dhb2, dcw0, dcb0, dgf0 = _ffn_bwd(dh2, h1, hh0, after(ex_ple0[4], row(norm_ffn[0])), wup[0],
                                                            cw[0], cb[0], wdown[0], tm)
    ex_ffn0 = _exchange_start(ffn_grads(0, xf0, dhh0, act0, dhb2), "send_ffn0")
    dx, xa, dz, gated, dhb1, dws, dbs, dgv, dga = _mixa_bwd(dh1, x0, z, after(ex_ffn0[4], row(norm_mix[0])), win, gv,
                                                            ws, wst, bsb, wout, tm)
    ex_mix = _exchange_start([_wgrad(xa, dz, "wgrad_a_in", tn=2 * AW // N_DEV, out_blocked=True, a_transposed=True),
                              row_blocks(_wgrad(gated, dhb1, "wgrad_a_out", tn=1024))], "send_mix")

    def as3(a):
        return a.reshape((-1,) + a.shape[-2:])

    def adam(name, parts):
        outs = _adamw_weight(parts, as3(w[name]), as3(mom_m[name]), as3(mom_v[name]), "adamw_" + name)
        return [o.reshape(w[name].shape) for o in outs]

    big = {}
    l_ple1 = _exchange_wait(ex_ple1, dx, "recv_ple1")
    l_ffn1 = _exchange_wait(ex_ffn1, l_ple1[0], "recv_ffn1")
    l_att = _exchange_wait(ex_att, l_ffn1[0], "recv_att")
    big['b_w_o'], big['b_w_q'], big['w_kv'] = adam('b_w_o', [l_att[0]]), adam('b_w_q', [l_att[1]]), adam('w_kv', [l_att[2]])
    l_ple0 = _exchange_wait(ex_ple0, big['w_kv'][0], "recv_ple0")
    big['ple_w_in'] = adam('ple_w_in', [l_ple0[0], l_ple1[0]])
    big['ple_w_gate'] = adam('ple_w_gate', [l_ple0[1], l_ple1[1]])
    l_ffn0 = _exchange_wait(ex_ffn0, big['ple_w_gate'][0], "recv_ffn0")
    big['f_w_up'] = adam('f_w_up', [l_ffn0[0], l_ffn1[0]])
    big['f_w_down'] = adam('f_w_down', [l_ffn0[1], l_ffn1[1]])
    l_mix = _exchange_wait(ex_mix, big['f_w_down'][0], "recv_mix")
    big['a_w_in'], big['a_w_out'] = adam('a_w_in', [l_mix[0]]), adam('a_w_out', [l_mix[1]])
    g_big, d_big, m_big, v_big = [{n: big[n][i] for n in _BIG} for i in range(4)]

    small_g = {
        'norm_mix': jnp.concatenate([dga, dgq], axis=0), 'norm_ffn': jnp.concatenate([dgf0, dgf1], axis=0),
        'norm_ple': jnp.concatenate([dgp0, dgp1], axis=0), 'norm_kv': dgkv.reshape(D),
        'norm_final': d_norm_final.reshape(D), 'a_w_s': dws[None], 'a_b_s': dbs[None, :, :, 0],
        'b_sinks': dsink[:, 0].reshape(1, N_Q),
        'f_conv_b': jnp.stack([dcb0.reshape(2 * D_FF), dcb1.reshape(2 * D_FF)]),
        'ple_b_gate': jnp.concatenate([dbg0, dbg1], axis=0),
        'a_norm_v': dgv, 'f_conv_w': jnp.stack([dcw0, dcw1], axis=1),
    }
    names_all = _REPL + _SMALL_SHARDED
    full_shapes = [small_g[n].shape for n in names_all]
    gsum = _sum8(_all_gather(_pack([small_g[n] for n in names_all], 8), "gather_small_grads"), "sum_small_grads")
    gs = dict(zip(names_all, _unpack(gsum, full_shapes, 8)))
    gs['a_norm_v'] = lax.dynamic_slice_in_dim(gs['a_norm_v'], me * (AW // N_DEV), AW // N_DEV, axis=1)
    gs['f_conv_w'] = lax.dynamic_index_in_dim(gs['f_conv_w'], me, axis=0, keepdims=False)
    gs = {n: gs[n].reshape(w[n].shape) for n in names_all}
    packs = lambda d: _pack([d[n] for n in names_all], 8)
    souts = _adamw_small(packs(gs), packs(w), packs(mom_m), packs(mom_v), "adamw_small")
    shp = [w[n].shape for n in names_all]
    d_s, m_s, v_s = [dict(zip(names_all, _unpack(o, shp, 8))) for o in souts]

    grads = {**g_big, **gs}
    delta = {**d_big, **d_s}
    new_m = {**m_big, **m_s}
    new_v = {**v_big, **v_s}
    loss = lax.psum(loss_blk[0, 0], ("x", "y", "c"))
    return (loss, dx[None], *[grads[n] for n in _PARAMS], *[delta[n] for n in _PARAMS],
            *[new_m[n] for n in _PARAMS], *[new_v[n] for n in _PARAMS])
```

```python
import functools

import numpy as np
import jax
import jax.numpy as jnp
from jax import lax
from jax.experimental import pallas as pl
from jax.experimental.pallas import tpu as pltpu

F32 = jnp.float32
BF = jnp.bfloat16

N_DEV = 8
D = 1024
AW = 1024
GROUPS = 8
CHUNK = 128
HD = 64
N_Q = 16
N_KV = 4
GQA = 4
BLOCK = 128
D_FF = 2816
FF_BLK = 704
N_FF_BLK = 4
PLE = 256
EPS = 1e-6
LR, B1, B2, ADAM_EPS, WD, STEP = 0.001, 0.9, 0.999, 1e-08, 0.01, 10
GELU_C0 = 0.7978845608028654
GELU_C1 = 0.044715
NEG = -1e30
ROW = 1024
VMEM_BIG = 56 * 1024 * 1024

_PARAMS = ['norm_mix', 'norm_ffn', 'norm_ple', 'norm_kv', 'norm_final', 'a_w_in', 'a_norm_v', 'a_w_s', 'a_b_s',
           'a_w_out', 'w_kv', 'b_w_q', 'b_sinks', 'b_w_o', 'f_w_up', 'f_conv_w', 'f_conv_b', 'f_w_down',
           'ple_w_in', 'ple_w_gate', 'ple_b_gate']
_BIG = ['a_w_in', 'a_w_out', 'w_kv', 'b_w_q', 'b_w_o', 'f_w_up', 'f_w_down', 'ple_w_in', 'ple_w_gate']
_SMALL_SHARDED = ['a_norm_v', 'f_conv_w']
_REPL = ['norm_mix', 'norm_ffn', 'norm_ple', 'norm_kv', 'norm_final', 'a_w_s', 'a_b_s', 'b_sinks', 'f_conv_b',
         'ple_b_gate']


def _mm(a, b):
    return jnp.dot(a, b, preferred_element_type=F32)


def _mm_nt(a, b):
    return lax.dot_general(a, b, (((1,), (1,)), ((), ())), preferred_element_type=F32)


def _mm_tn(a, b):
    return lax.dot_general(a, b, (((0,), (0,)), ((), ())), preferred_element_type=F32)


def _const_spec(shape, single=False):
    nd = len(shape)
    if single:
        return pl.BlockSpec(shape, lambda *_: (0,) * nd, pipeline_mode=pl.Buffered(1))
    return pl.BlockSpec(shape, lambda *_: (0,) * nd)


def _row_spec(tm, n):
    return pl.BlockSpec((tm, n), lambda i: (i, 0))


def _rms_fwd(x, g):
    r = lax.rsqrt(jnp.mean(x * x, axis=-1, keepdims=True) + EPS)
    xhat = x * r
    return xhat * g, xhat, r


def _rms_bwd(dy, xhat, r, g):
    a = dy * g
    dx = r * (a - xhat * jnp.mean(a * xhat, axis=-1, keepdims=True))
    return dx, jnp.sum(dy * xhat, axis=0, keepdims=True)


def _sigmoid(x):
    return 1.0 / (1.0 + jnp.exp(-x))


def _params(vmem=None):
    return pltpu.CompilerParams(dimension_semantics=("arbitrary",), vmem_limit_bytes=vmem)


def _tri_masks():
    t = lax.broadcasted_iota(jnp.int32, (CHUNK, CHUNK), 0)
    s = lax.broadcasted_iota(jnp.int32, (CHUNK, CHUNK), 1)
    return t >= s, s >= t


def _mixa_fwd(x, g, win, gv, ws, bsb, wout, tm):
    S = x.shape[0]
    nc = tm // CHUNK

    def body(x_ref, g_ref, win_ref, gv_ref, ws_ref, bsb_ref, wout_ref, h_ref, z_ref, gated):
        x = x_ref[...]
        xn, _, _ = _rms_fwd(x, g_ref[...])
        z = _mm(xn.astype(BF), win_ref[...])
        z_ref[...] = z.astype(BF)
        zg = 0.5 * z * (1.0 + jnp.tanh(GELU_C0 * (z + GELU_C1 * z * z * z)))
        u = zg[:, :AW]
        vn, _, _ = _rms_fwd(zg[:, AW:], gv_ref[...])
        vn = vn.astype(BF)
        tril, _ = _tri_masks()
        for h in range(GROUPS):
            wm = jnp.where(tril, ws_ref[h], jnp.zeros((), BF))
            for c in range(nc):
                rs, cs = slice(c * CHUNK, (c + 1) * CHUNK), slice(h * CHUNK, (h + 1) * CHUNK)
                s = _mm(wm, vn[rs, cs]) + bsb_ref[h]
                gated[rs, cs] = (u[rs, cs] * s).astype(BF)
        h_ref[...] = x + _mm(gated[...], wout_ref[...])

    return pl.pallas_call(
        body, name="mixa_fwd", grid=(S // tm,),
        in_specs=[_row_spec(tm, D), _const_spec((1, D)), _const_spec((D, 2 * AW)), _const_spec((1, AW)),
                  _const_spec((GROUPS, CHUNK, CHUNK)), _const_spec((GROUPS, CHUNK, CHUNK)), _const_spec((AW, D))],
        out_specs=[_row_spec(tm, D), _row_spec(tm, 2 * AW)],
        out_shape=[jax.ShapeDtypeStruct((S, D), F32), jax.ShapeDtypeStruct((S, 2 * AW), BF)],
        scratch_shapes=[pltpu.VMEM((tm, AW), BF)],
        compiler_params=_params(VMEM_BIG),
    )(x, g, win, gv, ws, bsb, wout)


def _mixa_bwd(dh, x, z, g, win, gv, ws, wst, bsb, wout, tm):
    S = x.shape[0]
    nt = S // tm
    nc = tm // CHUNK

    def body(dh_ref, x_ref, z_ref, g_ref, win_ref, gv_ref, ws_ref, wst_ref, bsb_ref, wout_ref,
             dx_ref, xn_ref, dz_ref, gated_ref, dhb_ref, dws_ref, dbs_ref, dgv_ref, dg_ref, du_scr, dvn_scr):
        i = pl.program_id(0)

        @pl.when(i == 0)
        def _():
            dws_ref[...] = jnp.zeros_like(dws_ref)
            dbs_ref[...] = jnp.zeros_like(dbs_ref)
            dgv_ref[...] = jnp.zeros_like(dgv_ref)
            dg_ref[...] = jnp.zeros_like(dg_ref)

        x = x_ref[...]
        gmix = g_ref[...]
        xn, xhat, r = _rms_fwd(x, gmix)
        xn_ref[...] = xn.T.astype(BF)
        z = z_ref[...].astype(F32)
        z2 = z * z
        t = jnp.tanh(GELU_C0 * (z + GELU_C1 * z2 * z))
        zg = 0.5 * z * (1.0 + t)
        gp = 0.5 * (1.0 + t) + 0.5 * z * (1.0 - t * t) * (GELU_C0 * (1.0 + 3.0 * GELU_C1 * z2))
        u = zg[:, :AW]
        gvv = gv_ref[...]
        vn, vhat, rv = _rms_fwd(zg[:, AW:], gvv)
        vn = vn.astype(BF)
        dh = dh_ref[...]
        dhb = dh.astype(BF)
        dhb_ref[...] = dhb
        dgated = _mm_nt(dhb, wout_ref[...])
        tril, triu = _tri_masks()
        for h in range(GROUPS):
            wm = jnp.where(tril, ws_ref[h], jnp.zeros((), BF))
            wmt = jnp.where(triu, wst_ref[h], jnp.zeros((), BF))
            for c in range(nc):
                rs, cs = slice(c * CHUNK, (c + 1) * CHUNK), slice(h * CHUNK, (h + 1) * CHUNK)
                vnb = vn[rs, cs]
                s = _mm(wm, vnb) + bsb_ref[h]
                gated_ref[rs, cs] = (u[rs, cs] * s).astype(BF)
                dgt = dgated[rs, cs]
                du_scr[rs, cs] = dgt * s
                ds = dgt * u[rs, cs]
                dsb = ds.astype(BF)
                dws_ref[h] += jnp.where(tril, _mm_nt(dsb, vnb), 0.0)
                dbs_ref[h] += ds
                dvn_scr[rs, cs] = _mm(wmt, dsb)
        dvn = dvn_scr[...]
        dv, dgv = _rms_bwd(dvn, vhat, rv, gvv)
        dgv_ref[...] += dgv
        dz_ref[:, :AW] = (du_scr[...] * gp[:, :AW]).astype(BF)
        dz_ref[:, AW:] = (dv * gp[:, AW:]).astype(BF)
        dxn = _mm_nt(dz_ref[...], win_ref[...])
        dxr, dg = _rms_bwd(dxn, xhat, r, gmix)
        dg_ref[...] += dg
        dx_ref[...] = dh + dxr

        @pl.when(i == nt - 1)
        def _():
            for h in range(GROUPS):
                dbs_ref[h] = jnp.broadcast_to(jnp.sum(dbs_ref[h], axis=-1, keepdims=True), (CHUNK, CHUNK))

    gshape = (GROUPS, CHUNK, CHUNK)
    return pl.pallas_call(
        body, name="mixa_bwd", grid=(nt,),
        in_specs=[_row_spec(tm, D), _row_spec(tm, D), _row_spec(tm, 2 * AW), _const_spec((1, D)),
                  _const_spec((D, 2 * AW)), _const_spec((1, AW)), _const_spec(gshape), _const_spec(gshape),
                  _const_spec(gshape), _const_spec((AW, D))],
        out_specs=[_row_spec(tm, D), pl.BlockSpec((D, tm), lambda i: (0, i)), _row_spec(tm, 2 * AW), _row_spec(tm, AW),
                   _row_spec(tm, D), _const_spec(gshape), _const_spec(gshape), _const_spec((1, AW)), _const_spec((1, D))],
        out_shape=[jax.ShapeDtypeStruct((S, D), F32), jax.ShapeDtypeStruct((D, S), BF),
                   jax.ShapeDtypeStruct((S, 2 * AW), BF), jax.ShapeDtypeStruct((S, AW), BF),
                   jax.ShapeDtypeStruct((S, D), BF), jax.ShapeDtypeStruct(gshape, F32),
                   jax.ShapeDtypeStruct(gshape, F32), jax.ShapeDtypeStruct((1, AW), F32),
                   jax.ShapeDtypeStruct((1, D), F32)],
        scratch_shapes=[pltpu.VMEM((tm, AW), F32), pltpu.VMEM((tm, AW), F32)],
        compiler_params=_params(VMEM_BIG),
    )(dh, x, z, g, win, gv, ws, wst, bsb, wout)


def _conv_fwd(h, prev8, w, b, rows):
    hm1 = jnp.where(rows == 0, prev8[7:8], pltpu.roll(h, 1, 0))
    hm2 = jnp.where(rows == 0, prev8[6:7], jnp.where(rows == 1, prev8[7:8], pltpu.roll(h, 2, 0)))
    return b + w[0:1] * hm2 + w[1:2] * hm1 + w[2:3] * h, hm1, hm2


def _ffn_fwd(h, g, wup, cw, cb, wdown, tm):
    S = h.shape[0]

    def body(h_ref, g_ref, wup_ref, cw_ref, cb_ref, wdown_ref, o_ref, hh_ref, carry):
        i = pl.program_id(0)

        @pl.when(i == 0)
        def _():
            carry[...] = jnp.zeros_like(carry)

        x = h_ref[...]
        xn, _, _ = _rms_fwd(x, g_ref[...])
        xn = xn.astype(BF)
        rows = lax.broadcasted_iota(jnp.int32, (tm, 1), 0)
        acc = x
        for d in range(N_FF_BLK):
            cs = []
            for blk in (d, N_FF_BLK + d):
                hb = _mm(xn, wup_ref[blk])
                hh_ref[blk] = hb.astype(BF)
                c, _, _ = _conv_fwd(hb, carry[blk], cw_ref[blk], cb_ref[blk], rows)
                carry[blk] = hb[tm - 8:, :]
                cs.append(c)
            act = cs[0] * _sigmoid(cs[0]) * cs[1]
            acc = acc + _mm(act.astype(BF), wdown_ref[d * FF_BLK:(d + 1) * FF_BLK, :])
        o_ref[...] = acc

    return pl.pallas_call(
        body, name="ffn_fwd", grid=(S // tm,),
        in_specs=[_row_spec(tm, D), _const_spec((1, D)), _const_spec((N_DEV, D, FF_BLK), True),
                  _const_spec((N_DEV, 3, FF_BLK)), _const_spec((N_DEV, 1, FF_BLK)), _const_spec((D_FF, D), True)],
        out_specs=[_row_spec(tm, D), pl.BlockSpec((N_DEV, tm, FF_BLK), lambda i: (0, i, 0))],
        out_shape=[jax.ShapeDtypeStruct((S, D), F32), jax.ShapeDtypeStruct((N_DEV, S, FF_BLK), BF)],
        scratch_shapes=[pltpu.VMEM((N_DEV, 8, FF_BLK), F32)],
        compiler_params=_params(VMEM_BIG),
    )(h, g, wup, cw, cb, wdown)


def _ffn_bwd(dh, h, hh, g, wup, cw, cb, wdown, tm):
    S = h.shape[0]
    nt = S // tm
    halo = 16
    per = tm // halo

    def rev(i):
        return (nt - 1 - i, 0)

    def body(dh_ref, h_ref, hh_ref, halo_ref, g_ref, wup_ref, cw_ref, cb_ref, wdown_ref,
             dx_ref, xn_ref, dhh_ref, act_ref, dhb_ref, dcw_ref, dcb_ref, dg_ref, carry):
        i = pl.program_id(0)

        @pl.when(i == 0)
        def _():
            carry[...] = jnp.zeros_like(carry)
            dcw_ref[...] = jnp.zeros_like(dcw_ref)
            dcb_ref[...] = jnp.zeros_like(dcb_ref)
            dg_ref[...] = jnp.zeros_like(dg_ref)

        first_tile = i == nt - 1
        x = h_ref[...]
        gain = g_ref[...]
        xn, xhat, r = _rms_fwd(x, gain)
        xn_ref[...] = xn.T.astype(BF)
        dh = dh_ref[...]
        dhb = dh.astype(BF)
        dhb_ref[...] = dhb
        rows = lax.broadcasted_iota(jnp.int32, (tm, 1), 0)
        dxn = jnp.zeros((tm, D), F32)
        for d in range(N_FF_BLK):
            blks = (d, N_FF_BLK + d)
            hs, cs, hm1s, hm2s = [], [], [], []
            for blk in blks:
                hb = hh_ref[blk].astype(F32)
                prev8 = jnp.where(first_tile, 0.0, halo_ref[blk][halo - 8:, :].astype(F32))
                c, hm1, hm2 = _conv_fwd(hb, prev8, cw_ref[blk], cb_ref[blk], rows)
                hs.append(hb), cs.append(c), hm1s.append(hm1), hm2s.append(hm2)
            cg, cu = cs
            sg = _sigmoid(cg)
            sil = cg * sg
            act_ref[d] = (sil * cu).astype(BF)
            dact = _mm_nt(dhb, wdown_ref[d * FF_BLK:(d + 1) * FF_BLK, :])
            dcs = (dact * cu * (sg * (1.0 + cg * (1.0 - sg))), dact * sil)
            for blk, dc, hb, hm1, hm2 in zip(blks, dcs, hs, hm1s, hm2s):
                dcb_ref[blk] += jnp.sum(dc, axis=0, keepdims=True)
                dcw_ref[blk, 0:1, :] += jnp.sum(dc * hm2, axis=0, keepdims=True)
                dcw_ref[blk, 1:2, :] += jnp.sum(dc * hm1, axis=0, keepdims=True)
                dcw_ref[blk, 2:3, :] += jnp.sum(dc * hb, axis=0, keepdims=True)
                nxt = carry[blk]
                dp1 = jnp.where(rows == tm - 1, nxt[0:1], pltpu.roll(dc, tm - 1, 0))
                dp2 = jnp.where(rows == tm - 1, nxt[1:2], jnp.where(rows == tm - 2, nxt[0:1], pltpu.roll(dc, tm - 2, 0)))
                w = cw_ref[blk]
                dhh = (w[2:3] * dc + w[1:2] * dp1 + w[0:1] * dp2).astype(BF)
                carry[blk] = dc[0:8, :]
                dhh_ref[blk] = dhh
                dxn = dxn + _mm_nt(dhh, wup_ref[blk])
        dxr, dg = _rms_bwd(dxn, xhat, r, gain)
        dg_ref[...] += dg
        dx_ref[...] = dh + dxr

    blk3 = lambda i: (0, nt - 1 - i, 0)
    return pl.pallas_call(
        body, name="ffn_bwd", grid=(nt,),
        in_specs=[pl.BlockSpec((tm, D), rev), pl.BlockSpec((tm, D), rev),
                  pl.BlockSpec((N_DEV, tm, FF_BLK), blk3),
                  pl.BlockSpec((N_DEV, halo, FF_BLK), lambda i: (0, jnp.maximum((nt - 1 - i) * per - 1, 0), 0)),
                  _const_spec((1, D)), _const_spec((N_DEV, D, FF_BLK), True), _const_spec((N_DEV, 3, FF_BLK)),
                  _const_spec((N_DEV, 1, FF_BLK)), _const_spec((D_FF, D), True)],
        out_specs=[pl.BlockSpec((tm, D), rev), pl.BlockSpec((D, tm), lambda i: (0, nt - 1 - i)),
                   pl.BlockSpec((N_DEV, tm, FF_BLK), blk3), pl.BlockSpec((N_FF_BLK, tm, FF_BLK), blk3),
                   pl.BlockSpec((tm, D), rev), _const_spec((N_DEV, 3, FF_BLK)), _const_spec((N_DEV, 1, FF_BLK)),
                   _const_spec((1, D))],
        out_shape=[jax.ShapeDtypeStruct((S, D), F32), jax.ShapeDtypeStruct((D, S), BF),
                   jax.ShapeDtypeStruct((N_DEV, S, FF_BLK), BF), jax.ShapeDtypeStruct((N_FF_BLK, S, FF_BLK), BF),
                   jax.ShapeDtypeStruct((S, D), BF), jax.ShapeDtypeStruct((N_DEV, 3, FF_BLK), F32),
                   jax.ShapeDtypeStruct((N_DEV, 1, FF_BLK), F32), jax.ShapeDtypeStruct((1, D), F32)],
        scratch_shapes=[pltpu.VMEM((N_DEV, 8, FF_BLK), F32)],
        compiler_params=_params(VMEM_BIG),
    )(dh, h, hh, hh, g, wup, cw, cb, wdown)


def _ple_fwd(h, g, wgate, bgate, p, wple, tm):
    S = h.shape[0]

    def body(h_ref, g_ref, wg_ref, bg_ref, p_ref, wp_ref, o_ref):
        x = h_ref[...]
        xn, _, _ = _rms_fwd(x, g_ref[...])
        gate = _sigmoid(_mm(xn.astype(BF), wg_ref[...]) + bg_ref[...])
        o_ref[...] = x + _mm(p_ref[...], wp_ref[...]) * gate

    return pl.pallas_call(
        body, name="ple_fwd", grid=(S // tm,),
        in_specs=[_row_spec(tm, D), _const_spec((1, D)), _const_spec((D, D)), _const_spec((1, D)),
                  _row_spec(tm, PLE), _const_spec((PLE, D))],
        out_specs=_row_spec(tm, D), out_shape=jax.ShapeDtypeStruct((S, D), F32),
        compiler_params=_params(),
    )(h, g, wgate, bgate, p, wple)


def _ple_bwd(dh, h, g, wgate, bgate, p, wple, tm):
    S = h.shape[0]

    def body(dh_ref, h_ref, g_ref, wg_ref, bg_ref, p_ref, wp_ref, dx_ref, de_ref, xn_ref, dpre_ref, dbg_ref, dg_ref):
        i = pl.program_id(0)

        @pl.when(i == 0)
        def _():
            dbg_ref[...] = jnp.zeros_like(dbg_ref)
            dg_ref[...] = jnp.zeros_like(dg_ref)

        x = h_ref[...]
        gain = g_ref[...]
        xn, xhat, r = _rms_fwd(x, gain)
        xnb = xn.astype(BF)
        xn_ref[...] = xnb
        gate = _sigmoid(_mm(xnb, wg_ref[...]) + bg_ref[...])
        e = _mm(p_ref[...], wp_ref[...])
        dh = dh_ref[...]
        de_ref[...] = (dh * gate).astype(BF)
        dpre = dh * e * gate * (1.0 - gate)
        dpreb = dpre.astype(BF)
        dpre_ref[...] = dpreb
        dbg_ref[...] += jnp.sum(dpre, axis=0, keepdims=True)
        dxr, dg = _rms_bwd(_mm_nt(dpreb, wg_ref[...]), xhat, r, gain)
        dg_ref[...] += dg
        dx_ref[...] = dh + dxr

    return pl.pallas_call(
        body, name="ple_bwd", grid=(S // tm,),
        in_specs=[_row_spec(tm, D), _row_spec(tm, D), _const_spec((1, D)), _const_spec((D, D)), _const_spec((1, D)),
                  _row_spec(tm, PLE), _const_spec((PLE, D))],
        out_specs=[_row_spec(tm, D), _row_spec(tm, D), _row_spec(tm, D), _row_spec(tm, D), _const_spec((1, D)),
                   _const_spec((1, D))],
        out_shape=[jax.ShapeDtypeStruct((S, D), F32), jax.ShapeDtypeStruct((S, D), BF),
                   jax.ShapeDtypeStruct((S, D), BF), jax.ShapeDtypeStruct((S, D), BF),
                   jax.ShapeDtypeStruct((1, D), F32), jax.ShapeDtypeStruct((1, D), F32)],
        compiler_params=_params(),
    )(dh, h, g, wgate, bgate, p, wple)


def _qkv_fwd(h, gq, wq, gkv, wkv, tm):
    S = h.shape[0]
    nkv = wkv.shape[1]

    def body(h_ref, gq_ref, wq_ref, gkv_ref, wkv_ref, q_ref, kv_ref):
        x = h_ref[...]
        _, xhat, _ = _rms_fwd(x, gq_ref[...])
        q_ref[...] = _mm((xhat * gq_ref[...]).astype(BF), wq_ref[...]).astype(BF)
        kv_ref[...] = _mm((xhat * gkv_ref[...]).astype(BF), wkv_ref[...]).astype(BF)

    return pl.pallas_call(
        body, name="qkv_fwd", grid=(S // tm,),
        in_specs=[_row_spec(tm, D), _const_spec((1, D)), _const_spec((D, D)), _const_spec((1, D)),
                  _const_spec((D, nkv))],
        out_specs=[_row_spec(tm, D), _row_spec(tm, nkv)],
        out_shape=[jax.ShapeDtypeStruct((S, D), BF), jax.ShapeDtypeStruct((S, nkv), BF)],
        compiler_params=_params(),
    )(h, gq, wq, gkv, wkv)


def _qkv_bwd(dh, h, dq, dkv, gq, wq, gkv, wkv, tm):
    S = h.shape[0]
    nkv = wkv.shape[1]

    def body(dh_ref, h_ref, dq_ref, dkv_ref, gq_ref, wq_ref, gkv_ref, wkv_ref,
             dx_ref, xq_ref, xkv_ref, dgq_ref, dgkv_ref):
        i = pl.program_id(0)

        @pl.when(i == 0)
        def _():
            dgq_ref[...] = jnp.zeros_like(dgq_ref)
            dgkv_ref[...] = jnp.zeros_like(dgkv_ref)

        x = h_ref[...]
        gq_, gkv_ = gq_ref[...], gkv_ref[...]
        _, xhat, r = _rms_fwd(x, gq_)
        xq_ref[...] = (xhat * gq_).astype(BF)
        xkv_ref[...] = (xhat * gkv_).astype(BF)
        d1, dg1 = _rms_bwd(_mm_nt(dq_ref[...], wq_ref[...]), xhat, r, gq_)
        d2, dg2 = _rms_bwd(_mm_nt(dkv_ref[...], wkv_ref[...]), xhat, r, gkv_)
        dgq_ref[...] += dg1
        dgkv_ref[...] += dg2
        dx_ref[...] = dh_ref[...] + d1 + d2

    return pl.pallas_call(
        body, name="qkv_bwd", grid=(S // tm,),
        in_specs=[_row_spec(tm, D), _row_spec(tm, D), _row_spec(tm, D), _row_spec(tm, nkv), _const_spec((1, D)),
                  _const_spec((D, D)), _const_spec((1, D)), _const_spec((D, nkv))],
        out_specs=[_row_spec(tm, D), _row_spec(tm, D), _row_spec(tm, D), _const_spec((1, D)), _const_spec((1, D))],
        out_shape=[jax.ShapeDtypeStruct((S, D), F32), jax.ShapeDtypeStruct((S, D), BF),
                   jax.ShapeDtypeStruct((S, D), BF), jax.ShapeDtypeStruct((1, D), F32),
                   jax.ShapeDtypeStruct((1, D), F32)],
        compiler_params=_params(),
    )(dh, h, dq, dkv, gq, wq, gkv, wkv)


def _oproj_fwd(a, w, res, tm):
    S = a.shape[0]

    def body(a_ref, w_ref, r_ref, o_ref):
        o_ref[...] = r_ref[...] + _mm(a_ref[...], w_ref[...])

    return pl.pallas_call(
        body, name="oproj_fwd", grid=(S // tm,),
        in_specs=[_row_spec(tm, D), _const_spec((D, D)), _row_spec(tm, D)],
        out_specs=_row_spec(tm, D), out_shape=jax.ShapeDtypeStruct((S, D), F32),
        compiler_params=_params(),
    )(a, w, res)


def _oproj_bwd(dh, w, tm):
    S = dh.shape[0]

    def body(dh_ref, w_ref, da_ref, dhb_ref):
        dhb = dh_ref[...].astype(BF)
        dhb_ref[...] = dhb
        da_ref[...] = _mm_nt(dhb, w_ref[...]).astype(BF)

    return pl.pallas_call(
        body, name="oproj_bwd", grid=(S // tm,),
        in_specs=[_row_spec(tm, D), _const_spec((D, D))],
        out_specs=[_row_spec(tm, D), _row_spec(tm, D)],
        out_shape=[jax.ShapeDtypeStruct((S, D), BF), jax.ShapeDtypeStruct((S, D), BF)],
        compiler_params=_params(),
    )(dh, w)


def _alibi_slope(hq):
    return float(np.float32(2.0 ** (-8.0 * (hq + 1) / N_Q)))


def _attn_consts():
    W = N_KV * HD
    r = np.zeros((W, N_KV * W), np.float32)
    for kh in range(N_KV):
        for c in range(W):
            r[kh * HD + c % HD, kh * W + c] = 1.0
    qi = np.arange(BLOCK)[:, None]
    kj = np.arange(2 * BLOCK)[None, :]
    dist = qi + BLOCK - kj
    band = (dist >= 0) & (dist < BLOCK)
    bias = np.full((2, N_KV, GQA * BLOCK, 2 * BLOCK), NEG, np.float32)
    for later in (0, 1):
        valid = band & ((kj >= BLOCK) | bool(later))
        for kh in range(N_KV):
            for g in range(GQA):
                slope = np.float32(_alibi_slope(kh * GQA + g))
                bias[later, kh, g * BLOCK:(g + 1) * BLOCK] = np.where(valid, -slope * dist.astype(np.float32), NEG)
    return jnp.asarray(r, BF), jnp.asarray(bias)


def _head_masks():
    W = N_KV * HD
    rq = lax.broadcasted_iota(jnp.int32, (GQA * BLOCK, W), 0) >> 7
    cq = lax.broadcasted_iota(jnp.int32, (GQA * BLOCK, W), 1) >> 6
    rk = lax.broadcasted_iota(jnp.int32, (GQA * 2 * BLOCK, W), 0) >> 8
    ck = lax.broadcasted_iota(jnp.int32, (GQA * 2 * BLOCK, W), 1) >> 6
    return rq == cq, rk == ck


def _sink_col(sink_ref, kh):
    rb = lax.broadcasted_iota(jnp.int32, (GQA * BLOCK, 1), 0) >> 7
    col = jnp.full((GQA * BLOCK, 1), sink_ref[kh * GQA + GQA - 1], F32)
    for g in range(GQA - 1):
        col = jnp.where(rb == g, sink_ref[kh * GQA + g], col)
    return col


def _stack4(a):
    return jnp.concatenate([a] * GQA, axis=0)


def _rows_to_lanes(a):
    return jnp.concatenate([a[g * BLOCK:(g + 1) * BLOCK] for g in range(GQA)], axis=1)


def _lanes_to_rows(a, n):
    return jnp.concatenate([a[:, g * n:(g + 1) * n] for g in range(GQA)], axis=0)


def _scores(qg, k4, bias, sink_col, maskq):
    qs = jnp.where(maskq, _stack4(qg), jnp.zeros((), BF))
    s = _mm_nt(qs, k4) * (HD ** -0.5) + bias
    m = jnp.maximum(jnp.max(s, axis=-1, keepdims=True), sink_col)
    return qs, jnp.exp(s - m), jnp.exp(sink_col - m)


def _bands(kvc_ref, kvp_ref, rep_ref):
    W = N_KV * HD
    kvp, kvc = kvp_ref[...], kvc_ref[...]
    kband = jnp.concatenate([kvp[:, :W], kvc[:, :W]], axis=0)
    vband = jnp.concatenate([kvp[:, W:], kvc[:, W:]], axis=0)
    return _mm(kband, rep_ref[...]).astype(BF), _mm(vband, rep_ref[...]).astype(BF)


def _attn_fwd(q, kv, sinks, rep, bias):
    S = q.shape[0]
    nb = S // BLOCK
    W = N_KV * HD

    def body(sink_ref, q_ref, kvc_ref, kvp_ref, rep_ref, bias_ref, o_ref):
        maskq, maskk = _head_masks()
        ones_bd = maskk.astype(BF)
        lane_head = lax.broadcasted_iota(jnp.int32, (BLOCK, W), 1) >> 6
        k4all, v4all = _bands(kvc_ref, kvp_ref, rep_ref)
        for kh in range(N_KV):
            cols = slice(kh * W, (kh + 1) * W)
            _, e, es = _scores(q_ref[:, cols], k4all[:, cols], bias_ref[kh], _sink_col(sink_ref, kh), maskq)
            vbd = jnp.concatenate([jnp.where(maskk, _stack4(v4all[:, cols]), jnp.zeros((), BF)), ones_bd], axis=1)
            nd = _mm(_rows_to_lanes(e.astype(BF)), vbd)
            es_l = jnp.broadcast_to(es[(GQA - 1) * BLOCK:], (BLOCK, W))
            for g in range(GQA - 1):
                es_l = jnp.where(lane_head == g, es[g * BLOCK:(g + 1) * BLOCK], es_l)
            o_ref[:, cols] = (nd[:, :W] / (nd[:, W:] + es_l)).astype(BF)

    return pl.pallas_call(
        body, name="attn_fwd", grid=(nb,),
        in_specs=[pl.BlockSpec(memory_space=pltpu.SMEM), _row_spec(BLOCK, D), _row_spec(BLOCK, 2 * W),
                  pl.BlockSpec((BLOCK, 2 * W), lambda i: (jnp.maximum(i - 1, 0), 0)), _const_spec((W, N_KV * W)),
                  pl.BlockSpec((None, N_KV, GQA * BLOCK, 2 * BLOCK), lambda i: (jnp.minimum(i, 1), 0, 0, 0))],
        out_specs=_row_spec(BLOCK, D), out_shape=jax.ShapeDtypeStruct((S, D), BF),
        compiler_params=_params(),
    )(sinks, q, kv, kv, rep, bias)


def _attn_bwd(q, kv, att, do, sinks, rep, bias):
    S = q.shape[0]
    nb = S // BLOCK
    W = N_KV * HD

    def qmap(i):
        return (jnp.minimum(i, nb - 1), 0)

    def body(sink_ref, q_ref, kvc_ref, kvp_ref, att_ref, do_ref, rep_ref, bias_ref,
             dq_ref, dkv_ref, dsink_ref, carry, sink_acc):
        i = pl.program_id(0)

        @pl.when(i == 0)
        def _():
            sink_acc[...] = jnp.zeros_like(sink_acc)
            carry[...] = jnp.zeros_like(carry)

        @pl.when(i < nb)
        def _():
            maskq, maskk = _head_masks()
            ones_k = jnp.ones((2 * BLOCK, BLOCK), BF)
            head_sum = ((lax.broadcasted_iota(jnp.int32, (W, GQA * BLOCK), 0) >> 6)
                        == (lax.broadcasted_iota(jnp.int32, (W, GQA * BLOCK), 1) >> 7)).astype(BF)
            k4all, v4all = _bands(kvc_ref, kvp_ref, rep_ref)
            dk4s, dv4s = [], []
            for kh in range(N_KV):
                cols = slice(kh * W, (kh + 1) * W)
                k4, v4 = k4all[:, cols], v4all[:, cols]
                qs, e, es = _scores(q_ref[:, cols], k4, bias_ref[kh], _sink_col(sink_ref, kh), maskq)
                inv = 1.0 / (_mm(e.astype(BF), ones_k) + es)
                p = e * jnp.concatenate([inv, inv], axis=1)
                dog = do_ref[:, cols]
                dos = jnp.where(maskq, _stack4(dog), jnp.zeros((), BF))
                dp = _mm_nt(dos, v4)
                prod = (dog.astype(F32) * att_ref[:, cols].astype(F32)).astype(BF)
                delta = _lanes_to_rows(_mm(prod, head_sum), BLOCK)
                ds = (p * (dp - jnp.concatenate([delta, delta], axis=1))).astype(BF)
                sink_acc[kh] += es * inv * delta
                kbd = jnp.where(maskk, _stack4(k4), jnp.zeros((), BF))
                dq_ref[:, cols] = (_mm(_rows_to_lanes(ds), kbd) * (HD ** -0.5)).astype(BF)
                dk4s.append((_mm_tn(ds, qs) * (HD ** -0.5)).astype(BF))
                dv4s.append(_mm_tn(p.astype(BF), dos).astype(BF))
            dkband = _mm_nt(jnp.concatenate(dk4s, axis=1), rep_ref[...])
            dvband = _mm_nt(jnp.concatenate(dv4s, axis=1), rep_ref[...])
            dkv_ref[:, :W] = (carry[:, :W] + dkband[:BLOCK]).astype(BF)
            dkv_ref[:, W:] = (carry[:, W:] + dvband[:BLOCK]).astype(BF)
            carry[:, :W] = dkband[BLOCK:]
            carry[:, W:] = dvband[BLOCK:]

        @pl.when(i == nb)
        def _():
            dkv_ref[...] = carry[...].astype(BF)
            for hq in range(N_Q):
                kh, g = divmod(hq, GQA)
                dsink_ref[hq:hq + 1, :] = -jnp.sum(sink_acc[kh, g * BLOCK:(g + 1) * BLOCK, :], axis=0, keepdims=True)

    return pl.pallas_call(
        body, name="attn_bwd", grid=(nb + 1,),
        in_specs=[pl.BlockSpec(memory_space=pltpu.SMEM), pl.BlockSpec((BLOCK, D), qmap),
                  pl.BlockSpec((BLOCK, 2 * W), qmap),
                  pl.BlockSpec((BLOCK, 2 * W), lambda i: (jnp.maximum(jnp.minimum(i, nb - 1) - 1, 0), 0)),
                  pl.BlockSpec((BLOCK, D), qmap), pl.BlockSpec((BLOCK, D), qmap), _const_spec((W, N_KV * W)),
                  pl.BlockSpec((None, N_KV, GQA * BLOCK, 2 * BLOCK), lambda i: (jnp.minimum(i, 1), 0, 0, 0))],
        out_specs=[pl.BlockSpec((BLOCK, D), qmap),
                   pl.BlockSpec((BLOCK, 2 * W), lambda i: (jnp.maximum(i - 1, 0), 0)),
                   _const_spec((N_Q, BLOCK))],
        out_shape=[jax.ShapeDtypeStruct((S, D), BF), jax.ShapeDtypeStruct((S, 2 * W), BF),
                   jax.ShapeDtypeStruct((N_Q, BLOCK), F32)],
        scratch_shapes=[pltpu.VMEM((BLOCK, 2 * W), F32), pltpu.VMEM((N_KV, GQA * BLOCK, BLOCK), F32)],
        compiler_params=_params(),
    )(sinks, q, kv, kv, att, do, rep, bias)


def _head(h, g, target, tm):
    S = h.shape[0]

    def body(h_ref, g_ref, t_ref, dh_ref, loss_ref, dg_ref):
        i = pl.program_id(0)

        @pl.when(i == 0)
        def _():
            loss_ref[...] = jnp.zeros_like(loss_ref)
            dg_ref[...] = jnp.zeros_like(dg_ref)

        gain = g_ref[...]
        y, xhat, r = _rms_fwd(h_ref[...], gain)
        err = y - t_ref[...]
        tile = 0.5 * jnp.sum(jnp.sum(err * err, axis=-1, keepdims=True) * (1.0 / D), axis=0, keepdims=True)
        loss_ref[...] += jnp.broadcast_to(tile, loss_ref.shape)
        dx, dg = _rms_bwd(err * (1.0 / D), xhat, r, gain)
        dg_ref[...] += dg
        dh_ref[...] = dx

    return pl.pallas_call(
        body, name="loss_head", grid=(S // tm,),
        in_specs=[_row_spec(tm, D), _const_spec((1, D)), _row_spec(tm, D)],
        out_specs=[_row_spec(tm, D), _const_spec((8, 128)), _const_spec((1, D))],
        out_shape=[jax.ShapeDtypeStruct((S, D), F32), jax.ShapeDtypeStruct((8, 128), F32),
                   jax.ShapeDtypeStruct((1, D), F32)],
        compiler_params=_params(),
    )(h, g, target)


def _wgrad(a, b, name, *, tn, out_blocked=False, a_transposed=False, ts=1024):
    a_blocked, b_blocked = a.ndim == 3, b.ndim == 3
    na = a.shape[0] if a_blocked else 1
    S = b.shape[-2]
    K = a.shape[0] if a_transposed else a.shape[-1]
    nbk = b.shape[0] if b_blocked else b.shape[1] // tn
    ts = min(ts, S)
    ns = S // ts
    assert not (a_blocked and (out_blocked or a_transposed))

    def body(a_ref, b_ref, o_ref, acc):
        s = pl.program_id(2)

        @pl.when(s == 0)
        def _():
            acc[...] = jnp.zeros_like(acc)

        if a_transposed:
            acc[...] += _mm(a_ref[...], b_ref[...])
        else:
            acc[...] += _mm_tn(a_ref[...], b_ref[...])

        @pl.when(s == ns - 1)
        def _():
            o_ref[...] = acc[...].astype(o_ref.dtype)

    if a_transposed:
        a_spec = pl.BlockSpec((K, ts), lambda i, j, s: (0, s))
    elif a_blocked:
        a_spec = pl.BlockSpec((None, ts, K), lambda i, j, s: (i, s, 0))
    else:
        a_spec = pl.BlockSpec((ts, K), lambda i, j, s: (s, 0))
    b_spec = (pl.BlockSpec((None, ts, tn), lambda i, j, s: (j, s, 0)) if b_blocked
              else pl.BlockSpec((ts, tn), lambda i, j, s: (s, j)))
    if out_blocked:
        o_spec = pl.BlockSpec((None, K, tn), lambda i, j, s: (j, 0, 0))
        o_shape = jax.ShapeDtypeStruct((nbk, K, tn), BF)
    else:
        o_spec = pl.BlockSpec((K, tn), lambda i, j, s: (i, j))
        o_shape = jax.ShapeDtypeStruct((na * K, nbk * tn), BF)
    return pl.pallas_call(
        body, name=name, grid=(na, nbk, ns), in_specs=[a_spec, b_spec], out_specs=o_spec, out_shape=o_shape,
        scratch_shapes=[pltpu.VMEM((K, tn), F32)],
        compiler_params=pltpu.CompilerParams(dimension_semantics=("arbitrary", "arbitrary", "arbitrary"),
                                             vmem_limit_bytes=VMEM_BIG),
    )(a, b)


def _adamw_math(w, g, m, v):
    m = B1 * m + (1.0 - B1) * g
    v = B2 * v + (1.0 - B2) * (g * g)
    m_hat = m / (1.0 - B1 ** STEP)
    v_hat = v / (1.0 - B2 ** STEP)
    return -LR * (m_hat / (jnp.sqrt(v_hat) + ADAM_EPS) + WD * w), m, v


def _sum8(parts, name):
    R = parts.shape[1]

    def body(p_ref, o_ref):
        g = p_ref[0]
        for s in range(1, N_DEV):
            g = g + p_ref[s]
        o_ref[...] = g

    return pl.pallas_call(
        body, name=name, grid=(1,),
        in_specs=[pl.BlockSpec((N_DEV, R, ROW), lambda i: (0, 0, 0))],
        out_specs=pl.BlockSpec((R, ROW), lambda i: (0, 0)), out_shape=jax.ShapeDtypeStruct((R, ROW), F32),
        compiler_params=_params(),
    )(parts)


def _adamw_small(g, w, m, v, name):
    R = w.shape[0]

    def body(g_ref, w_ref, m_ref, v_ref, d_ref, nm_ref, nv_ref):
        d_ref[...], nm_ref[...], nv_ref[...] = _adamw_math(w_ref[...], g_ref[...], m_ref[...], v_ref[...])

    spec = pl.BlockSpec((R, ROW), lambda i: (0, 0))
    return pl.pallas_call(
        body, name=name, grid=(1,), in_specs=[spec] * 4, out_specs=[spec] * 3,
        out_shape=[jax.ShapeDtypeStruct((R, ROW), F32)] * 3, compiler_params=_params(),
    )(g, w, m, v)


def _mesh_pos():
    return lax.axis_index("x"), lax.axis_index("y"), lax.axis_index("c")


def _flip(v, bit):
    return 1 - v if bit else v


def _peers(x, y, c):
    out = []
    for k in range(1, N_DEV):
        px, py, pc = _flip(x, k & 4), _flip(y, k & 2), _flip(c, k & 1)
        out.append((k - 1, (px, py, pc), 4 * px + 2 * py + pc))
    return out


def _send_start(srcs, blocked, name):
    n = len(srcs)
    MESH = pl.DeviceIdType.MESH

    def body(*refs):
        src_refs, land_refs = refs[:n], refs[n:2 * n]
        send_sems, recv_sems, local_sems = refs[2 * n:3 * n], refs[3 * n:4 * n], refs[4 * n:5 * n]
        token = refs[7 * n]
        x, y, c = _mesh_pos()
        me = 4 * x + 2 * y + c
        for a in range(n):
            for k, peer, blk in _peers(x, y, c):
                pltpu.make_async_remote_copy(
                    src_ref=src_refs[a].at[blk] if blocked else src_refs[a], dst_ref=land_refs[a].at[me],
                    send_sem=send_sems[a].at[k], recv_sem=recv_sems[a].at[k], device_id=peer,
                    device_id_type=MESH).start()
            pltpu.make_async_copy(src_refs[a].at[me] if blocked else src_refs[a], land_refs[a].at[me],
                                  local_sems[a]).start()
        token[...] = jnp.zeros_like(token)

    hbm = pl.BlockSpec(memory_space=pltpu.HBM)
    sem = pl.BlockSpec(memory_space=pltpu.SEMAPHORE)
    lshape = [s.shape if blocked else (N_DEV,) + s.shape for s in srcs]
    outs = pl.pallas_call(
        body, name=name,
        out_shape=([pltpu.SemaphoreType.DMA((N_DEV - 1,))] * (2 * n) + [pltpu.SemaphoreType.DMA(())] * n
                   + [pltpu.HBM(s.shape, s.dtype) for s in srcs]
                   + [pltpu.HBM(ls, s.dtype) for ls, s in zip(lshape, srcs)] + [jax.ShapeDtypeStruct((8, 128), F32)]),
        in_specs=[hbm] * (2 * n), out_specs=[sem] * (3 * n) + [hbm] * (2 * n) + [pl.BlockSpec(memory_space=pltpu.VMEM)],
        input_output_aliases={**{a: 3 * n + a for a in range(n)}, **{n + a: 4 * n + a for a in range(n)}},
        compiler_params=pltpu.CompilerParams(has_side_effects=pltpu.SideEffectType.DATAFLOW_SIDE_EFFECTING),
    )(*[pltpu.with_memory_space_constraint(s, pltpu.HBM) for s in srcs],
      *[pltpu.with_memory_space_constraint(lax.empty(ls, s.dtype), pltpu.HBM) for ls, s in zip(lshape, srcs)])
    return dict(send=outs[:n], recv=outs[n:2 * n], local=outs[2 * n:3 * n], srcs=outs[3 * n:4 * n],
                lands=outs[4 * n:5 * n], token=outs[5 * n], blocked=blocked)


def _send_wait(started, which, after, name):
    n = len(which)
    blocked = started["blocked"]
    pick = lambda key: [started[key][a] for a in which]
    MESH = pl.DeviceIdType.MESH

    def body(*refs):
        src_refs, land_refs = refs[:n], refs[n:2 * n]
        send, recv, local = refs[2 * n:3 * n], refs[3 * n:4 * n], refs[4 * n:5 * n]
        x, y, c = _mesh_pos()
        me = 4 * x + 2 * y + c
        for a in range(n):
            for k, peer, blk in _peers(x, y, c):
                cp = pltpu.make_async_remote_copy(
                    src_ref=src_refs[a].at[blk] if blocked else src_refs[a], dst_ref=land_refs[a].at[blk],
                    send_sem=send[a].at[k], recv_sem=recv[a].at[k], device_id=peer, device_id_type=MESH)
                cp.wait_send()
                cp.wait_recv()
            pltpu.make_async_copy(src_refs[a].at[me] if blocked else src_refs[a], land_refs[a].at[me],
                                  local[a]).wait()

    hbm = pl.BlockSpec(memory_space=pltpu.HBM)
    sem = pl.BlockSpec(memory_space=pltpu.SEMAPHORE)
    srcs, lands = pick("srcs"), pick("lands")
    outs = pl.pallas_call(
        body, name=name, out_shape=[pltpu.HBM(s.shape, s.dtype) for s in srcs + lands],
        in_specs=[hbm] * (2 * n) + [sem] * (3 * n) + [pl.BlockSpec(memory_space=pl.ANY)], out_specs=[hbm] * (2 * n),
        input_output_aliases={a: a for a in range(2 * n)},
        compiler_params=pltpu.CompilerParams(has_side_effects=pltpu.SideEffectType.DATAFLOW_SIDE_EFFECTING),
    )(*srcs, *lands, *pick("send"), *pick("recv"), *pick("local"), after)
    return outs[n:]


def _adamw_weight(parts, w, m, v, name):
    L, R, C = w.shape
    tr = max(t for t in range(16, min(R, 256) + 1, 16) if R % t == 0)
    nr = R // tr

    def body(*refs):
        p_refs = refs[:L]
        w_ref, m_ref, v_ref, g_ref, d_ref, nm_ref, nv_ref, gsum = refs[L:]
        layer = pl.program_id(0)
        for l in range(L):
            @pl.when(layer == l)
            def _(l=l):
                g = p_refs[l][0].astype(F32)
                for s in range(1, N_DEV):
                    g = g + p_refs[l][s].astype(F32)
                gsum[...] = g
        g = gsum[...]
        g_ref[...] = g
        d_ref[...], nm_ref[...], nv_ref[...] = _adamw_math(w_ref[...], g, m_ref[...], v_ref[...])

    def part_spec(l):
        return pl.BlockSpec((N_DEV, tr, C), lambda layer, i: (0, jnp.where(layer == l, i, jnp.where(layer < l, 0, nr - 1)), 0))

    ws = pl.BlockSpec((None, tr, C), lambda layer, i: (layer, i, 0))
    return pl.pallas_call(
        body, name=name, grid=(L, nr), in_specs=[part_spec(l) for l in range(L)] + [ws] * 3, out_specs=[ws] * 4,
        out_shape=[jax.ShapeDtypeStruct((L, R, C), F32)] * 4, scratch_shapes=[pltpu.VMEM((tr, C), F32)],
        compiler_params=pltpu.CompilerParams(dimension_semantics=("arbitrary", "arbitrary")),
    )(*parts, w, m, v)


def _seg_rows(n, align):
    rows = -(-n // ROW)
    return -(-rows // align) * align


def _pack(arrs, align, lead=()):
    parts = []
    for a in arrs:
        n = int(np.prod(a.shape[len(lead):]))
        rows = _seg_rows(n, align)
        flat = a.reshape(lead + (n,))
        flat = jnp.pad(flat, [(0, 0)] * len(lead) + [(0, rows * ROW - n)])
        parts.append(flat.reshape(lead + (rows, ROW)))
    return jnp.concatenate(parts, axis=len(lead))


def _unpack(buf, shapes, align, lead=()):
    out, r0 = [], 0
    for shp in shapes:
        n = int(np.prod(shp))
        rows = _seg_rows(n, align)
        seg = lax.slice_in_dim(buf, r0, r0 + rows, axis=len(lead))
        out.append(seg.reshape(lead + (rows * ROW,))[..., :n].reshape(lead + tuple(shp)))
        r0 += rows
    return out


def kernel(x, p, norm_mix, norm_ffn, norm_ple, norm_kv, norm_final, a_w_in, a_norm_v, a_w_s, a_b_s, a_w_out, w_kv, b_w_q, b_sinks, b_w_o, f_w_up, f_conv_w, f_conv_b, f_w_down, ple_w_in, ple_w_gate, ple_b_gate, loss_target, m_norm_mix, m_norm_ffn, m_norm_ple, m_norm_kv, m_norm_final, m_a_w_in, m_a_norm_v, m_a_w_s, m_a_b_s, m_a_w_out, m_w_kv, m_b_w_q, m_b_sinks, m_b_w_o, m_f_w_up, m_f_conv_w, m_f_conv_b, m_f_w_down, m_ple_w_in, m_ple_w_gate, m_ple_b_gate, v_norm_mix, v_norm_ffn, v_norm_ple, v_norm_kv, v_norm_final, v_a_w_in, v_a_norm_v, v_a_w_s, v_a_b_s, v_a_w_out, v_w_kv, v_b_w_q, v_b_sinks, v_b_w_o, v_f_w_up, v_f_conv_w, v_f_conv_b, v_f_w_down, v_ple_w_in, v_ple_w_gate, v_ple_b_gate):
    w = dict(norm_mix=norm_mix, norm_ffn=norm_ffn, norm_ple=norm_ple, norm_kv=norm_kv, norm_final=norm_final,
             a_w_in=a_w_in, a_norm_v=a_norm_v, a_w_s=a_w_s, a_b_s=a_b_s, a_w_out=a_w_out, w_kv=w_kv, b_w_q=b_w_q,
             b_sinks=b_sinks, b_w_o=b_w_o, f_w_up=f_w_up, f_conv_w=f_conv_w, f_conv_b=f_conv_b, f_w_down=f_w_down,
             ple_w_in=ple_w_in, ple_w_gate=ple_w_gate, ple_b_gate=ple_b_gate)
    mom_m = dict(norm_mix=m_norm_mix, norm_ffn=m_norm_ffn, norm_ple=m_norm_ple, norm_kv=m_norm_kv,
                 norm_final=m_norm_final, a_w_in=m_a_w_in, a_norm_v=m_a_norm_v, a_w_s=m_a_w_s, a_b_s=m_a_b_s,
                 a_w_out=m_a_w_out, w_kv=m_w_kv, b_w_q=m_b_w_q, b_sinks=m_b_sinks, b_w_o=m_b_w_o, f_w_up=m_f_w_up,
                 f_conv_w=m_f_conv_w, f_conv_b=m_f_conv_b, f_w_down=m_f_w_down, ple_w_in=m_ple_w_in,
                 ple_w_gate=m_ple_w_gate, ple_b_gate=m_ple_b_gate)
    mom_v = dict(norm_mix=v_norm_mix, norm_ffn=v_norm_ffn, norm_ple=v_norm_ple, norm_kv=v_norm_kv,
                 norm_final=v_norm_final, a_w_in=v_a_w_in, a_norm_v=v_a_norm_v, a_w_s=v_a_w_s, a_b_s=v_a_b_s,
                 a_w_out=v_a_w_out, w_kv=v_w_kv, b_w_q=v_b_w_q, b_sinks=v_b_sinks, b_w_o=v_b_w_o, f_w_up=v_f_w_up,
                 f_conv_w=v_f_conv_w, f_conv_b=v_f_conv_b, f_w_down=v_f_w_down, ple_w_in=v_ple_w_in,
                 ple_w_gate=v_ple_w_gate, ple_b_gate=v_ple_b_gate)
    S = x.shape[1]
    tm = min(256, S)
    me = 4 * lax.axis_index("x") + 2 * lax.axis_index("y") + lax.axis_index("c")

    def after_token(tok, a):
        return a + tok[0:1, 0:1].reshape((1,) * a.ndim)

    wb = {n: w[n].astype(BF) for n in _BIG}
    order = [('a_w_in', 0), ('a_w_out', 0), ('a_norm_v', None), ('f_conv_w', 0), ('f_conv_w', 1),
             ('f_w_up', 0), ('f_w_down', 0),
             ('ple_w_in', 0), ('ple_w_gate', 0), ('w_kv', None), ('b_w_q', 0), ('b_w_o', 0),
             ('f_w_up', 1), ('f_w_down', 1), ('ple_w_in', 1), ('ple_w_gate', 1)]
    src = lambda n, l: (wb[n] if n in wb else w[n]) if l is None else (wb[n] if n in wb else w[n])[l]
    gath = _send_start([src(n, l) for n, l in order], False, "gather_start")
    x0 = x[0]

    def gathered(names, after, name):
        which = [order.index(k) for k in names]
        return dict(zip(names, _send_wait(gath, which, after, name)))

    g = gathered(order[:5], x0, "gather_wait_mix")
    win = g[('a_w_in', 0)].transpose(1, 0, 2).reshape(D, 2 * AW)
    wout = g[('a_w_out', 0)].reshape(AW, D)
    gv = g[('a_norm_v', None)].reshape(1, AW)
    cw = [g[('f_conv_w', l)] for l in range(2)]
    cb = [w['f_conv_b'][l].reshape(N_DEV, 1, FF_BLK) for l in range(2)]
    ws = a_w_s[0].astype(BF)
    wst = jnp.swapaxes(ws, 1, 2)
    bsb = jnp.broadcast_to(a_b_s[0][:, :, None], (GROUPS, CHUNK, CHUNK))
    row = lambda a: a.reshape(1, -1)
    pb = [p[l, 0].astype(BF) for l in range(2)]
    sinks = b_sinks.reshape(N_Q)
    rep, bias = _attn_consts()

    h1, z = _mixa_fwd(x0, row(norm_mix[0]), win, gv, ws, bsb, wout, tm)
    g = gathered(order[5:7], h1, "gather_wait_ffn0")
    wup, wdown = [g[('f_w_up', 0)], None], [g[('f_w_down', 0)].reshape(D_FF, D), None]
    h2, hh0 = _ffn_fwd(h1, row(norm_ffn[0]), wup[0], cw[0], cb[0], wdown[0], tm)
    g = gathered(order[7:12], h2, "gather_wait_mid")
    wple = [g[('ple_w_in', 0)].transpose(1, 0, 2).reshape(PLE, D), None]
    wgate = [g[('ple_w_gate', 0)].reshape(D, D), None]
    wkv, wq, wo = g[('w_kv', None)].reshape(D, 2 * N_KV * HD), g[('b_w_q', 0)].reshape(D, D), g[('b_w_o', 0)].reshape(D, D)
    h3 = _ple_fwd(h2, row(norm_ple[0]), wgate[0], row(ple_b_gate[0]), pb[0], wple[0], tm)
    q, kv = _qkv_fwd(h3, row(norm_mix[1]), wq, row(norm_kv), wkv, tm)
    att = _attn_fwd(q, kv, sinks, rep, bias)
    h4 = _oproj_fwd(att, wo, h3, tm)
    g = gathered(order[12:], h4, "gather_wait_l1")
    wup[1], wdown[1] = g[('f_w_up', 1)], g[('f_w_down', 1)].reshape(D_FF, D)
    wple[1], wgate[1] = g[('ple_w_in', 1)].transpose(1, 0, 2).reshape(PLE, D), g[('ple_w_gate', 1)].reshape(D, D)
    h5, hh1 = _ffn_fwd(h4, row(norm_ffn[1]), wup[1], cw[1], cb[1], wdown[1], tm)
    h6 = _ple_fwd(h5, row(norm_ple[1]), wgate[1], row(ple_b_gate[1]), pb[1], wple[1], tm)

    def col_blocks(g2d, n):
        return g2d.reshape(g2d.shape[0], N_DEV, n).transpose(1, 0, 2)

    def row_blocks(g2d):
        return g2d.reshape(N_DEV, g2d.shape[0] // N_DEV, g2d.shape[1])

    def ple_grads(l, de, xp, dpre):
        return [col_blocks(_wgrad(pb[l], de, f"wgrad_ple_in{l}", tn=1024), D // N_DEV),
                row_blocks(_wgrad(xp, dpre, f"wgrad_gate{l}", tn=1024))]

    def ffn_grads(l, xf, dhh, act, dhb):
        return [_wgrad(xf, dhh, f"wgrad_up{l}", tn=FF_BLK, out_blocked=True, a_transposed=True),
                row_blocks(_wgrad(act, dhb, f"wgrad_down{l}", tn=1024))]

    dh6, loss_blk, d_norm_final = _head(h6, row(norm_final), loss_target[0], tm)
    dh5, de1, xp1, dpre1, dbg1, dgp1 = _ple_bwd(dh6, h5, row(norm_ple[1]), wgate[1], row(ple_b_gate[1]), pb[1],
                                                 wple[1], tm)
    ex_ple1 = _send_start(ple_grads(1, de1, xp1, dpre1), True, "send_ple1")
    dh4, xf1, dhh1, act1, dhb5, dcw1, dcb1, dgf1 = _ffn_bwd(dh5, h4, hh1, after_token(ex_ple1["token"], row(norm_ffn[1])), wup[1],
                                                            cw[1], cb[1], wdown[1], tm)
    ex_ffn1 = _send_start(ffn_grads(1, xf1, dhh1, act1, dhb5), True, "send_ffn1")
    datt, dhb4 = _oproj_bwd(dh4, wo, tm)
    dq, dkv, dsink = _attn_bwd(q, kv, att, datt, after_token(ex_ffn1["token"], sinks.reshape(1, N_Q)).reshape(N_Q), rep, bias)
    dh3, xq, xkv, dgq, dgkv = _qkv_bwd(dh4, h3, dq, dkv, row(norm_mix[1]), wq, row(norm_kv), wkv, tm)
    ex_att = _send_start([row_blocks(_wgrad(att, dhb4, "wgrad_o", tn=1024)),
                              row_blocks(_wgrad(xq, dq, "wgrad_q", tn=1024)),
                              row_blocks(_wgrad(xkv, dkv, "wgrad_kv", tn=512))], True, "send_att")
    dh2, de0, xp0, dpre0, dbg0, dgp0 = _ple_bwd(dh3, h2, after_token(ex_att["token"], row(norm_ple[0])), wgate[0],
                                                 row(ple_b_gate[0]), pb[0], wple[0], tm)
    ex_ple0 = _send_start(ple_grads(0, de0, xp0, dpre0), True, "send_ple0")
    dh1, xf0, dhh0, act0, dhb2, dcw0, dcb0, dgf0 = _ffn_bwd(dh2, h1, hh0, after_token(ex_ple0["token"], row(norm_ffn[0])), wup[0],
                                                            cw[0], cb[0], wdown[0], tm)
    ex_ffn0 = _send_start(ffn_grads(0, xf0, dhh0, act0, dhb2), True, "send_ffn0")
    dx, xa, dz, gated, dhb1, dws, dbs, dgv, dga = _mixa_bwd(dh1, x0, z, after_token(ex_ffn0["token"], row(norm_mix[0])), win, gv,
                                                            ws, wst, bsb, wout, tm)
    ex_mix = _send_start([_wgrad(xa, dz, "wgrad_a_in", tn=2 * AW // N_DEV, out_blocked=True, a_transposed=True),
                              row_blocks(_wgrad(gated, dhb1, "wgrad_a_out", tn=1024))], True, "send_mix")

    small_g = {
        'norm_mix': jnp.concatenate([dga, dgq], axis=0), 'norm_ffn': jnp.concatenate([dgf0, dgf1], axis=0),
        'norm_ple': jnp.concatenate([dgp0, dgp1], axis=0), 'norm_kv': dgkv.reshape(D),
        'norm_final': d_norm_final.reshape(D), 'a_w_s': dws[None], 'a_b_s': dbs[None, :, :, 0],
        'b_sinks': dsink[:, 0].reshape(1, N_Q),
        'f_conv_b': jnp.stack([dcb0.reshape(2 * D_FF), dcb1.reshape(2 * D_FF)]),
        'ple_b_gate': jnp.concatenate([dbg0, dbg1], axis=0),
        'a_norm_v': dgv, 'f_conv_w': jnp.stack([dcw0, dcw1], axis=1),
    }
    names_all = _REPL + _SMALL_SHARDED
    full_shapes = [small_g[n].shape for n in names_all]
    ex_small = _send_start([after_token(ex_mix["token"], _pack([small_g[n] for n in names_all], 8))], False, "send_small")

    def as3(a):
        return a.reshape((-1,) + a.shape[-2:])

    def adam(name, parts):
        outs = _adamw_weight(parts, as3(w[name]), as3(mom_m[name]), as3(mom_v[name]), "adamw_" + name)
        return [o.reshape(w[name].shape) for o in outs]

    big = {}
    l_ple1 = _send_wait(ex_ple1, list(range(len(ex_ple1["lands"]))), ex_small["token"], "recv_ple1")
    l_ffn1 = _send_wait(ex_ffn1, list(range(len(ex_ffn1["lands"]))), l_ple1[0], "recv_ffn1")
    l_att = _send_wait(ex_att, list(range(len(ex_att["lands"]))), l_ffn1[0], "recv_att")
    big['b_w_o'], big['b_w_q'], big['w_kv'] = adam('b_w_o', [l_att[0]]), adam('b_w_q', [l_att[1]]), adam('w_kv', [l_att[2]])
    l_ple0 = _send_wait(ex_ple0, list(range(len(ex_ple0["lands"]))), big['w_kv'][0], "recv_ple0")
    big['ple_w_in'] = adam('ple_w_in', [l_ple0[0], l_ple1[0]])
    big['ple_w_gate'] = adam('ple_w_gate', [l_ple0[1], l_ple1[1]])
    l_ffn0 = _send_wait(ex_ffn0, list(range(len(ex_ffn0["lands"]))), big['ple_w_gate'][0], "recv_ffn0")
    big['f_w_up'] = adam('f_w_up', [l_ffn0[0], l_ffn1[0]])
    big['f_w_down'] = adam('f_w_down', [l_ffn0[1], l_ffn1[1]])
    l_mix = _send_wait(ex_mix, list(range(len(ex_mix["lands"]))), big['f_w_down'][0], "recv_mix")
    big['a_w_in'], big['a_w_out'] = adam('a_w_in', [l_mix[0]]), adam('a_w_out', [l_mix[1]])
    g_big, d_big, m_big, v_big = [{n: big[n][i] for n in _BIG} for i in range(4)]

    gsum = _sum8(_send_wait(ex_small, [0], big['a_w_out'][0], "recv_small")[0], "sum_small_grads")
    gs = dict(zip(names_all, _unpack(gsum, full_shapes, 8)))
    gs['a_norm_v'] = lax.dynamic_slice_in_dim(gs['a_norm_v'], me * (AW // N_DEV), AW // N_DEV, axis=1)
    gs['f_conv_w'] = lax.dynamic_index_in_dim(gs['f_conv_w'], me, axis=0, keepdims=False)
    gs = {n: gs[n].reshape(w[n].shape) for n in names_all}
    packs = lambda d: _pack([d[n] for n in names_all], 8)
    souts = _adamw_small(packs(gs), packs(w), packs(mom_m), packs(mom_v), "adamw_small")
    shp = [w[n].shape for n in names_all]
    d_s, m_s, v_s = [dict(zip(names_all, _unpack(o, shp, 8))) for o in souts]

    grads = {**g_big, **gs}
    delta = {**d_big, **d_s}
    new_m = {**m_big, **m_s}
    new_v = {**v_big, **v_s}
    loss = lax.psum(loss_blk[0, 0], ("x", "y", "c"))
    return (loss, dx[None], *[grads[n] for n in _PARAMS], *[delta[n] for n in _PARAMS],
            *[new_m[n] for n in _PARAMS], *[new_v[n] for n in _PARAMS])
```

```python
import functools

import numpy as np
import jax
import jax.numpy as jnp
from jax import lax
from jax.experimental import pallas as pl
from jax.experimental.pallas import tpu as pltpu

F32 = jnp.float32
BF = jnp.bfloat16

N_DEV = 8
D = 1024
AW = 1024
GROUPS = 8
CHUNK = 128
HD = 64
N_Q = 16
N_KV = 4
GQA = 4
BLOCK = 128
D_FF = 2816
FF_BLK = 704
N_FF_BLK = 4
PLE = 256
EPS = 1e-6
LR, B1, B2, ADAM_EPS, WD, STEP = 0.001, 0.9, 0.999, 1e-08, 0.01, 10
GELU_C0 = 0.7978845608028654
GELU_C1 = 0.044715
NEG = -1e30
ROW = 1024
VMEM_BIG = 56 * 1024 * 1024

_PARAMS = ['norm_mix', 'norm_ffn', 'norm_ple', 'norm_kv', 'norm_final', 'a_w_in', 'a_norm_v', 'a_w_s', 'a_b_s',
           'a_w_out', 'w_kv', 'b_w_q', 'b_sinks', 'b_w_o', 'f_w_up', 'f_conv_w', 'f_conv_b', 'f_w_down',
           'ple_w_in', 'ple_w_gate', 'ple_b_gate']
_BIG = ['a_w_in', 'a_w_out', 'w_kv', 'b_w_q', 'b_w_o', 'f_w_up', 'f_w_down', 'ple_w_in', 'ple_w_gate']
_SMALL_SHARDED = ['a_norm_v', 'f_conv_w']
_REPL = ['norm_mix', 'norm_ffn', 'norm_ple', 'norm_kv', 'norm_final', 'a_w_s', 'a_b_s', 'b_sinks', 'f_conv_b',
         'ple_b_gate']


def _mm(a, b):
    return jnp.dot(a, b, preferred_element_type=F32)


def _mm_nt(a, b):
    return lax.dot_general(a, b, (((1,), (1,)), ((), ())), preferred_element_type=F32)


def _mm_tn(a, b):
    return lax.dot_general(a, b, (((0,), (0,)), ((), ())), preferred_element_type=F32)


def _const_spec(shape, single=False):
    nd = len(shape)
    if single:
        return pl.BlockSpec(shape, lambda *_: (0,) * nd, pipeline_mode=pl.Buffered(1))
    return pl.BlockSpec(shape, lambda *_: (0,) * nd)


def _row_spec(tm, n):
    return pl.BlockSpec((tm, n), lambda i: (i, 0))


def _rms_fwd(x, g):
    r = lax.rsqrt(jnp.mean(x * x, axis=-1, keepdims=True) + EPS)
    xhat = x * r
    return xhat * g, xhat, r


def _rms_bwd(dy, xhat, r, g):
    a = dy * g
    dx = r * (a - xhat * jnp.mean(a * xhat, axis=-1, keepdims=True))
    return dx, jnp.sum(dy * xhat, axis=0, keepdims=True)


def _sigmoid(x):
    return 1.0 / (1.0 + jnp.exp(-x))


def _params(vmem=None):
    return pltpu.CompilerParams(dimension_semantics=("arbitrary",), vmem_limit_bytes=vmem)


def _tri_masks():
    t = lax.broadcasted_iota(jnp.int32, (CHUNK, CHUNK), 0)
    s = lax.broadcasted_iota(jnp.int32, (CHUNK, CHUNK), 1)
    return t >= s, s >= t


def _mixa_fwd(x, g, win, gv, ws, bsb, wout, tm):
    S = x.shape[0]
    nc = tm // CHUNK

    def body(x_ref, g_ref, win_ref, gv_ref, ws_ref, bsb_ref, wout_ref, h_ref, z_ref, gated):
        x = x_ref[...]
        xn, _, _ = _rms_fwd(x, g_ref[...])
        z = _mm(xn.astype(BF), win_ref[...])
        z_ref[...] = z.astype(BF)
        zg = 0.5 * z * (1.0 + jnp.tanh(GELU_C0 * (z + GELU_C1 * z * z * z)))
        u = zg[:, :AW]
        vn, _, _ = _rms_fwd(zg[:, AW:], gv_ref[...])
        vn = vn.astype(BF)
        tril, _ = _tri_masks()
        for h in range(GROUPS):
            wm = jnp.where(tril, ws_ref[h], jnp.zeros((), BF))
            for c in range(nc):
                rs, cs = slice(c * CHUNK, (c + 1) * CHUNK), slice(h * CHUNK, (h + 1) * CHUNK)
                s = _mm(wm, vn[rs, cs]) + bsb_ref[h]
                gated[rs, cs] = (u[rs, cs] * s).astype(BF)
        h_ref[...] = x + _mm(gated[...], wout_ref[...])

    return pl.pallas_call(
        body, name="mixa_fwd", grid=(S // tm,),
        in_specs=[_row_spec(tm, D), _const_spec((1, D)), _const_spec((D, 2 * AW)), _const_spec((1, AW)),
                  _const_spec((GROUPS, CHUNK, CHUNK)), _const_spec((GROUPS, CHUNK, CHUNK)), _const_spec((AW, D))],
        out_specs=[_row_spec(tm, D), _row_spec(tm, 2 * AW)],
        out_shape=[jax.ShapeDtypeStruct((S, D), F32), jax.ShapeDtypeStruct((S, 2 * AW), BF)],
        scratch_shapes=[pltpu.VMEM((tm, AW), BF)],
        compiler_params=_params(VMEM_BIG),
    )(x, g, win, gv, ws, bsb, wout)


def _mixa_bwd(dh, x, z, g, win, gv, ws, wst, bsb, wout, tm):
    S = x.shape[0]
    nt = S // tm
    nc = tm // CHUNK

    def body(dh_ref, x_ref, z_ref, g_ref, win_ref, gv_ref, ws_ref, wst_ref, bsb_ref, wout_ref,
             dx_ref, xn_ref, dz_ref, gated_ref, dhb_ref, dws_ref, dbs_ref, dgv_ref, dg_ref, du_scr, dvn_scr):
        i = pl.program_id(0)

        @pl.when(i == 0)
        def _():
            dws_ref[...] = jnp.zeros_like(dws_ref)
            dbs_ref[...] = jnp.zeros_like(dbs_ref)
            dgv_ref[...] = jnp.zeros_like(dgv_ref)
            dg_ref[...] = jnp.zeros_like(dg_ref)

        x = x_ref[...]
        gmix = g_ref[...]
        xn, xhat, r = _rms_fwd(x, gmix)
        xn_ref[...] = xn.T.astype(BF)
        z = z_ref[...].astype(F32)
        z2 = z * z
        t = jnp.tanh(GELU_C0 * (z + GELU_C1 * z2 * z))
        zg = 0.5 * z * (1.0 + t)
        gp = 0.5 * (1.0 + t) + 0.5 * z * (1.0 - t * t) * (GELU_C0 * (1.0 + 3.0 * GELU_C1 * z2))
        u = zg[:, :AW]
        gvv = gv_ref[...]
        vn, vhat, rv = _rms_fwd(zg[:, AW:], gvv)
        vn = vn.astype(BF)
        dh = dh_ref[...]
        dhb = dh.astype(BF)
        dhb_ref[...] = dhb
        dgated = _mm_nt(dhb, wout_ref[...])
        tril, triu = _tri_masks()
        for h in range(GROUPS):
            wm = jnp.where(tril, ws_ref[h], jnp.zeros((), BF))
            wmt = jnp.where(triu, wst_ref[h], jnp.zeros((), BF))
            for c in range(nc):
                rs, cs = slice(c * CHUNK, (c + 1) * CHUNK), slice(h * CHUNK, (h + 1) * CHUNK)
                vnb = vn[rs, cs]
                s = _mm(wm, vnb) + bsb_ref[h]
                gated_ref[rs, cs] = (u[rs, cs] * s).astype(BF)
                dgt = dgated[rs, cs]
                du_scr[rs, cs] = dgt * s
                ds = dgt * u[rs, cs]
                dsb = ds.astype(BF)
                dws_ref[h] += jnp.where(tril, _mm_nt(dsb, vnb), 0.0)
                dbs_ref[h] += ds
                dvn_scr[rs, cs] = _mm(wmt, dsb)
        dvn = dvn_scr[...]
        dv, dgv = _rms_bwd(dvn, vhat, rv, gvv)
        dgv_ref[...] += dgv
        dz_ref[:, :AW] = (du_scr[...] * gp[:, :AW]).astype(BF)
        dz_ref[:, AW:] = (dv * gp[:, AW:]).astype(BF)
        dxn = _mm_nt(dz_ref[...], win_ref[...])
        dxr, dg = _rms_bwd(dxn, xhat, r, gmix)
        dg_ref[...] += dg
        dx_ref[...] = dh + dxr

        @pl.when(i == nt - 1)
        def _():
            for h in range(GROUPS):
                dbs_ref[h] = jnp.broadcast_to(jnp.sum(dbs_ref[h], axis=-1, keepdims=True), (CHUNK, CHUNK))

    gshape = (GROUPS, CHUNK, CHUNK)
    return pl.pallas_call(
        body, name="mixa_bwd", grid=(nt,),
        in_specs=[_row_spec(tm, D), _row_spec(tm, D), _row_spec(tm, 2 * AW), _const_spec((1, D)),
                  _const_spec((D, 2 * AW)), _const_spec((1, AW)), _const_spec(gshape), _const_spec(gshape),
                  _const_spec(gshape), _const_spec((AW, D))],
        out_specs=[_row_spec(tm, D), pl.BlockSpec((D, tm), lambda i: (0, i)), _row_spec(tm, 2 * AW), _row_spec(tm, AW),
                   _row_spec(tm, D), _const_spec(gshape), _const_spec(gshape), _const_spec((1, AW)), _const_spec((1, D))],
        out_shape=[jax.ShapeDtypeStruct((S, D), F32), jax.ShapeDtypeStruct((D, S), BF),
                   jax.ShapeDtypeStruct((S, 2 * AW), BF), jax.ShapeDtypeStruct((S, AW), BF),
                   jax.ShapeDtypeStruct((S, D), BF), jax.ShapeDtypeStruct(gshape, F32),
                   jax.ShapeDtypeStruct(gshape, F32), jax.ShapeDtypeStruct((1, AW), F32),
                   jax.ShapeDtypeStruct((1, D), F32)],
        scratch_shapes=[pltpu.VMEM((tm, AW), F32), pltpu.VMEM((tm, AW), F32)],
        compiler_params=_params(VMEM_BIG),
    )(dh, x, z, g, win, gv, ws, wst, bsb, wout)


def _shift_down(x, prev8, k):
    n, c = x.shape[0] // 8, x.shape[1]
    r = pltpu.roll(x.reshape(n, 8, c), k, axis=1)
    before = jnp.concatenate([pltpu.roll(prev8.reshape(1, 8, c), k, axis=1), r[:-1]], axis=0)
    sub = lax.broadcasted_iota(jnp.int32, (1, 8, c), 1)
    return jnp.where(sub < k, before, r).reshape(x.shape)


def _shift_up(x, next8, k):
    n, c = x.shape[0] // 8, x.shape[1]
    r = pltpu.roll(x.reshape(n, 8, c), 8 - k, axis=1)
    after = jnp.concatenate([r[1:], pltpu.roll(next8.reshape(1, 8, c), 8 - k, axis=1)], axis=0)
    sub = lax.broadcasted_iota(jnp.int32, (1, 8, c), 1)
    return jnp.where(sub >= 8 - k, after, r).reshape(x.shape)


def _ffn_fwd(h, g, wup, cw, cb, wdown, tm):
    S = h.shape[0]

    def body(h_ref, g_ref, wup_ref, cw_ref, cb_ref, wdown_ref, o_ref, hh_ref, cc_ref, sh):
        i = pl.program_id(0)

        @pl.when(i == 0)
        def _():
            sh[...] = jnp.zeros_like(sh)

        x = h_ref[...]
        xn, _, _ = _rms_fwd(x, g_ref[...])
        xn = xn.astype(BF)
        acc = x
        up = lambda d: [_mm(xn, wup_ref[d]), _mm(xn, wup_ref[N_FF_BLK + d])]
        ahead = up(0)
        for d in range(N_FF_BLK):
            cs = []
            hbs = ahead
            if d + 1 < N_FF_BLK:
                ahead = up(d + 1)
            for blk, hb in zip((d, N_FF_BLK + d), hbs):
                hh_ref[blk] = hb.astype(BF)
                w, prev8 = cw_ref[blk], sh[blk]
                c = cb_ref[blk] + w[0:1] * _shift_down(hb, prev8, 2) + w[1:2] * _shift_down(hb, prev8, 1) + w[2:3] * hb
                sh[blk] = hb[tm - 8:, :]
                cc_ref[blk] = c.astype(BF)
                cs.append(c)
            act = cs[0] * _sigmoid(cs[0]) * cs[1]
            acc = acc + _mm(act.astype(BF), wdown_ref[d * FF_BLK:(d + 1) * FF_BLK, :])
        o_ref[...] = acc

    blk3 = pl.BlockSpec((N_DEV, tm, FF_BLK), lambda i: (0, i, 0))
    return pl.pallas_call(
        body, name="ffn_fwd", grid=(S // tm,),
        in_specs=[_row_spec(tm, D), _const_spec((1, D)), _const_spec((N_DEV, D, FF_BLK), True),
                  _const_spec((N_DEV, 3, FF_BLK)), _const_spec((N_DEV, 1, FF_BLK)), _const_spec((D_FF, D), True)],
        out_specs=[_row_spec(tm, D), blk3, blk3],
        out_shape=[jax.ShapeDtypeStruct((S, D), F32), jax.ShapeDtypeStruct((N_DEV, S, FF_BLK), BF),
                   jax.ShapeDtypeStruct((N_DEV, S, FF_BLK), BF)],
        scratch_shapes=[pltpu.VMEM((N_DEV, 8, FF_BLK), F32)],
        compiler_params=_params(VMEM_BIG),
    )(h, g, wup, cw, cb, wdown)


def _ffn_bwd(dh, h, hh, cc, g, wup, cw, wdown, tm):
    S = h.shape[0]
    nt = S // tm

    def rev(i):
        return (nt - 1 - i, 0)

    def body(dh_ref, h_ref, hh_ref, cc_ref, g_ref, wup_ref, cw_ref, wdown_ref,
             dx_ref, xn_ref, dhh_ref, act_ref, dhb_ref, dcw_ref, dcb_ref, dg_ref, sh):
        i = pl.program_id(0)

        @pl.when(i == 0)
        def _():
            sh[...] = jnp.zeros_like(sh)
            dcw_ref[...] = jnp.zeros_like(dcw_ref)
            dcb_ref[...] = jnp.zeros_like(dcb_ref)
            dg_ref[...] = jnp.zeros_like(dg_ref)

        x = h_ref[...]
        gain = g_ref[...]
        xn, xhat, r = _rms_fwd(x, gain)
        xn_ref[...] = xn.T.astype(BF)
        dh = dh_ref[...]
        dhb = dh.astype(BF)
        dhb_ref[...] = dhb
        dxn = jnp.zeros((tm, D), F32)
        dact_of = lambda d: _mm_nt(dhb, wdown_ref[d * FF_BLK:(d + 1) * FF_BLK, :])
        ahead = dact_of(0)
        for d in range(N_FF_BLK):
            blks = (d, N_FF_BLK + d)
            dact = ahead
            if d + 1 < N_FF_BLK:
                ahead = dact_of(d + 1)
            cg, cu = cc_ref[blks[0]].astype(F32), cc_ref[blks[1]].astype(F32)
            sg = _sigmoid(cg)
            sil = cg * sg
            act_ref[d] = (sil * cu).astype(BF)
            dcs = (dact * cu * (sg + sil - sil * sg), dact * sil)
            for blk, dc in zip(blks, dcs):
                next8 = sh[blk]
                dp1, dp2 = _shift_up(dc, next8, 1), _shift_up(dc, next8, 2)
                hb = hh_ref[blk].astype(F32)
                dcb_ref[blk] += jnp.sum(dc, axis=0, keepdims=True)
                dcw_ref[blk, 0:1, :] += jnp.sum(dp2 * hb, axis=0, keepdims=True)
                dcw_ref[blk, 1:2, :] += jnp.sum(dp1 * hb, axis=0, keepdims=True)
                dcw_ref[blk, 2:3, :] += jnp.sum(dc * hb, axis=0, keepdims=True)
                w = cw_ref[blk]
                dhh = (w[2:3] * dc + w[1:2] * dp1 + w[0:1] * dp2).astype(BF)
                sh[blk] = dc[0:8, :]
                dhh_ref[blk] = dhh
                dxn = dxn + _mm_nt(dhh, wup_ref[blk])
        dxr, dg = _rms_bwd(dxn, xhat, r, gain)
        dg_ref[...] += dg
        dx_ref[...] = dh + dxr

    blk3 = lambda i: (0, nt - 1 - i, 0)
    return pl.pallas_call(
        body, name="ffn_bwd", grid=(nt,),
        in_specs=[pl.BlockSpec((tm, D), rev), pl.BlockSpec((tm, D), rev),
                  pl.BlockSpec((N_DEV, tm, FF_BLK), blk3), pl.BlockSpec((N_DEV, tm, FF_BLK), blk3),
                  _const_spec((1, D)), _const_spec((N_DEV, D, FF_BLK), True), _const_spec((N_DEV, 3, FF_BLK)),
                  _const_spec((D_FF, D), True)],
        out_specs=[pl.BlockSpec((tm, D), rev), pl.BlockSpec((D, tm), lambda i: (0, nt - 1 - i)),
                   pl.BlockSpec((N_DEV, tm, FF_BLK), blk3), pl.BlockSpec((N_FF_BLK, tm, FF_BLK), blk3),
                   pl.BlockSpec((tm, D), rev), _const_spec((N_DEV, 3, FF_BLK)), _const_spec((N_DEV, 1, FF_BLK)),
                   _const_spec((1, D))],
        out_shape=[jax.ShapeDtypeStruct((S, D), F32), jax.ShapeDtypeStruct((D, S), BF),
                   jax.ShapeDtypeStruct((N_DEV, S, FF_BLK), BF), jax.ShapeDtypeStruct((N_FF_BLK, S, FF_BLK), BF),
                   jax.ShapeDtypeStruct((S, D), BF), jax.ShapeDtypeStruct((N_DEV, 3, FF_BLK), F32),
                   jax.ShapeDtypeStruct((N_DEV, 1, FF_BLK), F32), jax.ShapeDtypeStruct((1, D), F32)],
        scratch_shapes=[pltpu.VMEM((N_DEV, 8, FF_BLK), F32)],
        compiler_params=_params(VMEM_BIG),
    )(dh, h, hh, cc, g, wup, cw, wdown)


def _ple_fwd(h, g, wgate, bgate, p, wple, tm):
    S = h.shape[0]

    def body(h_ref, g_ref, wg_ref, bg_ref, p_ref, wp_ref, o_ref):
        x = h_ref[...]
        xn, _, _ = _rms_fwd(x, g_ref[...])
        gate = _sigmoid(_mm(xn.astype(BF), wg_ref[...]) + bg_ref[...])
        o_ref[...] = x + _mm(p_ref[...], wp_ref[...]) * gate

    return pl.pallas_call(
        body, name="ple_fwd", grid=(S // tm,),
        in_specs=[_row_spec(tm, D), _const_spec((1, D)), _const_spec((D, D)), _const_spec((1, D)),
                  _row_spec(tm, PLE), _const_spec((PLE, D))],
        out_specs=_row_spec(tm, D), out_shape=jax.ShapeDtypeStruct((S, D), F32),
        compiler_params=_params(),
    )(h, g, wgate, bgate, p, wple)


def _ple_bwd(dh, h, g, wgate, bgate, p, wple, tm):
    S = h.shape[0]

    def body(dh_ref, h_ref, g_ref, wg_ref, bg_ref, p_ref, wp_ref, dx_ref, de_ref, xn_ref, dpre_ref, dbg_ref, dg_ref):
        i = pl.program_id(0)

        @pl.when(i == 0)
        def _():
            dbg_ref[...] = jnp.zeros_like(dbg_ref)
            dg_ref[...] = jnp.zeros_like(dg_ref)

        x = h_ref[...]
        gain = g_ref[...]
        xn, xhat, r = _rms_fwd(x, gain)
        xnb = xn.astype(BF)
        xn_ref[...] = xnb
        gate = _sigmoid(_mm(xnb, wg_ref[...]) + bg_ref[...])
        e = _mm(p_ref[...], wp_ref[...])
        dh = dh_ref[...]
        de_ref[...] = (dh * gate).astype(BF)
        dpre = dh * e * gate * (1.0 - gate)
        dpreb = dpre.astype(BF)
        dpre_ref[...] = dpreb
        dbg_ref[...] += jnp.sum(dpre, axis=0, keepdims=True)
        dxr, dg = _rms_bwd(_mm_nt(dpreb, wg_ref[...]), xhat, r, gain)
        dg_ref[...] += dg
        dx_ref[...] = dh + dxr

    return pl.pallas_call(
        body, name="ple_bwd", grid=(S // tm,),
        in_specs=[_row_spec(tm, D), _row_spec(tm, D), _const_spec((1, D)), _const_spec((D, D)), _const_spec((1, D)),
                  _row_spec(tm, PLE), _const_spec((PLE, D))],
        out_specs=[_row_spec(tm, D), _row_spec(tm, D), _row_spec(tm, D), _row_spec(tm, D), _const_spec((1, D)),
                   _const_spec((1, D))],
        out_shape=[jax.ShapeDtypeStruct((S, D), F32), jax.ShapeDtypeStruct((S, D), BF),
                   jax.ShapeDtypeStruct((S, D), BF), jax.ShapeDtypeStruct((S, D), BF),
                   jax.ShapeDtypeStruct((1, D), F32), jax.ShapeDtypeStruct((1, D), F32)],
        compiler_params=_params(),
    )(dh, h, g, wgate, bgate, p, wple)


def _qkv_fwd(h, gq, wq, gkv, wkv, tm):
    S = h.shape[0]
    nkv = wkv.shape[1]

    def body(h_ref, gq_ref, wq_ref, gkv_ref, wkv_ref, q_ref, kv_ref):
        x = h_ref[...]
        _, xhat, _ = _rms_fwd(x, gq_ref[...])
        q_ref[...] = _mm((xhat * gq_ref[...]).astype(BF), wq_ref[...]).astype(BF)
        kv_ref[...] = _mm((xhat * gkv_ref[...]).astype(BF), wkv_ref[...]).astype(BF)

    return pl.pallas_call(
        body, name="qkv_fwd", grid=(S // tm,),
        in_specs=[_row_spec(tm, D), _const_spec((1, D)), _const_spec((D, D)), _const_spec((1, D)),
                  _const_spec((D, nkv))],
        out_specs=[_row_spec(tm, D), _row_spec(tm, nkv)],
        out_shape=[jax.ShapeDtypeStruct((S, D), BF), jax.ShapeDtypeStruct((S, nkv), BF)],
        compiler_params=_params(),
    )(h, gq, wq, gkv, wkv)


def _qkv_bwd(dh, h, dq, dkv, gq, wq, gkv, wkv, tm):
    S = h.shape[0]
    nkv = wkv.shape[1]

    def body(dh_ref, h_ref, dq_ref, dkv_ref, gq_ref, wq_ref, gkv_ref, wkv_ref,
             dx_ref, xq_ref, xkv_ref, dgq_ref, dgkv_ref):
        i = pl.program_id(0)

        @pl.when(i == 0)
        def _():
            dgq_ref[...] = jnp.zeros_like(dgq_ref)
            dgkv_ref[...] = jnp.zeros_like(dgkv_ref)

        x = h_ref[...]
        gq_, gkv_ = gq_ref[...], gkv_ref[...]
        _, xhat, r = _rms_fwd(x, gq_)
        xq_ref[...] = (xhat * gq_).astype(BF)
        xkv_ref[...] = (xhat * gkv_).astype(BF)
        d1, dg1 = _rms_bwd(_mm_nt(dq_ref[...], wq_ref[...]), xhat, r, gq_)
        d2, dg2 = _rms_bwd(_mm_nt(dkv_ref[...], wkv_ref[...]), xhat, r, gkv_)
        dgq_ref[...] += dg1
        dgkv_ref[...] += dg2
        dx_ref[...] = dh_ref[...] + d1 + d2

    return pl.pallas_call(
        body, name="qkv_bwd", grid=(S // tm,),
        in_specs=[_row_spec(tm, D), _row_spec(tm, D), _row_spec(tm, D), _row_spec(tm, nkv), _const_spec((1, D)),
                  _const_spec((D, D)), _const_spec((1, D)), _const_spec((D, nkv))],
        out_specs=[_row_spec(tm, D), _row_spec(tm, D), _row_spec(tm, D), _const_spec((1, D)), _const_spec((1, D))],
        out_shape=[jax.ShapeDtypeStruct((S, D), F32), jax.ShapeDtypeStruct((S, D), BF),
                   jax.ShapeDtypeStruct((S, D), BF), jax.ShapeDtypeStruct((1, D), F32),
                   jax.ShapeDtypeStruct((1, D), F32)],
        compiler_params=_params(),
    )(dh, h, dq, dkv, gq, wq, gkv, wkv)


def _oproj_fwd(a, w, res, tm):
    S = a.shape[0]

    def body(a_ref, w_ref, r_ref, o_ref):
        o_ref[...] = r_ref[...] + _mm(a_ref[...], w_ref[...])

    return pl.pallas_call(
        body, name="oproj_fwd", grid=(S // tm,),
        in_specs=[_row_spec(tm, D), _const_spec((D, D)), _row_spec(tm, D)],
        out_specs=_row_spec(tm, D), out_shape=jax.ShapeDtypeStruct((S, D), F32),
        compiler_params=_params(),
    )(a, w, res)


def _oproj_bwd(dh, w, tm):
    S = dh.shape[0]

    def body(dh_ref, w_ref, da_ref, dhb_ref):
        dhb = dh_ref[...].astype(BF)
        dhb_ref[...] = dhb
        da_ref[...] = _mm_nt(dhb, w_ref[...]).astype(BF)

    return pl.pallas_call(
        body, name="oproj_bwd", grid=(S // tm,),
        in_specs=[_row_spec(tm, D), _const_spec((D, D))],
        out_specs=[_row_spec(tm, D), _row_spec(tm, D)],
        out_shape=[jax.ShapeDtypeStruct((S, D), BF), jax.ShapeDtypeStruct((S, D), BF)],
        compiler_params=_params(),
    )(dh, w)


def _alibi_slope(hq):
    return float(np.float32(2.0 ** (-8.0 * (hq + 1) / N_Q)))


def _attn_consts():
    W = N_KV * HD
    r = np.zeros((W, N_KV * W), np.float32)
    for kh in range(N_KV):
        for c in range(W):
            r[kh * HD + c % HD, kh * W + c] = 1.0
    qi = np.arange(BLOCK)[:, None]
    kj = np.arange(2 * BLOCK)[None, :]
    dist = qi + BLOCK - kj
    band = (dist >= 0) & (dist < BLOCK)
    bias = np.full((2, N_KV, GQA * BLOCK, 2 * BLOCK), NEG, np.float32)
    for later in (0, 1):
        valid = band & ((kj >= BLOCK) | bool(later))
        for kh in range(N_KV):
            for g in range(GQA):
                slope = np.float32(_alibi_slope(kh * GQA + g))
                bias[later, kh, g * BLOCK:(g + 1) * BLOCK] = np.where(valid, -slope * dist.astype(np.float32), NEG)
    lane_head = np.arange(W)[None, :] // HD
    maskq = (np.arange(GQA * BLOCK)[:, None] // BLOCK == lane_head).astype(np.float32)
    maskk = (np.arange(GQA * 2 * BLOCK)[:, None] // (2 * BLOCK) == lane_head).astype(np.float32)
    head_sum = (lane_head.T == np.arange(GQA * BLOCK)[None, :] // BLOCK).astype(np.float32)
    return dict(rep=jnp.asarray(r, BF), bias=jnp.asarray(bias), maskq=jnp.asarray(maskq, BF),
                maskk=jnp.asarray(maskk, BF), head_sum=jnp.asarray(head_sum, BF))


def _sink_col(sink_ref, kh):
    rb = lax.broadcasted_iota(jnp.int32, (GQA * BLOCK, 1), 0) >> 7
    col = jnp.full((GQA * BLOCK, 1), sink_ref[kh * GQA + GQA - 1], F32)
    for g in range(GQA - 1):
        col = jnp.where(rb == g, sink_ref[kh * GQA + g], col)
    return col


def _stack4(a):
    return jnp.concatenate([a] * GQA, axis=0)


def _rows_to_lanes(a):
    return jnp.concatenate([a[g * BLOCK:(g + 1) * BLOCK] for g in range(GQA)], axis=1)


def _lanes_to_rows(a, n):
    return jnp.concatenate([a[:, g * n:(g + 1) * n] for g in range(GQA)], axis=0)


def _scores(qg, k4, bias, sink_col, maskq):
    qs = _stack4(qg) * maskq
    s = _mm_nt(qs, k4) * (HD ** -0.5) + bias
    m = jnp.maximum(jnp.max(s, axis=-1, keepdims=True), sink_col)
    return qs, jnp.exp(s - m), jnp.exp(sink_col - m)


def _bands(kvc_ref, kvp_ref, rep_ref):
    W = N_KV * HD
    kvp, kvc = kvp_ref[...], kvc_ref[...]
    kband = jnp.concatenate([kvp[:, :W], kvc[:, :W]], axis=0)
    vband = jnp.concatenate([kvp[:, W:], kvc[:, W:]], axis=0)
    return _mm(kband, rep_ref[...]).astype(BF), _mm(vband, rep_ref[...]).astype(BF)


def _attn_fwd(q, kv, sinks, consts):
    S = q.shape[0]
    nb = S // BLOCK
    W = N_KV * HD

    def body(sink_ref, q_ref, kvc_ref, kvp_ref, rep_ref, bias_ref, maskq_ref, maskk_ref, o_ref):
        maskq, maskk = maskq_ref[...], maskk_ref[...]
        lane_head = lax.broadcasted_iota(jnp.int32, (BLOCK, W), 1) >> 6
        k4all, v4all = _bands(kvc_ref, kvp_ref, rep_ref)
        for kh in range(N_KV):
            cols = slice(kh * W, (kh + 1) * W)
            _, e, es = _scores(q_ref[:, cols], k4all[:, cols], bias_ref[kh], _sink_col(sink_ref, kh), maskq)
            vbd = jnp.concatenate([_stack4(v4all[:, cols]) * maskk, maskk], axis=1)
            nd = _mm(_rows_to_lanes(e.astype(BF)), vbd)
            es_l = jnp.broadcast_to(es[(GQA - 1) * BLOCK:], (BLOCK, W))
            for g in range(GQA - 1):
                es_l = jnp.where(lane_head == g, es[g * BLOCK:(g + 1) * BLOCK], es_l)
            o_ref[:, cols] = (nd[:, :W] / (nd[:, W:] + es_l)).astype(BF)

    return pl.pallas_call(
        body, name="attn_fwd", grid=(nb,),
        in_specs=[pl.BlockSpec(memory_space=pltpu.SMEM), _row_spec(BLOCK, D), _row_spec(BLOCK, 2 * W),
                  pl.BlockSpec((BLOCK, 2 * W), lambda i: (jnp.maximum(i - 1, 0), 0)), _const_spec((W, N_KV * W)),
                  pl.BlockSpec((None, N_KV, GQA * BLOCK, 2 * BLOCK), lambda i: (jnp.minimum(i, 1), 0, 0, 0)),
                  _const_spec((GQA * BLOCK, W)), _const_spec((GQA * 2 * BLOCK, W))],
        out_specs=_row_spec(BLOCK, D), out_shape=jax.ShapeDtypeStruct((S, D), BF),
        compiler_params=_params(),
    )(sinks, q, kv, kv, consts["rep"], consts["bias"], consts["maskq"], consts["maskk"])


def _attn_bwd(q, kv, att, do, sinks, consts):
    S = q.shape[0]
    nb = S // BLOCK
    W = N_KV * HD

    def qmap(i):
        return (jnp.minimum(i, nb - 1), 0)

    def body(sink_ref, q_ref, kvc_ref, kvp_ref, att_ref, do_ref, rep_ref, bias_ref, maskq_ref, maskk_ref, hsum_ref,
             dq_ref, dkv_ref, dsink_ref, carry, sink_acc):
        i = pl.program_id(0)

        @pl.when(i == 0)
        def _():
            sink_acc[...] = jnp.zeros_like(sink_acc)
            carry[...] = jnp.zeros_like(carry)

        @pl.when(i < nb)
        def _():
            maskq, maskk, head_sum = maskq_ref[...], maskk_ref[...], hsum_ref[...]
            ones_k = jnp.ones((2 * BLOCK, BLOCK), BF)
            k4all, v4all = _bands(kvc_ref, kvp_ref, rep_ref)
            dk4s, dv4s = [], []
            for kh in range(N_KV):
                cols = slice(kh * W, (kh + 1) * W)
                k4, v4 = k4all[:, cols], v4all[:, cols]
                qs, e, es = _scores(q_ref[:, cols], k4, bias_ref[kh], _sink_col(sink_ref, kh), maskq)
                inv = 1.0 / (_mm(e.astype(BF), ones_k) + es)
                p = e * jnp.concatenate([inv, inv], axis=1)
                dog = do_ref[:, cols]
                dos = _stack4(dog) * maskq
                dp = _mm_nt(dos, v4)
                prod = (dog.astype(F32) * att_ref[:, cols].astype(F32)).astype(BF)
                delta = _lanes_to_rows(_mm(prod, head_sum), BLOCK)
                ds = (p * (dp - jnp.concatenate([delta, delta], axis=1))).astype(BF)
                sink_acc[kh] += es * inv * delta
                kbd = _stack4(k4) * maskk
                dq_ref[:, cols] = (_mm(_rows_to_lanes(ds), kbd) * (HD ** -0.5)).astype(BF)
                dk4s.append((_mm_tn(ds, qs) * (HD ** -0.5)).astype(BF))
                dv4s.append(_mm_tn(p.astype(BF), dos).astype(BF))
            dkband = _mm_nt(jnp.concatenate(dk4s, axis=1), rep_ref[...])
            dvband = _mm_nt(jnp.concatenate(dv4s, axis=1), rep_ref[...])
            dkv_ref[:, :W] = (carry[:, :W] + dkband[:BLOCK]).astype(BF)
            dkv_ref[:, W:] = (carry[:, W:] + dvband[:BLOCK]).astype(BF)
            carry[:, :W] = dkband[BLOCK:]
            carry[:, W:] = dvband[BLOCK:]

        @pl.when(i == nb)
        def _():
            dkv_ref[...] = carry[...].astype(BF)
            for hq in range(N_Q):
                kh, g = divmod(hq, GQA)
                dsink_ref[hq:hq + 1, :] = -jnp.sum(sink_acc[kh, g * BLOCK:(g + 1) * BLOCK, :], axis=0, keepdims=True)

    return pl.pallas_call(
        body, name="attn_bwd", grid=(nb + 1,),
        in_specs=[pl.BlockSpec(memory_space=pltpu.SMEM), pl.BlockSpec((BLOCK, D), qmap),
                  pl.BlockSpec((BLOCK, 2 * W), qmap),
                  pl.BlockSpec((BLOCK, 2 * W), lambda i: (jnp.maximum(jnp.minimum(i, nb - 1) - 1, 0), 0)),
                  pl.BlockSpec((BLOCK, D), qmap), pl.BlockSpec((BLOCK, D), qmap), _const_spec((W, N_KV * W)),
                  pl.BlockSpec((None, N_KV, GQA * BLOCK, 2 * BLOCK), lambda i: (jnp.minimum(i, 1), 0, 0, 0)),
                  _const_spec((GQA * BLOCK, W)), _const_spec((GQA * 2 * BLOCK, W)), _const_spec((W, GQA * BLOCK))],
        out_specs=[pl.BlockSpec((BLOCK, D), qmap),
                   pl.BlockSpec((BLOCK, 2 * W), lambda i: (jnp.maximum(i - 1, 0), 0)),
                   _const_spec((N_Q, BLOCK))],
        out_shape=[jax.ShapeDtypeStruct((S, D), BF), jax.ShapeDtypeStruct((S, 2 * W), BF),
                   jax.ShapeDtypeStruct((N_Q, BLOCK), F32)],
        scratch_shapes=[pltpu.VMEM((BLOCK, 2 * W), F32), pltpu.VMEM((N_KV, GQA * BLOCK, BLOCK), F32)],
        compiler_params=_params(),
    )(sinks, q, kv, kv, att, do, consts["rep"], consts["bias"], consts["maskq"], consts["maskk"], consts["head_sum"])


def _head(h, g, target, tm):
    S = h.shape[0]

    def body(h_ref, g_ref, t_ref, dh_ref, loss_ref, dg_ref):
        i = pl.program_id(0)

        @pl.when(i == 0)
        def _():
            loss_ref[...] = jnp.zeros_like(loss_ref)
            dg_ref[...] = jnp.zeros_like(dg_ref)

        gain = g_ref[...]
        y, xhat, r = _rms_fwd(h_ref[...], gain)
        err = y - t_ref[...]
        tile = 0.5 * jnp.sum(jnp.sum(err * err, axis=-1, keepdims=True) * (1.0 / D), axis=0, keepdims=True)
        loss_ref[...] += jnp.broadcast_to(tile, loss_ref.shape)
        dx, dg = _rms_bwd(err * (1.0 / D), xhat, r, gain)
        dg_ref[...] += dg
        dh_ref[...] = dx

    return pl.pallas_call(
        body, name="loss_head", grid=(S // tm,),
        in_specs=[_row_spec(tm, D), _const_spec((1, D)), _row_spec(tm, D)],
        out_specs=[_row_spec(tm, D), _const_spec((8, 128)), _const_spec((1, D))],
        out_shape=[jax.ShapeDtypeStruct((S, D), F32), jax.ShapeDtypeStruct((8, 128), F32),
                   jax.ShapeDtypeStruct((1, D), F32)],
        compiler_params=_params(),
    )(h, g, target)


def _wgrad(a, b, name, *, tn, out_blocked=False, a_transposed=False, ts=1024):
    a_blocked, b_blocked = a.ndim == 3, b.ndim == 3
    na = a.shape[0] if a_blocked else 1
    S = b.shape[-2]
    K = a.shape[0] if a_transposed else a.shape[-1]
    nbk = b.shape[0] if b_blocked else b.shape[1] // tn
    ts = min(ts, S)
    ns = S // ts
    assert not (a_blocked and (out_blocked or a_transposed))

    def body(a_ref, b_ref, o_ref, acc):
        s = pl.program_id(2)

        @pl.when(s == 0)
        def _():
            acc[...] = jnp.zeros_like(acc)

        if a_transposed:
            acc[...] += _mm(a_ref[...], b_ref[...])
        else:
            acc[...] += _mm_tn(a_ref[...], b_ref[...])

        @pl.when(s == ns - 1)
        def _():
            o_ref[...] = acc[...].astype(o_ref.dtype)

    if a_transposed:
        a_spec = pl.BlockSpec((K, ts), lambda i, j, s: (0, s))
    elif a_blocked:
        a_spec = pl.BlockSpec((None, ts, K), lambda i, j, s: (i, s, 0))
    else:
        a_spec = pl.BlockSpec((ts, K), lambda i, j, s: (s, 0))
    b_spec = (pl.BlockSpec((None, ts, tn), lambda i, j, s: (j, s, 0)) if b_blocked
              else pl.BlockSpec((ts, tn), lambda i, j, s: (s, j)))
    if out_blocked:
        o_spec = pl.BlockSpec((None, K, tn), lambda i, j, s: (j, 0, 0))
        o_shape = jax.ShapeDtypeStruct((nbk, K, tn), BF)
    else:
        o_spec = pl.BlockSpec((K, tn), lambda i, j, s: (i, j))
        o_shape = jax.ShapeDtypeStruct((na * K, nbk * tn), BF)
    return pl.pallas_call(
        body, name=name, grid=(na, nbk, ns), in_specs=[a_spec, b_spec], out_specs=o_spec, out_shape=o_shape,
        scratch_shapes=[pltpu.VMEM((K, tn), F32)],
        compiler_params=pltpu.CompilerParams(dimension_semantics=("arbitrary", "arbitrary", "arbitrary"),
                                             vmem_limit_bytes=VMEM_BIG),
    )(a, b)


def _adamw_math(w, g, m, v):
    m = B1 * m + (1.0 - B1) * g
    v = B2 * v + (1.0 - B2) * (g * g)
    m_hat = m / (1.0 - B1 ** STEP)
    v_hat = v / (1.0 - B2 ** STEP)
    return -LR * (m_hat / (jnp.sqrt(v_hat) + ADAM_EPS) + WD * w), m, v


def _sum8(parts, name):
    R = parts.shape[1]

    def body(p_ref, o_ref):
        g = p_ref[0]
        for s in range(1, N_DEV):
            g = g + p_ref[s]
        o_ref[...] = g

    return pl.pallas_call(
        body, name=name, grid=(1,),
        in_specs=[pl.BlockSpec((N_DEV, R, ROW), lambda i: (0, 0, 0))],
        out_specs=pl.BlockSpec((R, ROW), lambda i: (0, 0)), out_shape=jax.ShapeDtypeStruct((R, ROW), F32),
        compiler_params=_params(),
    )(parts)


def _adamw_small(g, w, m, v, name):
    R = w.shape[0]

    def body(g_ref, w_ref, m_ref, v_ref, d_ref, nm_ref, nv_ref):
        d_ref[...], nm_ref[...], nv_ref[...] = _adamw_math(w_ref[...], g_ref[...], m_ref[...], v_ref[...])

    spec = pl.BlockSpec((R, ROW), lambda i: (0, 0))
    return pl.pallas_call(
        body, name=name, grid=(1,), in_specs=[spec] * 4, out_specs=[spec] * 3,
        out_shape=[jax.ShapeDtypeStruct((R, ROW), F32)] * 3, compiler_params=_params(),
    )(g, w, m, v)


def _mesh_pos():
    return lax.axis_index("x"), lax.axis_index("y"), lax.axis_index("c")


def _flip(v, bit):
    return 1 - v if bit else v


def _peers(x, y, c):
    out = []
    for k in range(1, N_DEV):
        px, py, pc = _flip(x, k & 4), _flip(y, k & 2), _flip(c, k & 1)
        out.append((k - 1, (px, py, pc), 4 * px + 2 * py + pc))
    return out


def _send_start(srcs, blocked, name):
    n = len(srcs)
    MESH = pl.DeviceIdType.MESH

    def body(*refs):
        src_refs, land_refs = refs[:n], refs[n:2 * n]
        send_sems, recv_sems, local_sems = refs[2 * n:3 * n], refs[3 * n:4 * n], refs[4 * n:5 * n]
        token = refs[7 * n]
        x, y, c = _mesh_pos()
        me = 4 * x + 2 * y + c
        for a in range(n):
            for k, peer, blk in _peers(x, y, c):
                pltpu.make_async_remote_copy(
                    src_ref=src_refs[a].at[blk] if blocked else src_refs[a], dst_ref=land_refs[a].at[me],
                    send_sem=send_sems[a].at[k], recv_sem=recv_sems[a].at[k], device_id=peer,
                    device_id_type=MESH).start()
            pltpu.make_async_copy(src_refs[a].at[me] if blocked else src_refs[a], land_refs[a].at[me],
                                  local_sems[a]).start()
        token[...] = jnp.zeros_like(token)

    hbm = pl.BlockSpec(memory_space=pltpu.HBM)
    sem = pl.BlockSpec(memory_space=pltpu.SEMAPHORE)
    lshape = [s.shape if blocked else (N_DEV,) + s.shape for s in srcs]
    outs = pl.pallas_call(
        body, name=name,
        out_shape=([pltpu.SemaphoreType.DMA((N_DEV - 1,))] * (2 * n) + [pltpu.SemaphoreType.DMA(())] * n
                   + [pltpu.HBM(s.shape, s.dtype) for s in srcs]
                   + [pltpu.HBM(ls, s.dtype) for ls, s in zip(lshape, srcs)] + [jax.ShapeDtypeStruct((8, 128), F32)]),
        in_specs=[hbm] * (2 * n), out_specs=[sem] * (3 * n) + [hbm] * (2 * n) + [pl.BlockSpec(memory_space=pltpu.VMEM)],
        input_output_aliases={**{a: 3 * n + a for a in range(n)}, **{n + a: 4 * n + a for a in range(n)}},
        compiler_params=pltpu.CompilerParams(has_side_effects=pltpu.SideEffectType.DATAFLOW_SIDE_EFFECTING),
    )(*[pltpu.with_memory_space_constraint(s, pltpu.HBM) for s in srcs],
      *[pltpu.with_memory_space_constraint(lax.empty(ls, s.dtype), pltpu.HBM) for ls, s in zip(lshape, srcs)])
    return dict(send=outs[:n], recv=outs[n:2 * n], local=outs[2 * n:3 * n], srcs=outs[3 * n:4 * n],
                lands=outs[4 * n:5 * n], token=outs[5 * n], blocked=blocked)


def _send_wait(started, which, after, name):
    n = len(which)
    blocked = started["blocked"]
    pick = lambda key: [started[key][a] for a in which]
    MESH = pl.DeviceIdType.MESH

    def body(*refs):
        src_refs, land_refs = refs[:n], refs[n:2 * n]
        send, recv, local = refs[2 * n:3 * n], refs[3 * n:4 * n], refs[4 * n:5 * n]
        x, y, c = _mesh_pos()
        me = 4 * x + 2 * y + c
        for a in range(n):
            for k, peer, blk in _peers(x, y, c):
                cp = pltpu.make_async_remote_copy(
                    src_ref=src_refs[a].at[blk] if blocked else src_refs[a], dst_ref=land_refs[a].at[blk],
                    send_sem=send[a].at[k], recv_sem=recv[a].at[k], device_id=peer, device_id_type=MESH)
                cp.wait_send()
                cp.wait_recv()
            pltpu.make_async_copy(src_refs[a].at[me] if blocked else src_refs[a], land_refs[a].at[me],
                                  local[a]).wait()

    hbm = pl.BlockSpec(memory_space=pltpu.HBM)
    sem = pl.BlockSpec(memory_space=pltpu.SEMAPHORE)
    srcs, lands = pick("srcs"), pick("lands")
    outs = pl.pallas_call(
        body, name=name, out_shape=[pltpu.HBM(s.shape, s.dtype) for s in srcs + lands],
        in_specs=[hbm] * (2 * n) + [sem] * (3 * n) + [pl.BlockSpec(memory_space=pl.ANY)], out_specs=[hbm] * (2 * n),
        input_output_aliases={a: a for a in range(2 * n)},
        compiler_params=pltpu.CompilerParams(has_side_effects=pltpu.SideEffectType.DATAFLOW_SIDE_EFFECTING),
    )(*srcs, *lands, *pick("send"), *pick("recv"), *pick("local"), after)
    return outs[n:]


def _adamw_weight(parts, w, m, v, name):
    L, R, C = w.shape
    tr = max(t for t in range(16, min(R, 256) + 1, 16) if R % t == 0)
    nr = R // tr

    def body(*refs):
        p_refs = refs[:L]
        w_ref, m_ref, v_ref, g_ref, d_ref, nm_ref, nv_ref, gsum = refs[L:]
        layer = pl.program_id(0)
        for l in range(L):
            @pl.when(layer == l)
            def _(l=l):
                g = p_refs[l][0].astype(F32)
                for s in range(1, N_DEV):
                    g = g + p_refs[l][s].astype(F32)
                gsum[...] = g
        g = gsum[...]
        g_ref[...] = g
        d_ref[...], nm_ref[...], nv_ref[...] = _adamw_math(w_ref[...], g, m_ref[...], v_ref[...])

    def part_spec(l):
        return pl.BlockSpec((N_DEV, tr, C), lambda layer, i: (0, jnp.where(layer == l, i, jnp.where(layer < l, 0, nr - 1)), 0))

    ws = pl.BlockSpec((None, tr, C), lambda layer, i: (layer, i, 0))
    return pl.pallas_call(
        body, name=name, grid=(L, nr), in_specs=[part_spec(l) for l in range(L)] + [ws] * 3, out_specs=[ws] * 4,
        out_shape=[jax.ShapeDtypeStruct((L, R, C), F32)] * 4, scratch_shapes=[pltpu.VMEM((tr, C), F32)],
        compiler_params=pltpu.CompilerParams(dimension_semantics=("arbitrary", "arbitrary")),
    )(*parts, w, m, v)


def _seg_rows(n, align):
    rows = -(-n // ROW)
    return -(-rows // align) * align


def _pack(arrs, align, lead=()):
    parts = []
    for a in arrs:
        n = int(np.prod(a.shape[len(lead):]))
        rows = _seg_rows(n, align)
        flat = a.reshape(lead + (n,))
        flat = jnp.pad(flat, [(0, 0)] * len(lead) + [(0, rows * ROW - n)])
        parts.append(flat.reshape(lead + (rows, ROW)))
    return jnp.concatenate(parts, axis=len(lead))


def _unpack(buf, shapes, align, lead=()):
    out, r0 = [], 0
    for shp in shapes:
        n = int(np.prod(shp))
        rows = _seg_rows(n, align)
        seg = lax.slice_in_dim(buf, r0, r0 + rows, axis=len(lead))
        out.append(seg.reshape(lead + (rows * ROW,))[..., :n].reshape(lead + tuple(shp)))
        r0 += rows
    return out


def kernel(x, p, norm_mix, norm_ffn, norm_ple, norm_kv, norm_final, a_w_in, a_norm_v, a_w_s, a_b_s, a_w_out, w_kv, b_w_q, b_sinks, b_w_o, f_w_up, f_conv_w, f_conv_b, f_w_down, ple_w_in, ple_w_gate, ple_b_gate, loss_target, m_norm_mix, m_norm_ffn, m_norm_ple, m_norm_kv, m_norm_final, m_a_w_in, m_a_norm_v, m_a_w_s, m_a_b_s, m_a_w_out, m_w_kv, m_b_w_q, m_b_sinks, m_b_w_o, m_f_w_up, m_f_conv_w, m_f_conv_b, m_f_w_down, m_ple_w_in, m_ple_w_gate, m_ple_b_gate, v_norm_mix, v_norm_ffn, v_norm_ple, v_norm_kv, v_norm_final, v_a_w_in, v_a_norm_v, v_a_w_s, v_a_b_s, v_a_w_out, v_w_kv, v_b_w_q, v_b_sinks, v_b_w_o, v_f_w_up, v_f_conv_w, v_f_conv_b, v_f_w_down, v_ple_w_in, v_ple_w_gate, v_ple_b_gate):
    w = dict(norm_mix=norm_mix, norm_ffn=norm_ffn, norm_ple=norm_ple, norm_kv=norm_kv, norm_final=norm_final,
             a_w_in=a_w_in, a_norm_v=a_norm_v, a_w_s=a_w_s, a_b_s=a_b_s, a_w_out=a_w_out, w_kv=w_kv, b_w_q=b_w_q,
             b_sinks=b_sinks, b_w_o=b_w_o, f_w_up=f_w_up, f_conv_w=f_conv_w, f_conv_b=f_conv_b, f_w_down=f_w_down,
             ple_w_in=ple_w_in, ple_w_gate=ple_w_gate, ple_b_gate=ple_b_gate)
    mom_m = dict(norm_mix=m_norm_mix, norm_ffn=m_norm_ffn, norm_ple=m_norm_ple, norm_kv=m_norm_kv,
                 norm_final=m_norm_final, a_w_in=m_a_w_in, a_norm_v=m_a_norm_v, a_w_s=m_a_w_s, a_b_s=m_a_b_s,
                 a_w_out=m_a_w_out, w_kv=m_w_kv, b_w_q=m_b_w_q, b_sinks=m_b_sinks, b_w_o=m_b_w_o, f_w_up=m_f_w_up,
                 f_conv_w=m_f_conv_w, f_conv_b=m_f_conv_b, f_w_down=m_f_w_down, ple_w_in=m_ple_w_in,
                 ple_w_gate=m_ple_w_gate, ple_b_gate=m_ple_b_gate)
    mom_v = dict(norm_mix=v_norm_mix, norm_ffn=v_norm_ffn, norm_ple=v_norm_ple, norm_kv=v_norm_kv,
                 norm_final=v_norm_final, a_w_in=v_a_w_in, a_norm_v=v_a_norm_v, a_w_s=v_a_w_s, a_b_s=v_a_b_s,
                 a_w_out=v_a_w_out, w_kv=v_w_kv, b_w_q=v_b_w_q, b_sinks=v_b_sinks, b_w_o=v_b_w_o, f_w_up=v_f_w_up,
                 f_conv_w=v_f_conv_w, f_conv_b=v_f_conv_b, f_w_down=v_f_w_down, ple_w_in=v_ple_w_in,
                 ple_w_gate=v_ple_w_gate, ple_b_gate=v_ple_b_gate)
    S = x.shape[1]
    tm = min(256, S)
    me = 4 * lax.axis_index("x") + 2 * lax.axis_index("y") + lax.axis_index("c")

    def after_token(tok, a):
        return a + tok[0:1, 0:1].reshape((1,) * a.ndim)

    wb = {n: w[n].astype(BF) for n in _BIG}
    order = [('a_w_in', 0), ('a_w_out', 0), ('a_norm_v', None), ('f_conv_w', 0), ('f_conv_w', 1),
             ('f_w_up', 0), ('f_w_down', 0),
             ('ple_w_in', 0), ('ple_w_gate', 0), ('w_kv', None), ('b_w_q', 0), ('b_w_o', 0),
             ('f_w_up', 1), ('f_w_down', 1), ('ple_w_in', 1), ('ple_w_gate', 1)]
    src = lambda n, l: (wb[n] if n in wb else w[n]) if l is None else (wb[n] if n in wb else w[n])[l]
    gath = _send_start([src(n, l) for n, l in order], False, "gather_start")
    x0 = x[0]

    def gathered(names, after, name):
        which = [order.index(k) for k in names]
        return dict(zip(names, _send_wait(gath, which, after, name)))

    g = gathered(order[:5], x0, "gather_wait_mix")
    win = g[('a_w_in', 0)].transpose(1, 0, 2).reshape(D, 2 * AW)
    wout = g[('a_w_out', 0)].reshape(AW, D)
    gv = g[('a_norm_v', None)].reshape(1, AW)
    cw = [g[('f_conv_w', l)] for l in range(2)]
    cb = [w['f_conv_b'][l].reshape(N_DEV, 1, FF_BLK) for l in range(2)]
    ws = a_w_s[0].astype(BF)
    wst = jnp.swapaxes(ws, 1, 2)
    bsb = jnp.broadcast_to(a_b_s[0][:, :, None], (GROUPS, CHUNK, CHUNK))
    row = lambda a: a.reshape(1, -1)
    pb = [p[l, 0].astype(BF) for l in range(2)]
    sinks = b_sinks.reshape(N_Q)
    consts = _attn_consts()

    h1, z = _mixa_fwd(x0, row(norm_mix[0]), win, gv, ws, bsb, wout, tm)
    g = gathered(order[5:7], h1, "gather_wait_ffn0")
    wup, wdown = [g[('f_w_up', 0)], None], [g[('f_w_down', 0)].reshape(D_FF, D), None]
    h2, hh0, cc0 = _ffn_fwd(h1, row(norm_ffn[0]), wup[0], cw[0], cb[0], wdown[0], tm)
    g = gathered(order[7:12], h2, "gather_wait_mid")
    wple = [g[('ple_w_in', 0)].transpose(1, 0, 2).reshape(PLE, D), None]
    wgate = [g[('ple_w_gate', 0)].reshape(D, D), None]
    wkv, wq, wo = g[('w_kv', None)].reshape(D, 2 * N_KV * HD), g[('b_w_q', 0)].reshape(D, D), g[('b_w_o', 0)].reshape(D, D)
    h3 = _ple_fwd(h2, row(norm_ple[0]), wgate[0], row(ple_b_gate[0]), pb[0], wple[0], tm)
    q, kv = _qkv_fwd(h3, row(norm_mix[1]), wq, row(norm_kv), wkv, tm)
    att = _attn_fwd(q, kv, sinks, consts)
    h4 = _oproj_fwd(att, wo, h3, tm)
    g = gathered(order[12:], h4, "gather_wait_l1")
    wup[1], wdown[1] = g[('f_w_up', 1)], g[('f_w_down', 1)].reshape(D_FF, D)
    wple[1], wgate[1] = g[('ple_w_in', 1)].transpose(1, 0, 2).reshape(PLE, D), g[('ple_w_gate', 1)].reshape(D, D)
    h5, hh1, cc1 = _ffn_fwd(h4, row(norm_ffn[1]), wup[1], cw[1], cb[1], wdown[1], tm)
    h6 = _ple_fwd(h5, row(norm_ple[1]), wgate[1], row(ple_b_gate[1]), pb[1], wple[1], tm)

    def col_blocks(g2d, n):
        return g2d.reshape(g2d.shape[0], N_DEV, n).transpose(1, 0, 2)

    def row_blocks(g2d):
        return g2d.reshape(N_DEV, g2d.shape[0] // N_DEV, g2d.shape[1])

    def ple_grads(l, de, xp, dpre):
        return [col_blocks(_wgrad(pb[l], de, f"wgrad_ple_in{l}", tn=1024), D // N_DEV),
                row_blocks(_wgrad(xp, dpre, f"wgrad_gate{l}", tn=1024))]

    def ffn_grads(l, xf, dhh, act, dhb):
        return [_wgrad(xf, dhh, f"wgrad_up{l}", tn=FF_BLK, out_blocked=True, a_transposed=True),
                row_blocks(_wgrad(act, dhb, f"wgrad_down{l}", tn=1024))]

    dh6, loss_blk, d_norm_final = _head(h6, row(norm_final), loss_target[0], tm)
    dh5, de1, xp1, dpre1, dbg1, dgp1 = _ple_bwd(dh6, h5, row(norm_ple[1]), wgate[1], row(ple_b_gate[1]), pb[1],
                                                 wple[1], tm)
    ex_ple1 = _send_start(ple_grads(1, de1, xp1, dpre1), True, "send_ple1")
    dh4, xf1, dhh1, act1, dhb5, dcw1, dcb1, dgf1 = _ffn_bwd(dh5, h4, hh1, cc1, after_token(ex_ple1["token"], row(norm_ffn[1])),
                                                            wup[1], cw[1], wdown[1], tm)
    ex_ffn1 = _send_start(ffn_grads(1, xf1, dhh1, act1, dhb5), True, "send_ffn1")
    datt, dhb4 = _oproj_bwd(dh4, wo, tm)
    dq, dkv, dsink = _attn_bwd(q, kv, att, datt, after_token(ex_ffn1["token"], sinks.reshape(1, N_Q)).reshape(N_Q), consts)
    dh3, xq, xkv, dgq, dgkv = _qkv_bwd(dh4, h3, dq, dkv, row(norm_mix[1]), wq, row(norm_kv), wkv, tm)
    ex_att = _send_start([row_blocks(_wgrad(att, dhb4, "wgrad_o", tn=1024)),
                              row_blocks(_wgrad(xq, dq, "wgrad_q", tn=1024)),
                              row_blocks(_wgrad(xkv, dkv, "wgrad_kv", tn=512))], True, "send_att")
    dh2, de0, xp0, dpre0, dbg0, dgp0 = _ple_bwd(dh3, h2, after_token(ex_att["token"], row(norm_ple[0])), wgate[0],
                                                 row(ple_b_gate[0]), pb[0], wple[0], tm)
    ex_ple0 = _send_start(ple_grads(0, de0, xp0, dpre0), True, "send_ple0")
    dh1, xf0, dhh0, act0, dhb2, dcw0, dcb0, dgf0 = _ffn_bwd(dh2, h1, hh0, cc0, after_token(ex_ple0["token"], row(norm_ffn[0])),
                                                            wup[0], cw[0], wdown[0], tm)
    ex_ffn0 = _send_start(ffn_grads(0, xf0, dhh0, act0, dhb2), True, "send_ffn0")
    dx, xa, dz, gated, dhb1, dws, dbs, dgv, dga = _mixa_bwd(dh1, x0, z, after_token(ex_ffn0["token"], row(norm_mix[0])), win, gv,
                                                            ws, wst, bsb, wout, tm)
    ex_mix = _send_start([_wgrad(xa, dz, "wgrad_a_in", tn=2 * AW // N_DEV, out_blocked=True, a_transposed=True),
                              row_blocks(_wgrad(gated, dhb1, "wgrad_a_out", tn=1024))], True, "send_mix")

    small_g = {
        'norm_mix': jnp.concatenate([dga, dgq], axis=0), 'norm_ffn': jnp.concatenate([dgf0, dgf1], axis=0),
        'norm_ple': jnp.concatenate([dgp0, dgp1], axis=0), 'norm_kv': dgkv.reshape(D),
        'norm_final': d_norm_final.reshape(D), 'a_w_s': dws[None], 'a_b_s': dbs[None, :, :, 0],
        'b_sinks': dsink[:, 0].reshape(1, N_Q),
        'f_conv_b': jnp.stack([dcb0.reshape(2 * D_FF), dcb1.reshape(2 * D_FF)]),
        'ple_b_gate': jnp.concatenate([dbg0, dbg1], axis=0),
        'a_norm_v': dgv, 'f_conv_w': jnp.stack([dcw0, dcw1], axis=1),
    }
    names_all = _REPL + _SMALL_SHARDED
    full_shapes = [small_g[n].shape for n in names_all]
    ex_small = _send_start([after_token(ex_mix["token"], _pack([small_g[n] for n in names_all], 8))], False, "send_small")

    def as3(a):
        return a.reshape((-1,) + a.shape[-2:])

    def adam(name, parts):
        outs = _adamw_weight(parts, as3(w[name]), as3(mom_m[name]), as3(mom_v[name]), "adamw_" + name)
        return [o.reshape(w[name].shape) for o in outs]

    big = {}
    l_ple1 = _send_wait(ex_ple1, list(range(len(ex_ple1["lands"]))), ex_small["token"], "recv_ple1")
    l_ffn1 = _send_wait(ex_ffn1, list(range(len(ex_ffn1["lands"]))), l_ple1[0], "recv_ffn1")
    l_att = _send_wait(ex_att, list(range(len(ex_att["lands"]))), l_ffn1[0], "recv_att")
    big['b_w_o'], big['b_w_q'], big['w_kv'] = adam('b_w_o', [l_att[0]]), adam('b_w_q', [l_att[1]]), adam('w_kv', [l_att[2]])
    l_ple0 = _send_wait(ex_ple0, list(range(len(ex_ple0["lands"]))), big['w_kv'][0], "recv_ple0")
    big['ple_w_in'] = adam('ple_w_in', [l_ple0[0], l_ple1[0]])
    big['ple_w_gate'] = adam('ple_w_gate', [l_ple0[1], l_ple1[1]])
    l_ffn0 = _send_wait(ex_ffn0, list(range(len(ex_ffn0["lands"]))), big['ple_w_gate'][0], "recv_ffn0")
    big['f_w_up'] = adam('f_w_up', [l_ffn0[0], l_ffn1[0]])
    big['f_w_down'] = adam('f_w_down', [l_ffn0[1], l_ffn1[1]])
    l_mix = _send_wait(ex_mix, list(range(len(ex_mix["lands"]))), big['f_w_down'][0], "recv_mix")
    big['a_w_in'], big['a_w_out'] = adam('a_w_in', [l_mix[0]]), adam('a_w_out', [l_mix[1]])
    g_big, d_big, m_big, v_big = [{n: big[n][i] for n in _BIG} for i in range(4)]

    gsum = _sum8(_send_wait(ex_small, [0], big['a_w_out'][0], "recv_small")[0], "sum_small_grads")
    gs = dict(zip(names_all, _unpack(gsum, full_shapes, 8)))
    gs['a_norm_v'] = lax.dynamic_slice_in_dim(gs['a_norm_v'], me * (AW // N_DEV), AW // N_DEV, axis=1)
    gs['f_conv_w'] = lax.dynamic_index_in_dim(gs['f_conv_w'], me, axis=0, keepdims=False)
    gs = {n: gs[n].reshape(w[n].shape) for n in names_all}
    packs = lambda d: _pack([d[n] for n in names_all], 8)
    souts = _adamw_small(packs(gs), packs(w), packs(mom_m), packs(mom_v), "adamw_small")
    shp = [w[n].shape for n in names_all]
    d_s, m_s, v_s = [dict(zip(names_all, _unpack(o, shp, 8))) for o in souts]

    grads = {**g_big, **gs}
    delta = {**d_big, **d_s}
    new_m = {**m_big, **m_s}
    new_v = {**v_big, **v_s}
    loss = lax.psum(loss_blk[0, 0], ("x", "y", "c"))
    return (loss, dx[None], *[grads[n] for n in _PARAMS], *[delta[n] for n in _PARAMS],
            *[new_m[n] for n in _PARAMS], *[new_v[n] for n in _PARAMS])
```

```python
import functools

import numpy as np
import jax
import jax.numpy as jnp
from jax import lax
from jax.experimental import pallas as pl
from jax.experimental.pallas import tpu as pltpu

F32 = jnp.float32
BF = jnp.bfloat16

N_DEV = 8
D = 1024
AW = 1024
GROUPS = 8
CHUNK = 128
HD = 64
N_Q = 16
N_KV = 4
GQA = 4
BLOCK = 128
D_FF = 2816
FF_BLK = 704
N_FF_BLK = 4
PLE = 256
EPS = 1e-6
LR, B1, B2, ADAM_EPS, WD, STEP = 0.001, 0.9, 0.999, 1e-08, 0.01, 10
GELU_C0 = 0.7978845608028654
GELU_C1 = 0.044715
NEG = -1e30
ROW = 1024
VMEM_BIG = 56 * 1024 * 1024

_PARAMS = ['norm_mix', 'norm_ffn', 'norm_ple', 'norm_kv', 'norm_final', 'a_w_in', 'a_norm_v', 'a_w_s', 'a_b_s',
           'a_w_out', 'w_kv', 'b_w_q', 'b_sinks', 'b_w_o', 'f_w_up', 'f_conv_w', 'f_conv_b', 'f_w_down',
           'ple_w_in', 'ple_w_gate', 'ple_b_gate']
_BIG = ['a_w_in', 'a_w_out', 'w_kv', 'b_w_q', 'b_w_o', 'f_w_up', 'f_w_down', 'ple_w_in', 'ple_w_gate']
_SMALL_SHARDED = ['a_norm_v', 'f_conv_w']
_REPL = ['norm_mix', 'norm_ffn', 'norm_ple', 'norm_kv', 'norm_final', 'a_w_s', 'a_b_s', 'b_sinks', 'f_conv_b',
         'ple_b_gate']


def _mm(a, b):
    return jnp.dot(a, b, preferred_element_type=F32)


def _mm_nt(a, b):
    return lax.dot_general(a, b, (((1,), (1,)), ((), ())), preferred_element_type=F32)


def _mm_tn(a, b):
    return lax.dot_general(a, b, (((0,), (0,)), ((), ())), preferred_element_type=F32)


def _const_spec(shape, single=False):
    nd = len(shape)
    if single:
        return pl.BlockSpec(shape, lambda *_: (0,) * nd, pipeline_mode=pl.Buffered(1))
    return pl.BlockSpec(shape, lambda *_: (0,) * nd)


def _row_spec(tm, n):
    return pl.BlockSpec((tm, n), lambda i: (i, 0))


def _rms_fwd(x, g):
    r = lax.rsqrt(jnp.mean(x * x, axis=-1, keepdims=True) + EPS)
    xhat = x * r
    return xhat * g, xhat, r


def _rms_bwd(dy, xhat, r, g):
    a = dy * g
    dx = r * (a - xhat * jnp.mean(a * xhat, axis=-1, keepdims=True))
    return dx, jnp.sum(dy * xhat, axis=0, keepdims=True)


def _sigmoid(x):
    return 1.0 / (1.0 + jnp.exp(-x))


def _params(vmem=None):
    return pltpu.CompilerParams(dimension_semantics=("arbitrary",), vmem_limit_bytes=vmem)


def _tri_masks():
    t = lax.broadcasted_iota(jnp.int32, (CHUNK, CHUNK), 0)
    s = lax.broadcasted_iota(jnp.int32, (CHUNK, CHUNK), 1)
    return t >= s, s >= t


def _mixa_fwd(x, g, win, gv, ws, bsb, wout, tm):
    S = x.shape[0]
    nc = tm // CHUNK

    def body(x_ref, g_ref, win_ref, gv_ref, ws_ref, bsb_ref, wout_ref, h_ref, z_ref, gated):
        x = x_ref[...]
        xn, _, _ = _rms_fwd(x, g_ref[...])
        z = _mm(xn.astype(BF), win_ref[...])
        z_ref[...] = z.astype(BF)
        zg = 0.5 * z * (1.0 + jnp.tanh(GELU_C0 * (z + GELU_C1 * z * z * z)))
        u = zg[:, :AW]
        vn, _, _ = _rms_fwd(zg[:, AW:], gv_ref[...])
        vn = vn.astype(BF)
        tril, _ = _tri_masks()
        for h in range(GROUPS):
            wm = jnp.where(tril, ws_ref[h], jnp.zeros((), BF))
            for c in range(nc):
                rs, cs = slice(c * CHUNK, (c + 1) * CHUNK), slice(h * CHUNK, (h + 1) * CHUNK)
                s = _mm(wm, vn[rs, cs]) + bsb_ref[h]
                gated[rs, cs] = (u[rs, cs] * s).astype(BF)
        h_ref[...] = x + _mm(gated[...], wout_ref[...])

    return pl.pallas_call(
        body, name="mixa_fwd", grid=(S // tm,),
        in_specs=[_row_spec(tm, D), _const_spec((1, D)), _const_spec((D, 2 * AW)), _const_spec((1, AW)),
                  _const_spec((GROUPS, CHUNK, CHUNK)), _const_spec((GROUPS, CHUNK, CHUNK)), _const_spec((AW, D))],
        out_specs=[_row_spec(tm, D), _row_spec(tm, 2 * AW)],
        out_shape=[jax.ShapeDtypeStruct((S, D), F32), jax.ShapeDtypeStruct((S, 2 * AW), BF)],
        scratch_shapes=[pltpu.VMEM((tm, AW), BF)],
        compiler_params=_params(VMEM_BIG),
    )(x, g, win, gv, ws, bsb, wout)


def _mixa_bwd(dh, x, z, g, win, gv, ws, wst, bsb, wout, tm):
    S = x.shape[0]
    nt = S // tm
    nc = tm // CHUNK

    def body(dh_ref, x_ref, z_ref, g_ref, win_ref, gv_ref, ws_ref, wst_ref, bsb_ref, wout_ref,
             dx_ref, xn_ref, dz_ref, gated_ref, dhb_ref, dws_ref, dbs_ref, dgv_ref, dg_ref, du_scr, dvn_scr):
        i = pl.program_id(0)

        @pl.when(i == 0)
        def _():
            dws_ref[...] = jnp.zeros_like(dws_ref)
            dbs_ref[...] = jnp.zeros_like(dbs_ref)
            dgv_ref[...] = jnp.zeros_like(dgv_ref)
            dg_ref[...] = jnp.zeros_like(dg_ref)

        x = x_ref[...]
        gmix = g_ref[...]
        xn, xhat, r = _rms_fwd(x, gmix)
        xn_ref[...] = xn.T.astype(BF)
        z = z_ref[...].astype(F32)
        z2 = z * z
        t = jnp.tanh(GELU_C0 * (z + GELU_C1 * z2 * z))
        zg = 0.5 * z * (1.0 + t)
        gp = 0.5 * (1.0 + t) + 0.5 * z * (1.0 - t * t) * (GELU_C0 * (1.0 + 3.0 * GELU_C1 * z2))
        u = zg[:, :AW]
        gvv = gv_ref[...]
        vn, vhat, rv = _rms_fwd(zg[:, AW:], gvv)
        vn = vn.astype(BF)
        dh = dh_ref[...]
        dhb = dh.astype(BF)
        dhb_ref[...] = dhb
        dgated = _mm_nt(dhb, wout_ref[...])
        tril, triu = _tri_masks()
        for h in range(GROUPS):
            wm = jnp.where(tril, ws_ref[h], jnp.zeros((), BF))
            wmt = jnp.where(triu, wst_ref[h], jnp.zeros((), BF))
            for c in range(nc):
                rs, cs = slice(c * CHUNK, (c + 1) * CHUNK), slice(h * CHUNK, (h + 1) * CHUNK)
                vnb = vn[rs, cs]
                s = _mm(wm, vnb) + bsb_ref[h]
                gated_ref[rs, cs] = (u[rs, cs] * s).astype(BF)
                dgt = dgated[rs, cs]
                du_scr[rs, cs] = dgt * s
                ds = dgt * u[rs, cs]
                dsb = ds.astype(BF)
                dws_ref[h] += jnp.where(tril, _mm_nt(dsb, vnb), 0.0)
                dbs_ref[h] += ds
                dvn_scr[rs, cs] = _mm(wmt, dsb)
        dvn = dvn_scr[...]
        dv, dgv = _rms_bwd(dvn, vhat, rv, gvv)
        dgv_ref[...] += dgv
        dz_ref[:, :AW] = (du_scr[...] * gp[:, :AW]).astype(BF)
        dz_ref[:, AW:] = (dv * gp[:, AW:]).astype(BF)
        dxn = _mm_nt(dz_ref[...], win_ref[...])
        dxr, dg = _rms_bwd(dxn, xhat, r, gmix)
        dg_ref[...] += dg
        dx_ref[...] = dh + dxr

        @pl.when(i == nt - 1)
        def _():
            for h in range(GROUPS):
                dbs_ref[h] = jnp.broadcast_to(jnp.sum(dbs_ref[h], axis=-1, keepdims=True), (CHUNK, CHUNK))

    gshape = (GROUPS, CHUNK, CHUNK)
    return pl.pallas_call(
        body, name="mixa_bwd", grid=(nt,),
        in_specs=[_row_spec(tm, D), _row_spec(tm, D), _row_spec(tm, 2 * AW), _const_spec((1, D)),
                  _const_spec((D, 2 * AW)), _const_spec((1, AW)), _const_spec(gshape), _const_spec(gshape),
                  _const_spec(gshape), _const_spec((AW, D))],
        out_specs=[_row_spec(tm, D), pl.BlockSpec((D, tm), lambda i: (0, i)), _row_spec(tm, 2 * AW), _row_spec(tm, AW),
                   _row_spec(tm, D), _const_spec(gshape), _const_spec(gshape), _const_spec((1, AW)), _const_spec((1, D))],
        out_shape=[jax.ShapeDtypeStruct((S, D), F32), jax.ShapeDtypeStruct((D, S), BF),
                   jax.ShapeDtypeStruct((S, 2 * AW), BF), jax.ShapeDtypeStruct((S, AW), BF),
                   jax.ShapeDtypeStruct((S, D), BF), jax.ShapeDtypeStruct(gshape, F32),
                   jax.ShapeDtypeStruct(gshape, F32), jax.ShapeDtypeStruct((1, AW), F32),
                   jax.ShapeDtypeStruct((1, D), F32)],
        scratch_shapes=[pltpu.VMEM((tm, AW), F32), pltpu.VMEM((tm, AW), F32)],
        compiler_params=_params(VMEM_BIG),
    )(dh, x, z, g, win, gv, ws, wst, bsb, wout)


def _shift_down(x, prev8, k):
    n, c = x.shape[0] // 8, x.shape[1]
    r = pltpu.roll(x.reshape(n, 8, c), k, axis=1)
    before = jnp.concatenate([pltpu.roll(prev8.reshape(1, 8, c), k, axis=1), r[:-1]], axis=0)
    sub = lax.broadcasted_iota(jnp.int32, (1, 8, c), 1)
    return jnp.where(sub < k, before, r).reshape(x.shape)


def _shift_up(x, next8, k):
    n, c = x.shape[0] // 8, x.shape[1]
    r = pltpu.roll(x.reshape(n, 8, c), 8 - k, axis=1)
    after = jnp.concatenate([r[1:], pltpu.roll(next8.reshape(1, 8, c), 8 - k, axis=1)], axis=0)
    sub = lax.broadcasted_iota(jnp.int32, (1, 8, c), 1)
    return jnp.where(sub >= 8 - k, after, r).reshape(x.shape)


def _ffn_fwd(h, g, wup, cw, cb, wdown, tm):
    S = h.shape[0]

    def body(h_ref, g_ref, wup_ref, cw_ref, cb_ref, wdown_ref, o_ref, hh_ref, cc_ref, sh):
        i = pl.program_id(0)

        @pl.when(i == 0)
        def _():
            sh[...] = jnp.zeros_like(sh)

        x = h_ref[...]
        xn, _, _ = _rms_fwd(x, g_ref[...])
        xn = xn.astype(BF)
        acc = x
        up = lambda d: [_mm(xn, wup_ref[d]), _mm(xn, wup_ref[N_FF_BLK + d])]
        ahead = up(0)
        for d in range(N_FF_BLK):
            cs = []
            hbs = ahead
            if d + 1 < N_FF_BLK:
                ahead = up(d + 1)
            for blk, hb in zip((d, N_FF_BLK + d), hbs):
                hh_ref[blk] = hb.astype(BF)
                w, prev8 = cw_ref[blk], sh[blk]
                c = cb_ref[blk] + w[0:1] * _shift_down(hb, prev8, 2) + w[1:2] * _shift_down(hb, prev8, 1) + w[2:3] * hb
                sh[blk] = hb[tm - 8:, :]
                cc_ref[blk] = c.astype(BF)
                cs.append(c)
            act = cs[0] * _sigmoid(cs[0]) * cs[1]
            acc = acc + _mm(act.astype(BF), wdown_ref[d * FF_BLK:(d + 1) * FF_BLK, :])
        o_ref[...] = acc

    blk3 = pl.BlockSpec((N_DEV, tm, FF_BLK), lambda i: (0, i, 0))
    return pl.pallas_call(
        body, name="ffn_fwd", grid=(S // tm,),
        in_specs=[_row_spec(tm, D), _const_spec((1, D)), _const_spec((N_DEV, D, FF_BLK), True),
                  _const_spec((N_DEV, 3, FF_BLK)), _const_spec((N_DEV, 1, FF_BLK)), _const_spec((D_FF, D), True)],
        out_specs=[_row_spec(tm, D), blk3, blk3],
        out_shape=[jax.ShapeDtypeStruct((S, D), F32), jax.ShapeDtypeStruct((N_DEV, S, FF_BLK), BF),
                   jax.ShapeDtypeStruct((N_DEV, S, FF_BLK), BF)],
        scratch_shapes=[pltpu.VMEM((N_DEV, 8, FF_BLK), F32)],
        compiler_params=_params(VMEM_BIG),
    )(h, g, wup, cw, cb, wdown)


def _ffn_bwd(dh, h, hh, cc, g, wup, cw, wdown, tm):
    S = h.shape[0]
    nt = S // tm

    def rev(i):
        return (nt - 1 - i, 0)

    def body(dh_ref, h_ref, hh_ref, cc_ref, g_ref, wup_ref, cw_ref, wdown_ref,
             dx_ref, xn_ref, dhh_ref, act_ref, dhb_ref, dcw_ref, dcb_ref, dg_ref, sh):
        i = pl.program_id(0)

        @pl.when(i == 0)
        def _():
            sh[...] = jnp.zeros_like(sh)
            dcw_ref[...] = jnp.zeros_like(dcw_ref)
            dcb_ref[...] = jnp.zeros_like(dcb_ref)
            dg_ref[...] = jnp.zeros_like(dg_ref)

        x = h_ref[...]
        gain = g_ref[...]
        xn, xhat, r = _rms_fwd(x, gain)
        xn_ref[...] = xn.T.astype(BF)
        dh = dh_ref[...]
        dhb = dh.astype(BF)
        dhb_ref[...] = dhb
        dxn = jnp.zeros((tm, D), F32)
        dact_of = lambda d: _mm_nt(dhb, wdown_ref[d * FF_BLK:(d + 1) * FF_BLK, :])
        ahead = dact_of(0)
        for d in range(N_FF_BLK):
            blks = (d, N_FF_BLK + d)
            dact = ahead
            if d + 1 < N_FF_BLK:
                ahead = dact_of(d + 1)
            cg, cu = cc_ref[blks[0]].astype(F32), cc_ref[blks[1]].astype(F32)
            sg = _sigmoid(cg)
            sil = cg * sg
            act_ref[d] = (sil * cu).astype(BF)
            dcs = (dact * cu * (sg + sil - sil * sg), dact * sil)
            for blk, dc in zip(blks, dcs):
                next8 = sh[blk]
                dp1, dp2 = _shift_up(dc, next8, 1), _shift_up(dc, next8, 2)
                hb = hh_ref[blk].astype(F32)
                dcb_ref[blk] += jnp.sum(dc, axis=0, keepdims=True)
                dcw_ref[blk, 0:1, :] += jnp.sum(dp2 * hb, axis=0, keepdims=True)
                dcw_ref[blk, 1:2, :] += jnp.sum(dp1 * hb, axis=0, keepdims=True)
                dcw_ref[blk, 2:3, :] += jnp.sum(dc * hb, axis=0, keepdims=True)
                w = cw_ref[blk]
                dhh = (w[2:3] * dc + w[1:2] * dp1 + w[0:1] * dp2).astype(BF)
                sh[blk] = dc[0:8, :]
                dhh_ref[blk] = dhh
                dxn = dxn + _mm_nt(dhh, wup_ref[blk])
        dxr, dg = _rms_bwd(dxn, xhat, r, gain)
        dg_ref[...] += dg
        dx_ref[...] = dh + dxr

    blk3 = lambda i: (0, nt - 1 - i, 0)
    return pl.pallas_call(
        body, name="ffn_bwd", grid=(nt,),
        in_specs=[pl.BlockSpec((tm, D), rev), pl.BlockSpec((tm, D), rev),
                  pl.BlockSpec((N_DEV, tm, FF_BLK), blk3), pl.BlockSpec((N_DEV, tm, FF_BLK), blk3),
                  _const_spec((1, D)), _const_spec((N_DEV, D, FF_BLK), True), _const_spec((N_DEV, 3, FF_BLK)),
                  _const_spec((D_FF, D), True)],
        out_specs=[pl.BlockSpec((tm, D), rev), pl.BlockSpec((D, tm), lambda i: (0, nt - 1 - i)),
                   pl.BlockSpec((N_DEV, tm, FF_BLK), blk3), pl.BlockSpec((N_FF_BLK, tm, FF_BLK), blk3),
                   pl.BlockSpec((tm, D), rev), _const_spec((N_DEV, 3, FF_BLK)), _const_spec((N_DEV, 1, FF_BLK)),
                   _const_spec((1, D))],
        out_shape=[jax.ShapeDtypeStruct((S, D), F32), jax.ShapeDtypeStruct((D, S), BF),
                   jax.ShapeDtypeStruct((N_DEV, S, FF_BLK), BF), jax.ShapeDtypeStruct((N_FF_BLK, S, FF_BLK), BF),
                   jax.ShapeDtypeStruct((S, D), BF), jax.ShapeDtypeStruct((N_DEV, 3, FF_BLK), F32),
                   jax.ShapeDtypeStruct((N_DEV, 1, FF_BLK), F32), jax.ShapeDtypeStruct((1, D), F32)],
        scratch_shapes=[pltpu.VMEM((N_DEV, 8, FF_BLK), F32)],
        compiler_params=_params(VMEM_BIG),
    )(dh, h, hh, cc, g, wup, cw, wdown)


def _ple_fwd(h, g, wgate, bgate, p, wple, tm):
    S = h.shape[0]

    def body(h_ref, g_ref, wg_ref, bg_ref, p_ref, wp_ref, o_ref):
        x = h_ref[...]
        xn, _, _ = _rms_fwd(x, g_ref[...])
        gate = _sigmoid(_mm(xn.astype(BF), wg_ref[...]) + bg_ref[...])
        o_ref[...] = x + _mm(p_ref[...], wp_ref[...]) * gate

    return pl.pallas_call(
        body, name="ple_fwd", grid=(S // tm,),
        in_specs=[_row_spec(tm, D), _const_spec((1, D)), _const_spec((D, D)), _const_spec((1, D)),
                  _row_spec(tm, PLE), _const_spec((PLE, D))],
        out_specs=_row_spec(tm, D), out_shape=jax.ShapeDtypeStruct((S, D), F32),
        compiler_params=_params(),
    )(h, g, wgate, bgate, p, wple)


def _ple_bwd(dh, h, g, wgate, bgate, p, wple, tm):
    S = h.shape[0]

    def body(dh_ref, h_ref, g_ref, wg_ref, bg_ref, p_ref, wp_ref, dx_ref, de_ref, xn_ref, dpre_ref, dbg_ref, dg_ref):
        i = pl.program_id(0)

        @pl.when(i == 0)
        def _():
            dbg_ref[...] = jnp.zeros_like(dbg_ref)
            dg_ref[...] = jnp.zeros_like(dg_ref)

        x = h_ref[...]
        gain = g_ref[...]
        xn, xhat, r = _rms_fwd(x, gain)
        xnb = xn.astype(BF)
        xn_ref[...] = xnb
        gate = _sigmoid(_mm(xnb, wg_ref[...]) + bg_ref[...])
        e = _mm(p_ref[...], wp_ref[...])
        dh = dh_ref[...]
        de_ref[...] = (dh * gate).astype(BF)
        dpre = dh * e * gate * (1.0 - gate)
        dpreb = dpre.astype(BF)
        dpre_ref[...] = dpreb
        dbg_ref[...] += jnp.sum(dpre, axis=0, keepdims=True)
        dxr, dg = _rms_bwd(_mm_nt(dpreb, wg_ref[...]), xhat, r, gain)
        dg_ref[...] += dg
        dx_ref[...] = dh + dxr

    return pl.pallas_call(
        body, name="ple_bwd", grid=(S // tm,),
        in_specs=[_row_spec(tm, D), _row_spec(tm, D), _const_spec((1, D)), _const_spec((D, D)), _const_spec((1, D)),
                  _row_spec(tm, PLE), _const_spec((PLE, D))],
        out_specs=[_row_spec(tm, D), _row_spec(tm, D), _row_spec(tm, D), _row_spec(tm, D), _const_spec((1, D)),
                   _const_spec((1, D))],
        out_shape=[jax.ShapeDtypeStruct((S, D), F32), jax.ShapeDtypeStruct((S, D), BF),
                   jax.ShapeDtypeStruct((S, D), BF), jax.ShapeDtypeStruct((S, D), BF),
                   jax.ShapeDtypeStruct((1, D), F32), jax.ShapeDtypeStruct((1, D), F32)],
        compiler_params=_params(),
    )(dh, h, g, wgate, bgate, p, wple)


def _qkv_fwd(h, gq, wq, gkv, wkv, tm):
    S = h.shape[0]
    nkv = wkv.shape[1]

    def body(h_ref, gq_ref, wq_ref, gkv_ref, wkv_ref, q_ref, kv_ref):
        x = h_ref[...]
        _, xhat, _ = _rms_fwd(x, gq_ref[...])
        q_ref[...] = _mm((xhat * gq_ref[...]).astype(BF), wq_ref[...]).astype(BF)
        kv_ref[...] = _mm((xhat * gkv_ref[...]).astype(BF), wkv_ref[...]).astype(BF)

    return pl.pallas_call(
        body, name="qkv_fwd", grid=(S // tm,),
        in_specs=[_row_spec(tm, D), _const_spec((1, D)), _const_spec((D, D)), _const_spec((1, D)),
                  _const_spec((D, nkv))],
        out_specs=[_row_spec(tm, D), _row_spec(tm, nkv)],
        out_shape=[jax.ShapeDtypeStruct((S, D), BF), jax.ShapeDtypeStruct((S, nkv), BF)],
        compiler_params=_params(),
    )(h, gq, wq, gkv, wkv)


def _qkv_bwd(dh, h, dq, dkv, gq, wq, gkv, wkv, tm):
    S = h.shape[0]
    nkv = wkv.shape[1]

    def body(dh_ref, h_ref, dq_ref, dkv_ref, gq_ref, wq_ref, gkv_ref, wkv_ref,
             dx_ref, xq_ref, xkv_ref, dgq_ref, dgkv_ref):
        i = pl.program_id(0)

        @pl.when(i == 0)
        def _():
            dgq_ref[...] = jnp.zeros_like(dgq_ref)
            dgkv_ref[...] = jnp.zeros_like(dgkv_ref)

        x = h_ref[...]
        gq_, gkv_ = gq_ref[...], gkv_ref[...]
        _, xhat, r = _rms_fwd(x, gq_)
        xq_ref[...] = (xhat * gq_).astype(BF)
        xkv_ref[...] = (xhat * gkv_).astype(BF)
        d1, dg1 = _rms_bwd(_mm_nt(dq_ref[...], wq_ref[...]), xhat, r, gq_)
        d2, dg2 = _rms_bwd(_mm_nt(dkv_ref[...], wkv_ref[...]), xhat, r, gkv_)
        dgq_ref[...] += dg1
        dgkv_ref[...] += dg2
        dx_ref[...] = dh_ref[...] + d1 + d2

    return pl.pallas_call(
        body, name="qkv_bwd", grid=(S // tm,),
        in_specs=[_row_spec(tm, D), _row_spec(tm, D), _row_spec(tm, D), _row_spec(tm, nkv), _const_spec((1, D)),
                  _const_spec((D, D)), _const_spec((1, D)), _const_spec((D, nkv))],
        out_specs=[_row_spec(tm, D), _row_spec(tm, D), _row_spec(tm, D), _const_spec((1, D)), _const_spec((1, D))],
        out_shape=[jax.ShapeDtypeStruct((S, D), F32), jax.ShapeDtypeStruct((S, D), BF),
                   jax.ShapeDtypeStruct((S, D), BF), jax.ShapeDtypeStruct((1, D), F32),
                   jax.ShapeDtypeStruct((1, D), F32)],
        compiler_params=_params(),
    )(dh, h, dq, dkv, gq, wq, gkv, wkv)


def _oproj_fwd(a, w, res, tm):
    S = a.shape[0]

    def body(a_ref, w_ref, r_ref, o_ref):
        o_ref[...] = r_ref[...] + _mm(a_ref[...], w_ref[...])

    return pl.pallas_call(
        body, name="oproj_fwd", grid=(S // tm,),
        in_specs=[_row_spec(tm, D), _const_spec((D, D)), _row_spec(tm, D)],
        out_specs=_row_spec(tm, D), out_shape=jax.ShapeDtypeStruct((S, D), F32),
        compiler_params=_params(),
    )(a, w, res)


def _oproj_bwd(dh, w, tm):
    S = dh.shape[0]

    def body(dh_ref, w_ref, da_ref, dhb_ref):
        dhb = dh_ref[...].astype(BF)
        dhb_ref[...] = dhb
        da_ref[...] = _mm_nt(dhb, w_ref[...]).astype(BF)

    return pl.pallas_call(
        body, name="oproj_bwd", grid=(S // tm,),
        in_specs=[_row_spec(tm, D), _const_spec((D, D))],
        out_specs=[_row_spec(tm, D), _row_spec(tm, D)],
        out_shape=[jax.ShapeDtypeStruct((S, D), BF), jax.ShapeDtypeStruct((S, D), BF)],
        compiler_params=_params(),
    )(dh, w)


def _alibi_slope(hq):
    return float(np.float32(2.0 ** (-8.0 * (hq + 1) / N_Q)))


def _attn_consts():
    W = N_KV * HD
    r = np.zeros((W, N_KV * W), np.float32)
    for kh in range(N_KV):
        for c in range(W):
            r[kh * HD + c % HD, kh * W + c] = 1.0
    qi = np.arange(BLOCK)[:, None]
    kj = np.arange(2 * BLOCK)[None, :]
    dist = qi + BLOCK - kj
    band = (dist >= 0) & (dist < BLOCK)
    bias = np.full((2, N_KV, GQA * BLOCK, 2 * BLOCK), NEG, np.float32)
    for later in (0, 1):
        valid = band & ((kj >= BLOCK) | bool(later))
        for kh in range(N_KV):
            for g in range(GQA):
                slope = np.float32(_alibi_slope(kh * GQA + g))
                bias[later, kh, g * BLOCK:(g + 1) * BLOCK] = np.where(valid, -slope * dist.astype(np.float32), NEG)
    lane_head = np.arange(W)[None, :] // HD
    maskq = (np.arange(GQA * BLOCK)[:, None] // BLOCK == lane_head).astype(np.float32)
    maskk = (np.arange(GQA * 2 * BLOCK)[:, None] // (2 * BLOCK) == lane_head).astype(np.float32)
    head_sum = (lane_head.T == np.arange(GQA * BLOCK)[None, :] // BLOCK).astype(np.float32)
    return dict(rep=jnp.asarray(r, BF), bias=jnp.asarray(bias), maskq=jnp.asarray(maskq, BF),
                maskk=jnp.asarray(maskk, BF), head_sum=jnp.asarray(head_sum, BF))


def _sink_col(sink_ref, kh):
    rb = lax.broadcasted_iota(jnp.int32, (GQA * BLOCK, 1), 0) >> 7
    col = jnp.full((GQA * BLOCK, 1), sink_ref[kh * GQA + GQA - 1], F32)
    for g in range(GQA - 1):
        col = jnp.where(rb == g, sink_ref[kh * GQA + g], col)
    return col


def _stack4(a):
    return jnp.concatenate([a] * GQA, axis=0)


def _rows_to_lanes(a):
    return jnp.concatenate([a[g * BLOCK:(g + 1) * BLOCK] for g in range(GQA)], axis=1)


def _lanes_to_rows(a, n):
    return jnp.concatenate([a[:, g * n:(g + 1) * n] for g in range(GQA)], axis=0)


def _scores(qg, k4, bias, sink_col, maskq):
    qs = _stack4(qg) * maskq
    s = _mm_nt(qs, k4) * (HD ** -0.5) + bias
    m = jnp.maximum(jnp.max(s, axis=-1, keepdims=True), sink_col)
    return qs, jnp.exp(s - m), jnp.exp(sink_col - m)


def _bands(kvc_ref, kvp_ref, rep_ref):
    W = N_KV * HD
    kvp, kvc = kvp_ref[...], kvc_ref[...]
    kband = jnp.concatenate([kvp[:, :W], kvc[:, :W]], axis=0)
    vband = jnp.concatenate([kvp[:, W:], kvc[:, W:]], axis=0)
    return _mm(kband, rep_ref[...]).astype(BF), _mm(vband, rep_ref[...]).astype(BF)


def _attn_fwd(q, kv, sinks, consts):
    S = q.shape[0]
    nb = S // BLOCK
    W = N_KV * HD

    def body(sink_ref, q_ref, kvc_ref, kvp_ref, rep_ref, bias_ref, maskq_ref, maskk_ref, o_ref):
        maskq, maskk = maskq_ref[...], maskk_ref[...]
        lane_head = lax.broadcasted_iota(jnp.int32, (BLOCK, W), 1) >> 6
        k4all, v4all = _bands(kvc_ref, kvp_ref, rep_ref)
        for kh in range(N_KV):
            cols = slice(kh * W, (kh + 1) * W)
            _, e, es = _scores(q_ref[:, cols], k4all[:, cols], bias_ref[kh], _sink_col(sink_ref, kh), maskq)
            vbd = jnp.concatenate([_stack4(v4all[:, cols]) * maskk, maskk], axis=1)
            nd = _mm(_rows_to_lanes(e.astype(BF)), vbd)
            es_l = jnp.broadcast_to(es[(GQA - 1) * BLOCK:], (BLOCK, W))
            for g in range(GQA - 1):
                es_l = jnp.where(lane_head == g, es[g * BLOCK:(g + 1) * BLOCK], es_l)
            o_ref[:, cols] = (nd[:, :W] / (nd[:, W:] + es_l)).astype(BF)

    return pl.pallas_call(
        body, name="attn_fwd", grid=(nb,),
        in_specs=[pl.BlockSpec(memory_space=pltpu.SMEM), _row_spec(BLOCK, D), _row_spec(BLOCK, 2 * W),
                  pl.BlockSpec((BLOCK, 2 * W), lambda i: (jnp.maximum(i - 1, 0), 0)), _const_spec((W, N_KV * W)),
                  pl.BlockSpec((None, N_KV, GQA * BLOCK, 2 * BLOCK), lambda i: (jnp.minimum(i, 1), 0, 0, 0)),
                  _const_spec((GQA * BLOCK, W)), _const_spec((GQA * 2 * BLOCK, W))],
        out_specs=_row_spec(BLOCK, D), out_shape=jax.ShapeDtypeStruct((S, D), BF),
        compiler_params=_params(),
    )(sinks, q, kv, kv, consts["rep"], consts["bias"], consts["maskq"], consts["maskk"])


def _attn_bwd(q, kv, att, do, sinks, consts):
    S = q.shape[0]
    nb = S // BLOCK
    W = N_KV * HD

    def qmap(i):
        return (jnp.minimum(i, nb - 1), 0)

    def body(sink_ref, q_ref, kvc_ref, kvp_ref, att_ref, do_ref, rep_ref, bias_ref, maskq_ref, maskk_ref, hsum_ref,
             dq_ref, dkv_ref, dsink_ref, carry, sink_acc):
        i = pl.program_id(0)

        @pl.when(i == 0)
        def _():
            sink_acc[...] = jnp.zeros_like(sink_acc)
            carry[...] = jnp.zeros_like(carry)

        @pl.when(i < nb)
        def _():
            maskq, maskk, head_sum = maskq_ref[...], maskk_ref[...], hsum_ref[...]
            ones_k = jnp.ones((2 * BLOCK, BLOCK), BF)
            k4all, v4all = _bands(kvc_ref, kvp_ref, rep_ref)
            dk4s, dv4s = [], []
            for kh in range(N_KV):
                cols = slice(kh * W, (kh + 1) * W)
                k4, v4 = k4all[:, cols], v4all[:, cols]
                qs, e, es = _scores(q_ref[:, cols], k4, bias_ref[kh], _sink_col(sink_ref, kh), maskq)
                inv = 1.0 / (_mm(e.astype(BF), ones_k) + es)
                p = e * jnp.concatenate([inv, inv], axis=1)
                dog = do_ref[:, cols]
                dos = _stack4(dog) * maskq
                dp = _mm_nt(dos, v4)
                prod = (dog.astype(F32) * att_ref[:, cols].astype(F32)).astype(BF)
                delta = _lanes_to_rows(_mm(prod, head_sum), BLOCK)
                ds = (p * (dp - jnp.concatenate([delta, delta], axis=1))).astype(BF)
                sink_acc[kh] += es * inv * delta
                kbd = _stack4(k4) * maskk
                dq_ref[:, cols] = (_mm(_rows_to_lanes(ds), kbd) * (HD ** -0.5)).astype(BF)
                dk4s.append((_mm_tn(ds, qs) * (HD ** -0.5)).astype(BF))
                dv4s.append(_mm_tn(p.astype(BF), dos).astype(BF))
            dkband = _mm_nt(jnp.concatenate(dk4s, axis=1), rep_ref[...])
            dvband = _mm_nt(jnp.concatenate(dv4s, axis=1), rep_ref[...])
            dkv_ref[:, :W] = (carry[:, :W] + dkband[:BLOCK]).astype(BF)
            dkv_ref[:, W:] = (carry[:, W:] + dvband[:BLOCK]).astype(BF)
            carry[:, :W] = dkband[BLOCK:]
            carry[:, W:] = dvband[BLOCK:]

        @pl.when(i == nb)
        def _():
            dkv_ref[...] = carry[...].astype(BF)
            for hq in range(N_Q):
                kh, g = divmod(hq, GQA)
                dsink_ref[hq:hq + 1, :] = -jnp.sum(sink_acc[kh, g * BLOCK:(g + 1) * BLOCK, :], axis=0, keepdims=True)

    return pl.pallas_call(
        body, name="attn_bwd", grid=(nb + 1,),
        in_specs=[pl.BlockSpec(memory_space=pltpu.SMEM), pl.BlockSpec((BLOCK, D), qmap),
                  pl.BlockSpec((BLOCK, 2 * W), qmap),
                  pl.BlockSpec((BLOCK, 2 * W), lambda i: (jnp.maximum(jnp.minimum(i, nb - 1) - 1, 0), 0)),
                  pl.BlockSpec((BLOCK, D), qmap), pl.BlockSpec((BLOCK, D), qmap), _const_spec((W, N_KV * W)),
                  pl.BlockSpec((None, N_KV, GQA * BLOCK, 2 * BLOCK), lambda i: (jnp.minimum(i, 1), 0, 0, 0)),
                  _const_spec((GQA * BLOCK, W)), _const_spec((GQA * 2 * BLOCK, W)), _const_spec((W, GQA * BLOCK))],
        out_specs=[pl.BlockSpec((BLOCK, D), qmap),
                   pl.BlockSpec((BLOCK, 2 * W), lambda i: (jnp.maximum(i - 1, 0), 0)),
                   _const_spec((N_Q, BLOCK))],
        out_shape=[jax.ShapeDtypeStruct((S, D), BF), jax.ShapeDtypeStruct((S, 2 * W), BF),
                   jax.ShapeDtypeStruct((N_Q, BLOCK), F32)],
        scratch_shapes=[pltpu.VMEM((BLOCK, 2 * W), F32), pltpu.VMEM((N_KV, GQA * BLOCK, BLOCK), F32)],
        compiler_params=_params(),
    )(sinks, q, kv, kv, att, do, consts["rep"], consts["bias"], consts["maskq"], consts["maskk"], consts["head_sum"])


def _head(h, g, target, tm):
    S = h.shape[0]

    def body(h_ref, g_ref, t_ref, dh_ref, loss_ref, dg_ref):
        i = pl.program_id(0)

        @pl.when(i == 0)
        def _():
            loss_ref[...] = jnp.zeros_like(loss_ref)
            dg_ref[...] = jnp.zeros_like(dg_ref)

        gain = g_ref[...]
        y, xhat, r = _rms_fwd(h_ref[...], gain)
        err = y - t_ref[...]
        tile = 0.5 * jnp.sum(jnp.sum(err * err, axis=-1, keepdims=True) * (1.0 / D), axis=0, keepdims=True)
        loss_ref[...] += jnp.broadcast_to(tile, loss_ref.shape)
        dx, dg = _rms_bwd(err * (1.0 / D), xhat, r, gain)
        dg_ref[...] += dg
        dh_ref[...] = dx

    return pl.pallas_call(
        body, name="loss_head", grid=(S // tm,),
        in_specs=[_row_spec(tm, D), _const_spec((1, D)), _row_spec(tm, D)],
        out_specs=[_row_spec(tm, D), _const_spec((8, 128)), _const_spec((1, D))],
        out_shape=[jax.ShapeDtypeStruct((S, D), F32), jax.ShapeDtypeStruct((8, 128), F32),
                   jax.ShapeDtypeStruct((1, D), F32)],
        compiler_params=_params(),
    )(h, g, target)


def _wgrad(a, b, name, *, tn, out_blocked=False, a_transposed=False, ts=1024, per_step=1):
    a_blocked, b_blocked = a.ndim == 3, b.ndim == 3
    na = a.shape[0] if a_blocked else 1
    S = b.shape[-2]
    K = a.shape[0] if a_transposed else a.shape[-1]
    wide = tn * per_step
    nbk = b.shape[0] if b_blocked else b.shape[1] // wide
    ts = min(ts, S)
    ns = S // ts
    assert not (a_blocked and (out_blocked or a_transposed)) and (per_step == 1 or (out_blocked and not b_blocked))

    def body(a_ref, b_ref, o_ref, acc):
        s = pl.program_id(2)

        @pl.when(s == 0)
        def _():
            acc[...] = jnp.zeros_like(acc)

        if a_transposed:
            acc[...] += _mm(a_ref[...], b_ref[...])
        else:
            acc[...] += _mm_tn(a_ref[...], b_ref[...])

        @pl.when(s == ns - 1)
        def _():
            if per_step == 1:
                o_ref[...] = acc[...].astype(o_ref.dtype)
            else:
                for k in range(per_step):
                    o_ref[k] = acc[:, k * tn:(k + 1) * tn].astype(o_ref.dtype)

    if a_transposed:
        a_spec = pl.BlockSpec((K, ts), lambda i, j, s: (0, s))
    elif a_blocked:
        a_spec = pl.BlockSpec((None, ts, K), lambda i, j, s: (i, s, 0))
    else:
        a_spec = pl.BlockSpec((ts, K), lambda i, j, s: (s, 0))
    b_spec = (pl.BlockSpec((None, ts, tn), lambda i, j, s: (j, s, 0)) if b_blocked
              else pl.BlockSpec((ts, wide), lambda i, j, s: (s, j)))
    if out_blocked:
        o_spec = pl.BlockSpec((None, K, tn) if per_step == 1 else (per_step, K, tn), lambda i, j, s: (j, 0, 0))
        o_shape = jax.ShapeDtypeStruct((nbk * per_step, K, tn), BF)
    else:
        o_spec = pl.BlockSpec((K, wide), lambda i, j, s: (i, j))
        o_shape = jax.ShapeDtypeStruct((na * K, nbk * wide), BF)
    return pl.pallas_call(
        body, name=name, grid=(na, nbk, ns), in_specs=[a_spec, b_spec], out_specs=o_spec, out_shape=o_shape,
        scratch_shapes=[pltpu.VMEM((K, wide), F32)],
        compiler_params=pltpu.CompilerParams(dimension_semantics=("arbitrary", "arbitrary", "arbitrary"),
                                             vmem_limit_bytes=VMEM_BIG),
    )(a, b)


def _adamw_math(w, g, m, v):
    m = B1 * m + (1.0 - B1) * g
    v = B2 * v + (1.0 - B2) * (g * g)
    m_hat = m / (1.0 - B1 ** STEP)
    v_hat = v / (1.0 - B2 ** STEP)
    return -LR * (m_hat / (jnp.sqrt(v_hat) + ADAM_EPS) + WD * w), m, v


def _sum8(parts, name):
    R = parts.shape[1]

    def body(p_ref, o_ref):
        g = p_ref[0]
        for s in range(1, N_DEV):
            g = g + p_ref[s]
        o_ref[...] = g

    return pl.pallas_call(
        body, name=name, grid=(1,),
        in_specs=[pl.BlockSpec((N_DEV, R, ROW), lambda i: (0, 0, 0))],
        out_specs=pl.BlockSpec((R, ROW), lambda i: (0, 0)), out_shape=jax.ShapeDtypeStruct((R, ROW), F32),
        compiler_params=_params(),
    )(parts)


def _adamw_small(g, w, m, v, name):
    R = w.shape[0]

    def body(g_ref, w_ref, m_ref, v_ref, d_ref, nm_ref, nv_ref):
        d_ref[...], nm_ref[...], nv_ref[...] = _adamw_math(w_ref[...], g_ref[...], m_ref[...], v_ref[...])

    spec = pl.BlockSpec((R, ROW), lambda i: (0, 0))
    return pl.pallas_call(
        body, name=name, grid=(1,), in_specs=[spec] * 4, out_specs=[spec] * 3,
        out_shape=[jax.ShapeDtypeStruct((R, ROW), F32)] * 3, compiler_params=_params(),
    )(g, w, m, v)


def _mesh_pos():
    return lax.axis_index("x"), lax.axis_index("y"), lax.axis_index("c")


def _flip(v, bit):
    return 1 - v if bit else v


def _peers(x, y, c):
    out = []
    for k in range(1, N_DEV):
        px, py, pc = _flip(x, k & 4), _flip(y, k & 2), _flip(c, k & 1)
        out.append((k - 1, (px, py, pc), 4 * px + 2 * py + pc))
    return out


def _send_start(srcs, blocked, name):
    n = len(srcs)
    MESH = pl.DeviceIdType.MESH

    def body(*refs):
        src_refs, land_refs = refs[:n], refs[n:2 * n]
        send_sems, recv_sems, local_sems = refs[2 * n:3 * n], refs[3 * n:4 * n], refs[4 * n:5 * n]
        token = refs[7 * n]
        x, y, c = _mesh_pos()
        me = 4 * x + 2 * y + c
        for a in range(n):
            for k, peer, blk in _peers(x, y, c):
                pltpu.make_async_remote_copy(
                    src_ref=src_refs[a].at[blk] if blocked else src_refs[a], dst_ref=land_refs[a].at[me],
                    send_sem=send_sems[a].at[k], recv_sem=recv_sems[a].at[k], device_id=peer,
                    device_id_type=MESH).start()
            pltpu.make_async_copy(src_refs[a].at[me] if blocked else src_refs[a], land_refs[a].at[me],
                                  local_sems[a]).start()
        token[...] = jnp.zeros_like(token)

    hbm = pl.BlockSpec(memory_space=pltpu.HBM)
    sem = pl.BlockSpec(memory_space=pltpu.SEMAPHORE)
    lshape = [s.shape if blocked else (N_DEV,) + s.shape for s in srcs]
    outs = pl.pallas_call(
        body, name=name,
        out_shape=([pltpu.SemaphoreType.DMA((N_DEV - 1,))] * (2 * n) + [pltpu.SemaphoreType.DMA(())] * n
                   + [pltpu.HBM(s.shape, s.dtype) for s in srcs]
                   + [pltpu.HBM(ls, s.dtype) for ls, s in zip(lshape, srcs)] + [jax.ShapeDtypeStruct((8, 128), F32)]),
        in_specs=[hbm] * (2 * n), out_specs=[sem] * (3 * n) + [hbm] * (2 * n) + [pl.BlockSpec(memory_space=pltpu.VMEM)],
        input_output_aliases={**{a: 3 * n + a for a in range(n)}, **{n + a: 4 * n + a for a in range(n)}},
        compiler_params=pltpu.CompilerParams(has_side_effects=pltpu.SideEffectType.DATAFLOW_SIDE_EFFECTING),
    )(*[pltpu.with_memory_space_constraint(s, pltpu.HBM) for s in srcs],
      *[pltpu.with_memory_space_constraint(lax.empty(ls, s.dtype), pltpu.HBM) for ls, s in zip(lshape, srcs)])
    return dict(send=outs[:n], recv=outs[n:2 * n], local=outs[2 * n:3 * n], srcs=outs[3 * n:4 * n],
                lands=outs[4 * n:5 * n], token=outs[5 * n], blocked=blocked)


def _send_wait(started, which, after, name):
    n = len(which)
    blocked = started["blocked"]
    pick = lambda key: [started[key][a] for a in which]
    MESH = pl.DeviceIdType.MESH

    def body(*refs):
        src_refs, land_refs = refs[:n], refs[n:2 * n]
        send, recv, local = refs[2 * n:3 * n], refs[3 * n:4 * n], refs[4 * n:5 * n]
        x, y, c = _mesh_pos()
        me = 4 * x + 2 * y + c
        for a in range(n):
            for k, peer, blk in _peers(x, y, c):
                cp = pltpu.make_async_remote_copy(
                    src_ref=src_refs[a].at[blk] if blocked else src_refs[a], dst_ref=land_refs[a].at[blk],
                    send_sem=send[a].at[k], recv_sem=recv[a].at[k], device_id=peer, device_id_type=MESH)
                cp.wait_send()
                cp.wait_recv()
            pltpu.make_async_copy(src_refs[a].at[me] if blocked else src_refs[a], land_refs[a].at[me],
                                  local[a]).wait()

    hbm = pl.BlockSpec(memory_space=pltpu.HBM)
    sem = pl.BlockSpec(memory_space=pltpu.SEMAPHORE)
    srcs, lands = pick("srcs"), pick("lands")
    outs = pl.pallas_call(
        body, name=name, out_shape=[pltpu.HBM(s.shape, s.dtype) for s in srcs + lands],
        in_specs=[hbm] * (2 * n) + [sem] * (3 * n) + [pl.BlockSpec(memory_space=pl.ANY)], out_specs=[hbm] * (2 * n),
        input_output_aliases={a: a for a in range(2 * n)},
        compiler_params=pltpu.CompilerParams(has_side_effects=pltpu.SideEffectType.DATAFLOW_SIDE_EFFECTING),
    )(*srcs, *lands, *pick("send"), *pick("recv"), *pick("local"), after)
    return outs[n:]


def _adamw_weight(parts, w, m, v, name):
    L, R, C = w.shape
    tr = max(t for t in range(16, min(R, 256) + 1, 16) if R % t == 0)
    nr = R // tr

    def body(*refs):
        p_refs = refs[:L]
        w_ref, m_ref, v_ref, g_ref, d_ref, nm_ref, nv_ref, gsum = refs[L:]
        layer = pl.program_id(0)
        for l in range(L):
            @pl.when(layer == l)
            def _(l=l):
                g = p_refs[l][0].astype(F32)
                for s in range(1, N_DEV):
                    g = g + p_refs[l][s].astype(F32)
                gsum[...] = g
        g = gsum[...]
        g_ref[...] = g
        d_ref[...], nm_ref[...], nv_ref[...] = _adamw_math(w_ref[...], g, m_ref[...], v_ref[...])

    def part_spec(l):
        return pl.BlockSpec((N_DEV, tr, C), lambda layer, i: (0, jnp.where(layer == l, i, jnp.where(layer < l, 0, nr - 1)), 0))

    ws = pl.BlockSpec((None, tr, C), lambda layer, i: (layer, i, 0))
    return pl.pallas_call(
        body, name=name, grid=(L, nr), in_specs=[part_spec(l) for l in range(L)] + [ws] * 3, out_specs=[ws] * 4,
        out_shape=[jax.ShapeDtypeStruct((L, R, C), F32)] * 4, scratch_shapes=[pltpu.VMEM((tr, C), F32)],
        compiler_params=pltpu.CompilerParams(dimension_semantics=("arbitrary", "arbitrary")),
    )(*parts, w, m, v)


def _seg_rows(n, align):
    rows = -(-n // ROW)
    return -(-rows // align) * align


def _pack(arrs, align, lead=()):
    parts = []
    for a in arrs:
        n = int(np.prod(a.shape[len(lead):]))
        rows = _seg_rows(n, align)
        flat = a.reshape(lead + (n,))
        flat = jnp.pad(flat, [(0, 0)] * len(lead) + [(0, rows * ROW - n)])
        parts.append(flat.reshape(lead + (rows, ROW)))
    return jnp.concatenate(parts, axis=len(lead))


def _unpack(buf, shapes, align, lead=()):
    out, r0 = [], 0
    for shp in shapes:
        n = int(np.prod(shp))
        rows = _seg_rows(n, align)
        seg = lax.slice_in_dim(buf, r0, r0 + rows, axis=len(lead))
        out.append(seg.reshape(lead + (rows * ROW,))[..., :n].reshape(lead + tuple(shp)))
        r0 += rows
    return out


def kernel(x, p, norm_mix, norm_ffn, norm_ple, norm_kv, norm_final, a_w_in, a_norm_v, a_w_s, a_b_s, a_w_out, w_kv, b_w_q, b_sinks, b_w_o, f_w_up, f_conv_w, f_conv_b, f_w_down, ple_w_in, ple_w_gate, ple_b_gate, loss_target, m_norm_mix, m_norm_ffn, m_norm_ple, m_norm_kv, m_norm_final, m_a_w_in, m_a_norm_v, m_a_w_s, m_a_b_s, m_a_w_out, m_w_kv, m_b_w_q, m_b_sinks, m_b_w_o, m_f_w_up, m_f_conv_w, m_f_conv_b, m_f_w_down, m_ple_w_in, m_ple_w_gate, m_ple_b_gate, v_norm_mix, v_norm_ffn, v_norm_ple, v_norm_kv, v_norm_final, v_a_w_in, v_a_norm_v, v_a_w_s, v_a_b_s, v_a_w_out, v_w_kv, v_b_w_q, v_b_sinks, v_b_w_o, v_f_w_up, v_f_conv_w, v_f_conv_b, v_f_w_down, v_ple_w_in, v_ple_w_gate, v_ple_b_gate):
    w = dict(norm_mix=norm_mix, norm_ffn=norm_ffn, norm_ple=norm_ple, norm_kv=norm_kv, norm_final=norm_final,
             a_w_in=a_w_in, a_norm_v=a_norm_v, a_w_s=a_w_s, a_b_s=a_b_s, a_w_out=a_w_out, w_kv=w_kv, b_w_q=b_w_q,
             b_sinks=b_sinks, b_w_o=b_w_o, f_w_up=f_w_up, f_conv_w=f_conv_w, f_conv_b=f_conv_b, f_w_down=f_w_down,
             ple_w_in=ple_w_in, ple_w_gate=ple_w_gate, ple_b_gate=ple_b_gate)
    mom_m = dict(norm_mix=m_norm_mix, norm_ffn=m_norm_ffn, norm_ple=m_norm_ple, norm_kv=m_norm_kv,
                 norm_final=m_norm_final, a_w_in=m_a_w_in, a_norm_v=m_a_norm_v, a_w_s=m_a_w_s, a_b_s=m_a_b_s,
                 a_w_out=m_a_w_out, w_kv=m_w_kv, b_w_q=m_b_w_q, b_sinks=m_b_sinks, b_w_o=m_b_w_o, f_w_up=m_f_w_up,
                 f_conv_w=m_f_conv_w, f_conv_b=m_f_conv_b, f_w_down=m_f_w_down, ple_w_in=m_ple_w_in,
                 ple_w_gate=m_ple_w_gate, ple_b_gate=m_ple_b_gate)
    mom_v = dict(norm_mix=v_norm_mix, norm_ffn=v_norm_ffn, norm_ple=v_norm_ple, norm_kv=v_norm_kv,
                 norm_final=v_norm_final, a_w_in=v_a_w_in, a_norm_v=v_a_norm_v, a_w_s=v_a_w_s, a_b_s=v_a_b_s,
                 a_w_out=v_a_w_out, w_kv=v_w_kv, b_w_q=v_b_w_q, b_sinks=v_b_sinks, b_w_o=v_b_w_o, f_w_up=v_f_w_up,
                 f_conv_w=v_f_conv_w, f_conv_b=v_f_conv_b, f_w_down=v_f_w_down, ple_w_in=v_ple_w_in,
                 ple_w_gate=v_ple_w_gate, ple_b_gate=v_ple_b_gate)
    S = x.shape[1]
    tm = min(256, S)
    tl = min(512, S)
    me = 4 * lax.axis_index("x") + 2 * lax.axis_index("y") + lax.axis_index("c")

    def after_token(tok, a):
        return a + tok[0:1, 0:1].reshape((1,) * a.ndim)

    wb = {n: w[n].astype(BF) for n in _BIG}
    order = [('a_w_in', 0), ('a_w_out', 0), ('a_norm_v', None), ('f_conv_w', 0), ('f_conv_w', 1),
             ('f_w_up', 0), ('f_w_down', 0),
             ('ple_w_in', 0), ('ple_w_gate', 0), ('w_kv', None), ('b_w_q', 0), ('b_w_o', 0),
             ('f_w_up', 1), ('f_w_down', 1), ('ple_w_in', 1), ('ple_w_gate', 1)]
    src = lambda n, l: (wb[n] if n in wb else w[n]) if l is None else (wb[n] if n in wb else w[n])[l]
    gath = _send_start([src(n, l) for n, l in order], False, "gather_start")
    x0 = x[0]

    def gathered(names, after, name):
        which = [order.index(k) for k in names]
        return dict(zip(names, _send_wait(gath, which, after, name)))

    g = gathered(order[:5], x0, "gather_wait_mix")
    win = g[('a_w_in', 0)].transpose(1, 0, 2).reshape(D, 2 * AW)
    wout = g[('a_w_out', 0)].reshape(AW, D)
    gv = g[('a_norm_v', None)].reshape(1, AW)
    cw = [g[('f_conv_w', l)] for l in range(2)]
    cb = [w['f_conv_b'][l].reshape(N_DEV, 1, FF_BLK) for l in range(2)]
    ws = a_w_s[0].astype(BF)
    wst = jnp.swapaxes(ws, 1, 2)
    bsb = jnp.broadcast_to(a_b_s[0][:, :, None], (GROUPS, CHUNK, CHUNK))
    row = lambda a: a.reshape(1, -1)
    pb = [p[l, 0].astype(BF) for l in range(2)]
    sinks = b_sinks.reshape(N_Q)
    consts = _attn_consts()

    h1, z = _mixa_fwd(x0, row(norm_mix[0]), win, gv, ws, bsb, wout, tm)
    g = gathered(order[5:7], h1, "gather_wait_ffn0")
    wup, wdown = [g[('f_w_up', 0)], None], [g[('f_w_down', 0)].reshape(D_FF, D), None]
    h2, hh0, cc0 = _ffn_fwd(h1, row(norm_ffn[0]), wup[0], cw[0], cb[0], wdown[0], tm)
    g = gathered(order[7:12], h2, "gather_wait_mid")
    wple = [g[('ple_w_in', 0)].transpose(1, 0, 2).reshape(PLE, D), None]
    wgate = [g[('ple_w_gate', 0)].reshape(D, D), None]
    wkv, wq, wo = g[('w_kv', None)].reshape(D, 2 * N_KV * HD), g[('b_w_q', 0)].reshape(D, D), g[('b_w_o', 0)].reshape(D, D)
    h3 = _ple_fwd(h2, row(norm_ple[0]), wgate[0], row(ple_b_gate[0]), pb[0], wple[0], tl)
    q, kv = _qkv_fwd(h3, row(norm_mix[1]), wq, row(norm_kv), wkv, tl)
    att = _attn_fwd(q, kv, sinks, consts)
    h4 = _oproj_fwd(att, wo, h3, tl)
    g = gathered(order[12:], h4, "gather_wait_l1")
    wup[1], wdown[1] = g[('f_w_up', 1)], g[('f_w_down', 1)].reshape(D_FF, D)
    wple[1], wgate[1] = g[('ple_w_in', 1)].transpose(1, 0, 2).reshape(PLE, D), g[('ple_w_gate', 1)].reshape(D, D)
    h5, hh1, cc1 = _ffn_fwd(h4, row(norm_ffn[1]), wup[1], cw[1], cb[1], wdown[1], tm)
    h6 = _ple_fwd(h5, row(norm_ple[1]), wgate[1], row(ple_b_gate[1]), pb[1], wple[1], tl)

    def col_blocks(g2d, n):
        return g2d.reshape(g2d.shape[0], N_DEV, n).transpose(1, 0, 2)

    def row_blocks(g2d):
        return g2d.reshape(N_DEV, g2d.shape[0] // N_DEV, g2d.shape[1])

    def ple_grads(l, de, xp, dpre):
        return [col_blocks(_wgrad(pb[l], de, f"wgrad_ple_in{l}", tn=1024), D // N_DEV),
                row_blocks(_wgrad(xp, dpre, f"wgrad_gate{l}", tn=1024))]

    def ffn_grads(l, xf, dhh, act, dhb):
        return [_wgrad(xf, dhh, f"wgrad_up{l}", tn=FF_BLK, out_blocked=True, a_transposed=True),
                row_blocks(_wgrad(act, dhb, f"wgrad_down{l}", tn=1024))]

    dh6, loss_blk, d_norm_final = _head(h6, row(norm_final), loss_target[0], tl)
    dh5, de1, xp1, dpre1, dbg1, dgp1 = _ple_bwd(dh6, h5, row(norm_ple[1]), wgate[1], row(ple_b_gate[1]), pb[1],
                                                 wple[1], tl)
    ex_ple1 = _send_start(ple_grads(1, de1, xp1, dpre1), True, "send_ple1")
    dh4, xf1, dhh1, act1, dhb5, dcw1, dcb1, dgf1 = _ffn_bwd(dh5, h4, hh1, cc1, after_token(ex_ple1["token"], row(norm_ffn[1])),
                                                            wup[1], cw[1], wdown[1], tm)
    ex_ffn1 = _send_start(ffn_grads(1, xf1, dhh1, act1, dhb5), True, "send_ffn1")
    datt, dhb4 = _oproj_bwd(dh4, wo, tl)
    dq, dkv, dsink = _attn_bwd(q, kv, att, datt, after_token(ex_ffn1["token"], sinks.reshape(1, N_Q)).reshape(N_Q), consts)
    dh3, xq, xkv, dgq, dgkv = _qkv_bwd(dh4, h3, dq, dkv, row(norm_mix[1]), wq, row(norm_kv), wkv, tl)
    ex_att = _send_start([row_blocks(_wgrad(att, dhb4, "wgrad_o", tn=1024)),
                              row_blocks(_wgrad(xq, dq, "wgrad_q", tn=1024)),
                              row_blocks(_wgrad(xkv, dkv, "wgrad_kv", tn=512))], True, "send_att")
    dh2, de0, xp0, dpre0, dbg0, dgp0 = _ple_bwd(dh3, h2, after_token(ex_att["token"], row(norm_ple[0])), wgate[0],
                                                 row(ple_b_gate[0]), pb[0], wple[0], tl)
    ex_ple0 = _send_start(ple_grads(0, de0, xp0, dpre0), True, "send_ple0")
    dh1, xf0, dhh0, act0, dhb2, dcw0, dcb0, dgf0 = _ffn_bwd(dh2, h1, hh0, cc0, after_token(ex_ple0["token"], row(norm_ffn[0])),
                                                            wup[0], cw[0], wdown[0], tm)
    ex_ffn0 = _send_start(ffn_grads(0, xf0, dhh0, act0, dhb2), True, "send_ffn0")
    dx, xa, dz, gated, dhb1, dws, dbs, dgv, dga = _mixa_bwd(dh1, x0, z, after_token(ex_ffn0["token"], row(norm_mix[0])), win, gv,
                                                            ws, wst, bsb, wout, tm)
    small_g = {
        'norm_mix': jnp.concatenate([dga, dgq], axis=0), 'norm_ffn': jnp.concatenate([dgf0, dgf1], axis=0),
        'norm_ple': jnp.concatenate([dgp0, dgp1], axis=0), 'norm_kv': dgkv.reshape(D),
        'norm_final': d_norm_final.reshape(D), 'a_w_s': dws[None], 'a_b_s': dbs[None, :, :, 0],
        'b_sinks': dsink[:, 0].reshape(1, N_Q),
        'f_conv_b': jnp.stack([dcb0.reshape(2 * D_FF), dcb1.reshape(2 * D_FF)]),
        'ple_b_gate': jnp.concatenate([dbg0, dbg1], axis=0),
        'a_norm_v': dgv, 'f_conv_w': jnp.stack([dcw0, dcw1], axis=1),
    }
    names_all = _REPL + _SMALL_SHARDED
    full_shapes = [small_g[n].shape for n in names_all]
    small_packed = _pack([small_g[n] for n in names_all], 8)
    ex_small = _send_start([small_packed.reshape(N_DEV, small_packed.shape[0] // N_DEV, ROW)], True, "send_small")
    ex_a_in = _send_start([_wgrad(xa, dz, "wgrad_a_in", tn=2 * AW // N_DEV, out_blocked=True, a_transposed=True,
                                  per_step=4)], True, "send_a_in")
    ex_a_out = _send_start([row_blocks(_wgrad(gated, dhb1, "wgrad_a_out", tn=1024))],
                           True, "send_a_out")
    small_sum = _sum8(_send_wait(ex_small, [0], ex_a_out["token"], "recv_small")[0], "sum_small_grads")
    ex_small_all = _send_start([small_sum], False, "gather_small_sums")

    def as3(a):
        return a.reshape((-1,) + a.shape[-2:])

    def adam(name, parts):
        outs = _adamw_weight(parts, as3(w[name]), as3(mom_m[name]), as3(mom_v[name]), "adamw_" + name)
        return [o.reshape(w[name].shape) for o in outs]

    big = {}
    l_ple1 = _send_wait(ex_ple1, list(range(len(ex_ple1["lands"]))), ex_small_all["token"], "recv_ple1")
    l_ffn1 = _send_wait(ex_ffn1, list(range(len(ex_ffn1["lands"]))), l_ple1[0], "recv_ffn1")
    l_att = _send_wait(ex_att, list(range(len(ex_att["lands"]))), l_ffn1[0], "recv_att")
    big['b_w_o'], big['b_w_q'], big['w_kv'] = adam('b_w_o', [l_att[0]]), adam('b_w_q', [l_att[1]]), adam('w_kv', [l_att[2]])
    l_ple0 = _send_wait(ex_ple0, list(range(len(ex_ple0["lands"]))), big['w_kv'][0], "recv_ple0")
    big['ple_w_in'] = adam('ple_w_in', [l_ple0[0], l_ple1[0]])
    big['ple_w_gate'] = adam('ple_w_gate', [l_ple0[1], l_ple1[1]])
    l_ffn0 = _send_wait(ex_ffn0, list(range(len(ex_ffn0["lands"]))), big['ple_w_gate'][0], "recv_ffn0")
    big['f_w_up'] = adam('f_w_up', [l_ffn0[0], l_ffn1[0]])
    big['f_w_down'] = adam('f_w_down', [l_ffn0[1], l_ffn1[1]])
    big['a_w_in'] = adam('a_w_in', _send_wait(ex_a_in, [0], big['f_w_down'][0], "recv_a_in"))
    big['a_w_out'] = adam('a_w_out', _send_wait(ex_a_out, [0], big['a_w_in'][0], "recv_a_out"))
    g_big, d_big, m_big, v_big = [{n: big[n][i] for n in _BIG} for i in range(4)]

    gsum = _send_wait(ex_small_all, [0], big['a_w_out'][0], "recv_small_sums")[0].reshape(small_packed.shape)
    gs = dict(zip(names_all, _unpack(gsum, full_shapes, 8)))
    gs['a_norm_v'] = lax.dynamic_slice_in_dim(gs['a_norm_v'], me * (AW // N_DEV), AW // N_DEV, axis=1)
    gs['f_conv_w'] = lax.dynamic_index_in_dim(gs['f_conv_w'], me, axis=0, keepdims=False)
    gs = {n: gs[n].reshape(w[n].shape) for n in names_all}
    packs = lambda d: _pack([d[n] for n in names_all], 8)
    souts = _adamw_small(packs(gs), packs(w), packs(mom_m), packs(mom_v), "adamw_small")
    shp = [w[n].shape for n in names_all]
    d_s, m_s, v_s = [dict(zip(names_all, _unpack(o, shp, 8))) for o in souts]

    grads = {**g_big, **gs}
    delta = {**d_big, **d_s}
    new_m = {**m_big, **m_s}
    new_v = {**v_big, **v_s}
    loss = lax.psum(loss_blk[0, 0], ("x", "y", "c"))
    return (loss, dx[None], *[grads[n] for n in _PARAMS], *[delta[n] for n in _PARAMS],
            *[new_m[n] for n in _PARAMS], *[new_v[n] for n in _PARAMS])
```

```python
import functools

import numpy as np
import jax
import jax.numpy as jnp
from jax import lax
from jax.experimental import pallas as pl
from jax.experimental.pallas import tpu as pltpu

F32 = jnp.float32
BF = jnp.bfloat16

N_DEV = 8
D = 1024
AW = 1024
GROUPS = 8
CHUNK = 128
HD = 64
N_Q = 16
N_KV = 4
GQA = 4
BLOCK = 128
D_FF = 2816
FF_BLK = 704
N_FF_BLK = 4
PLE = 256
EPS = 1e-6
LR, B1, B2, ADAM_EPS, WD, STEP = 0.001, 0.9, 0.999, 1e-08, 0.01, 10
GELU_C0 = 0.7978845608028654
GELU_C1 = 0.044715
NEG = -1e30
ROW = 1024
VMEM_BIG = 56 * 1024 * 1024

_PARAMS = ['norm_mix', 'norm_ffn', 'norm_ple', 'norm_kv', 'norm_final', 'a_w_in', 'a_norm_v', 'a_w_s', 'a_b_s',
           'a_w_out', 'w_kv', 'b_w_q', 'b_sinks', 'b_w_o', 'f_w_up', 'f_conv_w', 'f_conv_b', 'f_w_down',
           'ple_w_in', 'ple_w_gate', 'ple_b_gate']
_BIG = ['a_w_in', 'a_w_out', 'w_kv', 'b_w_q', 'b_w_o', 'f_w_up', 'f_w_down', 'ple_w_in', 'ple_w_gate']
_SMALL_SHARDED = ['a_norm_v', 'f_conv_w']
_REPL = ['norm_mix', 'norm_ffn', 'norm_ple', 'norm_kv', 'norm_final', 'a_w_s', 'a_b_s', 'b_sinks', 'f_conv_b',
         'ple_b_gate']


def _mm(a, b):
    return jnp.dot(a, b, preferred_element_type=F32)


def _mm_nt(a, b):
    return lax.dot_general(a, b, (((1,), (1,)), ((), ())), preferred_element_type=F32)


def _mm_tn(a, b):
    return lax.dot_general(a, b, (((0,), (0,)), ((), ())), preferred_element_type=F32)


def _const_spec(shape, single=False):
    nd = len(shape)
    if single:
        return pl.BlockSpec(shape, lambda *_: (0,) * nd, pipeline_mode=pl.Buffered(1))
    return pl.BlockSpec(shape, lambda *_: (0,) * nd)


def _row_spec(tm, n):
    return pl.BlockSpec((tm, n), lambda i: (i, 0))


def _rms_fwd(x, g):
    r = lax.rsqrt(jnp.mean(x * x, axis=-1, keepdims=True) + EPS)
    xhat = x * r
    return xhat * g, xhat, r


def _rms_bwd(dy, xhat, r, g):
    a = dy * g
    dx = r * (a - xhat * jnp.mean(a * xhat, axis=-1, keepdims=True))
    return dx, jnp.sum(dy * xhat, axis=0, keepdims=True)


def _sigmoid(x):
    return 1.0 / (1.0 + jnp.exp(-x))


def _params(vmem=None):
    return pltpu.CompilerParams(dimension_semantics=("arbitrary",), vmem_limit_bytes=vmem)


def _tri_masks():
    t = lax.broadcasted_iota(jnp.int32, (CHUNK, CHUNK), 0)
    s = lax.broadcasted_iota(jnp.int32, (CHUNK, CHUNK), 1)
    return t >= s, s >= t


def _mixa_fwd(x, g, win, gv, ws, bsb, wout, tm):
    S = x.shape[0]
    nc = tm // CHUNK

    def body(x_ref, g_ref, win_ref, gv_ref, ws_ref, bsb_ref, wout_ref, h_ref, zg_ref, gp_ref, gated):
        x = x_ref[...]
        xn, _, _ = _rms_fwd(x, g_ref[...])
        z = _mm(xn.astype(BF), win_ref[...])
        z2 = z * z
        t = jnp.tanh(GELU_C0 * (z + GELU_C1 * z2 * z))
        zg = 0.5 * z * (1.0 + t)
        zg_ref[...] = zg.astype(BF)
        gp_ref[...] = (0.5 * (1.0 + t) + 0.5 * z * (1.0 - t * t) * (GELU_C0 * (1.0 + 3.0 * GELU_C1 * z2))).astype(BF)
        u = zg[:, :AW]
        vn, _, _ = _rms_fwd(zg[:, AW:], gv_ref[...])
        vn = vn.astype(BF)
        tril, _ = _tri_masks()
        for h in range(GROUPS):
            wm = jnp.where(tril, ws_ref[h], jnp.zeros((), BF))
            for c in range(nc):
                rs, cs = slice(c * CHUNK, (c + 1) * CHUNK), slice(h * CHUNK, (h + 1) * CHUNK)
                s = _mm(wm, vn[rs, cs]) + bsb_ref[h]
                gated[rs, cs] = (u[rs, cs] * s).astype(BF)
        h_ref[...] = x + _mm(gated[...], wout_ref[...])

    return pl.pallas_call(
        body, name="mixa_fwd", grid=(S // tm,),
        in_specs=[_row_spec(tm, D), _const_spec((1, D)), _const_spec((D, 2 * AW)), _const_spec((1, AW)),
                  _const_spec((GROUPS, CHUNK, CHUNK)), _const_spec((GROUPS, CHUNK, CHUNK)), _const_spec((AW, D))],
        out_specs=[_row_spec(tm, D), _row_spec(tm, 2 * AW), _row_spec(tm, 2 * AW)],
        out_shape=[jax.ShapeDtypeStruct((S, D), F32), jax.ShapeDtypeStruct((S, 2 * AW), BF),
                   jax.ShapeDtypeStruct((S, 2 * AW), BF)],
        scratch_shapes=[pltpu.VMEM((tm, AW), BF)],
        compiler_params=_params(VMEM_BIG),
    )(x, g, win, gv, ws, bsb, wout)


def _mixa_bwd(dh, x, zg, gp, g, win, gv, ws, wst, bsb, wout, tm):
    S = x.shape[0]
    nt = S // tm
    nc = tm // CHUNK

    def body(dh_ref, x_ref, zg_ref, gp_ref, g_ref, win_ref, gv_ref, ws_ref, wst_ref, bsb_ref, wout_ref,
             dx_ref, xn_ref, dz_ref, gated_ref, dhb_ref, dws_ref, dbs_ref, dgv_ref, dg_ref, du_scr, dvn_scr):
        i = pl.program_id(0)

        @pl.when(i == 0)
        def _():
            dws_ref[...] = jnp.zeros_like(dws_ref)
            dbs_ref[...] = jnp.zeros_like(dbs_ref)
            dgv_ref[...] = jnp.zeros_like(dgv_ref)
            dg_ref[...] = jnp.zeros_like(dg_ref)

        x = x_ref[...]
        gmix = g_ref[...]
        xn, xhat, r = _rms_fwd(x, gmix)
        xn_ref[...] = xn.T.astype(BF)
        zg = zg_ref[...].astype(F32)
        gp = gp_ref[...].astype(F32)
        u = zg[:, :AW]
        gvv = gv_ref[...]
        vn, vhat, rv = _rms_fwd(zg[:, AW:], gvv)
        vn = vn.astype(BF)
        dh = dh_ref[...]
        dhb = dh.astype(BF)
        dhb_ref[...] = dhb
        dgated = _mm_nt(dhb, wout_ref[...])
        tril, triu = _tri_masks()
        for h in range(GROUPS):
            wm = jnp.where(tril, ws_ref[h], jnp.zeros((), BF))
            wmt = jnp.where(triu, wst_ref[h], jnp.zeros((), BF))
            for c in range(nc):
                rs, cs = slice(c * CHUNK, (c + 1) * CHUNK), slice(h * CHUNK, (h + 1) * CHUNK)
                vnb = vn[rs, cs]
                s = _mm(wm, vnb) + bsb_ref[h]
                gated_ref[rs, cs] = (u[rs, cs] * s).astype(BF)
                dgt = dgated[rs, cs]
                du_scr[rs, cs] = dgt * s
                ds = dgt * u[rs, cs]
                dsb = ds.astype(BF)
                dws_ref[h] += jnp.where(tril, _mm_nt(dsb, vnb), 0.0)
                dbs_ref[h] += ds
                dvn_scr[rs, cs] = _mm(wmt, dsb)
        dvn = dvn_scr[...]
        dv, dgv = _rms_bwd(dvn, vhat, rv, gvv)
        dgv_ref[...] += dgv
        dz_ref[:, :AW] = (du_scr[...] * gp[:, :AW]).astype(BF)
        dz_ref[:, AW:] = (dv * gp[:, AW:]).astype(BF)
        dxn = _mm_nt(dz_ref[...], win_ref[...])
        dxr, dg = _rms_bwd(dxn, xhat, r, gmix)
        dg_ref[...] += dg
        dx_ref[...] = dh + dxr

        @pl.when(i == nt - 1)
        def _():
            for h in range(GROUPS):
                dbs_ref[h] = jnp.broadcast_to(jnp.sum(dbs_ref[h], axis=-1, keepdims=True), (CHUNK, CHUNK))

    gshape = (GROUPS, CHUNK, CHUNK)
    return pl.pallas_call(
        body, name="mixa_bwd", grid=(nt,),
        in_specs=[_row_spec(tm, D), _row_spec(tm, D), _row_spec(tm, 2 * AW), _row_spec(tm, 2 * AW), _const_spec((1, D)),
                  _const_spec((D, 2 * AW)), _const_spec((1, AW)), _const_spec(gshape), _const_spec(gshape),
                  _const_spec(gshape), _const_spec((AW, D))],
        out_specs=[_row_spec(tm, D), pl.BlockSpec((D, tm), lambda i: (0, i)), _row_spec(tm, 2 * AW), _row_spec(tm, AW),
                   _row_spec(tm, D), _const_spec(gshape), _const_spec(gshape), _const_spec((1, AW)), _const_spec((1, D))],
        out_shape=[jax.ShapeDtypeStruct((S, D), F32), jax.ShapeDtypeStruct((D, S), BF),
                   jax.ShapeDtypeStruct((S, 2 * AW), BF), jax.ShapeDtypeStruct((S, AW), BF),
                   jax.ShapeDtypeStruct((S, D), BF), jax.ShapeDtypeStruct(gshape, F32),
                   jax.ShapeDtypeStruct(gshape, F32), jax.ShapeDtypeStruct((1, AW), F32),
                   jax.ShapeDtypeStruct((1, D), F32)],
        scratch_shapes=[pltpu.VMEM((tm, AW), F32), pltpu.VMEM((tm, AW), F32)],
        compiler_params=_params(VMEM_BIG),
    )(dh, x, zg, gp, g, win, gv, ws, wst, bsb, wout)


def _shift_down(x, prev8, k):
    n, c = x.shape[0] // 8, x.shape[1]
    r = pltpu.roll(x.reshape(n, 8, c), k, axis=1)
    before = jnp.concatenate([pltpu.roll(prev8.reshape(1, 8, c), k, axis=1), r[:-1]], axis=0)
    sub = lax.broadcasted_iota(jnp.int32, (1, 8, c), 1)
    return jnp.where(sub < k, before, r).reshape(x.shape)


def _shift_up(x, next8, k):
    n, c = x.shape[0] // 8, x.shape[1]
    r = pltpu.roll(x.reshape(n, 8, c), 8 - k, axis=1)
    after = jnp.concatenate([r[1:], pltpu.roll(next8.reshape(1, 8, c), 8 - k, axis=1)], axis=0)
    sub = lax.broadcasted_iota(jnp.int32, (1, 8, c), 1)
    return jnp.where(sub >= 8 - k, after, r).reshape(x.shape)


def _ffn_fwd(h, g, wup, cw, cb, wdown, tm):
    S = h.shape[0]

    def body(h_ref, g_ref, wup_ref, cw_ref, cb_ref, wdown_ref, o_ref, hh_ref, cc_ref, sh):
        i = pl.program_id(0)

        @pl.when(i == 0)
        def _():
            sh[...] = jnp.zeros_like(sh)

        x = h_ref[...]
        xn, _, _ = _rms_fwd(x, g_ref[...])
        xn = xn.astype(BF)
        acc = x
        up = lambda d: [_mm(xn, wup_ref[d]), _mm(xn, wup_ref[N_FF_BLK + d])]
        ahead = up(0)
        for d in range(N_FF_BLK):
            cs = []
            hbs = ahead
            if d + 1 < N_FF_BLK:
                ahead = up(d + 1)
            for blk, hb in zip((d, N_FF_BLK + d), hbs):
                hh_ref[blk] = hb.astype(BF)
                w, prev8 = cw_ref[blk], sh[blk]
                c = cb_ref[blk] + w[0:1] * _shift_down(hb, prev8, 2) + w[1:2] * _shift_down(hb, prev8, 1) + w[2:3] * hb
                sh[blk] = hb[tm - 8:, :]
                cc_ref[blk] = c.astype(BF)
                cs.append(c)
            act = cs[0] * _sigmoid(cs[0]) * cs[1]
            acc = acc + _mm(act.astype(BF), wdown_ref[d * FF_BLK:(d + 1) * FF_BLK, :])
        o_ref[...] = acc

    blk3 = pl.BlockSpec((N_DEV, tm, FF_BLK), lambda i: (0, i, 0))
    return pl.pallas_call(
        body, name="ffn_fwd", grid=(S // tm,),
        in_specs=[_row_spec(tm, D), _const_spec((1, D)), _const_spec((N_DEV, D, FF_BLK), True),
                  _const_spec((N_DEV, 3, FF_BLK)), _const_spec((N_DEV, 1, FF_BLK)), _const_spec((D_FF, D), True)],
        out_specs=[_row_spec(tm, D), blk3, blk3],
        out_shape=[jax.ShapeDtypeStruct((S, D), F32), jax.ShapeDtypeStruct((N_DEV, S, FF_BLK), BF),
                   jax.ShapeDtypeStruct((N_DEV, S, FF_BLK), BF)],
        scratch_shapes=[pltpu.VMEM((N_DEV, 8, FF_BLK), F32)],
        compiler_params=_params(VMEM_BIG),
    )(h, g, wup, cw, cb, wdown)


def _ffn_bwd(dh, h, hh, cc, g, wup, cw, wdown, tm):
    S = h.shape[0]
    nt = S // tm

    def rev(i):
        return (nt - 1 - i, 0)

    def body(dh_ref, h_ref, hh_ref, cc_ref, g_ref, wup_ref, cw_ref, wdown_ref,
             dx_ref, xn_ref, dhh_ref, act_ref, dhb_ref, dcw_ref, dcb_ref, dg_ref, sh):
        i = pl.program_id(0)

        @pl.when(i == 0)
        def _():
            sh[...] = jnp.zeros_like(sh)
            dcw_ref[...] = jnp.zeros_like(dcw_ref)
            dcb_ref[...] = jnp.zeros_like(dcb_ref)
            dg_ref[...] = jnp.zeros_like(dg_ref)

        x = h_ref[...]
        gain = g_ref[...]
        xn, xhat, r = _rms_fwd(x, gain)
        xn_ref[...] = xn.T.astype(BF)
        dh = dh_ref[...]
        dhb = dh.astype(BF)
        dhb_ref[...] = dhb
        dxn = jnp.zeros((tm, D), F32)
        dact_of = lambda d: _mm_nt(dhb, wdown_ref[d * FF_BLK:(d + 1) * FF_BLK, :])
        ahead = dact_of(0)
        for d in range(N_FF_BLK):
            blks = (d, N_FF_BLK + d)
            dact = ahead
            if d + 1 < N_FF_BLK:
                ahead = dact_of(d + 1)
            cg, cu = cc_ref[blks[0]].astype(F32), cc_ref[blks[1]].astype(F32)
            sg = _sigmoid(cg)
            sil = cg * sg
            act_ref[d] = (sil * cu).astype(BF)
            dcs = (dact * cu * (sg + sil - sil * sg), dact * sil)
            for blk, dc in zip(blks, dcs):
                next8 = sh[blk]
                dp1, dp2 = _shift_up(dc, next8, 1), _shift_up(dc, next8, 2)
                hb = hh_ref[blk].astype(F32)
                dcb_ref[blk] += jnp.sum(dc, axis=0, keepdims=True)
                dcw_ref[blk, 0:1, :] += jnp.sum(dp2 * hb, axis=0, keepdims=True)
                dcw_ref[blk, 1:2, :] += jnp.sum(dp1 * hb, axis=0, keepdims=True)
                dcw_ref[blk, 2:3, :] += jnp.sum(dc * hb, axis=0, keepdims=True)
                w = cw_ref[blk]
                dhh = (w[2:3] * dc + w[1:2] * dp1 + w[0:1] * dp2).astype(BF)
                sh[blk] = dc[0:8, :]
                dhh_ref[blk] = dhh
                dxn = dxn + _mm_nt(dhh, wup_ref[blk])
        dxr, dg = _rms_bwd(dxn, xhat, r, gain)
        dg_ref[...] += dg
        dx_ref[...] = dh + dxr

    blk3 = lambda i: (0, nt - 1 - i, 0)
    return pl.pallas_call(
        body, name="ffn_bwd", grid=(nt,),
        in_specs=[pl.BlockSpec((tm, D), rev), pl.BlockSpec((tm, D), rev),
                  pl.BlockSpec((N_DEV, tm, FF_BLK), blk3), pl.BlockSpec((N_DEV, tm, FF_BLK), blk3),
                  _const_spec((1, D)), _const_spec((N_DEV, D, FF_BLK), True), _const_spec((N_DEV, 3, FF_BLK)),
                  _const_spec((D_FF, D), True)],
        out_specs=[pl.BlockSpec((tm, D), rev), pl.BlockSpec((D, tm), lambda i: (0, nt - 1 - i)),
                   pl.BlockSpec((N_DEV, tm, FF_BLK), blk3), pl.BlockSpec((N_FF_BLK, tm, FF_BLK), blk3),
                   pl.BlockSpec((tm, D), rev), _const_spec((N_DEV, 3, FF_BLK)), _const_spec((N_DEV, 1, FF_BLK)),
                   _const_spec((1, D))],
        out_shape=[jax.ShapeDtypeStruct((S, D), F32), jax.ShapeDtypeStruct((D, S), BF),
                   jax.ShapeDtypeStruct((N_DEV, S, FF_BLK), BF), jax.ShapeDtypeStruct((N_FF_BLK, S, FF_BLK), BF),
                   jax.ShapeDtypeStruct((S, D), BF), jax.ShapeDtypeStruct((N_DEV, 3, FF_BLK), F32),
                   jax.ShapeDtypeStruct((N_DEV, 1, FF_BLK), F32), jax.ShapeDtypeStruct((1, D), F32)],
        scratch_shapes=[pltpu.VMEM((N_DEV, 8, FF_BLK), F32)],
        compiler_params=_params(VMEM_BIG),
    )(dh, h, hh, cc, g, wup, cw, wdown)


def _ple_fwd(h, g, wgate, bgate, p, wple, tm):
    S = h.shape[0]

    def body(h_ref, g_ref, wg_ref, bg_ref, p_ref, wp_ref, o_ref):
        x = h_ref[...]
        xn, _, _ = _rms_fwd(x, g_ref[...])
        gate = _sigmoid(_mm(xn.astype(BF), wg_ref[...]) + bg_ref[...])
        o_ref[...] = x + _mm(p_ref[...], wp_ref[...]) * gate

    return pl.pallas_call(
        body, name="ple_fwd", grid=(S // tm,),
        in_specs=[_row_spec(tm, D), _const_spec((1, D)), _const_spec((D, D)), _const_spec((1, D)),
                  _row_spec(tm, PLE), _const_spec((PLE, D))],
        out_specs=_row_spec(tm, D), out_shape=jax.ShapeDtypeStruct((S, D), F32),
        compiler_params=_params(),
    )(h, g, wgate, bgate, p, wple)


def _ple_bwd(dh, h, g, wgate, bgate, p, wple, tm):
    S = h.shape[0]

    def body(dh_ref, h_ref, g_ref, wg_ref, bg_ref, p_ref, wp_ref, dx_ref, de_ref, xn_ref, dpre_ref, dbg_ref, dg_ref):
        i = pl.program_id(0)

        @pl.when(i == 0)
        def _():
            dbg_ref[...] = jnp.zeros_like(dbg_ref)
            dg_ref[...] = jnp.zeros_like(dg_ref)

        x = h_ref[...]
        gain = g_ref[...]
        xn, xhat, r = _rms_fwd(x, gain)
        xnb = xn.astype(BF)
        xn_ref[...] = xnb
        gate = _sigmoid(_mm(xnb, wg_ref[...]) + bg_ref[...])
        e = _mm(p_ref[...], wp_ref[...])
        dh = dh_ref[...]
        de_ref[...] = (dh * gate).astype(BF)
        dpre = dh * e * gate * (1.0 - gate)
        dpreb = dpre.astype(BF)
        dpre_ref[...] = dpreb
        dbg_ref[...] += jnp.sum(dpre, axis=0, keepdims=True)
        dxr, dg = _rms_bwd(_mm_nt(dpreb, wg_ref[...]), xhat, r, gain)
        dg_ref[...] += dg
        dx_ref[...] = dh + dxr

    return pl.pallas_call(
        body, name="ple_bwd", grid=(S // tm,),
        in_specs=[_row_spec(tm, D), _row_spec(tm, D), _const_spec((1, D)), _const_spec((D, D)), _const_spec((1, D)),
                  _row_spec(tm, PLE), _const_spec((PLE, D))],
        out_specs=[_row_spec(tm, D), _row_spec(tm, D), _row_spec(tm, D), _row_spec(tm, D), _const_spec((1, D)),
                   _const_spec((1, D))],
        out_shape=[jax.ShapeDtypeStruct((S, D), F32), jax.ShapeDtypeStruct((S, D), BF),
                   jax.ShapeDtypeStruct((S, D), BF), jax.ShapeDtypeStruct((S, D), BF),
                   jax.ShapeDtypeStruct((1, D), F32), jax.ShapeDtypeStruct((1, D), F32)],
        compiler_params=_params(),
    )(dh, h, g, wgate, bgate, p, wple)


def _qkv_fwd(h, gq, wq, gkv, wkv, tm):
    S = h.shape[0]
    nkv = wkv.shape[1]

    def body(h_ref, gq_ref, wq_ref, gkv_ref, wkv_ref, q_ref, kv_ref):
        x = h_ref[...]
        _, xhat, _ = _rms_fwd(x, gq_ref[...])
        q_ref[...] = _mm((xhat * gq_ref[...]).astype(BF), wq_ref[...]).astype(BF)
        kv_ref[...] = _mm((xhat * gkv_ref[...]).astype(BF), wkv_ref[...]).astype(BF)

    return pl.pallas_call(
        body, name="qkv_fwd", grid=(S // tm,),
        in_specs=[_row_spec(tm, D), _const_spec((1, D)), _const_spec((D, D)), _const_spec((1, D)),
                  _const_spec((D, nkv))],
        out_specs=[_row_spec(tm, D), _row_spec(tm, nkv)],
        out_shape=[jax.ShapeDtypeStruct((S, D), BF), jax.ShapeDtypeStruct((S, nkv), BF)],
        compiler_params=_params(),
    )(h, gq, wq, gkv, wkv)


def _qkv_bwd(dh, h, dq, dkv, gq, wq, gkv, wkv, tm):
    S = h.shape[0]
    nkv = wkv.shape[1]

    def body(dh_ref, h_ref, dq_ref, dkv_ref, gq_ref, wq_ref, gkv_ref, wkv_ref,
             dx_ref, xq_ref, xkv_ref, dgq_ref, dgkv_ref):
        i = pl.program_id(0)

        @pl.when(i == 0)
        def _():
            dgq_ref[...] = jnp.zeros_like(dgq_ref)
            dgkv_ref[...] = jnp.zeros_like(dgkv_ref)

        x = h_ref[...]
        gq_, gkv_ = gq_ref[...], gkv_ref[...]
        _, xhat, r = _rms_fwd(x, gq_)
        xq_ref[...] = (xhat * gq_).astype(BF)
        xkv_ref[...] = (xhat * gkv_).astype(BF)
        d1, dg1 = _rms_bwd(_mm_nt(dq_ref[...], wq_ref[...]), xhat, r, gq_)
        d2, dg2 = _rms_bwd(_mm_nt(dkv_ref[...], wkv_ref[...]), xhat, r, gkv_)
        dgq_ref[...] += dg1
        dgkv_ref[...] += dg2
        dx_ref[...] = dh_ref[...] + d1 + d2

    return pl.pallas_call(
        body, name="qkv_bwd", grid=(S // tm,),
        in_specs=[_row_spec(tm, D), _row_spec(tm, D), _row_spec(tm, D), _row_spec(tm, nkv), _const_spec((1, D)),
                  _const_spec((D, D)), _const_spec((1, D)), _const_spec((D, nkv))],
        out_specs=[_row_spec(tm, D), _row_spec(tm, D), _row_spec(tm, D), _const_spec((1, D)), _const_spec((1, D))],
        out_shape=[jax.ShapeDtypeStruct((S, D), F32), jax.ShapeDtypeStruct((S, D), BF),
                   jax.ShapeDtypeStruct((S, D), BF), jax.ShapeDtypeStruct((1, D), F32),
                   jax.ShapeDtypeStruct((1, D), F32)],
        compiler_params=_params(),
    )(dh, h, dq, dkv, gq, wq, gkv, wkv)


def _oproj_fwd(a, w, res, tm):
    S = a.shape[0]

    def body(a_ref, w_ref, r_ref, o_ref):
        o_ref[...] = r_ref[...] + _mm(a_ref[...], w_ref[...])

    return pl.pallas_call(
        body, name="oproj_fwd", grid=(S // tm,),
        in_specs=[_row_spec(tm, D), _const_spec((D, D)), _row_spec(tm, D)],
        out_specs=_row_spec(tm, D), out_shape=jax.ShapeDtypeStruct((S, D), F32),
        compiler_params=_params(),
    )(a, w, res)


def _oproj_bwd(dh, w, tm):
    S = dh.shape[0]

    def body(dh_ref, w_ref, da_ref, dhb_ref):
        dhb = dh_ref[...].astype(BF)
        dhb_ref[...] = dhb
        da_ref[...] = _mm_nt(dhb, w_ref[...]).astype(BF)

    return pl.pallas_call(
        body, name="oproj_bwd", grid=(S // tm,),
        in_specs=[_row_spec(tm, D), _const_spec((D, D))],
        out_specs=[_row_spec(tm, D), _row_spec(tm, D)],
        out_shape=[jax.ShapeDtypeStruct((S, D), BF), jax.ShapeDtypeStruct((S, D), BF)],
        compiler_params=_params(),
    )(dh, w)


def _alibi_slope(hq):
    return float(np.float32(2.0 ** (-8.0 * (hq + 1) / N_Q)))


def _attn_consts():
    W = N_KV * HD
    r = np.zeros((W, N_KV * W), np.float32)
    for kh in range(N_KV):
        for c in range(W):
            r[kh * HD + c % HD, kh * W + c] = 1.0
    qi = np.arange(BLOCK)[:, None]
    kj = np.arange(2 * BLOCK)[None, :]
    dist = qi + BLOCK - kj
    band = (dist >= 0) & (dist < BLOCK)
    bias = np.full((2, N_KV, GQA * BLOCK, 2 * BLOCK), NEG, np.float32)
    for later in (0, 1):
        valid = band & ((kj >= BLOCK) | bool(later))
        for kh in range(N_KV):
            for g in range(GQA):
                slope = np.float32(_alibi_slope(kh * GQA + g))
                bias[later, kh, g * BLOCK:(g + 1) * BLOCK] = np.where(valid, -slope * dist.astype(np.float32), NEG)
    lane_head = np.arange(W)[None, :] // HD
    maskq = (np.arange(GQA * BLOCK)[:, None] // BLOCK == lane_head).astype(np.float32)
    maskk = (np.arange(GQA * 2 * BLOCK)[:, None] // (2 * BLOCK) == lane_head).astype(np.float32)
    head_sum = (lane_head.T == np.arange(GQA * BLOCK)[None, :] // BLOCK).astype(np.float32)
    return dict(rep=jnp.asarray(r, BF), bias=jnp.asarray(bias), maskq=jnp.asarray(maskq, BF),
                maskk=jnp.asarray(maskk, BF), head_sum=jnp.asarray(head_sum, BF))


def _sink_col(sink_ref, kh):
    rb = lax.broadcasted_iota(jnp.int32, (GQA * BLOCK, 1), 0) >> 7
    col = jnp.full((GQA * BLOCK, 1), sink_ref[kh * GQA + GQA - 1], F32)
    for g in range(GQA - 1):
        col = jnp.where(rb == g, sink_ref[kh * GQA + g], col)
    return col


def _stack4(a):
    return jnp.concatenate([a] * GQA, axis=0)


def _rows_to_lanes(a):
    return jnp.concatenate([a[g * BLOCK:(g + 1) * BLOCK] for g in range(GQA)], axis=1)


def _lanes_to_rows(a, n):
    return jnp.concatenate([a[:, g * n:(g + 1) * n] for g in range(GQA)], axis=0)


def _scores(qg, k4, bias, sink_col, maskq):
    qs = _stack4(qg) * maskq
    s = _mm_nt(qs, k4) * (HD ** -0.5) + bias
    m = jnp.maximum(jnp.max(s, axis=-1, keepdims=True), sink_col)
    return qs, jnp.exp(s - m), jnp.exp(sink_col - m)


def _bands(kvc_ref, kvp_ref, rep_ref):
    W = N_KV * HD
    kvp, kvc = kvp_ref[...], kvc_ref[...]
    kband = jnp.concatenate([kvp[:, :W], kvc[:, :W]], axis=0)
    vband = jnp.concatenate([kvp[:, W:], kvc[:, W:]], axis=0)
    return _mm(kband, rep_ref[...]).astype(BF), _mm(vband, rep_ref[...]).astype(BF)


def _attn_fwd(q, kv, sinks, consts):
    S = q.shape[0]
    nb = S // BLOCK
    W = N_KV * HD

    def body(sink_ref, q_ref, kvc_ref, kvp_ref, rep_ref, bias_ref, maskq_ref, maskk_ref, o_ref):
        maskq, maskk = maskq_ref[...], maskk_ref[...]
        lane_head = lax.broadcasted_iota(jnp.int32, (BLOCK, W), 1) >> 6
        k4all, v4all = _bands(kvc_ref, kvp_ref, rep_ref)
        for kh in range(N_KV):
            cols = slice(kh * W, (kh + 1) * W)
            _, e, es = _scores(q_ref[:, cols], k4all[:, cols], bias_ref[kh], _sink_col(sink_ref, kh), maskq)
            vbd = jnp.concatenate([_stack4(v4all[:, cols]) * maskk, maskk], axis=1)
            nd = _mm(_rows_to_lanes(e.astype(BF)), vbd)
            es_l = jnp.broadcast_to(es[(GQA - 1) * BLOCK:], (BLOCK, W))
            for g in range(GQA - 1):
                es_l = jnp.where(lane_head == g, es[g * BLOCK:(g + 1) * BLOCK], es_l)
            o_ref[:, cols] = (nd[:, :W] / (nd[:, W:] + es_l)).astype(BF)

    return pl.pallas_call(
        body, name="attn_fwd", grid=(nb,),
        in_specs=[pl.BlockSpec(memory_space=pltpu.SMEM), _row_spec(BLOCK, D), _row_spec(BLOCK, 2 * W),
                  pl.BlockSpec((BLOCK, 2 * W), lambda i: (jnp.maximum(i - 1, 0), 0)), _const_spec((W, N_KV * W)),
                  pl.BlockSpec((None, N_KV, GQA * BLOCK, 2 * BLOCK), lambda i: (jnp.minimum(i, 1), 0, 0, 0)),
                  _const_spec((GQA * BLOCK, W)), _const_spec((GQA * 2 * BLOCK, W))],
        out_specs=_row_spec(BLOCK, D), out_shape=jax.ShapeDtypeStruct((S, D), BF),
        compiler_params=_params(),
    )(sinks, q, kv, kv, consts["rep"], consts["bias"], consts["maskq"], consts["maskk"])


def _attn_bwd(q, kv, att, do, sinks, consts):
    S = q.shape[0]
    nb = S // BLOCK
    W = N_KV * HD

    def qmap(i):
        return (jnp.minimum(i, nb - 1), 0)

    def body(sink_ref, q_ref, kvc_ref, kvp_ref, att_ref, do_ref, rep_ref, bias_ref, maskq_ref, maskk_ref, hsum_ref,
             dq_ref, dkv_ref, dsink_ref, carry, sink_acc):
        i = pl.program_id(0)

        @pl.when(i == 0)
        def _():
            sink_acc[...] = jnp.zeros_like(sink_acc)
            carry[...] = jnp.zeros_like(carry)

        @pl.when(i < nb)
        def _():
            maskq, maskk, head_sum = maskq_ref[...], maskk_ref[...], hsum_ref[...]
            ones_k = jnp.ones((2 * BLOCK, BLOCK), BF)
            k4all, v4all = _bands(kvc_ref, kvp_ref, rep_ref)
            dk4s, dv4s = [], []
            for kh in range(N_KV):
                cols = slice(kh * W, (kh + 1) * W)
                k4, v4 = k4all[:, cols], v4all[:, cols]
                qs, e, es = _scores(q_ref[:, cols], k4, bias_ref[kh], _sink_col(sink_ref, kh), maskq)
                inv = 1.0 / (_mm(e.astype(BF), ones_k) + es)
                p = e * jnp.concatenate([inv, inv], axis=1)
                dog = do_ref[:, cols]
                dos = _stack4(dog) * maskq
                dp = _mm_nt(dos, v4)
                prod = (dog.astype(F32) * att_ref[:, cols].astype(F32)).astype(BF)
                delta = _lanes_to_rows(_mm(prod, head_sum), BLOCK)
                ds = (p * (dp - jnp.concatenate([delta, delta], axis=1))).astype(BF)
                sink_acc[kh] += es * inv * delta
                kbd = _stack4(k4) * maskk
                dq_ref[:, cols] = (_mm(_rows_to_lanes(ds), kbd) * (HD ** -0.5)).astype(BF)
                dk4s.append((_mm_tn(ds, qs) * (HD ** -0.5)).astype(BF))
                dv4s.append(_mm_tn(p.astype(BF), dos).astype(BF))
            dkband = _mm_nt(jnp.concatenate(dk4s, axis=1), rep_ref[...])
            dvband = _mm_nt(jnp.concatenate(dv4s, axis=1), rep_ref[...])
            dkv_ref[:, :W] = (carry[:, :W] + dkband[:BLOCK]).astype(BF)
            dkv_ref[:, W:] = (carry[:, W:] + dvband[:BLOCK]).astype(BF)
            carry[:, :W] = dkband[BLOCK:]
            carry[:, W:] = dvband[BLOCK:]

        @pl.when(i == nb)
        def _():
            dkv_ref[...] = carry[...].astype(BF)
            for hq in range(N_Q):
                kh, g = divmod(hq, GQA)
                dsink_ref[hq:hq + 1, :] = -jnp.sum(sink_acc[kh, g * BLOCK:(g + 1) * BLOCK, :], axis=0, keepdims=True)

    return pl.pallas_call(
        body, name="attn_bwd", grid=(nb + 1,),
        in_specs=[pl.BlockSpec(memory_space=pltpu.SMEM), pl.BlockSpec((BLOCK, D), qmap),
                  pl.BlockSpec((BLOCK, 2 * W), qmap),
                  pl.BlockSpec((BLOCK, 2 * W), lambda i: (jnp.maximum(jnp.minimum(i, nb - 1) - 1, 0), 0)),
                  pl.BlockSpec((BLOCK, D), qmap), pl.BlockSpec((BLOCK, D), qmap), _const_spec((W, N_KV * W)),
                  pl.BlockSpec((None, N_KV, GQA * BLOCK, 2 * BLOCK), lambda i: (jnp.minimum(i, 1), 0, 0, 0)),
                  _const_spec((GQA * BLOCK, W)), _const_spec((GQA * 2 * BLOCK, W)), _const_spec((W, GQA * BLOCK))],
        out_specs=[pl.BlockSpec((BLOCK, D), qmap),
                   pl.BlockSpec((BLOCK, 2 * W), lambda i: (jnp.maximum(i - 1, 0), 0)),
                   _const_spec((N_Q, BLOCK))],
        out_shape=[jax.ShapeDtypeStruct((S, D), BF), jax.ShapeDtypeStruct((S, 2 * W), BF),
                   jax.ShapeDtypeStruct((N_Q, BLOCK), F32)],
        scratch_shapes=[pltpu.VMEM((BLOCK, 2 * W), F32), pltpu.VMEM((N_KV, GQA * BLOCK, BLOCK), F32)],
        compiler_params=_params(),
    )(sinks, q, kv, kv, att, do, consts["rep"], consts["bias"], consts["maskq"], consts["maskk"], consts["head_sum"])


def _head(h, g, target, tm):
    S = h.shape[0]

    def body(h_ref, g_ref, t_ref, dh_ref, loss_ref, dg_ref):
        i = pl.program_id(0)

        @pl.when(i == 0)
        def _():
            loss_ref[...] = jnp.zeros_like(loss_ref)
            dg_ref[...] = jnp.zeros_like(dg_ref)

        gain = g_ref[...]
        y, xhat, r = _rms_fwd(h_ref[...], gain)
        err = y - t_ref[...]
        tile = 0.5 * jnp.sum(jnp.sum(err * err, axis=-1, keepdims=True) * (1.0 / D), axis=0, keepdims=True)
        loss_ref[...] += jnp.broadcast_to(tile, loss_ref.shape)
        dx, dg = _rms_bwd(err * (1.0 / D), xhat, r, gain)
        dg_ref[...] += dg
        dh_ref[...] = dx

    return pl.pallas_call(
        body, name="loss_head", grid=(S // tm,),
        in_specs=[_row_spec(tm, D), _const_spec((1, D)), _row_spec(tm, D)],
        out_specs=[_row_spec(tm, D), _const_spec((8, 128)), _const_spec((1, D))],
        out_shape=[jax.ShapeDtypeStruct((S, D), F32), jax.ShapeDtypeStruct((8, 128), F32),
                   jax.ShapeDtypeStruct((1, D), F32)],
        compiler_params=_params(),
    )(h, g, target)


def _wgrad(a, b, name, *, tn, out_blocked=False, a_transposed=False, ts=2048, per_step=1):
    a_blocked, b_blocked = a.ndim == 3, b.ndim == 3
    na = a.shape[0] if a_blocked else 1
    S = b.shape[-2]
    K = a.shape[0] if a_transposed else a.shape[-1]
    wide = tn * per_step
    nbk = b.shape[0] if b_blocked else b.shape[1] // wide
    ts = min(ts, S)
    ns = S // ts
    assert not (a_blocked and (out_blocked or a_transposed)) and (per_step == 1 or (out_blocked and not b_blocked))

    def body(a_ref, b_ref, o_ref, acc):
        s = pl.program_id(2)

        @pl.when(s == 0)
        def _():
            acc[...] = jnp.zeros_like(acc)

        if a_transposed:
            acc[...] += _mm(a_ref[...], b_ref[...])
        else:
            acc[...] += _mm_tn(a_ref[...], b_ref[...])

        @pl.when(s == ns - 1)
        def _():
            if per_step == 1:
                o_ref[...] = acc[...].astype(o_ref.dtype)
            else:
                for k in range(per_step):
                    o_ref[k] = acc[:, k * tn:(k + 1) * tn].astype(o_ref.dtype)

    if a_transposed:
        a_spec = pl.BlockSpec((K, ts), lambda i, j, s: (0, s))
    elif a_blocked:
        a_spec = pl.BlockSpec((None, ts, K), lambda i, j, s: (i, s, 0))
    else:
        a_spec = pl.BlockSpec((ts, K), lambda i, j, s: (s, 0))
    b_spec = (pl.BlockSpec((None, ts, tn), lambda i, j, s: (j, s, 0)) if b_blocked
              else pl.BlockSpec((ts, wide), lambda i, j, s: (s, j)))
    if out_blocked:
        o_spec = pl.BlockSpec((None, K, tn) if per_step == 1 else (per_step, K, tn), lambda i, j, s: (j, 0, 0))
        o_shape = jax.ShapeDtypeStruct((nbk * per_step, K, tn), BF)
    else:
        o_spec = pl.BlockSpec((K, wide), lambda i, j, s: (i, j))
        o_shape = jax.ShapeDtypeStruct((na * K, nbk * wide), BF)
    return pl.pallas_call(
        body, name=name, grid=(na, nbk, ns), in_specs=[a_spec, b_spec], out_specs=o_spec, out_shape=o_shape,
        scratch_shapes=[pltpu.VMEM((K, wide), F32)],
        compiler_params=pltpu.CompilerParams(dimension_semantics=("arbitrary", "arbitrary", "arbitrary"),
                                             vmem_limit_bytes=VMEM_BIG),
    )(a, b)


def _adamw_math(w, g, m, v):
    m = B1 * m + (1.0 - B1) * g
    v = B2 * v + (1.0 - B2) * (g * g)
    m_hat = m / (1.0 - B1 ** STEP)
    v_hat = v / (1.0 - B2 ** STEP)
    return -LR * (m_hat / (jnp.sqrt(v_hat) + ADAM_EPS) + WD * w), m, v


def _sum8(parts, name):
    R = parts.shape[1]

    def body(p_ref, o_ref):
        g = p_ref[0]
        for s in range(1, N_DEV):
            g = g + p_ref[s]
        o_ref[...] = g

    return pl.pallas_call(
        body, name=name, grid=(1,),
        in_specs=[pl.BlockSpec((N_DEV, R, ROW), lambda i: (0, 0, 0))],
        out_specs=pl.BlockSpec((R, ROW), lambda i: (0, 0)), out_shape=jax.ShapeDtypeStruct((R, ROW), F32),
        compiler_params=_params(),
    )(parts)


def _adamw_small(g, w, m, v, name):
    R = w.shape[0]

    def body(g_ref, w_ref, m_ref, v_ref, d_ref, nm_ref, nv_ref):
        d_ref[...], nm_ref[...], nv_ref[...] = _adamw_math(w_ref[...], g_ref[...], m_ref[...], v_ref[...])

    spec = pl.BlockSpec((R, ROW), lambda i: (0, 0))
    return pl.pallas_call(
        body, name=name, grid=(1,), in_specs=[spec] * 4, out_specs=[spec] * 3,
        out_shape=[jax.ShapeDtypeStruct((R, ROW), F32)] * 3, compiler_params=_params(),
    )(g, w, m, v)


def _mesh_pos():
    return lax.axis_index("x"), lax.axis_index("y"), lax.axis_index("c")


def _flip(v, bit):
    return 1 - v if bit else v


def _peers(x, y, c):
    out = []
    for k in range(1, N_DEV):
        px, py, pc = _flip(x, k & 4), _flip(y, k & 2), _flip(c, k & 1)
        out.append((k - 1, (px, py, pc), 4 * px + 2 * py + pc))
    return out


def _all_gather_now(xs, name):
    n = len(xs)
    MESH = pl.DeviceIdType.MESH

    def body(*refs):
        x_refs, out_refs = refs[:n], refs[n:2 * n]
        send_sems, recv_sems, local_sems = refs[2 * n:]
        x, y, c = _mesh_pos()
        me, sibling = (x, y, c), (x, y, 1 - c)
        chips = [(1 - x, y), (x, 1 - y), (1 - x, 1 - y)]
        blk = lambda p: 4 * p[0] + 2 * p[1] + p[2]

        def copy(a, k, block, to, src=None):
            return pltpu.make_async_remote_copy(
                src_ref=out_refs[a].at[blk(block)] if src is None else src, dst_ref=out_refs[a].at[blk(block)],
                send_sem=send_sems.at[a, k], recv_sem=recv_sems.at[a, k], device_id=to, device_id_type=MESH)

        sent, own = [], []
        for a in range(n):
            own.append(pltpu.make_async_copy(x_refs[a], out_refs[a].at[blk(me)], local_sems.at[a]))
            own[-1].start()
            first = [copy(a, 1 + j, me, (*chip, c), src=x_refs[a]) for j, chip in enumerate(chips)]
            first.append(copy(a, 0, me, sibling, src=x_refs[a]))
            for cp in first:
                cp.start()
            sent += first
        for a in range(n):
            for j, chip in enumerate(chips):
                copy(a, 1 + j, (*chip, c), me).wait_recv()
                sent.append(copy(a, 4 + j, (*chip, c), sibling))
                sent[-1].start()
        for a in range(n):
            copy(a, 0, sibling, me).wait_recv()
            for j, chip in enumerate(chips):
                copy(a, 4 + j, (*chip, 1 - c), me).wait_recv()
        for cp in sent:
            cp.wait_send()
        for cp in own:
            cp.wait()

    hbm = pl.BlockSpec(memory_space=pl.ANY)
    return pl.pallas_call(
        body, name=name, out_shape=[jax.ShapeDtypeStruct((N_DEV,) + x.shape, x.dtype) for x in xs],
        in_specs=[hbm] * n, out_specs=[hbm] * n,
        scratch_shapes=[pltpu.SemaphoreType.DMA((n, 7)), pltpu.SemaphoreType.DMA((n, 7)), pltpu.SemaphoreType.DMA((n,))],
    )(*xs)


def _send_start(srcs, blocked, name):
    n = len(srcs)
    MESH = pl.DeviceIdType.MESH

    def body(*refs):
        src_refs, land_refs = refs[:n], refs[n:2 * n]
        send_sems, recv_sems, local_sems = refs[2 * n:3 * n], refs[3 * n:4 * n], refs[4 * n:5 * n]
        token = refs[7 * n]
        x, y, c = _mesh_pos()
        me = 4 * x + 2 * y + c
        for a in range(n):
            for k, peer, blk in _peers(x, y, c):
                pltpu.make_async_remote_copy(
                    src_ref=src_refs[a].at[blk] if blocked else src_refs[a], dst_ref=land_refs[a].at[me],
                    send_sem=send_sems[a].at[k], recv_sem=recv_sems[a].at[k], device_id=peer,
                    device_id_type=MESH).start()
            pltpu.make_async_copy(src_refs[a].at[me] if blocked else src_refs[a], land_refs[a].at[me],
                                  local_sems[a]).start()
        token[...] = jnp.zeros_like(token)

    hbm = pl.BlockSpec(memory_space=pltpu.HBM)
    sem = pl.BlockSpec(memory_space=pltpu.SEMAPHORE)
    lshape = [s.shape if blocked else (N_DEV,) + s.shape for s in srcs]
    outs = pl.pallas_call(
        body, name=name,
        out_shape=([pltpu.SemaphoreType.DMA((N_DEV - 1,))] * (2 * n) + [pltpu.SemaphoreType.DMA(())] * n
                   + [pltpu.HBM(s.shape, s.dtype) for s in srcs]
                   + [pltpu.HBM(ls, s.dtype) for ls, s in zip(lshape, srcs)] + [jax.ShapeDtypeStruct((8, 128), F32)]),
        in_specs=[hbm] * (2 * n), out_specs=[sem] * (3 * n) + [hbm] * (2 * n) + [pl.BlockSpec(memory_space=pltpu.VMEM)],
        input_output_aliases={**{a: 3 * n + a for a in range(n)}, **{n + a: 4 * n + a for a in range(n)}},
        compiler_params=pltpu.CompilerParams(has_side_effects=pltpu.SideEffectType.DATAFLOW_SIDE_EFFECTING),
    )(*[pltpu.with_memory_space_constraint(s, pltpu.HBM) for s in srcs],
      *[pltpu.with_memory_space_constraint(lax.empty(ls, s.dtype), pltpu.HBM) for ls, s in zip(lshape, srcs)])
    return dict(send=outs[:n], recv=outs[n:2 * n], local=outs[2 * n:3 * n], srcs=outs[3 * n:4 * n],
                lands=outs[4 * n:5 * n], token=outs[5 * n], blocked=blocked)


def _send_wait(started, which, after, name):
    n = len(which)
    blocked = started["blocked"]
    pick = lambda key: [started[key][a] for a in which]
    MESH = pl.DeviceIdType.MESH

    def body(*refs):
        src_refs, land_refs = refs[:n], refs[n:2 * n]
        send, recv, local = refs[2 * n:3 * n], refs[3 * n:4 * n], refs[4 * n:5 * n]
        x, y, c = _mesh_pos()
        me = 4 * x + 2 * y + c
        for a in range(n):
            for k, peer, blk in _peers(x, y, c):
                cp = pltpu.make_async_remote_copy(
                    src_ref=src_refs[a].at[blk] if blocked else src_refs[a], dst_ref=land_refs[a].at[blk],
                    send_sem=send[a].at[k], recv_sem=recv[a].at[k], device_id=peer, device_id_type=MESH)
                cp.wait_send()
                cp.wait_recv()
            pltpu.make_async_copy(src_refs[a].at[me] if blocked else src_refs[a], land_refs[a].at[me],
                                  local[a]).wait()

    hbm = pl.BlockSpec(memory_space=pltpu.HBM)
    sem = pl.BlockSpec(memory_space=pltpu.SEMAPHORE)
    srcs, lands = pick("srcs"), pick("lands")
    outs = pl.pallas_call(
        body, name=name, out_shape=[pltpu.HBM(s.shape, s.dtype) for s in srcs + lands],
        in_specs=[hbm] * (2 * n) + [sem] * (3 * n) + [pl.BlockSpec(memory_space=pl.ANY)], out_specs=[hbm] * (2 * n),
        input_output_aliases={a: a for a in range(2 * n)},
        compiler_params=pltpu.CompilerParams(has_side_effects=pltpu.SideEffectType.DATAFLOW_SIDE_EFFECTING),
    )(*srcs, *lands, *pick("send"), *pick("recv"), *pick("local"), after)
    return outs[n:]


def _adamw_weight(parts, w, m, v, name):
    L, R, C = w.shape
    tr = max(t for t in range(16, min(R, 256) + 1, 16) if R % t == 0)
    nr = R // tr

    def body(*refs):
        p_refs = refs[:L]
        w_ref, m_ref, v_ref, g_ref, d_ref, nm_ref, nv_ref, gsum = refs[L:]
        layer = pl.program_id(0)
        for l in range(L):
            @pl.when(layer == l)
            def _(l=l):
                g = p_refs[l][0].astype(F32)
                for s in range(1, N_DEV):
                    g = g + p_refs[l][s].astype(F32)
                gsum[...] = g
        g = gsum[...]
        g_ref[...] = g
        d_ref[...], nm_ref[...], nv_ref[...] = _adamw_math(w_ref[...], g, m_ref[...], v_ref[...])

    def part_spec(l):
        return pl.BlockSpec((N_DEV, tr, C), lambda layer, i: (0, jnp.where(layer == l, i, jnp.where(layer < l, 0, nr - 1)), 0))

    ws = pl.BlockSpec((None, tr, C), lambda layer, i: (layer, i, 0))
    return pl.pallas_call(
        body, name=name, grid=(L, nr), in_specs=[part_spec(l) for l in range(L)] + [ws] * 3, out_specs=[ws] * 4,
        out_shape=[jax.ShapeDtypeStruct((L, R, C), F32)] * 4, scratch_shapes=[pltpu.VMEM((tr, C), F32)],
        compiler_params=pltpu.CompilerParams(dimension_semantics=("arbitrary", "arbitrary")),
    )(*parts, w, m, v)


def _seg_rows(n, align):
    rows = -(-n // ROW)
    return -(-rows // align) * align


def _pack(arrs, align, lead=()):
    parts = []
    for a in arrs:
        n = int(np.prod(a.shape[len(lead):]))
        rows = _seg_rows(n, align)
        flat = a.reshape(lead + (n,))
        flat = jnp.pad(flat, [(0, 0)] * len(lead) + [(0, rows * ROW - n)])
        parts.append(flat.reshape(lead + (rows, ROW)))
    return jnp.concatenate(parts, axis=len(lead))


def _unpack(buf, shapes, align, lead=()):
    out, r0 = [], 0
    for shp in shapes:
        n = int(np.prod(shp))
        rows = _seg_rows(n, align)
        seg = lax.slice_in_dim(buf, r0, r0 + rows, axis=len(lead))
        out.append(seg.reshape(lead + (rows * ROW,))[..., :n].reshape(lead + tuple(shp)))
        r0 += rows
    return out


def kernel(x, p, norm_mix, norm_ffn, norm_ple, norm_kv, norm_final, a_w_in, a_norm_v, a_w_s, a_b_s, a_w_out, w_kv, b_w_q, b_sinks, b_w_o, f_w_up, f_conv_w, f_conv_b, f_w_down, ple_w_in, ple_w_gate, ple_b_gate, loss_target, m_norm_mix, m_norm_ffn, m_norm_ple, m_norm_kv, m_norm_final, m_a_w_in, m_a_norm_v, m_a_w_s, m_a_b_s, m_a_w_out, m_w_kv, m_b_w_q, m_b_sinks, m_b_w_o, m_f_w_up, m_f_conv_w, m_f_conv_b, m_f_w_down, m_ple_w_in, m_ple_w_gate, m_ple_b_gate, v_norm_mix, v_norm_ffn, v_norm_ple, v_norm_kv, v_norm_final, v_a_w_in, v_a_norm_v, v_a_w_s, v_a_b_s, v_a_w_out, v_w_kv, v_b_w_q, v_b_sinks, v_b_w_o, v_f_w_up, v_f_conv_w, v_f_conv_b, v_f_w_down, v_ple_w_in, v_ple_w_gate, v_ple_b_gate):
    w = dict(norm_mix=norm_mix, norm_ffn=norm_ffn, norm_ple=norm_ple, norm_kv=norm_kv, norm_final=norm_final,
             a_w_in=a_w_in, a_norm_v=a_norm_v, a_w_s=a_w_s, a_b_s=a_b_s, a_w_out=a_w_out, w_kv=w_kv, b_w_q=b_w_q,
             b_sinks=b_sinks, b_w_o=b_w_o, f_w_up=f_w_up, f_conv_w=f_conv_w, f_conv_b=f_conv_b, f_w_down=f_w_down,
             ple_w_in=ple_w_in, ple_w_gate=ple_w_gate, ple_b_gate=ple_b_gate)
    mom_m = dict(norm_mix=m_norm_mix, norm_ffn=m_norm_ffn, norm_ple=m_norm_ple, norm_kv=m_norm_kv,
                 norm_final=m_norm_final, a_w_in=m_a_w_in, a_norm_v=m_a_norm_v, a_w_s=m_a_w_s, a_b_s=m_a_b_s,
                 a_w_out=m_a_w_out, w_kv=m_w_kv, b_w_q=m_b_w_q, b_sinks=m_b_sinks, b_w_o=m_b_w_o, f_w_up=m_f_w_up,
                 f_conv_w=m_f_conv_w, f_conv_b=m_f_conv_b, f_w_down=m_f_w_down, ple_w_in=m_ple_w_in,
                 ple_w_gate=m_ple_w_gate, ple_b_gate=m_ple_b_gate)
    mom_v = dict(norm_mix=v_norm_mix, norm_ffn=v_norm_ffn, norm_ple=v_norm_ple, norm_kv=v_norm_kv,
                 norm_final=v_norm_final, a_w_in=v_a_w_in, a_norm_v=v_a_norm_v, a_w_s=v_a_w_s, a_b_s=v_a_b_s,
                 a_w_out=v_a_w_out, w_kv=v_w_kv, b_w_q=v_b_w_q, b_sinks=v_b_sinks, b_w_o=v_b_w_o, f_w_up=v_f_w_up,
                 f_conv_w=v_f_conv_w, f_conv_b=v_f_conv_b, f_w_down=v_f_w_down, ple_w_in=v_ple_w_in,
                 ple_w_gate=v_ple_w_gate, ple_b_gate=v_ple_b_gate)
    S = x.shape[1]
    tm = min(256, S)
    tl = min(512, S)
    me = 4 * lax.axis_index("x") + 2 * lax.axis_index("y") + lax.axis_index("c")

    def after_token(tok, a):
        return a + tok[0:1, 0:1].reshape((1,) * a.ndim)

    wb = {n: w[n].astype(BF) for n in _BIG}
    order = [('a_w_in', 0), ('a_w_out', 0), ('a_norm_v', None), ('f_conv_w', 0), ('f_conv_w', 1),
             ('f_w_up', 0), ('f_w_down', 0),
             ('ple_w_in', 0), ('ple_w_gate', 0), ('w_kv', None), ('b_w_q', 0), ('b_w_o', 0),
             ('f_w_up', 1), ('f_w_down', 1), ('ple_w_in', 1), ('ple_w_gate', 1)]
    src = lambda n, l: (wb[n] if n in wb else w[n]) if l is None else (wb[n] if n in wb else w[n])[l]
    first = 5
    g = dict(zip(order[:first], _all_gather_now([src(n, l) for n, l in order[:first]], "gather_mix")))
    later = [src(n, l) for n, l in order[first:]]
    tie = order.index(('ple_w_in', 0)) - first
    later[tie] = later[tie] + (g[order[0]][0, 0:1, 0:1] * 0).astype(later[tie].dtype)
    gath = _send_start(later, False, "gather_start")
    x0 = x[0]

    def gathered(names, after, name):
        which = [order.index(k) - first for k in names]
        return dict(zip(names, _send_wait(gath, which, after, name)))

    win = g[('a_w_in', 0)].transpose(1, 0, 2).reshape(D, 2 * AW)
    wout = g[('a_w_out', 0)].reshape(AW, D)
    gv = g[('a_norm_v', None)].reshape(1, AW)
    cw = [g[('f_conv_w', l)] for l in range(2)]
    cb = [w['f_conv_b'][l].reshape(N_DEV, 1, FF_BLK) for l in range(2)]
    ws = a_w_s[0].astype(BF)
    wst = jnp.swapaxes(ws, 1, 2)
    bsb = jnp.broadcast_to(a_b_s[0][:, :, None], (GROUPS, CHUNK, CHUNK))
    row = lambda a: a.reshape(1, -1)
    pb = [p[l, 0].astype(BF) for l in range(2)]
    sinks = b_sinks.reshape(N_Q)
    consts = _attn_consts()

    h1, zg0, gp0 = _mixa_fwd(x0, row(norm_mix[0]), win, gv, ws, bsb, wout, tm)
    g = gathered(order[5:7], h1, "gather_wait_ffn0")
    wup, wdown = [g[('f_w_up', 0)], None], [g[('f_w_down', 0)].reshape(D_FF, D), None]
    h2, hh0, cc0 = _ffn_fwd(h1, row(norm_ffn[0]), wup[0], cw[0], cb[0], wdown[0], tm)
    g = gathered(order[7:12], h2, "gather_wait_mid")
    wple = [g[('ple_w_in', 0)].transpose(1, 0, 2).reshape(PLE, D), None]
    wgate = [g[('ple_w_gate', 0)].reshape(D, D), None]
    wkv, wq, wo = g[('w_kv', None)].reshape(D, 2 * N_KV * HD), g[('b_w_q', 0)].reshape(D, D), g[('b_w_o', 0)].reshape(D, D)
    h3 = _ple_fwd(h2, row(norm_ple[0]), wgate[0], row(ple_b_gate[0]), pb[0], wple[0], tl)
    q, kv = _qkv_fwd(h3, row(norm_mix[1]), wq, row(norm_kv), wkv, tl)
    att = _attn_fwd(q, kv, sinks, consts)
    h4 = _oproj_fwd(att, wo, h3, tl)
    g = gathered(order[12:], h4, "gather_wait_l1")
    wup[1], wdown[1] = g[('f_w_up', 1)], g[('f_w_down', 1)].reshape(D_FF, D)
    wple[1], wgate[1] = g[('ple_w_in', 1)].transpose(1, 0, 2).reshape(PLE, D), g[('ple_w_gate', 1)].reshape(D, D)
    h5, hh1, cc1 = _ffn_fwd(h4, row(norm_ffn[1]), wup[1], cw[1], cb[1], wdown[1], tm)
    h6 = _ple_fwd(h5, row(norm_ple[1]), wgate[1], row(ple_b_gate[1]), pb[1], wple[1], tl)

    def col_blocks(g2d, n):
        return g2d.reshape(g2d.shape[0], N_DEV, n).transpose(1, 0, 2)

    def row_blocks(g2d):
        return g2d.reshape(N_DEV, g2d.shape[0] // N_DEV, g2d.shape[1])

    def ple_grads(l, de, xp, dpre):
        return [col_blocks(_wgrad(pb[l], de, f"wgrad_ple_in{l}", tn=1024), D // N_DEV),
                row_blocks(_wgrad(xp, dpre, f"wgrad_gate{l}", tn=1024))]

    def ffn_grads(l, xf, dhh, act, dhb):
        return [_wgrad(xf, dhh, f"wgrad_up{l}", tn=FF_BLK, out_blocked=True, a_transposed=True),
                row_blocks(_wgrad(act, dhb, f"wgrad_down{l}", tn=1024))]

    dh6, loss_blk, d_norm_final = _head(h6, row(norm_final), loss_target[0], tl)
    dh5, de1, xp1, dpre1, dbg1, dgp1 = _ple_bwd(dh6, h5, row(norm_ple[1]), wgate[1], row(ple_b_gate[1]), pb[1],
                                                 wple[1], tl)
    ex_ple1 = _send_start(ple_grads(1, de1, xp1, dpre1), True, "send_ple1")
    dh4, xf1, dhh1, act1, dhb5, dcw1, dcb1, dgf1 = _ffn_bwd(dh5, h4, hh1, cc1, after_token(ex_ple1["token"], row(norm_ffn[1])),
                                                            wup[1], cw[1], wdown[1], tm)
    ex_ffn1 = _send_start(ffn_grads(1, xf1, dhh1, act1, dhb5), True, "send_ffn1")
    datt, dhb4 = _oproj_bwd(dh4, wo, tl)
    dq, dkv, dsink = _attn_bwd(q, kv, att, datt, after_token(ex_ffn1["token"], sinks.reshape(1, N_Q)).reshape(N_Q), consts)
    dh3, xq, xkv, dgq, dgkv = _qkv_bwd(dh4, h3, dq, dkv, row(norm_mix[1]), wq, row(norm_kv), wkv, tl)
    ex_att = _send_start([row_blocks(_wgrad(att, dhb4, "wgrad_o", tn=1024)),
                              row_blocks(_wgrad(xq, dq, "wgrad_q", tn=1024)),
                              row_blocks(_wgrad(xkv, dkv, "wgrad_kv", tn=512))], True, "send_att")
    dh2, de0, xp0, dpre0, dbg0, dgp0 = _ple_bwd(dh3, h2, after_token(ex_att["token"], row(norm_ple[0])), wgate[0],
                                                 row(ple_b_gate[0]), pb[0], wple[0], tl)
    ex_ple0 = _send_start(ple_grads(0, de0, xp0, dpre0), True, "send_ple0")
    dh1, xf0, dhh0, act0, dhb2, dcw0, dcb0, dgf0 = _ffn_bwd(dh2, h1, hh0, cc0, after_token(ex_ple0["token"], row(norm_ffn[0])),
                                                            wup[0], cw[0], wdown[0], tm)
    ex_ffn0 = _send_start(ffn_grads(0, xf0, dhh0, act0, dhb2), True, "send_ffn0")
    dx, xa, dz, gated, dhb1, dws, dbs, dgv, dga = _mixa_bwd(dh1, x0, zg0, gp0, after_token(ex_ffn0["token"], row(norm_mix[0])), win, gv,
                                                            ws, wst, bsb, wout, tm)
    small_g = {
        'norm_mix': jnp.concatenate([dga, dgq], axis=0), 'norm_ffn': jnp.concatenate([dgf0, dgf1], axis=0),
        'norm_ple': jnp.concatenate([dgp0, dgp1], axis=0), 'norm_kv': dgkv.reshape(D),
        'norm_final': d_norm_final.reshape(D), 'a_w_s': dws[None], 'a_b_s': dbs[None, :, :, 0],
        'b_sinks': dsink[:, 0].reshape(1, N_Q),
        'f_conv_b': jnp.stack([dcb0.reshape(2 * D_FF), dcb1.reshape(2 * D_FF)]),
        'ple_b_gate': jnp.concatenate([dbg0, dbg1], axis=0),
        'a_norm_v': dgv, 'f_conv_w': jnp.stack([dcw0, dcw1], axis=1),
    }
    names_all = _REPL + _SMALL_SHARDED
    full_shapes = [small_g[n].shape for n in names_all]
    small_packed = _pack([small_g[n] for n in names_all], 8)
    ex_small = _send_start([small_packed.reshape(N_DEV, small_packed.shape[0] // N_DEV, ROW)], True, "send_small")
    ex_a_in = _send_start([_wgrad(xa, dz, "wgrad_a_in", tn=2 * AW // N_DEV, out_blocked=True, a_transposed=True,
                                  per_step=4)], True, "send_a_in")
    ex_a_out = _send_start([row_blocks(_wgrad(gated, dhb1, "wgrad_a_out", tn=1024))],
                           True, "send_a_out")
    small_sum = _sum8(_send_wait(ex_small, [0], ex_a_out["token"], "recv_small")[0], "sum_small_grads")
    ex_small_all = _send_start([small_sum], False, "gather_small_sums")

    def as3(a):
        return a.reshape((-1,) + a.shape[-2:])

    def adam(name, parts):
        outs = _adamw_weight(parts, as3(w[name]), as3(mom_m[name]), as3(mom_v[name]), "adamw_" + name)
        return [o.reshape(w[name].shape) for o in outs]

    big = {}
    l_ple1 = _send_wait(ex_ple1, list(range(len(ex_ple1["lands"]))), ex_small_all["token"], "recv_ple1")
    l_ffn1 = _send_wait(ex_ffn1, list(range(len(ex_ffn1["lands"]))), l_ple1[0], "recv_ffn1")
    l_att = _send_wait(ex_att, list(range(len(ex_att["lands"]))), l_ffn1[0], "recv_att")
    big['b_w_o'], big['b_w_q'], big['w_kv'] = adam('b_w_o', [l_att[0]]), adam('b_w_q', [l_att[1]]), adam('w_kv', [l_att[2]])
    l_ple0 = _send_wait(ex_ple0, list(range(len(ex_ple0["lands"]))), big['w_kv'][0], "recv_ple0")
    big['ple_w_in'] = adam('ple_w_in', [l_ple0[0], l_ple1[0]])
    big['ple_w_gate'] = adam('ple_w_gate', [l_ple0[1], l_ple1[1]])
    l_ffn0 = _send_wait(ex_ffn0, list(range(len(ex_ffn0["lands"]))), big['ple_w_gate'][0], "recv_ffn0")
    big['f_w_up'] = adam('f_w_up', [l_ffn0[0], l_ffn1[0]])
    big['f_w_down'] = adam('f_w_down', [l_ffn0[1], l_ffn1[1]])
    big['a_w_in'] = adam('a_w_in', _send_wait(ex_a_in, [0], big['f_w_down'][0], "recv_a_in"))
    big['a_w_out'] = adam('a_w_out', _send_wait(ex_a_out, [0], big['a_w_in'][0], "recv_a_out"))
    g_big, d_big, m_big, v_big = [{n: big[n][i] for n in _BIG} for i in range(4)]

    gsum = _send_wait(ex_small_all, [0], big['a_w_out'][0], "recv_small_sums")[0].reshape(small_packed.shape)
    gs = dict(zip(names_all, _unpack(gsum, full_shapes, 8)))
    gs['a_norm_v'] = lax.dynamic_slice_in_dim(gs['a_norm_v'], me * (AW // N_DEV), AW // N_DEV, axis=1)
    gs['f_conv_w'] = lax.dynamic_index_in_dim(gs['f_conv_w'], me, axis=0, keepdims=False)
    gs = {n: gs[n].reshape(w[n].shape) for n in names_all}
    packs = lambda d: _pack([d[n] for n in names_all], 8)
    souts = _adamw_small(packs(gs), packs(w), packs(mom_m), packs(mom_v), "adamw_small")
    shp = [w[n].shape for n in names_all]
    d_s, m_s, v_s = [dict(zip(names_all, _unpack(o, shp, 8))) for o in souts]

    grads = {**g_big, **gs}
    delta = {**d_big, **d_s}
    new_m = {**m_big, **m_s}
    new_v = {**v_big, **v_s}
    loss = lax.psum(loss_blk[0, 0], ("x", "y", "c"))
    return (loss, dx[None], *[grads[n] for n in _PARAMS], *[delta[n] for n in _PARAMS],
            *[new_m[n] for n in _PARAMS], *[new_v[n] for n in _PARAMS])
```

```python
import functools

import numpy as np
import jax
import jax.numpy as jnp
from jax import lax
from jax.experimental import pallas as pl
from jax.experimental.pallas import tpu as pltpu

F32 = jnp.float32
BF = jnp.bfloat16

N_DEV = 8
D = 1024
AW = 1024
GROUPS = 8
CHUNK = 128
HD = 64
N_Q = 16
N_KV = 4
GQA = 4
BLOCK = 128
D_FF = 2816
FF_BLK = 704
N_FF_BLK = 4
PLE = 256
EPS = 1e-6
LR, B1, B2, ADAM_EPS, WD, STEP = 0.001, 0.9, 0.999, 1e-08, 0.01, 10
GELU_C0 = 0.7978845608028654
GELU_C1 = 0.044715
NEG = -1e30
ROW = 1024
VMEM_BIG = 56 * 1024 * 1024

_PARAMS = ['norm_mix', 'norm_ffn', 'norm_ple', 'norm_kv', 'norm_final', 'a_w_in', 'a_norm_v', 'a_w_s', 'a_b_s',
           'a_w_out', 'w_kv', 'b_w_q', 'b_sinks', 'b_w_o', 'f_w_up', 'f_conv_w', 'f_conv_b', 'f_w_down',
           'ple_w_in', 'ple_w_gate', 'ple_b_gate']
_BIG = ['a_w_in', 'a_w_out', 'w_kv', 'b_w_q', 'b_w_o', 'f_w_up', 'f_w_down', 'ple_w_in', 'ple_w_gate']
_SMALL_SHARDED = ['a_norm_v', 'f_conv_w']
_REPL = ['norm_mix', 'norm_ffn', 'norm_ple', 'norm_kv', 'norm_final', 'a_w_s', 'a_b_s', 'b_sinks', 'f_conv_b',
         'ple_b_gate']


def _mm(a, b):
    return jnp.dot(a, b, preferred_element_type=F32)


def _mm_nt(a, b):
    return lax.dot_general(a, b, (((1,), (1,)), ((), ())), preferred_element_type=F32)


def _mm_tn(a, b):
    return lax.dot_general(a, b, (((0,), (0,)), ((), ())), preferred_element_type=F32)


def _const_spec(shape, single=False):
    nd = len(shape)
    if single:
        return pl.BlockSpec(shape, lambda *_: (0,) * nd, pipeline_mode=pl.Buffered(1))
    return pl.BlockSpec(shape, lambda *_: (0,) * nd)


def _row_spec(tm, n):
    return pl.BlockSpec((tm, n), lambda i: (i, 0))


def _rms_fwd(x, g):
    r = lax.rsqrt(jnp.mean(x * x, axis=-1, keepdims=True) + EPS)
    xhat = x * r
    return xhat * g, xhat, r


def _rms_bwd(dy, xhat, r, g):
    a = dy * g
    dx = r * (a - xhat * jnp.mean(a * xhat, axis=-1, keepdims=True))
    return dx, jnp.sum(dy * xhat, axis=0, keepdims=True)


def _sigmoid(x):
    return 1.0 / (1.0 + jnp.exp(-x))


def _params(vmem=None):
    return pltpu.CompilerParams(dimension_semantics=("arbitrary",), vmem_limit_bytes=vmem)


def _tri_masks():
    t = lax.broadcasted_iota(jnp.int32, (CHUNK, CHUNK), 0)
    s = lax.broadcasted_iota(jnp.int32, (CHUNK, CHUNK), 1)
    return t >= s, s >= t


def _mixa_fwd(x, g, win, gv, ws, bsb, wout, tm):
    S = x.shape[0]
    nc = tm // CHUNK

    def body(x_ref, g_ref, win_ref, gv_ref, ws_ref, bsb_ref, wout_ref, h_ref, zg_ref, gp_ref, gated):
        x = x_ref[...]
        xn, _, _ = _rms_fwd(x, g_ref[...])
        z = _mm(xn.astype(BF), win_ref[...])
        z2 = z * z
        t = jnp.tanh(GELU_C0 * (z + GELU_C1 * z2 * z))
        zg = 0.5 * z * (1.0 + t)
        zg_ref[...] = zg.astype(BF)
        gp_ref[...] = (0.5 * (1.0 + t) + 0.5 * z * (1.0 - t * t) * (GELU_C0 * (1.0 + 3.0 * GELU_C1 * z2))).astype(BF)
        u = zg[:, :AW]
        vn, _, _ = _rms_fwd(zg[:, AW:], gv_ref[...])
        vn = vn.astype(BF)
        tril, _ = _tri_masks()
        for h in range(GROUPS):
            wm = jnp.where(tril, ws_ref[h], jnp.zeros((), BF))
            for c in range(nc):
                rs, cs = slice(c * CHUNK, (c + 1) * CHUNK), slice(h * CHUNK, (h + 1) * CHUNK)
                s = _mm(wm, vn[rs, cs]) + bsb_ref[h]
                gated[rs, cs] = (u[rs, cs] * s).astype(BF)
        h_ref[...] = x + _mm(gated[...], wout_ref[...])

    return pl.pallas_call(
        body, name="mixa_fwd", grid=(S // tm,),
        in_specs=[_row_spec(tm, D), _const_spec((1, D)), _const_spec((D, 2 * AW)), _const_spec((1, AW)),
                  _const_spec((GROUPS, CHUNK, CHUNK)), _const_spec((GROUPS, CHUNK, CHUNK)), _const_spec((AW, D))],
        out_specs=[_row_spec(tm, D), _row_spec(tm, 2 * AW), _row_spec(tm, 2 * AW)],
        out_shape=[jax.ShapeDtypeStruct((S, D), F32), jax.ShapeDtypeStruct((S, 2 * AW), BF),
                   jax.ShapeDtypeStruct((S, 2 * AW), BF)],
        scratch_shapes=[pltpu.VMEM((tm, AW), BF)],
        compiler_params=_params(VMEM_BIG),
    )(x, g, win, gv, ws, bsb, wout)


def _mixa_bwd(dh, x, zg, gp, g, win, gv, ws, wst, bsb, wout, tm):
    S = x.shape[0]
    nt = S // tm
    nc = tm // CHUNK

    def body(dh_ref, x_ref, zg_ref, gp_ref, g_ref, win_ref, gv_ref, ws_ref, wst_ref, bsb_ref, wout_ref,
             dx_ref, xn_ref, dz_ref, gated_ref, dhb_ref, dws_ref, dbs_ref, dgv_ref, dg_ref, du_scr, dvn_scr):
        i = pl.program_id(0)

        @pl.when(i == 0)
        def _():
            dws_ref[...] = jnp.zeros_like(dws_ref)
            dbs_ref[...] = jnp.zeros_like(dbs_ref)
            dgv_ref[...] = jnp.zeros_like(dgv_ref)
            dg_ref[...] = jnp.zeros_like(dg_ref)

        x = x_ref[...]
        gmix = g_ref[...]
        xn, xhat, r = _rms_fwd(x, gmix)
        xn_ref[...] = xn.T.astype(BF)
        zg = zg_ref[...].astype(F32)
        gp = gp_ref[...].astype(F32)
        u = zg[:, :AW]
        gvv = gv_ref[...]
        vn, vhat, rv = _rms_fwd(zg[:, AW:], gvv)
        vn = vn.astype(BF)
        dh = dh_ref[...]
        dhb = dh.astype(BF)
        dhb_ref[...] = dhb
        dgated = _mm_nt(dhb, wout_ref[...])
        tril, triu = _tri_masks()
        for h in range(GROUPS):
            wm = jnp.where(tril, ws_ref[h], jnp.zeros((), BF))
            wmt = jnp.where(triu, wst_ref[h], jnp.zeros((), BF))
            for c in range(nc):
                rs, cs = slice(c * CHUNK, (c + 1) * CHUNK), slice(h * CHUNK, (h + 1) * CHUNK)
                vnb = vn[rs, cs]
                s = _mm(wm, vnb) + bsb_ref[h]
                gated_ref[rs, cs] = (u[rs, cs] * s).astype(BF)
                dgt = dgated[rs, cs]
                du_scr[rs, cs] = dgt * s
                ds = dgt * u[rs, cs]
                dsb = ds.astype(BF)
                dws_ref[h] += jnp.where(tril, _mm_nt(dsb, vnb), 0.0)
                dbs_ref[h] += ds
                dvn_scr[rs, cs] = _mm(wmt, dsb)
        dvn = dvn_scr[...]
        dv, dgv = _rms_bwd(dvn, vhat, rv, gvv)
        dgv_ref[...] += dgv
        dz_ref[:, :AW] = (du_scr[...] * gp[:, :AW]).astype(BF)
        dz_ref[:, AW:] = (dv * gp[:, AW:]).astype(BF)
        dxn = _mm_nt(dz_ref[...], win_ref[...])
        dxr, dg = _rms_bwd(dxn, xhat, r, gmix)
        dg_ref[...] += dg
        dx_ref[...] = dh + dxr

        @pl.when(i == nt - 1)
        def _():
            for h in range(GROUPS):
                dbs_ref[h] = jnp.broadcast_to(jnp.sum(dbs_ref[h], axis=-1, keepdims=True), (CHUNK, CHUNK))

    gshape = (GROUPS, CHUNK, CHUNK)
    return pl.pallas_call(
        body, name="mixa_bwd", grid=(nt,),
        in_specs=[_row_spec(tm, D), _row_spec(tm, D), _row_spec(tm, 2 * AW), _row_spec(tm, 2 * AW), _const_spec((1, D)),
                  _const_spec((D, 2 * AW)), _const_spec((1, AW)), _const_spec(gshape), _const_spec(gshape),
                  _const_spec(gshape), _const_spec((AW, D))],
        out_specs=[_row_spec(tm, D), pl.BlockSpec((D, tm), lambda i: (0, i)), _row_spec(tm, 2 * AW), _row_spec(tm, AW),
                   _row_spec(tm, D), _const_spec(gshape), _const_spec(gshape), _const_spec((1, AW)), _const_spec((1, D))],
        out_shape=[jax.ShapeDtypeStruct((S, D), F32), jax.ShapeDtypeStruct((D, S), BF),
                   jax.ShapeDtypeStruct((S, 2 * AW), BF), jax.ShapeDtypeStruct((S, AW), BF),
                   jax.ShapeDtypeStruct((S, D), BF), jax.ShapeDtypeStruct(gshape, F32),
                   jax.ShapeDtypeStruct(gshape, F32), jax.ShapeDtypeStruct((1, AW), F32),
                   jax.ShapeDtypeStruct((1, D), F32)],
        scratch_shapes=[pltpu.VMEM((tm, AW), F32), pltpu.VMEM((tm, AW), F32)],
        compiler_params=_params(VMEM_BIG),
    )(dh, x, zg, gp, g, win, gv, ws, wst, bsb, wout)


def _shift_down(x, prev8, k):
    n, c = x.shape[0] // 8, x.shape[1]
    r = pltpu.roll(x.reshape(n, 8, c), k, axis=1)
    before = jnp.concatenate([pltpu.roll(prev8.reshape(1, 8, c), k, axis=1), r[:-1]], axis=0)
    sub = lax.broadcasted_iota(jnp.int32, (1, 8, c), 1)
    return jnp.where(sub < k, before, r).reshape(x.shape)


def _shift_up(x, next8, k):
    n, c = x.shape[0] // 8, x.shape[1]
    r = pltpu.roll(x.reshape(n, 8, c), 8 - k, axis=1)
    after = jnp.concatenate([r[1:], pltpu.roll(next8.reshape(1, 8, c), 8 - k, axis=1)], axis=0)
    sub = lax.broadcasted_iota(jnp.int32, (1, 8, c), 1)
    return jnp.where(sub >= 8 - k, after, r).reshape(x.shape)


def _ffn_fwd(h, g, wup, cw, cb, wdown, tm):
    S = h.shape[0]

    def body(h_ref, g_ref, wup_ref, cw_ref, cb_ref, wdown_ref, o_ref, hh_ref, cc_ref, sh):
        i = pl.program_id(0)

        @pl.when(i == 0)
        def _():
            sh[...] = jnp.zeros_like(sh)

        x = h_ref[...]
        xn, _, _ = _rms_fwd(x, g_ref[...])
        xn = xn.astype(BF)
        acc = x
        up = lambda d: [_mm(xn, wup_ref[d]), _mm(xn, wup_ref[N_FF_BLK + d])]
        ahead = up(0)
        for d in range(N_FF_BLK):
            cs = []
            hbs = ahead
            if d + 1 < N_FF_BLK:
                ahead = up(d + 1)
            for blk, hb in zip((d, N_FF_BLK + d), hbs):
                hh_ref[blk] = hb.astype(BF)
                w, prev8 = cw_ref[blk], sh[blk]
                c = cb_ref[blk] + w[0:1] * _shift_down(hb, prev8, 2) + w[1:2] * _shift_down(hb, prev8, 1) + w[2:3] * hb
                sh[blk] = hb[tm - 8:, :]
                cc_ref[blk] = c.astype(BF)
                cs.append(c)
            act = cs[0] * _sigmoid(cs[0]) * cs[1]
            acc = acc + _mm(act.astype(BF), wdown_ref[d * FF_BLK:(d + 1) * FF_BLK, :])
        o_ref[...] = acc

    blk3 = pl.BlockSpec((N_DEV, tm, FF_BLK), lambda i: (0, i, 0))
    return pl.pallas_call(
        body, name="ffn_fwd", grid=(S // tm,),
        in_specs=[_row_spec(tm, D), _const_spec((1, D)), _const_spec((N_DEV, D, FF_BLK), True),
                  _const_spec((N_DEV, 3, FF_BLK)), _const_spec((N_DEV, 1, FF_BLK)), _const_spec((D_FF, D), True)],
        out_specs=[_row_spec(tm, D), blk3, blk3],
        out_shape=[jax.ShapeDtypeStruct((S, D), F32), jax.ShapeDtypeStruct((N_DEV, S, FF_BLK), BF),
                   jax.ShapeDtypeStruct((N_DEV, S, FF_BLK), BF)],
        scratch_shapes=[pltpu.VMEM((N_DEV, 8, FF_BLK), F32)],
        compiler_params=_params(VMEM_BIG),
    )(h, g, wup, cw, cb, wdown)


def _ffn_bwd(dh, h, hh, cc, g, wup, cw, wdown, tm):
    S = h.shape[0]
    nt = S // tm

    def rev(i):
        return (nt - 1 - i, 0)

    def body(dh_ref, h_ref, hh_ref, cc_ref, g_ref, wup_ref, cw_ref, wdown_ref,
             dx_ref, xn_ref, dhh_ref, act_ref, dhb_ref, dcw_ref, dcb_ref, dg_ref, sh):
        i = pl.program_id(0)

        @pl.when(i == 0)
        def _():
            sh[...] = jnp.zeros_like(sh)
            dcw_ref[...] = jnp.zeros_like(dcw_ref)
            dcb_ref[...] = jnp.zeros_like(dcb_ref)
            dg_ref[...] = jnp.zeros_like(dg_ref)

        x = h_ref[...]
        gain = g_ref[...]
        xn, xhat, r = _rms_fwd(x, gain)
        xn_ref[...] = xn.T.astype(BF)
        dh = dh_ref[...]
        dhb = dh.astype(BF)
        dhb_ref[...] = dhb
        dxn = jnp.zeros((tm, D), F32)
        dact_of = lambda d: _mm_nt(dhb, wdown_ref[d * FF_BLK:(d + 1) * FF_BLK, :])
        ahead = dact_of(0)
        for d in range(N_FF_BLK):
            blks = (d, N_FF_BLK + d)
            dact = ahead
            if d + 1 < N_FF_BLK:
                ahead = dact_of(d + 1)
            cg, cu = cc_ref[blks[0]].astype(F32), cc_ref[blks[1]].astype(F32)
            sg = _sigmoid(cg)
            sil = cg * sg
            act_ref[d] = (sil * cu).astype(BF)
            dcs = (dact * cu * (sg + sil - sil * sg), dact * sil)
            for blk, dc in zip(blks, dcs):
                next8 = sh[blk]
                dp1, dp2 = _shift_up(dc, next8, 1), _shift_up(dc, next8, 2)
                hb = hh_ref[blk].astype(F32)
                dcb_ref[blk] += jnp.sum(dc, axis=0, keepdims=True)
                dcw_ref[blk, 0:1, :] += jnp.sum(dp2 * hb, axis=0, keepdims=True)
                dcw_ref[blk, 1:2, :] += jnp.sum(dp1 * hb, axis=0, keepdims=True)
                dcw_ref[blk, 2:3, :] += jnp.sum(dc * hb, axis=0, keepdims=True)
                w = cw_ref[blk]
                dhh = (w[2:3] * dc + w[1:2] * dp1 + w[0:1] * dp2).astype(BF)
                sh[blk] = dc[0:8, :]
                dhh_ref[blk] = dhh
                dxn = dxn + _mm_nt(dhh, wup_ref[blk])
        dxr, dg = _rms_bwd(dxn, xhat, r, gain)
        dg_ref[...] += dg
        dx_ref[...] = dh + dxr

    blk3 = lambda i: (0, nt - 1 - i, 0)
    return pl.pallas_call(
        body, name="ffn_bwd", grid=(nt,),
        in_specs=[pl.BlockSpec((tm, D), rev), pl.BlockSpec((tm, D), rev),
                  pl.BlockSpec((N_DEV, tm, FF_BLK), blk3), pl.BlockSpec((N_DEV, tm, FF_BLK), blk3),
                  _const_spec((1, D)), _const_spec((N_DEV, D, FF_BLK), True), _const_spec((N_DEV, 3, FF_BLK)),
                  _const_spec((D_FF, D), True)],
        out_specs=[pl.BlockSpec((tm, D), rev), pl.BlockSpec((D, tm), lambda i: (0, nt - 1 - i)),
                   pl.BlockSpec((N_DEV, tm, FF_BLK), blk3), pl.BlockSpec((N_FF_BLK, tm, FF_BLK), blk3),
                   pl.BlockSpec((tm, D), rev), _const_spec((N_DEV, 3, FF_BLK)), _const_spec((N_DEV, 1, FF_BLK)),
                   _const_spec((1, D))],
        out_shape=[jax.ShapeDtypeStruct((S, D), F32), jax.ShapeDtypeStruct((D, S), BF),
                   jax.ShapeDtypeStruct((N_DEV, S, FF_BLK), BF), jax.ShapeDtypeStruct((N_FF_BLK, S, FF_BLK), BF),
                   jax.ShapeDtypeStruct((S, D), BF), jax.ShapeDtypeStruct((N_DEV, 3, FF_BLK), F32),
                   jax.ShapeDtypeStruct((N_DEV, 1, FF_BLK), F32), jax.ShapeDtypeStruct((1, D), F32)],
        scratch_shapes=[pltpu.VMEM((N_DEV, 8, FF_BLK), F32)],
        compiler_params=_params(VMEM_BIG),
    )(dh, h, hh, cc, g, wup, cw, wdown)


def _ple_fwd(h, g, wgate, bgate, p, wple, tm):
    S = h.shape[0]

    def body(h_ref, g_ref, wg_ref, bg_ref, p_ref, wp_ref, o_ref):
        x = h_ref[...]
        xn, _, _ = _rms_fwd(x, g_ref[...])
        gate = _sigmoid(_mm(xn.astype(BF), wg_ref[...]) + bg_ref[...])
        o_ref[...] = x + _mm(p_ref[...], wp_ref[...]) * gate

    return pl.pallas_call(
        body, name="ple_fwd", grid=(S // tm,),
        in_specs=[_row_spec(tm, D), _const_spec((1, D)), _const_spec((D, D)), _const_spec((1, D)),
                  _row_spec(tm, PLE), _const_spec((PLE, D))],
        out_specs=_row_spec(tm, D), out_shape=jax.ShapeDtypeStruct((S, D), F32),
        compiler_params=_params(),
    )(h, g, wgate, bgate, p, wple)


def _ple_bwd(dh, h, g, wgate, bgate, p, wple, tm):
    S = h.shape[0]

    def body(dh_ref, h_ref, g_ref, wg_ref, bg_ref, p_ref, wp_ref, dx_ref, de_ref, xn_ref, dpre_ref, dbg_ref, dg_ref):
        i = pl.program_id(0)

        @pl.when(i == 0)
        def _():
            dbg_ref[...] = jnp.zeros_like(dbg_ref)
            dg_ref[...] = jnp.zeros_like(dg_ref)

        x = h_ref[...]
        gain = g_ref[...]
        xn, xhat, r = _rms_fwd(x, gain)
        xnb = xn.astype(BF)
        xn_ref[...] = xnb
        gate = _sigmoid(_mm(xnb, wg_ref[...]) + bg_ref[...])
        e = _mm(p_ref[...], wp_ref[...])
        dh = dh_ref[...]
        de_ref[...] = (dh * gate).astype(BF)
        dpre = dh * e * gate * (1.0 - gate)
        dpreb = dpre.astype(BF)
        dpre_ref[...] = dpreb
        dbg_ref[...] += jnp.sum(dpre, axis=0, keepdims=True)
        dxr, dg = _rms_bwd(_mm_nt(dpreb, wg_ref[...]), xhat, r, gain)
        dg_ref[...] += dg
        dx_ref[...] = dh + dxr

    return pl.pallas_call(
        body, name="ple_bwd", grid=(S // tm,),
        in_specs=[_row_spec(tm, D), _row_spec(tm, D), _const_spec((1, D)), _const_spec((D, D)), _const_spec((1, D)),
                  _row_spec(tm, PLE), _const_spec((PLE, D))],
        out_specs=[_row_spec(tm, D), _row_spec(tm, D), _row_spec(tm, D), _row_spec(tm, D), _const_spec((1, D)),
                   _const_spec((1, D))],
        out_shape=[jax.ShapeDtypeStruct((S, D), F32), jax.ShapeDtypeStruct((S, D), BF),
                   jax.ShapeDtypeStruct((S, D), BF), jax.ShapeDtypeStruct((S, D), BF),
                   jax.ShapeDtypeStruct((1, D), F32), jax.ShapeDtypeStruct((1, D), F32)],
        compiler_params=_params(),
    )(dh, h, g, wgate, bgate, p, wple)


def _qkv_fwd(h, gq, wq, gkv, wkv, tm):
    S = h.shape[0]
    nkv = wkv.shape[1]

    def body(h_ref, gq_ref, wq_ref, gkv_ref, wkv_ref, q_ref, kv_ref):
        x = h_ref[...]
        _, xhat, _ = _rms_fwd(x, gq_ref[...])
        q_ref[...] = _mm((xhat * gq_ref[...]).astype(BF), wq_ref[...]).astype(BF)
        kv_ref[...] = _mm((xhat * gkv_ref[...]).astype(BF), wkv_ref[...]).astype(BF)

    return pl.pallas_call(
        body, name="qkv_fwd", grid=(S // tm,),
        in_specs=[_row_spec(tm, D), _const_spec((1, D)), _const_spec((D, D)), _const_spec((1, D)),
                  _const_spec((D, nkv))],
        out_specs=[_row_spec(tm, D), _row_spec(tm, nkv)],
        out_shape=[jax.ShapeDtypeStruct((S, D), BF), jax.ShapeDtypeStruct((S, nkv), BF)],
        compiler_params=_params(),
    )(h, gq, wq, gkv, wkv)


def _qkv_bwd(dh, h, dq, dkv, gq, wq, gkv, wkv, tm):
    S = h.shape[0]
    nkv = wkv.shape[1]

    def body(dh_ref, h_ref, dq_ref, dkv_ref, gq_ref, wq_ref, gkv_ref, wkv_ref,
             dx_ref, xq_ref, xkv_ref, dgq_ref, dgkv_ref):
        i = pl.program_id(0)

        @pl.when(i == 0)
        def _():
            dgq_ref[...] = jnp.zeros_like(dgq_ref)
            dgkv_ref[...] = jnp.zeros_like(dgkv_ref)

        x = h_ref[...]
        gq_, gkv_ = gq_ref[...], gkv_ref[...]
        _, xhat, r = _rms_fwd(x, gq_)
        xq_ref[...] = (xhat * gq_).astype(BF)
        xkv_ref[...] = (xhat * gkv_).astype(BF)
        d1, dg1 = _rms_bwd(_mm_nt(dq_ref[...], wq_ref[...]), xhat, r, gq_)
        d2, dg2 = _rms_bwd(_mm_nt(dkv_ref[...], wkv_ref[...]), xhat, r, gkv_)
        dgq_ref[...] += dg1
        dgkv_ref[...] += dg2
        dx_ref[...] = dh_ref[...] + d1 + d2

    return pl.pallas_call(
        body, name="qkv_bwd", grid=(S // tm,),
        in_specs=[_row_spec(tm, D), _row_spec(tm, D), _row_spec(tm, D), _row_spec(tm, nkv), _const_spec((1, D)),
                  _const_spec((D, D)), _const_spec((1, D)), _const_spec((D, nkv))],
        out_specs=[_row_spec(tm, D), _row_spec(tm, D), _row_spec(tm, D), _const_spec((1, D)), _const_spec((1, D))],
        out_shape=[jax.ShapeDtypeStruct((S, D), F32), jax.ShapeDtypeStruct((S, D), BF),
                   jax.ShapeDtypeStruct((S, D), BF), jax.ShapeDtypeStruct((1, D), F32),
                   jax.ShapeDtypeStruct((1, D), F32)],
        compiler_params=_params(),
    )(dh, h, dq, dkv, gq, wq, gkv, wkv)


def _oproj_fwd(a, w, res, tm):
    S = a.shape[0]

    def body(a_ref, w_ref, r_ref, o_ref):
        o_ref[...] = r_ref[...] + _mm(a_ref[...], w_ref[...])

    return pl.pallas_call(
        body, name="oproj_fwd", grid=(S // tm,),
        in_specs=[_row_spec(tm, D), _const_spec((D, D)), _row_spec(tm, D)],
        out_specs=_row_spec(tm, D), out_shape=jax.ShapeDtypeStruct((S, D), F32),
        compiler_params=_params(),
    )(a, w, res)


def _oproj_bwd(dh, w, tm):
    S = dh.shape[0]

    def body(dh_ref, w_ref, da_ref, dhb_ref):
        dhb = dh_ref[...].astype(BF)
        dhb_ref[...] = dhb
        da_ref[...] = _mm_nt(dhb, w_ref[...]).astype(BF)

    return pl.pallas_call(
        body, name="oproj_bwd", grid=(S // tm,),
        in_specs=[_row_spec(tm, D), _const_spec((D, D))],
        out_specs=[_row_spec(tm, D), _row_spec(tm, D)],
        out_shape=[jax.ShapeDtypeStruct((S, D), BF), jax.ShapeDtypeStruct((S, D), BF)],
        compiler_params=_params(),
    )(dh, w)


def _alibi_slope(hq):
    return float(np.float32(2.0 ** (-8.0 * (hq + 1) / N_Q)))


def _attn_consts():
    W = N_KV * HD
    r = np.zeros((W, N_KV * W), np.float32)
    for kh in range(N_KV):
        for c in range(W):
            r[kh * HD + c % HD, kh * W + c] = 1.0
    qi = np.arange(BLOCK)[:, None]
    kj = np.arange(2 * BLOCK)[None, :]
    dist = qi + BLOCK - kj
    band = (dist >= 0) & (dist < BLOCK)
    bias = np.full((2, N_KV, GQA * BLOCK, 2 * BLOCK), NEG, np.float32)
    for later in (0, 1):
        valid = band & ((kj >= BLOCK) | bool(later))
        for kh in range(N_KV):
            for g in range(GQA):
                slope = np.float32(_alibi_slope(kh * GQA + g))
                bias[later, kh, g * BLOCK:(g + 1) * BLOCK] = np.where(valid, -slope * dist.astype(np.float32), NEG)
    lane_head = np.arange(W)[None, :] // HD
    maskq = (np.arange(GQA * BLOCK)[:, None] // BLOCK == lane_head).astype(np.float32)
    maskk = (np.arange(GQA * 2 * BLOCK)[:, None] // (2 * BLOCK) == lane_head).astype(np.float32)
    head_sum = (lane_head.T == np.arange(GQA * BLOCK)[None, :] // BLOCK).astype(np.float32)
    return dict(rep=jnp.asarray(r, BF), bias=jnp.asarray(bias), maskq=jnp.asarray(maskq, BF),
                maskk=jnp.asarray(maskk, BF), head_sum=jnp.asarray(head_sum, BF))


def _sink_col(sink_ref, kh):
    rb = lax.broadcasted_iota(jnp.int32, (GQA * BLOCK, 1), 0) >> 7
    col = jnp.full((GQA * BLOCK, 1), sink_ref[kh * GQA + GQA - 1], F32)
    for g in range(GQA - 1):
        col = jnp.where(rb == g, sink_ref[kh * GQA + g], col)
    return col


def _stack4(a):
    return jnp.concatenate([a] * GQA, axis=0)


def _rows_to_lanes(a):
    return jnp.concatenate([a[g * BLOCK:(g + 1) * BLOCK] for g in range(GQA)], axis=1)


def _lanes_to_rows(a, n):
    return jnp.concatenate([a[:, g * n:(g + 1) * n] for g in range(GQA)], axis=0)


def _scores(qg, k4, bias, sink_col, maskq):
    qs = _stack4(qg) * maskq
    s = _mm_nt(qs, k4) * (HD ** -0.5) + bias
    m = jnp.maximum(jnp.max(s, axis=-1, keepdims=True), sink_col)
    return qs, jnp.exp(s - m), jnp.exp(sink_col - m)


def _bands(kvc_ref, kvp_ref, rep_ref):
    W = N_KV * HD
    kvp, kvc = kvp_ref[...], kvc_ref[...]
    kband = jnp.concatenate([kvp[:, :W], kvc[:, :W]], axis=0)
    vband = jnp.concatenate([kvp[:, W:], kvc[:, W:]], axis=0)
    return _mm(kband, rep_ref[...]).astype(BF), _mm(vband, rep_ref[...]).astype(BF)


def _attn_fwd(q, kv, sinks, consts):
    S = q.shape[0]
    nb = S // BLOCK
    W = N_KV * HD

    def body(sink_ref, q_ref, kvc_ref, kvp_ref, rep_ref, bias_ref, maskq_ref, maskk_ref, o_ref):
        maskq, maskk = maskq_ref[...], maskk_ref[...]
        lane_head = lax.broadcasted_iota(jnp.int32, (BLOCK, W), 1) >> 6
        k4all, v4all = _bands(kvc_ref, kvp_ref, rep_ref)
        for kh in range(N_KV):
            cols = slice(kh * W, (kh + 1) * W)
            _, e, es = _scores(q_ref[:, cols], k4all[:, cols], bias_ref[kh], _sink_col(sink_ref, kh), maskq)
            vbd = jnp.concatenate([_stack4(v4all[:, cols]) * maskk, maskk], axis=1)
            nd = _mm(_rows_to_lanes(e.astype(BF)), vbd)
            es_l = jnp.broadcast_to(es[(GQA - 1) * BLOCK:], (BLOCK, W))
            for g in range(GQA - 1):
                es_l = jnp.where(lane_head == g, es[g * BLOCK:(g + 1) * BLOCK], es_l)
            o_ref[:, cols] = (nd[:, :W] / (nd[:, W:] + es_l)).astype(BF)

    return pl.pallas_call(
        body, name="attn_fwd", grid=(nb,),
        in_specs=[pl.BlockSpec(memory_space=pltpu.SMEM), _row_spec(BLOCK, D), _row_spec(BLOCK, 2 * W),
                  pl.BlockSpec((BLOCK, 2 * W), lambda i: (jnp.maximum(i - 1, 0), 0)), _const_spec((W, N_KV * W)),
                  pl.BlockSpec((None, N_KV, GQA * BLOCK, 2 * BLOCK), lambda i: (jnp.minimum(i, 1), 0, 0, 0)),
                  _const_spec((GQA * BLOCK, W)), _const_spec((GQA * 2 * BLOCK, W))],
        out_specs=_row_spec(BLOCK, D), out_shape=jax.ShapeDtypeStruct((S, D), BF),
        compiler_params=_params(),
    )(sinks, q, kv, kv, consts["rep"], consts["bias"], consts["maskq"], consts["maskk"])


def _attn_bwd(q, kv, att, do, sinks, consts):
    S = q.shape[0]
    nb = S // BLOCK
    W = N_KV * HD

    def qmap(i):
        return (jnp.minimum(i, nb - 1), 0)

    def body(sink_ref, q_ref, kvc_ref, kvp_ref, att_ref, do_ref, rep_ref, bias_ref, maskq_ref, maskk_ref, hsum_ref,
             dq_ref, dkv_ref, dsink_ref, carry, sink_acc):
        i = pl.program_id(0)

        @pl.when(i == 0)
        def _():
            sink_acc[...] = jnp.zeros_like(sink_acc)
            carry[...] = jnp.zeros_like(carry)

        @pl.when(i < nb)
        def _():
            maskq, maskk, head_sum = maskq_ref[...], maskk_ref[...], hsum_ref[...]
            ones_k = jnp.ones((2 * BLOCK, BLOCK), BF)
            k4all, v4all = _bands(kvc_ref, kvp_ref, rep_ref)
            dk4s, dv4s = [], []
            for kh in range(N_KV):
                cols = slice(kh * W, (kh + 1) * W)
                k4, v4 = k4all[:, cols], v4all[:, cols]
                qs, e, es = _scores(q_ref[:, cols], k4, bias_ref[kh], _sink_col(sink_ref, kh), maskq)
                inv = 1.0 / (_mm(e.astype(BF), ones_k) + es)
                p = e * jnp.concatenate([inv, inv], axis=1)
                dog = do_ref[:, cols]
                dos = _stack4(dog) * maskq
                dp = _mm_nt(dos, v4)
                prod = (dog.astype(F32) * att_ref[:, cols].astype(F32)).astype(BF)
                delta = _lanes_to_rows(_mm(prod, head_sum), BLOCK)
                ds = (p * (dp - jnp.concatenate([delta, delta], axis=1))).astype(BF)
                sink_acc[kh] += es * inv * delta
                kbd = _stack4(k4) * maskk
                dq_ref[:, cols] = (_mm(_rows_to_lanes(ds), kbd) * (HD ** -0.5)).astype(BF)
                dk4s.append((_mm_tn(ds, qs) * (HD ** -0.5)).astype(BF))
                dv4s.append(_mm_tn(p.astype(BF), dos).astype(BF))
            dkband = _mm_nt(jnp.concatenate(dk4s, axis=1), rep_ref[...])
            dvband = _mm_nt(jnp.concatenate(dv4s, axis=1), rep_ref[...])
            dkv_ref[:, :W] = (carry[:, :W] + dkband[:BLOCK]).astype(BF)
            dkv_ref[:, W:] = (carry[:, W:] + dvband[:BLOCK]).astype(BF)
            carry[:, :W] = dkband[BLOCK:]
            carry[:, W:] = dvband[BLOCK:]

        @pl.when(i == nb)
        def _():
            dkv_ref[...] = carry[...].astype(BF)
            for hq in range(N_Q):
                kh, g = divmod(hq, GQA)
                dsink_ref[hq:hq + 1, :] = -jnp.sum(sink_acc[kh, g * BLOCK:(g + 1) * BLOCK, :], axis=0, keepdims=True)

    return pl.pallas_call(
        body, name="attn_bwd", grid=(nb + 1,),
        in_specs=[pl.BlockSpec(memory_space=pltpu.SMEM), pl.BlockSpec((BLOCK, D), qmap),
                  pl.BlockSpec((BLOCK, 2 * W), qmap),
                  pl.BlockSpec((BLOCK, 2 * W), lambda i: (jnp.maximum(jnp.minimum(i, nb - 1) - 1, 0), 0)),
                  pl.BlockSpec((BLOCK, D), qmap), pl.BlockSpec((BLOCK, D), qmap), _const_spec((W, N_KV * W)),
                  pl.BlockSpec((None, N_KV, GQA * BLOCK, 2 * BLOCK), lambda i: (jnp.minimum(i, 1), 0, 0, 0)),
                  _const_spec((GQA * BLOCK, W)), _const_spec((GQA * 2 * BLOCK, W)), _const_spec((W, GQA * BLOCK))],
        out_specs=[pl.BlockSpec((BLOCK, D), qmap),
                   pl.BlockSpec((BLOCK, 2 * W), lambda i: (jnp.maximum(i - 1, 0), 0)),
                   _const_spec((N_Q, BLOCK))],
        out_shape=[jax.ShapeDtypeStruct((S, D), BF), jax.ShapeDtypeStruct((S, 2 * W), BF),
                   jax.ShapeDtypeStruct((N_Q, BLOCK), F32)],
        scratch_shapes=[pltpu.VMEM((BLOCK, 2 * W), F32), pltpu.VMEM((N_KV, GQA * BLOCK, BLOCK), F32)],
        compiler_params=_params(),
    )(sinks, q, kv, kv, att, do, consts["rep"], consts["bias"], consts["maskq"], consts["maskk"], consts["head_sum"])


def _head(h, g, target, tm):
    S = h.shape[0]

    def body(h_ref, g_ref, t_ref, dh_ref, loss_ref, dg_ref):
        i = pl.program_id(0)

        @pl.when(i == 0)
        def _():
            loss_ref[...] = jnp.zeros_like(loss_ref)
            dg_ref[...] = jnp.zeros_like(dg_ref)

        gain = g_ref[...]
        y, xhat, r = _rms_fwd(h_ref[...], gain)
        err = y - t_ref[...]
        tile = 0.5 * jnp.sum(jnp.sum(err * err, axis=-1, keepdims=True) * (1.0 / D), axis=0, keepdims=True)
        loss_ref[...] += jnp.broadcast_to(tile, loss_ref.shape)
        dx, dg = _rms_bwd(err * (1.0 / D), xhat, r, gain)
        dg_ref[...] += dg
        dh_ref[...] = dx

    return pl.pallas_call(
        body, name="loss_head", grid=(S // tm,),
        in_specs=[_row_spec(tm, D), _const_spec((1, D)), _row_spec(tm, D)],
        out_specs=[_row_spec(tm, D), _const_spec((8, 128)), _const_spec((1, D))],
        out_shape=[jax.ShapeDtypeStruct((S, D), F32), jax.ShapeDtypeStruct((8, 128), F32),
                   jax.ShapeDtypeStruct((1, D), F32)],
        compiler_params=_params(),
    )(h, g, target)


def _wgrad(a, b, name, *, tn, out_blocked=False, a_transposed=False, ts=2048, per_step=1):
    a_blocked, b_blocked = a.ndim == 3, b.ndim == 3
    na = a.shape[0] if a_blocked else 1
    S = b.shape[-2]
    K = a.shape[0] if a_transposed else a.shape[-1]
    wide = tn * per_step
    nbk = b.shape[0] if b_blocked else b.shape[1] // wide
    ts = min(ts, S)
    ns = S // ts
    assert not (a_blocked and (out_blocked or a_transposed)) and (per_step == 1 or (out_blocked and not b_blocked))

    def body(a_ref, b_ref, o_ref, acc):
        s = pl.program_id(2)

        @pl.when(s == 0)
        def _():
            acc[...] = jnp.zeros_like(acc)

        if a_transposed:
            acc[...] += _mm(a_ref[...], b_ref[...])
        else:
            acc[...] += _mm_tn(a_ref[...], b_ref[...])

        @pl.when(s == ns - 1)
        def _():
            if per_step == 1:
                o_ref[...] = acc[...].astype(o_ref.dtype)
            else:
                for k in range(per_step):
                    o_ref[k] = acc[:, k * tn:(k + 1) * tn].astype(o_ref.dtype)

    if a_transposed:
        a_spec = pl.BlockSpec((K, ts), lambda i, j, s: (0, s))
    elif a_blocked:
        a_spec = pl.BlockSpec((None, ts, K), lambda i, j, s: (i, s, 0))
    else:
        a_spec = pl.BlockSpec((ts, K), lambda i, j, s: (s, 0))
    b_spec = (pl.BlockSpec((None, ts, tn), lambda i, j, s: (j, s, 0)) if b_blocked
              else pl.BlockSpec((ts, wide), lambda i, j, s: (s, j)))
    if out_blocked:
        o_spec = pl.BlockSpec((None, K, tn) if per_step == 1 else (per_step, K, tn), lambda i, j, s: (j, 0, 0))
        o_shape = jax.ShapeDtypeStruct((nbk * per_step, K, tn), BF)
    else:
        o_spec = pl.BlockSpec((K, wide), lambda i, j, s: (i, j))
        o_shape = jax.ShapeDtypeStruct((na * K, nbk * wide), BF)
    return pl.pallas_call(
        body, name=name, grid=(na, nbk, ns), in_specs=[a_spec, b_spec], out_specs=o_spec, out_shape=o_shape,
        scratch_shapes=[pltpu.VMEM((K, wide), F32)],
        compiler_params=pltpu.CompilerParams(dimension_semantics=("arbitrary", "arbitrary", "arbitrary"),
                                             vmem_limit_bytes=VMEM_BIG),
    )(a, b)


def _adamw_math(w, g, m, v):
    m = B1 * m + (1.0 - B1) * g
    v = B2 * v + (1.0 - B2) * (g * g)
    m_hat = m / (1.0 - B1 ** STEP)
    v_hat = v / (1.0 - B2 ** STEP)
    return -LR * (m_hat / (jnp.sqrt(v_hat) + ADAM_EPS) + WD * w), m, v


def _sum8(parts, name):
    R = parts.shape[1]

    def body(p_ref, o_ref):
        g = p_ref[0]
        for s in range(1, N_DEV):
            g = g + p_ref[s]
        o_ref[...] = g

    return pl.pallas_call(
        body, name=name, grid=(1,),
        in_specs=[pl.BlockSpec((N_DEV, R, ROW), lambda i: (0, 0, 0))],
        out_specs=pl.BlockSpec((R, ROW), lambda i: (0, 0)), out_shape=jax.ShapeDtypeStruct((R, ROW), F32),
        compiler_params=_params(),
    )(parts)


def _adamw_small(g, w, m, v, name):
    R = w.shape[0]

    def body(g_ref, w_ref, m_ref, v_ref, d_ref, nm_ref, nv_ref):
        d_ref[...], nm_ref[...], nv_ref[...] = _adamw_math(w_ref[...], g_ref[...], m_ref[...], v_ref[...])

    spec = pl.BlockSpec((R, ROW), lambda i: (0, 0))
    return pl.pallas_call(
        body, name=name, grid=(1,), in_specs=[spec] * 4, out_specs=[spec] * 3,
        out_shape=[jax.ShapeDtypeStruct((R, ROW), F32)] * 3, compiler_params=_params(),
    )(g, w, m, v)


def _mesh_pos():
    return lax.axis_index("x"), lax.axis_index("y"), lax.axis_index("c")


def _flip(v, bit):
    return 1 - v if bit else v


def _peers(x, y, c):
    out = []
    for k in range(1, N_DEV):
        px, py, pc = _flip(x, k & 4), _flip(y, k & 2), _flip(c, k & 1)
        out.append((k - 1, (px, py, pc), 4 * px + 2 * py + pc))
    return out


def _all_gather_now(xs, name):
    n = len(xs)
    MESH = pl.DeviceIdType.MESH

    def body(*refs):
        x_refs, out_refs = refs[:n], refs[n:2 * n]
        send_sems, recv_sems, local_sems = refs[2 * n:]
        x, y, c = _mesh_pos()
        me, sibling = (x, y, c), (x, y, 1 - c)
        chips = [(1 - x, y), (x, 1 - y), (1 - x, 1 - y)]
        blk = lambda p: 4 * p[0] + 2 * p[1] + p[2]

        def copy(a, k, block, to, src=None):
            return pltpu.make_async_remote_copy(
                src_ref=out_refs[a].at[blk(block)] if src is None else src, dst_ref=out_refs[a].at[blk(block)],
                send_sem=send_sems.at[a, k], recv_sem=recv_sems.at[a, k], device_id=to, device_id_type=MESH)

        sent, own = [], []
        for a in range(n):
            own.append(pltpu.make_async_copy(x_refs[a], out_refs[a].at[blk(me)], local_sems.at[a]))
            own[-1].start()
            first = [copy(a, 1 + j, me, (*chip, c), src=x_refs[a]) for j, chip in enumerate(chips)]
            first.append(copy(a, 0, me, sibling, src=x_refs[a]))
            for cp in first:
                cp.start()
            sent += first
        for a in range(n):
            for j, chip in enumerate(chips):
                copy(a, 1 + j, (*chip, c), me).wait_recv()
                sent.append(copy(a, 4 + j, (*chip, c), sibling))
                sent[-1].start()
        for a in range(n):
            copy(a, 0, sibling, me).wait_recv()
            for j, chip in enumerate(chips):
                copy(a, 4 + j, (*chip, 1 - c), me).wait_recv()
        for cp in sent:
            cp.wait_send()
        for cp in own:
            cp.wait()

    hbm = pl.BlockSpec(memory_space=pl.ANY)
    return pl.pallas_call(
        body, name=name, out_shape=[jax.ShapeDtypeStruct((N_DEV,) + x.shape, x.dtype) for x in xs],
        in_specs=[hbm] * n, out_specs=[hbm] * n,
        scratch_shapes=[pltpu.SemaphoreType.DMA((n, 7)), pltpu.SemaphoreType.DMA((n, 7)), pltpu.SemaphoreType.DMA((n,))],
    )(*xs)


def _send_start(srcs, blocked, name):
    n = len(srcs)
    MESH = pl.DeviceIdType.MESH

    def body(*refs):
        src_refs, land_refs = refs[:n], refs[n:2 * n]
        send_sems, recv_sems, local_sems = refs[2 * n:3 * n], refs[3 * n:4 * n], refs[4 * n:5 * n]
        token = refs[7 * n]
        x, y, c = _mesh_pos()
        me = 4 * x + 2 * y + c
        for a in range(n):
            for k, peer, blk in _peers(x, y, c):
                pltpu.make_async_remote_copy(
                    src_ref=src_refs[a].at[blk] if blocked else src_refs[a], dst_ref=land_refs[a].at[me],
                    send_sem=send_sems[a].at[k], recv_sem=recv_sems[a].at[k], device_id=peer,
                    device_id_type=MESH).start()
            pltpu.make_async_copy(src_refs[a].at[me] if blocked else src_refs[a], land_refs[a].at[me],
                                  local_sems[a]).start()
        token[...] = jnp.zeros_like(token)

    hbm = pl.BlockSpec(memory_space=pltpu.HBM)
    sem = pl.BlockSpec(memory_space=pltpu.SEMAPHORE)
    lshape = [s.shape if blocked else (N_DEV,) + s.shape for s in srcs]
    outs = pl.pallas_call(
        body, name=name,
        out_shape=([pltpu.SemaphoreType.DMA((N_DEV - 1,))] * (2 * n) + [pltpu.SemaphoreType.DMA(())] * n
                   + [pltpu.HBM(s.shape, s.dtype) for s in srcs]
                   + [pltpu.HBM(ls, s.dtype) for ls, s in zip(lshape, srcs)] + [jax.ShapeDtypeStruct((8, 128), F32)]),
        in_specs=[hbm] * (2 * n), out_specs=[sem] * (3 * n) + [hbm] * (2 * n) + [pl.BlockSpec(memory_space=pltpu.VMEM)],
        input_output_aliases={**{a: 3 * n + a for a in range(n)}, **{n + a: 4 * n + a for a in range(n)}},
        compiler_params=pltpu.CompilerParams(has_side_effects=pltpu.SideEffectType.DATAFLOW_SIDE_EFFECTING),
    )(*[pltpu.with_memory_space_constraint(s, pltpu.HBM) for s in srcs],
      *[pltpu.with_memory_space_constraint(lax.empty(ls, s.dtype), pltpu.HBM) for ls, s in zip(lshape, srcs)])
    return dict(send=outs[:n], recv=outs[n:2 * n], local=outs[2 * n:3 * n], srcs=outs[3 * n:4 * n],
                lands=outs[4 * n:5 * n], token=outs[5 * n], blocked=blocked)


def _send_wait(started, which, after, name):
    n = len(which)
    blocked = started["blocked"]
    pick = lambda key: [started[key][a] for a in which]
    MESH = pl.DeviceIdType.MESH

    def body(*refs):
        src_refs, land_refs = refs[:n], refs[n:2 * n]
        send, recv, local = refs[2 * n:3 * n], refs[3 * n:4 * n], refs[4 * n:5 * n]
        x, y, c = _mesh_pos()
        me = 4 * x + 2 * y + c
        for a in range(n):
            for k, peer, blk in _peers(x, y, c):
                cp = pltpu.make_async_remote_copy(
                    src_ref=src_refs[a].at[blk] if blocked else src_refs[a], dst_ref=land_refs[a].at[blk],
                    send_sem=send[a].at[k], recv_sem=recv[a].at[k], device_id=peer, device_id_type=MESH)
                cp.wait_send()
                cp.wait_recv()
            pltpu.make_async_copy(src_refs[a].at[me] if blocked else src_refs[a], land_refs[a].at[me],
                                  local[a]).wait()

    hbm = pl.BlockSpec(memory_space=pltpu.HBM)
    sem = pl.BlockSpec(memory_space=pltpu.SEMAPHORE)
    srcs, lands = pick("srcs"), pick("lands")
    outs = pl.pallas_call(
        body, name=name, out_shape=[pltpu.HBM(s.shape, s.dtype) for s in srcs + lands],
        in_specs=[hbm] * (2 * n) + [sem] * (3 * n) + [pl.BlockSpec(memory_space=pl.ANY)], out_specs=[hbm] * (2 * n),
        input_output_aliases={a: a for a in range(2 * n)},
        compiler_params=pltpu.CompilerParams(has_side_effects=pltpu.SideEffectType.DATAFLOW_SIDE_EFFECTING),
    )(*srcs, *lands, *pick("send"), *pick("recv"), *pick("local"), after)
    return outs[n:]


def _adamw_weight(parts, w, m, v, name):
    L, R, C = w.shape
    tr = max(t for t in range(16, min(R, 256) + 1, 16) if R % t == 0)
    nr = R // tr

    def body(*refs):
        p_refs = refs[:L]
        w_ref, m_ref, v_ref, g_ref, d_ref, nm_ref, nv_ref, gsum = refs[L:]
        layer = pl.program_id(0)
        for l in range(L):
            @pl.when(layer == l)
            def _(l=l):
                g = p_refs[l][0].astype(F32)
                for s in range(1, N_DEV):
                    g = g + p_refs[l][s].astype(F32)
                gsum[...] = g
        g = gsum[...]
        g_ref[...] = g
        d_ref[...], nm_ref[...], nv_ref[...] = _adamw_math(w_ref[...], g, m_ref[...], v_ref[...])

    def part_spec(l):
        return pl.BlockSpec((N_DEV, tr, C), lambda layer, i: (0, jnp.where(layer == l, i, jnp.where(layer < l, 0, nr - 1)), 0))

    ws = pl.BlockSpec((None, tr, C), lambda layer, i: (layer, i, 0))
    return pl.pallas_call(
        body, name=name, grid=(L, nr), in_specs=[part_spec(l) for l in range(L)] + [ws] * 3, out_specs=[ws] * 4,
        out_shape=[jax.ShapeDtypeStruct((L, R, C), F32)] * 4, scratch_shapes=[pltpu.VMEM((tr, C), F32)],
        compiler_params=pltpu.CompilerParams(dimension_semantics=("arbitrary", "arbitrary")),
    )(*parts, w, m, v)


def _seg_rows(n, align):
    rows = -(-n // ROW)
    return -(-rows // align) * align


def _pack(arrs, align, lead=()):
    parts = []
    for a in arrs:
        n = int(np.prod(a.shape[len(lead):]))
        rows = _seg_rows(n, align)
        flat = a.reshape(lead + (n,))
        flat = jnp.pad(flat, [(0, 0)] * len(lead) + [(0, rows * ROW - n)])
        parts.append(flat.reshape(lead + (rows, ROW)))
    return jnp.concatenate(parts, axis=len(lead))


def _unpack(buf, shapes, align, lead=()):
    out, r0 = [], 0
    for shp in shapes:
        n = int(np.prod(shp))
        rows = _seg_rows(n, align)
        seg = lax.slice_in_dim(buf, r0, r0 + rows, axis=len(lead))
        out.append(seg.reshape(lead + (rows * ROW,))[..., :n].reshape(lead + tuple(shp)))
        r0 += rows
    return out


def kernel(x, p, norm_mix, norm_ffn, norm_ple, norm_kv, norm_final, a_w_in, a_norm_v, a_w_s, a_b_s, a_w_out, w_kv, b_w_q, b_sinks, b_w_o, f_w_up, f_conv_w, f_conv_b, f_w_down, ple_w_in, ple_w_gate, ple_b_gate, loss_target, m_norm_mix, m_norm_ffn, m_norm_ple, m_norm_kv, m_norm_final, m_a_w_in, m_a_norm_v, m_a_w_s, m_a_b_s, m_a_w_out, m_w_kv, m_b_w_q, m_b_sinks, m_b_w_o, m_f_w_up, m_f_conv_w, m_f_conv_b, m_f_w_down, m_ple_w_in, m_ple_w_gate, m_ple_b_gate, v_norm_mix, v_norm_ffn, v_norm_ple, v_norm_kv, v_norm_final, v_a_w_in, v_a_norm_v, v_a_w_s, v_a_b_s, v_a_w_out, v_w_kv, v_b_w_q, v_b_sinks, v_b_w_o, v_f_w_up, v_f_conv_w, v_f_conv_b, v_f_w_down, v_ple_w_in, v_ple_w_gate, v_ple_b_gate):
    w = dict(norm_mix=norm_mix, norm_ffn=norm_ffn, norm_ple=norm_ple, norm_kv=norm_kv, norm_final=norm_final,
             a_w_in=a_w_in, a_norm_v=a_norm_v, a_w_s=a_w_s, a_b_s=a_b_s, a_w_out=a_w_out, w_kv=w_kv, b_w_q=b_w_q,
             b_sinks=b_sinks, b_w_o=b_w_o, f_w_up=f_w_up, f_conv_w=f_conv_w, f_conv_b=f_conv_b, f_w_down=f_w_down,
             ple_w_in=ple_w_in, ple_w_gate=ple_w_gate, ple_b_gate=ple_b_gate)
    mom_m = dict(norm_mix=m_norm_mix, norm_ffn=m_norm_ffn, norm_ple=m_norm_ple, norm_kv=m_norm_kv,
                 norm_final=m_norm_final, a_w_in=m_a_w_in, a_norm_v=m_a_norm_v, a_w_s=m_a_w_s, a_b_s=m_a_b_s,
                 a_w_out=m_a_w_out, w_kv=m_w_kv, b_w_q=m_b_w_q, b_sinks=m_b_sinks, b_w_o=m_b_w_o, f_w_up=m_f_w_up,
                 f_conv_w=m_f_conv_w, f_conv_b=m_f_conv_b, f_w_down=m_f_w_down, ple_w_in=m_ple_w_in,
                 ple_w_gate=m_ple_w_gate, ple_b_gate=m_ple_b_gate)
    mom_v = dict(norm_mix=v_norm_mix, norm_ffn=v_norm_ffn, norm_ple=v_norm_ple, norm_kv=v_norm_kv,
                 norm_final=v_norm_final, a_w_in=v_a_w_in, a_norm_v=v_a_norm_v, a_w_s=v_a_w_s, a_b_s=v_a_b_s,
                 a_w_out=v_a_w_out, w_kv=v_w_kv, b_w_q=v_b_w_q, b_sinks=v_b_sinks, b_w_o=v_b_w_o, f_w_up=v_f_w_up,
                 f_conv_w=v_f_conv_w, f_conv_b=v_f_conv_b, f_w_down=v_f_w_down, ple_w_in=v_ple_w_in,
                 ple_w_gate=v_ple_w_gate, ple_b_gate=v_ple_b_gate)
    S = x.shape[1]
    tm = min(256, S)
    tl = min(512, S)
    me = 4 * lax.axis_index("x") + 2 * lax.axis_index("y") + lax.axis_index("c")

    def after_token(tok, a):
        return a + tok[0:1, 0:1].reshape((1,) * a.ndim)

    wb = {n: w[n].astype(BF) for n in _BIG}
    order = [('a_w_in', 0), ('a_w_out', 0), ('a_norm_v', None), ('f_conv_w', 0), ('f_conv_w', 1),
             ('f_w_up', 0), ('f_w_down', 0),
             ('ple_w_in', 0), ('ple_w_gate', 0), ('w_kv', None), ('b_w_q', 0), ('b_w_o', 0),
             ('f_w_up', 1), ('f_w_down', 1), ('ple_w_in', 1), ('ple_w_gate', 1)]
    src = lambda n, l: (wb[n] if n in wb else w[n]) if l is None else (wb[n] if n in wb else w[n])[l]
    groups = [order[:5], order[5:7], order[7:12], order[12:]]
    g = dict(zip(groups[0], _all_gather_now([src(n, l) for n, l in groups[0]], "gather_mix")))

    def start_group(keys, tie_to, name):
        srcs = [src(n, l) for n, l in keys]
        k = min(range(len(srcs)), key=lambda i: srcs[i].size)
        zero = (tie_to[(0,) * (tie_to.ndim - 1)][0:1] * 0).astype(srcs[k].dtype)
        srcs[k] = srcs[k] + zero.reshape((1,) * srcs[k].ndim)
        return _send_start(srcs, False, name)

    def finish_group(started, keys, after, name):
        return dict(zip(keys, _send_wait(started, list(range(len(keys))), after, name)))

    x0 = x[0]
    win = g[('a_w_in', 0)].transpose(1, 0, 2).reshape(D, 2 * AW)
    wout = g[('a_w_out', 0)].reshape(AW, D)
    gv = g[('a_norm_v', None)].reshape(1, AW)
    cw = [g[('f_conv_w', l)] for l in range(2)]
    cb = [w['f_conv_b'][l].reshape(N_DEV, 1, FF_BLK) for l in range(2)]
    ws = a_w_s[0].astype(BF)
    wst = jnp.swapaxes(ws, 1, 2)
    bsb = jnp.broadcast_to(a_b_s[0][:, :, None], (GROUPS, CHUNK, CHUNK))
    row = lambda a: a.reshape(1, -1)
    pb = [p[l, 0].astype(BF) for l in range(2)]
    sinks = b_sinks.reshape(N_Q)
    consts = _attn_consts()

    s1 = start_group(groups[1], g[('a_w_out', 0)], "gather_start_ffn0")
    h1, zg0, gp0 = _mixa_fwd(x0, after_token(s1["token"], row(norm_mix[0])), win, gv, ws, bsb, wout, tm)
    g = finish_group(s1, groups[1], h1, "gather_wait_ffn0")
    wup, wdown = [g[('f_w_up', 0)], None], [g[('f_w_down', 0)].reshape(D_FF, D), None]
    s2 = start_group(groups[2], g[('f_w_down', 0)], "gather_start_mid")
    h2, hh0, cc0 = _ffn_fwd(h1, after_token(s2["token"], row(norm_ffn[0])), wup[0], cw[0], cb[0], wdown[0], tm)
    g = finish_group(s2, groups[2], h2, "gather_wait_mid")
    wple = [g[('ple_w_in', 0)].transpose(1, 0, 2).reshape(PLE, D), None]
    wgate = [g[('ple_w_gate', 0)].reshape(D, D), None]
    wkv, wq, wo = g[('w_kv', None)].reshape(D, 2 * N_KV * HD), g[('b_w_q', 0)].reshape(D, D), g[('b_w_o', 0)].reshape(D, D)
    s3 = start_group(groups[3], g[('ple_w_in', 0)], "gather_start_l1")
    h3 = _ple_fwd(h2, after_token(s3["token"], row(norm_ple[0])), wgate[0], row(ple_b_gate[0]), pb[0], wple[0], tl)
    q, kv = _qkv_fwd(h3, row(norm_mix[1]), wq, row(norm_kv), wkv, tl)
    att = _attn_fwd(q, kv, sinks, consts)
    h4 = _oproj_fwd(att, wo, h3, tl)
    g = finish_group(s3, groups[3], h4, "gather_wait_l1")
    wup[1], wdown[1] = g[('f_w_up', 1)], g[('f_w_down', 1)].reshape(D_FF, D)
    wple[1], wgate[1] = g[('ple_w_in', 1)].transpose(1, 0, 2).reshape(PLE, D), g[('ple_w_gate', 1)].reshape(D, D)
    h5, hh1, cc1 = _ffn_fwd(h4, row(norm_ffn[1]), wup[1], cw[1], cb[1], wdown[1], tm)
    h6 = _ple_fwd(h5, row(norm_ple[1]), wgate[1], row(ple_b_gate[1]), pb[1], wple[1], tl)

    def col_blocks(g2d, n):
        return g2d.reshape(g2d.shape[0], N_DEV, n).transpose(1, 0, 2)

    def row_blocks(g2d):
        return g2d.reshape(N_DEV, g2d.shape[0] // N_DEV, g2d.shape[1])

    def ple_grads(l, de, xp, dpre):
        return [col_blocks(_wgrad(pb[l], de, f"wgrad_ple_in{l}", tn=1024), D // N_DEV),
                row_blocks(_wgrad(xp, dpre, f"wgrad_gate{l}", tn=1024))]

    def ffn_grads(l, xf, dhh, act, dhb):
        return [_wgrad(xf, dhh, f"wgrad_up{l}", tn=FF_BLK, out_blocked=True, a_transposed=True),
                row_blocks(_wgrad(act, dhb, f"wgrad_down{l}", tn=1024))]

    dh6, loss_blk, d_norm_final = _head(h6, row(norm_final), loss_target[0], tl)
    dh5, de1, xp1, dpre1, dbg1, dgp1 = _ple_bwd(dh6, h5, row(norm_ple[1]), wgate[1], row(ple_b_gate[1]), pb[1],
                                                 wple[1], tl)
    ex_ple1 = _send_start(ple_grads(1, de1, xp1, dpre1), True, "send_ple1")
    dh4, xf1, dhh1, act1, dhb5, dcw1, dcb1, dgf1 = _ffn_bwd(dh5, h4, hh1, cc1, after_token(ex_ple1["token"], row(norm_ffn[1])),
                                                            wup[1], cw[1], wdown[1], tm)
    ex_ffn1 = _send_start(ffn_grads(1, xf1, dhh1, act1, dhb5), True, "send_ffn1")
    datt, dhb4 = _oproj_bwd(dh4, wo, tl)
    dq, dkv, dsink = _attn_bwd(q, kv, att, datt, after_token(ex_ffn1["token"], sinks.reshape(1, N_Q)).reshape(N_Q), consts)
    dh3, xq, xkv, dgq, dgkv = _qkv_bwd(dh4, h3, dq, dkv, row(norm_mix[1]), wq, row(norm_kv), wkv, tl)
    ex_att = _send_start([row_blocks(_wgrad(att, dhb4, "wgrad_o", tn=1024)),
                              row_blocks(_wgrad(xq, dq, "wgrad_q", tn=1024)),
                              row_blocks(_wgrad(xkv, dkv, "wgrad_kv", tn=512))], True, "send_att")
    dh2, de0, xp0, dpre0, dbg0, dgp0 = _ple_bwd(dh3, h2, after_token(ex_att["token"], row(norm_ple[0])), wgate[0],
                                                 row(ple_b_gate[0]), pb[0], wple[0], tl)
    ex_ple0 = _send_start(ple_grads(0, de0, xp0, dpre0), True, "send_ple0")
    dh1, xf0, dhh0, act0, dhb2, dcw0, dcb0, dgf0 = _ffn_bwd(dh2, h1, hh0, cc0, after_token(ex_ple0["token"], row(norm_ffn[0])),
                                                            wup[0], cw[0], wdown[0], tm)
    ex_ffn0 = _send_start(ffn_grads(0, xf0, dhh0, act0, dhb2), True, "send_ffn0")
    dx, xa, dz, gated, dhb1, dws, dbs, dgv, dga = _mixa_bwd(dh1, x0, zg0, gp0, after_token(ex_ffn0["token"], row(norm_mix[0])), win, gv,
                                                            ws, wst, bsb, wout, tm)
    small_g = {
        'norm_mix': jnp.concatenate([dga, dgq], axis=0), 'norm_ffn': jnp.concatenate([dgf0, dgf1], axis=0),
        'norm_ple': jnp.concatenate([dgp0, dgp1], axis=0), 'norm_kv': dgkv.reshape(D),
        'norm_final': d_norm_final.reshape(D), 'a_w_s': dws[None], 'a_b_s': dbs[None, :, :, 0],
        'b_sinks': dsink[:, 0].reshape(1, N_Q),
        'f_conv_b': jnp.stack([dcb0.reshape(2 * D_FF), dcb1.reshape(2 * D_FF)]),
        'ple_b_gate': jnp.concatenate([dbg0, dbg1], axis=0),
        'a_norm_v': dgv, 'f_conv_w': jnp.stack([dcw0, dcw1], axis=1),
    }
    names_all = _REPL + _SMALL_SHARDED
    full_shapes = [small_g[n].shape for n in names_all]
    small_packed = _pack([small_g[n] for n in names_all], 8)
    ex_small = _send_start([small_packed.reshape(N_DEV, small_packed.shape[0] // N_DEV, ROW)], True, "send_small")
    ex_a_in = _send_start([_wgrad(xa, dz, "wgrad_a_in", tn=2 * AW // N_DEV, out_blocked=True, a_transposed=True,
                                  per_step=4)], True, "send_a_in")
    ex_a_out = _send_start([row_blocks(_wgrad(gated, dhb1, "wgrad_a_out", tn=1024))],
                           True, "send_a_out")
    small_sum = _sum8(_send_wait(ex_small, [0], ex_a_out["token"], "recv_small")[0], "sum_small_grads")
    ex_small_all = _send_start([small_sum], False, "gather_small_sums")

    def as3(a):
        return a.reshape((-1,) + a.shape[-2:])

    def adam(name, parts):
        outs = _adamw_weight(parts, as3(w[name]), as3(mom_m[name]), as3(mom_v[name]), "adamw_" + name)
        return [o.reshape(w[name].shape) for o in outs]

    big = {}
    l_ple1 = _send_wait(ex_ple1, list(range(len(ex_ple1["lands"]))), ex_small_all["token"], "recv_ple1")
    l_ffn1 = _send_wait(ex_ffn1, list(range(len(ex_ffn1["lands"]))), l_ple1[0], "recv_ffn1")
    l_att = _send_wait(ex_att, list(range(len(ex_att["lands"]))), l_ffn1[0], "recv_att")
    big['b_w_o'], big['b_w_q'], big['w_kv'] = adam('b_w_o', [l_att[0]]), adam('b_w_q', [l_att[1]]), adam('w_kv', [l_att[2]])
    l_ple0 = _send_wait(ex_ple0, list(range(len(ex_ple0["lands"]))), big['w_kv'][0], "recv_ple0")
    big['ple_w_in'] = adam('ple_w_in', [l_ple0[0], l_ple1[0]])
    big['ple_w_gate'] = adam('ple_w_gate', [l_ple0[1], l_ple1[1]])
    l_ffn0 = _send_wait(ex_ffn0, list(range(len(ex_ffn0["lands"]))), big['ple_w_gate'][0], "recv_ffn0")
    big['f_w_up'] = adam('f_w_up', [l_ffn0[0], l_ffn1[0]])
    big['f_w_down'] = adam('f_w_down', [l_ffn0[1], l_ffn1[1]])
    big['a_w_in'] = adam('a_w_in', _send_wait(ex_a_in, [0], big['f_w_down'][0], "recv_a_in"))
    big['a_w_out'] = adam('a_w_out', _send_wait(ex_a_out, [0], big['a_w_in'][0], "recv_a_out"))
    g_big, d_big, m_big, v_big = [{n: big[n][i] for n in _BIG} for i in range(4)]

    gsum = _send_wait(ex_small_all, [0], big['a_w_out'][0], "recv_small_sums")[0].reshape(small_packed.shape)
    gs = dict(zip(names_all, _unpack(gsum, full_shapes, 8)))
    gs['a_norm_v'] = lax.dynamic_slice_in_dim(gs['a_norm_v'], me * (AW // N_DEV), AW // N_DEV, axis=1)
    gs['f_conv_w'] = lax.dynamic_index_in_dim(gs['f_conv_w'], me, axis=0, keepdims=False)
    gs = {n: gs[n].reshape(w[n].shape) for n in names_all}
    packs = lambda d: _pack([d[n] for n in names_all], 8)
    souts = _adamw_small(packs(gs), packs(w), packs(mom_m), packs(mom_v), "adamw_small")
    shp = [w[n].shape for n in names_all]
    d_s, m_s, v_s = [dict(zip(names_all, _unpack(o, shp, 8))) for o in souts]

    grads = {**g_big, **gs}
    delta = {**d_big, **d_s}
    new_m = {**m_big, **m_s}
    new_v = {**v_big, **v_s}
    loss = lax.psum(loss_blk[0, 0], ("x", "y", "c"))
    return (loss, dx[None], *[grads[n] for n in _PARAMS], *[delta[n] for n in _PARAMS],
            *[new_m[n] for n in _PARAMS], *[new_v[n] for n in _PARAMS])
```

```python
import functools

import numpy as np
import jax
import jax.numpy as jnp
from jax import lax
from jax.experimental import pallas as pl
from jax.experimental.pallas import tpu as pltpu

F32 = jnp.float32
BF = jnp.bfloat16

N_DEV = 8
D = 1024
AW = 1024
GROUPS = 8
CHUNK = 128
HD = 64
N_Q = 16
N_KV = 4
GQA = 4
BLOCK = 128
D_FF = 2816
FF_BLK = 704
N_FF_BLK = 4
PLE = 256
EPS = 1e-6
LR, B1, B2, ADAM_EPS, WD, STEP = 0.001, 0.9, 0.999, 1e-08, 0.01, 10
GELU_C0 = 0.7978845608028654
GELU_C1 = 0.044715
NEG = -1e30
ROW = 1024
VMEM_BIG = 56 * 1024 * 1024

_PARAMS = ['norm_mix', 'norm_ffn', 'norm_ple', 'norm_kv', 'norm_final', 'a_w_in', 'a_norm_v', 'a_w_s', 'a_b_s',
           'a_w_out', 'w_kv', 'b_w_q', 'b_sinks', 'b_w_o', 'f_w_up', 'f_conv_w', 'f_conv_b', 'f_w_down',
           'ple_w_in', 'ple_w_gate', 'ple_b_gate']
_BIG = ['a_w_in', 'a_w_out', 'w_kv', 'b_w_q', 'b_w_o', 'f_w_up', 'f_w_down', 'ple_w_in', 'ple_w_gate']
_SMALL_SHARDED = ['a_norm_v', 'f_conv_w']
_REPL = ['norm_mix', 'norm_ffn', 'norm_ple', 'norm_kv', 'norm_final', 'a_w_s', 'a_b_s', 'b_sinks', 'f_conv_b',
         'ple_b_gate']


def _mm(a, b):
    return jnp.dot(a, b, preferred_element_type=F32)


def _mm_nt(a, b):
    return lax.dot_general(a, b, (((1,), (1,)), ((), ())), preferred_element_type=F32)


def _mm_tn(a, b):
    return lax.dot_general(a, b, (((0,), (0,)), ((), ())), preferred_element_type=F32)


def _const_spec(shape, single=False):
    nd = len(shape)
    if single:
        return pl.BlockSpec(shape, lambda *_: (0,) * nd, pipeline_mode=pl.Buffered(1))
    return pl.BlockSpec(shape, lambda *_: (0,) * nd)


def _row_spec(tm, n):
    return pl.BlockSpec((tm, n), lambda i: (i, 0))


def _rms_fwd(x, g):
    r = lax.rsqrt(jnp.mean(x * x, axis=-1, keepdims=True) + EPS)
    xhat = x * r
    return xhat * g, xhat, r


def _rms_bwd(dy, xhat, r, g):
    a = dy * g
    dx = r * (a - xhat * jnp.mean(a * xhat, axis=-1, keepdims=True))
    return dx, jnp.sum(dy * xhat, axis=0, keepdims=True)


def _sigmoid(x):
    return 1.0 / (1.0 + jnp.exp(-x))


def _params(vmem=None):
    return pltpu.CompilerParams(dimension_semantics=("arbitrary",), vmem_limit_bytes=vmem)


def _tri_masks():
    t = lax.broadcasted_iota(jnp.int32, (CHUNK, CHUNK), 0)
    s = lax.broadcasted_iota(jnp.int32, (CHUNK, CHUNK), 1)
    return t >= s, s >= t


def _mixa_fwd(x, g, win, gv, ws, bsb, wout, tm):
    S = x.shape[0]
    nc = tm // CHUNK

    def body(x_ref, g_ref, win_ref, gv_ref, ws_ref, bsb_ref, wout_ref, h_ref, zg_ref, gp_ref, gated):
        x = x_ref[...]
        xn, _, _ = _rms_fwd(x, g_ref[...])
        z = _mm(xn.astype(BF), win_ref[...])
        z2 = z * z
        t = jnp.tanh(GELU_C0 * (z + GELU_C1 * z2 * z))
        zg = 0.5 * z * (1.0 + t)
        zg_ref[...] = zg.astype(BF)
        gp_ref[...] = (0.5 * (1.0 + t) + 0.5 * z * (1.0 - t * t) * (GELU_C0 * (1.0 + 3.0 * GELU_C1 * z2))).astype(BF)
        u = zg[:, :AW]
        vn, _, _ = _rms_fwd(zg[:, AW:], gv_ref[...])
        vn = vn.astype(BF)
        tril, _ = _tri_masks()
        for h in range(GROUPS):
            wm = jnp.where(tril, ws_ref[h], jnp.zeros((), BF))
            for c in range(nc):
                rs, cs = slice(c * CHUNK, (c + 1) * CHUNK), slice(h * CHUNK, (h + 1) * CHUNK)
                s = _mm(wm, vn[rs, cs]) + bsb_ref[h]
                gated[rs, cs] = (u[rs, cs] * s).astype(BF)
        h_ref[...] = x + _mm(gated[...], wout_ref[...])

    return pl.pallas_call(
        body, name="mixa_fwd", grid=(S // tm,),
        in_specs=[_row_spec(tm, D), _const_spec((1, D)), _const_spec((D, 2 * AW)), _const_spec((1, AW)),
                  _const_spec((GROUPS, CHUNK, CHUNK)), _const_spec((GROUPS, CHUNK, CHUNK)), _const_spec((AW, D))],
        out_specs=[_row_spec(tm, D), _row_spec(tm, 2 * AW), _row_spec(tm, 2 * AW)],
        out_shape=[jax.ShapeDtypeStruct((S, D), F32), jax.ShapeDtypeStruct((S, 2 * AW), BF),
                   jax.ShapeDtypeStruct((S, 2 * AW), BF)],
        scratch_shapes=[pltpu.VMEM((tm, AW), BF)],
        compiler_params=_params(VMEM_BIG),
    )(x, g, win, gv, ws, bsb, wout)


def _mixa_bwd(dh, x, zg, gp, g, win, gv, ws, wst, bsb, wout, tm):
    S = x.shape[0]
    nt = S // tm
    nc = tm // CHUNK

    def body(dh_ref, x_ref, zg_ref, gp_ref, g_ref, win_ref, gv_ref, ws_ref, wst_ref, bsb_ref, wout_ref,
             dx_ref, xn_ref, dz_ref, gated_ref, dhb_ref, dws_ref, dbs_ref, dgv_ref, dg_ref, du_scr, dvn_scr):
        i = pl.program_id(0)

        @pl.when(i == 0)
        def _():
            dws_ref[...] = jnp.zeros_like(dws_ref)
            dbs_ref[...] = jnp.zeros_like(dbs_ref)
            dgv_ref[...] = jnp.zeros_like(dgv_ref)
            dg_ref[...] = jnp.zeros_like(dg_ref)

        x = x_ref[...]
        gmix = g_ref[...]
        xn, xhat, r = _rms_fwd(x, gmix)
        xn_ref[...] = xn.T.astype(BF)
        zg = zg_ref[...].astype(F32)
        gp = gp_ref[...].astype(F32)
        u = zg[:, :AW]
        gvv = gv_ref[...]
        vn, vhat, rv = _rms_fwd(zg[:, AW:], gvv)
        vn = vn.astype(BF)
        dh = dh_ref[...]
        dhb = dh.astype(BF)
        dhb_ref[...] = dhb
        dgated = _mm_nt(dhb, wout_ref[...])
        tril, triu = _tri_masks()
        for h in range(GROUPS):
            wm = jnp.where(tril, ws_ref[h], jnp.zeros((), BF))
            wmt = jnp.where(triu, wst_ref[h], jnp.zeros((), BF))
            for c in range(nc):
                rs, cs = slice(c * CHUNK, (c + 1) * CHUNK), slice(h * CHUNK, (h + 1) * CHUNK)
                vnb = vn[rs, cs]
                s = _mm(wm, vnb) + bsb_ref[h]
                gated_ref[rs, cs] = (u[rs, cs] * s).astype(BF)
                dgt = dgated[rs, cs]
                du_scr[rs, cs] = dgt * s
                ds = dgt * u[rs, cs]
                dsb = ds.astype(BF)
                dws_ref[h] += jnp.where(tril, _mm_nt(dsb, vnb), 0.0)
                dbs_ref[h] += ds
                dvn_scr[rs, cs] = _mm(wmt, dsb)
        dvn = dvn_scr[...]
        dv, dgv = _rms_bwd(dvn, vhat, rv, gvv)
        dgv_ref[...] += dgv
        dz_ref[:, :AW] = (du_scr[...] * gp[:, :AW]).astype(BF)
        dz_ref[:, AW:] = (dv * gp[:, AW:]).astype(BF)
        dxn = _mm_nt(dz_ref[...], win_ref[...])
        dxr, dg = _rms_bwd(dxn, xhat, r, gmix)
        dg_ref[...] += dg
        dx_ref[...] = dh + dxr

        @pl.when(i == nt - 1)
        def _():
            for h in range(GROUPS):
                dbs_ref[h] = jnp.broadcast_to(jnp.sum(dbs_ref[h], axis=-1, keepdims=True), (CHUNK, CHUNK))

    gshape = (GROUPS, CHUNK, CHUNK)
    return pl.pallas_call(
        body, name="mixa_bwd", grid=(nt,),
        in_specs=[_row_spec(tm, D), _row_spec(tm, D), _row_spec(tm, 2 * AW), _row_spec(tm, 2 * AW), _const_spec((1, D)),
                  _const_spec((D, 2 * AW)), _const_spec((1, AW)), _const_spec(gshape), _const_spec(gshape),
                  _const_spec(gshape), _const_spec((AW, D))],
        out_specs=[_row_spec(tm, D), pl.BlockSpec((D, tm), lambda i: (0, i)), _row_spec(tm, 2 * AW), _row_spec(tm, AW),
                   _row_spec(tm, D), _const_spec(gshape), _const_spec(gshape), _const_spec((1, AW)), _const_spec((1, D))],
        out_shape=[jax.ShapeDtypeStruct((S, D), F32), jax.ShapeDtypeStruct((D, S), BF),
                   jax.ShapeDtypeStruct((S, 2 * AW), BF), jax.ShapeDtypeStruct((S, AW), BF),
                   jax.ShapeDtypeStruct((S, D), BF), jax.ShapeDtypeStruct(gshape, F32),
                   jax.ShapeDtypeStruct(gshape, F32), jax.ShapeDtypeStruct((1, AW), F32),
                   jax.ShapeDtypeStruct((1, D), F32)],
        scratch_shapes=[pltpu.VMEM((tm, AW), F32), pltpu.VMEM((tm, AW), F32)],
        compiler_params=_params(VMEM_BIG),
    )(dh, x, zg, gp, g, win, gv, ws, wst, bsb, wout)


def _shift_down(x, prev8, k):
    n, c = x.shape[0] // 8, x.shape[1]
    r = pltpu.roll(x.reshape(n, 8, c), k, axis=1)
    before = jnp.concatenate([pltpu.roll(prev8.reshape(1, 8, c), k, axis=1), r[:-1]], axis=0)
    sub = lax.broadcasted_iota(jnp.int32, (1, 8, c), 1)
    return jnp.where(sub < k, before, r).reshape(x.shape)


def _shift_up(x, next8, k):
    n, c = x.shape[0] // 8, x.shape[1]
    r = pltpu.roll(x.reshape(n, 8, c), 8 - k, axis=1)
    after = jnp.concatenate([r[1:], pltpu.roll(next8.reshape(1, 8, c), 8 - k, axis=1)], axis=0)
    sub = lax.broadcasted_iota(jnp.int32, (1, 8, c), 1)
    return jnp.where(sub >= 8 - k, after, r).reshape(x.shape)


def _ffn_fwd(h, g, wup, cw, cb, wdown, tm):
    S = h.shape[0]

    def body(h_ref, g_ref, wup_ref, cw_ref, cb_ref, wdown_ref, o_ref, hh_ref, cc_ref, sh):
        i = pl.program_id(0)

        @pl.when(i == 0)
        def _():
            sh[...] = jnp.zeros_like(sh)

        x = h_ref[...]
        xn, _, _ = _rms_fwd(x, g_ref[...])
        xn = xn.astype(BF)
        acc = x
        up = lambda d: [_mm(xn, wup_ref[d]), _mm(xn, wup_ref[N_FF_BLK + d])]
        ahead = up(0)
        down = lambda d, act: _mm(act, wdown_ref[d * FF_BLK:(d + 1) * FF_BLK, :])
        behind = None
        for d in range(N_FF_BLK):
            cs = []
            hbs = ahead
            if d + 1 < N_FF_BLK:
                ahead = up(d + 1)
            if behind is not None:
                acc = acc + down(d - 1, behind)
            for blk, hb in zip((d, N_FF_BLK + d), hbs):
                hh_ref[blk] = hb.astype(BF)
                w, prev8 = cw_ref[blk], sh[blk]
                c = cb_ref[blk] + w[0:1] * _shift_down(hb, prev8, 2) + w[1:2] * _shift_down(hb, prev8, 1) + w[2:3] * hb
                sh[blk] = hb[tm - 8:, :]
                cc_ref[blk] = c.astype(BF)
                cs.append(c)
            behind = (cs[0] * _sigmoid(cs[0]) * cs[1]).astype(BF)
        o_ref[...] = acc + down(N_FF_BLK - 1, behind)

    blk3 = pl.BlockSpec((N_DEV, tm, FF_BLK), lambda i: (0, i, 0))
    return pl.pallas_call(
        body, name="ffn_fwd", grid=(S // tm,),
        in_specs=[_row_spec(tm, D), _const_spec((1, D)), _const_spec((N_DEV, D, FF_BLK), True),
                  _const_spec((N_DEV, 3, FF_BLK)), _const_spec((N_DEV, 1, FF_BLK)), _const_spec((D_FF, D), True)],
        out_specs=[_row_spec(tm, D), blk3, blk3],
        out_shape=[jax.ShapeDtypeStruct((S, D), F32), jax.ShapeDtypeStruct((N_DEV, S, FF_BLK), BF),
                   jax.ShapeDtypeStruct((N_DEV, S, FF_BLK), BF)],
        scratch_shapes=[pltpu.VMEM((N_DEV, 8, FF_BLK), F32)],
        compiler_params=_params(VMEM_BIG),
    )(h, g, wup, cw, cb, wdown)


def _ffn_bwd(dh, h, hh, cc, g, wup, cw, wdown, tm):
    S = h.shape[0]
    nt = S // tm

    def rev(i):
        return (nt - 1 - i, 0)

    def body(dh_ref, h_ref, hh_ref, cc_ref, g_ref, wup_ref, cw_ref, wdown_ref,
             dx_ref, xn_ref, dhh_ref, act_ref, dhb_ref, dcw_ref, dcb_ref, dg_ref, sh):
        i = pl.program_id(0)

        @pl.when(i == 0)
        def _():
            sh[...] = jnp.zeros_like(sh)
            dcw_ref[...] = jnp.zeros_like(dcw_ref)
            dcb_ref[...] = jnp.zeros_like(dcb_ref)
            dg_ref[...] = jnp.zeros_like(dg_ref)

        x = h_ref[...]
        gain = g_ref[...]
        xn, xhat, r = _rms_fwd(x, gain)
        xn_ref[...] = xn.T.astype(BF)
        dh = dh_ref[...]
        dhb = dh.astype(BF)
        dhb_ref[...] = dhb
        dxn = jnp.zeros((tm, D), F32)
        dact_of = lambda d: _mm_nt(dhb, wdown_ref[d * FF_BLK:(d + 1) * FF_BLK, :])
        ahead = dact_of(0)
        for d in range(N_FF_BLK):
            blks = (d, N_FF_BLK + d)
            dact = ahead
            if d + 1 < N_FF_BLK:
                ahead = dact_of(d + 1)
            cg, cu = cc_ref[blks[0]].astype(F32), cc_ref[blks[1]].astype(F32)
            sg = _sigmoid(cg)
            sil = cg * sg
            act_ref[d] = (sil * cu).astype(BF)
            dcs = (dact * cu * (sg + sil - sil * sg), dact * sil)
            for blk, dc in zip(blks, dcs):
                next8 = sh[blk]
                dp1, dp2 = _shift_up(dc, next8, 1), _shift_up(dc, next8, 2)
                hb = hh_ref[blk].astype(F32)
                dcb_ref[blk] += jnp.sum(dc, axis=0, keepdims=True)
                dcw_ref[blk, 0:1, :] += jnp.sum(dp2 * hb, axis=0, keepdims=True)
                dcw_ref[blk, 1:2, :] += jnp.sum(dp1 * hb, axis=0, keepdims=True)
                dcw_ref[blk, 2:3, :] += jnp.sum(dc * hb, axis=0, keepdims=True)
                w = cw_ref[blk]
                dhh = (w[2:3] * dc + w[1:2] * dp1 + w[0:1] * dp2).astype(BF)
                sh[blk] = dc[0:8, :]
                dhh_ref[blk] = dhh
                dxn = dxn + _mm_nt(dhh, wup_ref[blk])
        dxr, dg = _rms_bwd(dxn, xhat, r, gain)
        dg_ref[...] += dg
        dx_ref[...] = dh + dxr

    blk3 = lambda i: (0, nt - 1 - i, 0)
    return pl.pallas_call(
        body, name="ffn_bwd", grid=(nt,),
        in_specs=[pl.BlockSpec((tm, D), rev), pl.BlockSpec((tm, D), rev),
                  pl.BlockSpec((N_DEV, tm, FF_BLK), blk3), pl.BlockSpec((N_DEV, tm, FF_BLK), blk3),
                  _const_spec((1, D)), _const_spec((N_DEV, D, FF_BLK), True), _const_spec((N_DEV, 3, FF_BLK)),
                  _const_spec((D_FF, D), True)],
        out_specs=[pl.BlockSpec((tm, D), rev), pl.BlockSpec((D, tm), lambda i: (0, nt - 1 - i)),
                   pl.BlockSpec((N_DEV, tm, FF_BLK), blk3), pl.BlockSpec((N_FF_BLK, tm, FF_BLK), blk3),
                   pl.BlockSpec((tm, D), rev), _const_spec((N_DEV, 3, FF_BLK)), _const_spec((N_DEV, 1, FF_BLK)),
                   _const_spec((1, D))],
        out_shape=[jax.ShapeDtypeStruct((S, D), F32), jax.ShapeDtypeStruct((D, S), BF),
                   jax.ShapeDtypeStruct((N_DEV, S, FF_BLK), BF), jax.ShapeDtypeStruct((N_FF_BLK, S, FF_BLK), BF),
                   jax.ShapeDtypeStruct((S, D), BF), jax.ShapeDtypeStruct((N_DEV, 3, FF_BLK), F32),
                   jax.ShapeDtypeStruct((N_DEV, 1, FF_BLK), F32), jax.ShapeDtypeStruct((1, D), F32)],
        scratch_shapes=[pltpu.VMEM((N_DEV, 8, FF_BLK), F32)],
        compiler_params=_params(VMEM_BIG),
    )(dh, h, hh, cc, g, wup, cw, wdown)


def _ple_fwd(h, g, wgate, bgate, p, wple, tm):
    S = h.shape[0]

    def body(h_ref, g_ref, wg_ref, bg_ref, p_ref, wp_ref, o_ref):
        x = h_ref[...]
        xn, _, _ = _rms_fwd(x, g_ref[...])
        gate = _sigmoid(_mm(xn.astype(BF), wg_ref[...]) + bg_ref[...])
        o_ref[...] = x + _mm(p_ref[...], wp_ref[...]) * gate

    return pl.pallas_call(
        body, name="ple_fwd", grid=(S // tm,),
        in_specs=[_row_spec(tm, D), _const_spec((1, D)), _const_spec((D, D)), _const_spec((1, D)),
                  _row_spec(tm, PLE), _const_spec((PLE, D))],
        out_specs=_row_spec(tm, D), out_shape=jax.ShapeDtypeStruct((S, D), F32),
        compiler_params=_params(),
    )(h, g, wgate, bgate, p, wple)


def _ple_bwd(dh, h, g, wgate, bgate, p, wple, tm):
    S = h.shape[0]

    def body(dh_ref, h_ref, g_ref, wg_ref, bg_ref, p_ref, wp_ref, dx_ref, de_ref, xn_ref, dpre_ref, dbg_ref, dg_ref):
        i = pl.program_id(0)

        @pl.when(i == 0)
        def _():
            dbg_ref[...] = jnp.zeros_like(dbg_ref)
            dg_ref[...] = jnp.zeros_like(dg_ref)

        x = h_ref[...]
        gain = g_ref[...]
        xn, xhat, r = _rms_fwd(x, gain)
        xnb = xn.astype(BF)
        xn_ref[...] = xnb
        gate = _sigmoid(_mm(xnb, wg_ref[...]) + bg_ref[...])
        e = _mm(p_ref[...], wp_ref[...])
        dh = dh_ref[...]
        de_ref[...] = (dh * gate).astype(BF)
        dpre = dh * e * gate * (1.0 - gate)
        dpreb = dpre.astype(BF)
        dpre_ref[...] = dpreb
        dbg_ref[...] += jnp.sum(dpre, axis=0, keepdims=True)
        dxr, dg = _rms_bwd(_mm_nt(dpreb, wg_ref[...]), xhat, r, gain)
        dg_ref[...] += dg
        dx_ref[...] = dh + dxr

    return pl.pallas_call(
        body, name="ple_bwd", grid=(S // tm,),
        in_specs=[_row_spec(tm, D), _row_spec(tm, D), _const_spec((1, D)), _const_spec((D, D)), _const_spec((1, D)),
                  _row_spec(tm, PLE), _const_spec((PLE, D))],
        out_specs=[_row_spec(tm, D), _row_spec(tm, D), _row_spec(tm, D), _row_spec(tm, D), _const_spec((1, D)),
                   _const_spec((1, D))],
        out_shape=[jax.ShapeDtypeStruct((S, D), F32), jax.ShapeDtypeStruct((S, D), BF),
                   jax.ShapeDtypeStruct((S, D), BF), jax.ShapeDtypeStruct((S, D), BF),
                   jax.ShapeDtypeStruct((1, D), F32), jax.ShapeDtypeStruct((1, D), F32)],
        compiler_params=_params(),
    )(dh, h, g, wgate, bgate, p, wple)


def _qkv_fwd(h, gq, wq, gkv, wkv, tm):
    S = h.shape[0]
    nkv = wkv.shape[1]

    def body(h_ref, gq_ref, wq_ref, gkv_ref, wkv_ref, q_ref, kv_ref):
        x = h_ref[...]
        _, xhat, _ = _rms_fwd(x, gq_ref[...])
        q_ref[...] = _mm((xhat * gq_ref[...]).astype(BF), wq_ref[...]).astype(BF)
        kv_ref[...] = _mm((xhat * gkv_ref[...]).astype(BF), wkv_ref[...]).astype(BF)

    return pl.pallas_call(
        body, name="qkv_fwd", grid=(S // tm,),
        in_specs=[_row_spec(tm, D), _const_spec((1, D)), _const_spec((D, D)), _const_spec((1, D)),
                  _const_spec((D, nkv))],
        out_specs=[_row_spec(tm, D), _row_spec(tm, nkv)],
        out_shape=[jax.ShapeDtypeStruct((S, D), BF), jax.ShapeDtypeStruct((S, nkv), BF)],
        compiler_params=_params(),
    )(h, gq, wq, gkv, wkv)


def _qkv_bwd(dh, h, dq, dkv, gq, wq, gkv, wkv, tm):
    S = h.shape[0]
    nkv = wkv.shape[1]

    def body(dh_ref, h_ref, dq_ref, dkv_ref, gq_ref, wq_ref, gkv_ref, wkv_ref,
             dx_ref, xq_ref, xkv_ref, dgq_ref, dgkv_ref):
        i = pl.program_id(0)

        @pl.when(i == 0)
        def _():
            dgq_ref[...] = jnp.zeros_like(dgq_ref)
            dgkv_ref[...] = jnp.zeros_like(dgkv_ref)

        x = h_ref[...]
        gq_, gkv_ = gq_ref[...], gkv_ref[...]
        _, xhat, r = _rms_fwd(x, gq_)
        xq_ref[...] = (xhat * gq_).astype(BF)
        xkv_ref[...] = (xhat * gkv_).astype(BF)
        d1, dg1 = _rms_bwd(_mm_nt(dq_ref[...], wq_ref[...]), xhat, r, gq_)
        d2, dg2 = _rms_bwd(_mm_nt(dkv_ref[...], wkv_ref[...]), xhat, r, gkv_)
        dgq_ref[...] += dg1
        dgkv_ref[...] += dg2
        dx_ref[...] = dh_ref[...] + d1 + d2

    return pl.pallas_call(
        body, name="qkv_bwd", grid=(S // tm,),
        in_specs=[_row_spec(tm, D), _row_spec(tm, D), _row_spec(tm, D), _row_spec(tm, nkv), _const_spec((1, D)),
                  _const_spec((D, D)), _const_spec((1, D)), _const_spec((D, nkv))],
        out_specs=[_row_spec(tm, D), _row_spec(tm, D), _row_spec(tm, D), _const_spec((1, D)), _const_spec((1, D))],
        out_shape=[jax.ShapeDtypeStruct((S, D), F32), jax.ShapeDtypeStruct((S, D), BF),
                   jax.ShapeDtypeStruct((S, D), BF), jax.ShapeDtypeStruct((1, D), F32),
                   jax.ShapeDtypeStruct((1, D), F32)],
        compiler_params=_params(),
    )(dh, h, dq, dkv, gq, wq, gkv, wkv)


def _oproj_fwd(a, w, res, tm):
    S = a.shape[0]

    def body(a_ref, w_ref, r_ref, o_ref):
        o_ref[...] = r_ref[...] + _mm(a_ref[...], w_ref[...])

    return pl.pallas_call(
        body, name="oproj_fwd", grid=(S // tm,),
        in_specs=[_row_spec(tm, D), _const_spec((D, D)), _row_spec(tm, D)],
        out_specs=_row_spec(tm, D), out_shape=jax.ShapeDtypeStruct((S, D), F32),
        compiler_params=_params(),
    )(a, w, res)


def _oproj_bwd(dh, w, tm):
    S = dh.shape[0]

    def body(dh_ref, w_ref, da_ref, dhb_ref):
        dhb = dh_ref[...].astype(BF)
        dhb_ref[...] = dhb
        da_ref[...] = _mm_nt(dhb, w_ref[...]).astype(BF)

    return pl.pallas_call(
        body, name="oproj_bwd", grid=(S // tm,),
        in_specs=[_row_spec(tm, D), _const_spec((D, D))],
        out_specs=[_row_spec(tm, D), _row_spec(tm, D)],
        out_shape=[jax.ShapeDtypeStruct((S, D), BF), jax.ShapeDtypeStruct((S, D), BF)],
        compiler_params=_params(),
    )(dh, w)


def _alibi_slope(hq):
    return float(np.float32(2.0 ** (-8.0 * (hq + 1) / N_Q)))


def _attn_consts():
    W = N_KV * HD
    r = np.zeros((W, N_KV * W), np.float32)
    for kh in range(N_KV):
        for c in range(W):
            r[kh * HD + c % HD, kh * W + c] = 1.0
    qi = np.arange(BLOCK)[:, None]
    kj = np.arange(2 * BLOCK)[None, :]
    dist = qi + BLOCK - kj
    band = (dist >= 0) & (dist < BLOCK)
    bias = np.full((2, N_KV, GQA * BLOCK, 2 * BLOCK), NEG, np.float32)
    for later in (0, 1):
        valid = band & ((kj >= BLOCK) | bool(later))
        for kh in range(N_KV):
            for g in range(GQA):
                slope = np.float32(_alibi_slope(kh * GQA + g))
                bias[later, kh, g * BLOCK:(g + 1) * BLOCK] = np.where(valid, -slope * dist.astype(np.float32), NEG)
    lane_head = np.arange(W)[None, :] // HD
    maskq = (np.arange(GQA * BLOCK)[:, None] // BLOCK == lane_head).astype(np.float32)
    maskk = (np.arange(GQA * 2 * BLOCK)[:, None] // (2 * BLOCK) == lane_head).astype(np.float32)
    head_sum = (lane_head.T == np.arange(GQA * BLOCK)[None, :] // BLOCK).astype(np.float32)
    return dict(rep=jnp.asarray(r, BF), bias=jnp.asarray(bias), maskq=jnp.asarray(maskq, BF),
                maskk=jnp.asarray(maskk, BF), head_sum=jnp.asarray(head_sum, BF))


def _sink_col(sink_ref, kh):
    rb = lax.broadcasted_iota(jnp.int32, (GQA * BLOCK, 1), 0) >> 7
    col = jnp.full((GQA * BLOCK, 1), sink_ref[kh * GQA + GQA - 1], F32)
    for g in range(GQA - 1):
        col = jnp.where(rb == g, sink_ref[kh * GQA + g], col)
    return col


def _stack4(a):
    return jnp.concatenate([a] * GQA, axis=0)


def _rows_to_lanes(a):
    return jnp.concatenate([a[g * BLOCK:(g + 1) * BLOCK] for g in range(GQA)], axis=1)


def _lanes_to_rows(a, n):
    return jnp.concatenate([a[:, g * n:(g + 1) * n] for g in range(GQA)], axis=0)


def _scores(qg, k4, bias, sink_col, maskq):
    qs = _stack4(qg) * maskq
    s = _mm_nt(qs, k4) * (HD ** -0.5) + bias
    m = jnp.maximum(jnp.max(s, axis=-1, keepdims=True), sink_col)
    return qs, jnp.exp(s - m), jnp.exp(sink_col - m)


def _bands(kvc_ref, kvp_ref, rep_ref):
    W = N_KV * HD
    kvp, kvc = kvp_ref[...], kvc_ref[...]
    kband = jnp.concatenate([kvp[:, :W], kvc[:, :W]], axis=0)
    vband = jnp.concatenate([kvp[:, W:], kvc[:, W:]], axis=0)
    return _mm(kband, rep_ref[...]).astype(BF), _mm(vband, rep_ref[...]).astype(BF)


def _attn_fwd(q, kv, sinks, consts):
    S = q.shape[0]
    nb = S // BLOCK
    W = N_KV * HD

    def body(sink_ref, q_ref, kvc_ref, kvp_ref, rep_ref, bias_ref, maskq_ref, maskk_ref, o_ref):
        maskq, maskk = maskq_ref[...], maskk_ref[...]
        lane_head = lax.broadcasted_iota(jnp.int32, (BLOCK, W), 1) >> 6
        k4all, v4all = _bands(kvc_ref, kvp_ref, rep_ref)
        for kh in range(N_KV):
            cols = slice(kh * W, (kh + 1) * W)
            _, e, es = _scores(q_ref[:, cols], k4all[:, cols], bias_ref[kh], _sink_col(sink_ref, kh), maskq)
            vbd = jnp.concatenate([_stack4(v4all[:, cols]) * maskk, maskk], axis=1)
            nd = _mm(_rows_to_lanes(e.astype(BF)), vbd)
            es_l = jnp.broadcast_to(es[(GQA - 1) * BLOCK:], (BLOCK, W))
            for g in range(GQA - 1):
                es_l = jnp.where(lane_head == g, es[g * BLOCK:(g + 1) * BLOCK], es_l)
            o_ref[:, cols] = (nd[:, :W] / (nd[:, W:] + es_l)).astype(BF)

    return pl.pallas_call(
        body, name="attn_fwd", grid=(nb,),
        in_specs=[pl.BlockSpec(memory_space=pltpu.SMEM), _row_spec(BLOCK, D), _row_spec(BLOCK, 2 * W),
                  pl.BlockSpec((BLOCK, 2 * W), lambda i: (jnp.maximum(i - 1, 0), 0)), _const_spec((W, N_KV * W)),
                  pl.BlockSpec((None, N_KV, GQA * BLOCK, 2 * BLOCK), lambda i: (jnp.minimum(i, 1), 0, 0, 0)),
                  _const_spec((GQA * BLOCK, W)), _const_spec((GQA * 2 * BLOCK, W))],
        out_specs=_row_spec(BLOCK, D), out_shape=jax.ShapeDtypeStruct((S, D), BF),
        compiler_params=_params(),
    )(sinks, q, kv, kv, consts["rep"], consts["bias"], consts["maskq"], consts["maskk"])


def _attn_bwd(q, kv, att, do, sinks, consts):
    S = q.shape[0]
    nb = S // BLOCK
    W = N_KV * HD

    def qmap(i):
        return (jnp.minimum(i, nb - 1), 0)

    def body(sink_ref, q_ref, kvc_ref, kvp_ref, att_ref, do_ref, rep_ref, bias_ref, maskq_ref, maskk_ref, hsum_ref,
             dq_ref, dkv_ref, dsink_ref, carry, sink_acc):
        i = pl.program_id(0)

        @pl.when(i == 0)
        def _():
            sink_acc[...] = jnp.zeros_like(sink_acc)
            carry[...] = jnp.zeros_like(carry)

        @pl.when(i < nb)
        def _():
            maskq, maskk, head_sum = maskq_ref[...], maskk_ref[...], hsum_ref[...]
            ones_k = jnp.ones((2 * BLOCK, BLOCK), BF)
            k4all, v4all = _bands(kvc_ref, kvp_ref, rep_ref)
            dk4s, dv4s = [], []
            for kh in range(N_KV):
                cols = slice(kh * W, (kh + 1) * W)
                k4, v4 = k4all[:, cols], v4all[:, cols]
                qs, e, es = _scores(q_ref[:, cols], k4, bias_ref[kh], _sink_col(sink_ref, kh), maskq)
                inv = 1.0 / (_mm(e.astype(BF), ones_k) + es)
                p = e * jnp.concatenate([inv, inv], axis=1)
                dog = do_ref[:, cols]
                dos = _stack4(dog) * maskq
                dp = _mm_nt(dos, v4)
                prod = (dog.astype(F32) * att_ref[:, cols].astype(F32)).astype(BF)
                delta = _lanes_to_rows(_mm(prod, head_sum), BLOCK)
                ds = (p * (dp - jnp.concatenate([delta, delta], axis=1))).astype(BF)
                sink_acc[kh] += es * inv * delta
                kbd = _stack4(k4) * maskk
                dq_ref[:, cols] = (_mm(_rows_to_lanes(ds), kbd) * (HD ** -0.5)).astype(BF)
                dk4s.append((_mm_tn(ds, qs) * (HD ** -0.5)).astype(BF))
                dv4s.append(_mm_tn(p.astype(BF), dos).astype(BF))
            dkband = _mm_nt(jnp.concatenate(dk4s, axis=1), rep_ref[...])
            dvband = _mm_nt(jnp.concatenate(dv4s, axis=1), rep_ref[...])
            dkv_ref[:, :W] = (carry[:, :W] + dkband[:BLOCK]).astype(BF)
            dkv_ref[:, W:] = (carry[:, W:] + dvband[:BLOCK]).astype(BF)
            carry[:, :W] = dkband[BLOCK:]
            carry[:, W:] = dvband[BLOCK:]

        @pl.when(i == nb)
        def _():
            dkv_ref[...] = carry[...].astype(BF)
            for hq in range(N_Q):
                kh, g = divmod(hq, GQA)
                dsink_ref[hq:hq + 1, :] = -jnp.sum(sink_acc[kh, g * BLOCK:(g + 1) * BLOCK, :], axis=0, keepdims=True)

    return pl.pallas_call(
        body, name="attn_bwd", grid=(nb + 1,),
        in_specs=[pl.BlockSpec(memory_space=pltpu.SMEM), pl.BlockSpec((BLOCK, D), qmap),
                  pl.BlockSpec((BLOCK, 2 * W), qmap),
                  pl.BlockSpec((BLOCK, 2 * W), lambda i: (jnp.maximum(jnp.minimum(i, nb - 1) - 1, 0), 0)),
                  pl.BlockSpec((BLOCK, D), qmap), pl.BlockSpec((BLOCK, D), qmap), _const_spec((W, N_KV * W)),
                  pl.BlockSpec((None, N_KV, GQA * BLOCK, 2 * BLOCK), lambda i: (jnp.minimum(i, 1), 0, 0, 0)),
                  _const_spec((GQA * BLOCK, W)), _const_spec((GQA * 2 * BLOCK, W)), _const_spec((W, GQA * BLOCK))],
        out_specs=[pl.BlockSpec((BLOCK, D), qmap),
                   pl.BlockSpec((BLOCK, 2 * W), lambda i: (jnp.maximum(i - 1, 0), 0)),
                   _const_spec((N_Q, BLOCK))],
        out_shape=[jax.ShapeDtypeStruct((S, D), BF), jax.ShapeDtypeStruct((S, 2 * W), BF),
                   jax.ShapeDtypeStruct((N_Q, BLOCK), F32)],
        scratch_shapes=[pltpu.VMEM((BLOCK, 2 * W), F32), pltpu.VMEM((N_KV, GQA * BLOCK, BLOCK), F32)],
        compiler_params=_params(),
    )(sinks, q, kv, kv, att, do, consts["rep"], consts["bias"], consts["maskq"], consts["maskk"], consts["head_sum"])


def _head(h, g, target, tm):
    S = h.shape[0]

    def body(h_ref, g_ref, t_ref, dh_ref, loss_ref, dg_ref):
        i = pl.program_id(0)

        @pl.when(i == 0)
        def _():
            loss_ref[...] = jnp.zeros_like(loss_ref)
            dg_ref[...] = jnp.zeros_like(dg_ref)

        gain = g_ref[...]
        y, xhat, r = _rms_fwd(h_ref[...], gain)
        err = y - t_ref[...]
        tile = 0.5 * jnp.sum(jnp.sum(err * err, axis=-1, keepdims=True) * (1.0 / D), axis=0, keepdims=True)
        loss_ref[...] += jnp.broadcast_to(tile, loss_ref.shape)
        dx, dg = _rms_bwd(err * (1.0 / D), xhat, r, gain)
        dg_ref[...] += dg
        dh_ref[...] = dx

    return pl.pallas_call(
        body, name="loss_head", grid=(S // tm,),
        in_specs=[_row_spec(tm, D), _const_spec((1, D)), _row_spec(tm, D)],
        out_specs=[_row_spec(tm, D), _const_spec((8, 128)), _const_spec((1, D))],
        out_shape=[jax.ShapeDtypeStruct((S, D), F32), jax.ShapeDtypeStruct((8, 128), F32),
                   jax.ShapeDtypeStruct((1, D), F32)],
        compiler_params=_params(),
    )(h, g, target)


def _wgrad(a, b, name, *, tn, out_blocked=False, a_transposed=False, ts=2048, per_step=1):
    a_blocked, b_blocked = a.ndim == 3, b.ndim == 3
    na = a.shape[0] if a_blocked else 1
    S = b.shape[-2]
    K = a.shape[0] if a_transposed else a.shape[-1]
    wide = tn * per_step
    nbk = b.shape[0] if b_blocked else b.shape[1] // wide
    ts = min(ts, S)
    ns = S // ts
    assert not (a_blocked and (out_blocked or a_transposed)) and (per_step == 1 or (out_blocked and not b_blocked))

    def body(a_ref, b_ref, o_ref, acc):
        s = pl.program_id(2)

        @pl.when(s == 0)
        def _():
            acc[...] = jnp.zeros_like(acc)

        if a_transposed:
            acc[...] += _mm(a_ref[...], b_ref[...])
        else:
            acc[...] += _mm_tn(a_ref[...], b_ref[...])

        @pl.when(s == ns - 1)
        def _():
            if per_step == 1:
                o_ref[...] = acc[...].astype(o_ref.dtype)
            else:
                for k in range(per_step):
                    o_ref[k] = acc[:, k * tn:(k + 1) * tn].astype(o_ref.dtype)

    if a_transposed:
        a_spec = pl.BlockSpec((K, ts), lambda i, j, s: (0, s))
    elif a_blocked:
        a_spec = pl.BlockSpec((None, ts, K), lambda i, j, s: (i, s, 0))
    else:
        a_spec = pl.BlockSpec((ts, K), lambda i, j, s: (s, 0))
    b_spec = (pl.BlockSpec((None, ts, tn), lambda i, j, s: (j, s, 0)) if b_blocked
              else pl.BlockSpec((ts, wide), lambda i, j, s: (s, j)))
    if out_blocked:
        o_spec = pl.BlockSpec((None, K, tn) if per_step == 1 else (per_step, K, tn), lambda i, j, s: (j, 0, 0))
        o_shape = jax.ShapeDtypeStruct((nbk * per_step, K, tn), BF)
    else:
        o_spec = pl.BlockSpec((K, wide), lambda i, j, s: (i, j))
        o_shape = jax.ShapeDtypeStruct((na * K, nbk * wide), BF)
    return pl.pallas_call(
        body, name=name, grid=(na, nbk, ns), in_specs=[a_spec, b_spec], out_specs=o_spec, out_shape=o_shape,
        scratch_shapes=[pltpu.VMEM((K, wide), F32)],
        compiler_params=pltpu.CompilerParams(dimension_semantics=("arbitrary", "arbitrary", "arbitrary"),
                                             vmem_limit_bytes=VMEM_BIG),
    )(a, b)


def _adamw_math(w, g, m, v):
    m = B1 * m + (1.0 - B1) * g
    v = B2 * v + (1.0 - B2) * (g * g)
    m_hat = m / (1.0 - B1 ** STEP)
    v_hat = v / (1.0 - B2 ** STEP)
    return -LR * (m_hat / (jnp.sqrt(v_hat) + ADAM_EPS) + WD * w), m, v


def _sum8(parts, name):
    R = parts.shape[1]

    def body(p_ref, o_ref):
        g = p_ref[0]
        for s in range(1, N_DEV):
            g = g + p_ref[s]
        o_ref[...] = g

    return pl.pallas_call(
        body, name=name, grid=(1,),
        in_specs=[pl.BlockSpec((N_DEV, R, ROW), lambda i: (0, 0, 0))],
        out_specs=pl.BlockSpec((R, ROW), lambda i: (0, 0)), out_shape=jax.ShapeDtypeStruct((R, ROW), F32),
        compiler_params=_params(),
    )(parts)


def _adamw_small(g, w, m, v, name):
    R = w.shape[0]

    def body(g_ref, w_ref, m_ref, v_ref, d_ref, nm_ref, nv_ref):
        d_ref[...], nm_ref[...], nv_ref[...] = _adamw_math(w_ref[...], g_ref[...], m_ref[...], v_ref[...])

    spec = pl.BlockSpec((R, ROW), lambda i: (0, 0))
    return pl.pallas_call(
        body, name=name, grid=(1,), in_specs=[spec] * 4, out_specs=[spec] * 3,
        out_shape=[jax.ShapeDtypeStruct((R, ROW), F32)] * 3, compiler_params=_params(),
    )(g, w, m, v)


def _mesh_pos():
    return lax.axis_index("x"), lax.axis_index("y"), lax.axis_index("c")


def _flip(v, bit):
    return 1 - v if bit else v


def _peers(x, y, c):
    out = []
    for k in range(1, N_DEV):
        px, py, pc = _flip(x, k & 4), _flip(y, k & 2), _flip(c, k & 1)
        out.append((k - 1, (px, py, pc), 4 * px + 2 * py + pc))
    return out


def _all_gather_now(xs, name):
    n = len(xs)
    MESH = pl.DeviceIdType.MESH

    def body(*refs):
        x_refs, out_refs = refs[:n], refs[n:2 * n]
        send_sems, recv_sems, local_sems = refs[2 * n:]
        x, y, c = _mesh_pos()
        me, sibling = (x, y, c), (x, y, 1 - c)
        chips = [(1 - x, y), (x, 1 - y), (1 - x, 1 - y)]
        blk = lambda p: 4 * p[0] + 2 * p[1] + p[2]

        def copy(a, k, block, to, src=None):
            return pltpu.make_async_remote_copy(
                src_ref=out_refs[a].at[blk(block)] if src is None else src, dst_ref=out_refs[a].at[blk(block)],
                send_sem=send_sems.at[a, k], recv_sem=recv_sems.at[a, k], device_id=to, device_id_type=MESH)

        sent, own = [], []
        for a in range(n):
            own.append(pltpu.make_async_copy(x_refs[a], out_refs[a].at[blk(me)], local_sems.at[a]))
            own[-1].start()
            first = [copy(a, 1 + j, me, (*chip, c), src=x_refs[a]) for j, chip in enumerate(chips)]
            first.append(copy(a, 0, me, sibling, src=x_refs[a]))
            for cp in first:
                cp.start()
            sent += first
        for a in range(n):
            for j, chip in enumerate(chips):
                copy(a, 1 + j, (*chip, c), me).wait_recv()
                sent.append(copy(a, 4 + j, (*chip, c), sibling))
                sent[-1].start()
        for a in range(n):
            copy(a, 0, sibling, me).wait_recv()
            for j, chip in enumerate(chips):
                copy(a, 4 + j, (*chip, 1 - c), me).wait_recv()
        for cp in sent:
            cp.wait_send()
        for cp in own:
            cp.wait()

    hbm = pl.BlockSpec(memory_space=pl.ANY)
    return pl.pallas_call(
        body, name=name, out_shape=[jax.ShapeDtypeStruct((N_DEV,) + x.shape, x.dtype) for x in xs],
        in_specs=[hbm] * n, out_specs=[hbm] * n,
        scratch_shapes=[pltpu.SemaphoreType.DMA((n, 7)), pltpu.SemaphoreType.DMA((n, 7)), pltpu.SemaphoreType.DMA((n,))],
    )(*xs)


def _chip_peers(x, y, c):
    return (x, y, c), (x, y, 1 - c), [(1 - x, y, c), (x, 1 - y, c), (1 - x, 1 - y, c)]


def _blk(p):
    return 4 * p[0] + 2 * p[1] + p[2]


def _gather2_start(srcs, name):
    n = len(srcs)
    MESH = pl.DeviceIdType.MESH

    def body(*refs):
        src_refs, land_refs = refs[:n], refs[n:2 * n]
        send, recv_ici, recv_sib, local = (refs[(2 + k) * n:(3 + k) * n] for k in range(4))
        token = refs[8 * n]
        me, sibling, chips = _chip_peers(*_mesh_pos())
        for a in range(n):
            dst = land_refs[a].at[_blk(me)]
            pltpu.make_async_remote_copy(src_ref=src_refs[a], dst_ref=dst, send_sem=send[a].at[0], recv_sem=recv_sib[a],
                                         device_id=sibling, device_id_type=MESH).start()
            for j, chip in enumerate(chips):
                pltpu.make_async_remote_copy(src_ref=src_refs[a], dst_ref=dst, send_sem=send[a].at[1 + j],
                                             recv_sem=recv_ici[a].at[j], device_id=chip, device_id_type=MESH).start()
            pltpu.make_async_copy(src_refs[a], dst, local[a]).start()
        token[...] = jnp.zeros_like(token)

    hbm = pl.BlockSpec(memory_space=pltpu.HBM)
    sem = pl.BlockSpec(memory_space=pltpu.SEMAPHORE)
    dma = pltpu.SemaphoreType.DMA
    outs = pl.pallas_call(
        body, name=name,
        out_shape=([dma((4,))] * n + [dma((3,))] * n + [dma(())] * n + [dma(())] * n
                   + [pltpu.HBM(s.shape, s.dtype) for s in srcs]
                   + [pltpu.HBM((N_DEV,) + s.shape, s.dtype) for s in srcs] + [jax.ShapeDtypeStruct((8, 128), F32)]),
        in_specs=[hbm] * (2 * n), out_specs=[sem] * (4 * n) + [hbm] * (2 * n) + [pl.BlockSpec(memory_space=pltpu.VMEM)],
        input_output_aliases={**{a: 4 * n + a for a in range(n)}, **{n + a: 5 * n + a for a in range(n)}},
        compiler_params=pltpu.CompilerParams(has_side_effects=pltpu.SideEffectType.DATAFLOW_SIDE_EFFECTING),
    )(*[pltpu.with_memory_space_constraint(s, pltpu.HBM) for s in srcs],
      *[pltpu.with_memory_space_constraint(lax.empty((N_DEV,) + s.shape, s.dtype), pltpu.HBM) for s in srcs])
    keys = ["send", "recv_ici", "recv_sib", "local", "srcs", "lands"]
    return {**{k: outs[i * n:(i + 1) * n] for i, k in enumerate(keys)}, "token": outs[6 * n]}


def _gather2_forward(started, after, name):
    lands, recv_ici = started["lands"], started["recv_ici"]
    n = len(lands)
    MESH = pl.DeviceIdType.MESH

    def body(*refs):
        land_refs, recv_refs = refs[:n], refs[n:2 * n]
        fsend, frecv = refs[2 * n + 1 + n:2 * n + 1 + 2 * n], refs[2 * n + 1 + 2 * n:2 * n + 1 + 3 * n]
        me, sibling, chips = _chip_peers(*_mesh_pos())
        for a in range(n):
            for j, chip in enumerate(chips):
                rows = land_refs[a].at[_blk(chip)]
                pltpu.make_async_remote_copy(src_ref=rows, dst_ref=rows, send_sem=fsend[a].at[j], recv_sem=recv_refs[a].at[j],
                                             device_id=chip, device_id_type=MESH).wait_recv()
                pltpu.make_async_remote_copy(src_ref=rows, dst_ref=rows, send_sem=fsend[a].at[j], recv_sem=frecv[a].at[j],
                                             device_id=sibling, device_id_type=MESH).start()

    hbm = pl.BlockSpec(memory_space=pltpu.HBM)
    sem = pl.BlockSpec(memory_space=pltpu.SEMAPHORE)
    dma = pltpu.SemaphoreType.DMA
    outs = pl.pallas_call(
        body, name=name,
        out_shape=[pltpu.HBM(l.shape, l.dtype) for l in lands] + [dma((3,))] * (2 * n),
        in_specs=[hbm] * n + [sem] * n + [pl.BlockSpec(memory_space=pl.ANY)], out_specs=[hbm] * n + [sem] * (2 * n),
        input_output_aliases={a: a for a in range(n)},
        compiler_params=pltpu.CompilerParams(has_side_effects=pltpu.SideEffectType.DATAFLOW_SIDE_EFFECTING),
    )(*lands, *recv_ici, after)
    return {**started, "lands": outs[:n], "fsend": outs[n:2 * n], "frecv": outs[2 * n:3 * n]}


def _gather2_wait(st, after, name):
    n = len(st["lands"])
    MESH = pl.DeviceIdType.MESH

    def body(*refs):
        src_refs, land_refs = refs[:n], refs[n:2 * n]
        send, recv_sib, local, fsend, frecv = (refs[(2 + k) * n:(3 + k) * n] for k in range(5))
        me, sibling, chips = _chip_peers(*_mesh_pos())
        for a in range(n):
            def desc(src, dst, s, r):
                return pltpu.make_async_remote_copy(src_ref=src, dst_ref=dst, send_sem=s, recv_sem=r, device_id=sibling,
                                                    device_id_type=MESH)
            own = land_refs[a].at[_blk(me)]
            for k in range(4):
                desc(src_refs[a], own, send[a].at[k], recv_sib[a]).wait_send()
            desc(src_refs[a], land_refs[a].at[_blk(sibling)], send[a].at[0], recv_sib[a]).wait_recv()
            pltpu.make_async_copy(src_refs[a], own, local[a]).wait()
            for j, chip in enumerate(chips):
                mine, theirs = land_refs[a].at[_blk(chip)], land_refs[a].at[_blk((chip[0], chip[1], 1 - chip[2]))]
                desc(mine, mine, fsend[a].at[j], frecv[a].at[j]).wait_send()
                desc(theirs, theirs, fsend[a].at[j], frecv[a].at[j]).wait_recv()

    hbm = pl.BlockSpec(memory_space=pltpu.HBM)
    sem = pl.BlockSpec(memory_space=pltpu.SEMAPHORE)
    srcs, lands = st["srcs"], st["lands"]
    outs = pl.pallas_call(
        body, name=name, out_shape=[pltpu.HBM(s.shape, s.dtype) for s in list(srcs) + list(lands)],
        in_specs=[hbm] * (2 * n) + [sem] * (5 * n) + [pl.BlockSpec(memory_space=pl.ANY)], out_specs=[hbm] * (2 * n),
        input_output_aliases={a: a for a in range(2 * n)},
        compiler_params=pltpu.CompilerParams(has_side_effects=pltpu.SideEffectType.DATAFLOW_SIDE_EFFECTING),
    )(*srcs, *lands, *st["send"], *st["recv_sib"], *st["local"], *st["fsend"], *st["frecv"], after)
    return outs[n:]


def _send_start(srcs, blocked, name):
    n = len(srcs)
    MESH = pl.DeviceIdType.MESH

    def body(*refs):
        src_refs, land_refs = refs[:n], refs[n:2 * n]
        send_sems, recv_sems, local_sems = refs[2 * n:3 * n], refs[3 * n:4 * n], refs[4 * n:5 * n]
        token = refs[7 * n]
        x, y, c = _mesh_pos()
        me = 4 * x + 2 * y + c
        for a in range(n):
            for k, peer, blk in _peers(x, y, c):
                pltpu.make_async_remote_copy(
                    src_ref=src_refs[a].at[blk] if blocked else src_refs[a], dst_ref=land_refs[a].at[me],
                    send_sem=send_sems[a].at[k], recv_sem=recv_sems[a].at[k], device_id=peer,
                    device_id_type=MESH).start()
            pltpu.make_async_copy(src_refs[a].at[me] if blocked else src_refs[a], land_refs[a].at[me],
                                  local_sems[a]).start()
        token[...] = jnp.zeros_like(token)

    hbm = pl.BlockSpec(memory_space=pltpu.HBM)
    sem = pl.BlockSpec(memory_space=pltpu.SEMAPHORE)
    lshape = [s.shape if blocked else (N_DEV,) + s.shape for s in srcs]
    outs = pl.pallas_call(
        body, name=name,
        out_shape=([pltpu.SemaphoreType.DMA((N_DEV - 1,))] * (2 * n) + [pltpu.SemaphoreType.DMA(())] * n
                   + [pltpu.HBM(s.shape, s.dtype) for s in srcs]
                   + [pltpu.HBM(ls, s.dtype) for ls, s in zip(lshape, srcs)] + [jax.ShapeDtypeStruct((8, 128), F32)]),
        in_specs=[hbm] * (2 * n), out_specs=[sem] * (3 * n) + [hbm] * (2 * n) + [pl.BlockSpec(memory_space=pltpu.VMEM)],
        input_output_aliases={**{a: 3 * n + a for a in range(n)}, **{n + a: 4 * n + a for a in range(n)}},
        compiler_params=pltpu.CompilerParams(has_side_effects=pltpu.SideEffectType.DATAFLOW_SIDE_EFFECTING),
    )(*[pltpu.with_memory_space_constraint(s, pltpu.HBM) for s in srcs],
      *[pltpu.with_memory_space_constraint(lax.empty(ls, s.dtype), pltpu.HBM) for ls, s in zip(lshape, srcs)])
    return dict(send=outs[:n], recv=outs[n:2 * n], local=outs[2 * n:3 * n], srcs=outs[3 * n:4 * n],
                lands=outs[4 * n:5 * n], token=outs[5 * n], blocked=blocked)


def _send_wait(started, which, after, name):
    n = len(which)
    blocked = started["blocked"]
    pick = lambda key: [started[key][a] for a in which]
    MESH = pl.DeviceIdType.MESH

    def body(*refs):
        src_refs, land_refs = refs[:n], refs[n:2 * n]
        send, recv, local = refs[2 * n:3 * n], refs[3 * n:4 * n], refs[4 * n:5 * n]
        x, y, c = _mesh_pos()
        me = 4 * x + 2 * y + c
        for a in range(n):
            for k, peer, blk in _peers(x, y, c):
                cp = pltpu.make_async_remote_copy(
                    src_ref=src_refs[a].at[blk] if blocked else src_refs[a], dst_ref=land_refs[a].at[blk],
                    send_sem=send[a].at[k], recv_sem=recv[a].at[k], device_id=peer, device_id_type=MESH)
                cp.wait_send()
                cp.wait_recv()
            pltpu.make_async_copy(src_refs[a].at[me] if blocked else src_refs[a], land_refs[a].at[me],
                                  local[a]).wait()

    hbm = pl.BlockSpec(memory_space=pltpu.HBM)
    sem = pl.BlockSpec(memory_space=pltpu.SEMAPHORE)
    srcs, lands = pick("srcs"), pick("lands")
    outs = pl.pallas_call(
        body, name=name, out_shape=[pltpu.HBM(s.shape, s.dtype) for s in srcs + lands],
        in_specs=[hbm] * (2 * n) + [sem] * (3 * n) + [pl.BlockSpec(memory_space=pl.ANY)], out_specs=[hbm] * (2 * n),
        input_output_aliases={a: a for a in range(2 * n)},
        compiler_params=pltpu.CompilerParams(has_side_effects=pltpu.SideEffectType.DATAFLOW_SIDE_EFFECTING),
    )(*srcs, *lands, *pick("send"), *pick("recv"), *pick("local"), after)
    return outs[n:]


def _adamw_weight(parts, w, m, v, name):
    L, R, C = w.shape
    tr = max(t for t in range(16, min(R, 256) + 1, 16) if R % t == 0)
    nr = R // tr

    def body(*refs):
        p_refs = refs[:L]
        w_ref, m_ref, v_ref, g_ref, d_ref, nm_ref, nv_ref, gsum = refs[L:]
        layer = pl.program_id(0)
        for l in range(L):
            @pl.when(layer == l)
            def _(l=l):
                g = p_refs[l][0].astype(F32)
                for s in range(1, N_DEV):
                    g = g + p_refs[l][s].astype(F32)
                gsum[...] = g
        g = gsum[...]
        g_ref[...] = g
        d_ref[...], nm_ref[...], nv_ref[...] = _adamw_math(w_ref[...], g, m_ref[...], v_ref[...])

    def part_spec(l):
        return pl.BlockSpec((N_DEV, tr, C), lambda layer, i: (0, jnp.where(layer == l, i, jnp.where(layer < l, 0, nr - 1)), 0))

    ws = pl.BlockSpec((None, tr, C), lambda layer, i: (layer, i, 0))
    return pl.pallas_call(
        body, name=name, grid=(L, nr), in_specs=[part_spec(l) for l in range(L)] + [ws] * 3, out_specs=[ws] * 4,
        out_shape=[jax.ShapeDtypeStruct((L, R, C), F32)] * 4, scratch_shapes=[pltpu.VMEM((tr, C), F32)],
        compiler_params=pltpu.CompilerParams(dimension_semantics=("arbitrary", "arbitrary")),
    )(*parts, w, m, v)


def _seg_rows(n, align):
    rows = -(-n // ROW)
    return -(-rows // align) * align


def _pack(arrs, align, lead=()):
    parts = []
    for a in arrs:
        n = int(np.prod(a.shape[len(lead):]))
        rows = _seg_rows(n, align)
        flat = a.reshape(lead + (n,))
        flat = jnp.pad(flat, [(0, 0)] * len(lead) + [(0, rows * ROW - n)])
        parts.append(flat.reshape(lead + (rows, ROW)))
    return jnp.concatenate(parts, axis=len(lead))


def _unpack(buf, shapes, align, lead=()):
    out, r0 = [], 0
    for shp in shapes:
        n = int(np.prod(shp))
        rows = _seg_rows(n, align)
        seg = lax.slice_in_dim(buf, r0, r0 + rows, axis=len(lead))
        out.append(seg.reshape(lead + (rows * ROW,))[..., :n].reshape(lead + tuple(shp)))
        r0 += rows
    return out


def kernel(x, p, norm_mix, norm_ffn, norm_ple, norm_kv, norm_final, a_w_in, a_norm_v, a_w_s, a_b_s, a_w_out, w_kv, b_w_q, b_sinks, b_w_o, f_w_up, f_conv_w, f_conv_b, f_w_down, ple_w_in, ple_w_gate, ple_b_gate, loss_target, m_norm_mix, m_norm_ffn, m_norm_ple, m_norm_kv, m_norm_final, m_a_w_in, m_a_norm_v, m_a_w_s, m_a_b_s, m_a_w_out, m_w_kv, m_b_w_q, m_b_sinks, m_b_w_o, m_f_w_up, m_f_conv_w, m_f_conv_b, m_f_w_down, m_ple_w_in, m_ple_w_gate, m_ple_b_gate, v_norm_mix, v_norm_ffn, v_norm_ple, v_norm_kv, v_norm_final, v_a_w_in, v_a_norm_v, v_a_w_s, v_a_b_s, v_a_w_out, v_w_kv, v_b_w_q, v_b_sinks, v_b_w_o, v_f_w_up, v_f_conv_w, v_f_conv_b, v_f_w_down, v_ple_w_in, v_ple_w_gate, v_ple_b_gate):
    w = dict(norm_mix=norm_mix, norm_ffn=norm_ffn, norm_ple=norm_ple, norm_kv=norm_kv, norm_final=norm_final,
             a_w_in=a_w_in, a_norm_v=a_norm_v, a_w_s=a_w_s, a_b_s=a_b_s, a_w_out=a_w_out, w_kv=w_kv, b_w_q=b_w_q,
             b_sinks=b_sinks, b_w_o=b_w_o, f_w_up=f_w_up, f_conv_w=f_conv_w, f_conv_b=f_conv_b, f_w_down=f_w_down,
             ple_w_in=ple_w_in, ple_w_gate=ple_w_gate, ple_b_gate=ple_b_gate)
    mom_m = dict(norm_mix=m_norm_mix, norm_ffn=m_norm_ffn, norm_ple=m_norm_ple, norm_kv=m_norm_kv,
                 norm_final=m_norm_final, a_w_in=m_a_w_in, a_norm_v=m_a_norm_v, a_w_s=m_a_w_s, a_b_s=m_a_b_s,
                 a_w_out=m_a_w_out, w_kv=m_w_kv, b_w_q=m_b_w_q, b_sinks=m_b_sinks, b_w_o=m_b_w_o, f_w_up=m_f_w_up,
                 f_conv_w=m_f_conv_w, f_conv_b=m_f_conv_b, f_w_down=m_f_w_down, ple_w_in=m_ple_w_in,
                 ple_w_gate=m_ple_w_gate, ple_b_gate=m_ple_b_gate)
    mom_v = dict(norm_mix=v_norm_mix, norm_ffn=v_norm_ffn, norm_ple=v_norm_ple, norm_kv=v_norm_kv,
                 norm_final=v_norm_final, a_w_in=v_a_w_in, a_norm_v=v_a_norm_v, a_w_s=v_a_w_s, a_b_s=v_a_b_s,
                 a_w_out=v_a_w_out, w_kv=v_w_kv, b_w_q=v_b_w_q, b_sinks=v_b_sinks, b_w_o=v_b_w_o, f_w_up=v_f_w_up,
                 f_conv_w=v_f_conv_w, f_conv_b=v_f_conv_b, f_w_down=v_f_w_down, ple_w_in=v_ple_w_in,
                 ple_w_gate=v_ple_w_gate, ple_b_gate=v_ple_b_gate)
    S = x.shape[1]
    tm = min(256, S)
    tl = min(512, S)
    me = 4 * lax.axis_index("x") + 2 * lax.axis_index("y") + lax.axis_index("c")

    def after_token(tok, a):
        return a + tok[0:1, 0:1].reshape((1,) * a.ndim)

    wb = {n: w[n].astype(BF) for n in _BIG}
    order = [('a_w_in', 0), ('a_w_out', 0), ('a_norm_v', None), ('f_conv_w', 0), ('f_conv_w', 1),
             ('f_w_up', 0), ('f_w_down', 0),
             ('ple_w_in', 0), ('ple_w_gate', 0), ('w_kv', None), ('b_w_q', 0), ('b_w_o', 0),
             ('f_w_up', 1), ('f_w_down', 1), ('ple_w_in', 1), ('ple_w_gate', 1)]
    src = lambda n, l: (wb[n] if n in wb else w[n]) if l is None else (wb[n] if n in wb else w[n])[l]
    groups = [order[:5], order[5:7], order[7:12], order[12:]]
    g = dict(zip(groups[0], _all_gather_now([src(n, l) for n, l in groups[0]], "gather_mix")))

    def start_group(keys, tie_to, name):
        srcs = [src(n, l) for n, l in keys]
        k = min(range(len(srcs)), key=lambda i: srcs[i].size)
        zero = (tie_to[(0,) * (tie_to.ndim - 1)][0:1] * 0).astype(srcs[k].dtype)
        srcs[k] = srcs[k] + zero.reshape((1,) * srcs[k].ndim)
        return _send_start(srcs, False, name)

    def finish_group(started, keys, after, name):
        return dict(zip(keys, _send_wait(started, list(range(len(keys))), after, name)))

    x0 = x[0]
    win = g[('a_w_in', 0)].transpose(1, 0, 2).reshape(D, 2 * AW)
    wout = g[('a_w_out', 0)].reshape(AW, D)
    gv = g[('a_norm_v', None)].reshape(1, AW)
    cw = [g[('f_conv_w', l)] for l in range(2)]
    cb = [w['f_conv_b'][l].reshape(N_DEV, 1, FF_BLK) for l in range(2)]
    ws = a_w_s[0].astype(BF)
    wst = jnp.swapaxes(ws, 1, 2)
    bsb = jnp.broadcast_to(a_b_s[0][:, :, None], (GROUPS, CHUNK, CHUNK))
    row = lambda a: a.reshape(1, -1)
    pb = [p[l, 0].astype(BF) for l in range(2)]
    sinks = b_sinks.reshape(N_Q)
    consts = _attn_consts()

    srcs1 = [src(n, l) for n, l in groups[1]]
    srcs1[1] = srcs1[1] + (g[('a_w_out', 0)][0, 0:1, 0:1] * 0).astype(srcs1[1].dtype)
    s1 = _gather2_start(srcs1, "gather_start_ffn0")
    h1, zg0, gp0 = _mixa_fwd(x0, after_token(s1["token"], row(norm_mix[0])), win, gv, ws, bsb, wout, tl)
    s1 = _gather2_forward(s1, h1, "gather_forward_ffn0")
    g = dict(zip(groups[1], _gather2_wait(s1, h1, "gather_wait_ffn0")))
    wup, wdown = [g[('f_w_up', 0)], None], [g[('f_w_down', 0)].reshape(D_FF, D), None]
    s2 = start_group(groups[2], g[('f_w_down', 0)], "gather_start_mid")
    h2, hh0, cc0 = _ffn_fwd(h1, after_token(s2["token"], row(norm_ffn[0])), wup[0], cw[0], cb[0], wdown[0], tm)
    g = finish_group(s2, groups[2], h2, "gather_wait_mid")
    wple = [g[('ple_w_in', 0)].transpose(1, 0, 2).reshape(PLE, D), None]
    wgate = [g[('ple_w_gate', 0)].reshape(D, D), None]
    wkv, wq, wo = g[('w_kv', None)].reshape(D, 2 * N_KV * HD), g[('b_w_q', 0)].reshape(D, D), g[('b_w_o', 0)].reshape(D, D)
    s3 = start_group(groups[3], g[('ple_w_in', 0)], "gather_start_l1")
    h3 = _ple_fwd(h2, after_token(s3["token"], row(norm_ple[0])), wgate[0], row(ple_b_gate[0]), pb[0], wple[0], tl)
    q, kv = _qkv_fwd(h3, row(norm_mix[1]), wq, row(norm_kv), wkv, tl)
    att = _attn_fwd(q, kv, sinks, consts)
    h4 = _oproj_fwd(att, wo, h3, tl)
    g = finish_group(s3, groups[3], h4, "gather_wait_l1")
    wup[1], wdown[1] = g[('f_w_up', 1)], g[('f_w_down', 1)].reshape(D_FF, D)
    wple[1], wgate[1] = g[('ple_w_in', 1)].transpose(1, 0, 2).reshape(PLE, D), g[('ple_w_gate', 1)].reshape(D, D)
    h5, hh1, cc1 = _ffn_fwd(h4, row(norm_ffn[1]), wup[1], cw[1], cb[1], wdown[1], tm)
    h6 = _ple_fwd(h5, row(norm_ple[1]), wgate[1], row(ple_b_gate[1]), pb[1], wple[1], tl)

    def col_blocks(g2d, n):
        return g2d.reshape(g2d.shape[0], N_DEV, n).transpose(1, 0, 2)

    def row_blocks(g2d):
        return g2d.reshape(N_DEV, g2d.shape[0] // N_DEV, g2d.shape[1])

    def ple_grads(l, de, xp, dpre):
        return [col_blocks(_wgrad(pb[l], de, f"wgrad_ple_in{l}", tn=1024), D // N_DEV),
                row_blocks(_wgrad(xp, dpre, f"wgrad_gate{l}", tn=1024))]

    def ffn_grads(l, xf, dhh, act, dhb):
        return [_wgrad(xf, dhh, f"wgrad_up{l}", tn=FF_BLK, out_blocked=True, a_transposed=True),
                row_blocks(_wgrad(act, dhb, f"wgrad_down{l}", tn=1024))]

    dh6, loss_blk, d_norm_final = _head(h6, row(norm_final), loss_target[0], tl)
    dh5, de1, xp1, dpre1, dbg1, dgp1 = _ple_bwd(dh6, h5, row(norm_ple[1]), wgate[1], row(ple_b_gate[1]), pb[1],
                                                 wple[1], tl)
    ex_ple1 = _send_start(ple_grads(1, de1, xp1, dpre1), True, "send_ple1")
    dh4, xf1, dhh1, act1, dhb5, dcw1, dcb1, dgf1 = _ffn_bwd(dh5, h4, hh1, cc1, after_token(ex_ple1["token"], row(norm_ffn[1])),
                                                            wup[1], cw[1], wdown[1], tm)
    ex_ffn1 = _send_start(ffn_grads(1, xf1, dhh1, act1, dhb5), True, "send_ffn1")
    datt, dhb4 = _oproj_bwd(dh4, wo, tl)
    dq, dkv, dsink = _attn_bwd(q, kv, att, datt, after_token(ex_ffn1["token"], sinks.reshape(1, N_Q)).reshape(N_Q), consts)
    dh3, xq, xkv, dgq, dgkv = _qkv_bwd(dh4, h3, dq, dkv, row(norm_mix[1]), wq, row(norm_kv), wkv, tl)
    ex_att = _send_start([row_blocks(_wgrad(att, dhb4, "wgrad_o", tn=1024)),
                              row_blocks(_wgrad(xq, dq, "wgrad_q", tn=1024)),
                              row_blocks(_wgrad(xkv, dkv, "wgrad_kv", tn=512))], True, "send_att")
    dh2, de0, xp0, dpre0, dbg0, dgp0 = _ple_bwd(dh3, h2, after_token(ex_att["token"], row(norm_ple[0])), wgate[0],
                                                 row(ple_b_gate[0]), pb[0], wple[0], tl)
    ex_ple0 = _send_start(ple_grads(0, de0, xp0, dpre0), True, "send_ple0")
    dh1, xf0, dhh0, act0, dhb2, dcw0, dcb0, dgf0 = _ffn_bwd(dh2, h1, hh0, cc0, after_token(ex_ple0["token"], row(norm_ffn[0])),
                                                            wup[0], cw[0], wdown[0], tm)
    ex_ffn0 = _send_start(ffn_grads(0, xf0, dhh0, act0, dhb2), True, "send_ffn0")
    dx, xa, dz, gated, dhb1, dws, dbs, dgv, dga = _mixa_bwd(dh1, x0, zg0, gp0, after_token(ex_ffn0["token"], row(norm_mix[0])), win, gv,
                                                            ws, wst, bsb, wout, tl)
    small_g = {
        'norm_mix': jnp.concatenate([dga, dgq], axis=0), 'norm_ffn': jnp.concatenate([dgf0, dgf1], axis=0),
        'norm_ple': jnp.concatenate([dgp0, dgp1], axis=0), 'norm_kv': dgkv.reshape(D),
        'norm_final': d_norm_final.reshape(D), 'a_w_s': dws[None], 'a_b_s': dbs[None, :, :, 0],
        'b_sinks': dsink[:, 0].reshape(1, N_Q),
        'f_conv_b': jnp.stack([dcb0.reshape(2 * D_FF), dcb1.reshape(2 * D_FF)]),
        'ple_b_gate': jnp.concatenate([dbg0, dbg1], axis=0),
        'a_norm_v': dgv, 'f_conv_w': jnp.stack([dcw0, dcw1], axis=1),
    }
    names_all = _REPL + _SMALL_SHARDED
    full_shapes = [small_g[n].shape for n in names_all]
    small_packed = _pack([small_g[n] for n in names_all], 8)
    ex_small = _send_start([small_packed.reshape(N_DEV, small_packed.shape[0] // N_DEV, ROW)], True, "send_small")
    ex_a_in = _send_start([_wgrad(xa, dz, "wgrad_a_in", tn=2 * AW // N_DEV, out_blocked=True, a_transposed=True,
                                  per_step=4)], True, "send_a_in")
    ex_a_out = _send_start([row_blocks(_wgrad(gated, dhb1, "wgrad_a_out", tn=1024))],
                           True, "send_a_out")
    small_sum = _sum8(_send_wait(ex_small, [0], ex_a_out["token"], "recv_small")[0], "sum_small_grads")
    ex_small_all = _send_start([small_sum], False, "gather_small_sums")

    def as3(a):
        return a.reshape((-1,) + a.shape[-2:])

    def adam(name, parts):
        outs = _adamw_weight(parts, as3(w[name]), as3(mom_m[name]), as3(mom_v[name]), "adamw_" + name)
        return [o.reshape(w[name].shape) for o in outs]

    big = {}
    l_ple1 = _send_wait(ex_ple1, list(range(len(ex_ple1["lands"]))), ex_small_all["token"], "recv_ple1")
    l_ffn1 = _send_wait(ex_ffn1, list(range(len(ex_ffn1["lands"]))), l_ple1[0], "recv_ffn1")
    l_att = _send_wait(ex_att, list(range(len(ex_att["lands"]))), l_ffn1[0], "recv_att")
    big['b_w_o'], big['b_w_q'], big['w_kv'] = adam('b_w_o', [l_att[0]]), adam('b_w_q', [l_att[1]]), adam('w_kv', [l_att[2]])
    l_ple0 = _send_wait(ex_ple0, list(range(len(ex_ple0["lands"]))), big['w_kv'][0], "recv_ple0")
    big['ple_w_in'] = adam('ple_w_in', [l_ple0[0], l_ple1[0]])
    big['ple_w_gate'] = adam('ple_w_gate', [l_ple0[1], l_ple1[1]])
    l_ffn0 = _send_wait(ex_ffn0, list(range(len(ex_ffn0["lands"]))), big['ple_w_gate'][0], "recv_ffn0")
    big['f_w_up'] = adam('f_w_up', [l_ffn0[0], l_ffn1[0]])
    big['f_w_down'] = adam('f_w_down', [l_ffn0[1], l_ffn1[1]])
    big['a_w_in'] = adam('a_w_in', _send_wait(ex_a_in, [0], big['f_w_down'][0], "recv_a_in"))
    big['a_w_out'] = adam('a_w_out', _send_wait(ex_a_out, [0], big['a_w_in'][0], "recv_a_out"))
    g_big, d_big, m_big, v_big = [{n: big[n][i] for n in _BIG} for i in range(4)]

    gsum = _send_wait(ex_small_all, [0], big['a_w_out'][0], "recv_small_sums")[0].reshape(small_packed.shape)
    gs = dict(zip(names_all, _unpack(gsum, full_shapes, 8)))
    gs['a_norm_v'] = lax.dynamic_slice_in_dim(gs['a_norm_v'], me * (AW // N_DEV), AW // N_DEV, axis=1)
    gs['f_conv_w'] = lax.dynamic_index_in_dim(gs['f_conv_w'], me, axis=0, keepdims=False)
    gs = {n: gs[n].reshape(w[n].shape) for n in names_all}
    packs = lambda d: _pack([d[n] for n in names_all], 8)
    souts = _adamw_small(packs(gs), packs(w), packs(mom_m), packs(mom_v), "adamw_small")
    shp = [w[n].shape for n in names_all]
    d_s, m_s, v_s = [dict(zip(names_all, _unpack(o, shp, 8))) for o in souts]

    grads = {**g_big, **gs}
    delta = {**d_big, **d_s}
    new_m = {**m_big, **m_s}
    new_v = {**v_big, **v_s}
    loss = lax.psum(loss_blk[0, 0], ("x", "y", "c"))
    return (loss, dx[None], *[grads[n] for n in _PARAMS], *[delta[n] for n in _PARAMS],
            *[new_m[n] for n in _PARAMS], *[new_v[n] for n in _PARAMS])
```

```python
import functools

import numpy as np
import jax
import jax.numpy as jnp
from jax import lax
from jax.experimental import pallas as pl
from jax.experimental.pallas import tpu as pltpu

F32 = jnp.float32
BF = jnp.bfloat16

N_DEV = 8
D = 1024
AW = 1024
GROUPS = 8
CHUNK = 128
HD = 64
N_Q = 16
N_KV = 4
GQA = 4
BLOCK = 128
D_FF = 2816
FF_BLK = 704
N_FF_BLK = 4
PLE = 256
EPS = 1e-6
LR, B1, B2, ADAM_EPS, WD, STEP = 0.001, 0.9, 0.999, 1e-08, 0.01, 10
GELU_C0 = 0.7978845608028654
GELU_C1 = 0.044715
NEG = -1e30
ROW = 1024
VMEM_BIG = 56 * 1024 * 1024

_PARAMS = ['norm_mix', 'norm_ffn', 'norm_ple', 'norm_kv', 'norm_final', 'a_w_in', 'a_norm_v', 'a_w_s', 'a_b_s',
           'a_w_out', 'w_kv', 'b_w_q', 'b_sinks', 'b_w_o', 'f_w_up', 'f_conv_w', 'f_conv_b', 'f_w_down',
           'ple_w_in', 'ple_w_gate', 'ple_b_gate']
_BIG = ['a_w_in', 'a_w_out', 'w_kv', 'b_w_q', 'b_w_o', 'f_w_up', 'f_w_down', 'ple_w_in', 'ple_w_gate']
_SMALL_SHARDED = ['a_norm_v', 'f_conv_w']
_REPL = ['norm_mix', 'norm_ffn', 'norm_ple', 'norm_kv', 'norm_final', 'a_w_s', 'a_b_s', 'b_sinks', 'f_conv_b',
         'ple_b_gate']


def _mm(a, b):
    return jnp.dot(a, b, preferred_element_type=F32)


def _mm_nt(a, b):
    return lax.dot_general(a, b, (((1,), (1,)), ((), ())), preferred_element_type=F32)


def _mm_tn(a, b):
    return lax.dot_general(a, b, (((0,), (0,)), ((), ())), preferred_element_type=F32)


def _const_spec(shape, single=False):
    nd = len(shape)
    if single:
        return pl.BlockSpec(shape, lambda *_: (0,) * nd, pipeline_mode=pl.Buffered(1))
    return pl.BlockSpec(shape, lambda *_: (0,) * nd)


def _row_spec(tm, n):
    return pl.BlockSpec((tm, n), lambda i: (i, 0))


def _rms_fwd(x, g):
    r = lax.rsqrt(jnp.mean(x * x, axis=-1, keepdims=True) + EPS)
    xhat = x * r
    return xhat * g, xhat, r


def _rms_bwd(dy, xhat, r, g):
    a = dy * g
    dx = r * (a - xhat * jnp.mean(a * xhat, axis=-1, keepdims=True))
    return dx, jnp.sum(dy * xhat, axis=0, keepdims=True)


def _sigmoid(x):
    return 1.0 / (1.0 + jnp.exp(-x))


def _params(vmem=None):
    return pltpu.CompilerParams(dimension_semantics=("arbitrary",), vmem_limit_bytes=vmem)


def _tri_masks():
    t = lax.broadcasted_iota(jnp.int32, (CHUNK, CHUNK), 0)
    s = lax.broadcasted_iota(jnp.int32, (CHUNK, CHUNK), 1)
    return t >= s, s >= t


def _mixa_fwd(x, g, win, gv, ws, bsb, wout, tm):
    S = x.shape[0]
    nc = tm // CHUNK

    def body(x_ref, g_ref, win_ref, gv_ref, ws_ref, bsb_ref, wout_ref, h_ref, zg_ref, gp_ref, gated):
        x = x_ref[...]
        xn, _, _ = _rms_fwd(x, g_ref[...])
        z = _mm(xn.astype(BF), win_ref[...])
        z2 = z * z
        t = jnp.tanh(GELU_C0 * (z + GELU_C1 * z2 * z))
        zg = 0.5 * z * (1.0 + t)
        zg_ref[...] = zg.astype(BF)
        gp_ref[...] = (0.5 * (1.0 + t) + 0.5 * z * (1.0 - t * t) * (GELU_C0 * (1.0 + 3.0 * GELU_C1 * z2))).astype(BF)
        u = zg[:, :AW]
        vn, _, _ = _rms_fwd(zg[:, AW:], gv_ref[...])
        vn = vn.astype(BF)
        tril, _ = _tri_masks()
        for h in range(GROUPS):
            wm = jnp.where(tril, ws_ref[h], jnp.zeros((), BF))
            for c in range(nc):
                rs, cs = slice(c * CHUNK, (c + 1) * CHUNK), slice(h * CHUNK, (h + 1) * CHUNK)
                s = _mm(wm, vn[rs, cs]) + bsb_ref[h]
                gated[rs, cs] = (u[rs, cs] * s).astype(BF)
        h_ref[...] = x + _mm(gated[...], wout_ref[...])

    return pl.pallas_call(
        body, name="mixa_fwd", grid=(S // tm,),
        in_specs=[_row_spec(tm, D), _const_spec((1, D)), _const_spec((D, 2 * AW)), _const_spec((1, AW)),
                  _const_spec((GROUPS, CHUNK, CHUNK)), _const_spec((GROUPS, CHUNK, CHUNK)), _const_spec((AW, D))],
        out_specs=[_row_spec(tm, D), _row_spec(tm, 2 * AW), _row_spec(tm, 2 * AW)],
        out_shape=[jax.ShapeDtypeStruct((S, D), F32), jax.ShapeDtypeStruct((S, 2 * AW), BF),
                   jax.ShapeDtypeStruct((S, 2 * AW), BF)],
        scratch_shapes=[pltpu.VMEM((tm, AW), BF)],
        compiler_params=_params(VMEM_BIG),
    )(x, g, win, gv, ws, bsb, wout)


def _mixa_bwd(dh, x, zg, gp, g, win, gv, ws, wst, bsb, wout, tm):
    S = x.shape[0]
    nt = S // tm
    nc = tm // CHUNK

    def body(dh_ref, x_ref, zg_ref, gp_ref, g_ref, win_ref, gv_ref, ws_ref, wst_ref, bsb_ref, wout_ref,
             dx_ref, xn_ref, dz_ref, gated_ref, dhb_ref, dws_ref, dbs_ref, dgv_ref, dg_ref, du_scr, dvn_scr):
        i = pl.program_id(0)

        @pl.when(i == 0)
        def _():
            dws_ref[...] = jnp.zeros_like(dws_ref)
            dbs_ref[...] = jnp.zeros_like(dbs_ref)
            dgv_ref[...] = jnp.zeros_like(dgv_ref)
            dg_ref[...] = jnp.zeros_like(dg_ref)

        x = x_ref[...]
        gmix = g_ref[...]
        xn, xhat, r = _rms_fwd(x, gmix)
        xn_ref[...] = xn.T.astype(BF)
        zg = zg_ref[...].astype(F32)
        gp = gp_ref[...].astype(F32)
        u = zg[:, :AW]
        gvv = gv_ref[...]
        vn, vhat, rv = _rms_fwd(zg[:, AW:], gvv)
        vn = vn.astype(BF)
        dh = dh_ref[...]
        dhb = dh.astype(BF)
        dhb_ref[...] = dhb
        dgated = _mm_nt(dhb, wout_ref[...])
        tril, triu = _tri_masks()
        for h in range(GROUPS):
            wm = jnp.where(tril, ws_ref[h], jnp.zeros((), BF))
            wmt = jnp.where(triu, wst_ref[h], jnp.zeros((), BF))
            for c in range(nc):
                rs, cs = slice(c * CHUNK, (c + 1) * CHUNK), slice(h * CHUNK, (h + 1) * CHUNK)
                vnb = vn[rs, cs]
                s = _mm(wm, vnb) + bsb_ref[h]
                gated_ref[rs, cs] = (u[rs, cs] * s).astype(BF)
                dgt = dgated[rs, cs]
                du_scr[rs, cs] = dgt * s
                ds = dgt * u[rs, cs]
                dsb = ds.astype(BF)
                dws_ref[h] += jnp.where(tril, _mm_nt(dsb, vnb), 0.0)
                dbs_ref[h] += ds
                dvn_scr[rs, cs] = _mm(wmt, dsb)
        dvn = dvn_scr[...]
        dv, dgv = _rms_bwd(dvn, vhat, rv, gvv)
        dgv_ref[...] += dgv
        dz_ref[:, :AW] = (du_scr[...] * gp[:, :AW]).astype(BF)
        dz_ref[:, AW:] = (dv * gp[:, AW:]).astype(BF)
        dxn = _mm_nt(dz_ref[...], win_ref[...])
        dxr, dg = _rms_bwd(dxn, xhat, r, gmix)
        dg_ref[...] += dg
        dx_ref[...] = dh + dxr

        @pl.when(i == nt - 1)
        def _():
            for h in range(GROUPS):
                dbs_ref[h] = jnp.broadcast_to(jnp.sum(dbs_ref[h], axis=-1, keepdims=True), (CHUNK, CHUNK))

    gshape = (GROUPS, CHUNK, CHUNK)
    return pl.pallas_call(
        body, name="mixa_bwd", grid=(nt,),
        in_specs=[_row_spec(tm, D), _row_spec(tm, D), _row_spec(tm, 2 * AW), _row_spec(tm, 2 * AW), _const_spec((1, D)),
                  _const_spec((D, 2 * AW)), _const_spec((1, AW)), _const_spec(gshape), _const_spec(gshape),
                  _const_spec(gshape), _const_spec((AW, D))],
        out_specs=[_row_spec(tm, D), pl.BlockSpec((D, tm), lambda i: (0, i)), _row_spec(tm, 2 * AW), _row_spec(tm, AW),
                   _row_spec(tm, D), _const_spec(gshape), _const_spec(gshape), _const_spec((1, AW)), _const_spec((1, D))],
        out_shape=[jax.ShapeDtypeStruct((S, D), F32), jax.ShapeDtypeStruct((D, S), BF),
                   jax.ShapeDtypeStruct((S, 2 * AW), BF), jax.ShapeDtypeStruct((S, AW), BF),
                   jax.ShapeDtypeStruct((S, D), BF), jax.ShapeDtypeStruct(gshape, F32),
                   jax.ShapeDtypeStruct(gshape, F32), jax.ShapeDtypeStruct((1, AW), F32),
                   jax.ShapeDtypeStruct((1, D), F32)],
        scratch_shapes=[pltpu.VMEM((tm, AW), F32), pltpu.VMEM((tm, AW), F32)],
        compiler_params=_params(VMEM_BIG),
    )(dh, x, zg, gp, g, win, gv, ws, wst, bsb, wout)


def _shift_down(x, prev8, k):
    n, c = x.shape[0] // 8, x.shape[1]
    r = pltpu.roll(x.reshape(n, 8, c), k, axis=1)
    before = jnp.concatenate([pltpu.roll(prev8.reshape(1, 8, c), k, axis=1), r[:-1]], axis=0)
    sub = lax.broadcasted_iota(jnp.int32, (1, 8, c), 1)
    return jnp.where(sub < k, before, r).reshape(x.shape)


def _shift_up(x, next8, k):
    n, c = x.shape[0] // 8, x.shape[1]
    r = pltpu.roll(x.reshape(n, 8, c), 8 - k, axis=1)
    after = jnp.concatenate([r[1:], pltpu.roll(next8.reshape(1, 8, c), 8 - k, axis=1)], axis=0)
    sub = lax.broadcasted_iota(jnp.int32, (1, 8, c), 1)
    return jnp.where(sub >= 8 - k, after, r).reshape(x.shape)


def _ffn_fwd(h, g, wup, cw, cb, wdown, tm):
    S = h.shape[0]

    def body(h_ref, g_ref, wup_ref, cw_ref, cb_ref, wdown_ref, o_ref, hh_ref, cc_ref, sh):
        i = pl.program_id(0)

        @pl.when(i == 0)
        def _():
            sh[...] = jnp.zeros_like(sh)

        x = h_ref[...]
        xn, _, _ = _rms_fwd(x, g_ref[...])
        xn = xn.astype(BF)
        acc = x
        up = lambda d: [_mm(xn, wup_ref[d]), _mm(xn, wup_ref[N_FF_BLK + d])]
        ahead = up(0)
        down = lambda d, act: _mm(act, wdown_ref[d * FF_BLK:(d + 1) * FF_BLK, :])
        behind = None
        for d in range(N_FF_BLK):
            cs = []
            hbs = ahead
            if d + 1 < N_FF_BLK:
                ahead = up(d + 1)
            if behind is not None:
                acc = acc + down(d - 1, behind)
            for blk, hb in zip((d, N_FF_BLK + d), hbs):
                hh_ref[blk] = hb.astype(BF)
                w, prev8 = cw_ref[blk], sh[blk]
                c = cb_ref[blk] + w[0:1] * _shift_down(hb, prev8, 2) + w[1:2] * _shift_down(hb, prev8, 1) + w[2:3] * hb
                sh[blk] = hb[tm - 8:, :]
                cc_ref[blk] = c.astype(BF)
                cs.append(c)
            behind = (cs[0] * _sigmoid(cs[0]) * cs[1]).astype(BF)
        o_ref[...] = acc + down(N_FF_BLK - 1, behind)

    blk3 = pl.BlockSpec((N_DEV, tm, FF_BLK), lambda i: (0, i, 0))
    return pl.pallas_call(
        body, name="ffn_fwd", grid=(S // tm,),
        in_specs=[_row_spec(tm, D), _const_spec((1, D)), _const_spec((N_DEV, D, FF_BLK), True),
                  _const_spec((N_DEV, 3, FF_BLK)), _const_spec((N_DEV, 1, FF_BLK)), _const_spec((D_FF, D), True)],
        out_specs=[_row_spec(tm, D), blk3, blk3],
        out_shape=[jax.ShapeDtypeStruct((S, D), F32), jax.ShapeDtypeStruct((N_DEV, S, FF_BLK), BF),
                   jax.ShapeDtypeStruct((N_DEV, S, FF_BLK), BF)],
        scratch_shapes=[pltpu.VMEM((N_DEV, 8, FF_BLK), F32)],
        compiler_params=_params(VMEM_BIG),
    )(h, g, wup, cw, cb, wdown)


def _ffn_bwd(dh, h, hh, cc, g, wup, cw, wdown, tm):
    S = h.shape[0]
    nt = S // tm

    def rev(i):
        return (nt - 1 - i, 0)

    def body(dh_ref, h_ref, hh_ref, cc_ref, g_ref, wup_ref, cw_ref, wdown_ref,
             dx_ref, xn_ref, dhh_ref, act_ref, dhb_ref, dcw_ref, dcb_ref, dg_ref, sh):
        i = pl.program_id(0)

        @pl.when(i == 0)
        def _():
            sh[...] = jnp.zeros_like(sh)
            dcw_ref[...] = jnp.zeros_like(dcw_ref)
            dcb_ref[...] = jnp.zeros_like(dcb_ref)
            dg_ref[...] = jnp.zeros_like(dg_ref)

        x = h_ref[...]
        gain = g_ref[...]
        xn, xhat, r = _rms_fwd(x, gain)
        xn_ref[...] = xn.T.astype(BF)
        dh = dh_ref[...]
        dhb = dh.astype(BF)
        dhb_ref[...] = dhb
        dxn = jnp.zeros((tm, D), F32)
        dact_of = lambda d: _mm_nt(dhb, wdown_ref[d * FF_BLK:(d + 1) * FF_BLK, :])
        ahead = dact_of(0)
        for d in range(N_FF_BLK):
            blks = (d, N_FF_BLK + d)
            dact = ahead
            if d + 1 < N_FF_BLK:
                ahead = dact_of(d + 1)
            cg, cu = cc_ref[blks[0]].astype(F32), cc_ref[blks[1]].astype(F32)
            sg = _sigmoid(cg)
            sil = cg * sg
            act_ref[d] = (sil * cu).astype(BF)
            dcs = (dact * cu * (sg + sil - sil * sg), dact * sil)
            for blk, dc in zip(blks, dcs):
                next8 = sh[blk]
                dp1, dp2 = _shift_up(dc, next8, 1), _shift_up(dc, next8, 2)
                hb = hh_ref[blk].astype(F32)
                dcb_ref[blk] += jnp.sum(dc, axis=0, keepdims=True)
                dcw_ref[blk, 0:1, :] += jnp.sum(dp2 * hb, axis=0, keepdims=True)
                dcw_ref[blk, 1:2, :] += jnp.sum(dp1 * hb, axis=0, keepdims=True)
                dcw_ref[blk, 2:3, :] += jnp.sum(dc * hb, axis=0, keepdims=True)
                w = cw_ref[blk]
                dhh = (w[2:3] * dc + w[1:2] * dp1 + w[0:1] * dp2).astype(BF)
                sh[blk] = dc[0:8, :]
                dhh_ref[blk] = dhh
                dxn = dxn + _mm_nt(dhh, wup_ref[blk])
        dxr, dg = _rms_bwd(dxn, xhat, r, gain)
        dg_ref[...] += dg
        dx_ref[...] = dh + dxr

    blk3 = lambda i: (0, nt - 1 - i, 0)
    return pl.pallas_call(
        body, name="ffn_bwd", grid=(nt,),
        in_specs=[pl.BlockSpec((tm, D), rev), pl.BlockSpec((tm, D), rev),
                  pl.BlockSpec((N_DEV, tm, FF_BLK), blk3), pl.BlockSpec((N_DEV, tm, FF_BLK), blk3),
                  _const_spec((1, D)), _const_spec((N_DEV, D, FF_BLK), True), _const_spec((N_DEV, 3, FF_BLK)),
                  _const_spec((D_FF, D), True)],
        out_specs=[pl.BlockSpec((tm, D), rev), pl.BlockSpec((D, tm), lambda i: (0, nt - 1 - i)),
                   pl.BlockSpec((N_DEV, tm, FF_BLK), blk3), pl.BlockSpec((N_FF_BLK, tm, FF_BLK), blk3),
                   pl.BlockSpec((tm, D), rev), _const_spec((N_DEV, 3, FF_BLK)), _const_spec((N_DEV, 1, FF_BLK)),
                   _const_spec((1, D))],
        out_shape=[jax.ShapeDtypeStruct((S, D), F32), jax.ShapeDtypeStruct((D, S), BF),
                   jax.ShapeDtypeStruct((N_DEV, S, FF_BLK), BF), jax.ShapeDtypeStruct((N_FF_BLK, S, FF_BLK), BF),
                   jax.ShapeDtypeStruct((S, D), BF), jax.ShapeDtypeStruct((N_DEV, 3, FF_BLK), F32),
                   jax.ShapeDtypeStruct((N_DEV, 1, FF_BLK), F32), jax.ShapeDtypeStruct((1, D), F32)],
        scratch_shapes=[pltpu.VMEM((N_DEV, 8, FF_BLK), F32)],
        compiler_params=_params(VMEM_BIG),
    )(dh, h, hh, cc, g, wup, cw, wdown)


def _ple_fwd(h, g, wgate, bgate, p, wple, tm):
    S = h.shape[0]

    def body(h_ref, g_ref, wg_ref, bg_ref, p_ref, wp_ref, o_ref):
        x = h_ref[...]
        xn, _, _ = _rms_fwd(x, g_ref[...])
        gate = _sigmoid(_mm(xn.astype(BF), wg_ref[...]) + bg_ref[...])
        o_ref[...] = x + _mm(p_ref[...], wp_ref[...]) * gate

    return pl.pallas_call(
        body, name="ple_fwd", grid=(S // tm,),
        in_specs=[_row_spec(tm, D), _const_spec((1, D)), _const_spec((D, D)), _const_spec((1, D)),
                  _row_spec(tm, PLE), _const_spec((PLE, D))],
        out_specs=_row_spec(tm, D), out_shape=jax.ShapeDtypeStruct((S, D), F32),
        compiler_params=_params(),
    )(h, g, wgate, bgate, p, wple)


def _ple_bwd(dh, h, g, wgate, bgate, p, wple, tm):
    S = h.shape[0]

    def body(dh_ref, h_ref, g_ref, wg_ref, bg_ref, p_ref, wp_ref, dx_ref, de_ref, xn_ref, dpre_ref, dbg_ref, dg_ref):
        i = pl.program_id(0)

        @pl.when(i == 0)
        def _():
            dbg_ref[...] = jnp.zeros_like(dbg_ref)
            dg_ref[...] = jnp.zeros_like(dg_ref)

        x = h_ref[...]
        gain = g_ref[...]
        xn, xhat, r = _rms_fwd(x, gain)
        xnb = xn.astype(BF)
        xn_ref[...] = xnb
        gate = _sigmoid(_mm(xnb, wg_ref[...]) + bg_ref[...])
        e = _mm(p_ref[...], wp_ref[...])
        dh = dh_ref[...]
        de_ref[...] = (dh * gate).astype(BF)
        dpre = dh * e * gate * (1.0 - gate)
        dpreb = dpre.astype(BF)
        dpre_ref[...] = dpreb
        dbg_ref[...] += jnp.sum(dpre, axis=0, keepdims=True)
        dxr, dg = _rms_bwd(_mm_nt(dpreb, wg_ref[...]), xhat, r, gain)
        dg_ref[...] += dg
        dx_ref[...] = dh + dxr

    return pl.pallas_call(
        body, name="ple_bwd", grid=(S // tm,),
        in_specs=[_row_spec(tm, D), _row_spec(tm, D), _const_spec((1, D)), _const_spec((D, D)), _const_spec((1, D)),
                  _row_spec(tm, PLE), _const_spec((PLE, D))],
        out_specs=[_row_spec(tm, D), _row_spec(tm, D), _row_spec(tm, D), _row_spec(tm, D), _const_spec((1, D)),
                   _const_spec((1, D))],
        out_shape=[jax.ShapeDtypeStruct((S, D), F32), jax.ShapeDtypeStruct((S, D), BF),
                   jax.ShapeDtypeStruct((S, D), BF), jax.ShapeDtypeStruct((S, D), BF),
                   jax.ShapeDtypeStruct((1, D), F32), jax.ShapeDtypeStruct((1, D), F32)],
        compiler_params=_params(),
    )(dh, h, g, wgate, bgate, p, wple)


def _qkv_fwd(h, gq, wq, gkv, wkv, tm):
    S = h.shape[0]
    nkv = wkv.shape[1]

    def body(h_ref, gq_ref, wq_ref, gkv_ref, wkv_ref, q_ref, kv_ref):
        x = h_ref[...]
        _, xhat, _ = _rms_fwd(x, gq_ref[...])
        q_ref[...] = _mm((xhat * gq_ref[...]).astype(BF), wq_ref[...]).astype(BF)
        kv_ref[...] = _mm((xhat * gkv_ref[...]).astype(BF), wkv_ref[...]).astype(BF)

    return pl.pallas_call(
        body, name="qkv_fwd", grid=(S // tm,),
        in_specs=[_row_spec(tm, D), _const_spec((1, D)), _const_spec((D, D)), _const_spec((1, D)),
                  _const_spec((D, nkv))],
        out_specs=[_row_spec(tm, D), _row_spec(tm, nkv)],
        out_shape=[jax.ShapeDtypeStruct((S, D), BF), jax.ShapeDtypeStruct((S, nkv), BF)],
        compiler_params=_params(),
    )(h, gq, wq, gkv, wkv)


def _qkv_bwd(dh, h, dq, dkv, gq, wq, gkv, wkv, tm):
    S = h.shape[0]
    nkv = wkv.shape[1]

    def body(dh_ref, h_ref, dq_ref, dkv_ref, gq_ref, wq_ref, gkv_ref, wkv_ref,
             dx_ref, xq_ref, xkv_ref, dgq_ref, dgkv_ref):
        i = pl.program_id(0)

        @pl.when(i == 0)
        def _():
            dgq_ref[...] = jnp.zeros_like(dgq_ref)
            dgkv_ref[...] = jnp.zeros_like(dgkv_ref)

        x = h_ref[...]
        gq_, gkv_ = gq_ref[...], gkv_ref[...]
        _, xhat, r = _rms_fwd(x, gq_)
        xq_ref[...] = (xhat * gq_).astype(BF)
        xkv_ref[...] = (xhat * gkv_).astype(BF)
        d1, dg1 = _rms_bwd(_mm_nt(dq_ref[...], wq_ref[...]), xhat, r, gq_)
        d2, dg2 = _rms_bwd(_mm_nt(dkv_ref[...], wkv_ref[...]), xhat, r, gkv_)
        dgq_ref[...] += dg1
        dgkv_ref[...] += dg2
        dx_ref[...] = dh_ref[...] + d1 + d2

    return pl.pallas_call(
        body, name="qkv_bwd", grid=(S // tm,),
        in_specs=[_row_spec(tm, D), _row_spec(tm, D), _row_spec(tm, D), _row_spec(tm, nkv), _const_spec((1, D)),
                  _const_spec((D, D)), _const_spec((1, D)), _const_spec((D, nkv))],
        out_specs=[_row_spec(tm, D), _row_spec(tm, D), _row_spec(tm, D), _const_spec((1, D)), _const_spec((1, D))],
        out_shape=[jax.ShapeDtypeStruct((S, D), F32), jax.ShapeDtypeStruct((S, D), BF),
                   jax.ShapeDtypeStruct((S, D), BF), jax.ShapeDtypeStruct((1, D), F32),
                   jax.ShapeDtypeStruct((1, D), F32)],
        compiler_params=_params(),
    )(dh, h, dq, dkv, gq, wq, gkv, wkv)


def _oproj_fwd(a, w, res, tm):
    S = a.shape[0]

    def body(a_ref, w_ref, r_ref, o_ref):
        o_ref[...] = r_ref[...] + _mm(a_ref[...], w_ref[...])

    return pl.pallas_call(
        body, name="oproj_fwd", grid=(S // tm,),
        in_specs=[_row_spec(tm, D), _const_spec((D, D)), _row_spec(tm, D)],
        out_specs=_row_spec(tm, D), out_shape=jax.ShapeDtypeStruct((S, D), F32),
        compiler_params=_params(),
    )(a, w, res)


def _oproj_bwd(dh, w, tm):
    S = dh.shape[0]

    def body(dh_ref, w_ref, da_ref, dhb_ref):
        dhb = dh_ref[...].astype(BF)
        dhb_ref[...] = dhb
        da_ref[...] = _mm_nt(dhb, w_ref[...]).astype(BF)

    return pl.pallas_call(
        body, name="oproj_bwd", grid=(S // tm,),
        in_specs=[_row_spec(tm, D), _const_spec((D, D))],
        out_specs=[_row_spec(tm, D), _row_spec(tm, D)],
        out_shape=[jax.ShapeDtypeStruct((S, D), BF), jax.ShapeDtypeStruct((S, D), BF)],
        compiler_params=_params(),
    )(dh, w)


def _alibi_slope(hq):
    return float(np.float32(2.0 ** (-8.0 * (hq + 1) / N_Q)))


def _attn_consts():
    W = N_KV * HD
    r = np.zeros((W, N_KV * W), np.float32)
    for kh in range(N_KV):
        for c in range(W):
            r[kh * HD + c % HD, kh * W + c] = 1.0
    qi = np.arange(BLOCK)[:, None]
    kj = np.arange(2 * BLOCK)[None, :]
    dist = qi + BLOCK - kj
    band = (dist >= 0) & (dist < BLOCK)
    bias = np.full((2, N_KV, GQA * BLOCK, 2 * BLOCK), NEG, np.float32)
    for later in (0, 1):
        valid = band & ((kj >= BLOCK) | bool(later))
        for kh in range(N_KV):
            for g in range(GQA):
                slope = np.float32(_alibi_slope(kh * GQA + g))
                bias[later, kh, g * BLOCK:(g + 1) * BLOCK] = np.where(valid, -slope * dist.astype(np.float32), NEG)
    lane_head = np.arange(W)[None, :] // HD
    maskq = (np.arange(GQA * BLOCK)[:, None] // BLOCK == lane_head).astype(np.float32)
    maskk = (np.arange(GQA * 2 * BLOCK)[:, None] // (2 * BLOCK) == lane_head).astype(np.float32)
    head_sum = (lane_head.T == np.arange(GQA * BLOCK)[None, :] // BLOCK).astype(np.float32)
    return dict(rep=jnp.asarray(r, BF), bias=jnp.asarray(bias), maskq=jnp.asarray(maskq, BF),
                maskqs=jnp.asarray(maskq * HD ** -0.5, BF),
                maskk=jnp.asarray(maskk, BF), head_sum=jnp.asarray(head_sum, BF))


def _sink_col(sink_ref, kh):
    rb = lax.broadcasted_iota(jnp.int32, (GQA * BLOCK, 1), 0) >> 7
    col = jnp.full((GQA * BLOCK, 1), sink_ref[kh * GQA + GQA - 1], F32)
    for g in range(GQA - 1):
        col = jnp.where(rb == g, sink_ref[kh * GQA + g], col)
    return col


def _stack4(a):
    return jnp.concatenate([a] * GQA, axis=0)


def _rows_to_lanes(a):
    return jnp.concatenate([a[g * BLOCK:(g + 1) * BLOCK] for g in range(GQA)], axis=1)


def _lanes_to_rows(a, n):
    return jnp.concatenate([a[:, g * n:(g + 1) * n] for g in range(GQA)], axis=0)


def _scores(qg, k4, bias, sink_col, maskqs):
    qs = _stack4(qg) * maskqs
    s = _mm_nt(qs, k4) + bias
    m = jnp.maximum(jnp.max(s, axis=-1, keepdims=True), sink_col)
    return qs, jnp.exp(s - m), jnp.exp(sink_col - m)


def _bands(kvc_ref, kvp_ref, rep_ref):
    W = N_KV * HD
    kvp, kvc = kvp_ref[...], kvc_ref[...]
    kband = jnp.concatenate([kvp[:, :W], kvc[:, :W]], axis=0)
    vband = jnp.concatenate([kvp[:, W:], kvc[:, W:]], axis=0)
    return _mm(kband, rep_ref[...]).astype(BF), _mm(vband, rep_ref[...]).astype(BF)


def _attn_fwd(q, kv, sinks, consts):
    S = q.shape[0]
    nb = S // BLOCK
    W = N_KV * HD

    def body(sink_ref, q_ref, kvc_ref, kvp_ref, rep_ref, bias_ref, maskqs_ref, maskk_ref, o_ref):
        maskqs, maskk = maskqs_ref[...], maskk_ref[...]
        lane_head = lax.broadcasted_iota(jnp.int32, (BLOCK, W), 1) >> 6
        k4all, v4all = _bands(kvc_ref, kvp_ref, rep_ref)
        for kh in range(N_KV):
            cols = slice(kh * W, (kh + 1) * W)
            _, e, es = _scores(q_ref[:, cols], k4all[:, cols], bias_ref[kh], _sink_col(sink_ref, kh), maskqs)
            vbd = jnp.concatenate([_stack4(v4all[:, cols]) * maskk, maskk], axis=1)
            nd = _mm(_rows_to_lanes(e.astype(BF)), vbd)
            es_l = jnp.broadcast_to(es[(GQA - 1) * BLOCK:], (BLOCK, W))
            for g in range(GQA - 1):
                es_l = jnp.where(lane_head == g, es[g * BLOCK:(g + 1) * BLOCK], es_l)
            o_ref[:, cols] = (nd[:, :W] / (nd[:, W:] + es_l)).astype(BF)

    return pl.pallas_call(
        body, name="attn_fwd", grid=(nb,),
        in_specs=[pl.BlockSpec(memory_space=pltpu.SMEM), _row_spec(BLOCK, D), _row_spec(BLOCK, 2 * W),
                  pl.BlockSpec((BLOCK, 2 * W), lambda i: (jnp.maximum(i - 1, 0), 0)), _const_spec((W, N_KV * W)),
                  pl.BlockSpec((None, N_KV, GQA * BLOCK, 2 * BLOCK), lambda i: (jnp.minimum(i, 1), 0, 0, 0)),
                  _const_spec((GQA * BLOCK, W)), _const_spec((GQA * 2 * BLOCK, W))],
        out_specs=_row_spec(BLOCK, D), out_shape=jax.ShapeDtypeStruct((S, D), BF),
        compiler_params=_params(),
    )(sinks, q, kv, kv, consts["rep"], consts["bias"], consts["maskqs"], consts["maskk"])


def _attn_bwd(q, kv, att, do, sinks, consts):
    S = q.shape[0]
    nb = S // BLOCK
    W = N_KV * HD

    def qmap(i):
        return (jnp.minimum(i, nb - 1), 0)

    def body(sink_ref, q_ref, kvc_ref, kvp_ref, att_ref, do_ref, rep_ref, bias_ref, maskq_ref, maskqs_ref, maskk_ref,
             hsum_ref, dq_ref, dkv_ref, dsink_ref, carry, sink_acc):
        i = pl.program_id(0)

        @pl.when(i == 0)
        def _():
            sink_acc[...] = jnp.zeros_like(sink_acc)
            carry[...] = jnp.zeros_like(carry)

        @pl.when(i < nb)
        def _():
            maskq, maskqs, maskk, head_sum = maskq_ref[...], maskqs_ref[...], maskk_ref[...], hsum_ref[...]
            ones_k = jnp.ones((2 * BLOCK, BLOCK), BF)
            k4all, v4all = _bands(kvc_ref, kvp_ref, rep_ref)
            dk4s, dv4s = [], []
            for kh in range(N_KV):
                cols = slice(kh * W, (kh + 1) * W)
                k4, v4 = k4all[:, cols], v4all[:, cols]
                qs, e, es = _scores(q_ref[:, cols], k4, bias_ref[kh], _sink_col(sink_ref, kh), maskqs)
                inv = 1.0 / (_mm(e.astype(BF), ones_k) + es)
                p = e * jnp.concatenate([inv, inv], axis=1)
                dog = do_ref[:, cols]
                dos = _stack4(dog) * maskq
                dp = _mm_nt(dos, v4)
                prod = (dog.astype(F32) * att_ref[:, cols].astype(F32)).astype(BF)
                delta = _lanes_to_rows(_mm(prod, head_sum), BLOCK)
                ds = (p * (dp - jnp.concatenate([delta, delta], axis=1))).astype(BF)
                sink_acc[kh] += es * inv * delta
                kbd = _stack4(k4) * maskk
                dq_ref[:, cols] = (_mm(_rows_to_lanes(ds), kbd) * (HD ** -0.5)).astype(BF)
                dk4s.append(_mm_tn(ds, qs).astype(BF))
                dv4s.append(_mm_tn(p.astype(BF), dos).astype(BF))
            dkband = _mm_nt(jnp.concatenate(dk4s, axis=1), rep_ref[...])
            dvband = _mm_nt(jnp.concatenate(dv4s, axis=1), rep_ref[...])
            dkv_ref[:, :W] = (carry[:, :W] + dkband[:BLOCK]).astype(BF)
            dkv_ref[:, W:] = (carry[:, W:] + dvband[:BLOCK]).astype(BF)
            carry[:, :W] = dkband[BLOCK:]
            carry[:, W:] = dvband[BLOCK:]

        @pl.when(i == nb)
        def _():
            dkv_ref[...] = carry[...].astype(BF)
            for hq in range(N_Q):
                kh, g = divmod(hq, GQA)
                dsink_ref[hq:hq + 1, :] = -jnp.sum(sink_acc[kh, g * BLOCK:(g + 1) * BLOCK, :], axis=0, keepdims=True)

    return pl.pallas_call(
        body, name="attn_bwd", grid=(nb + 1,),
        in_specs=[pl.BlockSpec(memory_space=pltpu.SMEM), pl.BlockSpec((BLOCK, D), qmap),
                  pl.BlockSpec((BLOCK, 2 * W), qmap),
                  pl.BlockSpec((BLOCK, 2 * W), lambda i: (jnp.maximum(jnp.minimum(i, nb - 1) - 1, 0), 0)),
                  pl.BlockSpec((BLOCK, D), qmap), pl.BlockSpec((BLOCK, D), qmap), _const_spec((W, N_KV * W)),
                  pl.BlockSpec((None, N_KV, GQA * BLOCK, 2 * BLOCK), lambda i: (jnp.minimum(i, 1), 0, 0, 0)),
                  _const_spec((GQA * BLOCK, W)), _const_spec((GQA * BLOCK, W)), _const_spec((GQA * 2 * BLOCK, W)),
                  _const_spec((W, GQA * BLOCK))],
        out_specs=[pl.BlockSpec((BLOCK, D), qmap),
                   pl.BlockSpec((BLOCK, 2 * W), lambda i: (jnp.maximum(i - 1, 0), 0)),
                   _const_spec((N_Q, BLOCK))],
        out_shape=[jax.ShapeDtypeStruct((S, D), BF), jax.ShapeDtypeStruct((S, 2 * W), BF),
                   jax.ShapeDtypeStruct((N_Q, BLOCK), F32)],
        scratch_shapes=[pltpu.VMEM((BLOCK, 2 * W), F32), pltpu.VMEM((N_KV, GQA * BLOCK, BLOCK), F32)],
        compiler_params=_params(),
    )(sinks, q, kv, kv, att, do, consts["rep"], consts["bias"], consts["maskq"], consts["maskqs"], consts["maskk"], consts["head_sum"])


def _head(h, g, target, tm):
    S = h.shape[0]

    def body(h_ref, g_ref, t_ref, dh_ref, loss_ref, dg_ref):
        i = pl.program_id(0)

        @pl.when(i == 0)
        def _():
            loss_ref[...] = jnp.zeros_like(loss_ref)
            dg_ref[...] = jnp.zeros_like(dg_ref)

        gain = g_ref[...]
        y, xhat, r = _rms_fwd(h_ref[...], gain)
        err = y - t_ref[...]
        tile = 0.5 * jnp.sum(jnp.sum(err * err, axis=-1, keepdims=True) * (1.0 / D), axis=0, keepdims=True)
        loss_ref[...] += jnp.broadcast_to(tile, loss_ref.shape)
        dx, dg = _rms_bwd(err * (1.0 / D), xhat, r, gain)
        dg_ref[...] += dg
        dh_ref[...] = dx

    return pl.pallas_call(
        body, name="loss_head", grid=(S // tm,),
        in_specs=[_row_spec(tm, D), _const_spec((1, D)), _row_spec(tm, D)],
        out_specs=[_row_spec(tm, D), _const_spec((8, 128)), _const_spec((1, D))],
        out_shape=[jax.ShapeDtypeStruct((S, D), F32), jax.ShapeDtypeStruct((8, 128), F32),
                   jax.ShapeDtypeStruct((1, D), F32)],
        compiler_params=_params(),
    )(h, g, target)


def _wgrad(a, b, name, *, tn, out_blocked=False, a_transposed=False, ts=4096, per_step=1):
    a_blocked, b_blocked = a.ndim == 3, b.ndim == 3
    na = a.shape[0] if a_blocked else 1
    S = b.shape[-2]
    K = a.shape[0] if a_transposed else a.shape[-1]
    wide = tn * per_step
    nbk = b.shape[0] if b_blocked else b.shape[1] // wide
    ts = min(ts, S)
    ns = S // ts
    assert not (a_blocked and (out_blocked or a_transposed)) and (per_step == 1 or (out_blocked and not b_blocked))

    def body(a_ref, b_ref, o_ref, acc):
        s = pl.program_id(2)

        @pl.when(s == 0)
        def _():
            acc[...] = jnp.zeros_like(acc)

        if a_transposed:
            acc[...] += _mm(a_ref[...], b_ref[...])
        else:
            acc[...] += _mm_tn(a_ref[...], b_ref[...])

        @pl.when(s == ns - 1)
        def _():
            if per_step == 1:
                o_ref[...] = acc[...].astype(o_ref.dtype)
            else:
                for k in range(per_step):
                    o_ref[k] = acc[:, k * tn:(k + 1) * tn].astype(o_ref.dtype)

    if a_transposed:
        a_spec = pl.BlockSpec((K, ts), lambda i, j, s: (0, s))
    elif a_blocked:
        a_spec = pl.BlockSpec((None, ts, K), lambda i, j, s: (i, s, 0))
    else:
        a_spec = pl.BlockSpec((ts, K), lambda i, j, s: (s, 0))
    b_spec = (pl.BlockSpec((None, ts, tn), lambda i, j, s: (j, s, 0)) if b_blocked
              else pl.BlockSpec((ts, wide), lambda i, j, s: (s, j)))
    if out_blocked:
        o_spec = pl.BlockSpec((None, K, tn) if per_step == 1 else (per_step, K, tn), lambda i, j, s: (j, 0, 0))
        o_shape = jax.ShapeDtypeStruct((nbk * per_step, K, tn), BF)
    else:
        o_spec = pl.BlockSpec((K, wide), lambda i, j, s: (i, j))
        o_shape = jax.ShapeDtypeStruct((na * K, nbk * wide), BF)
    return pl.pallas_call(
        body, name=name, grid=(na, nbk, ns), in_specs=[a_spec, b_spec], out_specs=o_spec, out_shape=o_shape,
        scratch_shapes=[pltpu.VMEM((K, wide), F32)],
        compiler_params=pltpu.CompilerParams(dimension_semantics=("arbitrary", "arbitrary", "arbitrary"),
                                             vmem_limit_bytes=VMEM_BIG),
    )(a, b)


def _adamw_math(w, g, m, v):
    m = B1 * m + (1.0 - B1) * g
    v = B2 * v + (1.0 - B2) * (g * g)
    m_hat = m / (1.0 - B1 ** STEP)
    v_hat = v / (1.0 - B2 ** STEP)
    return -LR * (m_hat / (jnp.sqrt(v_hat) + ADAM_EPS) + WD * w), m, v


def _sum8(parts, name):
    R = parts.shape[1]

    def body(p_ref, o_ref):
        g = p_ref[0]
        for s in range(1, N_DEV):
            g = g + p_ref[s]
        o_ref[...] = g

    return pl.pallas_call(
        body, name=name, grid=(1,),
        in_specs=[pl.BlockSpec((N_DEV, R, ROW), lambda i: (0, 0, 0))],
        out_specs=pl.BlockSpec((R, ROW), lambda i: (0, 0)), out_shape=jax.ShapeDtypeStruct((R, ROW), F32),
        compiler_params=_params(),
    )(parts)


def _adamw_small(g, w, m, v, name):
    R = w.shape[0]

    def body(g_ref, w_ref, m_ref, v_ref, d_ref, nm_ref, nv_ref):
        d_ref[...], nm_ref[...], nv_ref[...] = _adamw_math(w_ref[...], g_ref[...], m_ref[...], v_ref[...])

    spec = pl.BlockSpec((R, ROW), lambda i: (0, 0))
    return pl.pallas_call(
        body, name=name, grid=(1,), in_specs=[spec] * 4, out_specs=[spec] * 3,
        out_shape=[jax.ShapeDtypeStruct((R, ROW), F32)] * 3, compiler_params=_params(),
    )(g, w, m, v)


def _mesh_pos():
    return lax.axis_index("x"), lax.axis_index("y"), lax.axis_index("c")


def _flip(v, bit):
    return 1 - v if bit else v


def _peers(x, y, c):
    out = []
    for k in range(1, N_DEV):
        px, py, pc = _flip(x, k & 4), _flip(y, k & 2), _flip(c, k & 1)
        out.append((k - 1, (px, py, pc), 4 * px + 2 * py + pc))
    return out


def _all_gather_now(xs, name):
    n = len(xs)
    MESH = pl.DeviceIdType.MESH

    def body(*refs):
        x_refs, out_refs = refs[:n], refs[n:2 * n]
        send_sems, recv_sems, local_sems = refs[2 * n:]
        x, y, c = _mesh_pos()
        me, sibling = (x, y, c), (x, y, 1 - c)
        chips = [(1 - x, y), (x, 1 - y), (1 - x, 1 - y)]
        blk = lambda p: 4 * p[0] + 2 * p[1] + p[2]

        def copy(a, k, block, to, src=None):
            return pltpu.make_async_remote_copy(
                src_ref=out_refs[a].at[blk(block)] if src is None else src, dst_ref=out_refs[a].at[blk(block)],
                send_sem=send_sems.at[a, k], recv_sem=recv_sems.at[a, k], device_id=to, device_id_type=MESH)

        sent, own = [], []
        for a in range(n):
            own.append(pltpu.make_async_copy(x_refs[a], out_refs[a].at[blk(me)], local_sems.at[a]))
            own[-1].start()
            first = [copy(a, 1 + j, me, (*chip, c), src=x_refs[a]) for j, chip in enumerate(chips)]
            first.append(copy(a, 0, me, sibling, src=x_refs[a]))
            for cp in first:
                cp.start()
            sent += first
        for a in range(n):
            for j, chip in enumerate(chips):
                copy(a, 1 + j, (*chip, c), me).wait_recv()
                sent.append(copy(a, 4 + j, (*chip, c), sibling))
                sent[-1].start()
        for a in range(n):
            copy(a, 0, sibling, me).wait_recv()
            for j, chip in enumerate(chips):
                copy(a, 4 + j, (*chip, 1 - c), me).wait_recv()
        for cp in sent:
            cp.wait_send()
        for cp in own:
            cp.wait()

    hbm = pl.BlockSpec(memory_space=pl.ANY)
    return pl.pallas_call(
        body, name=name, out_shape=[jax.ShapeDtypeStruct((N_DEV,) + x.shape, x.dtype) for x in xs],
        in_specs=[hbm] * n, out_specs=[hbm] * n,
        scratch_shapes=[pltpu.SemaphoreType.DMA((n, 7)), pltpu.SemaphoreType.DMA((n, 7)), pltpu.SemaphoreType.DMA((n,))],
    )(*xs)


def _chip_peers(x, y, c):
    return (x, y, c), (x, y, 1 - c), [(1 - x, y, c), (x, 1 - y, c), (1 - x, 1 - y, c)]


def _blk(p):
    return 4 * p[0] + 2 * p[1] + p[2]


def _gather2_start(srcs, name):
    n = len(srcs)
    MESH = pl.DeviceIdType.MESH

    def body(*refs):
        src_refs, land_refs = refs[:n], refs[n:2 * n]
        send, recv_ici, recv_sib, local = (refs[(2 + k) * n:(3 + k) * n] for k in range(4))
        token = refs[8 * n]
        me, sibling, chips = _chip_peers(*_mesh_pos())
        for a in range(n):
            dst = land_refs[a].at[_blk(me)]
            pltpu.make_async_remote_copy(src_ref=src_refs[a], dst_ref=dst, send_sem=send[a].at[0], recv_sem=recv_sib[a],
                                         device_id=sibling, device_id_type=MESH).start()
            for j, chip in enumerate(chips):
                pltpu.make_async_remote_copy(src_ref=src_refs[a], dst_ref=dst, send_sem=send[a].at[1 + j],
                                             recv_sem=recv_ici[a].at[j], device_id=chip, device_id_type=MESH).start()
            pltpu.make_async_copy(src_refs[a], dst, local[a]).start()
        token[...] = jnp.zeros_like(token)

    hbm = pl.BlockSpec(memory_space=pltpu.HBM)
    sem = pl.BlockSpec(memory_space=pltpu.SEMAPHORE)
    dma = pltpu.SemaphoreType.DMA
    outs = pl.pallas_call(
        body, name=name,
        out_shape=([dma((4,))] * n + [dma((3,))] * n + [dma(())] * n + [dma(())] * n
                   + [pltpu.HBM(s.shape, s.dtype) for s in srcs]
                   + [pltpu.HBM((N_DEV,) + s.shape, s.dtype) for s in srcs] + [jax.ShapeDtypeStruct((8, 128), F32)]),
        in_specs=[hbm] * (2 * n), out_specs=[sem] * (4 * n) + [hbm] * (2 * n) + [pl.BlockSpec(memory_space=pltpu.VMEM)],
        input_output_aliases={**{a: 4 * n + a for a in range(n)}, **{n + a: 5 * n + a for a in range(n)}},
        compiler_params=pltpu.CompilerParams(has_side_effects=pltpu.SideEffectType.DATAFLOW_SIDE_EFFECTING),
    )(*[pltpu.with_memory_space_constraint(s, pltpu.HBM) for s in srcs],
      *[pltpu.with_memory_space_constraint(lax.empty((N_DEV,) + s.shape, s.dtype), pltpu.HBM) for s in srcs])
    keys = ["send", "recv_ici", "recv_sib", "local", "srcs", "lands"]
    return {**{k: outs[i * n:(i + 1) * n] for i, k in enumerate(keys)}, "token": outs[6 * n]}


def _gather2_forward(started, after, name):
    lands, recv_ici = started["lands"], started["recv_ici"]
    n = len(lands)
    MESH = pl.DeviceIdType.MESH

    def body(*refs):
        land_refs, recv_refs = refs[:n], refs[n:2 * n]
        fsend, frecv = refs[2 * n + 1 + n:2 * n + 1 + 2 * n], refs[2 * n + 1 + 2 * n:2 * n + 1 + 3 * n]
        me, sibling, chips = _chip_peers(*_mesh_pos())
        for a in range(n):
            for j, chip in enumerate(chips):
                rows = land_refs[a].at[_blk(chip)]
                pltpu.make_async_remote_copy(src_ref=rows, dst_ref=rows, send_sem=fsend[a].at[j], recv_sem=recv_refs[a].at[j],
                                             device_id=chip, device_id_type=MESH).wait_recv()
                pltpu.make_async_remote_copy(src_ref=rows, dst_ref=rows, send_sem=fsend[a].at[j], recv_sem=frecv[a].at[j],
                                             device_id=sibling, device_id_type=MESH).start()

    hbm = pl.BlockSpec(memory_space=pltpu.HBM)
    sem = pl.BlockSpec(memory_space=pltpu.SEMAPHORE)
    dma = pltpu.SemaphoreType.DMA
    outs = pl.pallas_call(
        body, name=name,
        out_shape=[pltpu.HBM(l.shape, l.dtype) for l in lands] + [dma((3,))] * (2 * n),
        in_specs=[hbm] * n + [sem] * n + [pl.BlockSpec(memory_space=pl.ANY)], out_specs=[hbm] * n + [sem] * (2 * n),
        input_output_aliases={a: a for a in range(n)},
        compiler_params=pltpu.CompilerParams(has_side_effects=pltpu.SideEffectType.DATAFLOW_SIDE_EFFECTING),
    )(*lands, *recv_ici, after)
    return {**started, "lands": outs[:n], "fsend": outs[n:2 * n], "frecv": outs[2 * n:3 * n]}


def _gather2_wait(st, after, name):
    n = len(st["lands"])
    MESH = pl.DeviceIdType.MESH

    def body(*refs):
        src_refs, land_refs = refs[:n], refs[n:2 * n]
        send, recv_sib, local, fsend, frecv = (refs[(2 + k) * n:(3 + k) * n] for k in range(5))
        me, sibling, chips = _chip_peers(*_mesh_pos())
        for a in range(n):
            def desc(src, dst, s, r):
                return pltpu.make_async_remote_copy(src_ref=src, dst_ref=dst, send_sem=s, recv_sem=r, device_id=sibling,
                                                    device_id_type=MESH)
            own = land_refs[a].at[_blk(me)]
            for k in range(4):
                desc(src_refs[a], own, send[a].at[k], recv_sib[a]).wait_send()
            desc(src_refs[a], land_refs[a].at[_blk(sibling)], send[a].at[0], recv_sib[a]).wait_recv()
            pltpu.make_async_copy(src_refs[a], own, local[a]).wait()
            for j, chip in enumerate(chips):
                mine, theirs = land_refs[a].at[_blk(chip)], land_refs[a].at[_blk((chip[0], chip[1], 1 - chip[2]))]
                desc(mine, mine, fsend[a].at[j], frecv[a].at[j]).wait_send()
                desc(theirs, theirs, fsend[a].at[j], frecv[a].at[j]).wait_recv()

    hbm = pl.BlockSpec(memory_space=pltpu.HBM)
    sem = pl.BlockSpec(memory_space=pltpu.SEMAPHORE)
    srcs, lands = st["srcs"], st["lands"]
    outs = pl.pallas_call(
        body, name=name, out_shape=[pltpu.HBM(s.shape, s.dtype) for s in list(srcs) + list(lands)],
        in_specs=[hbm] * (2 * n) + [sem] * (5 * n) + [pl.BlockSpec(memory_space=pl.ANY)], out_specs=[hbm] * (2 * n),
        input_output_aliases={a: a for a in range(2 * n)},
        compiler_params=pltpu.CompilerParams(has_side_effects=pltpu.SideEffectType.DATAFLOW_SIDE_EFFECTING),
    )(*srcs, *lands, *st["send"], *st["recv_sib"], *st["local"], *st["fsend"], *st["frecv"], after)
    return outs[n:]


def _send_start(srcs, blocked, name):
    n = len(srcs)
    MESH = pl.DeviceIdType.MESH

    def body(*refs):
        src_refs, land_refs = refs[:n], refs[n:2 * n]
        send_sems, recv_sems, local_sems = refs[2 * n:3 * n], refs[3 * n:4 * n], refs[4 * n:5 * n]
        token = refs[7 * n]
        x, y, c = _mesh_pos()
        me = 4 * x + 2 * y + c
        for a in range(n):
            for k, peer, blk in _peers(x, y, c):
                pltpu.make_async_remote_copy(
                    src_ref=src_refs[a].at[blk] if blocked else src_refs[a], dst_ref=land_refs[a].at[me],
                    send_sem=send_sems[a].at[k], recv_sem=recv_sems[a].at[k], device_id=peer,
                    device_id_type=MESH).start()
            pltpu.make_async_copy(src_refs[a].at[me] if blocked else src_refs[a], land_refs[a].at[me],
                                  local_sems[a]).start()
        token[...] = jnp.zeros_like(token)

    hbm = pl.BlockSpec(memory_space=pltpu.HBM)
    sem = pl.BlockSpec(memory_space=pltpu.SEMAPHORE)
    lshape = [s.shape if blocked else (N_DEV,) + s.shape for s in srcs]
    outs = pl.pallas_call(
        body, name=name,
        out_shape=([pltpu.SemaphoreType.DMA((N_DEV - 1,))] * (2 * n) + [pltpu.SemaphoreType.DMA(())] * n
                   + [pltpu.HBM(s.shape, s.dtype) for s in srcs]
                   + [pltpu.HBM(ls, s.dtype) for ls, s in zip(lshape, srcs)] + [jax.ShapeDtypeStruct((8, 128), F32)]),
        in_specs=[hbm] * (2 * n), out_specs=[sem] * (3 * n) + [hbm] * (2 * n) + [pl.BlockSpec(memory_space=pltpu.VMEM)],
        input_output_aliases={**{a: 3 * n + a for a in range(n)}, **{n + a: 4 * n + a for a in range(n)}},
        compiler_params=pltpu.CompilerParams(has_side_effects=pltpu.SideEffectType.DATAFLOW_SIDE_EFFECTING),
    )(*[pltpu.with_memory_space_constraint(s, pltpu.HBM) for s in srcs],
      *[pltpu.with_memory_space_constraint(lax.empty(ls, s.dtype), pltpu.HBM) for ls, s in zip(lshape, srcs)])
    return dict(send=outs[:n], recv=outs[n:2 * n], local=outs[2 * n:3 * n], srcs=outs[3 * n:4 * n],
                lands=outs[4 * n:5 * n], token=outs[5 * n], blocked=blocked)


def _send_wait(started, which, after, name):
    n = len(which)
    blocked = started["blocked"]
    pick = lambda key: [started[key][a] for a in which]
    MESH = pl.DeviceIdType.MESH

    def body(*refs):
        src_refs, land_refs = refs[:n], refs[n:2 * n]
        send, recv, local = refs[2 * n:3 * n], refs[3 * n:4 * n], refs[4 * n:5 * n]
        x, y, c = _mesh_pos()
        me = 4 * x + 2 * y + c
        for a in range(n):
            for k, peer, blk in _peers(x, y, c):
                cp = pltpu.make_async_remote_copy(
                    src_ref=src_refs[a].at[blk] if blocked else src_refs[a], dst_ref=land_refs[a].at[blk],
                    send_sem=send[a].at[k], recv_sem=recv[a].at[k], device_id=peer, device_id_type=MESH)
                cp.wait_send()
                cp.wait_recv()
            pltpu.make_async_copy(src_refs[a].at[me] if blocked else src_refs[a], land_refs[a].at[me],
                                  local[a]).wait()

    hbm = pl.BlockSpec(memory_space=pltpu.HBM)
    sem = pl.BlockSpec(memory_space=pltpu.SEMAPHORE)
    srcs, lands = pick("srcs"), pick("lands")
    outs = pl.pallas_call(
        body, name=name, out_shape=[pltpu.HBM(s.shape, s.dtype) for s in srcs + lands],
        in_specs=[hbm] * (2 * n) + [sem] * (3 * n) + [pl.BlockSpec(memory_space=pl.ANY)], out_specs=[hbm] * (2 * n),
        input_output_aliases={a: a for a in range(2 * n)},
        compiler_params=pltpu.CompilerParams(has_side_effects=pltpu.SideEffectType.DATAFLOW_SIDE_EFFECTING),
    )(*srcs, *lands, *pick("send"), *pick("recv"), *pick("local"), after)
    return outs[n:]


def _adamw_weight(parts, w, m, v, name):
    L, R, C = w.shape
    tr = max(t for t in range(16, min(R, 256) + 1, 16) if R % t == 0)
    nr = R // tr

    def body(*refs):
        p_refs = refs[:L]
        w_ref, m_ref, v_ref, g_ref, d_ref, nm_ref, nv_ref, gsum = refs[L:]
        layer = pl.program_id(0)
        for l in range(L):
            @pl.when(layer == l)
            def _(l=l):
                g = p_refs[l][0].astype(F32)
                for s in range(1, N_DEV):
                    g = g + p_refs[l][s].astype(F32)
                gsum[...] = g
        g = gsum[...]
        g_ref[...] = g
        d_ref[...], nm_ref[...], nv_ref[...] = _adamw_math(w_ref[...], g, m_ref[...], v_ref[...])

    def part_spec(l):
        return pl.BlockSpec((N_DEV, tr, C), lambda layer, i: (0, jnp.where(layer == l, i, jnp.where(layer < l, 0, nr - 1)), 0))

    ws = pl.BlockSpec((None, tr, C), lambda layer, i: (layer, i, 0))
    return pl.pallas_call(
        body, name=name, grid=(L, nr), in_specs=[part_spec(l) for l in range(L)] + [ws] * 3, out_specs=[ws] * 4,
        out_shape=[jax.ShapeDtypeStruct((L, R, C), F32)] * 4, scratch_shapes=[pltpu.VMEM((tr, C), F32)],
        compiler_params=pltpu.CompilerParams(dimension_semantics=("arbitrary", "arbitrary")),
    )(*parts, w, m, v)


def _seg_rows(n, align):
    rows = -(-n // ROW)
    return -(-rows // align) * align


def _pack(arrs, align, lead=()):
    parts = []
    for a in arrs:
        n = int(np.prod(a.shape[len(lead):]))
        rows = _seg_rows(n, align)
        flat = a.reshape(lead + (n,))
        flat = jnp.pad(flat, [(0, 0)] * len(lead) + [(0, rows * ROW - n)])
        parts.append(flat.reshape(lead + (rows, ROW)))
    return jnp.concatenate(parts, axis=len(lead))


def _unpack(buf, shapes, align, lead=()):
    out, r0 = [], 0
    for shp in shapes:
        n = int(np.prod(shp))
        rows = _seg_rows(n, align)
        seg = lax.slice_in_dim(buf, r0, r0 + rows, axis=len(lead))
        out.append(seg.reshape(lead + (rows * ROW,))[..., :n].reshape(lead + tuple(shp)))
        r0 += rows
    return out


def kernel(x, p, norm_mix, norm_ffn, norm_ple, norm_kv, norm_final, a_w_in, a_norm_v, a_w_s, a_b_s, a_w_out, w_kv, b_w_q, b_sinks, b_w_o, f_w_up, f_conv_w, f_conv_b, f_w_down, ple_w_in, ple_w_gate, ple_b_gate, loss_target, m_norm_mix, m_norm_ffn, m_norm_ple, m_norm_kv, m_norm_final, m_a_w_in, m_a_norm_v, m_a_w_s, m_a_b_s, m_a_w_out, m_w_kv, m_b_w_q, m_b_sinks, m_b_w_o, m_f_w_up, m_f_conv_w, m_f_conv_b, m_f_w_down, m_ple_w_in, m_ple_w_gate, m_ple_b_gate, v_norm_mix, v_norm_ffn, v_norm_ple, v_norm_kv, v_norm_final, v_a_w_in, v_a_norm_v, v_a_w_s, v_a_b_s, v_a_w_out, v_w_kv, v_b_w_q, v_b_sinks, v_b_w_o, v_f_w_up, v_f_conv_w, v_f_conv_b, v_f_w_down, v_ple_w_in, v_ple_w_gate, v_ple_b_gate):
    w = dict(norm_mix=norm_mix, norm_ffn=norm_ffn, norm_ple=norm_ple, norm_kv=norm_kv, norm_final=norm_final,
             a_w_in=a_w_in, a_norm_v=a_norm_v, a_w_s=a_w_s, a_b_s=a_b_s, a_w_out=a_w_out, w_kv=w_kv, b_w_q=b_w_q,
             b_sinks=b_sinks, b_w_o=b_w_o, f_w_up=f_w_up, f_conv_w=f_conv_w, f_conv_b=f_conv_b, f_w_down=f_w_down,
             ple_w_in=ple_w_in, ple_w_gate=ple_w_gate, ple_b_gate=ple_b_gate)
    mom_m = dict(norm_mix=m_norm_mix, norm_ffn=m_norm_ffn, norm_ple=m_norm_ple, norm_kv=m_norm_kv,
                 norm_final=m_norm_final, a_w_in=m_a_w_in, a_norm_v=m_a_norm_v, a_w_s=m_a_w_s, a_b_s=m_a_b_s,
                 a_w_out=m_a_w_out, w_kv=m_w_kv, b_w_q=m_b_w_q, b_sinks=m_b_sinks, b_w_o=m_b_w_o, f_w_up=m_f_w_up,
                 f_conv_w=m_f_conv_w, f_conv_b=m_f_conv_b, f_w_down=m_f_w_down, ple_w_in=m_ple_w_in,
                 ple_w_gate=m_ple_w_gate, ple_b_gate=m_ple_b_gate)
    mom_v = dict(norm_mix=v_norm_mix, norm_ffn=v_norm_ffn, norm_ple=v_norm_ple, norm_kv=v_norm_kv,
                 norm_final=v_norm_final, a_w_in=v_a_w_in, a_norm_v=v_a_norm_v, a_w_s=v_a_w_s, a_b_s=v_a_b_s,
                 a_w_out=v_a_w_out, w_kv=v_w_kv, b_w_q=v_b_w_q, b_sinks=v_b_sinks, b_w_o=v_b_w_o, f_w_up=v_f_w_up,
                 f_conv_w=v_f_conv_w, f_conv_b=v_f_conv_b, f_w_down=v_f_w_down, ple_w_in=v_ple_w_in,
                 ple_w_gate=v_ple_w_gate, ple_b_gate=v_ple_b_gate)
    S = x.shape[1]
    tm = min(256, S)
    tl = min(512, S)
    me = 4 * lax.axis_index("x") + 2 * lax.axis_index("y") + lax.axis_index("c")

    def after_token(tok, a):
        return a + tok[0:1, 0:1].reshape((1,) * a.ndim)

    wb = {n: w[n].astype(BF) for n in _BIG}
    order = [('a_w_in', 0), ('a_w_out', 0), ('a_norm_v', None), ('f_conv_w', 0), ('f_conv_w', 1),
             ('f_w_up', 0), ('f_w_down', 0),
             ('ple_w_in', 0), ('ple_w_gate', 0), ('w_kv', None), ('b_w_q', 0), ('b_w_o', 0),
             ('f_w_up', 1), ('f_w_down', 1), ('ple_w_in', 1), ('ple_w_gate', 1)]
    src = lambda n, l: (wb[n] if n in wb else w[n]) if l is None else (wb[n] if n in wb else w[n])[l]
    groups = [order[:5], order[5:7], order[7:12], order[12:]]
    g = dict(zip(groups[0], _all_gather_now([src(n, l) for n, l in groups[0]], "gather_mix")))

    def start_group(keys, tie_to, name):
        srcs = [src(n, l) for n, l in keys]
        k = min(range(len(srcs)), key=lambda i: srcs[i].size)
        zero = (tie_to[(0,) * (tie_to.ndim - 1)][0:1] * 0).astype(srcs[k].dtype)
        srcs[k] = srcs[k] + zero.reshape((1,) * srcs[k].ndim)
        return _send_start(srcs, False, name)

    def finish_group(started, keys, after, name):
        return dict(zip(keys, _send_wait(started, list(range(len(keys))), after, name)))

    x0 = x[0]
    win = g[('a_w_in', 0)].transpose(1, 0, 2).reshape(D, 2 * AW)
    wout = g[('a_w_out', 0)].reshape(AW, D)
    gv = g[('a_norm_v', None)].reshape(1, AW)
    cw = [g[('f_conv_w', l)] for l in range(2)]
    cb = [w['f_conv_b'][l].reshape(N_DEV, 1, FF_BLK) for l in range(2)]
    ws = a_w_s[0].astype(BF)
    wst = jnp.swapaxes(ws, 1, 2)
    bsb = jnp.broadcast_to(a_b_s[0][:, :, None], (GROUPS, CHUNK, CHUNK))
    row = lambda a: a.reshape(1, -1)
    pb = [p[l, 0].astype(BF) for l in range(2)]
    sinks = b_sinks.reshape(N_Q)
    consts = _attn_consts()

    srcs1 = [src(n, l) for n, l in groups[1]]
    srcs1[1] = srcs1[1] + (g[('a_w_out', 0)][0, 0:1, 0:1] * 0).astype(srcs1[1].dtype)
    s1 = _gather2_start(srcs1, "gather_start_ffn0")
    h1, zg0, gp0 = _mixa_fwd(x0, after_token(s1["token"], row(norm_mix[0])), win, gv, ws, bsb, wout, tl)
    s1 = _gather2_forward(s1, h1, "gather_forward_ffn0")
    g = dict(zip(groups[1], _gather2_wait(s1, h1, "gather_wait_ffn0")))
    wup, wdown = [g[('f_w_up', 0)], None], [g[('f_w_down', 0)].reshape(D_FF, D), None]
    s2 = start_group(groups[2], g[('f_w_down', 0)], "gather_start_mid")
    h2, hh0, cc0 = _ffn_fwd(h1, after_token(s2["token"], row(norm_ffn[0])), wup[0], cw[0], cb[0], wdown[0], tm)
    g = finish_group(s2, groups[2], h2, "gather_wait_mid")
    wple = [g[('ple_w_in', 0)].transpose(1, 0, 2).reshape(PLE, D), None]
    wgate = [g[('ple_w_gate', 0)].reshape(D, D), None]
    wkv, wq, wo = g[('w_kv', None)].reshape(D, 2 * N_KV * HD), g[('b_w_q', 0)].reshape(D, D), g[('b_w_o', 0)].reshape(D, D)
    s3 = start_group(groups[3], g[('ple_w_in', 0)], "gather_start_l1")
    h3 = _ple_fwd(h2, after_token(s3["token"], row(norm_ple[0])), wgate[0], row(ple_b_gate[0]), pb[0], wple[0], tl)
    q, kv = _qkv_fwd(h3, row(norm_mix[1]), wq, row(norm_kv), wkv, tl)
    att = _attn_fwd(q, kv, sinks, consts)
    h4 = _oproj_fwd(att, wo, h3, tl)
    g = finish_group(s3, groups[3], h4, "gather_wait_l1")
    wup[1], wdown[1] = g[('f_w_up', 1)], g[('f_w_down', 1)].reshape(D_FF, D)
    wple[1], wgate[1] = g[('ple_w_in', 1)].transpose(1, 0, 2).reshape(PLE, D), g[('ple_w_gate', 1)].reshape(D, D)
    h5, hh1, cc1 = _ffn_fwd(h4, row(norm_ffn[1]), wup[1], cw[1], cb[1], wdown[1], tm)
    h6 = _ple_fwd(h5, row(norm_ple[1]), wgate[1], row(ple_b_gate[1]), pb[1], wple[1], tl)

    def col_blocks(g2d, n):
        return g2d.reshape(g2d.shape[0], N_DEV, n).transpose(1, 0, 2)

    def row_blocks(g2d):
        return g2d.reshape(N_DEV, g2d.shape[0] // N_DEV, g2d.shape[1])

    def ple_grads(l, de, xp, dpre):
        return [col_blocks(_wgrad(pb[l], de, f"wgrad_ple_in{l}", tn=1024), D // N_DEV),
                row_blocks(_wgrad(xp, dpre, f"wgrad_gate{l}", tn=1024))]

    def ffn_grads(l, xf, dhh, act, dhb):
        return [_wgrad(xf, dhh, f"wgrad_up{l}", tn=FF_BLK, out_blocked=True, a_transposed=True),
                row_blocks(_wgrad(act, dhb, f"wgrad_down{l}", tn=1024))]

    dh6, loss_blk, d_norm_final = _head(h6, row(norm_final), loss_target[0], tl)
    dh5, de1, xp1, dpre1, dbg1, dgp1 = _ple_bwd(dh6, h5, row(norm_ple[1]), wgate[1], row(ple_b_gate[1]), pb[1],
                                                 wple[1], tl)
    ex_ple1 = _send_start(ple_grads(1, de1, xp1, dpre1), True, "send_ple1")
    dh4, xf1, dhh1, act1, dhb5, dcw1, dcb1, dgf1 = _ffn_bwd(dh5, h4, hh1, cc1, after_token(ex_ple1["token"], row(norm_ffn[1])),
                                                            wup[1], cw[1], wdown[1], tm)
    ex_ffn1 = _send_start(ffn_grads(1, xf1, dhh1, act1, dhb5), True, "send_ffn1")
    datt, dhb4 = _oproj_bwd(dh4, wo, tl)
    dq, dkv, dsink = _attn_bwd(q, kv, att, datt, after_token(ex_ffn1["token"], sinks.reshape(1, N_Q)).reshape(N_Q), consts)
    dh3, xq, xkv, dgq, dgkv = _qkv_bwd(dh4, h3, dq, dkv, row(norm_mix[1]), wq, row(norm_kv), wkv, tl)
    ex_att = _send_start([row_blocks(_wgrad(att, dhb4, "wgrad_o", tn=1024)),
                              row_blocks(_wgrad(xq, dq, "wgrad_q", tn=1024)),
                              row_blocks(_wgrad(xkv, dkv, "wgrad_kv", tn=512))], True, "send_att")
    dh2, de0, xp0, dpre0, dbg0, dgp0 = _ple_bwd(dh3, h2, after_token(ex_att["token"], row(norm_ple[0])), wgate[0],
                                                 row(ple_b_gate[0]), pb[0], wple[0], tl)
    ex_ple0 = _send_start(ple_grads(0, de0, xp0, dpre0), True, "send_ple0")
    dh1, xf0, dhh0, act0, dhb2, dcw0, dcb0, dgf0 = _ffn_bwd(dh2, h1, hh0, cc0, after_token(ex_ple0["token"], row(norm_ffn[0])),
                                                            wup[0], cw[0], wdown[0], tm)
    ex_ffn0 = _send_start(ffn_grads(0, xf0, dhh0, act0, dhb2), True, "send_ffn0")
    dx, xa, dz, gated, dhb1, dws, dbs, dgv, dga = _mixa_bwd(dh1, x0, zg0, gp0, after_token(ex_ffn0["token"], row(norm_mix[0])), win, gv,
                                                            ws, wst, bsb, wout, tl)
    small_g = {
        'norm_mix': jnp.concatenate([dga, dgq], axis=0), 'norm_ffn': jnp.concatenate([dgf0, dgf1], axis=0),
        'norm_ple': jnp.concatenate([dgp0, dgp1], axis=0), 'norm_kv': dgkv.reshape(D),
        'norm_final': d_norm_final.reshape(D), 'a_w_s': dws[None], 'a_b_s': dbs[None, :, :, 0],
        'b_sinks': dsink[:, 0].reshape(1, N_Q),
        'f_conv_b': jnp.stack([dcb0.reshape(2 * D_FF), dcb1.reshape(2 * D_FF)]),
        'ple_b_gate': jnp.concatenate([dbg0, dbg1], axis=0),
        'a_norm_v': dgv, 'f_conv_w': jnp.stack([dcw0, dcw1], axis=1),
    }
    names_all = _REPL + _SMALL_SHARDED
    full_shapes = [small_g[n].shape for n in names_all]
    small_packed = _pack([small_g[n] for n in names_all], 8)
    ex_small = _send_start([small_packed.reshape(N_DEV, small_packed.shape[0] // N_DEV, ROW)], True, "send_small")
    ex_a_in = _send_start([_wgrad(xa, dz, "wgrad_a_in", tn=2 * AW // N_DEV, out_blocked=True, a_transposed=True,
                                  per_step=4)], True, "send_a_in")
    ex_a_out = _send_start([row_blocks(_wgrad(gated, dhb1, "wgrad_a_out", tn=1024))],
                           True, "send_a_out")

    def as3(a):
        return a.reshape((-1,) + a.shape[-2:])

    def adam(name, parts):
        outs = _adamw_weight(parts, as3(w[name]), as3(mom_m[name]), as3(mom_v[name]), "adamw_" + name)
        return [o.reshape(w[name].shape) for o in outs]

    big = {}
    l_ple1 = _send_wait(ex_ple1, list(range(len(ex_ple1["lands"]))), ex_a_out["token"], "recv_ple1")
    l_ffn1 = _send_wait(ex_ffn1, list(range(len(ex_ffn1["lands"]))), l_ple1[0], "recv_ffn1")
    l_att = _send_wait(ex_att, list(range(len(ex_att["lands"]))), l_ffn1[0], "recv_att")
    big['b_w_o'], big['b_w_q'], big['w_kv'] = adam('b_w_o', [l_att[0]]), adam('b_w_q', [l_att[1]]), adam('w_kv', [l_att[2]])
    l_ple0 = _send_wait(ex_ple0, list(range(len(ex_ple0["lands"]))), big['w_kv'][0], "recv_ple0")
    big['ple_w_in'] = adam('ple_w_in', [l_ple0[0], l_ple1[0]])
    big['ple_w_gate'] = adam('ple_w_gate', [l_ple0[1], l_ple1[1]])
    small_sum = _sum8(_send_wait(ex_small, [0], big['ple_w_gate'][0], "recv_small")[0], "sum_small_grads")
    ex_small_all = _send_start([small_sum], False, "gather_small_sums")
    l_ffn0 = _send_wait(ex_ffn0, list(range(len(ex_ffn0["lands"]))), ex_small_all["token"], "recv_ffn0")
    big['f_w_up'] = adam('f_w_up', [l_ffn0[0], l_ffn1[0]])
    big['f_w_down'] = adam('f_w_down', [l_ffn0[1], l_ffn1[1]])
    big['a_w_in'] = adam('a_w_in', _send_wait(ex_a_in, [0], big['f_w_down'][0], "recv_a_in"))
    big['a_w_out'] = adam('a_w_out', _send_wait(ex_a_out, [0], big['a_w_in'][0], "recv_a_out"))
    g_big, d_big, m_big, v_big = [{n: big[n][i] for n in _BIG} for i in range(4)]

    gsum = _send_wait(ex_small_all, [0], big['a_w_out'][0], "recv_small_sums")[0].reshape(small_packed.shape)
    gs = dict(zip(names_all, _unpack(gsum, full_shapes, 8)))
    gs['a_norm_v'] = lax.dynamic_slice_in_dim(gs['a_norm_v'], me * (AW // N_DEV), AW // N_DEV, axis=1)
    gs['f_conv_w'] = lax.dynamic_index_in_dim(gs['f_conv_w'], me, axis=0, keepdims=False)
    gs = {n: gs[n].reshape(w[n].shape) for n in names_all}
    packs = lambda d: _pack([d[n] for n in names_all], 8)
    souts = _adamw_small(packs(gs), packs(w), packs(mom_m), packs(mom_v), "adamw_small")
    shp = [w[n].shape for n in names_all]
    d_s, m_s, v_s = [dict(zip(names_all, _unpack(o, shp, 8))) for o in souts]

    grads = {**g_big, **gs}
    delta = {**d_big, **d_s}
    new_m = {**m_big, **m_s}
    new_v = {**v_big, **v_s}
    loss = lax.psum(loss_blk[0, 0], ("x", "y", "c"))
    return (loss, dx[None], *[grads[n] for n in _PARAMS], *[delta[n] for n in _PARAMS],
            *[new_m[n] for n in _PARAMS], *[new_v[n] for n in _PARAMS])
```

```python
import functools

import numpy as np
import jax
import jax.numpy as jnp
from jax import lax
from jax.experimental import pallas as pl
from jax.experimental.pallas import tpu as pltpu

F32 = jnp.float32
BF = jnp.bfloat16

N_DEV = 8
D = 1024
AW = 1024
GROUPS = 8
CHUNK = 128
HD = 64
N_Q = 16
N_KV = 4
GQA = 4
BLOCK = 128
D_FF = 2816
FF_BLK = 704
N_FF_BLK = 4
PLE = 256
EPS = 1e-6
LR, B1, B2, ADAM_EPS, WD, STEP = 0.001, 0.9, 0.999, 1e-08, 0.01, 10
GELU_C0 = 0.7978845608028654
GELU_C1 = 0.044715
NEG = -1e30
ROW = 1024
VMEM_BIG = 56 * 1024 * 1024

_PARAMS = ['norm_mix', 'norm_ffn', 'norm_ple', 'norm_kv', 'norm_final', 'a_w_in', 'a_norm_v', 'a_w_s', 'a_b_s',
           'a_w_out', 'w_kv', 'b_w_q', 'b_sinks', 'b_w_o', 'f_w_up', 'f_conv_w', 'f_conv_b', 'f_w_down',
           'ple_w_in', 'ple_w_gate', 'ple_b_gate']
_BIG = ['a_w_in', 'a_w_out', 'w_kv', 'b_w_q', 'b_w_o', 'f_w_up', 'f_w_down', 'ple_w_in', 'ple_w_gate']
_SMALL_SHARDED = ['a_norm_v', 'f_conv_w']
_REPL = ['norm_mix', 'norm_ffn', 'norm_ple', 'norm_kv', 'norm_final', 'a_w_s', 'a_b_s', 'b_sinks', 'f_conv_b',
         'ple_b_gate']


def _mm(a, b):
    return jnp.dot(a, b, preferred_element_type=F32)


def _mm_nt(a, b):
    return lax.dot_general(a, b, (((1,), (1,)), ((), ())), preferred_element_type=F32)


def _mm_tn(a, b):
    return lax.dot_general(a, b, (((0,), (0,)), ((), ())), preferred_element_type=F32)


def _const_spec(shape, single=False):
    nd = len(shape)
    if single:
        return pl.BlockSpec(shape, lambda *_: (0,) * nd, pipeline_mode=pl.Buffered(1))
    return pl.BlockSpec(shape, lambda *_: (0,) * nd)


def _row_spec(tm, n):
    return pl.BlockSpec((tm, n), lambda i: (i, 0))


def _rms_fwd(x, g):
    r = lax.rsqrt(jnp.mean(x * x, axis=-1, keepdims=True) + EPS)
    xhat = x * r
    return xhat * g, xhat, r


def _rms_bwd(dy, xhat, r, g):
    a = dy * g
    dx = r * (a - xhat * jnp.mean(a * xhat, axis=-1, keepdims=True))
    return dx, jnp.sum(dy * xhat, axis=0, keepdims=True)


def _sigmoid(x):
    return 1.0 / (1.0 + jnp.exp(-x))


def _params(vmem=None):
    return pltpu.CompilerParams(dimension_semantics=("arbitrary",), vmem_limit_bytes=vmem)


def _tri_masks():
    t = lax.broadcasted_iota(jnp.int32, (CHUNK, CHUNK), 0)
    s = lax.broadcasted_iota(jnp.int32, (CHUNK, CHUNK), 1)
    return t >= s, s >= t


def _mixa_fwd(x, g, win, gv, ws, bsb, wout, tm):
    S = x.shape[0]
    nc = tm // CHUNK

    def body(x_ref, g_ref, win_ref, gv_ref, ws_ref, bsb_ref, wout_ref, h_ref, zg_ref, gp_ref, gated):
        x = x_ref[...]
        xn, _, _ = _rms_fwd(x, g_ref[...])
        z = _mm(xn.astype(BF), win_ref[...])
        z2 = z * z
        t = jnp.tanh(GELU_C0 * (z + GELU_C1 * z2 * z))
        zg = 0.5 * z * (1.0 + t)
        zg_ref[...] = zg.astype(BF)
        gp_ref[...] = (0.5 * (1.0 + t) + 0.5 * z * (1.0 - t * t) * (GELU_C0 * (1.0 + 3.0 * GELU_C1 * z2))).astype(BF)
        u = zg[:, :AW]
        vn, _, _ = _rms_fwd(zg[:, AW:], gv_ref[...])
        vn = vn.astype(BF)
        tril, _ = _tri_masks()
        for h in range(GROUPS):
            wm = jnp.where(tril, ws_ref[h], jnp.zeros((), BF))
            for c in range(nc):
                rs, cs = slice(c * CHUNK, (c + 1) * CHUNK), slice(h * CHUNK, (h + 1) * CHUNK)
                s = _mm(wm, vn[rs, cs]) + bsb_ref[h]
                gated[rs, cs] = (u[rs, cs] * s).astype(BF)
        h_ref[...] = x + _mm(gated[...], wout_ref[...])

    return pl.pallas_call(
        body, name="mixa_fwd", grid=(S // tm,),
        in_specs=[_row_spec(tm, D), _const_spec((1, D)), _const_spec((D, 2 * AW)), _const_spec((1, AW)),
                  _const_spec((GROUPS, CHUNK, CHUNK)), _const_spec((GROUPS, CHUNK, CHUNK)), _const_spec((AW, D))],
        out_specs=[_row_spec(tm, D), _row_spec(tm, 2 * AW), _row_spec(tm, 2 * AW)],
        out_shape=[jax.ShapeDtypeStruct((S, D), F32), jax.ShapeDtypeStruct((S, 2 * AW), BF),
                   jax.ShapeDtypeStruct((S, 2 * AW), BF)],
        scratch_shapes=[pltpu.VMEM((tm, AW), BF)],
        compiler_params=_params(VMEM_BIG),
    )(x, g, win, gv, ws, bsb, wout)


def _mixa_bwd(dh, x, zg, gp, g, win, gv, ws, wst, bsb, wout, tm):
    S = x.shape[0]
    nt = S // tm
    nc = tm // CHUNK

    def body(dh_ref, x_ref, zg_ref, gp_ref, g_ref, win_ref, gv_ref, ws_ref, wst_ref, bsb_ref, wout_ref,
             dx_ref, xn_ref, dz_ref, gated_ref, dhb_ref, dws_ref, dbs_ref, dgv_ref, dg_ref, du_scr, dvn_scr):
        i = pl.program_id(0)

        @pl.when(i == 0)
        def _():
            dws_ref[...] = jnp.zeros_like(dws_ref)
            dbs_ref[...] = jnp.zeros_like(dbs_ref)
            dgv_ref[...] = jnp.zeros_like(dgv_ref)
            dg_ref[...] = jnp.zeros_like(dg_ref)

        x = x_ref[...]
        gmix = g_ref[...]
        xn, xhat, r = _rms_fwd(x, gmix)
        xn_ref[...] = xn.T.astype(BF)
        zg = zg_ref[...].astype(F32)
        gp = gp_ref[...].astype(F32)
        u = zg[:, :AW]
        gvv = gv_ref[...]
        vn, vhat, rv = _rms_fwd(zg[:, AW:], gvv)
        vn = vn.astype(BF)
        dh = dh_ref[...]
        dhb = dh.astype(BF)
        dhb_ref[...] = dhb
        dgated = _mm_nt(dhb, wout_ref[...])
        tril, triu = _tri_masks()
        for h in range(GROUPS):
            wm = jnp.where(tril, ws_ref[h], jnp.zeros((), BF))
            wmt = jnp.where(triu, wst_ref[h], jnp.zeros((), BF))
            for c in range(nc):
                rs, cs = slice(c * CHUNK, (c + 1) * CHUNK), slice(h * CHUNK, (h + 1) * CHUNK)
                vnb = vn[rs, cs]
                s = _mm(wm, vnb) + bsb_ref[h]
                gated_ref[rs, cs] = (u[rs, cs] * s).astype(BF)
                dgt = dgated[rs, cs]
                du_scr[rs, cs] = dgt * s
                ds = dgt * u[rs, cs]
                dsb = ds.astype(BF)
                dws_ref[h] += jnp.where(tril, _mm_nt(dsb, vnb), 0.0)
                dbs_ref[h] += ds
                dvn_scr[rs, cs] = _mm(wmt, dsb)
        dvn = dvn_scr[...]
        dv, dgv = _rms_bwd(dvn, vhat, rv, gvv)
        dgv_ref[...] += dgv
        dz_ref[:, :AW] = (du_scr[...] * gp[:, :AW]).astype(BF)
        dz_ref[:, AW:] = (dv * gp[:, AW:]).astype(BF)
        dxn = _mm_nt(dz_ref[...], win_ref[...])
        dxr, dg = _rms_bwd(dxn, xhat, r, gmix)
        dg_ref[...] += dg
        dx_ref[...] = dh + dxr

        @pl.when(i == nt - 1)
        def _():
            for h in range(GROUPS):
                dbs_ref[h] = jnp.broadcast_to(jnp.sum(dbs_ref[h], axis=-1, keepdims=True), (CHUNK, CHUNK))

    gshape = (GROUPS, CHUNK, CHUNK)
    return pl.pallas_call(
        body, name="mixa_bwd", grid=(nt,),
        in_specs=[_row_spec(tm, D), _row_spec(tm, D), _row_spec(tm, 2 * AW), _row_spec(tm, 2 * AW), _const_spec((1, D)),
                  _const_spec((D, 2 * AW)), _const_spec((1, AW)), _const_spec(gshape), _const_spec(gshape),
                  _const_spec(gshape), _const_spec((AW, D))],
        out_specs=[_row_spec(tm, D), pl.BlockSpec((D, tm), lambda i: (0, i)), _row_spec(tm, 2 * AW), _row_spec(tm, AW),
                   _row_spec(tm, D), _const_spec(gshape), _const_spec(gshape), _const_spec((1, AW)), _const_spec((1, D))],
        out_shape=[jax.ShapeDtypeStruct((S, D), F32), jax.ShapeDtypeStruct((D, S), BF),
                   jax.ShapeDtypeStruct((S, 2 * AW), BF), jax.ShapeDtypeStruct((S, AW), BF),
                   jax.ShapeDtypeStruct((S, D), BF), jax.ShapeDtypeStruct(gshape, F32),
                   jax.ShapeDtypeStruct(gshape, F32), jax.ShapeDtypeStruct((1, AW), F32),
                   jax.ShapeDtypeStruct((1, D), F32)],
        scratch_shapes=[pltpu.VMEM((tm, AW), F32), pltpu.VMEM((tm, AW), F32)],
        compiler_params=_params(VMEM_BIG),
    )(dh, x, zg, gp, g, win, gv, ws, wst, bsb, wout)


def _shift_down(x, prev8, k):
    n, c = x.shape[0] // 8, x.shape[1]
    r = pltpu.roll(x.reshape(n, 8, c), k, axis=1)
    before = jnp.concatenate([pltpu.roll(prev8.reshape(1, 8, c), k, axis=1), r[:-1]], axis=0)
    sub = lax.broadcasted_iota(jnp.int32, (1, 8, c), 1)
    return jnp.where(sub < k, before, r).reshape(x.shape)


def _shift_up(x, next8, k):
    n, c = x.shape[0] // 8, x.shape[1]
    r = pltpu.roll(x.reshape(n, 8, c), 8 - k, axis=1)
    after = jnp.concatenate([r[1:], pltpu.roll(next8.reshape(1, 8, c), 8 - k, axis=1)], axis=0)
    sub = lax.broadcasted_iota(jnp.int32, (1, 8, c), 1)
    return jnp.where(sub >= 8 - k, after, r).reshape(x.shape)


def _ffn_fwd(h, g, wup, cw, cb, wdown, tm):
    S = h.shape[0]

    def body(h_ref, g_ref, wup_ref, cw_ref, cb_ref, wdown_ref, o_ref, hh_ref, cc_ref, sh):
        i = pl.program_id(0)

        @pl.when(i == 0)
        def _():
            sh[...] = jnp.zeros_like(sh)

        x = h_ref[...]
        xn, _, _ = _rms_fwd(x, g_ref[...])
        xn = xn.astype(BF)
        acc = x
        up = lambda d: [_mm(xn, wup_ref[d]), _mm(xn, wup_ref[N_FF_BLK + d])]
        ahead = up(0)
        down = lambda d, act: _mm(act, wdown_ref[d * FF_BLK:(d + 1) * FF_BLK, :])
        behind = None
        for d in range(N_FF_BLK):
            cs = []
            hbs = ahead
            if d + 1 < N_FF_BLK:
                ahead = up(d + 1)
            if behind is not None:
                acc = acc + down(d - 1, behind)
            for blk, hb in zip((d, N_FF_BLK + d), hbs):
                hh_ref[blk] = hb.astype(BF)
                w, prev8 = cw_ref[blk], sh[blk]
                c = cb_ref[blk] + w[0:1] * _shift_down(hb, prev8, 2) + w[1:2] * _shift_down(hb, prev8, 1) + w[2:3] * hb
                sh[blk] = hb[tm - 8:, :]
                cc_ref[blk] = c.astype(BF)
                cs.append(c)
            behind = (cs[0] * _sigmoid(cs[0]) * cs[1]).astype(BF)
        o_ref[...] = acc + down(N_FF_BLK - 1, behind)

    blk3 = pl.BlockSpec((N_DEV, tm, FF_BLK), lambda i: (0, i, 0))
    return pl.pallas_call(
        body, name="ffn_fwd", grid=(S // tm,),
        in_specs=[_row_spec(tm, D), _const_spec((1, D)), _const_spec((N_DEV, D, FF_BLK), True),
                  _const_spec((N_DEV, 3, FF_BLK)), _const_spec((N_DEV, 1, FF_BLK)), _const_spec((D_FF, D), True)],
        out_specs=[_row_spec(tm, D), blk3, blk3],
        out_shape=[jax.ShapeDtypeStruct((S, D), F32), jax.ShapeDtypeStruct((N_DEV, S, FF_BLK), BF),
                   jax.ShapeDtypeStruct((N_DEV, S, FF_BLK), BF)],
        scratch_shapes=[pltpu.VMEM((N_DEV, 8, FF_BLK), F32)],
        compiler_params=_params(VMEM_BIG),
    )(h, g, wup, cw, cb, wdown)


def _ffn_bwd(dh, h, hh, cc, g, wup, cw, wdown, tm):
    S = h.shape[0]
    nt = S // tm

    def rev(i):
        return (nt - 1 - i, 0)

    def body(dh_ref, h_ref, hh_ref, cc_ref, g_ref, wup_ref, cw_ref, wdown_ref,
             dx_ref, xn_ref, dhh_ref, act_ref, dhb_ref, dcw_ref, dcb_ref, dg_ref, sh):
        i = pl.program_id(0)

        @pl.when(i == 0)
        def _():
            sh[...] = jnp.zeros_like(sh)
            dcw_ref[...] = jnp.zeros_like(dcw_ref)
            dcb_ref[...] = jnp.zeros_like(dcb_ref)
            dg_ref[...] = jnp.zeros_like(dg_ref)

        x = h_ref[...]
        gain = g_ref[...]
        xn, xhat, r = _rms_fwd(x, gain)
        xn_ref[...] = xn.T.astype(BF)
        dh = dh_ref[...]
        dhb = dh.astype(BF)
        dhb_ref[...] = dhb
        dxn = jnp.zeros((tm, D), F32)
        dact_of = lambda d: _mm_nt(dhb, wdown_ref[d * FF_BLK:(d + 1) * FF_BLK, :])
        ahead = dact_of(0)
        for d in range(N_FF_BLK):
            blks = (d, N_FF_BLK + d)
            dact = ahead
            if d + 1 < N_FF_BLK:
                ahead = dact_of(d + 1)
            cg, cu = cc_ref[blks[0]].astype(F32), cc_ref[blks[1]].astype(F32)
            sg = _sigmoid(cg)
            sil = cg * sg
            act_ref[d] = (sil * cu).astype(BF)
            dcs = (dact * cu * (sg + sil - sil * sg), dact * sil)
            for blk, dc in zip(blks, dcs):
                next8 = sh[blk]
                dp1, dp2 = _shift_up(dc, next8, 1), _shift_up(dc, next8, 2)
                hb = hh_ref[blk].astype(F32)
                dcb_ref[blk] += jnp.sum(dc, axis=0, keepdims=True)
                dcw_ref[blk, 0:1, :] += jnp.sum(dp2 * hb, axis=0, keepdims=True)
                dcw_ref[blk, 1:2, :] += jnp.sum(dp1 * hb, axis=0, keepdims=True)
                dcw_ref[blk, 2:3, :] += jnp.sum(dc * hb, axis=0, keepdims=True)
                w = cw_ref[blk]
                dhh = (w[2:3] * dc + w[1:2] * dp1 + w[0:1] * dp2).astype(BF)
                sh[blk] = dc[0:8, :]
                dhh_ref[blk] = dhh
                dxn = dxn + _mm_nt(dhh, wup_ref[blk])
        dxr, dg = _rms_bwd(dxn, xhat, r, gain)
        dg_ref[...] += dg
        dx_ref[...] = dh + dxr

    blk3 = lambda i: (0, nt - 1 - i, 0)
    return pl.pallas_call(
        body, name="ffn_bwd", grid=(nt,),
        in_specs=[pl.BlockSpec((tm, D), rev), pl.BlockSpec((tm, D), rev),
                  pl.BlockSpec((N_DEV, tm, FF_BLK), blk3), pl.BlockSpec((N_DEV, tm, FF_BLK), blk3),
                  _const_spec((1, D)), _const_spec((N_DEV, D, FF_BLK), True), _const_spec((N_DEV, 3, FF_BLK)),
                  _const_spec((D_FF, D), True)],
        out_specs=[pl.BlockSpec((tm, D), rev), pl.BlockSpec((D, tm), lambda i: (0, nt - 1 - i)),
                   pl.BlockSpec((N_DEV, tm, FF_BLK), blk3), pl.BlockSpec((N_FF_BLK, tm, FF_BLK), blk3),
                   pl.BlockSpec((tm, D), rev), _const_spec((N_DEV, 3, FF_BLK)), _const_spec((N_DEV, 1, FF_BLK)),
                   _const_spec((1, D))],
        out_shape=[jax.ShapeDtypeStruct((S, D), F32), jax.ShapeDtypeStruct((D, S), BF),
                   jax.ShapeDtypeStruct((N_DEV, S, FF_BLK), BF), jax.ShapeDtypeStruct((N_FF_BLK, S, FF_BLK), BF),
                   jax.ShapeDtypeStruct((S, D), BF), jax.ShapeDtypeStruct((N_DEV, 3, FF_BLK), F32),
                   jax.ShapeDtypeStruct((N_DEV, 1, FF_BLK), F32), jax.ShapeDtypeStruct((1, D), F32)],
        scratch_shapes=[pltpu.VMEM((N_DEV, 8, FF_BLK), F32)],
        compiler_params=_params(VMEM_BIG),
    )(dh, h, hh, cc, g, wup, cw, wdown)


def _ple_fwd(h, g, wgate, bgate, p, wple, tm):
    S = h.shape[0]

    def body(h_ref, g_ref, wg_ref, bg_ref, p_ref, wp_ref, o_ref):
        x = h_ref[...]
        xn, _, _ = _rms_fwd(x, g_ref[...])
        gate = _sigmoid(_mm(xn.astype(BF), wg_ref[...]) + bg_ref[...])
        o_ref[...] = x + _mm(p_ref[...], wp_ref[...]) * gate

    return pl.pallas_call(
        body, name="ple_fwd", grid=(S // tm,),
        in_specs=[_row_spec(tm, D), _const_spec((1, D)), _const_spec((D, D)), _const_spec((1, D)),
                  _row_spec(tm, PLE), _const_spec((PLE, D))],
        out_specs=_row_spec(tm, D), out_shape=jax.ShapeDtypeStruct((S, D), F32),
        compiler_params=_params(),
    )(h, g, wgate, bgate, p, wple)


def _ple_bwd(dh, h, g, wgate, bgate, p, wple, tm):
    S = h.shape[0]

    def body(dh_ref, h_ref, g_ref, wg_ref, bg_ref, p_ref, wp_ref, dx_ref, de_ref, xn_ref, dpre_ref, dbg_ref, dg_ref):
        i = pl.program_id(0)

        @pl.when(i == 0)
        def _():
            dbg_ref[...] = jnp.zeros_like(dbg_ref)
            dg_ref[...] = jnp.zeros_like(dg_ref)

        x = h_ref[...]
        gain = g_ref[...]
        xn, xhat, r = _rms_fwd(x, gain)
        xnb = xn.astype(BF)
        xn_ref[...] = xnb
        gate = _sigmoid(_mm(xnb, wg_ref[...]) + bg_ref[...])
        e = _mm(p_ref[...], wp_ref[...])
        dh = dh_ref[...]
        de_ref[...] = (dh * gate).astype(BF)
        dpre = dh * e * gate * (1.0 - gate)
        dpreb = dpre.astype(BF)
        dpre_ref[...] = dpreb
        dbg_ref[...] += jnp.sum(dpre, axis=0, keepdims=True)
        dxr, dg = _rms_bwd(_mm_nt(dpreb, wg_ref[...]), xhat, r, gain)
        dg_ref[...] += dg
        dx_ref[...] = dh + dxr

    return pl.pallas_call(
        body, name="ple_bwd", grid=(S // tm,),
        in_specs=[_row_spec(tm, D), _row_spec(tm, D), _const_spec((1, D)), _const_spec((D, D)), _const_spec((1, D)),
                  _row_spec(tm, PLE), _const_spec((PLE, D))],
        out_specs=[_row_spec(tm, D), _row_spec(tm, D), _row_spec(tm, D), _row_spec(tm, D), _const_spec((1, D)),
                   _const_spec((1, D))],
        out_shape=[jax.ShapeDtypeStruct((S, D), F32), jax.ShapeDtypeStruct((S, D), BF),
                   jax.ShapeDtypeStruct((S, D), BF), jax.ShapeDtypeStruct((S, D), BF),
                   jax.ShapeDtypeStruct((1, D), F32), jax.ShapeDtypeStruct((1, D), F32)],
        compiler_params=_params(),
    )(dh, h, g, wgate, bgate, p, wple)


def _qkv_fwd(h, gq, wq, gkv, wkv, tm):
    S = h.shape[0]
    nkv = wkv.shape[1]

    def body(h_ref, gq_ref, wq_ref, gkv_ref, wkv_ref, q_ref, kv_ref):
        x = h_ref[...]
        _, xhat, _ = _rms_fwd(x, gq_ref[...])
        q_ref[...] = _mm((xhat * gq_ref[...]).astype(BF), wq_ref[...]).astype(BF)
        kv_ref[...] = _mm((xhat * gkv_ref[...]).astype(BF), wkv_ref[...]).astype(BF)

    return pl.pallas_call(
        body, name="qkv_fwd", grid=(S // tm,),
        in_specs=[_row_spec(tm, D), _const_spec((1, D)), _const_spec((D, D)), _const_spec((1, D)),
                  _const_spec((D, nkv))],
        out_specs=[_row_spec(tm, D), _row_spec(tm, nkv)],
        out_shape=[jax.ShapeDtypeStruct((S, D), BF), jax.ShapeDtypeStruct((S, nkv), BF)],
        compiler_params=_params(),
    )(h, gq, wq, gkv, wkv)


def _qkv_bwd(dh, h, dq, dkv, gq, wq, gkv, wkv, tm):
    S = h.shape[0]
    nkv = wkv.shape[1]

    def body(dh_ref, h_ref, dq_ref, dkv_ref, gq_ref, wq_ref, gkv_ref, wkv_ref,
             dx_ref, xq_ref, xkv_ref, dgq_ref, dgkv_ref):
        i = pl.program_id(0)

        @pl.when(i == 0)
        def _():
            dgq_ref[...] = jnp.zeros_like(dgq_ref)
            dgkv_ref[...] = jnp.zeros_like(dgkv_ref)

        x = h_ref[...]
        gq_, gkv_ = gq_ref[...], gkv_ref[...]
        _, xhat, r = _rms_fwd(x, gq_)
        xq_ref[...] = (xhat * gq_).astype(BF)
        xkv_ref[...] = (xhat * gkv_).astype(BF)
        d1, dg1 = _rms_bwd(_mm_nt(dq_ref[...], wq_ref[...]), xhat, r, gq_)
        d2, dg2 = _rms_bwd(_mm_nt(dkv_ref[...], wkv_ref[...]), xhat, r, gkv_)
        dgq_ref[...] += dg1
        dgkv_ref[...] += dg2
        dx_ref[...] = dh_ref[...] + d1 + d2

    return pl.pallas_call(
        body, name="qkv_bwd", grid=(S // tm,),
        in_specs=[_row_spec(tm, D), _row_spec(tm, D), _row_spec(tm, D), _row_spec(tm, nkv), _const_spec((1, D)),
                  _const_spec((D, D)), _const_spec((1, D)), _const_spec((D, nkv))],
        out_specs=[_row_spec(tm, D), _row_spec(tm, D), _row_spec(tm, D), _const_spec((1, D)), _const_spec((1, D))],
        out_shape=[jax.ShapeDtypeStruct((S, D), F32), jax.ShapeDtypeStruct((S, D), BF),
                   jax.ShapeDtypeStruct((S, D), BF), jax.ShapeDtypeStruct((1, D), F32),
                   jax.ShapeDtypeStruct((1, D), F32)],
        compiler_params=_params(),
    )(dh, h, dq, dkv, gq, wq, gkv, wkv)


def _oproj_fwd(a, w, res, tm):
    S = a.shape[0]

    def body(a_ref, w_ref, r_ref, o_ref):
        o_ref[...] = r_ref[...] + _mm(a_ref[...], w_ref[...])

    return pl.pallas_call(
        body, name="oproj_fwd", grid=(S // tm,),
        in_specs=[_row_spec(tm, D), _const_spec((D, D)), _row_spec(tm, D)],
        out_specs=_row_spec(tm, D), out_shape=jax.ShapeDtypeStruct((S, D), F32),
        compiler_params=_params(),
    )(a, w, res)


def _oproj_bwd(dh, w, tm):
    S = dh.shape[0]

    def body(dh_ref, w_ref, da_ref, dhb_ref):
        dhb = dh_ref[...].astype(BF)
        dhb_ref[...] = dhb
        da_ref[...] = _mm_nt(dhb, w_ref[...]).astype(BF)

    return pl.pallas_call(
        body, name="oproj_bwd", grid=(S // tm,),
        in_specs=[_row_spec(tm, D), _const_spec((D, D))],
        out_specs=[_row_spec(tm, D), _row_spec(tm, D)],
        out_shape=[jax.ShapeDtypeStruct((S, D), BF), jax.ShapeDtypeStruct((S, D), BF)],
        compiler_params=_params(),
    )(dh, w)


def _alibi_slope(hq):
    return float(np.float32(2.0 ** (-8.0 * (hq + 1) / N_Q)))


def _attn_consts():
    W = N_KV * HD
    r = np.zeros((W, N_KV * W), np.float32)
    for kh in range(N_KV):
        for c in range(W):
            r[kh * HD + c % HD, kh * W + c] = 1.0
    qi = np.arange(BLOCK)[:, None]
    kj = np.arange(2 * BLOCK)[None, :]
    dist = qi + BLOCK - kj
    band = (dist >= 0) & (dist < BLOCK)
    bias = np.full((2, N_KV, GQA * BLOCK, 2 * BLOCK), NEG, np.float32)
    for later in (0, 1):
        valid = band & ((kj >= BLOCK) | bool(later))
        for kh in range(N_KV):
            for g in range(GQA):
                slope = np.float32(_alibi_slope(kh * GQA + g))
                bias[later, kh, g * BLOCK:(g + 1) * BLOCK] = np.where(valid, -slope * dist.astype(np.float32), NEG)
    lane_head = np.arange(W)[None, :] // HD
    maskq = (np.arange(GQA * BLOCK)[:, None] // BLOCK == lane_head).astype(np.float32)
    maskk = (np.arange(GQA * 2 * BLOCK)[:, None] // (2 * BLOCK) == lane_head).astype(np.float32)
    head_sum = (lane_head.T == np.arange(GQA * BLOCK)[None, :] // BLOCK).astype(np.float32)
    return dict(rep=jnp.asarray(r, BF), bias=jnp.asarray(bias), maskq=jnp.asarray(maskq, BF),
                maskqs=jnp.asarray(maskq * HD ** -0.5, BF),
                maskk=jnp.asarray(maskk, BF), head_sum=jnp.asarray(head_sum, BF))


def _sink_col(sink_ref, kh):
    rb = lax.broadcasted_iota(jnp.int32, (GQA * BLOCK, 1), 0) >> 7
    col = jnp.full((GQA * BLOCK, 1), sink_ref[kh * GQA + GQA - 1], F32)
    for g in range(GQA - 1):
        col = jnp.where(rb == g, sink_ref[kh * GQA + g], col)
    return col


def _stack4(a):
    return jnp.concatenate([a] * GQA, axis=0)


def _rows_to_lanes(a):
    return jnp.concatenate([a[g * BLOCK:(g + 1) * BLOCK] for g in range(GQA)], axis=1)


def _lanes_to_rows(a, n):
    return jnp.concatenate([a[:, g * n:(g + 1) * n] for g in range(GQA)], axis=0)


def _scores(qg, k4, bias, sink_col, maskqs):
    qs = _stack4(qg) * maskqs
    s = _mm_nt(qs, k4) + bias
    m = jnp.maximum(jnp.max(s, axis=-1, keepdims=True), sink_col)
    return qs, jnp.exp(s - m), jnp.exp(sink_col - m)


def _bands(kvc_ref, kvp_ref, rep_ref):
    W = N_KV * HD
    kvp, kvc = kvp_ref[...], kvc_ref[...]
    kband = jnp.concatenate([kvp[:, :W], kvc[:, :W]], axis=0)
    vband = jnp.concatenate([kvp[:, W:], kvc[:, W:]], axis=0)
    return _mm(kband, rep_ref[...]).astype(BF), _mm(vband, rep_ref[...]).astype(BF)


def _attn_fwd(q, kv, sinks, consts):
    S = q.shape[0]
    nb = S // BLOCK
    W = N_KV * HD

    def body(sink_ref, q_ref, kvc_ref, kvp_ref, rep_ref, bias_ref, maskqs_ref, maskk_ref, o_ref):
        maskqs, maskk = maskqs_ref[...], maskk_ref[...]
        lane_head = lax.broadcasted_iota(jnp.int32, (BLOCK, W), 1) >> 6
        k4all, v4all = _bands(kvc_ref, kvp_ref, rep_ref)
        for kh in range(N_KV):
            cols = slice(kh * W, (kh + 1) * W)
            _, e, es = _scores(q_ref[:, cols], k4all[:, cols], bias_ref[kh], _sink_col(sink_ref, kh), maskqs)
            vbd = jnp.concatenate([_stack4(v4all[:, cols]) * maskk, maskk], axis=1)
            nd = _mm(_rows_to_lanes(e.astype(BF)), vbd)
            es_l = jnp.broadcast_to(es[(GQA - 1) * BLOCK:], (BLOCK, W))
            for g in range(GQA - 1):
                es_l = jnp.where(lane_head == g, es[g * BLOCK:(g + 1) * BLOCK], es_l)
            o_ref[:, cols] = (nd[:, :W] / (nd[:, W:] + es_l)).astype(BF)

    return pl.pallas_call(
        body, name="attn_fwd", grid=(nb,),
        in_specs=[pl.BlockSpec(memory_space=pltpu.SMEM), _row_spec(BLOCK, D), _row_spec(BLOCK, 2 * W),
                  pl.BlockSpec((BLOCK, 2 * W), lambda i: (jnp.maximum(i - 1, 0), 0)), _const_spec((W, N_KV * W)),
                  pl.BlockSpec((None, N_KV, GQA * BLOCK, 2 * BLOCK), lambda i: (jnp.minimum(i, 1), 0, 0, 0)),
                  _const_spec((GQA * BLOCK, W)), _const_spec((GQA * 2 * BLOCK, W))],
        out_specs=_row_spec(BLOCK, D), out_shape=jax.ShapeDtypeStruct((S, D), BF),
        compiler_params=_params(),
    )(sinks, q, kv, kv, consts["rep"], consts["bias"], consts["maskqs"], consts["maskk"])


def _attn_bwd(q, kv, att, do, sinks, consts):
    S = q.shape[0]
    nb = S // BLOCK
    W = N_KV * HD

    def qmap(i):
        return (jnp.minimum(i, nb - 1), 0)

    def body(sink_ref, q_ref, kvc_ref, kvp_ref, att_ref, do_ref, rep_ref, bias_ref, maskq_ref, maskqs_ref, maskk_ref,
             hsum_ref, dq_ref, dkv_ref, dsink_ref, carry, sink_acc):
        i = pl.program_id(0)

        @pl.when(i == 0)
        def _():
            sink_acc[...] = jnp.zeros_like(sink_acc)
            carry[...] = jnp.zeros_like(carry)

        @pl.when(i < nb)
        def _():
            maskq, maskqs, maskk, head_sum = maskq_ref[...], maskqs_ref[...], maskk_ref[...], hsum_ref[...]
            ones_k = jnp.ones((2 * BLOCK, BLOCK), BF)
            k4all, v4all = _bands(kvc_ref, kvp_ref, rep_ref)
            dk4s, dv4s = [], []
            for kh in range(N_KV):
                cols = slice(kh * W, (kh + 1) * W)
                k4, v4 = k4all[:, cols], v4all[:, cols]
                qs, e, es = _scores(q_ref[:, cols], k4, bias_ref[kh], _sink_col(sink_ref, kh), maskqs)
                inv = 1.0 / (_mm(e.astype(BF), ones_k) + es)
                p = e * jnp.concatenate([inv, inv], axis=1)
                dog = do_ref[:, cols]
                dos = _stack4(dog) * maskq
                dp = _mm_nt(dos, v4)
                prod = (dog.astype(F32) * att_ref[:, cols].astype(F32)).astype(BF)
                delta = _lanes_to_rows(_mm(prod, head_sum), BLOCK)
                ds = (p * (dp - jnp.concatenate([delta, delta], axis=1))).astype(BF)
                sink_acc[kh] += es * inv * delta
                kbd = _stack4(k4) * maskk
                dq_ref[:, cols] = (_mm(_rows_to_lanes(ds), kbd) * (HD ** -0.5)).astype(BF)
                dk4s.append(_mm_tn(ds, qs).astype(BF))
                dv4s.append(_mm_tn(p.astype(BF), dos).astype(BF))
            dkband = _mm_nt(jnp.concatenate(dk4s, axis=1), rep_ref[...])
            dvband = _mm_nt(jnp.concatenate(dv4s, axis=1), rep_ref[...])
            dkv_ref[:, :W] = (carry[:, :W] + dkband[:BLOCK]).astype(BF)
            dkv_ref[:, W:] = (carry[:, W:] + dvband[:BLOCK]).astype(BF)
            carry[:, :W] = dkband[BLOCK:]
            carry[:, W:] = dvband[BLOCK:]

        @pl.when(i == nb)
        def _():
            dkv_ref[...] = carry[...].astype(BF)
            for hq in range(N_Q):
                kh, g = divmod(hq, GQA)
                dsink_ref[hq:hq + 1, :] = -jnp.sum(sink_acc[kh, g * BLOCK:(g + 1) * BLOCK, :], axis=0, keepdims=True)

    return pl.pallas_call(
        body, name="attn_bwd", grid=(nb + 1,),
        in_specs=[pl.BlockSpec(memory_space=pltpu.SMEM), pl.BlockSpec((BLOCK, D), qmap),
                  pl.BlockSpec((BLOCK, 2 * W), qmap),
                  pl.BlockSpec((BLOCK, 2 * W), lambda i: (jnp.maximum(jnp.minimum(i, nb - 1) - 1, 0), 0)),
                  pl.BlockSpec((BLOCK, D), qmap), pl.BlockSpec((BLOCK, D), qmap), _const_spec((W, N_KV * W)),
                  pl.BlockSpec((None, N_KV, GQA * BLOCK, 2 * BLOCK), lambda i: (jnp.minimum(i, 1), 0, 0, 0)),
                  _const_spec((GQA * BLOCK, W)), _const_spec((GQA * BLOCK, W)), _const_spec((GQA * 2 * BLOCK, W)),
                  _const_spec((W, GQA * BLOCK))],
        out_specs=[pl.BlockSpec((BLOCK, D), qmap),
                   pl.BlockSpec((BLOCK, 2 * W), lambda i: (jnp.maximum(i - 1, 0), 0)),
                   _const_spec((N_Q, BLOCK))],
        out_shape=[jax.ShapeDtypeStruct((S, D), BF), jax.ShapeDtypeStruct((S, 2 * W), BF),
                   jax.ShapeDtypeStruct((N_Q, BLOCK), F32)],
        scratch_shapes=[pltpu.VMEM((BLOCK, 2 * W), F32), pltpu.VMEM((N_KV, GQA * BLOCK, BLOCK), F32)],
        compiler_params=_params(),
    )(sinks, q, kv, kv, att, do, consts["rep"], consts["bias"], consts["maskq"], consts["maskqs"], consts["maskk"], consts["head_sum"])


def _head(h, g, target, tm):
    S = h.shape[0]

    def body(h_ref, g_ref, t_ref, dh_ref, loss_ref, dg_ref):
        i = pl.program_id(0)

        @pl.when(i == 0)
        def _():
            loss_ref[...] = jnp.zeros_like(loss_ref)
            dg_ref[...] = jnp.zeros_like(dg_ref)

        gain = g_ref[...]
        y, xhat, r = _rms_fwd(h_ref[...], gain)
        err = y - t_ref[...]
        tile = 0.5 * jnp.sum(jnp.sum(err * err, axis=-1, keepdims=True) * (1.0 / D), axis=0, keepdims=True)
        loss_ref[...] += jnp.broadcast_to(tile, loss_ref.shape)
        dx, dg = _rms_bwd(err * (1.0 / D), xhat, r, gain)
        dg_ref[...] += dg
        dh_ref[...] = dx

    return pl.pallas_call(
        body, name="loss_head", grid=(S // tm,),
        in_specs=[_row_spec(tm, D), _const_spec((1, D)), _row_spec(tm, D)],
        out_specs=[_row_spec(tm, D), _const_spec((8, 128)), _const_spec((1, D))],
        out_shape=[jax.ShapeDtypeStruct((S, D), F32), jax.ShapeDtypeStruct((8, 128), F32),
                   jax.ShapeDtypeStruct((1, D), F32)],
        compiler_params=_params(),
    )(h, g, target)


def _wgrad(a, b, name, *, tn, out_blocked=False, a_transposed=False, ts=2048, per_step=1):
    a_blocked, b_blocked = a.ndim == 3, b.ndim == 3
    na = a.shape[0] if a_blocked else 1
    S = b.shape[-2]
    K = a.shape[0] if a_transposed else a.shape[-1]
    wide = tn * per_step
    nbk = b.shape[0] if b_blocked else b.shape[1] // wide
    ts = min(ts, S)
    ns = S // ts
    assert not (a_blocked and (out_blocked or a_transposed)) and (per_step == 1 or (out_blocked and not b_blocked))

    def body(a_ref, b_ref, o_ref, acc):
        s = pl.program_id(2)

        @pl.when(s == 0)
        def _():
            acc[...] = jnp.zeros_like(acc)

        if a_transposed:
            acc[...] += _mm(a_ref[...], b_ref[...])
        else:
            acc[...] += _mm_tn(a_ref[...], b_ref[...])

        @pl.when(s == ns - 1)
        def _():
            if per_step == 1:
                o_ref[...] = acc[...].astype(o_ref.dtype)
            else:
                for k in range(per_step):
                    o_ref[k] = acc[:, k * tn:(k + 1) * tn].astype(o_ref.dtype)

    if a_transposed:
        a_spec = pl.BlockSpec((K, ts), lambda i, j, s: (0, s))
    elif a_blocked:
        a_spec = pl.BlockSpec((None, ts, K), lambda i, j, s: (i, s, 0))
    else:
        a_spec = pl.BlockSpec((ts, K), lambda i, j, s: (s, 0))
    b_spec = (pl.BlockSpec((None, ts, tn), lambda i, j, s: (j, s, 0)) if b_blocked
              else pl.BlockSpec((ts, wide), lambda i, j, s: (s, j)))
    if out_blocked:
        o_spec = pl.BlockSpec((None, K, tn) if per_step == 1 else (per_step, K, tn), lambda i, j, s: (j, 0, 0))
        o_shape = jax.ShapeDtypeStruct((nbk * per_step, K, tn), BF)
    else:
        o_spec = pl.BlockSpec((K, wide), lambda i, j, s: (i, j))
        o_shape = jax.ShapeDtypeStruct((na * K, nbk * wide), BF)
    return pl.pallas_call(
        body, name=name, grid=(na, nbk, ns), in_specs=[a_spec, b_spec], out_specs=o_spec, out_shape=o_shape,
        scratch_shapes=[pltpu.VMEM((K, wide), F32)],
        compiler_params=pltpu.CompilerParams(dimension_semantics=("arbitrary", "arbitrary", "arbitrary"),
                                             vmem_limit_bytes=VMEM_BIG),
    )(a, b)


def _adamw_math(w, g, m, v):
    m = B1 * m + (1.0 - B1) * g
    v = B2 * v + (1.0 - B2) * (g * g)
    m_hat = m / (1.0 - B1 ** STEP)
    v_hat = v / (1.0 - B2 ** STEP)
    return -LR * (m_hat / (jnp.sqrt(v_hat) + ADAM_EPS) + WD * w), m, v


def _sum8(parts, name):
    R = parts.shape[1]

    def body(p_ref, o_ref):
        g = p_ref[0]
        for s in range(1, N_DEV):
            g = g + p_ref[s]
        o_ref[...] = g

    return pl.pallas_call(
        body, name=name, grid=(1,),
        in_specs=[pl.BlockSpec((N_DEV, R, ROW), lambda i: (0, 0, 0))],
        out_specs=pl.BlockSpec((R, ROW), lambda i: (0, 0)), out_shape=jax.ShapeDtypeStruct((R, ROW), F32),
        compiler_params=_params(),
    )(parts)


def _adamw_small(g, w, m, v, name):
    R = w.shape[0]

    def body(g_ref, w_ref, m_ref, v_ref, d_ref, nm_ref, nv_ref):
        d_ref[...], nm_ref[...], nv_ref[...] = _adamw_math(w_ref[...], g_ref[...], m_ref[...], v_ref[...])

    spec = pl.BlockSpec((R, ROW), lambda i: (0, 0))
    return pl.pallas_call(
        body, name=name, grid=(1,), in_specs=[spec] * 4, out_specs=[spec] * 3,
        out_shape=[jax.ShapeDtypeStruct((R, ROW), F32)] * 3, compiler_params=_params(),
    )(g, w, m, v)


def _mesh_pos():
    return lax.axis_index("x"), lax.axis_index("y"), lax.axis_index("c")


def _flip(v, bit):
    return 1 - v if bit else v


def _peers(x, y, c):
    out = []
    for k in range(1, N_DEV):
        px, py, pc = _flip(x, k & 4), _flip(y, k & 2), _flip(c, k & 1)
        out.append((k - 1, (px, py, pc), 4 * px + 2 * py + pc))
    return out


def _all_gather_now(xs, name):
    n = len(xs)
    MESH = pl.DeviceIdType.MESH

    def body(*refs):
        x_refs, out_refs = refs[:n], refs[n:2 * n]
        send_sems, recv_sems, local_sems = refs[2 * n:]
        x, y, c = _mesh_pos()
        me, sibling = (x, y, c), (x, y, 1 - c)
        chips = [(1 - x, y), (x, 1 - y), (1 - x, 1 - y)]
        blk = lambda p: 4 * p[0] + 2 * p[1] + p[2]

        def copy(a, k, block, to, src=None):
            return pltpu.make_async_remote_copy(
                src_ref=out_refs[a].at[blk(block)] if src is None else src, dst_ref=out_refs[a].at[blk(block)],
                send_sem=send_sems.at[a, k], recv_sem=recv_sems.at[a, k], device_id=to, device_id_type=MESH)

        sent, own = [], []
        for a in range(n):
            own.append(pltpu.make_async_copy(x_refs[a], out_refs[a].at[blk(me)], local_sems.at[a]))
            own[-1].start()
            first = [copy(a, 1 + j, me, (*chip, c), src=x_refs[a]) for j, chip in enumerate(chips)]
            first.append(copy(a, 0, me, sibling, src=x_refs[a]))
            for cp in first:
                cp.start()
            sent += first
        for a in range(n):
            for j, chip in enumerate(chips):
                copy(a, 1 + j, (*chip, c), me).wait_recv()
                sent.append(copy(a, 4 + j, (*chip, c), sibling))
                sent[-1].start()
        for a in range(n):
            copy(a, 0, sibling, me).wait_recv()
            for j, chip in enumerate(chips):
                copy(a, 4 + j, (*chip, 1 - c), me).wait_recv()
        for cp in sent:
            cp.wait_send()
        for cp in own:
            cp.wait()

    hbm = pl.BlockSpec(memory_space=pl.ANY)
    return pl.pallas_call(
        body, name=name, out_shape=[jax.ShapeDtypeStruct((N_DEV,) + x.shape, x.dtype) for x in xs],
        in_specs=[hbm] * n, out_specs=[hbm] * n,
        scratch_shapes=[pltpu.SemaphoreType.DMA((n, 7)), pltpu.SemaphoreType.DMA((n, 7)), pltpu.SemaphoreType.DMA((n,))],
    )(*xs)


def _chip_peers(x, y, c):
    return (x, y, c), (x, y, 1 - c), [(1 - x, y, c), (x, 1 - y, c), (1 - x, 1 - y, c)]


def _blk(p):
    return 4 * p[0] + 2 * p[1] + p[2]


def _gather2_start(srcs, name):
    n = len(srcs)
    MESH = pl.DeviceIdType.MESH

    def body(*refs):
        src_refs, land_refs = refs[:n], refs[n:2 * n]
        send, recv_ici, recv_sib, local = (refs[(2 + k) * n:(3 + k) * n] for k in range(4))
        token = refs[8 * n]
        me, sibling, chips = _chip_peers(*_mesh_pos())
        for a in range(n):
            dst = land_refs[a].at[_blk(me)]
            pltpu.make_async_remote_copy(src_ref=src_refs[a], dst_ref=dst, send_sem=send[a].at[0], recv_sem=recv_sib[a],
                                         device_id=sibling, device_id_type=MESH).start()
            for j, chip in enumerate(chips):
                pltpu.make_async_remote_copy(src_ref=src_refs[a], dst_ref=dst, send_sem=send[a].at[1 + j],
                                             recv_sem=recv_ici[a].at[j], device_id=chip, device_id_type=MESH).start()
            pltpu.make_async_copy(src_refs[a], dst, local[a]).start()
        token[...] = jnp.zeros_like(token)

    hbm = pl.BlockSpec(memory_space=pltpu.HBM)
    sem = pl.BlockSpec(memory_space=pltpu.SEMAPHORE)
    dma = pltpu.SemaphoreType.DMA
    outs = pl.pallas_call(
        body, name=name,
        out_shape=([dma((4,))] * n + [dma((3,))] * n + [dma(())] * n + [dma(())] * n
                   + [pltpu.HBM(s.shape, s.dtype) for s in srcs]
                   + [pltpu.HBM((N_DEV,) + s.shape, s.dtype) for s in srcs] + [jax.ShapeDtypeStruct((8, 128), F32)]),
        in_specs=[hbm] * (2 * n), out_specs=[sem] * (4 * n) + [hbm] * (2 * n) + [pl.BlockSpec(memory_space=pltpu.VMEM)],
        input_output_aliases={**{a: 4 * n + a for a in range(n)}, **{n + a: 5 * n + a for a in range(n)}},
        compiler_params=pltpu.CompilerParams(has_side_effects=pltpu.SideEffectType.DATAFLOW_SIDE_EFFECTING),
    )(*[pltpu.with_memory_space_constraint(s, pltpu.HBM) for s in srcs],
      *[pltpu.with_memory_space_constraint(lax.empty((N_DEV,) + s.shape, s.dtype), pltpu.HBM) for s in srcs])
    keys = ["send", "recv_ici", "recv_sib", "local", "srcs", "lands"]
    return {**{k: outs[i * n:(i + 1) * n] for i, k in enumerate(keys)}, "token": outs[6 * n]}


def _gather2_forward(started, after, name):
    lands, recv_ici = started["lands"], started["recv_ici"]
    n = len(lands)
    MESH = pl.DeviceIdType.MESH

    def body(*refs):
        land_refs, recv_refs = refs[:n], refs[n:2 * n]
        fsend, frecv = refs[2 * n + 1 + n:2 * n + 1 + 2 * n], refs[2 * n + 1 + 2 * n:2 * n + 1 + 3 * n]
        me, sibling, chips = _chip_peers(*_mesh_pos())
        for a in range(n):
            for j, chip in enumerate(chips):
                rows = land_refs[a].at[_blk(chip)]
                pltpu.make_async_remote_copy(src_ref=rows, dst_ref=rows, send_sem=fsend[a].at[j], recv_sem=recv_refs[a].at[j],
                                             device_id=chip, device_id_type=MESH).wait_recv()
                pltpu.make_async_remote_copy(src_ref=rows, dst_ref=rows, send_sem=fsend[a].at[j], recv_sem=frecv[a].at[j],
                                             device_id=sibling, device_id_type=MESH).start()

    hbm = pl.BlockSpec(memory_space=pltpu.HBM)
    sem = pl.BlockSpec(memory_space=pltpu.SEMAPHORE)
    dma = pltpu.SemaphoreType.DMA
    outs = pl.pallas_call(
        body, name=name,
        out_shape=[pltpu.HBM(l.shape, l.dtype) for l in lands] + [dma((3,))] * (2 * n),
        in_specs=[hbm] * n + [sem] * n + [pl.BlockSpec(memory_space=pl.ANY)], out_specs=[hbm] * n + [sem] * (2 * n),
        input_output_aliases={a: a for a in range(n)},
        compiler_params=pltpu.CompilerParams(has_side_effects=pltpu.SideEffectType.DATAFLOW_SIDE_EFFECTING),
    )(*lands, *recv_ici, after)
    return {**started, "lands": outs[:n], "fsend": outs[n:2 * n], "frecv": outs[2 * n:3 * n]}


def _gather2_wait(st, after, name):
    n = len(st["lands"])
    MESH = pl.DeviceIdType.MESH

    def body(*refs):
        src_refs, land_refs = refs[:n], refs[n:2 * n]
        send, recv_sib, local, fsend, frecv = (refs[(2 + k) * n:(3 + k) * n] for k in range(5))
        me, sibling, chips = _chip_peers(*_mesh_pos())
        for a in range(n):
            def desc(src, dst, s, r):
                return pltpu.make_async_remote_copy(src_ref=src, dst_ref=dst, send_sem=s, recv_sem=r, device_id=sibling,
                                                    device_id_type=MESH)
            own = land_refs[a].at[_blk(me)]
            for k in range(4):
                desc(src_refs[a], own, send[a].at[k], recv_sib[a]).wait_send()
            desc(src_refs[a], land_refs[a].at[_blk(sibling)], send[a].at[0], recv_sib[a]).wait_recv()
            pltpu.make_async_copy(src_refs[a], own, local[a]).wait()
            for j, chip in enumerate(chips):
                mine, theirs = land_refs[a].at[_blk(chip)], land_refs[a].at[_blk((chip[0], chip[1], 1 - chip[2]))]
                desc(mine, mine, fsend[a].at[j], frecv[a].at[j]).wait_send()
                desc(theirs, theirs, fsend[a].at[j], frecv[a].at[j]).wait_recv()

    hbm = pl.BlockSpec(memory_space=pltpu.HBM)
    sem = pl.BlockSpec(memory_space=pltpu.SEMAPHORE)
    srcs, lands = st["srcs"], st["lands"]
    outs = pl.pallas_call(
        body, name=name, out_shape=[pltpu.HBM(s.shape, s.dtype) for s in list(srcs) + list(lands)],
        in_specs=[hbm] * (2 * n) + [sem] * (5 * n) + [pl.BlockSpec(memory_space=pl.ANY)], out_specs=[hbm] * (2 * n),
        input_output_aliases={a: a for a in range(2 * n)},
        compiler_params=pltpu.CompilerParams(has_side_effects=pltpu.SideEffectType.DATAFLOW_SIDE_EFFECTING),
    )(*srcs, *lands, *st["send"], *st["recv_sib"], *st["local"], *st["fsend"], *st["frecv"], after)
    return outs[n:]


def _send_start(srcs, blocked, name):
    n = len(srcs)
    MESH = pl.DeviceIdType.MESH

    def body(*refs):
        src_refs, land_refs = refs[:n], refs[n:2 * n]
        send_sems, recv_sems, local_sems = refs[2 * n:3 * n], refs[3 * n:4 * n], refs[4 * n:5 * n]
        token = refs[7 * n]
        x, y, c = _mesh_pos()
        me = 4 * x + 2 * y + c
        for a in range(n):
            for k, peer, blk in _peers(x, y, c):
                pltpu.make_async_remote_copy(
                    src_ref=src_refs[a].at[blk] if blocked else src_refs[a], dst_ref=land_refs[a].at[me],
                    send_sem=send_sems[a].at[k], recv_sem=recv_sems[a].at[k], device_id=peer,
                    device_id_type=MESH).start()
            pltpu.make_async_copy(src_refs[a].at[me] if blocked else src_refs[a], land_refs[a].at[me],
                                  local_sems[a]).start()
        token[...] = jnp.zeros_like(token)

    hbm = pl.BlockSpec(memory_space=pltpu.HBM)
    sem = pl.BlockSpec(memory_space=pltpu.SEMAPHORE)
    lshape = [s.shape if blocked else (N_DEV,) + s.shape for s in srcs]
    outs = pl.pallas_call(
        body, name=name,
        out_shape=([pltpu.SemaphoreType.DMA((N_DEV - 1,))] * (2 * n) + [pltpu.SemaphoreType.DMA(())] * n
                   + [pltpu.HBM(s.shape, s.dtype) for s in srcs]
                   + [pltpu.HBM(ls, s.dtype) for ls, s in zip(lshape, srcs)] + [jax.ShapeDtypeStruct((8, 128), F32)]),
        in_specs=[hbm] * (2 * n), out_specs=[sem] * (3 * n) + [hbm] * (2 * n) + [pl.BlockSpec(memory_space=pltpu.VMEM)],
        input_output_aliases={**{a: 3 * n + a for a in range(n)}, **{n + a: 4 * n + a for a in range(n)}},
        compiler_params=pltpu.CompilerParams(has_side_effects=pltpu.SideEffectType.DATAFLOW_SIDE_EFFECTING),
    )(*[pltpu.with_memory_space_constraint(s, pltpu.HBM) for s in srcs],
      *[pltpu.with_memory_space_constraint(lax.empty(ls, s.dtype), pltpu.HBM) for ls, s in zip(lshape, srcs)])
    return dict(send=outs[:n], recv=outs[n:2 * n], local=outs[2 * n:3 * n], srcs=outs[3 * n:4 * n],
                lands=outs[4 * n:5 * n], token=outs[5 * n], blocked=blocked)


def _send_wait(started, which, after, name):
    n = len(which)
    blocked = started["blocked"]
    pick = lambda key: [started[key][a] for a in which]
    MESH = pl.DeviceIdType.MESH

    def body(*refs):
        src_refs, land_refs = refs[:n], refs[n:2 * n]
        send, recv, local = refs[2 * n:3 * n], refs[3 * n:4 * n], refs[4 * n:5 * n]
        x, y, c = _mesh_pos()
        me = 4 * x + 2 * y + c
        for a in range(n):
            for k, peer, blk in _peers(x, y, c):
                cp = pltpu.make_async_remote_copy(
                    src_ref=src_refs[a].at[blk] if blocked else src_refs[a], dst_ref=land_refs[a].at[blk],
                    send_sem=send[a].at[k], recv_sem=recv[a].at[k], device_id=peer, device_id_type=MESH)
                cp.wait_send()
                cp.wait_recv()
            pltpu.make_async_copy(src_refs[a].at[me] if blocked else src_refs[a], land_refs[a].at[me],
                                  local[a]).wait()

    hbm = pl.BlockSpec(memory_space=pltpu.HBM)
    sem = pl.BlockSpec(memory_space=pltpu.SEMAPHORE)
    srcs, lands = pick("srcs"), pick("lands")
    outs = pl.pallas_call(
        body, name=name, out_shape=[pltpu.HBM(s.shape, s.dtype) for s in srcs + lands],
        in_specs=[hbm] * (2 * n) + [sem] * (3 * n) + [pl.BlockSpec(memory_space=pl.ANY)], out_specs=[hbm] * (2 * n),
        input_output_aliases={a: a for a in range(2 * n)},
        compiler_params=pltpu.CompilerParams(has_side_effects=pltpu.SideEffectType.DATAFLOW_SIDE_EFFECTING),
    )(*srcs, *lands, *pick("send"), *pick("recv"), *pick("local"), after)
    return outs[n:]


def _adamw_weight(parts, w, m, v, name):
    L, R, C = w.shape
    tr = max(t for t in range(16, min(R, 256) + 1, 16) if R % t == 0)
    nr = R // tr

    def body(*refs):
        p_refs = refs[:L]
        w_ref, m_ref, v_ref, g_ref, d_ref, nm_ref, nv_ref, gsum = refs[L:]
        layer = pl.program_id(0)
        for l in range(L):
            @pl.when(layer == l)
            def _(l=l):
                g = p_refs[l][0].astype(F32)
                for s in range(1, N_DEV):
                    g = g + p_refs[l][s].astype(F32)
                gsum[...] = g
        g = gsum[...]
        g_ref[...] = g
        d_ref[...], nm_ref[...], nv_ref[...] = _adamw_math(w_ref[...], g, m_ref[...], v_ref[...])

    def part_spec(l):
        return pl.BlockSpec((N_DEV, tr, C), lambda layer, i: (0, jnp.where(layer == l, i, jnp.where(layer < l, 0, nr - 1)), 0))

    ws = pl.BlockSpec((None, tr, C), lambda layer, i: (layer, i, 0))
    return pl.pallas_call(
        body, name=name, grid=(L, nr), in_specs=[part_spec(l) for l in range(L)] + [ws] * 3, out_specs=[ws] * 4,
        out_shape=[jax.ShapeDtypeStruct((L, R, C), F32)] * 4, scratch_shapes=[pltpu.VMEM((tr, C), F32)],
        compiler_params=pltpu.CompilerParams(dimension_semantics=("arbitrary", "arbitrary")),
    )(*parts, w, m, v)


def _seg_rows(n, align):
    rows = -(-n // ROW)
    return -(-rows // align) * align


def _pack(arrs, align, lead=()):
    parts = []
    for a in arrs:
        n = int(np.prod(a.shape[len(lead):]))
        rows = _seg_rows(n, align)
        flat = a.reshape(lead + (n,))
        flat = jnp.pad(flat, [(0, 0)] * len(lead) + [(0, rows * ROW - n)])
        parts.append(flat.reshape(lead + (rows, ROW)))
    return jnp.concatenate(parts, axis=len(lead))


def _unpack(buf, shapes, align, lead=()):
    out, r0 = [], 0
    for shp in shapes:
        n = int(np.prod(shp))
        rows = _seg_rows(n, align)
        seg = lax.slice_in_dim(buf, r0, r0 + rows, axis=len(lead))
        out.append(seg.reshape(lead + (rows * ROW,))[..., :n].reshape(lead + tuple(shp)))
        r0 += rows
    return out


def kernel(x, p, norm_mix, norm_ffn, norm_ple, norm_kv, norm_final, a_w_in, a_norm_v, a_w_s, a_b_s, a_w_out, w_kv, b_w_q, b_sinks, b_w_o, f_w_up, f_conv_w, f_conv_b, f_w_down, ple_w_in, ple_w_gate, ple_b_gate, loss_target, m_norm_mix, m_norm_ffn, m_norm_ple, m_norm_kv, m_norm_final, m_a_w_in, m_a_norm_v, m_a_w_s, m_a_b_s, m_a_w_out, m_w_kv, m_b_w_q, m_b_sinks, m_b_w_o, m_f_w_up, m_f_conv_w, m_f_conv_b, m_f_w_down, m_ple_w_in, m_ple_w_gate, m_ple_b_gate, v_norm_mix, v_norm_ffn, v_norm_ple, v_norm_kv, v_norm_final, v_a_w_in, v_a_norm_v, v_a_w_s, v_a_b_s, v_a_w_out, v_w_kv, v_b_w_q, v_b_sinks, v_b_w_o, v_f_w_up, v_f_conv_w, v_f_conv_b, v_f_w_down, v_ple_w_in, v_ple_w_gate, v_ple_b_gate):
    w = dict(norm_mix=norm_mix, norm_ffn=norm_ffn, norm_ple=norm_ple, norm_kv=norm_kv, norm_final=norm_final,
             a_w_in=a_w_in, a_norm_v=a_norm_v, a_w_s=a_w_s, a_b_s=a_b_s, a_w_out=a_w_out, w_kv=w_kv, b_w_q=b_w_q,
             b_sinks=b_sinks, b_w_o=b_w_o, f_w_up=f_w_up, f_conv_w=f_conv_w, f_conv_b=f_conv_b, f_w_down=f_w_down,
             ple_w_in=ple_w_in, ple_w_gate=ple_w_gate, ple_b_gate=ple_b_gate)
    mom_m = dict(norm_mix=m_norm_mix, norm_ffn=m_norm_ffn, norm_ple=m_norm_ple, norm_kv=m_norm_kv,
                 norm_final=m_norm_final, a_w_in=m_a_w_in, a_norm_v=m_a_norm_v, a_w_s=m_a_w_s, a_b_s=m_a_b_s,
                 a_w_out=m_a_w_out, w_kv=m_w_kv, b_w_q=m_b_w_q, b_sinks=m_b_sinks, b_w_o=m_b_w_o, f_w_up=m_f_w_up,
                 f_conv_w=m_f_conv_w, f_conv_b=m_f_conv_b, f_w_down=m_f_w_down, ple_w_in=m_ple_w_in,
                 ple_w_gate=m_ple_w_gate, ple_b_gate=m_ple_b_gate)
    mom_v = dict(norm_mix=v_norm_mix, norm_ffn=v_norm_ffn, norm_ple=v_norm_ple, norm_kv=v_norm_kv,
                 norm_final=v_norm_final, a_w_in=v_a_w_in, a_norm_v=v_a_norm_v, a_w_s=v_a_w_s, a_b_s=v_a_b_s,
                 a_w_out=v_a_w_out, w_kv=v_w_kv, b_w_q=v_b_w_q, b_sinks=v_b_sinks, b_w_o=v_b_w_o, f_w_up=v_f_w_up,
                 f_conv_w=v_f_conv_w, f_conv_b=v_f_conv_b, f_w_down=v_f_w_down, ple_w_in=v_ple_w_in,
                 ple_w_gate=v_ple_w_gate, ple_b_gate=v_ple_b_gate)
    S = x.shape[1]
    tm = min(256, S)
    tl = min(512, S)
    me = 4 * lax.axis_index("x") + 2 * lax.axis_index("y") + lax.axis_index("c")

    def after_token(tok, a):
        return a + tok[0:1, 0:1].reshape((1,) * a.ndim)

    wb = {n: w[n].astype(BF) for n in _BIG}
    order = [('a_w_in', 0), ('a_w_out', 0), ('a_norm_v', None), ('f_conv_w', 0), ('f_conv_w', 1),
             ('f_w_up', 0), ('f_w_down', 0),
             ('ple_w_in', 0), ('ple_w_gate', 0), ('w_kv', None), ('b_w_q', 0), ('b_w_o', 0),
             ('f_w_up', 1), ('f_w_down', 1), ('ple_w_in', 1), ('ple_w_gate', 1)]
    src = lambda n, l: (wb[n] if n in wb else w[n]) if l is None else (wb[n] if n in wb else w[n])[l]
    groups = [order[:5], order[5:7], order[7:12], order[12:]]
    g = dict(zip(groups[0], _all_gather_now([src(n, l) for n, l in groups[0]], "gather_mix")))

    def start_group(keys, tie_to, name):
        srcs = [src(n, l) for n, l in keys]
        k = min(range(len(srcs)), key=lambda i: srcs[i].size)
        zero = (tie_to[(0,) * (tie_to.ndim - 1)][0:1] * 0).astype(srcs[k].dtype)
        srcs[k] = srcs[k] + zero.reshape((1,) * srcs[k].ndim)
        return _send_start(srcs, False, name)

    def finish_group(started, keys, after, name):
        return dict(zip(keys, _send_wait(started, list(range(len(keys))), after, name)))

    x0 = x[0]
    win = g[('a_w_in', 0)].transpose(1, 0, 2).reshape(D, 2 * AW)
    wout = g[('a_w_out', 0)].reshape(AW, D)
    gv = g[('a_norm_v', None)].reshape(1, AW)
    cw = [g[('f_conv_w', l)] for l in range(2)]
    cb = [w['f_conv_b'][l].reshape(N_DEV, 1, FF_BLK) for l in range(2)]
    ws = a_w_s[0].astype(BF)
    wst = jnp.swapaxes(ws, 1, 2)
    bsb = jnp.broadcast_to(a_b_s[0][:, :, None], (GROUPS, CHUNK, CHUNK))
    row = lambda a: a.reshape(1, -1)
    pb = [p[l, 0].astype(BF) for l in range(2)]
    sinks = b_sinks.reshape(N_Q)
    consts = _attn_consts()

    srcs1 = [src(n, l) for n, l in groups[1]]
    srcs1[1] = srcs1[1] + (g[('a_w_out', 0)][0, 0:1, 0:1] * 0).astype(srcs1[1].dtype)
    s1 = _gather2_start(srcs1, "gather_start_ffn0")
    h1, zg0, gp0 = _mixa_fwd(x0, after_token(s1["token"], row(norm_mix[0])), win, gv, ws, bsb, wout, tl)
    s1 = _gather2_forward(s1, h1, "gather_forward_ffn0")
    g = dict(zip(groups[1], _gather2_wait(s1, h1, "gather_wait_ffn0")))
    wup, wdown = [g[('f_w_up', 0)], None], [g[('f_w_down', 0)].reshape(D_FF, D), None]
    s2 = start_group(groups[2], g[('f_w_down', 0)], "gather_start_mid")
    h2, hh0, cc0 = _ffn_fwd(h1, after_token(s2["token"], row(norm_ffn[0])), wup[0], cw[0], cb[0], wdown[0], tm)
    g = finish_group(s2, groups[2], h2, "gather_wait_mid")
    wple = [g[('ple_w_in', 0)].transpose(1, 0, 2).reshape(PLE, D), None]
    wgate = [g[('ple_w_gate', 0)].reshape(D, D), None]
    wkv, wq, wo = g[('w_kv', None)].reshape(D, 2 * N_KV * HD), g[('b_w_q', 0)].reshape(D, D), g[('b_w_o', 0)].reshape(D, D)
    s3 = start_group(groups[3], g[('ple_w_in', 0)], "gather_start_l1")
    h3 = _ple_fwd(h2, after_token(s3["token"], row(norm_ple[0])), wgate[0], row(ple_b_gate[0]), pb[0], wple[0], tl)
    q, kv = _qkv_fwd(h3, row(norm_mix[1]), wq, row(norm_kv), wkv, tl)
    att = _attn_fwd(q, kv, sinks, consts)
    h4 = _oproj_fwd(att, wo, h3, tl)
    g = finish_group(s3, groups[3], h4, "gather_wait_l1")
    wup[1], wdown[1] = g[('f_w_up', 1)], g[('f_w_down', 1)].reshape(D_FF, D)
    wple[1], wgate[1] = g[('ple_w_in', 1)].transpose(1, 0, 2).reshape(PLE, D), g[('ple_w_gate', 1)].reshape(D, D)
    h5, hh1, cc1 = _ffn_fwd(h4, row(norm_ffn[1]), wup[1], cw[1], cb[1], wdown[1], tm)
    h6 = _ple_fwd(h5, row(norm_ple[1]), wgate[1], row(ple_b_gate[1]), pb[1], wple[1], tl)

    def col_blocks(g2d, n):
        return g2d.reshape(g2d.shape[0], N_DEV, n).transpose(1, 0, 2)

    def row_blocks(g2d):
        return g2d.reshape(N_DEV, g2d.shape[0] // N_DEV, g2d.shape[1])

    def ple_grads(l, de, xp, dpre):
        return [col_blocks(_wgrad(pb[l], de, f"wgrad_ple_in{l}", tn=1024), D // N_DEV),
                row_blocks(_wgrad(xp, dpre, f"wgrad_gate{l}", tn=1024))]

    def up_grad(l, xf, dhh):
        return _wgrad(xf, dhh, f"wgrad_up{l}", tn=FF_BLK, out_blocked=True, a_transposed=True, ts=4096)

    def down_grad(l, act, dhb):
        return row_blocks(_wgrad(act, dhb, f"wgrad_down{l}", tn=1024, ts=4096))

    dh6, loss_blk, d_norm_final = _head(h6, row(norm_final), loss_target[0], tl)
    dh5, de1, xp1, dpre1, dbg1, dgp1 = _ple_bwd(dh6, h5, row(norm_ple[1]), wgate[1], row(ple_b_gate[1]), pb[1],
                                                 wple[1], tl)
    ex_ple1 = _send_start(ple_grads(1, de1, xp1, dpre1), True, "send_ple1")
    dh4, xf1, dhh1, act1, dhb5, dcw1, dcb1, dgf1 = _ffn_bwd(dh5, h4, hh1, cc1, after_token(ex_ple1["token"], row(norm_ffn[1])),
                                                            wup[1], cw[1], wdown[1], tm)
    ex_ffn1 = _send_start([up_grad(1, xf1, dhh1), down_grad(1, act1, dhb5)], True, "send_ffn1")
    datt, dhb4 = _oproj_bwd(dh4, wo, tl)
    dq, dkv, dsink = _attn_bwd(q, kv, att, datt, after_token(ex_ffn1["token"], sinks.reshape(1, N_Q)).reshape(N_Q), consts)
    dh3, xq, xkv, dgq, dgkv = _qkv_bwd(dh4, h3, dq, dkv, row(norm_mix[1]), wq, row(norm_kv), wkv, tl)
    ex_att = _send_start([row_blocks(_wgrad(att, dhb4, "wgrad_o", tn=1024)),
                              row_blocks(_wgrad(xq, dq, "wgrad_q", tn=1024)),
                              row_blocks(_wgrad(xkv, dkv, "wgrad_kv", tn=512))], True, "send_att")
    dh2, de0, xp0, dpre0, dbg0, dgp0 = _ple_bwd(dh3, h2, after_token(ex_att["token"], row(norm_ple[0])), wgate[0],
                                                 row(ple_b_gate[0]), pb[0], wple[0], tl)
    ex_ple0 = _send_start(ple_grads(0, de0, xp0, dpre0), True, "send_ple0")
    dh1, xf0, dhh0, act0, dhb2, dcw0, dcb0, dgf0 = _ffn_bwd(dh2, h1, hh0, cc0, after_token(ex_ple0["token"], row(norm_ffn[0])),
                                                            wup[0], cw[0], wdown[0], tm)
    ex_up0 = _send_start([up_grad(0, xf0, dhh0)], True, "send_up0")
    ex_down0 = _send_start([down_grad(0, act0, dhb2)], True, "send_down0")
    dx, xa, dz, gated, dhb1, dws, dbs, dgv, dga = _mixa_bwd(dh1, x0, zg0, gp0, after_token(ex_down0["token"], row(norm_mix[0])), win, gv,
                                                            ws, wst, bsb, wout, tl)
    small_g = {
        'norm_mix': jnp.concatenate([dga, dgq], axis=0), 'norm_ffn': jnp.concatenate([dgf0, dgf1], axis=0),
        'norm_ple': jnp.concatenate([dgp0, dgp1], axis=0), 'norm_kv': dgkv.reshape(D),
        'norm_final': d_norm_final.reshape(D), 'a_w_s': dws[None], 'a_b_s': dbs[None, :, :, 0],
        'b_sinks': dsink[:, 0].reshape(1, N_Q),
        'f_conv_b': jnp.stack([dcb0.reshape(2 * D_FF), dcb1.reshape(2 * D_FF)]),
        'ple_b_gate': jnp.concatenate([dbg0, dbg1], axis=0),
        'a_norm_v': dgv, 'f_conv_w': jnp.stack([dcw0, dcw1], axis=1),
    }
    names_all = _REPL + _SMALL_SHARDED
    full_shapes = [small_g[n].shape for n in names_all]
    small_packed = _pack([small_g[n] for n in names_all], 8)
    ex_small = _send_start([small_packed.reshape(N_DEV, small_packed.shape[0] // N_DEV, ROW)], True, "send_small")
    ex_a_in = _send_start([_wgrad(xa, dz, "wgrad_a_in", tn=2 * AW // N_DEV, out_blocked=True, a_transposed=True,
                                  per_step=4)], True, "send_a_in")
    ex_a_out = _send_start([row_blocks(_wgrad(gated, dhb1, "wgrad_a_out", tn=1024))],
                           True, "send_a_out")

    def as3(a):
        return a.reshape((-1,) + a.shape[-2:])

    def adam(name, parts):
        outs = _adamw_weight(parts, as3(w[name]), as3(mom_m[name]), as3(mom_v[name]), "adamw_" + name)
        return [o.reshape(w[name].shape) for o in outs]

    big = {}
    l_ple1 = _send_wait(ex_ple1, list(range(len(ex_ple1["lands"]))), ex_a_out["token"], "recv_ple1")
    l_ffn1 = _send_wait(ex_ffn1, list(range(len(ex_ffn1["lands"]))), l_ple1[0], "recv_ffn1")
    l_att = _send_wait(ex_att, list(range(len(ex_att["lands"]))), l_ffn1[0], "recv_att")
    big['b_w_o'], big['b_w_q'], big['w_kv'] = adam('b_w_o', [l_att[0]]), adam('b_w_q', [l_att[1]]), adam('w_kv', [l_att[2]])
    l_ple0 = _send_wait(ex_ple0, list(range(len(ex_ple0["lands"]))), big['w_kv'][0], "recv_ple0")
    big['ple_w_in'] = adam('ple_w_in', [l_ple0[0], l_ple1[0]])
    big['ple_w_gate'] = adam('ple_w_gate', [l_ple0[1], l_ple1[1]])
    small_sum = _sum8(_send_wait(ex_small, [0], big['ple_w_gate'][0], "recv_small")[0], "sum_small_grads")
    ex_small_all = _send_start([small_sum], False, "gather_small_sums")
    big['f_w_up'] = adam('f_w_up', [_send_wait(ex_up0, [0], ex_small_all["token"], "recv_up0")[0], l_ffn1[0]])
    big['f_w_down'] = adam('f_w_down', [_send_wait(ex_down0, [0], big['f_w_up'][0], "recv_down0")[0], l_ffn1[1]])
    big['a_w_in'] = adam('a_w_in', _send_wait(ex_a_in, [0], big['f_w_down'][0], "recv_a_in"))
    big['a_w_out'] = adam('a_w_out', _send_wait(ex_a_out, [0], big['a_w_in'][0], "recv_a_out"))
    g_big, d_big, m_big, v_big = [{n: big[n][i] for n in _BIG} for i in range(4)]

    gsum = _send_wait(ex_small_all, [0], big['a_w_out'][0], "recv_small_sums")[0].reshape(small_packed.shape)
    gs = dict(zip(names_all, _unpack(gsum, full_shapes, 8)))
    gs['a_norm_v'] = lax.dynamic_slice_in_dim(gs['a_norm_v'], me * (AW // N_DEV), AW // N_DEV, axis=1)
    gs['f_conv_w'] = lax.dynamic_index_in_dim(gs['f_conv_w'], me, axis=0, keepdims=False)
    gs = {n: gs[n].reshape(w[n].shape) for n in names_all}
    packs = lambda d: _pack([d[n] for n in names_all], 8)
    souts = _adamw_small(packs(gs), packs(w), packs(mom_m), packs(mom_v), "adamw_small")
    shp = [w[n].shape for n in names_all]
    d_s, m_s, v_s = [dict(zip(names_all, _unpack(o, shp, 8))) for o in souts]

    grads = {**g_big, **gs}
    delta = {**d_big, **d_s}
    new_m = {**m_big, **m_s}
    new_v = {**v_big, **v_s}
    loss = lax.psum(loss_blk[0, 0], ("x", "y", "c"))
    return (loss, dx[None], *[grads[n] for n in _PARAMS], *[delta[n] for n in _PARAMS],
            *[new_m[n] for n in _PARAMS], *[new_v[n] for n in _PARAMS])
```

```python
import functools

import numpy as np
import jax
import jax.numpy as jnp
from jax import lax
from jax.experimental import pallas as pl
from jax.experimental.pallas import tpu as pltpu

F32 = jnp.float32
BF = jnp.bfloat16

N_DEV = 8
D = 1024
AW = 1024
GROUPS = 8
CHUNK = 128
HD = 64
N_Q = 16
N_KV = 4
GQA = 4
BLOCK = 128
D_FF = 2816
FF_BLK = 704
N_FF_BLK = 4
PLE = 256
EPS = 1e-6
LR, B1, B2, ADAM_EPS, WD, STEP = 0.001, 0.9, 0.999, 1e-08, 0.01, 10
GELU_C0 = 0.7978845608028654
GELU_C1 = 0.044715
NEG = -1e30
ROW = 1024
VMEM_BIG = 56 * 1024 * 1024

_PARAMS = ['norm_mix', 'norm_ffn', 'norm_ple', 'norm_kv', 'norm_final', 'a_w_in', 'a_norm_v', 'a_w_s', 'a_b_s',
           'a_w_out', 'w_kv', 'b_w_q', 'b_sinks', 'b_w_o', 'f_w_up', 'f_conv_w', 'f_conv_b', 'f_w_down',
           'ple_w_in', 'ple_w_gate', 'ple_b_gate']
_BIG = ['a_w_in', 'a_w_out', 'w_kv', 'b_w_q', 'b_w_o', 'f_w_up', 'f_w_down', 'ple_w_in', 'ple_w_gate']
_SMALL_SHARDED = ['a_norm_v', 'f_conv_w']
_REPL = ['norm_mix', 'norm_ffn', 'norm_ple', 'norm_kv', 'norm_final', 'a_w_s', 'a_b_s', 'b_sinks', 'f_conv_b',
         'ple_b_gate']


def _mm(a, b):
    return jnp.dot(a, b, preferred_element_type=F32)


def _mm_nt(a, b):
    return lax.dot_general(a, b, (((1,), (1,)), ((), ())), preferred_element_type=F32)


def _mm_tn(a, b):
    return lax.dot_general(a, b, (((0,), (0,)), ((), ())), preferred_element_type=F32)


def _const_spec(shape, single=False):
    nd = len(shape)
    if single:
        return pl.BlockSpec(shape, lambda *_: (0,) * nd, pipeline_mode=pl.Buffered(1))
    return pl.BlockSpec(shape, lambda *_: (0,) * nd)


def _row_spec(tm, n):
    return pl.BlockSpec((tm, n), lambda i: (i, 0))


def _rms_fwd(x, g):
    r = lax.rsqrt(jnp.mean(x * x, axis=-1, keepdims=True) + EPS)
    xhat = x * r
    return xhat * g, xhat, r


def _rms_bwd(dy, xhat, r, g):
    a = dy * g
    dx = r * (a - xhat * jnp.mean(a * xhat, axis=-1, keepdims=True))
    return dx, jnp.sum(dy * xhat, axis=0, keepdims=True)


def _sigmoid(x):
    return 1.0 / (1.0 + jnp.exp(-x))


def _params(vmem=None):
    return pltpu.CompilerParams(dimension_semantics=("arbitrary",), vmem_limit_bytes=vmem)


def _tri_masks():
    t = lax.broadcasted_iota(jnp.int32, (CHUNK, CHUNK), 0)
    s = lax.broadcasted_iota(jnp.int32, (CHUNK, CHUNK), 1)
    return t >= s, s >= t


def _mixa_fwd(x, g, win, gv, ws, bsb, wout, tm):
    S = x.shape[0]
    nc = tm // CHUNK

    def body(x_ref, g_ref, win_ref, gv_ref, ws_ref, bsb_ref, wout_ref, h_ref, zg_ref, gp_ref, gated):
        x = x_ref[...]
        xn, _, _ = _rms_fwd(x, g_ref[...])
        z = _mm(xn.astype(BF), win_ref[...])
        z2 = z * z
        t = jnp.tanh(GELU_C0 * (z + GELU_C1 * z2 * z))
        zg = 0.5 * z * (1.0 + t)
        zg_ref[...] = zg.astype(BF)
        gp_ref[...] = (0.5 * (1.0 + t) + 0.5 * z * (1.0 - t * t) * (GELU_C0 * (1.0 + 3.0 * GELU_C1 * z2))).astype(BF)
        u = zg[:, :AW]
        vn, _, _ = _rms_fwd(zg[:, AW:], gv_ref[...])
        vn = vn.astype(BF)
        tril, _ = _tri_masks()
        for h in range(GROUPS):
            wm = jnp.where(tril, ws_ref[h], jnp.zeros((), BF))
            for c in range(nc):
                rs, cs = slice(c * CHUNK, (c + 1) * CHUNK), slice(h * CHUNK, (h + 1) * CHUNK)
                s = _mm(wm, vn[rs, cs]) + bsb_ref[h]
                gated[rs, cs] = (u[rs, cs] * s).astype(BF)
        h_ref[...] = x + _mm(gated[...], wout_ref[...])

    return pl.pallas_call(
        body, name="mixa_fwd", grid=(S // tm,),
        in_specs=[_row_spec(tm, D), _const_spec((1, D)), _const_spec((D, 2 * AW)), _const_spec((1, AW)),
                  _const_spec((GROUPS, CHUNK, CHUNK)), _const_spec((GROUPS, CHUNK, CHUNK)), _const_spec((AW, D))],
        out_specs=[_row_spec(tm, D), _row_spec(tm, 2 * AW), _row_spec(tm, 2 * AW)],
        out_shape=[jax.ShapeDtypeStruct((S, D), F32), jax.ShapeDtypeStruct((S, 2 * AW), BF),
                   jax.ShapeDtypeStruct((S, 2 * AW), BF)],
        scratch_shapes=[pltpu.VMEM((tm, AW), BF)],
        compiler_params=_params(VMEM_BIG),
    )(x, g, win, gv, ws, bsb, wout)


def _mixa_bwd(dh, x, zg, gp, g, win, gv, ws, wst, bsb, wout, tm):
    S = x.shape[0]
    nt = S // tm
    nc = tm // CHUNK

    def body(dh_ref, x_ref, zg_ref, gp_ref, g_ref, win_ref, gv_ref, ws_ref, wst_ref, bsb_ref, wout_ref,
             dx_ref, xn_ref, dz_ref, gated_ref, dhb_ref, dws_ref, dbs_ref, dgv_ref, dg_ref, du_scr, dvn_scr):
        i = pl.program_id(0)

        @pl.when(i == 0)
        def _():
            dws_ref[...] = jnp.zeros_like(dws_ref)
            dbs_ref[...] = jnp.zeros_like(dbs_ref)
            dgv_ref[...] = jnp.zeros_like(dgv_ref)
            dg_ref[...] = jnp.zeros_like(dg_ref)

        x = x_ref[...]
        gmix = g_ref[...]
        xn, xhat, r = _rms_fwd(x, gmix)
        xn_ref[...] = xn.T.astype(BF)
        zg = zg_ref[...].astype(F32)
        gp = gp_ref[...].astype(F32)
        u = zg[:, :AW]
        gvv = gv_ref[...]
        vn, vhat, rv = _rms_fwd(zg[:, AW:], gvv)
        vn = vn.astype(BF)
        dh = dh_ref[...]
        dhb = dh.astype(BF)
        dhb_ref[...] = dhb
        dgated = _mm_nt(dhb, wout_ref[...])
        tril, triu = _tri_masks()
        for h in range(GROUPS):
            wm = jnp.where(tril, ws_ref[h], jnp.zeros((), BF))
            wmt = jnp.where(triu, wst_ref[h], jnp.zeros((), BF))
            for c in range(nc):
                rs, cs = slice(c * CHUNK, (c + 1) * CHUNK), slice(h * CHUNK, (h + 1) * CHUNK)
                vnb = vn[rs, cs]
                s = _mm(wm, vnb) + bsb_ref[h]
                gated_ref[rs, cs] = (u[rs, cs] * s).astype(BF)
                dgt = dgated[rs, cs]
                du_scr[rs, cs] = dgt * s
                ds = dgt * u[rs, cs]
                dsb = ds.astype(BF)
                dws_ref[h] += jnp.where(tril, _mm_nt(dsb, vnb), 0.0)
                dbs_ref[h] += ds
                dvn_scr[rs, cs] = _mm(wmt, dsb)
        dvn = dvn_scr[...]
        dv, dgv = _rms_bwd(dvn, vhat, rv, gvv)
        dgv_ref[...] += dgv
        dz_ref[:, :AW] = (du_scr[...] * gp[:, :AW]).astype(BF)
        dz_ref[:, AW:] = (dv * gp[:, AW:]).astype(BF)
        dxn = _mm_nt(dz_ref[...], win_ref[...])
        dxr, dg = _rms_bwd(dxn, xhat, r, gmix)
        dg_ref[...] += dg
        dx_ref[...] = dh + dxr

        @pl.when(i == nt - 1)
        def _():
            for h in range(GROUPS):
                dbs_ref[h] = jnp.broadcast_to(jnp.sum(dbs_ref[h], axis=-1, keepdims=True), (CHUNK, CHUNK))

    gshape = (GROUPS, CHUNK, CHUNK)
    return pl.pallas_call(
        body, name="mixa_bwd", grid=(nt,),
        in_specs=[_row_spec(tm, D), _row_spec(tm, D), _row_spec(tm, 2 * AW), _row_spec(tm, 2 * AW), _const_spec((1, D)),
                  _const_spec((D, 2 * AW)), _const_spec((1, AW)), _const_spec(gshape), _const_spec(gshape),
                  _const_spec(gshape), _const_spec((AW, D))],
        out_specs=[_row_spec(tm, D), pl.BlockSpec((D, tm), lambda i: (0, i)), _row_spec(tm, 2 * AW), _row_spec(tm, AW),
                   _row_spec(tm, D), _const_spec(gshape), _const_spec(gshape), _const_spec((1, AW)), _const_spec((1, D))],
        out_shape=[jax.ShapeDtypeStruct((S, D), F32), jax.ShapeDtypeStruct((D, S), BF),
                   jax.ShapeDtypeStruct((S, 2 * AW), BF), jax.ShapeDtypeStruct((S, AW), BF),
                   jax.ShapeDtypeStruct((S, D), BF), jax.ShapeDtypeStruct(gshape, F32),
                   jax.ShapeDtypeStruct(gshape, F32), jax.ShapeDtypeStruct((1, AW), F32),
                   jax.ShapeDtypeStruct((1, D), F32)],
        scratch_shapes=[pltpu.VMEM((tm, AW), F32), pltpu.VMEM((tm, AW), F32)],
        compiler_params=_params(VMEM_BIG),
    )(dh, x, zg, gp, g, win, gv, ws, wst, bsb, wout)


def _shift_down(x, prev8, k):
    n, c = x.shape[0] // 8, x.shape[1]
    r = pltpu.roll(x.reshape(n, 8, c), k, axis=1)
    before = jnp.concatenate([pltpu.roll(prev8.reshape(1, 8, c), k, axis=1), r[:-1]], axis=0)
    sub = lax.broadcasted_iota(jnp.int32, (1, 8, c), 1)
    return jnp.where(sub < k, before, r).reshape(x.shape)


def _shift_up(x, next8, k):
    n, c = x.shape[0] // 8, x.shape[1]
    r = pltpu.roll(x.reshape(n, 8, c), 8 - k, axis=1)
    after = jnp.concatenate([r[1:], pltpu.roll(next8.reshape(1, 8, c), 8 - k, axis=1)], axis=0)
    sub = lax.broadcasted_iota(jnp.int32, (1, 8, c), 1)
    return jnp.where(sub >= 8 - k, after, r).reshape(x.shape)


def _ffn_fwd(h, g, wup, cw, cb, wdown, tm):
    S = h.shape[0]

    def body(h_ref, g_ref, wup_ref, cw_ref, cb_ref, wdown_ref, o_ref, hh_ref, cc_ref, sh):
        i = pl.program_id(0)

        @pl.when(i == 0)
        def _():
            sh[...] = jnp.zeros_like(sh)

        x = h_ref[...]
        xn, _, _ = _rms_fwd(x, g_ref[...])
        xn = xn.astype(BF)
        acc = x
        up = lambda d: [_mm(xn, wup_ref[d]), _mm(xn, wup_ref[N_FF_BLK + d])]
        ahead = up(0)
        down = lambda d, act: _mm(act, wdown_ref[d * FF_BLK:(d + 1) * FF_BLK, :])
        behind = None
        for d in range(N_FF_BLK):
            cs = []
            hbs = ahead
            if d + 1 < N_FF_BLK:
                ahead = up(d + 1)
            if behind is not None:
                acc = acc + down(d - 1, behind)
            for blk, hb in zip((d, N_FF_BLK + d), hbs):
                hh_ref[blk] = hb.astype(BF)
                w, prev8 = cw_ref[blk], sh[blk]
                c = cb_ref[blk] + w[0:1] * _shift_down(hb, prev8, 2) + w[1:2] * _shift_down(hb, prev8, 1) + w[2:3] * hb
                sh[blk] = hb[tm - 8:, :]
                cc_ref[blk] = c.astype(BF)
                cs.append(c)
            behind = (cs[0] * _sigmoid(cs[0]) * cs[1]).astype(BF)
        o_ref[...] = acc + down(N_FF_BLK - 1, behind)

    blk3 = pl.BlockSpec((N_DEV, tm, FF_BLK), lambda i: (0, i, 0))
    return pl.pallas_call(
        body, name="ffn_fwd", grid=(S // tm,),
        in_specs=[_row_spec(tm, D), _const_spec((1, D)), _const_spec((N_DEV, D, FF_BLK), True),
                  _const_spec((N_DEV, 3, FF_BLK)), _const_spec((N_DEV, 1, FF_BLK)), _const_spec((D_FF, D), True)],
        out_specs=[_row_spec(tm, D), blk3, blk3],
        out_shape=[jax.ShapeDtypeStruct((S, D), F32), jax.ShapeDtypeStruct((N_DEV, S, FF_BLK), BF),
                   jax.ShapeDtypeStruct((N_DEV, S, FF_BLK), BF)],
        scratch_shapes=[pltpu.VMEM((N_DEV, 8, FF_BLK), F32)],
        compiler_params=_params(VMEM_BIG),
    )(h, g, wup, cw, cb, wdown)


def _ffn_bwd(dh, h, hh, cc, g, wup, cw, wdown, tm):
    S = h.shape[0]
    nt = S // tm

    def rev(i):
        return (nt - 1 - i, 0)

    def body(dh_ref, h_ref, hh_ref, cc_ref, g_ref, wup_ref, cw_ref, wdown_ref,
             dx_ref, xn_ref, dhh_ref, act_ref, dhb_ref, dcw_ref, dcb_ref, dg_ref, sh):
        i = pl.program_id(0)

        @pl.when(i == 0)
        def _():
            sh[...] = jnp.zeros_like(sh)
            dcw_ref[...] = jnp.zeros_like(dcw_ref)
            dcb_ref[...] = jnp.zeros_like(dcb_ref)
            dg_ref[...] = jnp.zeros_like(dg_ref)

        x = h_ref[...]
        gain = g_ref[...]
        xn, xhat, r = _rms_fwd(x, gain)
        xn_ref[...] = xn.T.astype(BF)
        dh = dh_ref[...]
        dhb = dh.astype(BF)
        dhb_ref[...] = dhb
        dxn = jnp.zeros((tm, D), F32)
        dact_of = lambda d: _mm_nt(dhb, wdown_ref[d * FF_BLK:(d + 1) * FF_BLK, :])
        ahead = dact_of(0)
        for d in range(N_FF_BLK):
            blks = (d, N_FF_BLK + d)
            dact = ahead
            if d + 1 < N_FF_BLK:
                ahead = dact_of(d + 1)
            cg, cu = cc_ref[blks[0]].astype(F32), cc_ref[blks[1]].astype(F32)
            sg = _sigmoid(cg)
            sil = cg * sg
            act_ref[d] = (sil * cu).astype(BF)
            dcs = (dact * cu * (sg + sil - sil * sg), dact * sil)
            for blk, dc in zip(blks, dcs):
                next8 = sh[blk]
                dp1, dp2 = _shift_up(dc, next8, 1), _shift_up(dc, next8, 2)
                hb = hh_ref[blk].astype(F32)
                dcb_ref[blk] += jnp.sum(dc, axis=0, keepdims=True)
                dcw_ref[blk, 0:1, :] += jnp.sum(dp2 * hb, axis=0, keepdims=True)
                dcw_ref[blk, 1:2, :] += jnp.sum(dp1 * hb, axis=0, keepdims=True)
                dcw_ref[blk, 2:3, :] += jnp.sum(dc * hb, axis=0, keepdims=True)
                w = cw_ref[blk]
                dhh = (w[2:3] * dc + w[1:2] * dp1 + w[0:1] * dp2).astype(BF)
                sh[blk] = dc[0:8, :]
                dhh_ref[blk] = dhh
                dxn = dxn + _mm_nt(dhh, wup_ref[blk])
        dxr, dg = _rms_bwd(dxn, xhat, r, gain)
        dg_ref[...] += dg
        dx_ref[...] = dh + dxr

    blk3 = lambda i: (0, nt - 1 - i, 0)
    return pl.pallas_call(
        body, name="ffn_bwd", grid=(nt,),
        in_specs=[pl.BlockSpec((tm, D), rev), pl.BlockSpec((tm, D), rev),
                  pl.BlockSpec((N_DEV, tm, FF_BLK), blk3), pl.BlockSpec((N_DEV, tm, FF_BLK), blk3),
                  _const_spec((1, D)), _const_spec((N_DEV, D, FF_BLK), True), _const_spec((N_DEV, 3, FF_BLK)),
                  _const_spec((D_FF, D), True)],
        out_specs=[pl.BlockSpec((tm, D), rev), pl.BlockSpec((D, tm), lambda i: (0, nt - 1 - i)),
                   pl.BlockSpec((N_DEV, tm, FF_BLK), blk3), pl.BlockSpec((N_FF_BLK, tm, FF_BLK), blk3),
                   pl.BlockSpec((tm, D), rev), _const_spec((N_DEV, 3, FF_BLK)), _const_spec((N_DEV, 1, FF_BLK)),
                   _const_spec((1, D))],
        out_shape=[jax.ShapeDtypeStruct((S, D), F32), jax.ShapeDtypeStruct((D, S), BF),
                   jax.ShapeDtypeStruct((N_DEV, S, FF_BLK), BF), jax.ShapeDtypeStruct((N_FF_BLK, S, FF_BLK), BF),
                   jax.ShapeDtypeStruct((S, D), BF), jax.ShapeDtypeStruct((N_DEV, 3, FF_BLK), F32),
                   jax.ShapeDtypeStruct((N_DEV, 1, FF_BLK), F32), jax.ShapeDtypeStruct((1, D), F32)],
        scratch_shapes=[pltpu.VMEM((N_DEV, 8, FF_BLK), F32)],
        compiler_params=_params(VMEM_BIG),
    )(dh, h, hh, cc, g, wup, cw, wdown)


def _ple_fwd(h, g, wgate, bgate, p, wple, tm):
    S = h.shape[0]

    def body(h_ref, g_ref, wg_ref, bg_ref, p_ref, wp_ref, o_ref):
        x = h_ref[...]
        xn, _, _ = _rms_fwd(x, g_ref[...])
        gate = _sigmoid(_mm(xn.astype(BF), wg_ref[...]) + bg_ref[...])
        o_ref[...] = x + _mm(p_ref[...], wp_ref[...]) * gate

    return pl.pallas_call(
        body, name="ple_fwd", grid=(S // tm,),
        in_specs=[_row_spec(tm, D), _const_spec((1, D)), _const_spec((D, D)), _const_spec((1, D)),
                  _row_spec(tm, PLE), _const_spec((PLE, D))],
        out_specs=_row_spec(tm, D), out_shape=jax.ShapeDtypeStruct((S, D), F32),
        compiler_params=_params(),
    )(h, g, wgate, bgate, p, wple)


def _ple_bwd(dh, h, g, wgate, bgate, p, wple, tm):
    S = h.shape[0]

    def body(dh_ref, h_ref, g_ref, wg_ref, bg_ref, p_ref, wp_ref, dx_ref, de_ref, xn_ref, dpre_ref, dbg_ref, dg_ref):
        i = pl.program_id(0)

        @pl.when(i == 0)
        def _():
            dbg_ref[...] = jnp.zeros_like(dbg_ref)
            dg_ref[...] = jnp.zeros_like(dg_ref)

        x = h_ref[...]
        gain = g_ref[...]
        xn, xhat, r = _rms_fwd(x, gain)
        xnb = xn.astype(BF)
        xn_ref[...] = xnb
        gate = _sigmoid(_mm(xnb, wg_ref[...]) + bg_ref[...])
        e = _mm(p_ref[...], wp_ref[...])
        dh = dh_ref[...]
        de_ref[...] = (dh * gate).astype(BF)
        dpre = dh * e * gate * (1.0 - gate)
        dpreb = dpre.astype(BF)
        dpre_ref[...] = dpreb
        dbg_ref[...] += jnp.sum(dpre, axis=0, keepdims=True)
        dxr, dg = _rms_bwd(_mm_nt(dpreb, wg_ref[...]), xhat, r, gain)
        dg_ref[...] += dg
        dx_ref[...] = dh + dxr

    return pl.pallas_call(
        body, name="ple_bwd", grid=(S // tm,),
        in_specs=[_row_spec(tm, D), _row_spec(tm, D), _const_spec((1, D)), _const_spec((D, D)), _const_spec((1, D)),
                  _row_spec(tm, PLE), _const_spec((PLE, D))],
        out_specs=[_row_spec(tm, D), _row_spec(tm, D), _row_spec(tm, D), _row_spec(tm, D), _const_spec((1, D)),
                   _const_spec((1, D))],
        out_shape=[jax.ShapeDtypeStruct((S, D), F32), jax.ShapeDtypeStruct((S, D), BF),
                   jax.ShapeDtypeStruct((S, D), BF), jax.ShapeDtypeStruct((S, D), BF),
                   jax.ShapeDtypeStruct((1, D), F32), jax.ShapeDtypeStruct((1, D), F32)],
        compiler_params=_params(),
    )(dh, h, g, wgate, bgate, p, wple)


def _qkv_fwd(h, gq, wq, gkv, wkv, tm):
    S = h.shape[0]
    nkv = wkv.shape[1]

    def body(h_ref, gq_ref, wq_ref, gkv_ref, wkv_ref, q_ref, kv_ref):
        x = h_ref[...]
        _, xhat, _ = _rms_fwd(x, gq_ref[...])
        q_ref[...] = _mm((xhat * gq_ref[...]).astype(BF), wq_ref[...]).astype(BF)
        kv_ref[...] = _mm((xhat * gkv_ref[...]).astype(BF), wkv_ref[...]).astype(BF)

    return pl.pallas_call(
        body, name="qkv_fwd", grid=(S // tm,),
        in_specs=[_row_spec(tm, D), _const_spec((1, D)), _const_spec((D, D)), _const_spec((1, D)),
                  _const_spec((D, nkv))],
        out_specs=[_row_spec(tm, D), _row_spec(tm, nkv)],
        out_shape=[jax.ShapeDtypeStruct((S, D), BF), jax.ShapeDtypeStruct((S, nkv), BF)],
        compiler_params=_params(),
    )(h, gq, wq, gkv, wkv)


def _qkv_bwd(dh, h, dq, dkv, gq, wq, gkv, wkv, tm):
    S = h.shape[0]
    nkv = wkv.shape[1]

    def body(dh_ref, h_ref, dq_ref, dkv_ref, gq_ref, wq_ref, gkv_ref, wkv_ref,
             dx_ref, xq_ref, xkv_ref, dgq_ref, dgkv_ref):
        i = pl.program_id(0)

        @pl.when(i == 0)
        def _():
            dgq_ref[...] = jnp.zeros_like(dgq_ref)
            dgkv_ref[...] = jnp.zeros_like(dgkv_ref)

        x = h_ref[...]
        gq_, gkv_ = gq_ref[...], gkv_ref[...]
        _, xhat, r = _rms_fwd(x, gq_)
        xq_ref[...] = (xhat * gq_).astype(BF)
        xkv_ref[...] = (xhat * gkv_).astype(BF)
        d1, dg1 = _rms_bwd(_mm_nt(dq_ref[...], wq_ref[...]), xhat, r, gq_)
        d2, dg2 = _rms_bwd(_mm_nt(dkv_ref[...], wkv_ref[...]), xhat, r, gkv_)
        dgq_ref[...] += dg1
        dgkv_ref[...] += dg2
        dx_ref[...] = dh_ref[...] + d1 + d2

    return pl.pallas_call(
        body, name="qkv_bwd", grid=(S // tm,),
        in_specs=[_row_spec(tm, D), _row_spec(tm, D), _row_spec(tm, D), _row_spec(tm, nkv), _const_spec((1, D)),
                  _const_spec((D, D)), _const_spec((1, D)), _const_spec((D, nkv))],
        out_specs=[_row_spec(tm, D), _row_spec(tm, D), _row_spec(tm, D), _const_spec((1, D)), _const_spec((1, D))],
        out_shape=[jax.ShapeDtypeStruct((S, D), F32), jax.ShapeDtypeStruct((S, D), BF),
                   jax.ShapeDtypeStruct((S, D), BF), jax.ShapeDtypeStruct((1, D), F32),
                   jax.ShapeDtypeStruct((1, D), F32)],
        compiler_params=_params(),
    )(dh, h, dq, dkv, gq, wq, gkv, wkv)


def _oproj_fwd(a, w, res, tm):
    S = a.shape[0]

    def body(a_ref, w_ref, r_ref, o_ref):
        o_ref[...] = r_ref[...] + _mm(a_ref[...], w_ref[...])

    return pl.pallas_call(
        body, name="oproj_fwd", grid=(S // tm,),
        in_specs=[_row_spec(tm, D), _const_spec((D, D)), _row_spec(tm, D)],
        out_specs=_row_spec(tm, D), out_shape=jax.ShapeDtypeStruct((S, D), F32),
        compiler_params=_params(),
    )(a, w, res)


def _oproj_bwd(dh, w, tm):
    S = dh.shape[0]

    def body(dh_ref, w_ref, da_ref, dhb_ref):
        dhb = dh_ref[...].astype(BF)
        dhb_ref[...] = dhb
        da_ref[...] = _mm_nt(dhb, w_ref[...]).astype(BF)

    return pl.pallas_call(
        body, name="oproj_bwd", grid=(S // tm,),
        in_specs=[_row_spec(tm, D), _const_spec((D, D))],
        out_specs=[_row_spec(tm, D), _row_spec(tm, D)],
        out_shape=[jax.ShapeDtypeStruct((S, D), BF), jax.ShapeDtypeStruct((S, D), BF)],
        compiler_params=_params(),
    )(dh, w)


def _alibi_slope(hq):
    return float(np.float32(2.0 ** (-8.0 * (hq + 1) / N_Q)))


def _attn_consts():
    W = N_KV * HD
    r = np.zeros((W, N_KV * W), np.float32)
    for kh in range(N_KV):
        for c in range(W):
            r[kh * HD + c % HD, kh * W + c] = 1.0
    qi = np.arange(BLOCK)[:, None]
    kj = np.arange(2 * BLOCK)[None, :]
    dist = qi + BLOCK - kj
    band = (dist >= 0) & (dist < BLOCK)
    bias = np.full((2, N_KV, GQA * BLOCK, 2 * BLOCK), NEG, np.float32)
    for later in (0, 1):
        valid = band & ((kj >= BLOCK) | bool(later))
        for kh in range(N_KV):
            for g in range(GQA):
                slope = np.float32(_alibi_slope(kh * GQA + g))
                bias[later, kh, g * BLOCK:(g + 1) * BLOCK] = np.where(valid, -slope * dist.astype(np.float32), NEG)
    lane_head = np.arange(W)[None, :] // HD
    maskq = (np.arange(GQA * BLOCK)[:, None] // BLOCK == lane_head).astype(np.float32)
    maskk = (np.arange(GQA * 2 * BLOCK)[:, None] // (2 * BLOCK) == lane_head).astype(np.float32)
    head_sum = (lane_head.T == np.arange(GQA * BLOCK)[None, :] // BLOCK).astype(np.float32)
    return dict(rep=jnp.asarray(r, BF), bias=jnp.asarray(bias), maskq=jnp.asarray(maskq, BF),
                maskqs=jnp.asarray(maskq * HD ** -0.5, BF),
                maskk=jnp.asarray(maskk, BF), head_sum=jnp.asarray(head_sum, BF))


def _sink_col(sink_ref, kh):
    rb = lax.broadcasted_iota(jnp.int32, (GQA * BLOCK, 1), 0) >> 7
    col = jnp.full((GQA * BLOCK, 1), sink_ref[kh * GQA + GQA - 1], F32)
    for g in range(GQA - 1):
        col = jnp.where(rb == g, sink_ref[kh * GQA + g], col)
    return col


def _stack4(a):
    return jnp.concatenate([a] * GQA, axis=0)


def _rows_to_lanes(a):
    return jnp.concatenate([a[g * BLOCK:(g + 1) * BLOCK] for g in range(GQA)], axis=1)


def _lanes_to_rows(a, n):
    return jnp.concatenate([a[:, g * n:(g + 1) * n] for g in range(GQA)], axis=0)


def _scores(qg, k4, bias, sink_col, maskqs):
    qs = _stack4(qg) * maskqs
    s = _mm_nt(qs, k4) + bias
    m = jnp.maximum(jnp.max(s, axis=-1, keepdims=True), sink_col)
    return qs, jnp.exp(s - m), jnp.exp(sink_col - m)


def _bands(kvc_ref, rep_ref, band):
    W = N_KV * HD
    kvc = kvc_ref[...]
    band[0, BLOCK:, :] = _mm(kvc[:, :W], rep_ref[...]).astype(BF)
    band[1, BLOCK:, :] = _mm(kvc[:, W:], rep_ref[...]).astype(BF)
    return band[0], band[1]


def _bands_next(band):
    band[:, :BLOCK, :] = band[:, BLOCK:, :]


def _attn_fwd(q, kv, sinks, consts):
    S = q.shape[0]
    nb = S // BLOCK
    W = N_KV * HD

    def body(sink_ref, q_ref, kvc_ref, rep_ref, bias_ref, maskqs_ref, maskk_ref, o_ref, band):
        @pl.when(pl.program_id(0) == 0)
        def _():
            band[...] = jnp.zeros_like(band)

        maskqs, maskk = maskqs_ref[...], maskk_ref[...]
        lane_head = lax.broadcasted_iota(jnp.int32, (BLOCK, W), 1) >> 6
        k4all, v4all = _bands(kvc_ref, rep_ref, band)
        for kh in range(N_KV):
            cols = slice(kh * W, (kh + 1) * W)
            _, e, es = _scores(q_ref[:, cols], k4all[:, cols], bias_ref[kh], _sink_col(sink_ref, kh), maskqs)
            vbd = jnp.concatenate([_stack4(v4all[:, cols]) * maskk, maskk], axis=1)
            nd = _mm(_rows_to_lanes(e.astype(BF)), vbd)
            es_l = jnp.broadcast_to(es[(GQA - 1) * BLOCK:], (BLOCK, W))
            for g in range(GQA - 1):
                es_l = jnp.where(lane_head == g, es[g * BLOCK:(g + 1) * BLOCK], es_l)
            o_ref[:, cols] = (nd[:, :W] / (nd[:, W:] + es_l)).astype(BF)
        _bands_next(band)

    return pl.pallas_call(
        body, name="attn_fwd", grid=(nb,),
        in_specs=[pl.BlockSpec(memory_space=pltpu.SMEM), _row_spec(BLOCK, D), _row_spec(BLOCK, 2 * W),
                  _const_spec((W, N_KV * W)),
                  pl.BlockSpec((None, N_KV, GQA * BLOCK, 2 * BLOCK), lambda i: (jnp.minimum(i, 1), 0, 0, 0)),
                  _const_spec((GQA * BLOCK, W)), _const_spec((GQA * 2 * BLOCK, W))],
        out_specs=_row_spec(BLOCK, D), out_shape=jax.ShapeDtypeStruct((S, D), BF),
        scratch_shapes=[pltpu.VMEM((2, 2 * BLOCK, N_KV * W), BF)],
        compiler_params=_params(),
    )(sinks, q, kv, consts["rep"], consts["bias"], consts["maskqs"], consts["maskk"])


def _attn_bwd(q, kv, att, do, sinks, consts):
    S = q.shape[0]
    nb = S // BLOCK
    W = N_KV * HD

    def qmap(i):
        return (jnp.minimum(i, nb - 1), 0)

    def body(sink_ref, q_ref, kvc_ref, att_ref, do_ref, rep_ref, bias_ref, maskq_ref, maskqs_ref, maskk_ref,
             hsum_ref, dq_ref, dkv_ref, dsink_ref, carry, sink_acc, band):
        i = pl.program_id(0)

        @pl.when(i == 0)
        def _():
            sink_acc[...] = jnp.zeros_like(sink_acc)
            carry[...] = jnp.zeros_like(carry)
            band[...] = jnp.zeros_like(band)

        @pl.when(i < nb)
        def _():
            maskq, maskqs, maskk, head_sum = maskq_ref[...], maskqs_ref[...], maskk_ref[...], hsum_ref[...]
            ones_k = jnp.ones((2 * BLOCK, BLOCK), BF)
            k4all, v4all = _bands(kvc_ref, rep_ref, band)
            dk4s, dv4s = [], []
            for kh in range(N_KV):
                cols = slice(kh * W, (kh + 1) * W)
                k4, v4 = k4all[:, cols], v4all[:, cols]
                qs, e, es = _scores(q_ref[:, cols], k4, bias_ref[kh], _sink_col(sink_ref, kh), maskqs)
                inv = 1.0 / (_mm(e.astype(BF), ones_k) + es)
                p = e * jnp.concatenate([inv, inv], axis=1)
                dog = do_ref[:, cols]
                dos = _stack4(dog) * maskq
                dp = _mm_nt(dos, v4)
                prod = (dog.astype(F32) * att_ref[:, cols].astype(F32)).astype(BF)
                delta = _lanes_to_rows(_mm(prod, head_sum), BLOCK)
                ds = (p * (dp - jnp.concatenate([delta, delta], axis=1))).astype(BF)
                sink_acc[kh] += es * inv * delta
                kbd = _stack4(k4) * maskk
                dq_ref[:, cols] = (_mm(_rows_to_lanes(ds), kbd) * (HD ** -0.5)).astype(BF)
                dk4s.append(_mm_tn(ds, qs).astype(BF))
                dv4s.append(_mm_tn(p.astype(BF), dos).astype(BF))
            dkband = _mm_nt(jnp.concatenate(dk4s, axis=1), rep_ref[...])
            dvband = _mm_nt(jnp.concatenate(dv4s, axis=1), rep_ref[...])
            dkv_ref[:, :W] = (carry[:, :W] + dkband[:BLOCK]).astype(BF)
            dkv_ref[:, W:] = (carry[:, W:] + dvband[:BLOCK]).astype(BF)
            carry[:, :W] = dkband[BLOCK:]
            carry[:, W:] = dvband[BLOCK:]
            _bands_next(band)

        @pl.when(i == nb)
        def _():
            dkv_ref[...] = carry[...].astype(BF)
            for hq in range(N_Q):
                kh, g = divmod(hq, GQA)
                dsink_ref[hq:hq + 1, :] = -jnp.sum(sink_acc[kh, g * BLOCK:(g + 1) * BLOCK, :], axis=0, keepdims=True)

    return pl.pallas_call(
        body, name="attn_bwd", grid=(nb + 1,),
        in_specs=[pl.BlockSpec(memory_space=pltpu.SMEM), pl.BlockSpec((BLOCK, D), qmap),
                  pl.BlockSpec((BLOCK, 2 * W), qmap),
                  pl.BlockSpec((BLOCK, D), qmap), pl.BlockSpec((BLOCK, D), qmap), _const_spec((W, N_KV * W)),
                  pl.BlockSpec((None, N_KV, GQA * BLOCK, 2 * BLOCK), lambda i: (jnp.minimum(i, 1), 0, 0, 0)),
                  _const_spec((GQA * BLOCK, W)), _const_spec((GQA * BLOCK, W)), _const_spec((GQA * 2 * BLOCK, W)),
                  _const_spec((W, GQA * BLOCK))],
        out_specs=[pl.BlockSpec((BLOCK, D), qmap),
                   pl.BlockSpec((BLOCK, 2 * W), lambda i: (jnp.maximum(i - 1, 0), 0)),
                   _const_spec((N_Q, BLOCK))],
        out_shape=[jax.ShapeDtypeStruct((S, D), BF), jax.ShapeDtypeStruct((S, 2 * W), BF),
                   jax.ShapeDtypeStruct((N_Q, BLOCK), F32)],
        scratch_shapes=[pltpu.VMEM((BLOCK, 2 * W), F32), pltpu.VMEM((N_KV, GQA * BLOCK, BLOCK), F32),
                        pltpu.VMEM((2, 2 * BLOCK, N_KV * W), BF)],
        compiler_params=_params(),
    )(sinks, q, kv, att, do, consts["rep"], consts["bias"], consts["maskq"], consts["maskqs"], consts["maskk"], consts["head_sum"])


def _head(h, g, target, tm):
    S = h.shape[0]

    def body(h_ref, g_ref, t_ref, dh_ref, loss_ref, dg_ref):
        i = pl.program_id(0)

        @pl.when(i == 0)
        def _():
            loss_ref[...] = jnp.zeros_like(loss_ref)
            dg_ref[...] = jnp.zeros_like(dg_ref)

        gain = g_ref[...]
        y, xhat, r = _rms_fwd(h_ref[...], gain)
        err = y - t_ref[...]
        tile = 0.5 * jnp.sum(jnp.sum(err * err, axis=-1, keepdims=True) * (1.0 / D), axis=0, keepdims=True)
        loss_ref[...] += jnp.broadcast_to(tile, loss_ref.shape)
        dx, dg = _rms_bwd(err * (1.0 / D), xhat, r, gain)
        dg_ref[...] += dg
        dh_ref[...] = dx

    return pl.pallas_call(
        body, name="loss_head", grid=(S // tm,),
        in_specs=[_row_spec(tm, D), _const_spec((1, D)), _row_spec(tm, D)],
        out_specs=[_row_spec(tm, D), _const_spec((8, 128)), _const_spec((1, D))],
        out_shape=[jax.ShapeDtypeStruct((S, D), F32), jax.ShapeDtypeStruct((8, 128), F32),
                   jax.ShapeDtypeStruct((1, D), F32)],
        compiler_params=_params(),
    )(h, g, target)


def _wgrad(a, b, name, *, tn, out_blocked=False, a_transposed=False, ts=2048, per_step=1):
    a_blocked, b_blocked = a.ndim == 3, b.ndim == 3
    na = a.shape[0] if a_blocked else 1
    S = b.shape[-2]
    K = a.shape[0] if a_transposed else a.shape[-1]
    wide = tn * per_step
    nbk = b.shape[0] if b_blocked else b.shape[1] // wide
    ts = min(ts, S)
    ns = S // ts
    assert not (a_blocked and (out_blocked or a_transposed)) and (per_step == 1 or (out_blocked and not b_blocked))

    def body(a_ref, b_ref, o_ref, acc):
        s = pl.program_id(2)

        @pl.when(s == 0)
        def _():
            acc[...] = jnp.zeros_like(acc)

        if a_transposed:
            acc[...] += _mm(a_ref[...], b_ref[...])
        else:
            acc[...] += _mm_tn(a_ref[...], b_ref[...])

        @pl.when(s == ns - 1)
        def _():
            if per_step == 1:
                o_ref[...] = acc[...].astype(o_ref.dtype)
            else:
                for k in range(per_step):
                    o_ref[k] = acc[:, k * tn:(k + 1) * tn].astype(o_ref.dtype)

    if a_transposed:
        a_spec = pl.BlockSpec((K, ts), lambda i, j, s: (0, s))
    elif a_blocked:
        a_spec = pl.BlockSpec((None, ts, K), lambda i, j, s: (i, s, 0))
    else:
        a_spec = pl.BlockSpec((ts, K), lambda i, j, s: (s, 0))
    b_spec = (pl.BlockSpec((None, ts, tn), lambda i, j, s: (j, s, 0)) if b_blocked
              else pl.BlockSpec((ts, wide), lambda i, j, s: (s, j)))
    if out_blocked:
        o_spec = pl.BlockSpec((None, K, tn) if per_step == 1 else (per_step, K, tn), lambda i, j, s: (j, 0, 0))
        o_shape = jax.ShapeDtypeStruct((nbk * per_step, K, tn), BF)
    else:
        o_spec = pl.BlockSpec((K, wide), lambda i, j, s: (i, j))
        o_shape = jax.ShapeDtypeStruct((na * K, nbk * wide), BF)
    return pl.pallas_call(
        body, name=name, grid=(na, nbk, ns), in_specs=[a_spec, b_spec], out_specs=o_spec, out_shape=o_shape,
        scratch_shapes=[pltpu.VMEM((K, wide), F32)],
        compiler_params=pltpu.CompilerParams(dimension_semantics=("arbitrary", "arbitrary", "arbitrary"),
                                             vmem_limit_bytes=VMEM_BIG),
    )(a, b)


def _adamw_math(w, g, m, v):
    m = B1 * m + (1.0 - B1) * g
    v = B2 * v + (1.0 - B2) * (g * g)
    m_hat = m / (1.0 - B1 ** STEP)
    v_hat = v / (1.0 - B2 ** STEP)
    return -LR * (m_hat / (jnp.sqrt(v_hat) + ADAM_EPS) + WD * w), m, v


def _sum8(parts, name):
    R = parts.shape[1]

    def body(p_ref, o_ref):
        g = p_ref[0]
        for s in range(1, N_DEV):
            g = g + p_ref[s]
        o_ref[...] = g

    return pl.pallas_call(
        body, name=name, grid=(1,),
        in_specs=[pl.BlockSpec((N_DEV, R, ROW), lambda i: (0, 0, 0))],
        out_specs=pl.BlockSpec((R, ROW), lambda i: (0, 0)), out_shape=jax.ShapeDtypeStruct((R, ROW), F32),
        compiler_params=_params(),
    )(parts)


def _adamw_small(g, w, m, v, name):
    R = w.shape[0]

    def body(g_ref, w_ref, m_ref, v_ref, d_ref, nm_ref, nv_ref):
        d_ref[...], nm_ref[...], nv_ref[...] = _adamw_math(w_ref[...], g_ref[...], m_ref[...], v_ref[...])

    spec = pl.BlockSpec((R, ROW), lambda i: (0, 0))
    return pl.pallas_call(
        body, name=name, grid=(1,), in_specs=[spec] * 4, out_specs=[spec] * 3,
        out_shape=[jax.ShapeDtypeStruct((R, ROW), F32)] * 3, compiler_params=_params(),
    )(g, w, m, v)


def _mesh_pos():
    return lax.axis_index("x"), lax.axis_index("y"), lax.axis_index("c")


def _flip(v, bit):
    return 1 - v if bit else v


def _peers(x, y, c):
    out = []
    for k in range(1, N_DEV):
        px, py, pc = _flip(x, k & 4), _flip(y, k & 2), _flip(c, k & 1)
        out.append((k - 1, (px, py, pc), 4 * px + 2 * py + pc))
    return out


def _all_gather_now(xs, name):
    n = len(xs)
    MESH = pl.DeviceIdType.MESH

    def body(*refs):
        x_refs, out_refs = refs[:n], refs[n:2 * n]
        send_sems, recv_sems, local_sems = refs[2 * n:]
        x, y, c = _mesh_pos()
        me, sibling = (x, y, c), (x, y, 1 - c)
        chips = [(1 - x, y), (x, 1 - y), (1 - x, 1 - y)]
        blk = lambda p: 4 * p[0] + 2 * p[1] + p[2]

        def copy(a, k, block, to, src=None):
            return pltpu.make_async_remote_copy(
                src_ref=out_refs[a].at[blk(block)] if src is None else src, dst_ref=out_refs[a].at[blk(block)],
                send_sem=send_sems.at[a, k], recv_sem=recv_sems.at[a, k], device_id=to, device_id_type=MESH)

        sent, own = [], []
        for a in range(n):
            own.append(pltpu.make_async_copy(x_refs[a], out_refs[a].at[blk(me)], local_sems.at[a]))
            own[-1].start()
            first = [copy(a, 1 + j, me, (*chip, c), src=x_refs[a]) for j, chip in enumerate(chips)]
            first.append(copy(a, 0, me, sibling, src=x_refs[a]))
            for cp in first:
                cp.start()
            sent += first
        for a in range(n):
            for j, chip in enumerate(chips):
                copy(a, 1 + j, (*chip, c), me).wait_recv()
                sent.append(copy(a, 4 + j, (*chip, c), sibling))
                sent[-1].start()
        for a in range(n):
            copy(a, 0, sibling, me).wait_recv()
            for j, chip in enumerate(chips):
                copy(a, 4 + j, (*chip, 1 - c), me).wait_recv()
        for cp in sent:
            cp.wait_send()
        for cp in own:
            cp.wait()

    hbm = pl.BlockSpec(memory_space=pl.ANY)
    return pl.pallas_call(
        body, name=name, out_shape=[jax.ShapeDtypeStruct((N_DEV,) + x.shape, x.dtype) for x in xs],
        in_specs=[hbm] * n, out_specs=[hbm] * n,
        scratch_shapes=[pltpu.SemaphoreType.DMA((n, 7)), pltpu.SemaphoreType.DMA((n, 7)), pltpu.SemaphoreType.DMA((n,))],
    )(*xs)


def _chip_peers(x, y, c):
    return (x, y, c), (x, y, 1 - c), [(1 - x, y, c), (x, 1 - y, c), (1 - x, 1 - y, c)]


def _blk(p):
    return 4 * p[0] + 2 * p[1] + p[2]


def _gather2_start(srcs, name):
    n = len(srcs)
    MESH = pl.DeviceIdType.MESH

    def body(*refs):
        src_refs, land_refs = refs[:n], refs[n:2 * n]
        send, recv_ici, recv_sib, local = (refs[(2 + k) * n:(3 + k) * n] for k in range(4))
        token = refs[8 * n]
        me, sibling, chips = _chip_peers(*_mesh_pos())
        for a in range(n):
            dst = land_refs[a].at[_blk(me)]
            pltpu.make_async_remote_copy(src_ref=src_refs[a], dst_ref=dst, send_sem=send[a].at[0], recv_sem=recv_sib[a],
                                         device_id=sibling, device_id_type=MESH).start()
            for j, chip in enumerate(chips):
                pltpu.make_async_remote_copy(src_ref=src_refs[a], dst_ref=dst, send_sem=send[a].at[1 + j],
                                             recv_sem=recv_ici[a].at[j], device_id=chip, device_id_type=MESH).start()
            pltpu.make_async_copy(src_refs[a], dst, local[a]).start()
        token[...] = jnp.zeros_like(token)

    hbm = pl.BlockSpec(memory_space=pltpu.HBM)
    sem = pl.BlockSpec(memory_space=pltpu.SEMAPHORE)
    dma = pltpu.SemaphoreType.DMA
    outs = pl.pallas_call(
        body, name=name,
        out_shape=([dma((4,))] * n + [dma((3,))] * n + [dma(())] * n + [dma(())] * n
                   + [pltpu.HBM(s.shape, s.dtype) for s in srcs]
                   + [pltpu.HBM((N_DEV,) + s.shape, s.dtype) for s in srcs] + [jax.ShapeDtypeStruct((8, 128), F32)]),
        in_specs=[hbm] * (2 * n), out_specs=[sem] * (4 * n) + [hbm] * (2 * n) + [pl.BlockSpec(memory_space=pltpu.VMEM)],
        input_output_aliases={**{a: 4 * n + a for a in range(n)}, **{n + a: 5 * n + a for a in range(n)}},
        compiler_params=pltpu.CompilerParams(has_side_effects=pltpu.SideEffectType.DATAFLOW_SIDE_EFFECTING),
    )(*[pltpu.with_memory_space_constraint(s, pltpu.HBM) for s in srcs],
      *[pltpu.with_memory_space_constraint(lax.empty((N_DEV,) + s.shape, s.dtype), pltpu.HBM) for s in srcs])
    keys = ["send", "recv_ici", "recv_sib", "local", "srcs", "lands"]
    return {**{k: outs[i * n:(i + 1) * n] for i, k in enumerate(keys)}, "token": outs[6 * n]}


def _gather2_forward(started, after, name):
    lands, recv_ici = started["lands"], started["recv_ici"]
    n = len(lands)
    MESH = pl.DeviceIdType.MESH

    def body(*refs):
        land_refs, recv_refs = refs[:n], refs[n:2 * n]
        fsend, frecv = refs[2 * n + 1 + n:2 * n + 1 + 2 * n], refs[2 * n + 1 + 2 * n:2 * n + 1 + 3 * n]
        me, sibling, chips = _chip_peers(*_mesh_pos())
        for a in range(n):
            for j, chip in enumerate(chips):
                rows = land_refs[a].at[_blk(chip)]
                pltpu.make_async_remote_copy(src_ref=rows, dst_ref=rows, send_sem=fsend[a].at[j], recv_sem=recv_refs[a].at[j],
                                             device_id=chip, device_id_type=MESH).wait_recv()
                pltpu.make_async_remote_copy(src_ref=rows, dst_ref=rows, send_sem=fsend[a].at[j], recv_sem=frecv[a].at[j],
                                             device_id=sibling, device_id_type=MESH).start()

    hbm = pl.BlockSpec(memory_space=pltpu.HBM)
    sem = pl.BlockSpec(memory_space=pltpu.SEMAPHORE)
    dma = pltpu.SemaphoreType.DMA
    outs = pl.pallas_call(
        body, name=name,
        out_shape=[pltpu.HBM(l.shape, l.dtype) for l in lands] + [dma((3,))] * (2 * n),
        in_specs=[hbm] * n + [sem] * n + [pl.BlockSpec(memory_space=pl.ANY)], out_specs=[hbm] * n + [sem] * (2 * n),
        input_output_aliases={a: a for a in range(n)},
        compiler_params=pltpu.CompilerParams(has_side_effects=pltpu.SideEffectType.DATAFLOW_SIDE_EFFECTING),
    )(*lands, *recv_ici, after)
    return {**started, "lands": outs[:n], "fsend": outs[n:2 * n], "frecv": outs[2 * n:3 * n]}


def _gather2_wait(st, after, name):
    n = len(st["lands"])
    MESH = pl.DeviceIdType.MESH

    def body(*refs):
        src_refs, land_refs = refs[:n], refs[n:2 * n]
        send, recv_sib, local, fsend, frecv = (refs[(2 + k) * n:(3 + k) * n] for k in range(5))
        me, sibling, chips = _chip_peers(*_mesh_pos())
        for a in range(n):
            def desc(src, dst, s, r):
                return pltpu.make_async_remote_copy(src_ref=src, dst_ref=dst, send_sem=s, recv_sem=r, device_id=sibling,
                                                    device_id_type=MESH)
            own = land_refs[a].at[_blk(me)]
            for k in range(4):
                desc(src_refs[a], own, send[a].at[k], recv_sib[a]).wait_send()
            desc(src_refs[a], land_refs[a].at[_blk(sibling)], send[a].at[0], recv_sib[a]).wait_recv()
            pltpu.make_async_copy(src_refs[a], own, local[a]).wait()
            for j, chip in enumerate(chips):
                mine, theirs = land_refs[a].at[_blk(chip)], land_refs[a].at[_blk((chip[0], chip[1], 1 - chip[2]))]
                desc(mine, mine, fsend[a].at[j], frecv[a].at[j]).wait_send()
                desc(theirs, theirs, fsend[a].at[j], frecv[a].at[j]).wait_recv()

    hbm = pl.BlockSpec(memory_space=pltpu.HBM)
    sem = pl.BlockSpec(memory_space=pltpu.SEMAPHORE)
    srcs, lands = st["srcs"], st["lands"]
    outs = pl.pallas_call(
        body, name=name, out_shape=[pltpu.HBM(s.shape, s.dtype) for s in list(srcs) + list(lands)],
        in_specs=[hbm] * (2 * n) + [sem] * (5 * n) + [pl.BlockSpec(memory_space=pl.ANY)], out_specs=[hbm] * (2 * n),
        input_output_aliases={a: a for a in range(2 * n)},
        compiler_params=pltpu.CompilerParams(has_side_effects=pltpu.SideEffectType.DATAFLOW_SIDE_EFFECTING),
    )(*srcs, *lands, *st["send"], *st["recv_sib"], *st["local"], *st["fsend"], *st["frecv"], after)
    return outs[n:]


def _send_start(srcs, blocked, name):
    n = len(srcs)
    MESH = pl.DeviceIdType.MESH

    def body(*refs):
        src_refs, land_refs = refs[:n], refs[n:2 * n]
        send_sems, recv_sems, local_sems = refs[2 * n:3 * n], refs[3 * n:4 * n], refs[4 * n:5 * n]
        token = refs[7 * n]
        x, y, c = _mesh_pos()
        me = 4 * x + 2 * y + c
        for a in range(n):
            for k, peer, blk in _peers(x, y, c):
                pltpu.make_async_remote_copy(
                    src_ref=src_refs[a].at[blk] if blocked else src_refs[a], dst_ref=land_refs[a].at[me],
                    send_sem=send_sems[a].at[k], recv_sem=recv_sems[a].at[k], device_id=peer,
                    device_id_type=MESH).start()
            pltpu.make_async_copy(src_refs[a].at[me] if blocked else src_refs[a], land_refs[a].at[me],
                                  local_sems[a]).start()
        token[...] = jnp.zeros_like(token)

    hbm = pl.BlockSpec(memory_space=pltpu.HBM)
    sem = pl.BlockSpec(memory_space=pltpu.SEMAPHORE)
    lshape = [s.shape if blocked else (N_DEV,) + s.shape for s in srcs]
    outs = pl.pallas_call(
        body, name=name,
        out_shape=([pltpu.SemaphoreType.DMA((N_DEV - 1,))] * (2 * n) + [pltpu.SemaphoreType.DMA(())] * n
                   + [pltpu.HBM(s.shape, s.dtype) for s in srcs]
                   + [pltpu.HBM(ls, s.dtype) for ls, s in zip(lshape, srcs)] + [jax.ShapeDtypeStruct((8, 128), F32)]),
        in_specs=[hbm] * (2 * n), out_specs=[sem] * (3 * n) + [hbm] * (2 * n) + [pl.BlockSpec(memory_space=pltpu.VMEM)],
        input_output_aliases={**{a: 3 * n + a for a in range(n)}, **{n + a: 4 * n + a for a in range(n)}},
        compiler_params=pltpu.CompilerParams(has_side_effects=pltpu.SideEffectType.DATAFLOW_SIDE_EFFECTING),
    )(*[pltpu.with_memory_space_constraint(s, pltpu.HBM) for s in srcs],
      *[pltpu.with_memory_space_constraint(lax.empty(ls, s.dtype), pltpu.HBM) for ls, s in zip(lshape, srcs)])
    return dict(send=outs[:n], recv=outs[n:2 * n], local=outs[2 * n:3 * n], srcs=outs[3 * n:4 * n],
                lands=outs[4 * n:5 * n], token=outs[5 * n], blocked=blocked)


def _send_wait(started, which, after, name):
    n = len(which)
    blocked = started["blocked"]
    pick = lambda key: [started[key][a] for a in which]
    MESH = pl.DeviceIdType.MESH

    def body(*refs):
        src_refs, land_refs = refs[:n], refs[n:2 * n]
        send, recv, local = refs[2 * n:3 * n], refs[3 * n:4 * n], refs[4 * n:5 * n]
        x, y, c = _mesh_pos()
        me = 4 * x + 2 * y + c
        for a in range(n):
            for k, peer, blk in _peers(x, y, c):
                cp = pltpu.make_async_remote_copy(
                    src_ref=src_refs[a].at[blk] if blocked else src_refs[a], dst_ref=land_refs[a].at[blk],
                    send_sem=send[a].at[k], recv_sem=recv[a].at[k], device_id=peer, device_id_type=MESH)
                cp.wait_send()
                cp.wait_recv()
            pltpu.make_async_copy(src_refs[a].at[me] if blocked else src_refs[a], land_refs[a].at[me],
                                  local[a]).wait()

    hbm = pl.BlockSpec(memory_space=pltpu.HBM)
    sem = pl.BlockSpec(memory_space=pltpu.SEMAPHORE)
    srcs, lands = pick("srcs"), pick("lands")
    outs = pl.pallas_call(
        body, name=name, out_shape=[pltpu.HBM(s.shape, s.dtype) for s in srcs + lands],
        in_specs=[hbm] * (2 * n) + [sem] * (3 * n) + [pl.BlockSpec(memory_space=pl.ANY)], out_specs=[hbm] * (2 * n),
        input_output_aliases={a: a for a in range(2 * n)},
        compiler_params=pltpu.CompilerParams(has_side_effects=pltpu.SideEffectType.DATAFLOW_SIDE_EFFECTING),
    )(*srcs, *lands, *pick("send"), *pick("recv"), *pick("local"), after)
    return outs[n:]


def _adamw_weight(parts, w, m, v, name):
    L, R, C = w.shape
    tr = max(t for t in range(16, min(R, 256) + 1, 16) if R % t == 0)
    nr = R // tr

    def body(*refs):
        p_refs = refs[:L]
        w_ref, m_ref, v_ref, g_ref, d_ref, nm_ref, nv_ref, gsum = refs[L:]
        layer = pl.program_id(0)
        for l in range(L):
            @pl.when(layer == l)
            def _(l=l):
                g = p_refs[l][0].astype(F32)
                for s in range(1, N_DEV):
                    g = g + p_refs[l][s].astype(F32)
                gsum[...] = g
        g = gsum[...]
        g_ref[...] = g
        d_ref[...], nm_ref[...], nv_ref[...] = _adamw_math(w_ref[...], g, m_ref[...], v_ref[...])

    def part_spec(l):
        return pl.BlockSpec((N_DEV, tr, C), lambda layer, i: (0, jnp.where(layer == l, i, jnp.where(layer < l, 0, nr - 1)), 0))

    ws = pl.BlockSpec((None, tr, C), lambda layer, i: (layer, i, 0))
    return pl.pallas_call(
        body, name=name, grid=(L, nr), in_specs=[part_spec(l) for l in range(L)] + [ws] * 3, out_specs=[ws] * 4,
        out_shape=[jax.ShapeDtypeStruct((L, R, C), F32)] * 4, scratch_shapes=[pltpu.VMEM((tr, C), F32)],
        compiler_params=pltpu.CompilerParams(dimension_semantics=("arbitrary", "arbitrary")),
    )(*parts, w, m, v)


def _seg_rows(n, align):
    rows = -(-n // ROW)
    return -(-rows // align) * align


def _pack(arrs, align, lead=()):
    parts = []
    for a in arrs:
        n = int(np.prod(a.shape[len(lead):]))
        rows = _seg_rows(n, align)
        flat = a.reshape(lead + (n,))
        flat = jnp.pad(flat, [(0, 0)] * len(lead) + [(0, rows * ROW - n)])
        parts.append(flat.reshape(lead + (rows, ROW)))
    return jnp.concatenate(parts, axis=len(lead))


def _unpack(buf, shapes, align, lead=()):
    out, r0 = [], 0
    for shp in shapes:
        n = int(np.prod(shp))
        rows = _seg_rows(n, align)
        seg = lax.slice_in_dim(buf, r0, r0 + rows, axis=len(lead))
        out.append(seg.reshape(lead + (rows * ROW,))[..., :n].reshape(lead + tuple(shp)))
        r0 += rows
    return out


def kernel(x, p, norm_mix, norm_ffn, norm_ple, norm_kv, norm_final, a_w_in, a_norm_v, a_w_s, a_b_s, a_w_out, w_kv, b_w_q, b_sinks, b_w_o, f_w_up, f_conv_w, f_conv_b, f_w_down, ple_w_in, ple_w_gate, ple_b_gate, loss_target, m_norm_mix, m_norm_ffn, m_norm_ple, m_norm_kv, m_norm_final, m_a_w_in, m_a_norm_v, m_a_w_s, m_a_b_s, m_a_w_out, m_w_kv, m_b_w_q, m_b_sinks, m_b_w_o, m_f_w_up, m_f_conv_w, m_f_conv_b, m_f_w_down, m_ple_w_in, m_ple_w_gate, m_ple_b_gate, v_norm_mix, v_norm_ffn, v_norm_ple, v_norm_kv, v_norm_final, v_a_w_in, v_a_norm_v, v_a_w_s, v_a_b_s, v_a_w_out, v_w_kv, v_b_w_q, v_b_sinks, v_b_w_o, v_f_w_up, v_f_conv_w, v_f_conv_b, v_f_w_down, v_ple_w_in, v_ple_w_gate, v_ple_b_gate):
    w = dict(norm_mix=norm_mix, norm_ffn=norm_ffn, norm_ple=norm_ple, norm_kv=norm_kv, norm_final=norm_final,
             a_w_in=a_w_in, a_norm_v=a_norm_v, a_w_s=a_w_s, a_b_s=a_b_s, a_w_out=a_w_out, w_kv=w_kv, b_w_q=b_w_q,
             b_sinks=b_sinks, b_w_o=b_w_o, f_w_up=f_w_up, f_conv_w=f_conv_w, f_conv_b=f_conv_b, f_w_down=f_w_down,
             ple_w_in=ple_w_in, ple_w_gate=ple_w_gate, ple_b_gate=ple_b_gate)
    mom_m = dict(norm_mix=m_norm_mix, norm_ffn=m_norm_ffn, norm_ple=m_norm_ple, norm_kv=m_norm_kv,
                 norm_final=m_norm_final, a_w_in=m_a_w_in, a_norm_v=m_a_norm_v, a_w_s=m_a_w_s, a_b_s=m_a_b_s,
                 a_w_out=m_a_w_out, w_kv=m_w_kv, b_w_q=m_b_w_q, b_sinks=m_b_sinks, b_w_o=m_b_w_o, f_w_up=m_f_w_up,
                 f_conv_w=m_f_conv_w, f_conv_b=m_f_conv_b, f_w_down=m_f_w_down, ple_w_in=m_ple_w_in,
                 ple_w_gate=m_ple_w_gate, ple_b_gate=m_ple_b_gate)
    mom_v = dict(norm_mix=v_norm_mix, norm_ffn=v_norm_ffn, norm_ple=v_norm_ple, norm_kv=v_norm_kv,
                 norm_final=v_norm_final, a_w_in=v_a_w_in, a_norm_v=v_a_norm_v, a_w_s=v_a_w_s, a_b_s=v_a_b_s,
                 a_w_out=v_a_w_out, w_kv=v_w_kv, b_w_q=v_b_w_q, b_sinks=v_b_sinks, b_w_o=v_b_w_o, f_w_up=v_f_w_up,
                 f_conv_w=v_f_conv_w, f_conv_b=v_f_conv_b, f_w_down=v_f_w_down, ple_w_in=v_ple_w_in,
                 ple_w_gate=v_ple_w_gate, ple_b_gate=v_ple_b_gate)
    S = x.shape[1]
    tm = min(256, S)
    tl = min(512, S)
    me = 4 * lax.axis_index("x") + 2 * lax.axis_index("y") + lax.axis_index("c")

    def after_token(tok, a):
        return a + tok[0:1, 0:1].reshape((1,) * a.ndim)

    wb = {n: w[n].astype(BF) for n in _BIG}
    order = [('a_w_in', 0), ('a_w_out', 0), ('a_norm_v', None), ('f_conv_w', 0), ('f_conv_w', 1),
             ('f_w_up', 0), ('f_w_down', 0),
             ('ple_w_in', 0), ('ple_w_gate', 0), ('w_kv', None), ('b_w_q', 0), ('b_w_o', 0),
             ('f_w_up', 1), ('f_w_down', 1), ('ple_w_in', 1), ('ple_w_gate', 1)]
    src = lambda n, l: (wb[n] if n in wb else w[n]) if l is None else (wb[n] if n in wb else w[n])[l]
    groups = [order[:5], order[5:7], order[7:12], order[12:]]
    g = dict(zip(groups[0], _all_gather_now([src(n, l) for n, l in groups[0]], "gather_mix")))

    def start_group(keys, tie_to, name):
        srcs = [src(n, l) for n, l in keys]
        k = min(range(len(srcs)), key=lambda i: srcs[i].size)
        zero = (tie_to[(0,) * (tie_to.ndim - 1)][0:1] * 0).astype(srcs[k].dtype)
        srcs[k] = srcs[k] + zero.reshape((1,) * srcs[k].ndim)
        return _send_start(srcs, False, name)

    def finish_group(started, keys, after, name):
        return dict(zip(keys, _send_wait(started, list(range(len(keys))), after, name)))

    x0 = x[0]
    win = g[('a_w_in', 0)].transpose(1, 0, 2).reshape(D, 2 * AW)
    wout = g[('a_w_out', 0)].reshape(AW, D)
    gv = g[('a_norm_v', None)].reshape(1, AW)
    cw = [g[('f_conv_w', l)] for l in range(2)]
    cb = [w['f_conv_b'][l].reshape(N_DEV, 1, FF_BLK) for l in range(2)]
    ws = a_w_s[0].astype(BF)
    wst = jnp.swapaxes(ws, 1, 2)
    bsb = jnp.broadcast_to(a_b_s[0][:, :, None], (GROUPS, CHUNK, CHUNK))
    row = lambda a: a.reshape(1, -1)
    pb = [p[l, 0].astype(BF) for l in range(2)]
    sinks = b_sinks.reshape(N_Q)
    consts = _attn_consts()

    srcs1 = [src(n, l) for n, l in groups[1]]
    srcs1[1] = srcs1[1] + (g[('a_w_out', 0)][0, 0:1, 0:1] * 0).astype(srcs1[1].dtype)
    s1 = _gather2_start(srcs1, "gather_start_ffn0")
    h1, zg0, gp0 = _mixa_fwd(x0, after_token(s1["token"], row(norm_mix[0])), win, gv, ws, bsb, wout, tl)
    s1 = _gather2_forward(s1, h1, "gather_forward_ffn0")
    g = dict(zip(groups[1], _gather2_wait(s1, h1, "gather_wait_ffn0")))
    wup, wdown = [g[('f_w_up', 0)], None], [g[('f_w_down', 0)].reshape(D_FF, D), None]
    s2 = start_group(groups[2], g[('f_w_down', 0)], "gather_start_mid")
    h2, hh0, cc0 = _ffn_fwd(h1, after_token(s2["token"], row(norm_ffn[0])), wup[0], cw[0], cb[0], wdown[0], tm)
    g = finish_group(s2, groups[2], h2, "gather_wait_mid")
    wple = [g[('ple_w_in', 0)].transpose(1, 0, 2).reshape(PLE, D), None]
    wgate = [g[('ple_w_gate', 0)].reshape(D, D), None]
    wkv, wq, wo = g[('w_kv', None)].reshape(D, 2 * N_KV * HD), g[('b_w_q', 0)].reshape(D, D), g[('b_w_o', 0)].reshape(D, D)
    s3 = start_group(groups[3], g[('ple_w_in', 0)], "gather_start_l1")
    h3 = _ple_fwd(h2, after_token(s3["token"], row(norm_ple[0])), wgate[0], row(ple_b_gate[0]), pb[0], wple[0], tl)
    q, kv = _qkv_fwd(h3, row(norm_mix[1]), wq, row(norm_kv), wkv, tl)
    att = _attn_fwd(q, kv, sinks, consts)
    h4 = _oproj_fwd(att, wo, h3, tl)
    g = finish_group(s3, groups[3], h4, "gather_wait_l1")
    wup[1], wdown[1] = g[('f_w_up', 1)], g[('f_w_down', 1)].reshape(D_FF, D)
    wple[1], wgate[1] = g[('ple_w_in', 1)].transpose(1, 0, 2).reshape(PLE, D), g[('ple_w_gate', 1)].reshape(D, D)
    h5, hh1, cc1 = _ffn_fwd(h4, row(norm_ffn[1]), wup[1], cw[1], cb[1], wdown[1], tm)
    h6 = _ple_fwd(h5, row(norm_ple[1]), wgate[1], row(ple_b_gate[1]), pb[1], wple[1], tl)

    def col_blocks(g2d, n):
        return g2d.reshape(g2d.shape[0], N_DEV, n).transpose(1, 0, 2)

    def row_blocks(g2d):
        return g2d.reshape(N_DEV, g2d.shape[0] // N_DEV, g2d.shape[1])

    def ple_grads(l, de, xp, dpre):
        return [col_blocks(_wgrad(pb[l], de, f"wgrad_ple_in{l}", tn=1024), D // N_DEV),
                row_blocks(_wgrad(xp, dpre, f"wgrad_gate{l}", tn=1024))]

    def ffn_grads(l, xf, dhh, act, dhb):
        return [_wgrad(xf, dhh, f"wgrad_up{l}", tn=FF_BLK, out_blocked=True, a_transposed=True, ts=4096),
                row_blocks(_wgrad(act, dhb, f"wgrad_down{l}", tn=1024, ts=4096))]

    dh6, loss_blk, d_norm_final = _head(h6, row(norm_final), loss_target[0], tl)
    dh5, de1, xp1, dpre1, dbg1, dgp1 = _ple_bwd(dh6, h5, row(norm_ple[1]), wgate[1], row(ple_b_gate[1]), pb[1],
                                                 wple[1], tl)
    ex_ple1 = _send_start(ple_grads(1, de1, xp1, dpre1), True, "send_ple1")
    dh4, xf1, dhh1, act1, dhb5, dcw1, dcb1, dgf1 = _ffn_bwd(dh5, h4, hh1, cc1, after_token(ex_ple1["token"], row(norm_ffn[1])),
                                                            wup[1], cw[1], wdown[1], tm)
    ex_ffn1 = _send_start(ffn_grads(1, xf1, dhh1, act1, dhb5), True, "send_ffn1")
    datt, dhb4 = _oproj_bwd(dh4, wo, tl)
    dq, dkv, dsink = _attn_bwd(q, kv, att, datt, after_token(ex_ffn1["token"], sinks.reshape(1, N_Q)).reshape(N_Q), consts)
    dh3, xq, xkv, dgq, dgkv = _qkv_bwd(dh4, h3, dq, dkv, row(norm_mix[1]), wq, row(norm_kv), wkv, tl)
    ex_att = _send_start([row_blocks(_wgrad(att, dhb4, "wgrad_o", tn=1024)),
                              row_blocks(_wgrad(xq, dq, "wgrad_q", tn=1024)),
                              row_blocks(_wgrad(xkv, dkv, "wgrad_kv", tn=512))], True, "send_att")
    dh2, de0, xp0, dpre0, dbg0, dgp0 = _ple_bwd(dh3, h2, after_token(ex_att["token"], row(norm_ple[0])), wgate[0],
                                                 row(ple_b_gate[0]), pb[0], wple[0], tl)
    ex_ple0 = _send_start(ple_grads(0, de0, xp0, dpre0), True, "send_ple0")
    dh1, xf0, dhh0, act0, dhb2, dcw0, dcb0, dgf0 = _ffn_bwd(dh2, h1, hh0, cc0, after_token(ex_ple0["token"], row(norm_ffn[0])),
                                                            wup[0], cw[0], wdown[0], tm)
    ex_ffn0 = _send_start(ffn_grads(0, xf0, dhh0, act0, dhb2), True, "send_ffn0")
    dx, xa, dz, gated, dhb1, dws, dbs, dgv, dga = _mixa_bwd(dh1, x0, zg0, gp0, after_token(ex_ffn0["token"], row(norm_mix[0])), win, gv,
                                                            ws, wst, bsb, wout, tl)
    small_g = {
        'norm_mix': jnp.concatenate([dga, dgq], axis=0), 'norm_ffn': jnp.concatenate([dgf0, dgf1], axis=0),
        'norm_ple': jnp.concatenate([dgp0, dgp1], axis=0), 'norm_kv': dgkv.reshape(D),
        'norm_final': d_norm_final.reshape(D), 'a_w_s': dws[None], 'a_b_s': dbs[None, :, :, 0],
        'b_sinks': dsink[:, 0].reshape(1, N_Q),
        'f_conv_b': jnp.stack([dcb0.reshape(2 * D_FF), dcb1.reshape(2 * D_FF)]),
        'ple_b_gate': jnp.concatenate([dbg0, dbg1], axis=0),
        'a_norm_v': dgv, 'f_conv_w': jnp.stack([dcw0, dcw1], axis=1),
    }
    names_all = _REPL + _SMALL_SHARDED
    full_shapes = [small_g[n].shape for n in names_all]
    small_packed = _pack([small_g[n] for n in names_all], 8)
    ex_small = _send_start([small_packed.reshape(N_DEV, small_packed.shape[0] // N_DEV, ROW)], True, "send_small")
    ex_a_in = _send_start([_wgrad(xa, dz, "wgrad_a_in", tn=2 * AW // N_DEV, out_blocked=True, a_transposed=True,
                                  per_step=4)], True, "send_a_in")
    ex_a_out = _send_start([row_blocks(_wgrad(gated, dhb1, "wgrad_a_out", tn=1024))],
                           True, "send_a_out")

    def as3(a):
        return a.reshape((-1,) + a.shape[-2:])

    def adam(name, parts):
        outs = _adamw_weight(parts, as3(w[name]), as3(mom_m[name]), as3(mom_v[name]), "adamw_" + name)
        return [o.reshape(w[name].shape) for o in outs]

    big = {}
    l_ple1 = _send_wait(ex_ple1, list(range(len(ex_ple1["lands"]))), ex_a_out["token"], "recv_ple1")
    l_ffn1 = _send_wait(ex_ffn1, list(range(len(ex_ffn1["lands"]))), l_ple1[0], "recv_ffn1")
    l_att = _send_wait(ex_att, list(range(len(ex_att["lands"]))), l_ffn1[0], "recv_att")
    big['b_w_o'], big['b_w_q'], big['w_kv'] = adam('b_w_o', [l_att[0]]), adam('b_w_q', [l_att[1]]), adam('w_kv', [l_att[2]])
    l_ple0 = _send_wait(ex_ple0, list(range(len(ex_ple0["lands"]))), big['w_kv'][0], "recv_ple0")
    big['ple_w_in'] = adam('ple_w_in', [l_ple0[0], l_ple1[0]])
    big['ple_w_gate'] = adam('ple_w_gate', [l_ple0[1], l_ple1[1]])
    small_sum = _sum8(_send_wait(ex_small, [0], big['ple_w_gate'][0], "recv_small")[0], "sum_small_grads")
    ex_small_all = _send_start([small_sum], False, "gather_small_sums")
    l_ffn0 = _send_wait(ex_ffn0, list(range(len(ex_ffn0["lands"]))), ex_small_all["token"], "recv_ffn0")
    big['f_w_up'] = adam('f_w_up', [l_ffn0[0], l_ffn1[0]])
    big['f_w_down'] = adam('f_w_down', [l_ffn0[1], l_ffn1[1]])
    big['a_w_in'] = adam('a_w_in', _send_wait(ex_a_in, [0], big['f_w_down'][0], "recv_a_in"))
    big['a_w_out'] = adam('a_w_out', _send_wait(ex_a_out, [0], big['a_w_in'][0], "recv_a_out"))
    g_big, d_big, m_big, v_big = [{n: big[n][i] for n in _BIG} for i in range(4)]

    gsum = _send_wait(ex_small_all, [0], big['a_w_out'][0], "recv_small_sums")[0].reshape(small_packed.shape)
    gs = dict(zip(names_all, _unpack(gsum, full_shapes, 8)))
    gs['a_norm_v'] = lax.dynamic_slice_in_dim(gs['a_norm_v'], me * (AW // N_DEV), AW // N_DEV, axis=1)
    gs['f_conv_w'] = lax.dynamic_index_in_dim(gs['f_conv_w'], me, axis=0, keepdims=False)
    gs = {n: gs[n].reshape(w[n].shape) for n in names_all}
    packs = lambda d: _pack([d[n] for n in names_all], 8)
    souts = _adamw_small(packs(gs), packs(w), packs(mom_m), packs(mom_v), "adamw_small")
    shp = [w[n].shape for n in names_all]
    d_s, m_s, v_s = [dict(zip(names_all, _unpack(o, shp, 8))) for o in souts]

    grads = {**g_big, **gs}
    delta = {**d_big, **d_s}
    new_m = {**m_big, **m_s}
    new_v = {**v_big, **v_s}
    loss = lax.psum(loss_blk[0, 0], ("x", "y", "c"))
    return (loss, dx[None], *[grads[n] for n in _PARAMS], *[delta[n] for n in _PARAMS],
            *[new_m[n] for n in _PARAMS], *[new_v[n] for n in _PARAMS])
```

```python
import functools

import numpy as np
import jax
import jax.numpy as jnp
from jax import lax
from jax.experimental import pallas as pl
from jax.experimental.pallas import tpu as pltpu

F32 = jnp.float32
BF = jnp.bfloat16

N_DEV = 8
D = 1024
AW = 1024
GROUPS = 8
CHUNK = 128
HD = 64
N_Q = 16
N_KV = 4
GQA = 4
BLOCK = 128
D_FF = 2816
FF_BLK = 704
N_FF_BLK = 4
PLE = 256
EPS = 1e-6
LR, B1, B2, ADAM_EPS, WD, STEP = 0.001, 0.9, 0.999, 1e-08, 0.01, 10
GELU_C0 = 0.7978845608028654
GELU_C1 = 0.044715
NEG = -1e30
ROW = 1024
VMEM_BIG = 56 * 1024 * 1024

_PARAMS = ['norm_mix', 'norm_ffn', 'norm_ple', 'norm_kv', 'norm_final', 'a_w_in', 'a_norm_v', 'a_w_s', 'a_b_s',
           'a_w_out', 'w_kv', 'b_w_q', 'b_sinks', 'b_w_o', 'f_w_up', 'f_conv_w', 'f_conv_b', 'f_w_down',
           'ple_w_in', 'ple_w_gate', 'ple_b_gate']
_BIG = ['a_w_in', 'a_w_out', 'w_kv', 'b_w_q', 'b_w_o', 'f_w_up', 'f_w_down', 'ple_w_in', 'ple_w_gate']
_SMALL_SHARDED = ['a_norm_v', 'f_conv_w']
_REPL = ['norm_mix', 'norm_ffn', 'norm_ple', 'norm_kv', 'norm_final', 'a_w_s', 'a_b_s', 'b_sinks', 'f_conv_b',
         'ple_b_gate']


def _mm(a, b):
    return jnp.dot(a, b, preferred_element_type=F32)


def _mm_nt(a, b):
    return lax.dot_general(a, b, (((1,), (1,)), ((), ())), preferred_element_type=F32)


def _mm_tn(a, b):
    return lax.dot_general(a, b, (((0,), (0,)), ((), ())), preferred_element_type=F32)


def _const_spec(shape, single=False):
    nd = len(shape)
    if single:
        return pl.BlockSpec(shape, lambda *_: (0,) * nd, pipeline_mode=pl.Buffered(1))
    return pl.BlockSpec(shape, lambda *_: (0,) * nd)


def _row_spec(tm, n):
    return pl.BlockSpec((tm, n), lambda i: (i, 0))


def _rms_fwd(x, g):
    r = lax.rsqrt(jnp.mean(x * x, axis=-1, keepdims=True) + EPS)
    xhat = x * r
    return xhat * g, xhat, r


def _rms_bwd(dy, xhat, r, g):
    a = dy * g
    dx = r * (a - xhat * jnp.mean(a * xhat, axis=-1, keepdims=True))
    return dx, jnp.sum(dy * xhat, axis=0, keepdims=True)


def _sigmoid(x):
    return 1.0 / (1.0 + jnp.exp(-x))


def _params(vmem=None):
    return pltpu.CompilerParams(dimension_semantics=("arbitrary",), vmem_limit_bytes=vmem)


def _tri_masks():
    t = lax.broadcasted_iota(jnp.int32, (CHUNK, CHUNK), 0)
    s = lax.broadcasted_iota(jnp.int32, (CHUNK, CHUNK), 1)
    return t >= s, s >= t


def _mixa_fwd(x, g, win, gv, ws, bsb, wout, tm):
    S = x.shape[0]
    nc = tm // CHUNK

    def body(x_ref, g_ref, win_ref, gv_ref, ws_ref, bsb_ref, wout_ref, h_ref, zg_ref, gp_ref, gated):
        x = x_ref[...]
        xn, _, _ = _rms_fwd(x, g_ref[...])
        z = _mm(xn.astype(BF), win_ref[...])
        z2 = z * z
        t = jnp.tanh(GELU_C0 * (z + GELU_C1 * z2 * z))
        zg = 0.5 * z * (1.0 + t)
        zg_ref[...] = zg.astype(BF)
        gp_ref[...] = (0.5 * (1.0 + t) + 0.5 * z * (1.0 - t * t) * (GELU_C0 * (1.0 + 3.0 * GELU_C1 * z2))).astype(BF)
        u = zg[:, :AW]
        vn, _, _ = _rms_fwd(zg[:, AW:], gv_ref[...])
        vn = vn.astype(BF)
        tril, _ = _tri_masks()
        for h in range(GROUPS):
            wm = jnp.where(tril, ws_ref[h], jnp.zeros((), BF))
            for c in range(nc):
                rs, cs = slice(c * CHUNK, (c + 1) * CHUNK), slice(h * CHUNK, (h + 1) * CHUNK)
                s = _mm(wm, vn[rs, cs]) + bsb_ref[h]
                gated[rs, cs] = (u[rs, cs] * s).astype(BF)
        h_ref[...] = x + _mm(gated[...], wout_ref[...])

    return pl.pallas_call(
        body, name="mixa_fwd", grid=(S // tm,),
        in_specs=[_row_spec(tm, D), _const_spec((1, D)), _const_spec((D, 2 * AW)), _const_spec((1, AW)),
                  _const_spec((GROUPS, CHUNK, CHUNK)), _const_spec((GROUPS, CHUNK, CHUNK)), _const_spec((AW, D))],
        out_specs=[_row_spec(tm, D), _row_spec(tm, 2 * AW), _row_spec(tm, 2 * AW)],
        out_shape=[jax.ShapeDtypeStruct((S, D), F32), jax.ShapeDtypeStruct((S, 2 * AW), BF),
                   jax.ShapeDtypeStruct((S, 2 * AW), BF)],
        scratch_shapes=[pltpu.VMEM((tm, AW), BF)],
        compiler_params=_params(VMEM_BIG),
    )(x, g, win, gv, ws, bsb, wout)


def _mixa_bwd(dh, x, zg, gp, g, win, gv, ws, wst, bsb, wout, tm):
    S = x.shape[0]
    nt = S // tm
    nc = tm // CHUNK

    def body(dh_ref, x_ref, zg_ref, gp_ref, g_ref, win_ref, gv_ref, ws_ref, wst_ref, bsb_ref, wout_ref,
             dx_ref, xn_ref, dz_ref, gated_ref, dhb_ref, dws_ref, dbs_ref, dgv_ref, dg_ref, du_scr, dvn_scr):
        i = pl.program_id(0)

        @pl.when(i == 0)
        def _():
            dws_ref[...] = jnp.zeros_like(dws_ref)
            dbs_ref[...] = jnp.zeros_like(dbs_ref)
            dgv_ref[...] = jnp.zeros_like(dgv_ref)
            dg_ref[...] = jnp.zeros_like(dg_ref)

        x = x_ref[...]
        gmix = g_ref[...]
        xn, xhat, r = _rms_fwd(x, gmix)
        xn_ref[...] = xn.T.astype(BF)
        zg = zg_ref[...].astype(F32)
        gp = gp_ref[...].astype(F32)
        u = zg[:, :AW]
        gvv = gv_ref[...]
        vn, vhat, rv = _rms_fwd(zg[:, AW:], gvv)
        vn = vn.astype(BF)
        dh = dh_ref[...]
        dhb = dh.astype(BF)
        dhb_ref[...] = dhb
        dgated = _mm_nt(dhb, wout_ref[...])
        tril, triu = _tri_masks()
        for h in range(GROUPS):
            wm = jnp.where(tril, ws_ref[h], jnp.zeros((), BF))
            wmt = jnp.where(triu, wst_ref[h], jnp.zeros((), BF))
            for c in range(nc):
                rs, cs = slice(c * CHUNK, (c + 1) * CHUNK), slice(h * CHUNK, (h + 1) * CHUNK)
                vnb = vn[rs, cs]
                s = _mm(wm, vnb) + bsb_ref[h]
                gated_ref[rs, cs] = (u[rs, cs] * s).astype(BF)
                dgt = dgated[rs, cs]
                du_scr[rs, cs] = dgt * s
                ds = dgt * u[rs, cs]
                dsb = ds.astype(BF)
                dws_ref[h] += jnp.where(tril, _mm_nt(dsb, vnb), 0.0)
                dbs_ref[h] += ds
                dvn_scr[rs, cs] = _mm(wmt, dsb)
        dvn = dvn_scr[...]
        dv, dgv = _rms_bwd(dvn, vhat, rv, gvv)
        dgv_ref[...] += dgv
        dz_ref[:, :AW] = (du_scr[...] * gp[:, :AW]).astype(BF)
        dz_ref[:, AW:] = (dv * gp[:, AW:]).astype(BF)
        dxn = _mm_nt(dz_ref[...], win_ref[...])
        dxr, dg = _rms_bwd(dxn, xhat, r, gmix)
        dg_ref[...] += dg
        dx_ref[...] = dh + dxr

        @pl.when(i == nt - 1)
        def _():
            for h in range(GROUPS):
                dbs_ref[h] = jnp.broadcast_to(jnp.sum(dbs_ref[h], axis=-1, keepdims=True), (CHUNK, CHUNK))

    gshape = (GROUPS, CHUNK, CHUNK)
    return pl.pallas_call(
        body, name="mixa_bwd", grid=(nt,),
        in_specs=[_row_spec(tm, D), _row_spec(tm, D), _row_spec(tm, 2 * AW), _row_spec(tm, 2 * AW), _const_spec((1, D)),
                  _const_spec((D, 2 * AW)), _const_spec((1, AW)), _const_spec(gshape), _const_spec(gshape),
                  _const_spec(gshape), _const_spec((AW, D))],
        out_specs=[_row_spec(tm, D), pl.BlockSpec((D, tm), lambda i: (0, i)), _row_spec(tm, 2 * AW), _row_spec(tm, AW),
                   _row_spec(tm, D), _const_spec(gshape), _const_spec(gshape), _const_spec((1, AW)), _const_spec((1, D))],
        out_shape=[jax.ShapeDtypeStruct((S, D), F32), jax.ShapeDtypeStruct((D, S), BF),
                   jax.ShapeDtypeStruct((S, 2 * AW), BF), jax.ShapeDtypeStruct((S, AW), BF),
                   jax.ShapeDtypeStruct((S, D), BF), jax.ShapeDtypeStruct(gshape, F32),
                   jax.ShapeDtypeStruct(gshape, F32), jax.ShapeDtypeStruct((1, AW), F32),
                   jax.ShapeDtypeStruct((1, D), F32)],
        scratch_shapes=[pltpu.VMEM((tm, AW), F32), pltpu.VMEM((tm, AW), F32)],
        compiler_params=_params(VMEM_BIG),
    )(dh, x, zg, gp, g, win, gv, ws, wst, bsb, wout)


def _shift_down(x, prev8, k):
    n, c = x.shape[0] // 8, x.shape[1]
    r = pltpu.roll(x.reshape(n, 8, c), k, axis=1)
    before = jnp.concatenate([pltpu.roll(prev8.reshape(1, 8, c), k, axis=1), r[:-1]], axis=0)
    sub = lax.broadcasted_iota(jnp.int32, (1, 8, c), 1)
    return jnp.where(sub < k, before, r).reshape(x.shape)


def _shift_up(x, next8, k):
    n, c = x.shape[0] // 8, x.shape[1]
    r = pltpu.roll(x.reshape(n, 8, c), 8 - k, axis=1)
    after = jnp.concatenate([r[1:], pltpu.roll(next8.reshape(1, 8, c), 8 - k, axis=1)], axis=0)
    sub = lax.broadcasted_iota(jnp.int32, (1, 8, c), 1)
    return jnp.where(sub >= 8 - k, after, r).reshape(x.shape)


def _ffn_fwd(h, g, wup, cw, cb, wdown, tm):
    S = h.shape[0]

    def body(h_ref, g_ref, wup_ref, cw_ref, cb_ref, wdown_ref, o_ref, hh_ref, cc_ref, sh):
        i = pl.program_id(0)

        @pl.when(i == 0)
        def _():
            sh[...] = jnp.zeros_like(sh)

        x = h_ref[...]
        xn, _, _ = _rms_fwd(x, g_ref[...])
        xn = xn.astype(BF)
        acc = x
        up = lambda d: [_mm(xn, wup_ref[d]), _mm(xn, wup_ref[N_FF_BLK + d])]
        ahead = up(0)
        down = lambda d, act: _mm(act, wdown_ref[d * FF_BLK:(d + 1) * FF_BLK, :])
        behind = None
        for d in range(N_FF_BLK):
            cs = []
            hbs = ahead
            if d + 1 < N_FF_BLK:
                ahead = up(d + 1)
            if behind is not None:
                acc = acc + down(d - 1, behind)
            for blk, hb in zip((d, N_FF_BLK + d), hbs):
                hh_ref[blk] = hb.astype(BF)
                w, prev8 = cw_ref[blk], sh[blk]
                c = cb_ref[blk] + w[0:1] * _shift_down(hb, prev8, 2) + w[1:2] * _shift_down(hb, prev8, 1) + w[2:3] * hb
                sh[blk] = hb[tm - 8:, :]
                cc_ref[blk] = c.astype(BF)
                cs.append(c)
            behind = (cs[0] * _sigmoid(cs[0]) * cs[1]).astype(BF)
        o_ref[...] = acc + down(N_FF_BLK - 1, behind)

    blk3 = pl.BlockSpec((N_DEV, tm, FF_BLK), lambda i: (0, i, 0))
    return pl.pallas_call(
        body, name="ffn_fwd", grid=(S // tm,),
        in_specs=[_row_spec(tm, D), _const_spec((1, D)), _const_spec((N_DEV, D, FF_BLK), True),
                  _const_spec((N_DEV, 3, FF_BLK)), _const_spec((N_DEV, 1, FF_BLK)), _const_spec((D_FF, D), True)],
        out_specs=[_row_spec(tm, D), blk3, blk3],
        out_shape=[jax.ShapeDtypeStruct((S, D), F32), jax.ShapeDtypeStruct((N_DEV, S, FF_BLK), BF),
                   jax.ShapeDtypeStruct((N_DEV, S, FF_BLK), BF)],
        scratch_shapes=[pltpu.VMEM((N_DEV, 8, FF_BLK), F32)],
        compiler_params=_params(VMEM_BIG),
    )(h, g, wup, cw, cb, wdown)


def _ffn_bwd(dh, h, hh, cc, g, wup, cw, wdown, tm):
    S = h.shape[0]
    nt = S // tm

    def rev(i):
        return (nt - 1 - i, 0)

    def body(dh_ref, h_ref, hh_ref, cc_ref, g_ref, wup_ref, cw_ref, wdown_ref,
             dx_ref, xn_ref, dhh_ref, act_ref, dhb_ref, dcw_ref, dcb_ref, dg_ref, sh):
        i = pl.program_id(0)

        @pl.when(i == 0)
        def _():
            sh[...] = jnp.zeros_like(sh)
            dcw_ref[...] = jnp.zeros_like(dcw_ref)
            dcb_ref[...] = jnp.zeros_like(dcb_ref)
            dg_ref[...] = jnp.zeros_like(dg_ref)

        x = h_ref[...]
        gain = g_ref[...]
        xn, xhat, r = _rms_fwd(x, gain)
        xn_ref[...] = xn.T.astype(BF)
        dh = dh_ref[...]
        dhb = dh.astype(BF)
        dhb_ref[...] = dhb
        dxn = jnp.zeros((tm, D), F32)
        dact_of = lambda d: _mm_nt(dhb, wdown_ref[d * FF_BLK:(d + 1) * FF_BLK, :])
        ahead = dact_of(0)
        for d in range(N_FF_BLK):
            blks = (d, N_FF_BLK + d)
            dact = ahead
            if d + 1 < N_FF_BLK:
                ahead = dact_of(d + 1)
            cg, cu = cc_ref[blks[0]].astype(F32), cc_ref[blks[1]].astype(F32)
            sg = _sigmoid(cg)
            sil = cg * sg
            act_ref[d] = (sil * cu).astype(BF)
            dcs = (dact * cu * (sg + sil - sil * sg), dact * sil)
            for blk, dc in zip(blks, dcs):
                next8 = sh[blk]
                dp1, dp2 = _shift_up(dc, next8, 1), _shift_up(dc, next8, 2)
                hb = hh_ref[blk].astype(F32)
                dcb_ref[blk] += jnp.sum(dc, axis=0, keepdims=True)
                dcw_ref[blk, 0:1, :] += jnp.sum(dp2 * hb, axis=0, keepdims=True)
                dcw_ref[blk, 1:2, :] += jnp.sum(dp1 * hb, axis=0, keepdims=True)
                dcw_ref[blk, 2:3, :] += jnp.sum(dc * hb, axis=0, keepdims=True)
                w = cw_ref[blk]
                dhh = (w[2:3] * dc + w[1:2] * dp1 + w[0:1] * dp2).astype(BF)
                sh[blk] = dc[0:8, :]
                dhh_ref[blk] = dhh
                dxn = dxn + _mm_nt(dhh, wup_ref[blk])
        dxr, dg = _rms_bwd(dxn, xhat, r, gain)
        dg_ref[...] += dg
        dx_ref[...] = dh + dxr

    blk3 = lambda i: (0, nt - 1 - i, 0)
    return pl.pallas_call(
        body, name="ffn_bwd", grid=(nt,),
        in_specs=[pl.BlockSpec((tm, D), rev), pl.BlockSpec((tm, D), rev),
                  pl.BlockSpec((N_DEV, tm, FF_BLK), blk3), pl.BlockSpec((N_DEV, tm, FF_BLK), blk3),
                  _const_spec((1, D)), _const_spec((N_DEV, D, FF_BLK), True), _const_spec((N_DEV, 3, FF_BLK)),
                  _const_spec((D_FF, D), True)],
        out_specs=[pl.BlockSpec((tm, D), rev), pl.BlockSpec((D, tm), lambda i: (0, nt - 1 - i)),
                   pl.BlockSpec((N_DEV, tm, FF_BLK), blk3), pl.BlockSpec((N_FF_BLK, tm, FF_BLK), blk3),
                   pl.BlockSpec((tm, D), rev), _const_spec((N_DEV, 3, FF_BLK)), _const_spec((N_DEV, 1, FF_BLK)),
                   _const_spec((1, D))],
        out_shape=[jax.ShapeDtypeStruct((S, D), F32), jax.ShapeDtypeStruct((D, S), BF),
                   jax.ShapeDtypeStruct((N_DEV, S, FF_BLK), BF), jax.ShapeDtypeStruct((N_FF_BLK, S, FF_BLK), BF),
                   jax.ShapeDtypeStruct((S, D), BF), jax.ShapeDtypeStruct((N_DEV, 3, FF_BLK), F32),
                   jax.ShapeDtypeStruct((N_DEV, 1, FF_BLK), F32), jax.ShapeDtypeStruct((1, D), F32)],
        scratch_shapes=[pltpu.VMEM((N_DEV, 8, FF_BLK), F32)],
        compiler_params=_params(VMEM_BIG),
    )(dh, h, hh, cc, g, wup, cw, wdown)


def _ple_fwd(h, g, wgate, bgate, p, wple, tm):
    S = h.shape[0]

    def body(h_ref, g_ref, wg_ref, bg_ref, p_ref, wp_ref, o_ref):
        x = h_ref[...]
        xn, _, _ = _rms_fwd(x, g_ref[...])
        gate = _sigmoid(_mm(xn.astype(BF), wg_ref[...]) + bg_ref[...])
        o_ref[...] = x + _mm(p_ref[...], wp_ref[...]) * gate

    return pl.pallas_call(
        body, name="ple_fwd", grid=(S // tm,),
        in_specs=[_row_spec(tm, D), _const_spec((1, D)), _const_spec((D, D)), _const_spec((1, D)),
                  _row_spec(tm, PLE), _const_spec((PLE, D))],
        out_specs=_row_spec(tm, D), out_shape=jax.ShapeDtypeStruct((S, D), F32),
        compiler_params=_params(),
    )(h, g, wgate, bgate, p, wple)


def _ple_bwd(dh, h, g, wgate, bgate, p, wple, tm):
    S = h.shape[0]

    def body(dh_ref, h_ref, g_ref, wg_ref, bg_ref, p_ref, wp_ref, dx_ref, de_ref, xn_ref, dpre_ref, dbg_ref, dg_ref):
        i = pl.program_id(0)

        @pl.when(i == 0)
        def _():
            dbg_ref[...] = jnp.zeros_like(dbg_ref)
            dg_ref[...] = jnp.zeros_like(dg_ref)

        x = h_ref[...]
        gain = g_ref[...]
        xn, xhat, r = _rms_fwd(x, gain)
        xnb = xn.astype(BF)
        xn_ref[...] = xnb
        gate = _sigmoid(_mm(xnb, wg_ref[...]) + bg_ref[...])
        e = _mm(p_ref[...], wp_ref[...])
        dh = dh_ref[...]
        de_ref[...] = (dh * gate).astype(BF)
        dpre = dh * e * gate * (1.0 - gate)
        dpreb = dpre.astype(BF)
        dpre_ref[...] = dpreb
        dbg_ref[...] += jnp.sum(dpre, axis=0, keepdims=True)
        dxr, dg = _rms_bwd(_mm_nt(dpreb, wg_ref[...]), xhat, r, gain)
        dg_ref[...] += dg
        dx_ref[...] = dh + dxr

    return pl.pallas_call(
        body, name="ple_bwd", grid=(S // tm,),
        in_specs=[_row_spec(tm, D), _row_spec(tm, D), _const_spec((1, D)), _const_spec((D, D)), _const_spec((1, D)),
                  _row_spec(tm, PLE), _const_spec((PLE, D))],
        out_specs=[_row_spec(tm, D), _row_spec(tm, D), _row_spec(tm, D), _row_spec(tm, D), _const_spec((1, D)),
                   _const_spec((1, D))],
        out_shape=[jax.ShapeDtypeStruct((S, D), F32), jax.ShapeDtypeStruct((S, D), BF),
                   jax.ShapeDtypeStruct((S, D), BF), jax.ShapeDtypeStruct((S, D), BF),
                   jax.ShapeDtypeStruct((1, D), F32), jax.ShapeDtypeStruct((1, D), F32)],
        compiler_params=_params(),
    )(dh, h, g, wgate, bgate, p, wple)


def _qkv_fwd(h, gq, wq, gkv, wkv, tm):
    S = h.shape[0]
    nkv = wkv.shape[1]

    def body(h_ref, gq_ref, wq_ref, gkv_ref, wkv_ref, q_ref, kv_ref):
        x = h_ref[...]
        _, xhat, _ = _rms_fwd(x, gq_ref[...])
        q_ref[...] = _mm((xhat * gq_ref[...]).astype(BF), wq_ref[...]).astype(BF)
        kv_ref[...] = _mm((xhat * gkv_ref[...]).astype(BF), wkv_ref[...]).astype(BF)

    return pl.pallas_call(
        body, name="qkv_fwd", grid=(S // tm,),
        in_specs=[_row_spec(tm, D), _const_spec((1, D)), _const_spec((D, D)), _const_spec((1, D)),
                  _const_spec((D, nkv))],
        out_specs=[_row_spec(tm, D), _row_spec(tm, nkv)],
        out_shape=[jax.ShapeDtypeStruct((S, D), BF), jax.ShapeDtypeStruct((S, nkv), BF)],
        compiler_params=_params(),
    )(h, gq, wq, gkv, wkv)


def _qkv_bwd(dh, h, dq, dkv, gq, wq, gkv, wkv, tm):
    S = h.shape[0]
    nkv = wkv.shape[1]

    def body(dh_ref, h_ref, dq_ref, dkv_ref, gq_ref, wq_ref, gkv_ref, wkv_ref,
             dx_ref, xq_ref, xkv_ref, dgq_ref, dgkv_ref):
        i = pl.program_id(0)

        @pl.when(i == 0)
        def _():
            dgq_ref[...] = jnp.zeros_like(dgq_ref)
            dgkv_ref[...] = jnp.zeros_like(dgkv_ref)

        x = h_ref[...]
        gq_, gkv_ = gq_ref[...], gkv_ref[...]
        _, xhat, r = _rms_fwd(x, gq_)
        xq_ref[...] = (xhat * gq_).astype(BF)
        xkv_ref[...] = (xhat * gkv_).astype(BF)
        d1, dg1 = _rms_bwd(_mm_nt(dq_ref[...], wq_ref[...]), xhat, r, gq_)
        d2, dg2 = _rms_bwd(_mm_nt(dkv_ref[...], wkv_ref[...]), xhat, r, gkv_)
        dgq_ref[...] += dg1
        dgkv_ref[...] += dg2
        dx_ref[...] = dh_ref[...] + d1 + d2

    return pl.pallas_call(
        body, name="qkv_bwd", grid=(S // tm,),
        in_specs=[_row_spec(tm, D), _row_spec(tm, D), _row_spec(tm, D), _row_spec(tm, nkv), _const_spec((1, D)),
                  _const_spec((D, D)), _const_spec((1, D)), _const_spec((D, nkv))],
        out_specs=[_row_spec(tm, D), _row_spec(tm, D), _row_spec(tm, D), _const_spec((1, D)), _const_spec((1, D))],
        out_shape=[jax.ShapeDtypeStruct((S, D), F32), jax.ShapeDtypeStruct((S, D), BF),
                   jax.ShapeDtypeStruct((S, D), BF), jax.ShapeDtypeStruct((1, D), F32),
                   jax.ShapeDtypeStruct((1, D), F32)],
        compiler_params=_params(),
    )(dh, h, dq, dkv, gq, wq, gkv, wkv)


def _oproj_fwd(a, w, res, tm):
    S = a.shape[0]

    def body(a_ref, w_ref, r_ref, o_ref):
        o_ref[...] = r_ref[...] + _mm(a_ref[...], w_ref[...])

    return pl.pallas_call(
        body, name="oproj_fwd", grid=(S // tm,),
        in_specs=[_row_spec(tm, D), _const_spec((D, D)), _row_spec(tm, D)],
        out_specs=_row_spec(tm, D), out_shape=jax.ShapeDtypeStruct((S, D), F32),
        compiler_params=_params(),
    )(a, w, res)


def _oproj_bwd(dh, w, tm):
    S = dh.shape[0]

    def body(dh_ref, w_ref, da_ref, dhb_ref):
        dhb = dh_ref[...].astype(BF)
        dhb_ref[...] = dhb
        da_ref[...] = _mm_nt(dhb, w_ref[...]).astype(BF)

    return pl.pallas_call(
        body, name="oproj_bwd", grid=(S // tm,),
        in_specs=[_row_spec(tm, D), _const_spec((D, D))],
        out_specs=[_row_spec(tm, D), _row_spec(tm, D)],
        out_shape=[jax.ShapeDtypeStruct((S, D), BF), jax.ShapeDtypeStruct((S, D), BF)],
        compiler_params=_params(),
    )(dh, w)


def _alibi_slope(hq):
    return float(np.float32(2.0 ** (-8.0 * (hq + 1) / N_Q)))


def _attn_consts():
    W = N_KV * HD
    r = np.zeros((W, N_KV * W), np.float32)
    for kh in range(N_KV):
        for c in range(W):
            r[kh * HD + c % HD, kh * W + c] = 1.0
    qi = np.arange(BLOCK)[:, None]
    kj = np.arange(2 * BLOCK)[None, :]
    dist = qi + BLOCK - kj
    band = (dist >= 0) & (dist < BLOCK)
    bias = np.full((2, N_KV, GQA * BLOCK, 2 * BLOCK), NEG, np.float32)
    for later in (0, 1):
        valid = band & ((kj >= BLOCK) | bool(later))
        for kh in range(N_KV):
            for g in range(GQA):
                slope = np.float32(_alibi_slope(kh * GQA + g))
                bias[later, kh, g * BLOCK:(g + 1) * BLOCK] = np.where(valid, -slope * dist.astype(np.float32), NEG)
    lane_head = np.arange(W)[None, :] // HD
    maskq = (np.arange(GQA * BLOCK)[:, None] // BLOCK == lane_head).astype(np.float32)
    maskk = (np.arange(GQA * 2 * BLOCK)[:, None] // (2 * BLOCK) == lane_head).astype(np.float32)
    head_sum = (lane_head.T == np.arange(GQA * BLOCK)[None, :] // BLOCK).astype(np.float32)
    return dict(rep=jnp.asarray(r, BF), bias=jnp.asarray(bias), maskq=jnp.asarray(maskq, BF),
                maskqs=jnp.asarray(maskq * HD ** -0.5, BF),
                maskk=jnp.asarray(maskk, BF), head_sum=jnp.asarray(head_sum, BF))


def _sink_col(sink_ref, kh):
    rb = lax.broadcasted_iota(jnp.int32, (GQA * BLOCK, 1), 0) >> 7
    col = jnp.full((GQA * BLOCK, 1), sink_ref[kh * GQA + GQA - 1], F32)
    for g in range(GQA - 1):
        col = jnp.where(rb == g, sink_ref[kh * GQA + g], col)
    return col


def _stack4(a):
    return jnp.concatenate([a] * GQA, axis=0)


def _rows_to_lanes(a):
    return jnp.concatenate([a[g * BLOCK:(g + 1) * BLOCK] for g in range(GQA)], axis=1)


def _lanes_to_rows(a, n):
    return jnp.concatenate([a[:, g * n:(g + 1) * n] for g in range(GQA)], axis=0)


def _scores(qg, k4, bias, sink_col, maskqs):
    qs = _stack4(qg) * maskqs
    s = _mm_nt(qs, k4) + bias
    m = jnp.maximum(jnp.max(s, axis=-1, keepdims=True), sink_col)
    return qs, jnp.exp(s - m), jnp.exp(sink_col - m)


def _bands(kvc_ref, rep_ref, band):
    W = N_KV * HD
    kvc = kvc_ref[...]
    band[0, BLOCK:, :] = _mm(kvc[:, :W], rep_ref[...]).astype(BF)
    band[1, BLOCK:, :] = _mm(kvc[:, W:], rep_ref[...]).astype(BF)
    return band[0], band[1]


def _bands_next(band):
    band[:, :BLOCK, :] = band[:, BLOCK:, :]


def _attn_fwd(q, kv, sinks, consts):
    S = q.shape[0]
    nb = S // BLOCK
    W = N_KV * HD

    def body(sink_ref, q_ref, kvc_ref, rep_ref, bias_ref, maskqs_ref, maskk_ref, o_ref, band):
        @pl.when(pl.program_id(0) == 0)
        def _():
            band[...] = jnp.zeros_like(band)

        maskqs, maskk = maskqs_ref[...], maskk_ref[...]
        lane_head = lax.broadcasted_iota(jnp.int32, (BLOCK, W), 1) >> 6
        k4all, v4all = _bands(kvc_ref, rep_ref, band)
        for kh in range(N_KV):
            cols = slice(kh * W, (kh + 1) * W)
            _, e, es = _scores(q_ref[:, cols], k4all[:, cols], bias_ref[kh], _sink_col(sink_ref, kh), maskqs)
            vbd = jnp.concatenate([_stack4(v4all[:, cols]) * maskk, maskk], axis=1)
            nd = _mm(_rows_to_lanes(e.astype(BF)), vbd)
            es_l = jnp.broadcast_to(es[(GQA - 1) * BLOCK:], (BLOCK, W))
            for g in range(GQA - 1):
                es_l = jnp.where(lane_head == g, es[g * BLOCK:(g + 1) * BLOCK], es_l)
            o_ref[:, cols] = (nd[:, :W] / (nd[:, W:] + es_l)).astype(BF)
        _bands_next(band)

    return pl.pallas_call(
        body, name="attn_fwd", grid=(nb,),
        in_specs=[pl.BlockSpec(memory_space=pltpu.SMEM), _row_spec(BLOCK, D), _row_spec(BLOCK, 2 * W),
                  _const_spec((W, N_KV * W)),
                  pl.BlockSpec((None, N_KV, GQA * BLOCK, 2 * BLOCK), lambda i: (jnp.minimum(i, 1), 0, 0, 0)),
                  _const_spec((GQA * BLOCK, W)), _const_spec((GQA * 2 * BLOCK, W))],
        out_specs=_row_spec(BLOCK, D), out_shape=jax.ShapeDtypeStruct((S, D), BF),
        scratch_shapes=[pltpu.VMEM((2, 2 * BLOCK, N_KV * W), BF)],
        compiler_params=_params(),
    )(sinks, q, kv, consts["rep"], consts["bias"], consts["maskqs"], consts["maskk"])


def _attn_bwd(q, kv, att, do, sinks, consts):
    S = q.shape[0]
    nb = S // BLOCK
    W = N_KV * HD

    def qmap(i):
        return (jnp.minimum(i, nb - 1), 0)

    def body(sink_ref, q_ref, kvc_ref, att_ref, do_ref, rep_ref, bias_ref, maskq_ref, maskqs_ref, maskk_ref,
             hsum_ref, dq_ref, dkv_ref, dsink_ref, carry, sink_acc, band):
        i = pl.program_id(0)

        @pl.when(i == 0)
        def _():
            sink_acc[...] = jnp.zeros_like(sink_acc)
            carry[...] = jnp.zeros_like(carry)
            band[...] = jnp.zeros_like(band)

        @pl.when(i < nb)
        def _():
            maskq, maskqs, maskk, head_sum = maskq_ref[...], maskqs_ref[...], maskk_ref[...], hsum_ref[...]
            ones_k = jnp.ones((2 * BLOCK, BLOCK), BF)
            k4all, v4all = _bands(kvc_ref, rep_ref, band)
            dk4s, dv4s = [], []
            for kh in range(N_KV):
                cols = slice(kh * W, (kh + 1) * W)
                k4, v4 = k4all[:, cols], v4all[:, cols]
                qs, e, es = _scores(q_ref[:, cols], k4, bias_ref[kh], _sink_col(sink_ref, kh), maskqs)
                inv = 1.0 / (_mm(e.astype(BF), ones_k) + es)
                p = e * jnp.concatenate([inv, inv], axis=1)
                dog = do_ref[:, cols]
                dos = _stack4(dog) * maskq
                dp = _mm_nt(dos, v4)
                prod = (dog.astype(F32) * att_ref[:, cols].astype(F32)).astype(BF)
                delta = _lanes_to_rows(_mm(prod, head_sum), BLOCK)
                ds = (p * (dp - jnp.concatenate([delta, delta], axis=1))).astype(BF)
                sink_acc[kh] += es * inv * delta
                kbd = _stack4(k4) * maskk
                dq_ref[:, cols] = (_mm(_rows_to_lanes(ds), kbd) * (HD ** -0.5)).astype(BF)
                dk4s.append(_mm_tn(ds, qs).astype(BF))
                dv4s.append(_mm_tn(p.astype(BF), dos).astype(BF))
            dkband = _mm_nt(jnp.concatenate(dk4s, axis=1), rep_ref[...])
            dvband = _mm_nt(jnp.concatenate(dv4s, axis=1), rep_ref[...])
            dkv_ref[:, :W] = (carry[:, :W] + dkband[:BLOCK]).astype(BF)
            dkv_ref[:, W:] = (carry[:, W:] + dvband[:BLOCK]).astype(BF)
            carry[:, :W] = dkband[BLOCK:]
            carry[:, W:] = dvband[BLOCK:]
            _bands_next(band)

        @pl.when(i == nb)
        def _():
            dkv_ref[...] = carry[...].astype(BF)
            for hq in range(N_Q):
                kh, g = divmod(hq, GQA)
                dsink_ref[hq:hq + 1, :] = -jnp.sum(sink_acc[kh, g * BLOCK:(g + 1) * BLOCK, :], axis=0, keepdims=True)

    return pl.pallas_call(
        body, name="attn_bwd", grid=(nb + 1,),
        in_specs=[pl.BlockSpec(memory_space=pltpu.SMEM), pl.BlockSpec((BLOCK, D), qmap),
                  pl.BlockSpec((BLOCK, 2 * W), qmap),
                  pl.BlockSpec((BLOCK, D), qmap), pl.BlockSpec((BLOCK, D), qmap), _const_spec((W, N_KV * W)),
                  pl.BlockSpec((None, N_KV, GQA * BLOCK, 2 * BLOCK), lambda i: (jnp.minimum(i, 1), 0, 0, 0)),
                  _const_spec((GQA * BLOCK, W)), _const_spec((GQA * BLOCK, W)), _const_spec((GQA * 2 * BLOCK, W)),
                  _const_spec((W, GQA * BLOCK))],
        out_specs=[pl.BlockSpec((BLOCK, D), qmap),
                   pl.BlockSpec((BLOCK, 2 * W), lambda i: (jnp.maximum(i - 1, 0), 0)),
                   _const_spec((N_Q, BLOCK))],
        out_shape=[jax.ShapeDtypeStruct((S, D), BF), jax.ShapeDtypeStruct((S, 2 * W), BF),
                   jax.ShapeDtypeStruct((N_Q, BLOCK), F32)],
        scratch_shapes=[pltpu.VMEM((BLOCK, 2 * W), F32), pltpu.VMEM((N_KV, GQA * BLOCK, BLOCK), F32),
                        pltpu.VMEM((2, 2 * BLOCK, N_KV * W), BF)],
        compiler_params=_params(),
    )(sinks, q, kv, att, do, consts["rep"], consts["bias"], consts["maskq"], consts["maskqs"], consts["maskk"], consts["head_sum"])


def _head(h, g, target, tm):
    S = h.shape[0]

    def body(h_ref, g_ref, t_ref, dh_ref, loss_ref, dg_ref):
        i = pl.program_id(0)

        @pl.when(i == 0)
        def _():
            loss_ref[...] = jnp.zeros_like(loss_ref)
            dg_ref[...] = jnp.zeros_like(dg_ref)

        gain = g_ref[...]
        y, xhat, r = _rms_fwd(h_ref[...], gain)
        err = y - t_ref[...]
        tile = 0.5 * jnp.sum(jnp.sum(err * err, axis=-1, keepdims=True) * (1.0 / D), axis=0, keepdims=True)
        loss_ref[...] += jnp.broadcast_to(tile, loss_ref.shape)
        dx, dg = _rms_bwd(err * (1.0 / D), xhat, r, gain)
        dg_ref[...] += dg
        dh_ref[...] = dx

    return pl.pallas_call(
        body, name="loss_head", grid=(S // tm,),
        in_specs=[_row_spec(tm, D), _const_spec((1, D)), _row_spec(tm, D)],
        out_specs=[_row_spec(tm, D), _const_spec((8, 128)), _const_spec((1, D))],
        out_shape=[jax.ShapeDtypeStruct((S, D), F32), jax.ShapeDtypeStruct((8, 128), F32),
                   jax.ShapeDtypeStruct((1, D), F32)],
        compiler_params=_params(),
    )(h, g, target)


def _wgrad(a, b, name, *, tn, out_blocked=False, a_transposed=False, ts=2048, per_step=1):
    a_blocked, b_blocked = a.ndim == 3, b.ndim == 3
    na = a.shape[0] if a_blocked else 1
    S = b.shape[-2]
    K = a.shape[0] if a_transposed else a.shape[-1]
    wide = tn * per_step
    nbk = b.shape[0] if b_blocked else b.shape[1] // wide
    ts = min(ts, S)
    ns = S // ts
    assert not (a_blocked and (out_blocked or a_transposed)) and (per_step == 1 or (out_blocked and not b_blocked))

    def body(a_ref, b_ref, o_ref, acc):
        s = pl.program_id(2)

        @pl.when(s == 0)
        def _():
            acc[...] = jnp.zeros_like(acc)

        if a_transposed:
            acc[...] += _mm(a_ref[...], b_ref[...])
        else:
            acc[...] += _mm_tn(a_ref[...], b_ref[...])

        @pl.when(s == ns - 1)
        def _():
            if per_step == 1:
                o_ref[...] = acc[...].astype(o_ref.dtype)
            else:
                for k in range(per_step):
                    o_ref[k] = acc[:, k * tn:(k + 1) * tn].astype(o_ref.dtype)

    if a_transposed:
        a_spec = pl.BlockSpec((K, ts), lambda i, j, s: (0, s))
    elif a_blocked:
        a_spec = pl.BlockSpec((None, ts, K), lambda i, j, s: (i, s, 0))
    else:
        a_spec = pl.BlockSpec((ts, K), lambda i, j, s: (s, 0))
    b_spec = (pl.BlockSpec((None, ts, tn), lambda i, j, s: (j, s, 0)) if b_blocked
              else pl.BlockSpec((ts, wide), lambda i, j, s: (s, j)))
    if out_blocked:
        o_spec = pl.BlockSpec((None, K, tn) if per_step == 1 else (per_step, K, tn), lambda i, j, s: (j, 0, 0))
        o_shape = jax.ShapeDtypeStruct((nbk * per_step, K, tn), BF)
    else:
        o_spec = pl.BlockSpec((K, wide), lambda i, j, s: (i, j))
        o_shape = jax.ShapeDtypeStruct((na * K, nbk * wide), BF)
    return pl.pallas_call(
        body, name=name, grid=(na, nbk, ns), in_specs=[a_spec, b_spec], out_specs=o_spec, out_shape=o_shape,
        scratch_shapes=[pltpu.VMEM((K, wide), F32)],
        compiler_params=pltpu.CompilerParams(dimension_semantics=("arbitrary", "arbitrary", "arbitrary"),
                                             vmem_limit_bytes=VMEM_BIG),
    )(a, b)


def _adamw_math(w, g, m, v):
    m = B1 * m + (1.0 - B1) * g
    v = B2 * v + (1.0 - B2) * (g * g)
    m_hat = m / (1.0 - B1 ** STEP)
    v_hat = v / (1.0 - B2 ** STEP)
    return -LR * (m_hat / (jnp.sqrt(v_hat) + ADAM_EPS) + WD * w), m, v


def _sum8(parts, name):
    R = parts.shape[1]

    def body(p_ref, o_ref):
        g = p_ref[0]
        for s in range(1, N_DEV):
            g = g + p_ref[s]
        o_ref[...] = g

    return pl.pallas_call(
        body, name=name, grid=(1,),
        in_specs=[pl.BlockSpec((N_DEV, R, ROW), lambda i: (0, 0, 0))],
        out_specs=pl.BlockSpec((R, ROW), lambda i: (0, 0)), out_shape=jax.ShapeDtypeStruct((R, ROW), F32),
        compiler_params=_params(),
    )(parts)


def _adamw_small(g, w, m, v, name):
    R = w.shape[0]

    def body(g_ref, w_ref, m_ref, v_ref, d_ref, nm_ref, nv_ref):
        d_ref[...], nm_ref[...], nv_ref[...] = _adamw_math(w_ref[...], g_ref[...], m_ref[...], v_ref[...])

    spec = pl.BlockSpec((R, ROW), lambda i: (0, 0))
    return pl.pallas_call(
        body, name=name, grid=(1,), in_specs=[spec] * 4, out_specs=[spec] * 3,
        out_shape=[jax.ShapeDtypeStruct((R, ROW), F32)] * 3, compiler_params=_params(),
    )(g, w, m, v)


def _mesh_pos():
    return lax.axis_index("x"), lax.axis_index("y"), lax.axis_index("c")


def _flip(v, bit):
    return 1 - v if bit else v


def _peers(x, y, c):
    out = []
    for k in range(1, N_DEV):
        px, py, pc = _flip(x, k & 4), _flip(y, k & 2), _flip(c, k & 1)
        out.append((k - 1, (px, py, pc), 4 * px + 2 * py + pc))
    return out


def _all_gather_now(xs, name):
    n = len(xs)
    MESH = pl.DeviceIdType.MESH

    def body(*refs):
        x_refs, out_refs = refs[:n], refs[n:2 * n]
        send_sems, recv_sems, local_sems = refs[2 * n:]
        x, y, c = _mesh_pos()
        me, sibling = (x, y, c), (x, y, 1 - c)
        chips = [(1 - x, y), (x, 1 - y), (1 - x, 1 - y)]
        blk = lambda p: 4 * p[0] + 2 * p[1] + p[2]

        def copy(a, k, block, to, src=None):
            return pltpu.make_async_remote_copy(
                src_ref=out_refs[a].at[blk(block)] if src is None else src, dst_ref=out_refs[a].at[blk(block)],
                send_sem=send_sems.at[a, k], recv_sem=recv_sems.at[a, k], device_id=to, device_id_type=MESH)

        sent, own = [], []
        for a in range(n):
            own.append(pltpu.make_async_copy(x_refs[a], out_refs[a].at[blk(me)], local_sems.at[a]))
            own[-1].start()
            first = [copy(a, 1 + j, me, (*chip, c), src=x_refs[a]) for j, chip in enumerate(chips)]
            first.append(copy(a, 0, me, sibling, src=x_refs[a]))
            for cp in first:
                cp.start()
            sent += first
        for a in range(n):
            for j, chip in enumerate(chips):
                copy(a, 1 + j, (*chip, c), me).wait_recv()
                sent.append(copy(a, 4 + j, (*chip, c), sibling))
                sent[-1].start()
        for a in range(n):
            copy(a, 0, sibling, me).wait_recv()
            for j, chip in enumerate(chips):
                copy(a, 4 + j, (*chip, 1 - c), me).wait_recv()
        for cp in sent:
            cp.wait_send()
        for cp in own:
            cp.wait()

    hbm = pl.BlockSpec(memory_space=pl.ANY)
    return pl.pallas_call(
        body, name=name, out_shape=[jax.ShapeDtypeStruct((N_DEV,) + x.shape, x.dtype) for x in xs],
        in_specs=[hbm] * n, out_specs=[hbm] * n,
        scratch_shapes=[pltpu.SemaphoreType.DMA((n, 7)), pltpu.SemaphoreType.DMA((n, 7)), pltpu.SemaphoreType.DMA((n,))],
    )(*xs)


def _chip_peers(x, y, c):
    return (x, y, c), (x, y, 1 - c), [(1 - x, y, c), (x, 1 - y, c), (1 - x, 1 - y, c)]


def _blk(p):
    return 4 * p[0] + 2 * p[1] + p[2]


def _gather2_start(srcs, name):
    n = len(srcs)
    MESH = pl.DeviceIdType.MESH

    def body(*refs):
        src_refs, land_refs = refs[:n], refs[n:2 * n]
        send, recv_ici, recv_sib, local = (refs[(2 + k) * n:(3 + k) * n] for k in range(4))
        token = refs[8 * n]
        me, sibling, chips = _chip_peers(*_mesh_pos())
        for a in range(n):
            dst = land_refs[a].at[_blk(me)]
            pltpu.make_async_remote_copy(src_ref=src_refs[a], dst_ref=dst, send_sem=send[a].at[0], recv_sem=recv_sib[a],
                                         device_id=sibling, device_id_type=MESH).start()
            for j, chip in enumerate(chips):
                pltpu.make_async_remote_copy(src_ref=src_refs[a], dst_ref=dst, send_sem=send[a].at[1 + j],
                                             recv_sem=recv_ici[a].at[j], device_id=chip, device_id_type=MESH).start()
            pltpu.make_async_copy(src_refs[a], dst, local[a]).start()
        token[...] = jnp.zeros_like(token)

    hbm = pl.BlockSpec(memory_space=pltpu.HBM)
    sem = pl.BlockSpec(memory_space=pltpu.SEMAPHORE)
    dma = pltpu.SemaphoreType.DMA
    outs = pl.pallas_call(
        body, name=name,
        out_shape=([dma((4,))] * n + [dma((3,))] * n + [dma(())] * n + [dma(())] * n
                   + [pltpu.HBM(s.shape, s.dtype) for s in srcs]
                   + [pltpu.HBM((N_DEV,) + s.shape, s.dtype) for s in srcs] + [jax.ShapeDtypeStruct((8, 128), F32)]),
        in_specs=[hbm] * (2 * n), out_specs=[sem] * (4 * n) + [hbm] * (2 * n) + [pl.BlockSpec(memory_space=pltpu.VMEM)],
        input_output_aliases={**{a: 4 * n + a for a in range(n)}, **{n + a: 5 * n + a for a in range(n)}},
        compiler_params=pltpu.CompilerParams(has_side_effects=pltpu.SideEffectType.DATAFLOW_SIDE_EFFECTING),
    )(*[pltpu.with_memory_space_constraint(s, pltpu.HBM) for s in srcs],
      *[pltpu.with_memory_space_constraint(lax.empty((N_DEV,) + s.shape, s.dtype), pltpu.HBM) for s in srcs])
    keys = ["send", "recv_ici", "recv_sib", "local", "srcs", "lands"]
    return {**{k: outs[i * n:(i + 1) * n] for i, k in enumerate(keys)}, "token": outs[6 * n]}


def _gather2_forward(started, after, name):
    lands, recv_ici = started["lands"], started["recv_ici"]
    n = len(lands)
    MESH = pl.DeviceIdType.MESH

    def body(*refs):
        land_refs, recv_refs = refs[:n], refs[n:2 * n]
        fsend, frecv = refs[2 * n + 1 + n:2 * n + 1 + 2 * n], refs[2 * n + 1 + 2 * n:2 * n + 1 + 3 * n]
        me, sibling, chips = _chip_peers(*_mesh_pos())
        for a in range(n):
            for j, chip in enumerate(chips):
                rows = land_refs[a].at[_blk(chip)]
                pltpu.make_async_remote_copy(src_ref=rows, dst_ref=rows, send_sem=fsend[a].at[j], recv_sem=recv_refs[a].at[j],
                                             device_id=chip, device_id_type=MESH).wait_recv()
                pltpu.make_async_remote_copy(src_ref=rows, dst_ref=rows, send_sem=fsend[a].at[j], recv_sem=frecv[a].at[j],
                                             device_id=sibling, device_id_type=MESH).start()

    hbm = pl.BlockSpec(memory_space=pltpu.HBM)
    sem = pl.BlockSpec(memory_space=pltpu.SEMAPHORE)
    dma = pltpu.SemaphoreType.DMA
    outs = pl.pallas_call(
        body, name=name,
        out_shape=[pltpu.HBM(l.shape, l.dtype) for l in lands] + [dma((3,))] * (2 * n),
        in_specs=[hbm] * n + [sem] * n + [pl.BlockSpec(memory_space=pl.ANY)], out_specs=[hbm] * n + [sem] * (2 * n),
        input_output_aliases={a: a for a in range(n)},
        compiler_params=pltpu.CompilerParams(has_side_effects=pltpu.SideEffectType.DATAFLOW_SIDE_EFFECTING),
    )(*lands, *recv_ici, after)
    return {**started, "lands": outs[:n], "fsend": outs[n:2 * n], "frecv": outs[2 * n:3 * n]}


def _gather2_wait(st, after, name):
    n = len(st["lands"])
    MESH = pl.DeviceIdType.MESH

    def body(*refs):
        src_refs, land_refs = refs[:n], refs[n:2 * n]
        send, recv_sib, local, fsend, frecv = (refs[(2 + k) * n:(3 + k) * n] for k in range(5))
        me, sibling, chips = _chip_peers(*_mesh_pos())
        for a in range(n):
            def desc(src, dst, s, r):
                return pltpu.make_async_remote_copy(src_ref=src, dst_ref=dst, send_sem=s, recv_sem=r, device_id=sibling,
                                                    device_id_type=MESH)
            own = land_refs[a].at[_blk(me)]
            for k in range(4):
                desc(src_refs[a], own, send[a].at[k], recv_sib[a]).wait_send()
            desc(src_refs[a], land_refs[a].at[_blk(sibling)], send[a].at[0], recv_sib[a]).wait_recv()
            pltpu.make_async_copy(src_refs[a], own, local[a]).wait()
            for j, chip in enumerate(chips):
                mine, theirs = land_refs[a].at[_blk(chip)], land_refs[a].at[_blk((chip[0], chip[1], 1 - chip[2]))]
                desc(mine, mine, fsend[a].at[j], frecv[a].at[j]).wait_send()
                desc(theirs, theirs, fsend[a].at[j], frecv[a].at[j]).wait_recv()

    hbm = pl.BlockSpec(memory_space=pltpu.HBM)
    sem = pl.BlockSpec(memory_space=pltpu.SEMAPHORE)
    srcs, lands = st["srcs"], st["lands"]
    outs = pl.pallas_call(
        body, name=name, out_shape=[pltpu.HBM(s.shape, s.dtype) for s in list(srcs) + list(lands)],
        in_specs=[hbm] * (2 * n) + [sem] * (5 * n) + [pl.BlockSpec(memory_space=pl.ANY)], out_specs=[hbm] * (2 * n),
        input_output_aliases={a: a for a in range(2 * n)},
        compiler_params=pltpu.CompilerParams(has_side_effects=pltpu.SideEffectType.DATAFLOW_SIDE_EFFECTING),
    )(*srcs, *lands, *st["send"], *st["recv_sib"], *st["local"], *st["fsend"], *st["frecv"], after)
    return outs[n:]


def _send_start(srcs, blocked, name):
    n = len(srcs)
    MESH = pl.DeviceIdType.MESH

    def body(*refs):
        src_refs, land_refs = refs[:n], refs[n:2 * n]
        send_sems, recv_sems, local_sems = refs[2 * n:3 * n], refs[3 * n:4 * n], refs[4 * n:5 * n]
        token = refs[7 * n]
        x, y, c = _mesh_pos()
        me = 4 * x + 2 * y + c
        for a in range(n):
            for k, peer, blk in _peers(x, y, c):
                pltpu.make_async_remote_copy(
                    src_ref=src_refs[a].at[blk] if blocked else src_refs[a], dst_ref=land_refs[a].at[me],
                    send_sem=send_sems[a].at[k], recv_sem=recv_sems[a].at[k], device_id=peer,
                    device_id_type=MESH).start()
            pltpu.make_async_copy(src_refs[a].at[me] if blocked else src_refs[a], land_refs[a].at[me],
                                  local_sems[a]).start()
        token[...] = jnp.zeros_like(token)

    hbm = pl.BlockSpec(memory_space=pltpu.HBM)
    sem = pl.BlockSpec(memory_space=pltpu.SEMAPHORE)
    lshape = [s.shape if blocked else (N_DEV,) + s.shape for s in srcs]
    outs = pl.pallas_call(
        body, name=name,
        out_shape=([pltpu.SemaphoreType.DMA((N_DEV - 1,))] * (2 * n) + [pltpu.SemaphoreType.DMA(())] * n
                   + [pltpu.HBM(s.shape, s.dtype) for s in srcs]
                   + [pltpu.HBM(ls, s.dtype) for ls, s in zip(lshape, srcs)] + [jax.ShapeDtypeStruct((8, 128), F32)]),
        in_specs=[hbm] * (2 * n), out_specs=[sem] * (3 * n) + [hbm] * (2 * n) + [pl.BlockSpec(memory_space=pltpu.VMEM)],
        input_output_aliases={**{a: 3 * n + a for a in range(n)}, **{n + a: 4 * n + a for a in range(n)}},
        compiler_params=pltpu.CompilerParams(has_side_effects=pltpu.SideEffectType.DATAFLOW_SIDE_EFFECTING),
    )(*[pltpu.with_memory_space_constraint(s, pltpu.HBM) for s in srcs],
      *[pltpu.with_memory_space_constraint(lax.empty(ls, s.dtype), pltpu.HBM) for ls, s in zip(lshape, srcs)])
    return dict(send=outs[:n], recv=outs[n:2 * n], local=outs[2 * n:3 * n], srcs=outs[3 * n:4 * n],
                lands=outs[4 * n:5 * n], token=outs[5 * n], blocked=blocked)


def _send_wait(started, which, after, name):
    n = len(which)
    blocked = started["blocked"]
    pick = lambda key: [started[key][a] for a in which]
    MESH = pl.DeviceIdType.MESH

    def body(*refs):
        src_refs, land_refs = refs[:n], refs[n:2 * n]
        send, recv, local = refs[2 * n:3 * n], refs[3 * n:4 * n], refs[4 * n:5 * n]
        x, y, c = _mesh_pos()
        me = 4 * x + 2 * y + c
        for a in range(n):
            for k, peer, blk in _peers(x, y, c):
                cp = pltpu.make_async_remote_copy(
                    src_ref=src_refs[a].at[blk] if blocked else src_refs[a], dst_ref=land_refs[a].at[blk],
                    send_sem=send[a].at[k], recv_sem=recv[a].at[k], device_id=peer, device_id_type=MESH)
                cp.wait_send()
                cp.wait_recv()
            pltpu.make_async_copy(src_refs[a].at[me] if blocked else src_refs[a], land_refs[a].at[me],
                                  local[a]).wait()

    hbm = pl.BlockSpec(memory_space=pltpu.HBM)
    sem = pl.BlockSpec(memory_space=pltpu.SEMAPHORE)
    srcs, lands = pick("srcs"), pick("lands")
    outs = pl.pallas_call(
        body, name=name, out_shape=[pltpu.HBM(s.shape, s.dtype) for s in srcs + lands],
        in_specs=[hbm] * (2 * n) + [sem] * (3 * n) + [pl.BlockSpec(memory_space=pl.ANY)], out_specs=[hbm] * (2 * n),
        input_output_aliases={a: a for a in range(2 * n)},
        compiler_params=pltpu.CompilerParams(has_side_effects=pltpu.SideEffectType.DATAFLOW_SIDE_EFFECTING),
    )(*srcs, *lands, *pick("send"), *pick("recv"), *pick("local"), after)
    return outs[n:]


def _adamw_weight(parts, w, m, v, name):
    L, R, C = w.shape
    tr = max(t for t in range(16, min(R, 256) + 1, 16) if R % t == 0)
    nr = R // tr

    def body(*refs):
        p_refs = refs[:L]
        w_ref, m_ref, v_ref, g_ref, d_ref, nm_ref, nv_ref, gsum = refs[L:]
        layer = pl.program_id(0)
        for l in range(L):
            @pl.when(layer == l)
            def _(l=l):
                g = p_refs[l][0].astype(F32)
                for s in range(1, N_DEV):
                    g = g + p_refs[l][s].astype(F32)
                gsum[...] = g
        g = gsum[...]
        g_ref[...] = g
        d_ref[...], nm_ref[...], nv_ref[...] = _adamw_math(w_ref[...], g, m_ref[...], v_ref[...])

    def part_spec(l):
        return pl.BlockSpec((N_DEV, tr, C), lambda layer, i: (0, jnp.where(layer == l, i, jnp.where(layer < l, 0, nr - 1)), 0))

    ws = pl.BlockSpec((None, tr, C), lambda layer, i: (layer, i, 0))
    return pl.pallas_call(
        body, name=name, grid=(L, nr), in_specs=[part_spec(l) for l in range(L)] + [ws] * 3, out_specs=[ws] * 4,
        out_shape=[jax.ShapeDtypeStruct((L, R, C), F32)] * 4, scratch_shapes=[pltpu.VMEM((tr, C), F32)],
        compiler_params=pltpu.CompilerParams(dimension_semantics=("arbitrary", "arbitrary")),
    )(*parts, w, m, v)


def _seg_rows(n, align):
    rows = -(-n // ROW)
    return -(-rows // align) * align


def _pack(arrs, align, lead=()):
    parts = []
    for a in arrs:
        n = int(np.prod(a.shape[len(lead):]))
        rows = _seg_rows(n, align)
        flat = a.reshape(lead + (n,))
        flat = jnp.pad(flat, [(0, 0)] * len(lead) + [(0, rows * ROW - n)])
        parts.append(flat.reshape(lead + (rows, ROW)))
    return jnp.concatenate(parts, axis=len(lead))


def _unpack(buf, shapes, align, lead=()):
    out, r0 = [], 0
    for shp in shapes:
        n = int(np.prod(shp))
        rows = _seg_rows(n, align)
        seg = lax.slice_in_dim(buf, r0, r0 + rows, axis=len(lead))
        out.append(seg.reshape(lead + (rows * ROW,))[..., :n].reshape(lead + tuple(shp)))
        r0 += rows
    return out


def kernel(x, p, norm_mix, norm_ffn, norm_ple, norm_kv, norm_final, a_w_in, a_norm_v, a_w_s, a_b_s, a_w_out, w_kv, b_w_q, b_sinks, b_w_o, f_w_up, f_conv_w, f_conv_b, f_w_down, ple_w_in, ple_w_gate, ple_b_gate, loss_target, m_norm_mix, m_norm_ffn, m_norm_ple, m_norm_kv, m_norm_final, m_a_w_in, m_a_norm_v, m_a_w_s, m_a_b_s, m_a_w_out, m_w_kv, m_b_w_q, m_b_sinks, m_b_w_o, m_f_w_up, m_f_conv_w, m_f_conv_b, m_f_w_down, m_ple_w_in, m_ple_w_gate, m_ple_b_gate, v_norm_mix, v_norm_ffn, v_norm_ple, v_norm_kv, v_norm_final, v_a_w_in, v_a_norm_v, v_a_w_s, v_a_b_s, v_a_w_out, v_w_kv, v_b_w_q, v_b_sinks, v_b_w_o, v_f_w_up, v_f_conv_w, v_f_conv_b, v_f_w_down, v_ple_w_in, v_ple_w_gate, v_ple_b_gate):
    w = dict(norm_mix=norm_mix, norm_ffn=norm_ffn, norm_ple=norm_ple, norm_kv=norm_kv, norm_final=norm_final,
             a_w_in=a_w_in, a_norm_v=a_norm_v, a_w_s=a_w_s, a_b_s=a_b_s, a_w_out=a_w_out, w_kv=w_kv, b_w_q=b_w_q,
             b_sinks=b_sinks, b_w_o=b_w_o, f_w_up=f_w_up, f_conv_w=f_conv_w, f_conv_b=f_conv_b, f_w_down=f_w_down,
             ple_w_in=ple_w_in, ple_w_gate=ple_w_gate, ple_b_gate=ple_b_gate)
    mom_m = dict(norm_mix=m_norm_mix, norm_ffn=m_norm_ffn, norm_ple=m_norm_ple, norm_kv=m_norm_kv,
                 norm_final=m_norm_final, a_w_in=m_a_w_in, a_norm_v=m_a_norm_v, a_w_s=m_a_w_s, a_b_s=m_a_b_s,
                 a_w_out=m_a_w_out, w_kv=m_w_kv, b_w_q=m_b_w_q, b_sinks=m_b_sinks, b_w_o=m_b_w_o, f_w_up=m_f_w_up,
                 f_conv_w=m_f_conv_w, f_conv_b=m_f_conv_b, f_w_down=m_f_w_down, ple_w_in=m_ple_w_in,
                 ple_w_gate=m_ple_w_gate, ple_b_gate=m_ple_b_gate)
    mom_v = dict(norm_mix=v_norm_mix, norm_ffn=v_norm_ffn, norm_ple=v_norm_ple, norm_kv=v_norm_kv,
                 norm_final=v_norm_final, a_w_in=v_a_w_in, a_norm_v=v_a_norm_v, a_w_s=v_a_w_s, a_b_s=v_a_b_s,
                 a_w_out=v_a_w_out, w_kv=v_w_kv, b_w_q=v_b_w_q, b_sinks=v_b_sinks, b_w_o=v_b_w_o, f_w_up=v_f_w_up,
                 f_conv_w=v_f_conv_w, f_conv_b=v_f_conv_b, f_w_down=v_f_w_down, ple_w_in=v_ple_w_in,
                 ple_w_gate=v_ple_w_gate, ple_b_gate=v_ple_b_gate)
    S = x.shape[1]
    tm = min(256, S)
    tl = min(512, S)
    me = 4 * lax.axis_index("x") + 2 * lax.axis_index("y") + lax.axis_index("c")

    def after_token(tok, a):
        return a + tok[0:1, 0:1].reshape((1,) * a.ndim)

    wb = {n: w[n].astype(BF) for n in _BIG}
    order = [('a_w_in', 0), ('a_w_out', 0), ('a_norm_v', None), ('f_conv_w', 0), ('f_conv_w', 1),
             ('f_w_up', 0), ('f_w_down', 0),
             ('ple_w_in', 0), ('ple_w_gate', 0), ('w_kv', None), ('b_w_q', 0), ('b_w_o', 0),
             ('f_w_up', 1), ('f_w_down', 1), ('ple_w_in', 1), ('ple_w_gate', 1)]
    src = lambda n, l: (wb[n] if n in wb else w[n]) if l is None else (wb[n] if n in wb else w[n])[l]
    groups = [order[:5], order[5:7], order[7:12], order[12:]]
    g = dict(zip(groups[0], _all_gather_now([src(n, l) for n, l in groups[0]], "gather_mix")))

    def start_group(keys, tie_to, name):
        srcs = [src(n, l) for n, l in keys]
        k = min(range(len(srcs)), key=lambda i: srcs[i].size)
        zero = (tie_to[(0,) * (tie_to.ndim - 1)][0:1] * 0).astype(srcs[k].dtype)
        srcs[k] = srcs[k] + zero.reshape((1,) * srcs[k].ndim)
        return _send_start(srcs, False, name)

    def finish_group(started, keys, after, name):
        return dict(zip(keys, _send_wait(started, list(range(len(keys))), after, name)))

    x0 = x[0]
    win = g[('a_w_in', 0)].transpose(1, 0, 2).reshape(D, 2 * AW)
    wout = g[('a_w_out', 0)].reshape(AW, D)
    gv = g[('a_norm_v', None)].reshape(1, AW)
    cw = [g[('f_conv_w', l)] for l in range(2)]
    cb = [w['f_conv_b'][l].reshape(N_DEV, 1, FF_BLK) for l in range(2)]
    ws = a_w_s[0].astype(BF)
    wst = jnp.swapaxes(ws, 1, 2)
    bsb = jnp.broadcast_to(a_b_s[0][:, :, None], (GROUPS, CHUNK, CHUNK))
    row = lambda a: a.reshape(1, -1)
    pb = [p[l, 0].astype(BF) for l in range(2)]
    sinks = b_sinks.reshape(N_Q)
    consts = _attn_consts()

    srcs1 = [src(n, l) for n, l in groups[1]]
    srcs1[1] = srcs1[1] + (g[('a_w_out', 0)][0, 0:1, 0:1] * 0).astype(srcs1[1].dtype)
    s1 = _gather2_start(srcs1, "gather_start_ffn0")
    h1, zg0, gp0 = _mixa_fwd(x0, after_token(s1["token"], row(norm_mix[0])), win, gv, ws, bsb, wout, tl)
    s1 = _gather2_forward(s1, h1, "gather_forward_ffn0")
    g = dict(zip(groups[1], _gather2_wait(s1, h1, "gather_wait_ffn0")))
    wup, wdown = [g[('f_w_up', 0)], None], [g[('f_w_down', 0)].reshape(D_FF, D), None]
    s2 = start_group(groups[2], g[('f_w_down', 0)], "gather_start_mid")
    h2, hh0, cc0 = _ffn_fwd(h1, after_token(s2["token"], row(norm_ffn[0])), wup[0], cw[0], cb[0], wdown[0], tm)
    g = finish_group(s2, groups[2], h2, "gather_wait_mid")
    wple = [g[('ple_w_in', 0)].transpose(1, 0, 2).reshape(PLE, D), None]
    wgate = [g[('ple_w_gate', 0)].reshape(D, D), None]
    wkv, wq, wo = g[('w_kv', None)].reshape(D, 2 * N_KV * HD), g[('b_w_q', 0)].reshape(D, D), g[('b_w_o', 0)].reshape(D, D)
    s3 = start_group(groups[3], g[('ple_w_in', 0)], "gather_start_l1")
    h3 = _ple_fwd(h2, after_token(s3["token"], row(norm_ple[0])), wgate[0], row(ple_b_gate[0]), pb[0], wple[0], tl)
    q, kv = _qkv_fwd(h3, row(norm_mix[1]), wq, row(norm_kv), wkv, tl)
    att = _attn_fwd(q, kv, sinks, consts)
    h4 = _oproj_fwd(att, wo, h3, tl)
    g = finish_group(s3, groups[3], h4, "gather_wait_l1")
    wup[1], wdown[1] = g[('f_w_up', 1)], g[('f_w_down', 1)].reshape(D_FF, D)
    wple[1], wgate[1] = g[('ple_w_in', 1)].transpose(1, 0, 2).reshape(PLE, D), g[('ple_w_gate', 1)].reshape(D, D)
    h5, hh1, cc1 = _ffn_fwd(h4, row(norm_ffn[1]), wup[1], cw[1], cb[1], wdown[1], tm)
    h6 = _ple_fwd(h5, row(norm_ple[1]), wgate[1], row(ple_b_gate[1]), pb[1], wple[1], tl)

    def col_blocks(g2d, n):
        return g2d.reshape(g2d.shape[0], N_DEV, n).transpose(1, 0, 2)

    def row_blocks(g2d):
        return g2d.reshape(N_DEV, g2d.shape[0] // N_DEV, g2d.shape[1])

    def ple_grads(l, de, xp, dpre):
        return [col_blocks(_wgrad(pb[l], de, f"wgrad_ple_in{l}", tn=1024), D // N_DEV),
                row_blocks(_wgrad(xp, dpre, f"wgrad_gate{l}", tn=1024))]

    def ffn_grads(l, xf, dhh, act, dhb):
        return [_wgrad(xf, dhh, f"wgrad_up{l}", tn=FF_BLK, out_blocked=True, a_transposed=True, ts=4096),
                row_blocks(_wgrad(act, dhb, f"wgrad_down{l}", tn=1024, ts=4096))]

    dh6, loss_blk, d_norm_final = _head(h6, row(norm_final), loss_target[0], tl)
    dh5, de1, xp1, dpre1, dbg1, dgp1 = _ple_bwd(dh6, h5, row(norm_ple[1]), wgate[1], row(ple_b_gate[1]), pb[1],
                                                 wple[1], tl)
    dh4, xf1, dhh1, act1, dhb5, dcw1, dcb1, dgf1 = _ffn_bwd(dh5, h4, hh1, cc1, row(norm_ffn[1]), wup[1], cw[1], wdown[1], tm)
    ex_ffn1 = _send_start(ple_grads(1, de1, xp1, dpre1) + ffn_grads(1, xf1, dhh1, act1, dhb5), True, "send_ffn1")
    datt, dhb4 = _oproj_bwd(dh4, wo, tl)
    dq, dkv, dsink = _attn_bwd(q, kv, att, datt, after_token(ex_ffn1["token"], sinks.reshape(1, N_Q)).reshape(N_Q), consts)
    dh3, xq, xkv, dgq, dgkv = _qkv_bwd(dh4, h3, dq, dkv, row(norm_mix[1]), wq, row(norm_kv), wkv, tl)
    ex_att = _send_start([row_blocks(_wgrad(att, dhb4, "wgrad_o", tn=1024)),
                              row_blocks(_wgrad(xq, dq, "wgrad_q", tn=1024)),
                              row_blocks(_wgrad(xkv, dkv, "wgrad_kv", tn=512))], True, "send_att")
    dh2, de0, xp0, dpre0, dbg0, dgp0 = _ple_bwd(dh3, h2, after_token(ex_att["token"], row(norm_ple[0])), wgate[0],
                                                 row(ple_b_gate[0]), pb[0], wple[0], tl)
    dh1, xf0, dhh0, act0, dhb2, dcw0, dcb0, dgf0 = _ffn_bwd(dh2, h1, hh0, cc0, row(norm_ffn[0]), wup[0], cw[0], wdown[0], tm)
    ex_ffn0 = _send_start(ple_grads(0, de0, xp0, dpre0) + ffn_grads(0, xf0, dhh0, act0, dhb2), True, "send_ffn0")
    dx, xa, dz, gated, dhb1, dws, dbs, dgv, dga = _mixa_bwd(dh1, x0, zg0, gp0, after_token(ex_ffn0["token"], row(norm_mix[0])), win, gv,
                                                            ws, wst, bsb, wout, tl)
    small_g = {
        'norm_mix': jnp.concatenate([dga, dgq], axis=0), 'norm_ffn': jnp.concatenate([dgf0, dgf1], axis=0),
        'norm_ple': jnp.concatenate([dgp0, dgp1], axis=0), 'norm_kv': dgkv.reshape(D),
        'norm_final': d_norm_final.reshape(D), 'a_w_s': dws[None], 'a_b_s': dbs[None, :, :, 0],
        'b_sinks': dsink[:, 0].reshape(1, N_Q),
        'f_conv_b': jnp.stack([dcb0.reshape(2 * D_FF), dcb1.reshape(2 * D_FF)]),
        'ple_b_gate': jnp.concatenate([dbg0, dbg1], axis=0),
        'a_norm_v': dgv, 'f_conv_w': jnp.stack([dcw0, dcw1], axis=1),
    }
    names_all = _REPL + _SMALL_SHARDED
    full_shapes = [small_g[n].shape for n in names_all]
    small_packed = _pack([small_g[n] for n in names_all], 8)
    ex_small = _send_start([small_packed.reshape(N_DEV, small_packed.shape[0] // N_DEV, ROW)], True, "send_small")
    ex_a_in = _send_start([_wgrad(xa, dz, "wgrad_a_in", tn=2 * AW // N_DEV, out_blocked=True, a_transposed=True,
                                  per_step=4)], True, "send_a_in")
    ex_a_out = _send_start([row_blocks(_wgrad(gated, dhb1, "wgrad_a_out", tn=1024))],
                           True, "send_a_out")

    def as3(a):
        return a.reshape((-1,) + a.shape[-2:])

    def adam(name, parts):
        outs = _adamw_weight(parts, as3(w[name]), as3(mom_m[name]), as3(mom_v[name]), "adamw_" + name)
        return [o.reshape(w[name].shape) for o in outs]

    big = {}
    l_ffn1 = _send_wait(ex_ffn1, [0, 1, 2, 3], ex_a_out["token"], "recv_ffn1")
    l_ple1, l_ffn1 = l_ffn1[:2], l_ffn1[2:]
    l_att = _send_wait(ex_att, list(range(len(ex_att["lands"]))), l_ffn1[0], "recv_att")
    big['b_w_o'], big['b_w_q'], big['w_kv'] = adam('b_w_o', [l_att[0]]), adam('b_w_q', [l_att[1]]), adam('w_kv', [l_att[2]])
    l_ple0 = _send_wait(ex_ffn0, [0, 1], big['w_kv'][0], "recv_ple0")
    big['ple_w_in'] = adam('ple_w_in', [l_ple0[0], l_ple1[0]])
    big['ple_w_gate'] = adam('ple_w_gate', [l_ple0[1], l_ple1[1]])
    small_sum = _sum8(_send_wait(ex_small, [0], big['ple_w_gate'][0], "recv_small")[0], "sum_small_grads")
    ex_small_all = _send_start([small_sum], False, "gather_small_sums")
    l_ffn0 = _send_wait(ex_ffn0, [2, 3], ex_small_all["token"], "recv_ffn0")
    big['f_w_up'] = adam('f_w_up', [l_ffn0[0], l_ffn1[0]])
    big['f_w_down'] = adam('f_w_down', [l_ffn0[1], l_ffn1[1]])
    big['a_w_in'] = adam('a_w_in', _send_wait(ex_a_in, [0], big['f_w_down'][0], "recv_a_in"))
    big['a_w_out'] = adam('a_w_out', _send_wait(ex_a_out, [0], big['a_w_in'][0], "recv_a_out"))
    g_big, d_big, m_big, v_big = [{n: big[n][i] for n in _BIG} for i in range(4)]

    gsum = _send_wait(ex_small_all, [0], big['a_w_out'][0], "recv_small_sums")[0].reshape(small_packed.shape)
    gs = dict(zip(names_all, _unpack(gsum, full_shapes, 8)))
    gs['a_norm_v'] = lax.dynamic_slice_in_dim(gs['a_norm_v'], me * (AW // N_DEV), AW // N_DEV, axis=1)
    gs['f_conv_w'] = lax.dynamic_index_in_dim(gs['f_conv_w'], me, axis=0, keepdims=False)
    gs = {n: gs[n].reshape(w[n].shape) for n in names_all}
    packs = lambda d: _pack([d[n] for n in names_all], 8)
    souts = _adamw_small(packs(gs), packs(w), packs(mom_m), packs(mom_v), "adamw_small")
    shp = [w[n].shape for n in names_all]
    d_s, m_s, v_s = [dict(zip(names_all, _unpack(o, shp, 8))) for o in souts]

    grads = {**g_big, **gs}
    delta = {**d_big, **d_s}
    new_m = {**m_big, **m_s}
    new_v = {**v_big, **v_s}
    loss = lax.psum(loss_blk[0, 0], ("x", "y", "c"))
    return (loss, dx[None], *[grads[n] for n in _PARAMS], *[delta[n] for n in _PARAMS],
            *[new_m[n] for n in _PARAMS], *[new_v[n] for n in _PARAMS])
```

```python
import functools

import numpy as np
import jax
import jax.numpy as jnp
from jax import lax
from jax.experimental import pallas as pl
from jax.experimental.pallas import tpu as pltpu

F32 = jnp.float32
BF = jnp.bfloat16

N_DEV = 8
D = 1024
AW = 1024
GROUPS = 8
CHUNK = 128
HD = 64
N_Q = 16
N_KV = 4
GQA = 4
BLOCK = 128
D_FF = 2816
FF_BLK = 704
N_FF_BLK = 4
PLE = 256
EPS = 1e-6
LR, B1, B2, ADAM_EPS, WD, STEP = 0.001, 0.9, 0.999, 1e-08, 0.01, 10
GELU_C0 = 0.7978845608028654
GELU_C1 = 0.044715
NEG = -1e30
ROW = 1024
VMEM_BIG = 56 * 1024 * 1024

_PARAMS = ['norm_mix', 'norm_ffn', 'norm_ple', 'norm_kv', 'norm_final', 'a_w_in', 'a_norm_v', 'a_w_s', 'a_b_s',
           'a_w_out', 'w_kv', 'b_w_q', 'b_sinks', 'b_w_o', 'f_w_up', 'f_conv_w', 'f_conv_b', 'f_w_down',
           'ple_w_in', 'ple_w_gate', 'ple_b_gate']
_BIG = ['a_w_in', 'a_w_out', 'w_kv', 'b_w_q', 'b_w_o', 'f_w_up', 'f_w_down', 'ple_w_in', 'ple_w_gate']
_SMALL_SHARDED = ['a_norm_v', 'f_conv_w']
_REPL = ['norm_mix', 'norm_ffn', 'norm_ple', 'norm_kv', 'norm_final', 'a_w_s', 'a_b_s', 'b_sinks', 'f_conv_b',
         'ple_b_gate']


def _mm(a, b):
    return jnp.dot(a, b, preferred_element_type=F32)


def _mm_nt(a, b):
    return lax.dot_general(a, b, (((1,), (1,)), ((), ())), preferred_element_type=F32)


def _mm_tn(a, b):
    return lax.dot_general(a, b, (((0,), (0,)), ((), ())), preferred_element_type=F32)


def _const_spec(shape, single=False):
    nd = len(shape)
    if single:
        return pl.BlockSpec(shape, lambda *_: (0,) * nd, pipeline_mode=pl.Buffered(1))
    return pl.BlockSpec(shape, lambda *_: (0,) * nd)


def _row_spec(tm, n):
    return pl.BlockSpec((tm, n), lambda i: (i, 0))


def _rms_fwd(x, g):
    r = lax.rsqrt(jnp.mean(x * x, axis=-1, keepdims=True) + EPS)
    xhat = x * r
    return xhat * g, xhat, r


def _rms_bwd(dy, xhat, r, g):
    a = dy * g
    dx = r * (a - xhat * jnp.mean(a * xhat, axis=-1, keepdims=True))
    return dx, jnp.sum(dy * xhat, axis=0, keepdims=True)


def _sigmoid(x):
    return 1.0 / (1.0 + jnp.exp(-x))


def _params(vmem=None):
    return pltpu.CompilerParams(dimension_semantics=("arbitrary",), vmem_limit_bytes=vmem)


def _tri_masks():
    t = lax.broadcasted_iota(jnp.int32, (CHUNK, CHUNK), 0)
    s = lax.broadcasted_iota(jnp.int32, (CHUNK, CHUNK), 1)
    return t >= s, s >= t


def _mixa_fwd(x, g, win, gv, ws, bsb, wout, tm):
    S = x.shape[0]
    nc = tm // CHUNK

    def body(x_ref, g_ref, win_ref, gv_ref, ws_ref, bsb_ref, wout_ref, h_ref, zg_ref, gp_ref, gated):
        x = x_ref[...]
        xn, _, _ = _rms_fwd(x, g_ref[...])
        z = _mm(xn.astype(BF), win_ref[...])
        z2 = z * z
        t = jnp.tanh(GELU_C0 * (z + GELU_C1 * z2 * z))
        zg = 0.5 * z * (1.0 + t)
        zg_ref[...] = zg.astype(BF)
        gp_ref[...] = (0.5 * (1.0 + t) + 0.5 * z * (1.0 - t * t) * (GELU_C0 * (1.0 + 3.0 * GELU_C1 * z2))).astype(BF)
        u = zg[:, :AW]
        vn, _, _ = _rms_fwd(zg[:, AW:], gv_ref[...])
        vn = vn.astype(BF)
        tril, _ = _tri_masks()
        for h in range(GROUPS):
            wm = jnp.where(tril, ws_ref[h], jnp.zeros((), BF))
            for c in range(nc):
                rs, cs = slice(c * CHUNK, (c + 1) * CHUNK), slice(h * CHUNK, (h + 1) * CHUNK)
                s = _mm(wm, vn[rs, cs]) + bsb_ref[h]
                gated[rs, cs] = (u[rs, cs] * s).astype(BF)
        h_ref[...] = x + _mm(gated[...], wout_ref[...])

    return pl.pallas_call(
        body, name="mixa_fwd", grid=(S // tm,),
        in_specs=[_row_spec(tm, D), _const_spec((1, D)), _const_spec((D, 2 * AW)), _const_spec((1, AW)),
                  _const_spec((GROUPS, CHUNK, CHUNK)), _const_spec((GROUPS, CHUNK, CHUNK)), _const_spec((AW, D))],
        out_specs=[_row_spec(tm, D), _row_spec(tm, 2 * AW), _row_spec(tm, 2 * AW)],
        out_shape=[jax.ShapeDtypeStruct((S, D), F32), jax.ShapeDtypeStruct((S, 2 * AW), BF),
                   jax.ShapeDtypeStruct((S, 2 * AW), BF)],
        scratch_shapes=[pltpu.VMEM((tm, AW), BF)],
        compiler_params=_params(VMEM_BIG),
    )(x, g, win, gv, ws, bsb, wout)


def _mixa_bwd(dh, x, zg, gp, g, win, gv, ws, wst, bsb, wout, tm):
    S = x.shape[0]
    nt = S // tm
    nc = tm // CHUNK

    def body(dh_ref, x_ref, zg_ref, gp_ref, g_ref, win_ref, gv_ref, ws_ref, wst_ref, bsb_ref, wout_ref,
             dx_ref, xn_ref, dz_ref, gated_ref, dhb_ref, dws_ref, dbs_ref, dgv_ref, dg_ref, du_scr, dvn_scr):
        i = pl.program_id(0)

        @pl.when(i == 0)
        def _():
            dws_ref[...] = jnp.zeros_like(dws_ref)
            dbs_ref[...] = jnp.zeros_like(dbs_ref)
            dgv_ref[...] = jnp.zeros_like(dgv_ref)
            dg_ref[...] = jnp.zeros_like(dg_ref)

        x = x_ref[...]
        gmix = g_ref[...]
        xn, xhat, r = _rms_fwd(x, gmix)
        xn_ref[...] = xn.T.astype(BF)
        zg = zg_ref[...].astype(F32)
        gp = gp_ref[...].astype(F32)
        u = zg[:, :AW]
        gvv = gv_ref[...]
        vn, vhat, rv = _rms_fwd(zg[:, AW:], gvv)
        vn = vn.astype(BF)
        dh = dh_ref[...]
        dhb = dh.astype(BF)
        dhb_ref[...] = dhb
        dgated = _mm_nt(dhb, wout_ref[...])
        tril, triu = _tri_masks()
        for h in range(GROUPS):
            wm = jnp.where(tril, ws_ref[h], jnp.zeros((), BF))
            wmt = jnp.where(triu, wst_ref[h], jnp.zeros((), BF))
            for c in range(nc):
                rs, cs = slice(c * CHUNK, (c + 1) * CHUNK), slice(h * CHUNK, (h + 1) * CHUNK)
                vnb = vn[rs, cs]
                s = _mm(wm, vnb) + bsb_ref[h]
                gated_ref[rs, cs] = (u[rs, cs] * s).astype(BF)
                dgt = dgated[rs, cs]
                du_scr[rs, cs] = dgt * s
                ds = dgt * u[rs, cs]
                dsb = ds.astype(BF)
                dws_ref[h] += jnp.where(tril, _mm_nt(dsb, vnb), 0.0)
                dbs_ref[h] += ds
                dvn_scr[rs, cs] = _mm(wmt, dsb)
        dvn = dvn_scr[...]
        dv, dgv = _rms_bwd(dvn, vhat, rv, gvv)
        dgv_ref[...] += dgv
        dz_ref[:, :AW] = (du_scr[...] * gp[:, :AW]).astype(BF)
        dz_ref[:, AW:] = (dv * gp[:, AW:]).astype(BF)
        dxn = _mm_nt(dz_ref[...], win_ref[...])
        dxr, dg = _rms_bwd(dxn, xhat, r, gmix)
        dg_ref[...] += dg
        dx_ref[...] = dh + dxr

        @pl.when(i == nt - 1)
        def _():
            for h in range(GROUPS):
                dbs_ref[h] = jnp.broadcast_to(jnp.sum(dbs_ref[h], axis=-1, keepdims=True), (CHUNK, CHUNK))

    gshape = (GROUPS, CHUNK, CHUNK)
    return pl.pallas_call(
        body, name="mixa_bwd", grid=(nt,),
        in_specs=[_row_spec(tm, D), _row_spec(tm, D), _row_spec(tm, 2 * AW), _row_spec(tm, 2 * AW), _const_spec((1, D)),
                  _const_spec((D, 2 * AW)), _const_spec((1, AW)), _const_spec(gshape), _const_spec(gshape),
                  _const_spec(gshape), _const_spec((AW, D))],
        out_specs=[_row_spec(tm, D), pl.BlockSpec((D, tm), lambda i: (0, i)), _row_spec(tm, 2 * AW), _row_spec(tm, AW),
                   _row_spec(tm, D), _const_spec(gshape), _const_spec(gshape), _const_spec((1, AW)), _const_spec((1, D))],
        out_shape=[jax.ShapeDtypeStruct((S, D), F32), jax.ShapeDtypeStruct((D, S), BF),
                   jax.ShapeDtypeStruct((S, 2 * AW), BF), jax.ShapeDtypeStruct((S, AW), BF),
                   jax.ShapeDtypeStruct((S, D), BF), jax.ShapeDtypeStruct(gshape, F32),
                   jax.ShapeDtypeStruct(gshape, F32), jax.ShapeDtypeStruct((1, AW), F32),
                   jax.ShapeDtypeStruct((1, D), F32)],
        scratch_shapes=[pltpu.VMEM((tm, AW), F32), pltpu.VMEM((tm, AW), F32)],
        compiler_params=_params(VMEM_BIG),
    )(dh, x, zg, gp, g, win, gv, ws, wst, bsb, wout)


def _shift_down(x, prev8, k):
    n, c = x.shape[0] // 8, x.shape[1]
    r = pltpu.roll(x.reshape(n, 8, c), k, axis=1)
    before = jnp.concatenate([pltpu.roll(prev8.reshape(1, 8, c), k, axis=1), r[:-1]], axis=0)
    sub = lax.broadcasted_iota(jnp.int32, (1, 8, c), 1)
    return jnp.where(sub < k, before, r).reshape(x.shape)


def _shift_up(x, next8, k):
    n, c = x.shape[0] // 8, x.shape[1]
    r = pltpu.roll(x.reshape(n, 8, c), 8 - k, axis=1)
    after = jnp.concatenate([r[1:], pltpu.roll(next8.reshape(1, 8, c), 8 - k, axis=1)], axis=0)
    sub = lax.broadcasted_iota(jnp.int32, (1, 8, c), 1)
    return jnp.where(sub >= 8 - k, after, r).reshape(x.shape)


def _ffn_fwd(h, g, wup, cw, cb, wdown, tm):
    S = h.shape[0]

    def body(h_ref, g_ref, wup_ref, cw_ref, cb_ref, wdown_ref, o_ref, hh_ref, cc_ref, sh):
        i = pl.program_id(0)

        @pl.when(i == 0)
        def _():
            sh[...] = jnp.zeros_like(sh)

        x = h_ref[...]
        xn, _, _ = _rms_fwd(x, g_ref[...])
        xn = xn.astype(BF)
        acc = x
        up = lambda d: [_mm(xn, wup_ref[d]), _mm(xn, wup_ref[N_FF_BLK + d])]
        ahead = up(0)
        down = lambda d, act: _mm(act, wdown_ref[d * FF_BLK:(d + 1) * FF_BLK, :])
        behind = None
        for d in range(N_FF_BLK):
            cs = []
            hbs = ahead
            if d + 1 < N_FF_BLK:
                ahead = up(d + 1)
            if behind is not None:
                acc = acc + down(d - 1, behind)
            for blk, hb in zip((d, N_FF_BLK + d), hbs):
                hh_ref[blk] = hb.astype(BF)
                w, prev8 = cw_ref[blk], sh[blk]
                c = cb_ref[blk] + w[0:1] * _shift_down(hb, prev8, 2) + w[1:2] * _shift_down(hb, prev8, 1) + w[2:3] * hb
                sh[blk] = hb[tm - 8:, :]
                cc_ref[blk] = c.astype(BF)
                cs.append(c)
            behind = (cs[0] * _sigmoid(cs[0]) * cs[1]).astype(BF)
        o_ref[...] = acc + down(N_FF_BLK - 1, behind)

    blk3 = pl.BlockSpec((N_DEV, tm, FF_BLK), lambda i: (0, i, 0))
    return pl.pallas_call(
        body, name="ffn_fwd", grid=(S // tm,),
        in_specs=[_row_spec(tm, D), _const_spec((1, D)), _const_spec((N_DEV, D, FF_BLK), True),
                  _const_spec((N_DEV, 3, FF_BLK)), _const_spec((N_DEV, 1, FF_BLK)), _const_spec((D_FF, D), True)],
        out_specs=[_row_spec(tm, D), blk3, blk3],
        out_shape=[jax.ShapeDtypeStruct((S, D), F32), jax.ShapeDtypeStruct((N_DEV, S, FF_BLK), BF),
                   jax.ShapeDtypeStruct((N_DEV, S, FF_BLK), BF)],
        scratch_shapes=[pltpu.VMEM((N_DEV, 8, FF_BLK), F32)],
        compiler_params=_params(VMEM_BIG),
    )(h, g, wup, cw, cb, wdown)


def _ffn_bwd(dh, h, hh, cc, g, wup, cw, wdown, tm):
    S = h.shape[0]
    nt = S // tm

    def rev(i):
        return (nt - 1 - i, 0)

    def body(dh_ref, h_ref, hh_ref, cc_ref, g_ref, wup_ref, cw_ref, wdown_ref,
             dx_ref, xn_ref, dhh_ref, act_ref, dhb_ref, dcw_ref, dcb_ref, dg_ref, sh):
        i = pl.program_id(0)

        @pl.when(i == 0)
        def _():
            sh[...] = jnp.zeros_like(sh)
            dcw_ref[...] = jnp.zeros_like(dcw_ref)
            dcb_ref[...] = jnp.zeros_like(dcb_ref)
            dg_ref[...] = jnp.zeros_like(dg_ref)

        x = h_ref[...]
        gain = g_ref[...]
        xn, xhat, r = _rms_fwd(x, gain)
        xn_ref[...] = xn.T.astype(BF)
        dh = dh_ref[...]
        dhb = dh.astype(BF)
        dhb_ref[...] = dhb
        dxn = jnp.zeros((tm, D), F32)
        dact_of = lambda d: _mm_nt(dhb, wdown_ref[d * FF_BLK:(d + 1) * FF_BLK, :])
        ahead = dact_of(0)
        for d in range(N_FF_BLK):
            blks = (d, N_FF_BLK + d)
            dact = ahead
            if d + 1 < N_FF_BLK:
                ahead = dact_of(d + 1)
            cg, cu = cc_ref[blks[0]].astype(F32), cc_ref[blks[1]].astype(F32)
            sg = _sigmoid(cg)
            sil = cg * sg
            act_ref[d] = (sil * cu).astype(BF)
            dcs = (dact * cu * (sg + sil - sil * sg), dact * sil)
            for blk, dc in zip(blks, dcs):
                next8 = sh[blk]
                dp1, dp2 = _shift_up(dc, next8, 1), _shift_up(dc, next8, 2)
                hb = hh_ref[blk].astype(F32)
                dcb_ref[blk] += jnp.sum(dc, axis=0, keepdims=True)
                dcw_ref[blk, 0:1, :] += jnp.sum(dp2 * hb, axis=0, keepdims=True)
                dcw_ref[blk, 1:2, :] += jnp.sum(dp1 * hb, axis=0, keepdims=True)
                dcw_ref[blk, 2:3, :] += jnp.sum(dc * hb, axis=0, keepdims=True)
                w = cw_ref[blk]
                dhh = (w[2:3] * dc + w[1:2] * dp1 + w[0:1] * dp2).astype(BF)
                sh[blk] = dc[0:8, :]
                dhh_ref[blk] = dhh
                dxn = dxn + _mm_nt(dhh, wup_ref[blk])
        dxr, dg = _rms_bwd(dxn, xhat, r, gain)
        dg_ref[...] += dg
        dx_ref[...] = dh + dxr

    blk3 = lambda i: (0, nt - 1 - i, 0)
    return pl.pallas_call(
        body, name="ffn_bwd", grid=(nt,),
        in_specs=[pl.BlockSpec((tm, D), rev), pl.BlockSpec((tm, D), rev),
                  pl.BlockSpec((N_DEV, tm, FF_BLK), blk3), pl.BlockSpec((N_DEV, tm, FF_BLK), blk3),
                  _const_spec((1, D)), _const_spec((N_DEV, D, FF_BLK), True), _const_spec((N_DEV, 3, FF_BLK)),
                  _const_spec((D_FF, D), True)],
        out_specs=[pl.BlockSpec((tm, D), rev), pl.BlockSpec((D, tm), lambda i: (0, nt - 1 - i)),
                   pl.BlockSpec((N_DEV, tm, FF_BLK), blk3), pl.BlockSpec((N_FF_BLK, tm, FF_BLK), blk3),
                   pl.BlockSpec((tm, D), rev), _const_spec((N_DEV, 3, FF_BLK)), _const_spec((N_DEV, 1, FF_BLK)),
                   _const_spec((1, D))],
        out_shape=[jax.ShapeDtypeStruct((S, D), F32), jax.ShapeDtypeStruct((D, S), BF),
                   jax.ShapeDtypeStruct((N_DEV, S, FF_BLK), BF), jax.ShapeDtypeStruct((N_FF_BLK, S, FF_BLK), BF),
                   jax.ShapeDtypeStruct((S, D), BF), jax.ShapeDtypeStruct((N_DEV, 3, FF_BLK), F32),
                   jax.ShapeDtypeStruct((N_DEV, 1, FF_BLK), F32), jax.ShapeDtypeStruct((1, D), F32)],
        scratch_shapes=[pltpu.VMEM((N_DEV, 8, FF_BLK), F32)],
        compiler_params=_params(VMEM_BIG),
    )(dh, h, hh, cc, g, wup, cw, wdown)


def _ple_fwd(h, g, wgate, bgate, p, wple, tm):
    S = h.shape[0]

    def body(h_ref, g_ref, wg_ref, bg_ref, p_ref, wp_ref, o_ref):
        x = h_ref[...]
        xn, _, _ = _rms_fwd(x, g_ref[...])
        gate = _sigmoid(_mm(xn.astype(BF), wg_ref[...]) + bg_ref[...])
        o_ref[...] = x + _mm(p_ref[...], wp_ref[...]) * gate

    return pl.pallas_call(
        body, name="ple_fwd", grid=(S // tm,),
        in_specs=[_row_spec(tm, D), _const_spec((1, D)), _const_spec((D, D)), _const_spec((1, D)),
                  _row_spec(tm, PLE), _const_spec((PLE, D))],
        out_specs=_row_spec(tm, D), out_shape=jax.ShapeDtypeStruct((S, D), F32),
        compiler_params=_params(),
    )(h, g, wgate, bgate, p, wple)


def _ple_bwd(dh, h, g, wgate, bgate, p, wple, tm):
    S = h.shape[0]

    def body(dh_ref, h_ref, g_ref, wg_ref, bg_ref, p_ref, wp_ref, dx_ref, de_ref, xn_ref, dpre_ref, dbg_ref, dg_ref):
        i = pl.program_id(0)

        @pl.when(i == 0)
        def _():
            dbg_ref[...] = jnp.zeros_like(dbg_ref)
            dg_ref[...] = jnp.zeros_like(dg_ref)

        x = h_ref[...]
        gain = g_ref[...]
        xn, xhat, r = _rms_fwd(x, gain)
        xnb = xn.astype(BF)
        xn_ref[...] = xnb
        gate = _sigmoid(_mm(xnb, wg_ref[...]) + bg_ref[...])
        e = _mm(p_ref[...], wp_ref[...])
        dh = dh_ref[...]
        de_ref[...] = (dh * gate).astype(BF)
        dpre = dh * e * gate * (1.0 - gate)
        dpreb = dpre.astype(BF)
        dpre_ref[...] = dpreb
        dbg_ref[...] += jnp.sum(dpre, axis=0, keepdims=True)
        dxr, dg = _rms_bwd(_mm_nt(dpreb, wg_ref[...]), xhat, r, gain)
        dg_ref[...] += dg
        dx_ref[...] = dh + dxr

    return pl.pallas_call(
        body, name="ple_bwd", grid=(S // tm,),
        in_specs=[_row_spec(tm, D), _row_spec(tm, D), _const_spec((1, D)), _const_spec((D, D)), _const_spec((1, D)),
                  _row_spec(tm, PLE), _const_spec((PLE, D))],
        out_specs=[_row_spec(tm, D), _row_spec(tm, D), _row_spec(tm, D), _row_spec(tm, D), _const_spec((1, D)),
                   _const_spec((1, D))],
        out_shape=[jax.ShapeDtypeStruct((S, D), F32), jax.ShapeDtypeStruct((S, D), BF),
                   jax.ShapeDtypeStruct((S, D), BF), jax.ShapeDtypeStruct((S, D), BF),
                   jax.ShapeDtypeStruct((1, D), F32), jax.ShapeDtypeStruct((1, D), F32)],
        compiler_params=_params(),
    )(dh, h, g, wgate, bgate, p, wple)


def _head_ple_bwd(h, target, gfin, g, wgate, bgate, p, wple, tm):
    S = h.shape[0]

    def body(h_ref, t_ref, gf_ref, g_ref, wg_ref, bg_ref, p_ref, wp_ref,
             dx_ref, de_ref, xn_ref, dpre_ref, dbg_ref, dg_ref, loss_ref, dgf_ref):
        i = pl.program_id(0)

        @pl.when(i == 0)
        def _():
            dbg_ref[...] = jnp.zeros_like(dbg_ref)
            dg_ref[...] = jnp.zeros_like(dg_ref)
            loss_ref[...] = jnp.zeros_like(loss_ref)
            dgf_ref[...] = jnp.zeros_like(dgf_ref)

        x = h_ref[...]
        gain, gfin_ = g_ref[...], gf_ref[...]
        xn, xhat, r = _rms_fwd(x, gain)
        xnb = xn.astype(BF)
        xn_ref[...] = xnb
        gate = _sigmoid(_mm(xnb, wg_ref[...]) + bg_ref[...])
        e = _mm(p_ref[...], wp_ref[...])
        y, yhat, ry = _rms_fwd(x + e * gate, gfin_)
        err = y - t_ref[...]
        tile = 0.5 * jnp.sum(jnp.sum(err * err, axis=-1, keepdims=True) * (1.0 / D), axis=0, keepdims=True)
        loss_ref[...] += jnp.broadcast_to(tile, loss_ref.shape)
        dh, dgf = _rms_bwd(err * (1.0 / D), yhat, ry, gfin_)
        dgf_ref[...] += dgf
        de_ref[...] = (dh * gate).astype(BF)
        dpre = dh * e * gate * (1.0 - gate)
        dpreb = dpre.astype(BF)
        dpre_ref[...] = dpreb
        dbg_ref[...] += jnp.sum(dpre, axis=0, keepdims=True)
        dxr, dg = _rms_bwd(_mm_nt(dpreb, wg_ref[...]), xhat, r, gain)
        dg_ref[...] += dg
        dx_ref[...] = dh + dxr

    vec = _const_spec((1, D))
    return pl.pallas_call(
        body, name="head_ple_bwd", grid=(S // tm,),
        in_specs=[_row_spec(tm, D), _row_spec(tm, D), vec, vec, _const_spec((D, D)), vec, _row_spec(tm, PLE),
                  _const_spec((PLE, D))],
        out_specs=[_row_spec(tm, D), _row_spec(tm, D), _row_spec(tm, D), _row_spec(tm, D), vec, vec,
                   _const_spec((8, 128)), vec],
        out_shape=[jax.ShapeDtypeStruct((S, D), F32), jax.ShapeDtypeStruct((S, D), BF),
                   jax.ShapeDtypeStruct((S, D), BF), jax.ShapeDtypeStruct((S, D), BF),
                   jax.ShapeDtypeStruct((1, D), F32), jax.ShapeDtypeStruct((1, D), F32),
                   jax.ShapeDtypeStruct((8, 128), F32), jax.ShapeDtypeStruct((1, D), F32)],
        compiler_params=_params(VMEM_BIG),
    )(h, target, gfin, g, wgate, bgate, p, wple)


def _qkv_fwd(h, gq, wq, gkv, wkv, tm):
    S = h.shape[0]
    nkv = wkv.shape[1]

    def body(h_ref, gq_ref, wq_ref, gkv_ref, wkv_ref, q_ref, kv_ref):
        x = h_ref[...]
        _, xhat, _ = _rms_fwd(x, gq_ref[...])
        q_ref[...] = _mm((xhat * gq_ref[...]).astype(BF), wq_ref[...]).astype(BF)
        kv_ref[...] = _mm((xhat * gkv_ref[...]).astype(BF), wkv_ref[...]).astype(BF)

    return pl.pallas_call(
        body, name="qkv_fwd", grid=(S // tm,),
        in_specs=[_row_spec(tm, D), _const_spec((1, D)), _const_spec((D, D)), _const_spec((1, D)),
                  _const_spec((D, nkv))],
        out_specs=[_row_spec(tm, D), _row_spec(tm, nkv)],
        out_shape=[jax.ShapeDtypeStruct((S, D), BF), jax.ShapeDtypeStruct((S, nkv), BF)],
        compiler_params=_params(),
    )(h, gq, wq, gkv, wkv)


def _qkv_bwd(dh, h, dq, dkv, gq, wq, gkv, wkv, tm):
    S = h.shape[0]
    nkv = wkv.shape[1]

    def body(dh_ref, h_ref, dq_ref, dkv_ref, gq_ref, wq_ref, gkv_ref, wkv_ref,
             dx_ref, xq_ref, xkv_ref, dgq_ref, dgkv_ref):
        i = pl.program_id(0)

        @pl.when(i == 0)
        def _():
            dgq_ref[...] = jnp.zeros_like(dgq_ref)
            dgkv_ref[...] = jnp.zeros_like(dgkv_ref)

        x = h_ref[...]
        gq_, gkv_ = gq_ref[...], gkv_ref[...]
        _, xhat, r = _rms_fwd(x, gq_)
        xq_ref[...] = (xhat * gq_).astype(BF)
        xkv_ref[...] = (xhat * gkv_).astype(BF)
        d1, dg1 = _rms_bwd(_mm_nt(dq_ref[...], wq_ref[...]), xhat, r, gq_)
        d2, dg2 = _rms_bwd(_mm_nt(dkv_ref[...], wkv_ref[...]), xhat, r, gkv_)
        dgq_ref[...] += dg1
        dgkv_ref[...] += dg2
        dx_ref[...] = dh_ref[...] + d1 + d2

    return pl.pallas_call(
        body, name="qkv_bwd", grid=(S // tm,),
        in_specs=[_row_spec(tm, D), _row_spec(tm, D), _row_spec(tm, D), _row_spec(tm, nkv), _const_spec((1, D)),
                  _const_spec((D, D)), _const_spec((1, D)), _const_spec((D, nkv))],
        out_specs=[_row_spec(tm, D), _row_spec(tm, D), _row_spec(tm, D), _const_spec((1, D)), _const_spec((1, D))],
        out_shape=[jax.ShapeDtypeStruct((S, D), F32), jax.ShapeDtypeStruct((S, D), BF),
                   jax.ShapeDtypeStruct((S, D), BF), jax.ShapeDtypeStruct((1, D), F32),
                   jax.ShapeDtypeStruct((1, D), F32)],
        compiler_params=_params(),
    )(dh, h, dq, dkv, gq, wq, gkv, wkv)


def _oproj_fwd(a, w, res, tm):
    S = a.shape[0]

    def body(a_ref, w_ref, r_ref, o_ref):
        o_ref[...] = r_ref[...] + _mm(a_ref[...], w_ref[...])

    return pl.pallas_call(
        body, name="oproj_fwd", grid=(S // tm,),
        in_specs=[_row_spec(tm, D), _const_spec((D, D)), _row_spec(tm, D)],
        out_specs=_row_spec(tm, D), out_shape=jax.ShapeDtypeStruct((S, D), F32),
        compiler_params=_params(),
    )(a, w, res)


def _oproj_bwd(dh, w, tm):
    S = dh.shape[0]

    def body(dh_ref, w_ref, da_ref, dhb_ref):
        dhb = dh_ref[...].astype(BF)
        dhb_ref[...] = dhb
        da_ref[...] = _mm_nt(dhb, w_ref[...]).astype(BF)

    return pl.pallas_call(
        body, name="oproj_bwd", grid=(S // tm,),
        in_specs=[_row_spec(tm, D), _const_spec((D, D))],
        out_specs=[_row_spec(tm, D), _row_spec(tm, D)],
        out_shape=[jax.ShapeDtypeStruct((S, D), BF), jax.ShapeDtypeStruct((S, D), BF)],
        compiler_params=_params(),
    )(dh, w)


def _alibi_slope(hq):
    return float(np.float32(2.0 ** (-8.0 * (hq + 1) / N_Q)))


def _attn_consts():
    W = N_KV * HD
    r = np.zeros((W, N_KV * W), np.float32)
    for kh in range(N_KV):
        for c in range(W):
            r[kh * HD + c % HD, kh * W + c] = 1.0
    qi = np.arange(BLOCK)[:, None]
    kj = np.arange(2 * BLOCK)[None, :]
    dist = qi + BLOCK - kj
    band = (dist >= 0) & (dist < BLOCK)
    bias = np.full((2, N_KV, GQA * BLOCK, 2 * BLOCK), NEG, np.float32)
    for later in (0, 1):
        valid = band & ((kj >= BLOCK) | bool(later))
        for kh in range(N_KV):
            for g in range(GQA):
                slope = np.float32(_alibi_slope(kh * GQA + g))
                bias[later, kh, g * BLOCK:(g + 1) * BLOCK] = np.where(valid, -slope * dist.astype(np.float32), NEG)
    lane_head = np.arange(W)[None, :] // HD
    maskq = (np.arange(GQA * BLOCK)[:, None] // BLOCK == lane_head).astype(np.float32)
    maskk = (np.arange(GQA * 2 * BLOCK)[:, None] // (2 * BLOCK) == lane_head).astype(np.float32)
    head_sum = (lane_head.T == np.arange(GQA * BLOCK)[None, :] // BLOCK).astype(np.float32)
    return dict(rep=jnp.asarray(r, BF), bias=jnp.asarray(bias), maskq=jnp.asarray(maskq, BF),
                maskqs=jnp.asarray(maskq * HD ** -0.5, BF),
                maskk=jnp.asarray(maskk, BF), head_sum=jnp.asarray(head_sum, BF))


def _sink_col(sink_ref, kh):
    rb = lax.broadcasted_iota(jnp.int32, (GQA * BLOCK, 1), 0) >> 7
    col = jnp.full((GQA * BLOCK, 1), sink_ref[kh * GQA + GQA - 1], F32)
    for g in range(GQA - 1):
        col = jnp.where(rb == g, sink_ref[kh * GQA + g], col)
    return col


def _stack4(a):
    return jnp.concatenate([a] * GQA, axis=0)


def _rows_to_lanes(a):
    return jnp.concatenate([a[g * BLOCK:(g + 1) * BLOCK] for g in range(GQA)], axis=1)


def _lanes_to_rows(a, n):
    return jnp.concatenate([a[:, g * n:(g + 1) * n] for g in range(GQA)], axis=0)


def _scores(qg, k4, bias, sink_col, maskqs):
    qs = _stack4(qg) * maskqs
    s = _mm_nt(qs, k4) + bias
    m = jnp.maximum(jnp.max(s, axis=-1, keepdims=True), sink_col)
    return qs, jnp.exp(s - m), jnp.exp(sink_col - m)


def _bands(kvc_ref, rep_ref, band):
    W = N_KV * HD
    kvc = kvc_ref[...]
    band[0, BLOCK:, :] = _mm(kvc[:, :W], rep_ref[...]).astype(BF)
    band[1, BLOCK:, :] = _mm(kvc[:, W:], rep_ref[...]).astype(BF)
    return band[0], band[1]


def _bands_next(band):
    band[:, :BLOCK, :] = band[:, BLOCK:, :]


def _attn_fwd(q, kv, sinks, consts):
    S = q.shape[0]
    nb = S // BLOCK
    W = N_KV * HD

    def body(sink_ref, q_ref, kvc_ref, rep_ref, bias_ref, maskqs_ref, maskk_ref, o_ref, band):
        @pl.when(pl.program_id(0) == 0)
        def _():
            band[...] = jnp.zeros_like(band)

        maskqs, maskk = maskqs_ref[...], maskk_ref[...]
        lane_head = lax.broadcasted_iota(jnp.int32, (BLOCK, W), 1) >> 6
        k4all, v4all = _bands(kvc_ref, rep_ref, band)
        for kh in range(N_KV):
            cols = slice(kh * W, (kh + 1) * W)
            _, e, es = _scores(q_ref[:, cols], k4all[:, cols], bias_ref[kh], _sink_col(sink_ref, kh), maskqs)
            vbd = jnp.concatenate([_stack4(v4all[:, cols]) * maskk, maskk], axis=1)
            nd = _mm(_rows_to_lanes(e.astype(BF)), vbd)
            es_l = jnp.broadcast_to(es[(GQA - 1) * BLOCK:], (BLOCK, W))
            for g in range(GQA - 1):
                es_l = jnp.where(lane_head == g, es[g * BLOCK:(g + 1) * BLOCK], es_l)
            o_ref[:, cols] = (nd[:, :W] / (nd[:, W:] + es_l)).astype(BF)
        _bands_next(band)

    return pl.pallas_call(
        body, name="attn_fwd", grid=(nb,),
        in_specs=[pl.BlockSpec(memory_space=pltpu.SMEM), _row_spec(BLOCK, D), _row_spec(BLOCK, 2 * W),
                  _const_spec((W, N_KV * W)),
                  pl.BlockSpec((None, N_KV, GQA * BLOCK, 2 * BLOCK), lambda i: (jnp.minimum(i, 1), 0, 0, 0)),
                  _const_spec((GQA * BLOCK, W)), _const_spec((GQA * 2 * BLOCK, W))],
        out_specs=_row_spec(BLOCK, D), out_shape=jax.ShapeDtypeStruct((S, D), BF),
        scratch_shapes=[pltpu.VMEM((2, 2 * BLOCK, N_KV * W), BF)],
        compiler_params=_params(),
    )(sinks, q, kv, consts["rep"], consts["bias"], consts["maskqs"], consts["maskk"])


def _attn_bwd(q, kv, att, do, sinks, consts):
    S = q.shape[0]
    nb = S // BLOCK
    W = N_KV * HD

    def qmap(i):
        return (jnp.minimum(i, nb - 1), 0)

    def body(sink_ref, q_ref, kvc_ref, att_ref, do_ref, rep_ref, bias_ref, maskq_ref, maskqs_ref, maskk_ref,
             hsum_ref, dq_ref, dkv_ref, dsink_ref, carry, sink_acc, band):
        i = pl.program_id(0)

        @pl.when(i == 0)
        def _():
            sink_acc[...] = jnp.zeros_like(sink_acc)
            carry[...] = jnp.zeros_like(carry)
            band[...] = jnp.zeros_like(band)

        @pl.when(i < nb)
        def _():
            maskq, maskqs, maskk, head_sum = maskq_ref[...], maskqs_ref[...], maskk_ref[...], hsum_ref[...]
            ones_k = jnp.ones((2 * BLOCK, BLOCK), BF)
            k4all, v4all = _bands(kvc_ref, rep_ref, band)
            dk4s, dv4s = [], []
            for kh in range(N_KV):
                cols = slice(kh * W, (kh + 1) * W)
                k4, v4 = k4all[:, cols], v4all[:, cols]
                qs, e, es = _scores(q_ref[:, cols], k4, bias_ref[kh], _sink_col(sink_ref, kh), maskqs)
                inv = 1.0 / (_mm(e.astype(BF), ones_k) + es)
                p = e * jnp.concatenate([inv, inv], axis=1)
                dog = do_ref[:, cols]
                dos = _stack4(dog) * maskq
                dp = _mm_nt(dos, v4)
                prod = (dog.astype(F32) * att_ref[:, cols].astype(F32)).astype(BF)
                delta = _lanes_to_rows(_mm(prod, head_sum), BLOCK)
                ds = (p * (dp - jnp.concatenate([delta, delta], axis=1))).astype(BF)
                sink_acc[kh] += es * inv * delta
                kbd = _stack4(k4) * maskk
                dq_ref[:, cols] = (_mm(_rows_to_lanes(ds), kbd) * (HD ** -0.5)).astype(BF)
                dk4s.append(_mm_tn(ds, qs).astype(BF))
                dv4s.append(_mm_tn(p.astype(BF), dos).astype(BF))
            dkband = _mm_nt(jnp.concatenate(dk4s, axis=1), rep_ref[...])
            dvband = _mm_nt(jnp.concatenate(dv4s, axis=1), rep_ref[...])
            dkv_ref[:, :W] = (carry[:, :W] + dkband[:BLOCK]).astype(BF)
            dkv_ref[:, W:] = (carry[:, W:] + dvband[:BLOCK]).astype(BF)
            carry[:, :W] = dkband[BLOCK:]
            carry[:, W:] = dvband[BLOCK:]
            _bands_next(band)

        @pl.when(i == nb)
        def _():
            dkv_ref[...] = carry[...].astype(BF)
            for hq in range(N_Q):
                kh, g = divmod(hq, GQA)
                dsink_ref[hq:hq + 1, :] = -jnp.sum(sink_acc[kh, g * BLOCK:(g + 1) * BLOCK, :], axis=0, keepdims=True)

    return pl.pallas_call(
        body, name="attn_bwd", grid=(nb + 1,),
        in_specs=[pl.BlockSpec(memory_space=pltpu.SMEM), pl.BlockSpec((BLOCK, D), qmap),
                  pl.BlockSpec((BLOCK, 2 * W), qmap),
                  pl.BlockSpec((BLOCK, D), qmap), pl.BlockSpec((BLOCK, D), qmap), _const_spec((W, N_KV * W)),
                  pl.BlockSpec((None, N_KV, GQA * BLOCK, 2 * BLOCK), lambda i: (jnp.minimum(i, 1), 0, 0, 0)),
                  _const_spec((GQA * BLOCK, W)), _const_spec((GQA * BLOCK, W)), _const_spec((GQA * 2 * BLOCK, W)),
                  _const_spec((W, GQA * BLOCK))],
        out_specs=[pl.BlockSpec((BLOCK, D), qmap),
                   pl.BlockSpec((BLOCK, 2 * W), lambda i: (jnp.maximum(i - 1, 0), 0)),
                   _const_spec((N_Q, BLOCK))],
        out_shape=[jax.ShapeDtypeStruct((S, D), BF), jax.ShapeDtypeStruct((S, 2 * W), BF),
                   jax.ShapeDtypeStruct((N_Q, BLOCK), F32)],
        scratch_shapes=[pltpu.VMEM((BLOCK, 2 * W), F32), pltpu.VMEM((N_KV, GQA * BLOCK, BLOCK), F32),
                        pltpu.VMEM((2, 2 * BLOCK, N_KV * W), BF)],
        compiler_params=_params(),
    )(sinks, q, kv, att, do, consts["rep"], consts["bias"], consts["maskq"], consts["maskqs"], consts["maskk"], consts["head_sum"])


def _wgrad(a, b, name, *, tn, out_blocked=False, a_transposed=False, ts=2048, per_step=1):
    a_blocked, b_blocked = a.ndim == 3, b.ndim == 3
    na = a.shape[0] if a_blocked else 1
    S = b.shape[-2]
    K = a.shape[0] if a_transposed else a.shape[-1]
    wide = tn * per_step
    nbk = b.shape[0] if b_blocked else b.shape[1] // wide
    ts = min(ts, S)
    ns = S // ts
    assert not (a_blocked and (out_blocked or a_transposed)) and (per_step == 1 or (out_blocked and not b_blocked))

    def body(a_ref, b_ref, o_ref, acc):
        s = pl.program_id(2)

        @pl.when(s == 0)
        def _():
            acc[...] = jnp.zeros_like(acc)

        if a_transposed:
            acc[...] += _mm(a_ref[...], b_ref[...])
        else:
            acc[...] += _mm_tn(a_ref[...], b_ref[...])

        @pl.when(s == ns - 1)
        def _():
            if per_step == 1:
                o_ref[...] = acc[...].astype(o_ref.dtype)
            else:
                for k in range(per_step):
                    o_ref[k] = acc[:, k * tn:(k + 1) * tn].astype(o_ref.dtype)

    if a_transposed:
        a_spec = pl.BlockSpec((K, ts), lambda i, j, s: (0, s))
    elif a_blocked:
        a_spec = pl.BlockSpec((None, ts, K), lambda i, j, s: (i, s, 0))
    else:
        a_spec = pl.BlockSpec((ts, K), lambda i, j, s: (s, 0))
    b_spec = (pl.BlockSpec((None, ts, tn), lambda i, j, s: (j, s, 0)) if b_blocked
              else pl.BlockSpec((ts, wide), lambda i, j, s: (s, j)))
    if out_blocked:
        o_spec = pl.BlockSpec((None, K, tn) if per_step == 1 else (per_step, K, tn), lambda i, j, s: (j, 0, 0))
        o_shape = jax.ShapeDtypeStruct((nbk * per_step, K, tn), BF)
    else:
        o_spec = pl.BlockSpec((K, wide), lambda i, j, s: (i, j))
        o_shape = jax.ShapeDtypeStruct((na * K, nbk * wide), BF)
    return pl.pallas_call(
        body, name=name, grid=(na, nbk, ns), in_specs=[a_spec, b_spec], out_specs=o_spec, out_shape=o_shape,
        scratch_shapes=[pltpu.VMEM((K, wide), F32)],
        compiler_params=pltpu.CompilerParams(dimension_semantics=("arbitrary", "arbitrary", "arbitrary"),
                                             vmem_limit_bytes=VMEM_BIG),
    )(a, b)


def _adamw_math(w, g, m, v):
    m = B1 * m + (1.0 - B1) * g
    v = B2 * v + (1.0 - B2) * (g * g)
    m_hat = m / (1.0 - B1 ** STEP)
    v_hat = v / (1.0 - B2 ** STEP)
    return -LR * (m_hat / (jnp.sqrt(v_hat) + ADAM_EPS) + WD * w), m, v


def _sum8(parts, name):
    R = parts.shape[1]

    def body(p_ref, o_ref):
        g = p_ref[0]
        for s in range(1, N_DEV):
            g = g + p_ref[s]
        o_ref[...] = g

    return pl.pallas_call(
        body, name=name, grid=(1,),
        in_specs=[pl.BlockSpec((N_DEV, R, ROW), lambda i: (0, 0, 0))],
        out_specs=pl.BlockSpec((R, ROW), lambda i: (0, 0)), out_shape=jax.ShapeDtypeStruct((R, ROW), F32),
        compiler_params=_params(),
    )(parts)


def _adamw_small(g, w, m, v, name):
    R = w.shape[0]

    def body(g_ref, w_ref, m_ref, v_ref, d_ref, nm_ref, nv_ref):
        d_ref[...], nm_ref[...], nv_ref[...] = _adamw_math(w_ref[...], g_ref[...], m_ref[...], v_ref[...])

    spec = pl.BlockSpec((R, ROW), lambda i: (0, 0))
    return pl.pallas_call(
        body, name=name, grid=(1,), in_specs=[spec] * 4, out_specs=[spec] * 3,
        out_shape=[jax.ShapeDtypeStruct((R, ROW), F32)] * 3, compiler_params=_params(),
    )(g, w, m, v)


def _mesh_pos():
    return lax.axis_index("x"), lax.axis_index("y"), lax.axis_index("c")


def _flip(v, bit):
    return 1 - v if bit else v


def _peers(x, y, c):
    out = []
    for k in range(1, N_DEV):
        px, py, pc = _flip(x, k & 4), _flip(y, k & 2), _flip(c, k & 1)
        out.append((k - 1, (px, py, pc), 4 * px + 2 * py + pc))
    return out


def _all_gather_now(xs, name):
    n = len(xs)
    MESH = pl.DeviceIdType.MESH

    def body(*refs):
        x_refs, out_refs = refs[:n], refs[n:2 * n]
        send_sems, recv_sems, local_sems = refs[2 * n:]
        x, y, c = _mesh_pos()
        me, sibling = (x, y, c), (x, y, 1 - c)
        chips = [(1 - x, y), (x, 1 - y), (1 - x, 1 - y)]
        blk = lambda p: 4 * p[0] + 2 * p[1] + p[2]

        def copy(a, k, block, to, src=None):
            return pltpu.make_async_remote_copy(
                src_ref=out_refs[a].at[blk(block)] if src is None else src, dst_ref=out_refs[a].at[blk(block)],
                send_sem=send_sems.at[a, k], recv_sem=recv_sems.at[a, k], device_id=to, device_id_type=MESH)

        sent, own = [], []
        for a in range(n):
            own.append(pltpu.make_async_copy(x_refs[a], out_refs[a].at[blk(me)], local_sems.at[a]))
            own[-1].start()
            first = [copy(a, 1 + j, me, (*chip, c), src=x_refs[a]) for j, chip in enumerate(chips)]
            first.append(copy(a, 0, me, sibling, src=x_refs[a]))
            for cp in first:
                cp.start()
            sent += first
        for a in range(n):
            for j, chip in enumerate(chips):
                copy(a, 1 + j, (*chip, c), me).wait_recv()
                sent.append(copy(a, 4 + j, (*chip, c), sibling))
                sent[-1].start()
        for a in range(n):
            copy(a, 0, sibling, me).wait_recv()
            for j, chip in enumerate(chips):
                copy(a, 4 + j, (*chip, 1 - c), me).wait_recv()
        for cp in sent:
            cp.wait_send()
        for cp in own:
            cp.wait()

    hbm = pl.BlockSpec(memory_space=pl.ANY)
    return pl.pallas_call(
        body, name=name, out_shape=[jax.ShapeDtypeStruct((N_DEV,) + x.shape, x.dtype) for x in xs],
        in_specs=[hbm] * n, out_specs=[hbm] * n,
        scratch_shapes=[pltpu.SemaphoreType.DMA((n, 7)), pltpu.SemaphoreType.DMA((n, 7)), pltpu.SemaphoreType.DMA((n,))],
    )(*xs)


def _chip_peers(x, y, c):
    return (x, y, c), (x, y, 1 - c), [(1 - x, y, c), (x, 1 - y, c), (1 - x, 1 - y, c)]


def _blk(p):
    return 4 * p[0] + 2 * p[1] + p[2]


def _gather2_start(srcs, name):
    n = len(srcs)
    MESH = pl.DeviceIdType.MESH

    def body(*refs):
        src_refs, land_refs = refs[:n], refs[n:2 * n]
        send, recv_ici, recv_sib, local = (refs[(2 + k) * n:(3 + k) * n] for k in range(4))
        token = refs[8 * n]
        me, sibling, chips = _chip_peers(*_mesh_pos())
        for a in range(n):
            dst = land_refs[a].at[_blk(me)]
            pltpu.make_async_remote_copy(src_ref=src_refs[a], dst_ref=dst, send_sem=send[a].at[0], recv_sem=recv_sib[a],
                                         device_id=sibling, device_id_type=MESH).start()
            for j, chip in enumerate(chips):
                pltpu.make_async_remote_copy(src_ref=src_refs[a], dst_ref=dst, send_sem=send[a].at[1 + j],
                                             recv_sem=recv_ici[a].at[j], device_id=chip, device_id_type=MESH).start()
            pltpu.make_async_copy(src_refs[a], dst, local[a]).start()
        token[...] = jnp.zeros_like(token)

    hbm = pl.BlockSpec(memory_space=pltpu.HBM)
    sem = pl.BlockSpec(memory_space=pltpu.SEMAPHORE)
    dma = pltpu.SemaphoreType.DMA
    outs = pl.pallas_call(
        body, name=name,
        out_shape=([dma((4,))] * n + [dma((3,))] * n + [dma(())] * n + [dma(())] * n
                   + [pltpu.HBM(s.shape, s.dtype) for s in srcs]
                   + [pltpu.HBM((N_DEV,) + s.shape, s.dtype) for s in srcs] + [jax.ShapeDtypeStruct((8, 128), F32)]),
        in_specs=[hbm] * (2 * n), out_specs=[sem] * (4 * n) + [hbm] * (2 * n) + [pl.BlockSpec(memory_space=pltpu.VMEM)],
        input_output_aliases={**{a: 4 * n + a for a in range(n)}, **{n + a: 5 * n + a for a in range(n)}},
        compiler_params=pltpu.CompilerParams(has_side_effects=pltpu.SideEffectType.DATAFLOW_SIDE_EFFECTING),
    )(*[pltpu.with_memory_space_constraint(s, pltpu.HBM) for s in srcs],
      *[pltpu.with_memory_space_constraint(lax.empty((N_DEV,) + s.shape, s.dtype), pltpu.HBM) for s in srcs])
    keys = ["send", "recv_ici", "recv_sib", "local", "srcs", "lands"]
    return {**{k: outs[i * n:(i + 1) * n] for i, k in enumerate(keys)}, "token": outs[6 * n]}


def _gather2_forward(started, after, name):
    lands, recv_ici = started["lands"], started["recv_ici"]
    n = len(lands)
    MESH = pl.DeviceIdType.MESH

    def body(*refs):
        land_refs, recv_refs = refs[:n], refs[n:2 * n]
        fsend, frecv = refs[2 * n + 1 + n:2 * n + 1 + 2 * n], refs[2 * n + 1 + 2 * n:2 * n + 1 + 3 * n]
        me, sibling, chips = _chip_peers(*_mesh_pos())
        for a in range(n):
            for j, chip in enumerate(chips):
                rows = land_refs[a].at[_blk(chip)]
                pltpu.make_async_remote_copy(src_ref=rows, dst_ref=rows, send_sem=fsend[a].at[j], recv_sem=recv_refs[a].at[j],
                                             device_id=chip, device_id_type=MESH).wait_recv()
                pltpu.make_async_remote_copy(src_ref=rows, dst_ref=rows, send_sem=fsend[a].at[j], recv_sem=frecv[a].at[j],
                                             device_id=sibling, device_id_type=MESH).start()

    hbm = pl.BlockSpec(memory_space=pltpu.HBM)
    sem = pl.BlockSpec(memory_space=pltpu.SEMAPHORE)
    dma = pltpu.SemaphoreType.DMA
    outs = pl.pallas_call(
        body, name=name,
        out_shape=[pltpu.HBM(l.shape, l.dtype) for l in lands] + [dma((3,))] * (2 * n),
        in_specs=[hbm] * n + [sem] * n + [pl.BlockSpec(memory_space=pl.ANY)], out_specs=[hbm] * n + [sem] * (2 * n),
        input_output_aliases={a: a for a in range(n)},
        compiler_params=pltpu.CompilerParams(has_side_effects=pltpu.SideEffectType.DATAFLOW_SIDE_EFFECTING),
    )(*lands, *recv_ici, after)
    return {**started, "lands": outs[:n], "fsend": outs[n:2 * n], "frecv": outs[2 * n:3 * n]}


def _gather2_wait(st, after, name):
    n = len(st["lands"])
    MESH = pl.DeviceIdType.MESH

    def body(*refs):
        src_refs, land_refs = refs[:n], refs[n:2 * n]
        send, recv_sib, local, fsend, frecv = (refs[(2 + k) * n:(3 + k) * n] for k in range(5))
        me, sibling, chips = _chip_peers(*_mesh_pos())
        for a in range(n):
            def desc(src, dst, s, r):
                return pltpu.make_async_remote_copy(src_ref=src, dst_ref=dst, send_sem=s, recv_sem=r, device_id=sibling,
                                                    device_id_type=MESH)
            own = land_refs[a].at[_blk(me)]
            for k in range(4):
                desc(src_refs[a], own, send[a].at[k], recv_sib[a]).wait_send()
            desc(src_refs[a], land_refs[a].at[_blk(sibling)], send[a].at[0], recv_sib[a]).wait_recv()
            pltpu.make_async_copy(src_refs[a], own, local[a]).wait()
            for j, chip in enumerate(chips):
                mine, theirs = land_refs[a].at[_blk(chip)], land_refs[a].at[_blk((chip[0], chip[1], 1 - chip[2]))]
                desc(mine, mine, fsend[a].at[j], frecv[a].at[j]).wait_send()
                desc(theirs, theirs, fsend[a].at[j], frecv[a].at[j]).wait_recv()

    hbm = pl.BlockSpec(memory_space=pltpu.HBM)
    sem = pl.BlockSpec(memory_space=pltpu.SEMAPHORE)
    srcs, lands = st["srcs"], st["lands"]
    outs = pl.pallas_call(
        body, name=name, out_shape=[pltpu.HBM(s.shape, s.dtype) for s in list(srcs) + list(lands)],
        in_specs=[hbm] * (2 * n) + [sem] * (5 * n) + [pl.BlockSpec(memory_space=pl.ANY)], out_specs=[hbm] * (2 * n),
        input_output_aliases={a: a for a in range(2 * n)},
        compiler_params=pltpu.CompilerParams(has_side_effects=pltpu.SideEffectType.DATAFLOW_SIDE_EFFECTING),
    )(*srcs, *lands, *st["send"], *st["recv_sib"], *st["local"], *st["fsend"], *st["frecv"], after)
    return outs[n:]


def _send_start(srcs, blocked, name):
    n = len(srcs)
    MESH = pl.DeviceIdType.MESH

    def body(*refs):
        src_refs, land_refs = refs[:n], refs[n:2 * n]
        send_sems, recv_sems, local_sems = refs[2 * n:3 * n], refs[3 * n:4 * n], refs[4 * n:5 * n]
        token = refs[7 * n]
        x, y, c = _mesh_pos()
        me = 4 * x + 2 * y + c
        for a in range(n):
            for k, peer, blk in _peers(x, y, c):
                pltpu.make_async_remote_copy(
                    src_ref=src_refs[a].at[blk] if blocked else src_refs[a], dst_ref=land_refs[a].at[me],
                    send_sem=send_sems[a].at[k], recv_sem=recv_sems[a].at[k], device_id=peer,
                    device_id_type=MESH).start()
            pltpu.make_async_copy(src_refs[a].at[me] if blocked else src_refs[a], land_refs[a].at[me],
                                  local_sems[a]).start()
        token[...] = jnp.zeros_like(token)

    hbm = pl.BlockSpec(memory_space=pltpu.HBM)
    sem = pl.BlockSpec(memory_space=pltpu.SEMAPHORE)
    lshape = [s.shape if blocked else (N_DEV,) + s.shape for s in srcs]
    outs = pl.pallas_call(
        body, name=name,
        out_shape=([pltpu.SemaphoreType.DMA((N_DEV - 1,))] * (2 * n) + [pltpu.SemaphoreType.DMA(())] * n
                   + [pltpu.HBM(s.shape, s.dtype) for s in srcs]
                   + [pltpu.HBM(ls, s.dtype) for ls, s in zip(lshape, srcs)] + [jax.ShapeDtypeStruct((8, 128), F32)]),
        in_specs=[hbm] * (2 * n), out_specs=[sem] * (3 * n) + [hbm] * (2 * n) + [pl.BlockSpec(memory_space=pltpu.VMEM)],
        input_output_aliases={**{a: 3 * n + a for a in range(n)}, **{n + a: 4 * n + a for a in range(n)}},
        compiler_params=pltpu.CompilerParams(has_side_effects=pltpu.SideEffectType.DATAFLOW_SIDE_EFFECTING),
    )(*[pltpu.with_memory_space_constraint(s, pltpu.HBM) for s in srcs],
      *[pltpu.with_memory_space_constraint(lax.empty(ls, s.dtype), pltpu.HBM) for ls, s in zip(lshape, srcs)])
    return dict(send=outs[:n], recv=outs[n:2 * n], local=outs[2 * n:3 * n], srcs=outs[3 * n:4 * n],
                lands=outs[4 * n:5 * n], token=outs[5 * n], blocked=blocked)


def _send_wait(started, which, after, name):
    n = len(which)
    blocked = started["blocked"]
    pick = lambda key: [started[key][a] for a in which]
    MESH = pl.DeviceIdType.MESH

    def body(*refs):
        src_refs, land_refs = refs[:n], refs[n:2 * n]
        send, recv, local = refs[2 * n:3 * n], refs[3 * n:4 * n], refs[4 * n:5 * n]
        x, y, c = _mesh_pos()
        me = 4 * x + 2 * y + c
        for a in range(n):
            for k, peer, blk in _peers(x, y, c):
                cp = pltpu.make_async_remote_copy(
                    src_ref=src_refs[a].at[blk] if blocked else src_refs[a], dst_ref=land_refs[a].at[blk],
                    send_sem=send[a].at[k], recv_sem=recv[a].at[k], device_id=peer, device_id_type=MESH)
                cp.wait_send()
                cp.wait_recv()
            pltpu.make_async_copy(src_refs[a].at[me] if blocked else src_refs[a], land_refs[a].at[me],
                                  local[a]).wait()

    hbm = pl.BlockSpec(memory_space=pltpu.HBM)
    sem = pl.BlockSpec(memory_space=pltpu.SEMAPHORE)
    srcs, lands = pick("srcs"), pick("lands")
    outs = pl.pallas_call(
        body, name=name, out_shape=[pltpu.HBM(s.shape, s.dtype) for s in srcs + lands],
        in_specs=[hbm] * (2 * n) + [sem] * (3 * n) + [pl.BlockSpec(memory_space=pl.ANY)], out_specs=[hbm] * (2 * n),
        input_output_aliases={a: a for a in range(2 * n)},
        compiler_params=pltpu.CompilerParams(has_side_effects=pltpu.SideEffectType.DATAFLOW_SIDE_EFFECTING),
    )(*srcs, *lands, *pick("send"), *pick("recv"), *pick("local"), after)
    return outs[n:]


def _adamw_weight(parts, w, m, v, name):
    L, R, C = w.shape
    tr = max(t for t in range(16, min(R, 256) + 1, 16) if R % t == 0)
    nr = R // tr

    def body(*refs):
        p_refs = refs[:L]
        w_ref, m_ref, v_ref, g_ref, d_ref, nm_ref, nv_ref, gsum = refs[L:]
        layer = pl.program_id(0)
        for l in range(L):
            @pl.when(layer == l)
            def _(l=l):
                g = p_refs[l][0].astype(F32)
                for s in range(1, N_DEV):
                    g = g + p_refs[l][s].astype(F32)
                gsum[...] = g
        g = gsum[...]
        g_ref[...] = g
        d_ref[...], nm_ref[...], nv_ref[...] = _adamw_math(w_ref[...], g, m_ref[...], v_ref[...])

    def part_spec(l):
        return pl.BlockSpec((N_DEV, tr, C), lambda layer, i: (0, jnp.where(layer == l, i, jnp.where(layer < l, 0, nr - 1)), 0))

    ws = pl.BlockSpec((None, tr, C), lambda layer, i: (layer, i, 0))
    return pl.pallas_call(
        body, name=name, grid=(L, nr), in_specs=[part_spec(l) for l in range(L)] + [ws] * 3, out_specs=[ws] * 4,
        out_shape=[jax.ShapeDtypeStruct((L, R, C), F32)] * 4, scratch_shapes=[pltpu.VMEM((tr, C), F32)],
        compiler_params=pltpu.CompilerParams(dimension_semantics=("arbitrary", "arbitrary")),
    )(*parts, w, m, v)


def _seg_rows(n, align):
    rows = -(-n // ROW)
    return -(-rows // align) * align


def _pack(arrs, align, lead=()):
    parts = []
    for a in arrs:
        n = int(np.prod(a.shape[len(lead):]))
        rows = _seg_rows(n, align)
        flat = a.reshape(lead + (n,))
        flat = jnp.pad(flat, [(0, 0)] * len(lead) + [(0, rows * ROW - n)])
        parts.append(flat.reshape(lead + (rows, ROW)))
    return jnp.concatenate(parts, axis=len(lead))


def _unpack(buf, shapes, align, lead=()):
    out, r0 = [], 0
    for shp in shapes:
        n = int(np.prod(shp))
        rows = _seg_rows(n, align)
        seg = lax.slice_in_dim(buf, r0, r0 + rows, axis=len(lead))
        out.append(seg.reshape(lead + (rows * ROW,))[..., :n].reshape(lead + tuple(shp)))
        r0 += rows
    return out


def kernel(x, p, norm_mix, norm_ffn, norm_ple, norm_kv, norm_final, a_w_in, a_norm_v, a_w_s, a_b_s, a_w_out, w_kv, b_w_q, b_sinks, b_w_o, f_w_up, f_conv_w, f_conv_b, f_w_down, ple_w_in, ple_w_gate, ple_b_gate, loss_target, m_norm_mix, m_norm_ffn, m_norm_ple, m_norm_kv, m_norm_final, m_a_w_in, m_a_norm_v, m_a_w_s, m_a_b_s, m_a_w_out, m_w_kv, m_b_w_q, m_b_sinks, m_b_w_o, m_f_w_up, m_f_conv_w, m_f_conv_b, m_f_w_down, m_ple_w_in, m_ple_w_gate, m_ple_b_gate, v_norm_mix, v_norm_ffn, v_norm_ple, v_norm_kv, v_norm_final, v_a_w_in, v_a_norm_v, v_a_w_s, v_a_b_s, v_a_w_out, v_w_kv, v_b_w_q, v_b_sinks, v_b_w_o, v_f_w_up, v_f_conv_w, v_f_conv_b, v_f_w_down, v_ple_w_in, v_ple_w_gate, v_ple_b_gate):
    w = dict(norm_mix=norm_mix, norm_ffn=norm_ffn, norm_ple=norm_ple, norm_kv=norm_kv, norm_final=norm_final,
             a_w_in=a_w_in, a_norm_v=a_norm_v, a_w_s=a_w_s, a_b_s=a_b_s, a_w_out=a_w_out, w_kv=w_kv, b_w_q=b_w_q,
             b_sinks=b_sinks, b_w_o=b_w_o, f_w_up=f_w_up, f_conv_w=f_conv_w, f_conv_b=f_conv_b, f_w_down=f_w_down,
             ple_w_in=ple_w_in, ple_w_gate=ple_w_gate, ple_b_gate=ple_b_gate)
    mom_m = dict(norm_mix=m_norm_mix, norm_ffn=m_norm_ffn, norm_ple=m_norm_ple, norm_kv=m_norm_kv,
                 norm_final=m_norm_final, a_w_in=m_a_w_in, a_norm_v=m_a_norm_v, a_w_s=m_a_w_s, a_b_s=m_a_b_s,
                 a_w_out=m_a_w_out, w_kv=m_w_kv, b_w_q=m_b_w_q, b_sinks=m_b_sinks, b_w_o=m_b_w_o, f_w_up=m_f_w_up,
                 f_conv_w=m_f_conv_w, f_conv_b=m_f_conv_b, f_w_down=m_f_w_down, ple_w_in=m_ple_w_in,
                 ple_w_gate=m_ple_w_gate, ple_b_gate=m_ple_b_gate)
    mom_v = dict(norm_mix=v_norm_mix, norm_ffn=v_norm_ffn, norm_ple=v_norm_ple, norm_kv=v_norm_kv,
                 norm_final=v_norm_final, a_w_in=v_a_w_in, a_norm_v=v_a_norm_v, a_w_s=v_a_w_s, a_b_s=v_a_b_s,
                 a_w_out=v_a_w_out, w_kv=v_w_kv, b_w_q=v_b_w_q, b_sinks=v_b_sinks, b_w_o=v_b_w_o, f_w_up=v_f_w_up,
                 f_conv_w=v_f_conv_w, f_conv_b=v_f_conv_b, f_w_down=v_f_w_down, ple_w_in=v_ple_w_in,
                 ple_w_gate=v_ple_w_gate, ple_b_gate=v_ple_b_gate)
    S = x.shape[1]
    tm = min(256, S)
    tl = min(512, S)
    me = 4 * lax.axis_index("x") + 2 * lax.axis_index("y") + lax.axis_index("c")

    def after_token(tok, a):
        return a + tok[0:1, 0:1].reshape((1,) * a.ndim)

    wb = {n: w[n].astype(BF) for n in _BIG}
    order = [('a_w_in', 0), ('a_w_out', 0), ('a_norm_v', None), ('f_conv_w', 0), ('f_conv_w', 1),
             ('f_w_up', 0), ('f_w_down', 0),
             ('ple_w_in', 0), ('ple_w_gate', 0), ('w_kv', None), ('b_w_q', 0), ('b_w_o', 0),
             ('f_w_up', 1), ('f_w_down', 1), ('ple_w_in', 1), ('ple_w_gate', 1)]
    src = lambda n, l: (wb[n] if n in wb else w[n]) if l is None else (wb[n] if n in wb else w[n])[l]
    groups = [order[:5], order[5:7], order[7:12], order[12:]]
    g = dict(zip(groups[0], _all_gather_now([src(n, l) for n, l in groups[0]], "gather_mix")))

    def start_group(keys, tie_to, name):
        srcs = [src(n, l) for n, l in keys]
        k = min(range(len(srcs)), key=lambda i: srcs[i].size)
        zero = (tie_to[(0,) * (tie_to.ndim - 1)][0:1] * 0).astype(srcs[k].dtype)
        srcs[k] = srcs[k] + zero.reshape((1,) * srcs[k].ndim)
        return _send_start(srcs, False, name)

    def finish_group(started, keys, after, name):
        return dict(zip(keys, _send_wait(started, list(range(len(keys))), after, name)))

    x0 = x[0]
    win = g[('a_w_in', 0)].transpose(1, 0, 2).reshape(D, 2 * AW)
    wout = g[('a_w_out', 0)].reshape(AW, D)
    gv = g[('a_norm_v', None)].reshape(1, AW)
    cw = [g[('f_conv_w', l)] for l in range(2)]
    cb = [w['f_conv_b'][l].reshape(N_DEV, 1, FF_BLK) for l in range(2)]
    ws = a_w_s[0].astype(BF)
    wst = jnp.swapaxes(ws, 1, 2)
    bsb = jnp.broadcast_to(a_b_s[0][:, :, None], (GROUPS, CHUNK, CHUNK))
    row = lambda a: a.reshape(1, -1)
    pb = [p[l, 0].astype(BF) for l in range(2)]
    sinks = b_sinks.reshape(N_Q)
    consts = _attn_consts()

    srcs1 = [src(n, l) for n, l in groups[1]]
    srcs1[1] = srcs1[1] + (g[('a_w_out', 0)][0, 0:1, 0:1] * 0).astype(srcs1[1].dtype)
    s1 = _gather2_start(srcs1, "gather_start_ffn0")
    h1, zg0, gp0 = _mixa_fwd(x0, after_token(s1["token"], row(norm_mix[0])), win, gv, ws, bsb, wout, tl)
    s1 = _gather2_forward(s1, h1, "gather_forward_ffn0")
    g = dict(zip(groups[1], _gather2_wait(s1, h1, "gather_wait_ffn0")))
    wup, wdown = [g[('f_w_up', 0)], None], [g[('f_w_down', 0)].reshape(D_FF, D), None]
    s2 = start_group(groups[2], g[('f_w_down', 0)], "gather_start_mid")
    h2, hh0, cc0 = _ffn_fwd(h1, after_token(s2["token"], row(norm_ffn[0])), wup[0], cw[0], cb[0], wdown[0], tm)
    g = finish_group(s2, groups[2], h2, "gather_wait_mid")
    wple = [g[('ple_w_in', 0)].transpose(1, 0, 2).reshape(PLE, D), None]
    wgate = [g[('ple_w_gate', 0)].reshape(D, D), None]
    wkv, wq, wo = g[('w_kv', None)].reshape(D, 2 * N_KV * HD), g[('b_w_q', 0)].reshape(D, D), g[('b_w_o', 0)].reshape(D, D)
    s3 = start_group(groups[3], g[('ple_w_in', 0)], "gather_start_l1")
    h3 = _ple_fwd(h2, after_token(s3["token"], row(norm_ple[0])), wgate[0], row(ple_b_gate[0]), pb[0], wple[0], tl)
    q, kv = _qkv_fwd(h3, row(norm_mix[1]), wq, row(norm_kv), wkv, tl)
    att = _attn_fwd(q, kv, sinks, consts)
    h4 = _oproj_fwd(att, wo, h3, tl)
    g = finish_group(s3, groups[3], h4, "gather_wait_l1")
    wup[1], wdown[1] = g[('f_w_up', 1)], g[('f_w_down', 1)].reshape(D_FF, D)
    wple[1], wgate[1] = g[('ple_w_in', 1)].transpose(1, 0, 2).reshape(PLE, D), g[('ple_w_gate', 1)].reshape(D, D)
    h5, hh1, cc1 = _ffn_fwd(h4, row(norm_ffn[1]), wup[1], cw[1], cb[1], wdown[1], tm)

    def col_blocks(g2d, n):
        return g2d.reshape(g2d.shape[0], N_DEV, n).transpose(1, 0, 2)

    def row_blocks(g2d):
        return g2d.reshape(N_DEV, g2d.shape[0] // N_DEV, g2d.shape[1])

    def ple_grads(l, de, xp, dpre):
        return [col_blocks(_wgrad(pb[l], de, f"wgrad_ple_in{l}", tn=1024), D // N_DEV),
                row_blocks(_wgrad(xp, dpre, f"wgrad_gate{l}", tn=1024))]

    def ffn_grads(l, xf, dhh, act, dhb):
        return [_wgrad(xf, dhh, f"wgrad_up{l}", tn=FF_BLK, out_blocked=True, a_transposed=True, ts=4096),
                row_blocks(_wgrad(act, dhb, f"wgrad_down{l}", tn=1024, ts=4096))]

    dh5, de1, xp1, dpre1, dbg1, dgp1, loss_blk, d_norm_final = _head_ple_bwd(
        h5, loss_target[0], row(norm_final), row(norm_ple[1]), wgate[1], row(ple_b_gate[1]), pb[1], wple[1], tl)
    ex_ple1 = _send_start(ple_grads(1, de1, xp1, dpre1), True, "send_ple1")
    dh4, xf1, dhh1, act1, dhb5, dcw1, dcb1, dgf1 = _ffn_bwd(dh5, h4, hh1, cc1, after_token(ex_ple1["token"], row(norm_ffn[1])),
                                                            wup[1], cw[1], wdown[1], tm)
    ex_ffn1 = _send_start(ffn_grads(1, xf1, dhh1, act1, dhb5), True, "send_ffn1")
    datt, dhb4 = _oproj_bwd(dh4, wo, tl)
    dq, dkv, dsink = _attn_bwd(q, kv, att, datt, after_token(ex_ffn1["token"], sinks.reshape(1, N_Q)).reshape(N_Q), consts)
    dh3, xq, xkv, dgq, dgkv = _qkv_bwd(dh4, h3, dq, dkv, row(norm_mix[1]), wq, row(norm_kv), wkv, tl)
    ex_att = _send_start([row_blocks(_wgrad(att, dhb4, "wgrad_o", tn=1024)),
                              row_blocks(_wgrad(xq, dq, "wgrad_q", tn=1024)),
                              row_blocks(_wgrad(xkv, dkv, "wgrad_kv", tn=512))], True, "send_att")
    dh2, de0, xp0, dpre0, dbg0, dgp0 = _ple_bwd(dh3, h2, after_token(ex_att["token"], row(norm_ple[0])), wgate[0],
                                                 row(ple_b_gate[0]), pb[0], wple[0], tl)
    ex_ple0 = _send_start(ple_grads(0, de0, xp0, dpre0), True, "send_ple0")
    dh1, xf0, dhh0, act0, dhb2, dcw0, dcb0, dgf0 = _ffn_bwd(dh2, h1, hh0, cc0, after_token(ex_ple0["token"], row(norm_ffn[0])),
                                                            wup[0], cw[0], wdown[0], tm)
    ex_ffn0 = _send_start(ffn_grads(0, xf0, dhh0, act0, dhb2), True, "send_ffn0")
    dx, xa, dz, gated, dhb1, dws, dbs, dgv, dga = _mixa_bwd(dh1, x0, zg0, gp0, after_token(ex_ffn0["token"], row(norm_mix[0])), win, gv,
                                                            ws, wst, bsb, wout, tl)
    small_g = {
        'norm_mix': jnp.concatenate([dga, dgq], axis=0), 'norm_ffn': jnp.concatenate([dgf0, dgf1], axis=0),
        'norm_ple': jnp.concatenate([dgp0, dgp1], axis=0), 'norm_kv': dgkv.reshape(D),
        'norm_final': d_norm_final.reshape(D), 'a_w_s': dws[None], 'a_b_s': dbs[None, :, :, 0],
        'b_sinks': dsink[:, 0].reshape(1, N_Q),
        'f_conv_b': jnp.stack([dcb0.reshape(2 * D_FF), dcb1.reshape(2 * D_FF)]),
        'ple_b_gate': jnp.concatenate([dbg0, dbg1], axis=0),
        'a_norm_v': dgv, 'f_conv_w': jnp.stack([dcw0, dcw1], axis=1),
    }
    names_all = _REPL + _SMALL_SHARDED
    full_shapes = [small_g[n].shape for n in names_all]
    small_packed = _pack([small_g[n] for n in names_all], 8)
    ex_small = _send_start([small_packed.reshape(N_DEV, small_packed.shape[0] // N_DEV, ROW)], True, "send_small")
    ex_a_in = _send_start([_wgrad(xa, dz, "wgrad_a_in", tn=2 * AW // N_DEV, out_blocked=True, a_transposed=True,
                                  per_step=4)], True, "send_a_in")
    ex_a_out = _send_start([row_blocks(_wgrad(gated, dhb1, "wgrad_a_out", tn=1024))],
                           True, "send_a_out")

    def as3(a):
        return a.reshape((-1,) + a.shape[-2:])

    def adam(name, parts):
        outs = _adamw_weight(parts, as3(w[name]), as3(mom_m[name]), as3(mom_v[name]), "adamw_" + name)
        return [o.reshape(w[name].shape) for o in outs]

    big = {}
    l_ple1 = _send_wait(ex_ple1, list(range(len(ex_ple1["lands"]))), ex_a_out["token"], "recv_ple1")
    l_ffn1 = _send_wait(ex_ffn1, list(range(len(ex_ffn1["lands"]))), l_ple1[0], "recv_ffn1")
    l_att = _send_wait(ex_att, list(range(len(ex_att["lands"]))), l_ffn1[0], "recv_att")
    big['b_w_o'], big['b_w_q'], big['w_kv'] = adam('b_w_o', [l_att[0]]), adam('b_w_q', [l_att[1]]), adam('w_kv', [l_att[2]])
    l_ple0 = _send_wait(ex_ple0, list(range(len(ex_ple0["lands"]))), big['w_kv'][0], "recv_ple0")
    big['ple_w_in'] = adam('ple_w_in', [l_ple0[0], l_ple1[0]])
    big['ple_w_gate'] = adam('ple_w_gate', [l_ple0[1], l_ple1[1]])
    small_sum = _sum8(_send_wait(ex_small, [0], big['ple_w_gate'][0], "recv_small")[0], "sum_small_grads")
    ex_small_all = _send_start([small_sum], False, "gather_small_sums")
    l_ffn0 = _send_wait(ex_ffn0, list(range(len(ex_ffn0["lands"]))), ex_small_all["token"], "recv_ffn0")
    big['f_w_up'] = adam('f_w_up', [l_ffn0[0], l_ffn1[0]])
    big['f_w_down'] = adam('f_w_down', [l_ffn0[1], l_ffn1[1]])
    big['a_w_in'] = adam('a_w_in', _send_wait(ex_a_in, [0], big['f_w_down'][0], "recv_a_in"))
    big['a_w_out'] = adam('a_w_out', _send_wait(ex_a_out, [0], big['a_w_in'][0], "recv_a_out"))
    g_big, d_big, m_big, v_big = [{n: big[n][i] for n in _BIG} for i in range(4)]

    gsum = _send_wait(ex_small_all, [0], big['a_w_out'][0], "recv_small_sums")[0].reshape(small_packed.shape)
    gs = dict(zip(names_all, _unpack(gsum, full_shapes, 8)))
    gs['a_norm_v'] = lax.dynamic_slice_in_dim(gs['a_norm_v'], me * (AW // N_DEV), AW // N_DEV, axis=1)
    gs['f_conv_w'] = lax.dynamic_index_in_dim(gs['f_conv_w'], me, axis=0, keepdims=False)
    gs = {n: gs[n].reshape(w[n].shape) for n in names_all}
    packs = lambda d: _pack([d[n] for n in names_all], 8)
    souts = _adamw_small(packs(gs), packs(w), packs(mom_m), packs(mom_v), "adamw_small")
    shp = [w[n].shape for n in names_all]
    d_s, m_s, v_s = [dict(zip(names_all, _unpack(o, shp, 8))) for o in souts]

    grads = {**g_big, **gs}
    delta = {**d_big, **d_s}
    new_m = {**m_big, **m_s}
    new_v = {**v_big, **v_s}
    loss = lax.psum(loss_blk[0, 0], ("x", "y", "c"))
    return (loss, dx[None], *[grads[n] for n in _PARAMS], *[delta[n] for n in _PARAMS],
            *[new_m[n] for n in _PARAMS], *[new_v[n] for n in _PARAMS])
```

```python
import functools

import numpy as np
import jax
import jax.numpy as jnp
from jax import lax
from jax.experimental import pallas as pl
from jax.experimental.pallas import tpu as pltpu

F32 = jnp.float32
BF = jnp.bfloat16

N_DEV = 8
D = 1024
AW = 1024
GROUPS = 8
CHUNK = 128
HD = 64
N_Q = 16
N_KV = 4
GQA = 4
BLOCK = 128
D_FF = 2816
FF_BLK = 704
N_FF_BLK = 4
PLE = 256
EPS = 1e-6
LR, B1, B2, ADAM_EPS, WD, STEP = 0.001, 0.9, 0.999, 1e-08, 0.01, 10
GELU_C0 = 0.7978845608028654
GELU_C1 = 0.044715
NEG = -1e30
ROW = 1024
VMEM_BIG = 56 * 1024 * 1024

_PARAMS = ['norm_mix', 'norm_ffn', 'norm_ple', 'norm_kv', 'norm_final', 'a_w_in', 'a_norm_v', 'a_w_s', 'a_b_s',
           'a_w_out', 'w_kv', 'b_w_q', 'b_sinks', 'b_w_o', 'f_w_up', 'f_conv_w', 'f_conv_b', 'f_w_down',
           'ple_w_in', 'ple_w_gate', 'ple_b_gate']
_BIG = ['a_w_in', 'a_w_out', 'w_kv', 'b_w_q', 'b_w_o', 'f_w_up', 'f_w_down', 'ple_w_in', 'ple_w_gate']
_SMALL_SHARDED = ['a_norm_v', 'f_conv_w']
_REPL = ['norm_mix', 'norm_ffn', 'norm_ple', 'norm_kv', 'norm_final', 'a_w_s', 'a_b_s', 'b_sinks', 'f_conv_b',
         'ple_b_gate']


def _mm(a, b):
    return jnp.dot(a, b, preferred_element_type=F32)


def _mm_nt(a, b):
    return lax.dot_general(a, b, (((1,), (1,)), ((), ())), preferred_element_type=F32)


def _mm_tn(a, b):
    return lax.dot_general(a, b, (((0,), (0,)), ((), ())), preferred_element_type=F32)


def _const_spec(shape, single=False):
    nd = len(shape)
    if single:
        return pl.BlockSpec(shape, lambda *_: (0,) * nd, pipeline_mode=pl.Buffered(1))
    return pl.BlockSpec(shape, lambda *_: (0,) * nd)


def _row_spec(tm, n):
    return pl.BlockSpec((tm, n), lambda i: (i, 0))


def _rms_fwd(x, g):
    r = lax.rsqrt(jnp.mean(x * x, axis=-1, keepdims=True) + EPS)
    xhat = x * r
    return xhat * g, xhat, r


def _rms_bwd(dy, xhat, r, g):
    a = dy * g
    dx = r * (a - xhat * jnp.mean(a * xhat, axis=-1, keepdims=True))
    return dx, jnp.sum(dy * xhat, axis=0, keepdims=True)


def _sigmoid(x):
    return 1.0 / (1.0 + jnp.exp(-x))


def _params(vmem=None):
    return pltpu.CompilerParams(dimension_semantics=("arbitrary",), vmem_limit_bytes=vmem)


def _tri_masks():
    t = lax.broadcasted_iota(jnp.int32, (CHUNK, CHUNK), 0)
    s = lax.broadcasted_iota(jnp.int32, (CHUNK, CHUNK), 1)
    return t >= s, s >= t


def _mixa_fwd(x, g, win, gv, ws, bsb, wout, tm):
    S = x.shape[0]
    nc = tm // CHUNK

    def body(x_ref, g_ref, win_ref, gv_ref, ws_ref, bsb_ref, wout_ref, h_ref, zg_ref, gp_ref, gated):
        x = x_ref[...]
        xn, _, _ = _rms_fwd(x, g_ref[...])
        z = _mm(xn.astype(BF), win_ref[...])
        z2 = z * z
        t = jnp.tanh(GELU_C0 * (z + GELU_C1 * z2 * z))
        zg = 0.5 * z * (1.0 + t)
        zg_ref[...] = zg.astype(BF)
        gp_ref[...] = (0.5 * (1.0 + t) + 0.5 * z * (1.0 - t * t) * (GELU_C0 * (1.0 + 3.0 * GELU_C1 * z2))).astype(BF)
        u = zg[:, :AW]
        vn, _, _ = _rms_fwd(zg[:, AW:], gv_ref[...])
        vn = vn.astype(BF)
        tril, _ = _tri_masks()
        for h in range(GROUPS):
            wm = jnp.where(tril, ws_ref[h], jnp.zeros((), BF))
            for c in range(nc):
                rs, cs = slice(c * CHUNK, (c + 1) * CHUNK), slice(h * CHUNK, (h + 1) * CHUNK)
                s = _mm(wm, vn[rs, cs]) + bsb_ref[h]
                gated[rs, cs] = (u[rs, cs] * s).astype(BF)
        h_ref[...] = x + _mm(gated[...], wout_ref[...])

    return pl.pallas_call(
        body, name="mixa_fwd", grid=(S // tm,),
        in_specs=[_row_spec(tm, D), _const_spec((1, D)), _const_spec((D, 2 * AW)), _const_spec((1, AW)),
                  _const_spec((GROUPS, CHUNK, CHUNK)), _const_spec((GROUPS, CHUNK, CHUNK)), _const_spec((AW, D))],
        out_specs=[_row_spec(tm, D), _row_spec(tm, 2 * AW), _row_spec(tm, 2 * AW)],
        out_shape=[jax.ShapeDtypeStruct((S, D), F32), jax.ShapeDtypeStruct((S, 2 * AW), BF),
                   jax.ShapeDtypeStruct((S, 2 * AW), BF)],
        scratch_shapes=[pltpu.VMEM((tm, AW), BF)],
        compiler_params=_params(VMEM_BIG),
    )(x, g, win, gv, ws, bsb, wout)


def _mixa_bwd(dh, x, zg, gp, g, win, gv, ws, wst, bsb, wout, tm):
    S = x.shape[0]
    nt = S // tm
    nc = tm // CHUNK

    def body(dh_ref, x_ref, zg_ref, gp_ref, g_ref, win_ref, gv_ref, ws_ref, wst_ref, bsb_ref, wout_ref,
             dx_ref, xn_ref, dz_ref, gated_ref, dhb_ref, dws_ref, dbs_ref, dgv_ref, dg_ref, du_scr, dvn_scr):
        i = pl.program_id(0)

        @pl.when(i == 0)
        def _():
            dws_ref[...] = jnp.zeros_like(dws_ref)
            dbs_ref[...] = jnp.zeros_like(dbs_ref)
            dgv_ref[...] = jnp.zeros_like(dgv_ref)
            dg_ref[...] = jnp.zeros_like(dg_ref)

        x = x_ref[...]
        gmix = g_ref[...]
        xn, xhat, r = _rms_fwd(x, gmix)
        xn_ref[...] = xn.T.astype(BF)
        zg = zg_ref[...].astype(F32)
        gp = gp_ref[...].astype(F32)
        u = zg[:, :AW]
        gvv = gv_ref[...]
        vn, vhat, rv = _rms_fwd(zg[:, AW:], gvv)
        vn = vn.astype(BF)
        dh = dh_ref[...]
        dhb = dh.astype(BF)
        dhb_ref[...] = dhb
        dgated = _mm_nt(dhb, wout_ref[...])
        tril, triu = _tri_masks()
        for h in range(GROUPS):
            wm = jnp.where(tril, ws_ref[h], jnp.zeros((), BF))
            wmt = jnp.where(triu, wst_ref[h], jnp.zeros((), BF))
            for c in range(nc):
                rs, cs = slice(c * CHUNK, (c + 1) * CHUNK), slice(h * CHUNK, (h + 1) * CHUNK)
                vnb = vn[rs, cs]
                s = _mm(wm, vnb) + bsb_ref[h]
                gated_ref[rs, cs] = (u[rs, cs] * s).astype(BF)
                dgt = dgated[rs, cs]
                du_scr[rs, cs] = dgt * s
                ds = dgt * u[rs, cs]
                dsb = ds.astype(BF)
                dws_ref[h] += jnp.where(tril, _mm_nt(dsb, vnb), 0.0)
                dbs_ref[h] += ds
                dvn_scr[rs, cs] = _mm(wmt, dsb)
        dvn = dvn_scr[...]
        dv, dgv = _rms_bwd(dvn, vhat, rv, gvv)
        dgv_ref[...] += dgv
        dz_ref[:, :AW] = (du_scr[...] * gp[:, :AW]).astype(BF)
        dz_ref[:, AW:] = (dv * gp[:, AW:]).astype(BF)
        dxn = _mm_nt(dz_ref[...], win_ref[...])
        dxr, dg = _rms_bwd(dxn, xhat, r, gmix)
        dg_ref[...] += dg
        dx_ref[...] = dh + dxr

        @pl.when(i == nt - 1)
        def _():
            for h in range(GROUPS):
                dbs_ref[h] = jnp.broadcast_to(jnp.sum(dbs_ref[h], axis=-1, keepdims=True), (CHUNK, CHUNK))

    gshape = (GROUPS, CHUNK, CHUNK)
    return pl.pallas_call(
        body, name="mixa_bwd", grid=(nt,),
        in_specs=[_row_spec(tm, D), _row_spec(tm, D), _row_spec(tm, 2 * AW), _row_spec(tm, 2 * AW), _const_spec((1, D)),
                  _const_spec((D, 2 * AW)), _const_spec((1, AW)), _const_spec(gshape), _const_spec(gshape),
                  _const_spec(gshape), _const_spec((AW, D))],
        out_specs=[_row_spec(tm, D), pl.BlockSpec((D, tm), lambda i: (0, i)), _row_spec(tm, 2 * AW), _row_spec(tm, AW),
                   _row_spec(tm, D), _const_spec(gshape), _const_spec(gshape), _const_spec((1, AW)), _const_spec((1, D))],
        out_shape=[jax.ShapeDtypeStruct((S, D), F32), jax.ShapeDtypeStruct((D, S), BF),
                   jax.ShapeDtypeStruct((S, 2 * AW), BF), jax.ShapeDtypeStruct((S, AW), BF),
                   jax.ShapeDtypeStruct((S, D), BF), jax.ShapeDtypeStruct(gshape, F32),
                   jax.ShapeDtypeStruct(gshape, F32), jax.ShapeDtypeStruct((1, AW), F32),
                   jax.ShapeDtypeStruct((1, D), F32)],
        scratch_shapes=[pltpu.VMEM((tm, AW), F32), pltpu.VMEM((tm, AW), F32)],
        compiler_params=_params(VMEM_BIG),
    )(dh, x, zg, gp, g, win, gv, ws, wst, bsb, wout)


def _shift_down(x, prev8, k):
    n, c = x.shape[0] // 8, x.shape[1]
    r = pltpu.roll(x.reshape(n, 8, c), k, axis=1)
    before = jnp.concatenate([pltpu.roll(prev8.reshape(1, 8, c), k, axis=1), r[:-1]], axis=0)
    sub = lax.broadcasted_iota(jnp.int32, (1, 8, c), 1)
    return jnp.where(sub < k, before, r).reshape(x.shape)


def _shift_up(x, next8, k):
    n, c = x.shape[0] // 8, x.shape[1]
    r = pltpu.roll(x.reshape(n, 8, c), 8 - k, axis=1)
    after = jnp.concatenate([r[1:], pltpu.roll(next8.reshape(1, 8, c), 8 - k, axis=1)], axis=0)
    sub = lax.broadcasted_iota(jnp.int32, (1, 8, c), 1)
    return jnp.where(sub >= 8 - k, after, r).reshape(x.shape)


def _ffn_fwd(h, g, wup, cw, cb, wdown, tm):
    S = h.shape[0]

    def body(h_ref, g_ref, wup_ref, cw_ref, cb_ref, wdown_ref, o_ref, hh_ref, cc_ref, sh):
        i = pl.program_id(0)

        @pl.when(i == 0)
        def _():
            sh[...] = jnp.zeros_like(sh)

        x = h_ref[...]
        xn, _, _ = _rms_fwd(x, g_ref[...])
        xn = xn.astype(BF)
        acc = x
        up = lambda d: [_mm(xn, wup_ref[d]), _mm(xn, wup_ref[N_FF_BLK + d])]
        ahead = up(0)
        down = lambda d, act: _mm(act, wdown_ref[d * FF_BLK:(d + 1) * FF_BLK, :])
        behind = None
        for d in range(N_FF_BLK):
            cs = []
            hbs = ahead
            if d + 1 < N_FF_BLK:
                ahead = up(d + 1)
            if behind is not None:
                acc = acc + down(d - 1, behind)
            for blk, hb in zip((d, N_FF_BLK + d), hbs):
                hh_ref[blk] = hb.astype(BF)
                w, prev8 = cw_ref[blk], sh[blk]
                c = cb_ref[blk] + w[0:1] * _shift_down(hb, prev8, 2) + w[1:2] * _shift_down(hb, prev8, 1) + w[2:3] * hb
                sh[blk] = hb[tm - 8:, :]
                cc_ref[blk] = c.astype(BF)
                cs.append(c)
            behind = (cs[0] * _sigmoid(cs[0]) * cs[1]).astype(BF)
        o_ref[...] = acc + down(N_FF_BLK - 1, behind)

    blk3 = pl.BlockSpec((N_DEV, tm, FF_BLK), lambda i: (0, i, 0))
    return pl.pallas_call(
        body, name="ffn_fwd", grid=(S // tm,),
        in_specs=[_row_spec(tm, D), _const_spec((1, D)), _const_spec((N_DEV, D, FF_BLK), True),
                  _const_spec((N_DEV, 3, FF_BLK)), _const_spec((N_DEV, 1, FF_BLK)), _const_spec((D_FF, D), True)],
        out_specs=[_row_spec(tm, D), blk3, blk3],
        out_shape=[jax.ShapeDtypeStruct((S, D), F32), jax.ShapeDtypeStruct((N_DEV, S, FF_BLK), BF),
                   jax.ShapeDtypeStruct((N_DEV, S, FF_BLK), BF)],
        scratch_shapes=[pltpu.VMEM((N_DEV, 8, FF_BLK), F32)],
        compiler_params=_params(VMEM_BIG),
    )(h, g, wup, cw, cb, wdown)


def _ffn_bwd(dh, h, hh, cc, g, wup, cw, wdown, tm):
    S = h.shape[0]
    nt = S // tm

    def rev(i):
        return (nt - 1 - i, 0)

    def body(dh_ref, h_ref, hh_ref, cc_ref, g_ref, wup_ref, cw_ref, wdown_ref,
             dx_ref, xn_ref, dhh_ref, act_ref, dhb_ref, dcw_ref, dcb_ref, dg_ref, sh):
        i = pl.program_id(0)

        @pl.when(i == 0)
        def _():
            sh[...] = jnp.zeros_like(sh)
            dcw_ref[...] = jnp.zeros_like(dcw_ref)
            dcb_ref[...] = jnp.zeros_like(dcb_ref)
            dg_ref[...] = jnp.zeros_like(dg_ref)

        x = h_ref[...]
        gain = g_ref[...]
        xn, xhat, r = _rms_fwd(x, gain)
        xn_ref[...] = xn.T.astype(BF)
        dh = dh_ref[...]
        dhb = dh.astype(BF)
        dhb_ref[...] = dhb
        dxn = jnp.zeros((tm, D), F32)
        dact_of = lambda d: _mm_nt(dhb, wdown_ref[d * FF_BLK:(d + 1) * FF_BLK, :])
        ahead = dact_of(0)
        for d in range(N_FF_BLK):
            blks = (d, N_FF_BLK + d)
            dact = ahead
            if d + 1 < N_FF_BLK:
                ahead = dact_of(d + 1)
            cg, cu = cc_ref[blks[0]].astype(F32), cc_ref[blks[1]].astype(F32)
            sg = _sigmoid(cg)
            sil = cg * sg
            act_ref[d] = (sil * cu).astype(BF)
            dcs = (dact * cu * (sg + sil - sil * sg), dact * sil)
            for blk, dc in zip(blks, dcs):
                next8 = sh[blk]
                dp1, dp2 = _shift_up(dc, next8, 1), _shift_up(dc, next8, 2)
                hb = hh_ref[blk].astype(F32)
                dcb_ref[blk] += jnp.sum(dc, axis=0, keepdims=True)
                dcw_ref[blk, 0:1, :] += jnp.sum(dp2 * hb, axis=0, keepdims=True)
                dcw_ref[blk, 1:2, :] += jnp.sum(dp1 * hb, axis=0, keepdims=True)
                dcw_ref[blk, 2:3, :] += jnp.sum(dc * hb, axis=0, keepdims=True)
                w = cw_ref[blk]
                dhh = (w[2:3] * dc + w[1:2] * dp1 + w[0:1] * dp2).astype(BF)
                sh[blk] = dc[0:8, :]
                dhh_ref[blk] = dhh
                dxn = dxn + _mm_nt(dhh, wup_ref[blk])
        dxr, dg = _rms_bwd(dxn, xhat, r, gain)
        dg_ref[...] += dg
        dx_ref[...] = dh + dxr

    blk3 = lambda i: (0, nt - 1 - i, 0)
    return pl.pallas_call(
        body, name="ffn_bwd", grid=(nt,),
        in_specs=[pl.BlockSpec((tm, D), rev), pl.BlockSpec((tm, D), rev),
                  pl.BlockSpec((N_DEV, tm, FF_BLK), blk3), pl.BlockSpec((N_DEV, tm, FF_BLK), blk3),
                  _const_spec((1, D)), _const_spec((N_DEV, D, FF_BLK), True), _const_spec((N_DEV, 3, FF_BLK)),
                  _const_spec((D_FF, D), True)],
        out_specs=[pl.BlockSpec((tm, D), rev), pl.BlockSpec((D, tm), lambda i: (0, nt - 1 - i)),
                   pl.BlockSpec((N_DEV, tm, FF_BLK), blk3), pl.BlockSpec((N_FF_BLK, tm, FF_BLK), blk3),
                   pl.BlockSpec((tm, D), rev), _const_spec((N_DEV, 3, FF_BLK)), _const_spec((N_DEV, 1, FF_BLK)),
                   _const_spec((1, D))],
        out_shape=[jax.ShapeDtypeStruct((S, D), F32), jax.ShapeDtypeStruct((D, S), BF),
                   jax.ShapeDtypeStruct((N_DEV, S, FF_BLK), BF), jax.ShapeDtypeStruct((N_FF_BLK, S, FF_BLK), BF),
                   jax.ShapeDtypeStruct((S, D), BF), jax.ShapeDtypeStruct((N_DEV, 3, FF_BLK), F32),
                   jax.ShapeDtypeStruct((N_DEV, 1, FF_BLK), F32), jax.ShapeDtypeStruct((1, D), F32)],
        scratch_shapes=[pltpu.VMEM((N_DEV, 8, FF_BLK), F32)],
        compiler_params=_params(VMEM_BIG),
    )(dh, h, hh, cc, g, wup, cw, wdown)


def _ple_fwd(h, g, wgate, bgate, p, wple, tm):
    S = h.shape[0]

    def body(h_ref, g_ref, wg_ref, bg_ref, p_ref, wp_ref, o_ref):
        x = h_ref[...]
        xn, _, _ = _rms_fwd(x, g_ref[...])
        gate = _sigmoid(_mm(xn.astype(BF), wg_ref[...]) + bg_ref[...])
        o_ref[...] = x + _mm(p_ref[...], wp_ref[...]) * gate

    return pl.pallas_call(
        body, name="ple_fwd", grid=(S // tm,),
        in_specs=[_row_spec(tm, D), _const_spec((1, D)), _const_spec((D, D)), _const_spec((1, D)),
                  _row_spec(tm, PLE), _const_spec((PLE, D))],
        out_specs=_row_spec(tm, D), out_shape=jax.ShapeDtypeStruct((S, D), F32),
        compiler_params=_params(),
    )(h, g, wgate, bgate, p, wple)


def _head_ple_bwd(h, target, gfin, g, wgate, bgate, p, wple, tm):
    S = h.shape[0]

    def body(h_ref, t_ref, gf_ref, g_ref, wg_ref, bg_ref, p_ref, wp_ref,
             dx_ref, de_ref, xn_ref, dpre_ref, dbg_ref, dg_ref, loss_ref, dgf_ref):
        i = pl.program_id(0)

        @pl.when(i == 0)
        def _():
            dbg_ref[...] = jnp.zeros_like(dbg_ref)
            dg_ref[...] = jnp.zeros_like(dg_ref)
            loss_ref[...] = jnp.zeros_like(loss_ref)
            dgf_ref[...] = jnp.zeros_like(dgf_ref)

        x = h_ref[...]
        gain, gfin_ = g_ref[...], gf_ref[...]
        xn, xhat, r = _rms_fwd(x, gain)
        xnb = xn.astype(BF)
        xn_ref[...] = xnb
        gate = _sigmoid(_mm(xnb, wg_ref[...]) + bg_ref[...])
        e = _mm(p_ref[...], wp_ref[...])
        y, yhat, ry = _rms_fwd(x + e * gate, gfin_)
        err = y - t_ref[...]
        tile = 0.5 * jnp.sum(jnp.sum(err * err, axis=-1, keepdims=True) * (1.0 / D), axis=0, keepdims=True)
        loss_ref[...] += jnp.broadcast_to(tile, loss_ref.shape)
        dh, dgf = _rms_bwd(err * (1.0 / D), yhat, ry, gfin_)
        dgf_ref[...] += dgf
        de_ref[...] = (dh * gate).astype(BF)
        dpre = dh * e * gate * (1.0 - gate)
        dpreb = dpre.astype(BF)
        dpre_ref[...] = dpreb
        dbg_ref[...] += jnp.sum(dpre, axis=0, keepdims=True)
        dxr, dg = _rms_bwd(_mm_nt(dpreb, wg_ref[...]), xhat, r, gain)
        dg_ref[...] += dg
        dx_ref[...] = dh + dxr

    vec = _const_spec((1, D))
    return pl.pallas_call(
        body, name="head_ple_bwd", grid=(S // tm,),
        in_specs=[_row_spec(tm, D), _row_spec(tm, D), vec, vec, _const_spec((D, D)), vec, _row_spec(tm, PLE),
                  _const_spec((PLE, D))],
        out_specs=[_row_spec(tm, D), _row_spec(tm, D), _row_spec(tm, D), _row_spec(tm, D), vec, vec,
                   _const_spec((8, 128)), vec],
        out_shape=[jax.ShapeDtypeStruct((S, D), F32), jax.ShapeDtypeStruct((S, D), BF),
                   jax.ShapeDtypeStruct((S, D), BF), jax.ShapeDtypeStruct((S, D), BF),
                   jax.ShapeDtypeStruct((1, D), F32), jax.ShapeDtypeStruct((1, D), F32),
                   jax.ShapeDtypeStruct((8, 128), F32), jax.ShapeDtypeStruct((1, D), F32)],
        compiler_params=_params(VMEM_BIG),
    )(h, target, gfin, g, wgate, bgate, p, wple)


def _qkv_fwd(h, gq, wq, gkv, wkv, tm):
    S = h.shape[0]
    nkv = wkv.shape[1]

    def body(h_ref, gq_ref, wq_ref, gkv_ref, wkv_ref, q_ref, kv_ref):
        x = h_ref[...]
        _, xhat, _ = _rms_fwd(x, gq_ref[...])
        q_ref[...] = _mm((xhat * gq_ref[...]).astype(BF), wq_ref[...]).astype(BF)
        kv_ref[...] = _mm((xhat * gkv_ref[...]).astype(BF), wkv_ref[...]).astype(BF)

    return pl.pallas_call(
        body, name="qkv_fwd", grid=(S // tm,),
        in_specs=[_row_spec(tm, D), _const_spec((1, D)), _const_spec((D, D)), _const_spec((1, D)),
                  _const_spec((D, nkv))],
        out_specs=[_row_spec(tm, D), _row_spec(tm, nkv)],
        out_shape=[jax.ShapeDtypeStruct((S, D), BF), jax.ShapeDtypeStruct((S, nkv), BF)],
        compiler_params=_params(),
    )(h, gq, wq, gkv, wkv)


def _qkv_ple_bwd(dh, h, dq, dkv, gq, wq, gkv, wkv, h_in, g, wgate, bgate, p, wple, tm):
    S = h.shape[0]
    nkv = wkv.shape[1]

    def body(dh_ref, h_ref, dq_ref, dkv_ref, gq_ref, wq_ref, gkv_ref, wkv_ref, hi_ref, g_ref, wg_ref, bg_ref, p_ref,
             wp_ref, dx_ref, xq_ref, xkv_ref, dgq_ref, dgkv_ref, de_ref, xn_ref, dpre_ref, dbg_ref, dg_ref):
        i = pl.program_id(0)

        @pl.when(i == 0)
        def _():
            for ref in (dgq_ref, dgkv_ref, dbg_ref, dg_ref):
                ref[...] = jnp.zeros_like(ref)

        gq_, gkv_ = gq_ref[...], gkv_ref[...]
        _, hhat, rh = _rms_fwd(h_ref[...], gq_)
        xq_ref[...] = (hhat * gq_).astype(BF)
        xkv_ref[...] = (hhat * gkv_).astype(BF)
        d1, dg1 = _rms_bwd(_mm_nt(dq_ref[...], wq_ref[...]), hhat, rh, gq_)
        d2, dg2 = _rms_bwd(_mm_nt(dkv_ref[...], wkv_ref[...]), hhat, rh, gkv_)
        dgq_ref[...] += dg1
        dgkv_ref[...] += dg2
        dh = dh_ref[...] + d1 + d2

        x = hi_ref[...]
        gain = g_ref[...]
        xn, xhat, r = _rms_fwd(x, gain)
        xnb = xn.astype(BF)
        xn_ref[...] = xnb
        gate = _sigmoid(_mm(xnb, wg_ref[...]) + bg_ref[...])
        e = _mm(p_ref[...], wp_ref[...])
        de_ref[...] = (dh * gate).astype(BF)
        dpre = dh * e * gate * (1.0 - gate)
        dpreb = dpre.astype(BF)
        dpre_ref[...] = dpreb
        dbg_ref[...] += jnp.sum(dpre, axis=0, keepdims=True)
        dxr, dg = _rms_bwd(_mm_nt(dpreb, wg_ref[...]), xhat, r, gain)
        dg_ref[...] += dg
        dx_ref[...] = dh + dxr

    vec, rows = _const_spec((1, D)), _row_spec(tm, D)
    return pl.pallas_call(
        body, name="qkv_ple_bwd", grid=(S // tm,),
        in_specs=[rows, rows, rows, _row_spec(tm, nkv), vec, _const_spec((D, D)), vec, _const_spec((D, nkv)),
                  rows, vec, _const_spec((D, D)), vec, _row_spec(tm, PLE), _const_spec((PLE, D))],
        out_specs=[rows, rows, rows, vec, vec, rows, rows, rows, vec, vec],
        out_shape=[jax.ShapeDtypeStruct((S, D), F32), jax.ShapeDtypeStruct((S, D), BF), jax.ShapeDtypeStruct((S, D), BF),
                   jax.ShapeDtypeStruct((1, D), F32), jax.ShapeDtypeStruct((1, D), F32),
                   jax.ShapeDtypeStruct((S, D), BF), jax.ShapeDtypeStruct((S, D), BF), jax.ShapeDtypeStruct((S, D), BF),
                   jax.ShapeDtypeStruct((1, D), F32), jax.ShapeDtypeStruct((1, D), F32)],
        compiler_params=_params(VMEM_BIG),
    )(dh, h, dq, dkv, gq, wq, gkv, wkv, h_in, g, wgate, bgate, p, wple)


def _oproj_fwd(a, w, res, tm):
    S = a.shape[0]

    def body(a_ref, w_ref, r_ref, o_ref):
        o_ref[...] = r_ref[...] + _mm(a_ref[...], w_ref[...])

    return pl.pallas_call(
        body, name="oproj_fwd", grid=(S // tm,),
        in_specs=[_row_spec(tm, D), _const_spec((D, D)), _row_spec(tm, D)],
        out_specs=_row_spec(tm, D), out_shape=jax.ShapeDtypeStruct((S, D), F32),
        compiler_params=_params(),
    )(a, w, res)


def _oproj_bwd(dh, w, tm):
    S = dh.shape[0]

    def body(dh_ref, w_ref, da_ref, dhb_ref):
        dhb = dh_ref[...].astype(BF)
        dhb_ref[...] = dhb
        da_ref[...] = _mm_nt(dhb, w_ref[...]).astype(BF)

    return pl.pallas_call(
        body, name="oproj_bwd", grid=(S // tm,),
        in_specs=[_row_spec(tm, D), _const_spec((D, D))],
        out_specs=[_row_spec(tm, D), _row_spec(tm, D)],
        out_shape=[jax.ShapeDtypeStruct((S, D), BF), jax.ShapeDtypeStruct((S, D), BF)],
        compiler_params=_params(),
    )(dh, w)


def _alibi_slope(hq):
    return float(np.float32(2.0 ** (-8.0 * (hq + 1) / N_Q)))


def _attn_consts():
    W = N_KV * HD
    r = np.zeros((W, N_KV * W), np.float32)
    for kh in range(N_KV):
        for c in range(W):
            r[kh * HD + c % HD, kh * W + c] = 1.0
    qi = np.arange(BLOCK)[:, None]
    kj = np.arange(2 * BLOCK)[None, :]
    dist = qi + BLOCK - kj
    band = (dist >= 0) & (dist < BLOCK)
    bias = np.full((2, N_KV, GQA * BLOCK, 2 * BLOCK), NEG, np.float32)
    for later in (0, 1):
        valid = band & ((kj >= BLOCK) | bool(later))
        for kh in range(N_KV):
            for g in range(GQA):
                slope = np.float32(_alibi_slope(kh * GQA + g))
                bias[later, kh, g * BLOCK:(g + 1) * BLOCK] = np.where(valid, -slope * dist.astype(np.float32), NEG)
    lane_head = np.arange(W)[None, :] // HD
    maskq = (np.arange(GQA * BLOCK)[:, None] // BLOCK == lane_head).astype(np.float32)
    maskk = (np.arange(GQA * 2 * BLOCK)[:, None] // (2 * BLOCK) == lane_head).astype(np.float32)
    head_sum = (lane_head.T == np.arange(GQA * BLOCK)[None, :] // BLOCK).astype(np.float32)
    return dict(rep=jnp.asarray(r, BF), bias=jnp.asarray(bias), maskq=jnp.asarray(maskq, BF),
                maskqs=jnp.asarray(maskq * HD ** -0.5, BF),
                maskk=jnp.asarray(maskk, BF), head_sum=jnp.asarray(head_sum, BF))


def _sink_col(sink_ref, kh):
    rb = lax.broadcasted_iota(jnp.int32, (GQA * BLOCK, 1), 0) >> 7
    col = jnp.full((GQA * BLOCK, 1), sink_ref[kh * GQA + GQA - 1], F32)
    for g in range(GQA - 1):
        col = jnp.where(rb == g, sink_ref[kh * GQA + g], col)
    return col


def _stack4(a):
    return jnp.concatenate([a] * GQA, axis=0)


def _rows_to_lanes(a):
    return jnp.concatenate([a[g * BLOCK:(g + 1) * BLOCK] for g in range(GQA)], axis=1)


def _lanes_to_rows(a, n):
    return jnp.concatenate([a[:, g * n:(g + 1) * n] for g in range(GQA)], axis=0)


def _scores(qg, k4, bias, sink_col, maskqs):
    qs = _stack4(qg) * maskqs
    s = _mm_nt(qs, k4) + bias
    m = jnp.maximum(jnp.max(s, axis=-1, keepdims=True), sink_col)
    return qs, jnp.exp(s - m), jnp.exp(sink_col - m)


def _bands(kvc_ref, rep_ref, band):
    W = N_KV * HD
    kvc = kvc_ref[...]
    band[0, BLOCK:, :] = _mm(kvc[:, :W], rep_ref[...]).astype(BF)
    band[1, BLOCK:, :] = _mm(kvc[:, W:], rep_ref[...]).astype(BF)
    return band[0], band[1]


def _bands_next(band):
    band[:, :BLOCK, :] = band[:, BLOCK:, :]


def _attn_fwd(q, kv, sinks, consts):
    S = q.shape[0]
    nb = S // BLOCK
    W = N_KV * HD

    def body(sink_ref, q_ref, kvc_ref, rep_ref, bias_ref, maskqs_ref, maskk_ref, o_ref, band):
        @pl.when(pl.program_id(0) == 0)
        def _():
            band[...] = jnp.zeros_like(band)

        maskqs, maskk = maskqs_ref[...], maskk_ref[...]
        lane_head = lax.broadcasted_iota(jnp.int32, (BLOCK, W), 1) >> 6
        k4all, v4all = _bands(kvc_ref, rep_ref, band)
        for kh in range(N_KV):
            cols = slice(kh * W, (kh + 1) * W)
            _, e, es = _scores(q_ref[:, cols], k4all[:, cols], bias_ref[kh], _sink_col(sink_ref, kh), maskqs)
            vbd = jnp.concatenate([_stack4(v4all[:, cols]) * maskk, maskk], axis=1)
            nd = _mm(_rows_to_lanes(e.astype(BF)), vbd)
            es_l = jnp.broadcast_to(es[(GQA - 1) * BLOCK:], (BLOCK, W))
            for g in range(GQA - 1):
                es_l = jnp.where(lane_head == g, es[g * BLOCK:(g + 1) * BLOCK], es_l)
            o_ref[:, cols] = (nd[:, :W] / (nd[:, W:] + es_l)).astype(BF)
        _bands_next(band)

    return pl.pallas_call(
        body, name="attn_fwd", grid=(nb,),
        in_specs=[pl.BlockSpec(memory_space=pltpu.SMEM), _row_spec(BLOCK, D), _row_spec(BLOCK, 2 * W),
                  _const_spec((W, N_KV * W)),
                  pl.BlockSpec((None, N_KV, GQA * BLOCK, 2 * BLOCK), lambda i: (jnp.minimum(i, 1), 0, 0, 0)),
                  _const_spec((GQA * BLOCK, W)), _const_spec((GQA * 2 * BLOCK, W))],
        out_specs=_row_spec(BLOCK, D), out_shape=jax.ShapeDtypeStruct((S, D), BF),
        scratch_shapes=[pltpu.VMEM((2, 2 * BLOCK, N_KV * W), BF)],
        compiler_params=_params(),
    )(sinks, q, kv, consts["rep"], consts["bias"], consts["maskqs"], consts["maskk"])


def _attn_bwd(q, kv, att, do, sinks, consts):
    S = q.shape[0]
    nb = S // BLOCK
    W = N_KV * HD

    def qmap(i):
        return (jnp.minimum(i, nb - 1), 0)

    def body(sink_ref, q_ref, kvc_ref, att_ref, do_ref, rep_ref, bias_ref, maskq_ref, maskqs_ref, maskk_ref,
             hsum_ref, dq_ref, dkv_ref, dsink_ref, carry, sink_acc, band):
        i = pl.program_id(0)

        @pl.when(i == 0)
        def _():
            sink_acc[...] = jnp.zeros_like(sink_acc)
            carry[...] = jnp.zeros_like(carry)
            band[...] = jnp.zeros_like(band)

        @pl.when(i < nb)
        def _():
            maskq, maskqs, maskk, head_sum = maskq_ref[...], maskqs_ref[...], maskk_ref[...], hsum_ref[...]
            ones_k = jnp.ones((2 * BLOCK, BLOCK), BF)
            k4all, v4all = _bands(kvc_ref, rep_ref, band)
            dk4s, dv4s = [], []
            for kh in range(N_KV):
                cols = slice(kh * W, (kh + 1) * W)
                k4, v4 = k4all[:, cols], v4all[:, cols]
                qs, e, es = _scores(q_ref[:, cols], k4, bias_ref[kh], _sink_col(sink_ref, kh), maskqs)
                inv = 1.0 / (_mm(e.astype(BF), ones_k) + es)
                p = e * jnp.concatenate([inv, inv], axis=1)
                dog = do_ref[:, cols]
                dos = _stack4(dog) * maskq
                dp = _mm_nt(dos, v4)
                prod = (dog.astype(F32) * att_ref[:, cols].astype(F32)).astype(BF)
                delta = _lanes_to_rows(_mm(prod, head_sum), BLOCK)
                ds = (p * (dp - jnp.concatenate([delta, delta], axis=1))).astype(BF)
                sink_acc[kh] += es * inv * delta
                kbd = _stack4(k4) * maskk
                dq_ref[:, cols] = (_mm(_rows_to_lanes(ds), kbd) * (HD ** -0.5)).astype(BF)
                dk4s.append(_mm_tn(ds, qs).astype(BF))
                dv4s.append(_mm_tn(p.astype(BF), dos).astype(BF))
            dkband = _mm_nt(jnp.concatenate(dk4s, axis=1), rep_ref[...])
            dvband = _mm_nt(jnp.concatenate(dv4s, axis=1), rep_ref[...])
            dkv_ref[:, :W] = (carry[:, :W] + dkband[:BLOCK]).astype(BF)
            dkv_ref[:, W:] = (carry[:, W:] + dvband[:BLOCK]).astype(BF)
            carry[:, :W] = dkband[BLOCK:]
            carry[:, W:] = dvband[BLOCK:]
            _bands_next(band)

        @pl.when(i == nb)
        def _():
            dkv_ref[...] = carry[...].astype(BF)
            for hq in range(N_Q):
                kh, g = divmod(hq, GQA)
                dsink_ref[hq:hq + 1, :] = -jnp.sum(sink_acc[kh, g * BLOCK:(g + 1) * BLOCK, :], axis=0, keepdims=True)

    return pl.pallas_call(
        body, name="attn_bwd", grid=(nb + 1,),
        in_specs=[pl.BlockSpec(memory_space=pltpu.SMEM), pl.BlockSpec((BLOCK, D), qmap),
                  pl.BlockSpec((BLOCK, 2 * W), qmap),
                  pl.BlockSpec((BLOCK, D), qmap), pl.BlockSpec((BLOCK, D), qmap), _const_spec((W, N_KV * W)),
                  pl.BlockSpec((None, N_KV, GQA * BLOCK, 2 * BLOCK), lambda i: (jnp.minimum(i, 1), 0, 0, 0)),
                  _const_spec((GQA * BLOCK, W)), _const_spec((GQA * BLOCK, W)), _const_spec((GQA * 2 * BLOCK, W)),
                  _const_spec((W, GQA * BLOCK))],
        out_specs=[pl.BlockSpec((BLOCK, D), qmap),
                   pl.BlockSpec((BLOCK, 2 * W), lambda i: (jnp.maximum(i - 1, 0), 0)),
                   _const_spec((N_Q, BLOCK))],
        out_shape=[jax.ShapeDtypeStruct((S, D), BF), jax.ShapeDtypeStruct((S, 2 * W), BF),
                   jax.ShapeDtypeStruct((N_Q, BLOCK), F32)],
        scratch_shapes=[pltpu.VMEM((BLOCK, 2 * W), F32), pltpu.VMEM((N_KV, GQA * BLOCK, BLOCK), F32),
                        pltpu.VMEM((2, 2 * BLOCK, N_KV * W), BF)],
        compiler_params=_params(),
    )(sinks, q, kv, att, do, consts["rep"], consts["bias"], consts["maskq"], consts["maskqs"], consts["maskk"], consts["head_sum"])


def _wgrad(a, b, name, *, tn, out_blocked=False, a_transposed=False, ts=2048, per_step=1):
    a_blocked, b_blocked = a.ndim == 3, b.ndim == 3
    na = a.shape[0] if a_blocked else 1
    S = b.shape[-2]
    K = a.shape[0] if a_transposed else a.shape[-1]
    wide = tn * per_step
    nbk = b.shape[0] if b_blocked else b.shape[1] // wide
    ts = min(ts, S)
    ns = S // ts
    assert not (a_blocked and (out_blocked or a_transposed)) and (per_step == 1 or (out_blocked and not b_blocked))

    def body(a_ref, b_ref, o_ref, acc):
        s = pl.program_id(2)

        @pl.when(s == 0)
        def _():
            acc[...] = jnp.zeros_like(acc)

        if a_transposed:
            acc[...] += _mm(a_ref[...], b_ref[...])
        else:
            acc[...] += _mm_tn(a_ref[...], b_ref[...])

        @pl.when(s == ns - 1)
        def _():
            if per_step == 1:
                o_ref[...] = acc[...].astype(o_ref.dtype)
            else:
                for k in range(per_step):
                    o_ref[k] = acc[:, k * tn:(k + 1) * tn].astype(o_ref.dtype)

    if a_transposed:
        a_spec = pl.BlockSpec((K, ts), lambda i, j, s: (0, s))
    elif a_blocked:
        a_spec = pl.BlockSpec((None, ts, K), lambda i, j, s: (i, s, 0))
    else:
        a_spec = pl.BlockSpec((ts, K), lambda i, j, s: (s, 0))
    b_spec = (pl.BlockSpec((None, ts, tn), lambda i, j, s: (j, s, 0)) if b_blocked
              else pl.BlockSpec((ts, wide), lambda i, j, s: (s, j)))
    if out_blocked:
        o_spec = pl.BlockSpec((None, K, tn) if per_step == 1 else (per_step, K, tn), lambda i, j, s: (j, 0, 0))
        o_shape = jax.ShapeDtypeStruct((nbk * per_step, K, tn), BF)
    else:
        o_spec = pl.BlockSpec((K, wide), lambda i, j, s: (i, j))
        o_shape = jax.ShapeDtypeStruct((na * K, nbk * wide), BF)
    return pl.pallas_call(
        body, name=name, grid=(na, nbk, ns), in_specs=[a_spec, b_spec], out_specs=o_spec, out_shape=o_shape,
        scratch_shapes=[pltpu.VMEM((K, wide), F32)],
        compiler_params=pltpu.CompilerParams(dimension_semantics=("arbitrary", "arbitrary", "arbitrary"),
                                             vmem_limit_bytes=VMEM_BIG),
    )(a, b)


def _adamw_math(w, g, m, v):
    m = B1 * m + (1.0 - B1) * g
    v = B2 * v + (1.0 - B2) * (g * g)
    m_hat = m / (1.0 - B1 ** STEP)
    v_hat = v / (1.0 - B2 ** STEP)
    return -LR * (m_hat / (jnp.sqrt(v_hat) + ADAM_EPS) + WD * w), m, v


def _sum8(parts, name):
    R = parts.shape[1]

    def body(p_ref, o_ref):
        g = p_ref[0]
        for s in range(1, N_DEV):
            g = g + p_ref[s]
        o_ref[...] = g

    return pl.pallas_call(
        body, name=name, grid=(1,),
        in_specs=[pl.BlockSpec((N_DEV, R, ROW), lambda i: (0, 0, 0))],
        out_specs=pl.BlockSpec((R, ROW), lambda i: (0, 0)), out_shape=jax.ShapeDtypeStruct((R, ROW), F32),
        compiler_params=_params(),
    )(parts)


def _adamw_small(g, w, m, v, name):
    R = w.shape[0]

    def body(g_ref, w_ref, m_ref, v_ref, d_ref, nm_ref, nv_ref):
        d_ref[...], nm_ref[...], nv_ref[...] = _adamw_math(w_ref[...], g_ref[...], m_ref[...], v_ref[...])

    spec = pl.BlockSpec((R, ROW), lambda i: (0, 0))
    return pl.pallas_call(
        body, name=name, grid=(1,), in_specs=[spec] * 4, out_specs=[spec] * 3,
        out_shape=[jax.ShapeDtypeStruct((R, ROW), F32)] * 3, compiler_params=_params(),
    )(g, w, m, v)


def _mesh_pos():
    return lax.axis_index("x"), lax.axis_index("y"), lax.axis_index("c")


def _flip(v, bit):
    return 1 - v if bit else v


def _peers(x, y, c):
    out = []
    for k in range(1, N_DEV):
        px, py, pc = _flip(x, k & 4), _flip(y, k & 2), _flip(c, k & 1)
        out.append((k - 1, (px, py, pc), 4 * px + 2 * py + pc))
    return out


def _all_gather_now(xs, name):
    n = len(xs)
    MESH = pl.DeviceIdType.MESH

    def body(*refs):
        x_refs, out_refs = refs[:n], refs[n:2 * n]
        send_sems, recv_sems, local_sems = refs[2 * n:]
        x, y, c = _mesh_pos()
        me, sibling = (x, y, c), (x, y, 1 - c)
        chips = [(1 - x, y), (x, 1 - y), (1 - x, 1 - y)]
        blk = lambda p: 4 * p[0] + 2 * p[1] + p[2]

        def copy(a, k, block, to, src=None):
            return pltpu.make_async_remote_copy(
                src_ref=out_refs[a].at[blk(block)] if src is None else src, dst_ref=out_refs[a].at[blk(block)],
                send_sem=send_sems.at[a, k], recv_sem=recv_sems.at[a, k], device_id=to, device_id_type=MESH)

        sent, own = [], []
        for a in range(n):
            own.append(pltpu.make_async_copy(x_refs[a], out_refs[a].at[blk(me)], local_sems.at[a]))
            own[-1].start()
            first = [copy(a, 1 + j, me, (*chip, c), src=x_refs[a]) for j, chip in enumerate(chips)]
            first.append(copy(a, 0, me, sibling, src=x_refs[a]))
            for cp in first:
                cp.start()
            sent += first
        for a in range(n):
            for j, chip in enumerate(chips):
                copy(a, 1 + j, (*chip, c), me).wait_recv()
                sent.append(copy(a, 4 + j, (*chip, c), sibling))
                sent[-1].start()
        for a in range(n):
            copy(a, 0, sibling, me).wait_recv()
            for j, chip in enumerate(chips):
                copy(a, 4 + j, (*chip, 1 - c), me).wait_recv()
        for cp in sent:
            cp.wait_send()
        for cp in own:
            cp.wait()

    hbm = pl.BlockSpec(memory_space=pl.ANY)
    return pl.pallas_call(
        body, name=name, out_shape=[jax.ShapeDtypeStruct((N_DEV,) + x.shape, x.dtype) for x in xs],
        in_specs=[hbm] * n, out_specs=[hbm] * n,
        scratch_shapes=[pltpu.SemaphoreType.DMA((n, 7)), pltpu.SemaphoreType.DMA((n, 7)), pltpu.SemaphoreType.DMA((n,))],
    )(*xs)


def _chip_peers(x, y, c):
    return (x, y, c), (x, y, 1 - c), [(1 - x, y, c), (x, 1 - y, c), (1 - x, 1 - y, c)]


def _blk(p):
    return 4 * p[0] + 2 * p[1] + p[2]


def _gather2_start(srcs, name):
    n = len(srcs)
    MESH = pl.DeviceIdType.MESH

    def body(*refs):
        src_refs, land_refs = refs[:n], refs[n:2 * n]
        send, recv_ici, recv_sib, local = (refs[(2 + k) * n:(3 + k) * n] for k in range(4))
        token = refs[8 * n]
        me, sibling, chips = _chip_peers(*_mesh_pos())
        for a in range(n):
            dst = land_refs[a].at[_blk(me)]
            pltpu.make_async_remote_copy(src_ref=src_refs[a], dst_ref=dst, send_sem=send[a].at[0], recv_sem=recv_sib[a],
                                         device_id=sibling, device_id_type=MESH).start()
            for j, chip in enumerate(chips):
                pltpu.make_async_remote_copy(src_ref=src_refs[a], dst_ref=dst, send_sem=send[a].at[1 + j],
                                             recv_sem=recv_ici[a].at[j], device_id=chip, device_id_type=MESH).start()
            pltpu.make_async_copy(src_refs[a], dst, local[a]).start()
        token[...] = jnp.zeros_like(token)

    hbm = pl.BlockSpec(memory_space=pltpu.HBM)
    sem = pl.BlockSpec(memory_space=pltpu.SEMAPHORE)
    dma = pltpu.SemaphoreType.DMA
    outs = pl.pallas_call(
        body, name=name,
        out_shape=([dma((4,))] * n + [dma((3,))] * n + [dma(())] * n + [dma(())] * n
                   + [pltpu.HBM(s.shape, s.dtype) for s in srcs]
                   + [pltpu.HBM((N_DEV,) + s.shape, s.dtype) for s in srcs] + [jax.ShapeDtypeStruct((8, 128), F32)]),
        in_specs=[hbm] * (2 * n), out_specs=[sem] * (4 * n) + [hbm] * (2 * n) + [pl.BlockSpec(memory_space=pltpu.VMEM)],
        input_output_aliases={**{a: 4 * n + a for a in range(n)}, **{n + a: 5 * n + a for a in range(n)}},
        compiler_params=pltpu.CompilerParams(has_side_effects=pltpu.SideEffectType.DATAFLOW_SIDE_EFFECTING),
    )(*[pltpu.with_memory_space_constraint(s, pltpu.HBM) for s in srcs],
      *[pltpu.with_memory_space_constraint(lax.empty((N_DEV,) + s.shape, s.dtype), pltpu.HBM) for s in srcs])
    keys = ["send", "recv_ici", "recv_sib", "local", "srcs", "lands"]
    return {**{k: outs[i * n:(i + 1) * n] for i, k in enumerate(keys)}, "token": outs[6 * n]}


def _gather2_forward(started, after, name):
    lands, recv_ici = started["lands"], started["recv_ici"]
    n = len(lands)
    MESH = pl.DeviceIdType.MESH

    def body(*refs):
        land_refs, recv_refs = refs[:n], refs[n:2 * n]
        fsend, frecv = refs[2 * n + 1 + n:2 * n + 1 + 2 * n], refs[2 * n + 1 + 2 * n:2 * n + 1 + 3 * n]
        me, sibling, chips = _chip_peers(*_mesh_pos())
        for a in range(n):
            for j, chip in enumerate(chips):
                rows = land_refs[a].at[_blk(chip)]
                pltpu.make_async_remote_copy(src_ref=rows, dst_ref=rows, send_sem=fsend[a].at[j], recv_sem=recv_refs[a].at[j],
                                             device_id=chip, device_id_type=MESH).wait_recv()
                pltpu.make_async_remote_copy(src_ref=rows, dst_ref=rows, send_sem=fsend[a].at[j], recv_sem=frecv[a].at[j],
                                             device_id=sibling, device_id_type=MESH).start()

    hbm = pl.BlockSpec(memory_space=pltpu.HBM)
    sem = pl.BlockSpec(memory_space=pltpu.SEMAPHORE)
    dma = pltpu.SemaphoreType.DMA
    outs = pl.pallas_call(
        body, name=name,
        out_shape=[pltpu.HBM(l.shape, l.dtype) for l in lands] + [dma((3,))] * (2 * n),
        in_specs=[hbm] * n + [sem] * n + [pl.BlockSpec(memory_space=pl.ANY)], out_specs=[hbm] * n + [sem] * (2 * n),
        input_output_aliases={a: a for a in range(n)},
        compiler_params=pltpu.CompilerParams(has_side_effects=pltpu.SideEffectType.DATAFLOW_SIDE_EFFECTING),
    )(*lands, *recv_ici, after)
    return {**started, "lands": outs[:n], "fsend": outs[n:2 * n], "frecv": outs[2 * n:3 * n]}


def _gather2_wait(st, after, name):
    n = len(st["lands"])
    MESH = pl.DeviceIdType.MESH

    def body(*refs):
        src_refs, land_refs = refs[:n], refs[n:2 * n]
        send, recv_sib, local, fsend, frecv = (refs[(2 + k) * n:(3 + k) * n] for k in range(5))
        me, sibling, chips = _chip_peers(*_mesh_pos())
        for a in range(n):
            def desc(src, dst, s, r):
                return pltpu.make_async_remote_copy(src_ref=src, dst_ref=dst, send_sem=s, recv_sem=r, device_id=sibling,
                                                    device_id_type=MESH)
            own = land_refs[a].at[_blk(me)]
            for k in range(4):
                desc(src_refs[a], own, send[a].at[k], recv_sib[a]).wait_send()
            desc(src_refs[a], land_refs[a].at[_blk(sibling)], send[a].at[0], recv_sib[a]).wait_recv()
            pltpu.make_async_copy(src_refs[a], own, local[a]).wait()
            for j, chip in enumerate(chips):
                mine, theirs = land_refs[a].at[_blk(chip)], land_refs[a].at[_blk((chip[0], chip[1], 1 - chip[2]))]
                desc(mine, mine, fsend[a].at[j], frecv[a].at[j]).wait_send()
                desc(theirs, theirs, fsend[a].at[j], frecv[a].at[j]).wait_recv()

    hbm = pl.BlockSpec(memory_space=pltpu.HBM)
    sem = pl.BlockSpec(memory_space=pltpu.SEMAPHORE)
    srcs, lands = st["srcs"], st["lands"]
    outs = pl.pallas_call(
        body, name=name, out_shape=[pltpu.HBM(s.shape, s.dtype) for s in list(srcs) + list(lands)],
        in_specs=[hbm] * (2 * n) + [sem] * (5 * n) + [pl.BlockSpec(memory_space=pl.ANY)], out_specs=[hbm] * (2 * n),
        input_output_aliases={a: a for a in range(2 * n)},
        compiler_params=pltpu.CompilerParams(has_side_effects=pltpu.SideEffectType.DATAFLOW_SIDE_EFFECTING),
    )(*srcs, *lands, *st["send"], *st["recv_sib"], *st["local"], *st["fsend"], *st["frecv"], after)
    return outs[n:]


def _send_start(srcs, blocked, name):
    n = len(srcs)
    MESH = pl.DeviceIdType.MESH

    def body(*refs):
        src_refs, land_refs = refs[:n], refs[n:2 * n]
        send_sems, recv_sems, local_sems = refs[2 * n:3 * n], refs[3 * n:4 * n], refs[4 * n:5 * n]
        token = refs[7 * n]
        x, y, c = _mesh_pos()
        me = 4 * x + 2 * y + c
        for a in range(n):
            for k, peer, blk in _peers(x, y, c):
                pltpu.make_async_remote_copy(
                    src_ref=src_refs[a].at[blk] if blocked else src_refs[a], dst_ref=land_refs[a].at[me],
                    send_sem=send_sems[a].at[k], recv_sem=recv_sems[a].at[k], device_id=peer,
                    device_id_type=MESH).start()
            pltpu.make_async_copy(src_refs[a].at[me] if blocked else src_refs[a], land_refs[a].at[me],
                                  local_sems[a]).start()
        token[...] = jnp.zeros_like(token)

    hbm = pl.BlockSpec(memory_space=pltpu.HBM)
    sem = pl.BlockSpec(memory_space=pltpu.SEMAPHORE)
    lshape = [s.shape if blocked else (N_DEV,) + s.shape for s in srcs]
    outs = pl.pallas_call(
        body, name=name,
        out_shape=([pltpu.SemaphoreType.DMA((N_DEV - 1,))] * (2 * n) + [pltpu.SemaphoreType.DMA(())] * n
                   + [pltpu.HBM(s.shape, s.dtype) for s in srcs]
                   + [pltpu.HBM(ls, s.dtype) for ls, s in zip(lshape, srcs)] + [jax.ShapeDtypeStruct((8, 128), F32)]),
        in_specs=[hbm] * (2 * n), out_specs=[sem] * (3 * n) + [hbm] * (2 * n) + [pl.BlockSpec(memory_space=pltpu.VMEM)],
        input_output_aliases={**{a: 3 * n + a for a in range(n)}, **{n + a: 4 * n + a for a in range(n)}},
        compiler_params=pltpu.CompilerParams(has_side_effects=pltpu.SideEffectType.DATAFLOW_SIDE_EFFECTING),
    )(*[pltpu.with_memory_space_constraint(s, pltpu.HBM) for s in srcs],
      *[pltpu.with_memory_space_constraint(lax.empty(ls, s.dtype), pltpu.HBM) for ls, s in zip(lshape, srcs)])
    return dict(send=outs[:n], recv=outs[n:2 * n], local=outs[2 * n:3 * n], srcs=outs[3 * n:4 * n],
                lands=outs[4 * n:5 * n], token=outs[5 * n], blocked=blocked)


def _send_wait(started, which, after, name):
    n = len(which)
    blocked = started["blocked"]
    pick = lambda key: [started[key][a] for a in which]
    MESH = pl.DeviceIdType.MESH

    def body(*refs):
        src_refs, land_refs = refs[:n], refs[n:2 * n]
        send, recv, local = refs[2 * n:3 * n], refs[3 * n:4 * n], refs[4 * n:5 * n]
        x, y, c = _mesh_pos()
        me = 4 * x + 2 * y + c
        for a in range(n):
            for k, peer, blk in _peers(x, y, c):
                cp = pltpu.make_async_remote_copy(
                    src_ref=src_refs[a].at[blk] if blocked else src_refs[a], dst_ref=land_refs[a].at[blk],
                    send_sem=send[a].at[k], recv_sem=recv[a].at[k], device_id=peer, device_id_type=MESH)
                cp.wait_send()
                cp.wait_recv()
            pltpu.make_async_copy(src_refs[a].at[me] if blocked else src_refs[a], land_refs[a].at[me],
                                  local[a]).wait()

    hbm = pl.BlockSpec(memory_space=pltpu.HBM)
    sem = pl.BlockSpec(memory_space=pltpu.SEMAPHORE)
    srcs, lands = pick("srcs"), pick("lands")
    outs = pl.pallas_call(
        body, name=name, out_shape=[pltpu.HBM(s.shape, s.dtype) for s in srcs + lands],
        in_specs=[hbm] * (2 * n) + [sem] * (3 * n) + [pl.BlockSpec(memory_space=pl.ANY)], out_specs=[hbm] * (2 * n),
        input_output_aliases={a: a for a in range(2 * n)},
        compiler_params=pltpu.CompilerParams(has_side_effects=pltpu.SideEffectType.DATAFLOW_SIDE_EFFECTING),
    )(*srcs, *lands, *pick("send"), *pick("recv"), *pick("local"), after)
    return outs[n:]


def _adamw_weight(parts, w, m, v, name):
    L, R, C = w.shape
    tr = max(t for t in range(16, min(R, 256) + 1, 16) if R % t == 0)
    nr = R // tr

    def body(*refs):
        p_refs = refs[:L]
        w_ref, m_ref, v_ref, g_ref, d_ref, nm_ref, nv_ref, gsum = refs[L:]
        layer = pl.program_id(0)
        for l in range(L):
            @pl.when(layer == l)
            def _(l=l):
                g = p_refs[l][0].astype(F32)
                for s in range(1, N_DEV):
                    g = g + p_refs[l][s].astype(F32)
                gsum[...] = g
        g = gsum[...]
        g_ref[...] = g
        d_ref[...], nm_ref[...], nv_ref[...] = _adamw_math(w_ref[...], g, m_ref[...], v_ref[...])

    def part_spec(l):
        return pl.BlockSpec((N_DEV, tr, C), lambda layer, i: (0, jnp.where(layer == l, i, jnp.where(layer < l, 0, nr - 1)), 0))

    ws = pl.BlockSpec((None, tr, C), lambda layer, i: (layer, i, 0))
    return pl.pallas_call(
        body, name=name, grid=(L, nr), in_specs=[part_spec(l) for l in range(L)] + [ws] * 3, out_specs=[ws] * 4,
        out_shape=[jax.ShapeDtypeStruct((L, R, C), F32)] * 4, scratch_shapes=[pltpu.VMEM((tr, C), F32)],
        compiler_params=pltpu.CompilerParams(dimension_semantics=("arbitrary", "arbitrary")),
    )(*parts, w, m, v)


def _seg_rows(n, align):
    rows = -(-n // ROW)
    return -(-rows // align) * align


def _pack(arrs, align, lead=()):
    parts = []
    for a in arrs:
        n = int(np.prod(a.shape[len(lead):]))
        rows = _seg_rows(n, align)
        flat = a.reshape(lead + (n,))
        flat = jnp.pad(flat, [(0, 0)] * len(lead) + [(0, rows * ROW - n)])
        parts.append(flat.reshape(lead + (rows, ROW)))
    return jnp.concatenate(parts, axis=len(lead))


def _unpack(buf, shapes, align, lead=()):
    out, r0 = [], 0
    for shp in shapes:
        n = int(np.prod(shp))
        rows = _seg_rows(n, align)
        seg = lax.slice_in_dim(buf, r0, r0 + rows, axis=len(lead))
        out.append(seg.reshape(lead + (rows * ROW,))[..., :n].reshape(lead + tuple(shp)))
        r0 += rows
    return out


def kernel(x, p, norm_mix, norm_ffn, norm_ple, norm_kv, norm_final, a_w_in, a_norm_v, a_w_s, a_b_s, a_w_out, w_kv, b_w_q, b_sinks, b_w_o, f_w_up, f_conv_w, f_conv_b, f_w_down, ple_w_in, ple_w_gate, ple_b_gate, loss_target, m_norm_mix, m_norm_ffn, m_norm_ple, m_norm_kv, m_norm_final, m_a_w_in, m_a_norm_v, m_a_w_s, m_a_b_s, m_a_w_out, m_w_kv, m_b_w_q, m_b_sinks, m_b_w_o, m_f_w_up, m_f_conv_w, m_f_conv_b, m_f_w_down, m_ple_w_in, m_ple_w_gate, m_ple_b_gate, v_norm_mix, v_norm_ffn, v_norm_ple, v_norm_kv, v_norm_final, v_a_w_in, v_a_norm_v, v_a_w_s, v_a_b_s, v_a_w_out, v_w_kv, v_b_w_q, v_b_sinks, v_b_w_o, v_f_w_up, v_f_conv_w, v_f_conv_b, v_f_w_down, v_ple_w_in, v_ple_w_gate, v_ple_b_gate):
    w = dict(norm_mix=norm_mix, norm_ffn=norm_ffn, norm_ple=norm_ple, norm_kv=norm_kv, norm_final=norm_final,
             a_w_in=a_w_in, a_norm_v=a_norm_v, a_w_s=a_w_s, a_b_s=a_b_s, a_w_out=a_w_out, w_kv=w_kv, b_w_q=b_w_q,
             b_sinks=b_sinks, b_w_o=b_w_o, f_w_up=f_w_up, f_conv_w=f_conv_w, f_conv_b=f_conv_b, f_w_down=f_w_down,
             ple_w_in=ple_w_in, ple_w_gate=ple_w_gate, ple_b_gate=ple_b_gate)
    mom_m = dict(norm_mix=m_norm_mix, norm_ffn=m_norm_ffn, norm_ple=m_norm_ple, norm_kv=m_norm_kv,
                 norm_final=m_norm_final, a_w_in=m_a_w_in, a_norm_v=m_a_norm_v, a_w_s=m_a_w_s, a_b_s=m_a_b_s,
                 a_w_out=m_a_w_out, w_kv=m_w_kv, b_w_q=m_b_w_q, b_sinks=m_b_sinks, b_w_o=m_b_w_o, f_w_up=m_f_w_up,
                 f_conv_w=m_f_conv_w, f_conv_b=m_f_conv_b, f_w_down=m_f_w_down, ple_w_in=m_ple_w_in,
                 ple_w_gate=m_ple_w_gate, ple_b_gate=m_ple_b_gate)
    mom_v = dict(norm_mix=v_norm_mix, norm_ffn=v_norm_ffn, norm_ple=v_norm_ple, norm_kv=v_norm_kv,
                 norm_final=v_norm_final, a_w_in=v_a_w_in, a_norm_v=v_a_norm_v, a_w_s=v_a_w_s, a_b_s=v_a_b_s,
                 a_w_out=v_a_w_out, w_kv=v_w_kv, b_w_q=v_b_w_q, b_sinks=v_b_sinks, b_w_o=v_b_w_o, f_w_up=v_f_w_up,
                 f_conv_w=v_f_conv_w, f_conv_b=v_f_conv_b, f_w_down=v_f_w_down, ple_w_in=v_ple_w_in,
                 ple_w_gate=v_ple_w_gate, ple_b_gate=v_ple_b_gate)
    S = x.shape[1]
    tm = min(256, S)
    tl = min(512, S)
    me = 4 * lax.axis_index("x") + 2 * lax.axis_index("y") + lax.axis_index("c")

    def after_token(tok, a):
        return a + tok[0:1, 0:1].reshape((1,) * a.ndim)

    wb = {n: w[n].astype(BF) for n in _BIG}
    order = [('a_w_in', 0), ('a_w_out', 0), ('a_norm_v', None), ('f_conv_w', 0), ('f_conv_w', 1),
             ('f_w_up', 0), ('f_w_down', 0),
             ('ple_w_in', 0), ('ple_w_gate', 0), ('w_kv', None), ('b_w_q', 0), ('b_w_o', 0),
             ('f_w_up', 1), ('f_w_down', 1), ('ple_w_in', 1), ('ple_w_gate', 1)]
    src = lambda n, l: (wb[n] if n in wb else w[n]) if l is None else (wb[n] if n in wb else w[n])[l]
    groups = [order[:5], order[5:7], order[7:12], order[12:]]
    g = dict(zip(groups[0], _all_gather_now([src(n, l) for n, l in groups[0]], "gather_mix")))

    def start_group(keys, tie_to, name):
        srcs = [src(n, l) for n, l in keys]
        k = min(range(len(srcs)), key=lambda i: srcs[i].size)
        zero = (tie_to[(0,) * (tie_to.ndim - 1)][0:1] * 0).astype(srcs[k].dtype)
        srcs[k] = srcs[k] + zero.reshape((1,) * srcs[k].ndim)
        return _send_start(srcs, False, name)

    def finish_group(started, keys, after, name):
        return dict(zip(keys, _send_wait(started, list(range(len(keys))), after, name)))

    x0 = x[0]
    win = g[('a_w_in', 0)].transpose(1, 0, 2).reshape(D, 2 * AW)
    wout = g[('a_w_out', 0)].reshape(AW, D)
    gv = g[('a_norm_v', None)].reshape(1, AW)
    cw = [g[('f_conv_w', l)] for l in range(2)]
    cb = [w['f_conv_b'][l].reshape(N_DEV, 1, FF_BLK) for l in range(2)]
    ws = a_w_s[0].astype(BF)
    wst = jnp.swapaxes(ws, 1, 2)
    bsb = jnp.broadcast_to(a_b_s[0][:, :, None], (GROUPS, CHUNK, CHUNK))
    row = lambda a: a.reshape(1, -1)
    pb = [p[l, 0].astype(BF) for l in range(2)]
    sinks = b_sinks.reshape(N_Q)
    consts = _attn_consts()

    srcs1 = [src(n, l) for n, l in groups[1]]
    srcs1[1] = srcs1[1] + (g[('a_w_out', 0)][0, 0:1, 0:1] * 0).astype(srcs1[1].dtype)
    s1 = _gather2_start(srcs1, "gather_start_ffn0")
    h1, zg0, gp0 = _mixa_fwd(x0, after_token(s1["token"], row(norm_mix[0])), win, gv, ws, bsb, wout, tl)
    s1 = _gather2_forward(s1, h1, "gather_forward_ffn0")
    g = dict(zip(groups[1], _gather2_wait(s1, h1, "gather_wait_ffn0")))
    wup, wdown = [g[('f_w_up', 0)], None], [g[('f_w_down', 0)].reshape(D_FF, D), None]
    s2 = start_group(groups[2], g[('f_w_down', 0)], "gather_start_mid")
    h2, hh0, cc0 = _ffn_fwd(h1, after_token(s2["token"], row(norm_ffn[0])), wup[0], cw[0], cb[0], wdown[0], tm)
    g = finish_group(s2, groups[2], h2, "gather_wait_mid")
    wple = [g[('ple_w_in', 0)].transpose(1, 0, 2).reshape(PLE, D), None]
    wgate = [g[('ple_w_gate', 0)].reshape(D, D), None]
    wkv, wq, wo = g[('w_kv', None)].reshape(D, 2 * N_KV * HD), g[('b_w_q', 0)].reshape(D, D), g[('b_w_o', 0)].reshape(D, D)
    s3 = start_group(groups[3], g[('ple_w_in', 0)], "gather_start_l1")
    h3 = _ple_fwd(h2, after_token(s3["token"], row(norm_ple[0])), wgate[0], row(ple_b_gate[0]), pb[0], wple[0], tl)
    q, kv = _qkv_fwd(h3, row(norm_mix[1]), wq, row(norm_kv), wkv, tl)
    att = _attn_fwd(q, kv, sinks, consts)
    h4 = _oproj_fwd(att, wo, h3, tl)
    g = finish_group(s3, groups[3], h4, "gather_wait_l1")
    wup[1], wdown[1] = g[('f_w_up', 1)], g[('f_w_down', 1)].reshape(D_FF, D)
    wple[1], wgate[1] = g[('ple_w_in', 1)].transpose(1, 0, 2).reshape(PLE, D), g[('ple_w_gate', 1)].reshape(D, D)
    h5, hh1, cc1 = _ffn_fwd(h4, row(norm_ffn[1]), wup[1], cw[1], cb[1], wdown[1], tm)

    def col_blocks(g2d, n):
        return g2d.reshape(g2d.shape[0], N_DEV, n).transpose(1, 0, 2)

    def row_blocks(g2d):
        return g2d.reshape(N_DEV, g2d.shape[0] // N_DEV, g2d.shape[1])

    def ple_grads(l, de, xp, dpre):
        return [col_blocks(_wgrad(pb[l], de, f"wgrad_ple_in{l}", tn=1024), D // N_DEV),
                row_blocks(_wgrad(xp, dpre, f"wgrad_gate{l}", tn=1024))]

    def ffn_grads(l, xf, dhh, act, dhb):
        return [_wgrad(xf, dhh, f"wgrad_up{l}", tn=FF_BLK, out_blocked=True, a_transposed=True, ts=4096),
                row_blocks(_wgrad(act, dhb, f"wgrad_down{l}", tn=1024, ts=4096))]

    dh5, de1, xp1, dpre1, dbg1, dgp1, loss_blk, d_norm_final = _head_ple_bwd(
        h5, loss_target[0], row(norm_final), row(norm_ple[1]), wgate[1], row(ple_b_gate[1]), pb[1], wple[1], tl)
    ex_ple1 = _send_start(ple_grads(1, de1, xp1, dpre1), True, "send_ple1")
    dh4, xf1, dhh1, act1, dhb5, dcw1, dcb1, dgf1 = _ffn_bwd(dh5, h4, hh1, cc1, after_token(ex_ple1["token"], row(norm_ffn[1])),
                                                            wup[1], cw[1], wdown[1], tm)
    ex_ffn1 = _send_start(ffn_grads(1, xf1, dhh1, act1, dhb5), True, "send_ffn1")
    datt, dhb4 = _oproj_bwd(dh4, wo, tl)
    dq, dkv, dsink = _attn_bwd(q, kv, att, datt, after_token(ex_ffn1["token"], sinks.reshape(1, N_Q)).reshape(N_Q), consts)
    dh2, xq, xkv, dgq, dgkv, de0, xp0, dpre0, dbg0, dgp0 = _qkv_ple_bwd(
        dh4, h3, dq, dkv, row(norm_mix[1]), wq, row(norm_kv), wkv, h2, row(norm_ple[0]), wgate[0], row(ple_b_gate[0]),
        pb[0], wple[0], tl)
    ex_att = _send_start([row_blocks(_wgrad(att, dhb4, "wgrad_o", tn=1024)),
                              row_blocks(_wgrad(xq, dq, "wgrad_q", tn=1024)),
                              row_blocks(_wgrad(xkv, dkv, "wgrad_kv", tn=512))], True, "send_att")
    ex_ple0 = _send_start(ple_grads(0, de0, xp0, dpre0), True, "send_ple0")
    dh1, xf0, dhh0, act0, dhb2, dcw0, dcb0, dgf0 = _ffn_bwd(dh2, h1, hh0, cc0, after_token(ex_ple0["token"], row(norm_ffn[0])),
                                                            wup[0], cw[0], wdown[0], tm)
    ex_ffn0 = _send_start(ffn_grads(0, xf0, dhh0, act0, dhb2), True, "send_ffn0")
    dx, xa, dz, gated, dhb1, dws, dbs, dgv, dga = _mixa_bwd(dh1, x0, zg0, gp0, after_token(ex_ffn0["token"], row(norm_mix[0])), win, gv,
                                                            ws, wst, bsb, wout, tl)
    small_g = {
        'norm_mix': jnp.concatenate([dga, dgq], axis=0), 'norm_ffn': jnp.concatenate([dgf0, dgf1], axis=0),
        'norm_ple': jnp.concatenate([dgp0, dgp1], axis=0), 'norm_kv': dgkv.reshape(D),
        'norm_final': d_norm_final.reshape(D), 'a_w_s': dws[None], 'a_b_s': dbs[None, :, :, 0],
        'b_sinks': dsink[:, 0].reshape(1, N_Q),
        'f_conv_b': jnp.stack([dcb0.reshape(2 * D_FF), dcb1.reshape(2 * D_FF)]),
        'ple_b_gate': jnp.concatenate([dbg0, dbg1], axis=0),
        'a_norm_v': dgv, 'f_conv_w': jnp.stack([dcw0, dcw1], axis=1),
    }
    names_all = _REPL + _SMALL_SHARDED
    full_shapes = [small_g[n].shape for n in names_all]
    small_packed = _pack([small_g[n] for n in names_all], 8)
    ex_small = _send_start([small_packed.reshape(N_DEV, small_packed.shape[0] // N_DEV, ROW)], True, "send_small")
    ex_a_in = _send_start([_wgrad(xa, dz, "wgrad_a_in", tn=2 * AW // N_DEV, out_blocked=True, a_transposed=True,
                                  per_step=4)], True, "send_a_in")
    ex_a_out = _send_start([row_blocks(_wgrad(gated, dhb1, "wgrad_a_out", tn=1024))],
                           True, "send_a_out")

    def as3(a):
        return a.reshape((-1,) + a.shape[-2:])

    def adam(name, parts):
        outs = _adamw_weight(parts, as3(w[name]), as3(mom_m[name]), as3(mom_v[name]), "adamw_" + name)
        return [o.reshape(w[name].shape) for o in outs]

    big = {}
    l_ple1 = _send_wait(ex_ple1, list(range(len(ex_ple1["lands"]))), ex_a_out["token"], "recv_ple1")
    l_ffn1 = _send_wait(ex_ffn1, list(range(len(ex_ffn1["lands"]))), l_ple1[0], "recv_ffn1")
    l_att = _send_wait(ex_att, list(range(len(ex_att["lands"]))), l_ffn1[0], "recv_att")
    big['b_w_o'], big['b_w_q'], big['w_kv'] = adam('b_w_o', [l_att[0]]), adam('b_w_q', [l_att[1]]), adam('w_kv', [l_att[2]])
    l_ple0 = _send_wait(ex_ple0, list(range(len(ex_ple0["lands"]))), big['w_kv'][0], "recv_ple0")
    big['ple_w_in'] = adam('ple_w_in', [l_ple0[0], l_ple1[0]])
    big['ple_w_gate'] = adam('ple_w_gate', [l_ple0[1], l_ple1[1]])
    small_sum = _sum8(_send_wait(ex_small, [0], big['ple_w_gate'][0], "recv_small")[0], "sum_small_grads")
    ex_small_all = _send_start([small_sum], False, "gather_small_sums")
    l_ffn0 = _send_wait(ex_ffn0, list(range(len(ex_ffn0["lands"]))), ex_small_all["token"], "recv_ffn0")
    big['f_w_up'] = adam('f_w_up', [l_ffn0[0], l_ffn1[0]])
    big['f_w_down'] = adam('f_w_down', [l_ffn0[1], l_ffn1[1]])
    big['a_w_in'] = adam('a_w_in', _send_wait(ex_a_in, [0], big['f_w_down'][0], "recv_a_in"))
    big['a_w_out'] = adam('a_w_out', _send_wait(ex_a_out, [0], big['a_w_in'][0], "recv_a_out"))
    g_big, d_big, m_big, v_big = [{n: big[n][i] for n in _BIG} for i in range(4)]

    gsum = _send_wait(ex_small_all, [0], big['a_w_out'][0], "recv_small_sums")[0].reshape(small_packed.shape)
    gs = dict(zip(names_all, _unpack(gsum, full_shapes, 8)))
    gs['a_norm_v'] = lax.dynamic_slice_in_dim(gs['a_norm_v'], me * (AW // N_DEV), AW // N_DEV, axis=1)
    gs['f_conv_w'] = lax.dynamic_index_in_dim(gs['f_conv_w'], me, axis=0, keepdims=False)
    gs = {n: gs[n].reshape(w[n].shape) for n in names_all}
    packs = lambda d: _pack([d[n] for n in names_all], 8)
    souts = _adamw_small(packs(gs), packs(w), packs(mom_m), packs(mom_v), "adamw_small")
    shp = [w[n].shape for n in names_all]
    d_s, m_s, v_s = [dict(zip(names_all, _unpack(o, shp, 8))) for o in souts]

    grads = {**g_big, **gs}
    delta = {**d_big, **d_s}
    new_m = {**m_big, **m_s}
    new_v = {**v_big, **v_s}
    loss = lax.psum(loss_blk[0, 0], ("x", "y", "c"))
    return (loss, dx[None], *[grads[n] for n in _PARAMS], *[delta[n] for n in _PARAMS],
            *[new_m[n] for n in _PARAMS], *[new_v[n] for n in _PARAMS])
```
